```python
import math
import jax, jax.numpy as jnp
from jax import lax
import numpy as np

D_MODEL = 1024
BATCH = 8
SEQ = 8192
DEPTH = 1

PLE_DIM = 256
HG_HEADS = 8
HG_KEY = 128
HG_VAL = 128
HG_KWIDTH = HG_HEADS * HG_KEY
HG_WIDTH = HG_HEADS * HG_VAL
HG_CHUNK = 64
S5_GROUP = 16
S5_WIDTH = 512
S5_GROUPS = S5_WIDTH // S5_GROUP
S5_STATE = 64
DT_MIN = 0.001
DT_MAX = 0.1
NORM_EPS = 1e-6
IN_SIZES = (HG_KWIDTH, HG_KWIDTH, HG_WIDTH, HG_WIDTH, S5_WIDTH, S5_WIDTH, D_MODEL, D_MODEL)
IN_COLS = HG_KWIDTH * 2 + HG_WIDTH * 2 + S5_WIDTH * 2 + D_MODEL * 2

kernel_name = "hybrid_hgrn2_s5_gated_block"


def _split_points():
    pts, acc = [], 0
    for s in IN_SIZES[:-1]:
        acc += s
        pts.append(acc)
    return pts


def rms_norm(x, g):
    xf = x.astype(jnp.float32)
    y = xf * lax.rsqrt(jnp.mean(xf * xf, axis=-1, keepdims=True) + NORM_EPS)
    return (y * g.astype(jnp.float32)).astype(x.dtype)


def hgrn2_mix(q, f_logit, iv, lb):
    f32 = jnp.float32
    bsz, seq = q.shape[0], q.shape[1]
    nc = seq // HG_CHUNK
    lb = lb.reshape(HG_HEADS, HG_KEY).astype(f32)
    sig = jax.nn.sigmoid(f_logit.astype(f32))
    log_f = jnp.log(lb + (1.0 - lb) * sig)
    k = (1.0 - lb) * (1.0 - sig)
    chunk = lambda t: t.astype(f32).reshape(bsz, nc, HG_CHUNK, HG_HEADS, t.shape[-1])
    q, k, iv, log_f = chunk(q), chunk(k), chunk(iv), chunk(log_f)
    b = jnp.cumsum(log_f, axis=2)
    b_mid = b[:, :, HG_CHUNK // 2 - 1:HG_CHUNK // 2]
    b_last = b[:, :, HG_CHUNK - 1:HG_CHUNK]
    scores = jnp.einsum('bnthk,bnshk->bnhts', q * jnp.exp(b - b_mid), k * jnp.exp(b_mid - b))
    causal = jnp.tril(jnp.ones((HG_CHUNK, HG_CHUNK), dtype=bool))
    scores = jnp.where(causal, scores, 0.0)
    o_intra = jnp.einsum('bnhts,bnshv->bnthv', scores, iv)
    q_dec = q * jnp.exp(b)
    k_dec = k * jnp.exp(b_last - b)
    chunk_decay = jnp.exp(b_last[:, :, 0])

    def step(state, xs):
        qc, kc, ic, dc = xs
        o = jnp.einsum('bthk,bhkv->bthv', qc, state)
        state = dc[..., None] * state + jnp.einsum('bthk,bthv->bhkv', kc, ic)
        return state, o

    s0 = jnp.zeros((bsz, HG_HEADS, HG_KEY, HG_VAL), f32)
    mv = lambda t: jnp.moveaxis(t, 1, 0)
    _, o_inter = lax.scan(step, s0, (mv(q_dec), mv(k_dec), mv(iv), mv(chunk_decay)))
    o = o_intra + jnp.moveaxis(o_inter, 0, 1)
    return o.reshape(bsz, seq, HG_HEADS, HG_VAL)


def _cmul_combine(e1, e2):
    a1r, a1i, x1r, x1i = e1
    a2r, a2i, x2r, x2i = e2
    return (a1r * a2r - a1i * a2i,
            a1r * a2i + a1i * a2r,
            a2r * x1r - a2i * x1i + x2r,
            a2r * x1i + a2i * x1r + x2i)


def s5_mix(u, a_re, a_im, log_dt, b_re, b_im, c_re, c_im, d):
    f32 = jnp.float32
    uf = u.astype(f32)
    seq = u.shape[1]
    dt = jnp.exp(log_dt.astype(f32))[:, None]
    ar, ai = a_re.astype(f32), a_im.astype(f32)
    mag = jnp.exp(ar * dt)
    lr, li = mag * jnp.cos(ai * dt), mag * jnp.sin(ai * dt)
    den = ar * ar + ai * ai
    nr = lr - 1.0
    sr = (nr * ar + li * ai) / den
    si = (li * ar - nr * ai) / den
    br, bi = b_re.astype(f32), b_im.astype(f32)
    bbr = sr[..., None] * br - si[..., None] * bi
    bbi = sr[..., None] * bi + si[..., None] * br
    xr = jnp.einsum('bsgc,gnc->bsgn', uf, bbr)
    xi = jnp.einsum('bsgc,gnc->bsgn', uf, bbi)
    shape = (1, seq) + lr.shape
    lam_r = jnp.broadcast_to(lr[None, None], shape)
    lam_i = jnp.broadcast_to(li[None, None], shape)
    _, _, hr, hi = lax.associative_scan(_cmul_combine, (lam_r, lam_i, xr, xi), axis=1)
    y = (jnp.einsum('bsgn,gcn->bsgc', hr, c_re.astype(f32))
         - jnp.einsum('bsgn,gcn->bsgc', hi, c_im.astype(f32))
         + d.astype(f32) * uf)
    return y


def _fwd_setup_inputs(seed: int = 0) -> dict:
    key = jax.random.key(seed)
    ks = jax.random.split(key, 24)
    n = lambda k, shape, scale: jax.random.normal(k, shape, jnp.float32) * scale
    L, G, N, C = DEPTH, S5_GROUPS, S5_STATE, S5_GROUP
    a_im_base = jnp.pi * jnp.arange(N, dtype=jnp.float32)
    return {
        "x": n(ks[0], (BATCH, SEQ, D_MODEL), 1.0),
        "p": n(ks[1], (DEPTH, BATCH, SEQ, PLE_DIM), 1.0),
        "norm_g": 1.0 + n(ks[2], (L, D_MODEL), 0.02),
        "w_in": n(ks[3], (L, D_MODEL, IN_COLS), D_MODEL ** -0.5),
        "hg_lb": n(ks[4], (DEPTH + 1, HG_KWIDTH), 0.1),
        "hg_norm_g": 1.0 + n(ks[5], (L, HG_WIDTH), 0.02),
        "w_o_hg": n(ks[6], (L, HG_WIDTH, D_MODEL), HG_WIDTH ** -0.5),
        "s5_a_re": -0.5 + n(ks[7], (L, G, N), 0.01),
        "s5_a_im": a_im_base[None, None, :] + n(ks[8], (L, G, N), 0.01),
        "s5_log_dt": jax.random.uniform(ks[9], (L, G), jnp.float32, math.log(DT_MIN), math.log(DT_MAX)),
        "s5_b_re": n(ks[10], (L, G, N, C), (2.0 * C) ** -0.5),
        "s5_b_im": n(ks[11], (L, G, N, C), (2.0 * C) ** -0.5),
        "s5_c_re": n(ks[12], (L, G, C, N), N ** -0.5),
        "s5_c_im": n(ks[13], (L, G, C, N), N ** -0.5),
        "s5_d": n(ks[14], (L, G, C), 1.0),
        "w_glu": n(ks[15], (L, S5_WIDTH, 2 * S5_WIDTH), S5_WIDTH ** -0.5),
        "b_glu": n(ks[16], (L, 2 * S5_WIDTH), 0.01),
        "w_o_s5": n(ks[17], (L, S5_WIDTH, D_MODEL), S5_WIDTH ** -0.5),
        "w_out": n(ks[18], (L, D_MODEL, D_MODEL), D_MODEL ** -0.5),
        "ple_norm_g": 1.0 + n(ks[19], (L, D_MODEL), 0.02),
        "w_ple": n(ks[20], (L, PLE_DIM, D_MODEL), PLE_DIM ** -0.5),
        "w_ple_gate": n(ks[21], (L, D_MODEL, D_MODEL), D_MODEL ** -0.5),
        "final_norm_g": 1.0 + n(ks[22], (D_MODEL,), 0.02),
    }


def _fwd_reference(x, p, norm_g, w_in, hg_lb, hg_norm_g, w_o_hg, s5_a_re, s5_a_im, s5_log_dt,
              s5_b_re, s5_b_im, s5_c_re, s5_c_im, s5_d, w_glu, b_glu, w_o_s5, w_out,
              ple_norm_g, w_ple, w_ple_gate, final_norm_g):
    bsz, seq = x.shape[0], x.shape[1]
    h = x
    lb_all = jnp.cumsum(jax.nn.softmax(hg_lb.astype(jnp.float32), axis=0), axis=0)
    for l in range(DEPTH):
        u = rms_norm(h, norm_g[l])
        proj = u @ w_in[l]
        q, f_logit, iv, g_hg, u_s, z_s, gate_hg, gate_s5 = jnp.split(proj, _split_points(), axis=-1)

        o = hgrn2_mix(q.reshape(bsz, seq, HG_HEADS, HG_KEY),
                      f_logit.reshape(bsz, seq, HG_HEADS, HG_KEY),
                      iv.reshape(bsz, seq, HG_HEADS, HG_VAL), lb_all[l])
        o = rms_norm(o, hg_norm_g[l].reshape(HG_HEADS, HG_VAL)).reshape(bsz, seq, HG_WIDTH)
        y_hg = ((o * jax.nn.silu(g_hg.astype(jnp.float32))).astype(h.dtype) @ w_o_hg[l]).astype(h.dtype)

        ys = s5_mix(u_s.reshape(bsz, seq, S5_GROUPS, S5_GROUP), s5_a_re[l], s5_a_im[l], s5_log_dt[l],
                    s5_b_re[l], s5_b_im[l], s5_c_re[l], s5_c_im[l], s5_d[l]).reshape(bsz, seq, S5_WIDTH)
        ys = jax.nn.gelu(ys).astype(h.dtype)
        glu_a, glu_b = jnp.split(ys @ w_glu[l] + b_glu[l], 2, axis=-1)
        ys = glu_a * jax.nn.sigmoid(glu_b) * jax.nn.silu(z_s)
        y_s5 = (ys @ w_o_s5[l]).astype(h.dtype)

        merged = jax.nn.sigmoid(gate_hg) * y_hg + jax.nn.sigmoid(gate_s5) * y_s5
        h = h + (merged @ w_out[l]).astype(h.dtype)

        pe = p[l] @ w_ple[l]
        gate = jax.nn.sigmoid(rms_norm(h, ple_norm_g[l]) @ w_ple_gate[l])
        h = h + (pe * gate).astype(h.dtype)
    return rms_norm(h, final_norm_g)


import jax as _jax
import jax.numpy as _jnp

TWIN_FORMAT = 'train_step'
FWD_PARAMS = ['x', 'p', 'norm_g', 'w_in', 'hg_lb', 'hg_norm_g', 'w_o_hg', 's5_a_re', 's5_a_im', 's5_log_dt', 's5_b_re', 's5_b_im', 's5_c_re', 's5_c_im', 's5_d', 'w_glu', 'b_glu', 'w_o_s5', 'w_out', 'ple_norm_g', 'w_ple', 'w_ple_gate', 'final_norm_g']
TWIN_WEIGHTS = ['norm_g', 'w_in', 'hg_lb', 'hg_norm_g', 'w_o_hg', 's5_a_re', 's5_a_im', 's5_log_dt', 's5_b_re', 's5_b_im', 's5_c_re', 's5_c_im', 's5_d', 'w_glu', 'b_glu', 'w_o_s5', 'w_out', 'ple_norm_g', 'w_ple', 'w_ple_gate', 'final_norm_g']
TWIN_DIFF_INPUT = 'x'
TWIN_INPUTS = ['x', 'p', 'norm_g', 'w_in', 'hg_lb', 'hg_norm_g', 'w_o_hg', 's5_a_re', 's5_a_im', 's5_log_dt', 's5_b_re', 's5_b_im', 's5_c_re', 's5_c_im', 's5_d', 'w_glu', 'b_glu', 'w_o_s5', 'w_out', 'ple_norm_g', 'w_ple', 'w_ple_gate', 'final_norm_g', 'loss_target', 'm_norm_g', 'm_w_in', 'm_hg_lb', 'm_hg_norm_g', 'm_w_o_hg', 'm_s5_a_re', 'm_s5_a_im', 'm_s5_log_dt', 'm_s5_b_re', 'm_s5_b_im', 'm_s5_c_re', 'm_s5_c_im', 'm_s5_d', 'm_w_glu', 'm_b_glu', 'm_w_o_s5', 'm_w_out', 'm_ple_norm_g', 'm_w_ple', 'm_w_ple_gate', 'm_final_norm_g', 'v_norm_g', 'v_w_in', 'v_hg_lb', 'v_hg_norm_g', 'v_w_o_hg', 'v_s5_a_re', 'v_s5_a_im', 'v_s5_log_dt', 'v_s5_b_re', 'v_s5_b_im', 'v_s5_c_re', 'v_s5_c_im', 'v_s5_d', 'v_w_glu', 'v_b_glu', 'v_w_o_s5', 'v_w_out', 'v_ple_norm_g', 'v_w_ple', 'v_w_ple_gate', 'v_final_norm_g']
TWIN_OUTPUTS = ['loss', 'grad_x', 'grad_norm_g', 'grad_w_in', 'grad_hg_lb', 'grad_hg_norm_g', 'grad_w_o_hg', 'grad_s5_a_re', 'grad_s5_a_im', 'grad_s5_log_dt', 'grad_s5_b_re', 'grad_s5_b_im', 'grad_s5_c_re', 'grad_s5_c_im', 'grad_s5_d', 'grad_w_glu', 'grad_b_glu', 'grad_w_o_s5', 'grad_w_out', 'grad_ple_norm_g', 'grad_w_ple', 'grad_w_ple_gate', 'grad_final_norm_g', 'delta_norm_g', 'delta_w_in', 'delta_hg_lb', 'delta_hg_norm_g', 'delta_w_o_hg', 'delta_s5_a_re', 'delta_s5_a_im', 'delta_s5_log_dt', 'delta_s5_b_re', 'delta_s5_b_im', 'delta_s5_c_re', 'delta_s5_c_im', 'delta_s5_d', 'delta_w_glu', 'delta_b_glu', 'delta_w_o_s5', 'delta_w_out', 'delta_ple_norm_g', 'delta_w_ple', 'delta_w_ple_gate', 'delta_final_norm_g', 'new_m_norm_g', 'new_m_w_in', 'new_m_hg_lb', 'new_m_hg_norm_g', 'new_m_w_o_hg', 'new_m_s5_a_re', 'new_m_s5_a_im', 'new_m_s5_log_dt', 'new_m_s5_b_re', 'new_m_s5_b_im', 'new_m_s5_c_re', 'new_m_s5_c_im', 'new_m_s5_d', 'new_m_w_glu', 'new_m_b_glu', 'new_m_w_o_s5', 'new_m_w_out', 'new_m_ple_norm_g', 'new_m_w_ple', 'new_m_w_ple_gate', 'new_m_final_norm_g', 'new_v_norm_g', 'new_v_w_in', 'new_v_hg_lb', 'new_v_hg_norm_g', 'new_v_w_o_hg', 'new_v_s5_a_re', 'new_v_s5_a_im', 'new_v_s5_log_dt', 'new_v_s5_b_re', 'new_v_s5_b_im', 'new_v_s5_c_re', 'new_v_s5_c_im', 'new_v_s5_d', 'new_v_w_glu', 'new_v_b_glu', 'new_v_w_o_s5', 'new_v_w_out', 'new_v_ple_norm_g', 'new_v_w_ple', 'new_v_w_ple_gate', 'new_v_final_norm_g']
TWIN_LEAF_KINDS = {'loss': 'loss', 'grad_x': 'grad_x', 'grad_norm_g': 'grad_w', 'grad_w_in': 'grad_w', 'grad_hg_lb': 'grad_w', 'grad_hg_norm_g': 'grad_w', 'grad_w_o_hg': 'grad_w', 'grad_s5_a_re': 'grad_w', 'grad_s5_a_im': 'grad_w', 'grad_s5_log_dt': 'grad_w', 'grad_s5_b_re': 'grad_w', 'grad_s5_b_im': 'grad_w', 'grad_s5_c_re': 'grad_w', 'grad_s5_c_im': 'grad_w', 'grad_s5_d': 'grad_w', 'grad_w_glu': 'grad_w', 'grad_b_glu': 'grad_w', 'grad_w_o_s5': 'grad_w', 'grad_w_out': 'grad_w', 'grad_ple_norm_g': 'grad_w', 'grad_w_ple': 'grad_w', 'grad_w_ple_gate': 'grad_w', 'grad_final_norm_g': 'grad_w', 'delta_norm_g': 'delta_w', 'delta_w_in': 'delta_w', 'delta_hg_lb': 'delta_w', 'delta_hg_norm_g': 'delta_w', 'delta_w_o_hg': 'delta_w', 'delta_s5_a_re': 'delta_w', 'delta_s5_a_im': 'delta_w', 'delta_s5_log_dt': 'delta_w', 'delta_s5_b_re': 'delta_w', 'delta_s5_b_im': 'delta_w', 'delta_s5_c_re': 'delta_w', 'delta_s5_c_im': 'delta_w', 'delta_s5_d': 'delta_w', 'delta_w_glu': 'delta_w', 'delta_b_glu': 'delta_w', 'delta_w_o_s5': 'delta_w', 'delta_w_out': 'delta_w', 'delta_ple_norm_g': 'delta_w', 'delta_w_ple': 'delta_w', 'delta_w_ple_gate': 'delta_w', 'delta_final_norm_g': 'delta_w', 'new_m_norm_g': 'new_m', 'new_m_w_in': 'new_m', 'new_m_hg_lb': 'new_m', 'new_m_hg_norm_g': 'new_m', 'new_m_w_o_hg': 'new_m', 'new_m_s5_a_re': 'new_m', 'new_m_s5_a_im': 'new_m', 'new_m_s5_log_dt': 'new_m', 'new_m_s5_b_re': 'new_m', 'new_m_s5_b_im': 'new_m', 'new_m_s5_c_re': 'new_m', 'new_m_s5_c_im': 'new_m', 'new_m_s5_d': 'new_m', 'new_m_w_glu': 'new_m', 'new_m_b_glu': 'new_m', 'new_m_w_o_s5': 'new_m', 'new_m_w_out': 'new_m', 'new_m_ple_norm_g': 'new_m', 'new_m_w_ple': 'new_m', 'new_m_w_ple_gate': 'new_m', 'new_m_final_norm_g': 'new_m', 'new_v_norm_g': 'new_v', 'new_v_w_in': 'new_v', 'new_v_hg_lb': 'new_v', 'new_v_hg_norm_g': 'new_v', 'new_v_w_o_hg': 'new_v', 'new_v_s5_a_re': 'new_v', 'new_v_s5_a_im': 'new_v', 'new_v_s5_log_dt': 'new_v', 'new_v_s5_b_re': 'new_v', 'new_v_s5_b_im': 'new_v', 'new_v_s5_c_re': 'new_v', 'new_v_s5_c_im': 'new_v', 'new_v_s5_d': 'new_v', 'new_v_w_glu': 'new_v', 'new_v_b_glu': 'new_v', 'new_v_w_o_s5': 'new_v', 'new_v_w_out': 'new_v', 'new_v_ple_norm_g': 'new_v', 'new_v_w_ple': 'new_v', 'new_v_w_ple_gate': 'new_v', 'new_v_final_norm_g': 'new_v'}


def _forward(args):
    return _fwd_reference(*[args[k] for k in FWD_PARAMS])


def _output_shape():
    def fwd():
        inp = _fwd_setup_inputs(0)
        return _fwd_reference(*[inp[k] for k in FWD_PARAMS])
    out = _jax.eval_shape(fwd)
    return out.shape, out.dtype

N_MICROBATCH = 1
ADAM_LR = 0.001
ADAM_B1 = 0.9
ADAM_B2 = 0.999
ADAM_EPS = 1e-08
ADAM_WD = 0.01
ADAM_STEP = 10
PER_EXAMPLE_BATCH_AXIS = {'x': 0, 'p': 1, 'loss_target': 0}
SHARED_INPUTS = []
_WEIGHT_DTYPES = {'norm_g': _jnp.float32, 'w_in': _jnp.float32, 'hg_lb': _jnp.float32, 'hg_norm_g': _jnp.float32, 'w_o_hg': _jnp.float32, 's5_a_re': _jnp.float32, 's5_a_im': _jnp.float32, 's5_log_dt': _jnp.float32, 's5_b_re': _jnp.float32, 's5_b_im': _jnp.float32, 's5_c_re': _jnp.float32, 's5_c_im': _jnp.float32, 's5_d': _jnp.float32, 'w_glu': _jnp.float32, 'b_glu': _jnp.float32, 'w_o_s5': _jnp.float32, 'w_out': _jnp.float32, 'ple_norm_g': _jnp.float32, 'w_ple': _jnp.float32, 'w_ple_gate': _jnp.float32, 'final_norm_g': _jnp.float32}
MOMENT_SCALE = {'norm_g': 1.607283e-01, 'w_in': 6.041263e-02, 'hg_lb': 4.763100e-02, 'hg_norm_g': 7.038708e-02, 'w_o_hg': 6.937730e-02, 's5_a_re': 2.836116e-03, 's5_a_im': 2.789450e-03, 's5_log_dt': 3.179551e+00, 's5_b_re': 1.695798e-03, 's5_b_im': 1.717010e-03, 's5_c_re': 2.381749e-03, 's5_c_im': 2.471352e-03, 's5_d': 3.971186e-02, 'w_glu': 2.494544e-02, 'b_glu': 3.878883e-02, 'w_o_s5': 2.382106e-02, 'w_out': 7.350156e-02, 'ple_norm_g': 4.583970e-02, 'w_ple': 1.154055e-01, 'w_ple_gate': 4.450210e-02, 'final_norm_g': 6.397155e+01}


def _to_microbatches(a, axis):
    t = _jnp.moveaxis(a, axis, 0)
    t = t.reshape((N_MICROBATCH, t.shape[0] // N_MICROBATCH) + t.shape[1:])
    return _jnp.moveaxis(t, 1, axis + 1)


def setup_inputs(seed: int = 0) -> dict:
    inp = _fwd_setup_inputs(seed)
    key = _jax.random.fold_in(_jax.random.key(seed), 7919)
    shape, _ = _output_shape()
    out = dict(inp)
    out["loss_target"] = _jax.random.normal(_jax.random.fold_in(key, 0), shape, _jnp.float32)
    for i, name in enumerate(TWIN_WEIGHTS):
        w = inp[name].astype(_jnp.float32)
        if MOMENT_SCALE is None:
            s = _jnp.sqrt(_jnp.mean(_jnp.square(w)) + 1e-30)
        else:
            s = MOMENT_SCALE[name]
        km, kv = _jax.random.split(_jax.random.fold_in(key, i + 1))
        out[name] = w
        out["m_" + name] = s * _jax.random.normal(km, w.shape, _jnp.float32)
        out["v_" + name] = (s * s) * _jax.random.uniform(kv, w.shape, _jnp.float32, 0.5, 1.5)
    if N_MICROBATCH > 1:
        for name, axis in PER_EXAMPLE_BATCH_AXIS.items():
            out[name] = _to_microbatches(out[name], axis)
    return {'x': out['x'], 'p': out['p'], 'norm_g': out['norm_g'], 'w_in': out['w_in'], 'hg_lb': out['hg_lb'], 'hg_norm_g': out['hg_norm_g'], 'w_o_hg': out['w_o_hg'], 's5_a_re': out['s5_a_re'], 's5_a_im': out['s5_a_im'], 's5_log_dt': out['s5_log_dt'], 's5_b_re': out['s5_b_re'], 's5_b_im': out['s5_b_im'], 's5_c_re': out['s5_c_re'], 's5_c_im': out['s5_c_im'], 's5_d': out['s5_d'], 'w_glu': out['w_glu'], 'b_glu': out['b_glu'], 'w_o_s5': out['w_o_s5'], 'w_out': out['w_out'], 'ple_norm_g': out['ple_norm_g'], 'w_ple': out['w_ple'], 'w_ple_gate': out['w_ple_gate'], 'final_norm_g': out['final_norm_g'], 'loss_target': out['loss_target'], 'm_norm_g': out['m_norm_g'], 'm_w_in': out['m_w_in'], 'm_hg_lb': out['m_hg_lb'], 'm_hg_norm_g': out['m_hg_norm_g'], 'm_w_o_hg': out['m_w_o_hg'], 'm_s5_a_re': out['m_s5_a_re'], 'm_s5_a_im': out['m_s5_a_im'], 'm_s5_log_dt': out['m_s5_log_dt'], 'm_s5_b_re': out['m_s5_b_re'], 'm_s5_b_im': out['m_s5_b_im'], 'm_s5_c_re': out['m_s5_c_re'], 'm_s5_c_im': out['m_s5_c_im'], 'm_s5_d': out['m_s5_d'], 'm_w_glu': out['m_w_glu'], 'm_b_glu': out['m_b_glu'], 'm_w_o_s5': out['m_w_o_s5'], 'm_w_out': out['m_w_out'], 'm_ple_norm_g': out['m_ple_norm_g'], 'm_w_ple': out['m_w_ple'], 'm_w_ple_gate': out['m_w_ple_gate'], 'm_final_norm_g': out['m_final_norm_g'], 'v_norm_g': out['v_norm_g'], 'v_w_in': out['v_w_in'], 'v_hg_lb': out['v_hg_lb'], 'v_hg_norm_g': out['v_hg_norm_g'], 'v_w_o_hg': out['v_w_o_hg'], 'v_s5_a_re': out['v_s5_a_re'], 'v_s5_a_im': out['v_s5_a_im'], 'v_s5_log_dt': out['v_s5_log_dt'], 'v_s5_b_re': out['v_s5_b_re'], 'v_s5_b_im': out['v_s5_b_im'], 'v_s5_c_re': out['v_s5_c_re'], 'v_s5_c_im': out['v_s5_c_im'], 'v_s5_d': out['v_s5_d'], 'v_w_glu': out['v_w_glu'], 'v_b_glu': out['v_b_glu'], 'v_w_o_s5': out['v_w_o_s5'], 'v_w_out': out['v_w_out'], 'v_ple_norm_g': out['v_ple_norm_g'], 'v_w_ple': out['v_w_ple'], 'v_w_ple_gate': out['v_w_ple_gate'], 'v_final_norm_g': out['v_final_norm_g']}


def _loss(weights, diff, rest, loss_target):
    with _jax.named_scope("forward"):
        args = {**rest, TWIN_DIFF_INPUT: diff, **{k: w.astype(_WEIGHT_DTYPES[k]) for k, w in weights.items()}}
        y = _forward(args)
    with _jax.named_scope("loss_head"):
        err = _jnp.square(y.astype(_jnp.float32) - loss_target)
        return 0.5 * _jnp.sum(_jnp.mean(err, axis=-1)) if err.ndim else 0.5 * err


def _adamw(w, g, m, v):
    m = ADAM_B1 * m + (1.0 - ADAM_B1) * g
    v = ADAM_B2 * v + (1.0 - ADAM_B2) * _jnp.square(g)
    m_hat = m / (1.0 - ADAM_B1 ** ADAM_STEP)
    v_hat = v / (1.0 - ADAM_B2 ** ADAM_STEP)
    delta = -ADAM_LR * (m_hat / (_jnp.sqrt(v_hat) + ADAM_EPS) + ADAM_WD * w)
    return delta, m, v


def reference(x, p, norm_g, w_in, hg_lb, hg_norm_g, w_o_hg, s5_a_re, s5_a_im, s5_log_dt, s5_b_re, s5_b_im, s5_c_re, s5_c_im, s5_d, w_glu, b_glu, w_o_s5, w_out, ple_norm_g, w_ple, w_ple_gate, final_norm_g, loss_target, m_norm_g, m_w_in, m_hg_lb, m_hg_norm_g, m_w_o_hg, m_s5_a_re, m_s5_a_im, m_s5_log_dt, m_s5_b_re, m_s5_b_im, m_s5_c_re, m_s5_c_im, m_s5_d, m_w_glu, m_b_glu, m_w_o_s5, m_w_out, m_ple_norm_g, m_w_ple, m_w_ple_gate, m_final_norm_g, v_norm_g, v_w_in, v_hg_lb, v_hg_norm_g, v_w_o_hg, v_s5_a_re, v_s5_a_im, v_s5_log_dt, v_s5_b_re, v_s5_b_im, v_s5_c_re, v_s5_c_im, v_s5_d, v_w_glu, v_b_glu, v_w_o_s5, v_w_out, v_ple_norm_g, v_w_ple, v_w_ple_gate, v_final_norm_g):
    given = dict(x=x, p=p, norm_g=norm_g, w_in=w_in, hg_lb=hg_lb, hg_norm_g=hg_norm_g, w_o_hg=w_o_hg, s5_a_re=s5_a_re, s5_a_im=s5_a_im, s5_log_dt=s5_log_dt, s5_b_re=s5_b_re, s5_b_im=s5_b_im, s5_c_re=s5_c_re, s5_c_im=s5_c_im, s5_d=s5_d, w_glu=w_glu, b_glu=b_glu, w_o_s5=w_o_s5, w_out=w_out, ple_norm_g=ple_norm_g, w_ple=w_ple, w_ple_gate=w_ple_gate, final_norm_g=final_norm_g, loss_target=loss_target, m_norm_g=m_norm_g, m_w_in=m_w_in, m_hg_lb=m_hg_lb, m_hg_norm_g=m_hg_norm_g, m_w_o_hg=m_w_o_hg, m_s5_a_re=m_s5_a_re, m_s5_a_im=m_s5_a_im, m_s5_log_dt=m_s5_log_dt, m_s5_b_re=m_s5_b_re, m_s5_b_im=m_s5_b_im, m_s5_c_re=m_s5_c_re, m_s5_c_im=m_s5_c_im, m_s5_d=m_s5_d, m_w_glu=m_w_glu, m_b_glu=m_b_glu, m_w_o_s5=m_w_o_s5, m_w_out=m_w_out, m_ple_norm_g=m_ple_norm_g, m_w_ple=m_w_ple, m_w_ple_gate=m_w_ple_gate, m_final_norm_g=m_final_norm_g, v_norm_g=v_norm_g, v_w_in=v_w_in, v_hg_lb=v_hg_lb, v_hg_norm_g=v_hg_norm_g, v_w_o_hg=v_w_o_hg, v_s5_a_re=v_s5_a_re, v_s5_a_im=v_s5_a_im, v_s5_log_dt=v_s5_log_dt, v_s5_b_re=v_s5_b_re, v_s5_b_im=v_s5_b_im, v_s5_c_re=v_s5_c_re, v_s5_c_im=v_s5_c_im, v_s5_d=v_s5_d, v_w_glu=v_w_glu, v_b_glu=v_b_glu, v_w_o_s5=v_w_o_s5, v_w_out=v_w_out, v_ple_norm_g=v_ple_norm_g, v_w_ple=v_w_ple, v_w_ple_gate=v_w_ple_gate, v_final_norm_g=v_final_norm_g)
    weights = {n: given[n] for n in TWIN_WEIGHTS}
    shared = {n: given[n] for n in SHARED_INPUTS}
    per_example = {n: given[n] for n in ['x', 'p']}
    grad_fn = _jax.value_and_grad(_loss, argnums=(0, 1))

    def one_microbatch(ex, loss_target):
        ex = dict(ex)
        diff = ex.pop(TWIN_DIFF_INPUT)
        return grad_fn(weights, diff, {**shared, **ex}, loss_target)

    if N_MICROBATCH == 1:
        loss, (grad_w, grad_x) = one_microbatch(per_example, given["loss_target"])
    else:
        def body(carry, xs):
            loss_sum, grad_sum = carry
            l_k, (gw_k, gx_k) = one_microbatch(xs[0], xs[1])
            with _jax.named_scope("update"):
                return (loss_sum + l_k, _jax.tree.map(_jnp.add, grad_sum, gw_k)), gx_k

        init = (_jnp.zeros((), _jnp.float32), _jax.tree.map(_jnp.zeros_like, weights))
        (loss, grad_w), grad_x = _jax.lax.scan(body, init, (per_example, given["loss_target"]))
    with _jax.named_scope("update"):
        delta_w, new_m, new_v = {}, {}, {}
        for n in TWIN_WEIGHTS:
            delta_w[n], new_m[n], new_v[n] = _adamw(weights[n], grad_w[n], given["m_" + n], given["v_" + n])
    return (loss, grad_x, *[grad_w[n] for n in TWIN_WEIGHTS], *[delta_w[n] for n in TWIN_WEIGHTS],
            *[new_m[n] for n in TWIN_WEIGHTS], *[new_v[n] for n in TWIN_WEIGHTS])
```

```python
import functools
import math

import jax
import jax.numpy as jnp
from jax import lax
from jax.experimental import pallas as pl
from jax.experimental.pallas import tpu as pltpu

F32 = jnp.float32
BF16 = jnp.bfloat16

D_MODEL = 1024
N_DEV = 8
IN_COLS = 7168
SHARD_IN = IN_COLS // N_DEV
HG_HEADS = 8
HG_DIM = 128
HG_CHUNK = 64
HG_SUPER = 256
S5_WIDTH = 512
S5_GROUPS = 32
S5_STATE = 64
S5_CH = 16
S5_SEG = 8
S5_QUADS = 4
S5_QCOLS = 1024
S5_COLS = S5_QUADS * S5_QCOLS
S5_TILE_STEPS = 64
NORM_EPS = 1e-6
ADAM_LR = 0.001
ADAM_B1 = 0.9
ADAM_B2 = 0.999
ADAM_EPS = 1e-08
ADAM_WD = 0.01
ADAM_STEP = 10
MIB = 1024 * 1024
MESH = pl.DeviceIdType.MESH

COL_Q, COL_F, COL_I, COL_G, COL_US, COL_ZS, COL_GH, COL_GS = 0, 1024, 2048, 3072, 4096, 4608, 5120, 6144


def _call(body, **kw):
    return pl.pallas_call(body, **kw)


def _params(vmem_mb, n_grid=1):
    return pltpu.CompilerParams(
        dimension_semantics=("arbitrary",) * n_grid, vmem_limit_bytes=vmem_mb * MIB
    )


def _bdot(a, b):
    return jnp.dot(a.astype(BF16), b.astype(BF16), preferred_element_type=F32)


def _bdot_nt(a, b):
    return lax.dot_general(a.astype(BF16), b.astype(BF16), (((1,), (1,)), ((), ())), preferred_element_type=F32)


def _bdot_tn(a, b):
    return lax.dot_general(a.astype(BF16), b.astype(BF16), (((0,), (0,)), ((), ())), preferred_element_type=F32)


def _sigmoid(x):
    return jax.nn.sigmoid(x)


def _silu(x):
    return x * _sigmoid(x)


def _dsilu(x):
    s = _sigmoid(x)
    return s * (1.0 + x * (1.0 - s))


_GELU_C = math.sqrt(2.0 / math.pi)


def _gelu(x):
    return 0.5 * x * (1.0 + jnp.tanh(_GELU_C * (x + 0.044715 * x * x * x)))


def _dgelu(x):
    t = jnp.tanh(_GELU_C * (x + 0.044715 * x * x * x))
    return 0.5 * (1.0 + t) + 0.5 * x * (1.0 - t * t) * _GELU_C * (1.0 + 3.0 * 0.044715 * x * x)


def _rms_fwd(x, g):
    r = lax.rsqrt(jnp.mean(x * x, axis=-1, keepdims=True) + NORM_EPS)
    n = x * r
    return n * g, n, r


def _rms_bwd(dy, n, r, g):
    dn = dy * g
    dx = r * (dn - n * jnp.mean(dn * n, axis=-1, keepdims=True))
    return dx, jnp.sum(dy * n, axis=0, keepdims=True)


def _head_rms_fwd(o, g):
    ns, rs = [], []
    for h in range(HG_HEADS):
        oh = o[:, h * HG_DIM:(h + 1) * HG_DIM]
        r = lax.rsqrt(jnp.mean(oh * oh, axis=-1, keepdims=True) + NORM_EPS)
        ns.append(oh * r)
        rs.append(r)
    n = jnp.concatenate(ns, axis=1)
    return n * g, n, rs


def _head_rms_bwd(dy, n, rs, g):
    dn = dy * g
    dxs = []
    for h in range(HG_HEADS):
        sl = slice(h * HG_DIM, (h + 1) * HG_DIM)
        dxs.append(rs[h] * (dn[:, sl] - n[:, sl] * jnp.mean(dn[:, sl] * n[:, sl], axis=-1, keepdims=True)))
    return jnp.concatenate(dxs, axis=1), jnp.sum(dy * n, axis=0, keepdims=True)


def _rowwise(name, fn, n_rows, tm, rows, consts, out_rows, out_accs, vmem_mb):
    n_r, n_c, n_or, n_oa = len(rows), len(consts), len(out_rows), len(out_accs)

    def body(*refs):
        r_refs = refs[:n_r]
        c_refs = refs[n_r:n_r + n_c]
        or_refs = refs[n_r + n_c:n_r + n_c + n_or]
        oa_refs = refs[n_r + n_c + n_or:]
        outs, accs = fn([r[...] for r in r_refs], c_refs, or_refs)

        if n_oa:
            @pl.when(pl.program_id(0) == 0)
            def _():
                for ref in oa_refs:
                    ref[...] = jnp.zeros(ref.shape, ref.dtype)

        for ref, v in zip(or_refs, outs):
            if v is not None:
                ref[...] = v.astype(ref.dtype)
        for ref, v in zip(oa_refs, accs):
            ref[...] += v.astype(ref.dtype)

    in_specs = [pl.BlockSpec((tm, w), functools.partial(lambda i, c: (i, c), c=cb)) for (_, w, cb) in rows]
    in_specs += [pl.BlockSpec(c.shape, functools.partial(lambda i, nd: (0,) * nd, nd=c.ndim)) for c in consts]
    out_specs = [pl.BlockSpec((tm, w), lambda i: (i, 0)) for (w, _) in out_rows]
    out_specs += [pl.BlockSpec(s, functools.partial(lambda i, nd: (0,) * nd, nd=len(s))) for (s, _) in out_accs]
    out_shape = [jax.ShapeDtypeStruct((n_rows, w), dt) for (w, dt) in out_rows]
    out_shape += [jax.ShapeDtypeStruct(s, dt) for (s, dt) in out_accs]
    res = _call(
        body, name=name, grid=(n_rows // tm,), in_specs=in_specs, out_specs=out_specs, out_shape=out_shape,
        compiler_params=_params(vmem_mb),
    )(*[a for (a, _, _) in rows], *consts)
    return res[:n_or], res[n_or:]


def _stage_inproj(x, norm_g, w_in_all):
    S = x.shape[0]

    def fn(rv, cr, out):
        g_ref, w_ref = cr
        y, _, _ = _rms_fwd(rv[0], g_ref[...])
        ub = y.astype(BF16)
        for j in range(N_DEV):
            out[1][:, j * SHARD_IN:(j + 1) * SHARD_IN] = jnp.dot(ub, w_ref[j], preferred_element_type=F32)
        return [ub, None], []

    (u, proj), _ = _rowwise("inproj", fn, S, 256, [(x, D_MODEL, 0)], [norm_g, w_in_all],
                            [(D_MODEL, BF16), (IN_COLS, F32)], [], 56)
    return u, proj


def _stage_branches(o, proj, ys, x, hg_norm_g, w_o_hg, w_glu, b_glu, w_o_s5, w_out):
    S = x.shape[0]

    def fn(rv, cr, out):
        o_b, g_hg, z_s, gate_hg, gate_s5, ys_b, x_b = rv
        gn_ref, wohg_ref, wglu_ref, bglu_ref, wos5_ref, wout_ref = cr
        on, _, _ = _head_rms_fwd(o_b, gn_ref[...])
        a = on * _silu(g_hg)
        y_hg = jnp.dot(a.astype(BF16), wohg_ref[...], preferred_element_type=F32)
        gl = _gelu(ys_b)
        glu = jnp.dot(gl.astype(BF16), wglu_ref[...], preferred_element_type=F32) + bglu_ref[...]
        ys2 = glu[:, :S5_WIDTH] * _sigmoid(glu[:, S5_WIDTH:]) * _silu(z_s)
        y_s5 = jnp.dot(ys2.astype(BF16), wos5_ref[...], preferred_element_type=F32)
        merged = _sigmoid(gate_hg) * y_hg + _sigmoid(gate_s5) * y_s5
        h1 = x_b + jnp.dot(merged.astype(BF16), wout_ref[...], preferred_element_type=F32)
        return [y_hg, y_s5, glu, h1], []

    rows = [(o, D_MODEL, 0), (proj, D_MODEL, COL_G // D_MODEL), (proj, S5_WIDTH, COL_ZS // S5_WIDTH),
            (proj, D_MODEL, COL_GH // D_MODEL), (proj, D_MODEL, COL_GS // D_MODEL), (ys, S5_WIDTH, 0), (x, D_MODEL, 0)]
    (y_hg, y_s5, glu, h1), _ = _rowwise(
        "branches", fn, S, 256, rows, [hg_norm_g, w_o_hg, w_glu, b_glu, w_o_s5, w_out],
        [(D_MODEL, F32)] * 4, [], 56)
    return y_hg, y_s5, glu, h1


def _stage_ple_loss(h1, p, target, ple_norm_g, w_ple, w_ple_gate, final_norm_g):
    S = h1.shape[0]

    def fn(rv, cr, out):
        h1_b, p_b, t_b = rv
        gp_ref, wple_ref, wpg_ref, gf_ref = cr
        n2g, n2, r2 = _rms_fwd(h1_b, gp_ref[...])
        z = jnp.dot(n2g.astype(BF16), wpg_ref[...], preferred_element_type=F32)
        gate = _sigmoid(z)
        pe = jnp.dot(p_b.astype(BF16), wple_ref[...], preferred_element_type=F32)
        h2 = h1_b + pe * gate
        y, nf, rf = _rms_fwd(h2, gf_ref[...])
        err = y - t_b
        loss_rows = 0.5 * jnp.mean(err * err, axis=-1, keepdims=True)
        loss_inc = jnp.broadcast_to(jnp.sum(loss_rows, axis=0, keepdims=True), (1, 128))
        dy = err * (1.0 / D_MODEL)
        dh2, d_gf = _rms_bwd(dy, nf, rf, gf_ref[...])
        d_pe = dh2 * gate
        dz = dh2 * pe * gate * (1.0 - gate)
        d_wple = _bdot_tn(p_b, d_pe)
        d_wpg = _bdot_tn(n2g, dz)
        dn2g = _bdot_nt(dz, wpg_ref[...])
        dh1n, d_gp = _rms_bwd(dn2g, n2, r2, gp_ref[...])
        return [dh2 + dh1n], [loss_inc, d_gf, d_gp, d_wple, d_wpg]

    (dh1,), accs = _rowwise(
        "ple_loss", fn, S, 256, [(h1, D_MODEL, 0), (p, 256, 0), (target, D_MODEL, 0)],
        [ple_norm_g, w_ple, w_ple_gate, final_norm_g], [(D_MODEL, F32)],
        [((1, 128), F32), ((1, D_MODEL), F32), ((1, D_MODEL), F32), ((256, D_MODEL), F32), ((D_MODEL, D_MODEL), F32)], 56)
    return dh1, accs


def _stage_bwd_merge_hg(dh1, y_hg, y_s5, proj, o, hg_norm_g, w_out, w_o_hg):
    S = dh1.shape[0]

    def fn(rv, cr, out):
        dh1_b, yhg, ys5, gate_hg, gate_s5, o_b, g_hg = rv
        gn_ref, wout_ref, wohg_ref = cr
        sg_h, sg_s = _sigmoid(gate_hg), _sigmoid(gate_s5)
        merged = sg_h * yhg + sg_s * ys5
        d_wout = _bdot_tn(merged, dh1_b)
        d_merged = _bdot_nt(dh1_b, wout_ref[...])
        d_gate_hg = d_merged * yhg * sg_h * (1.0 - sg_h)
        d_gate_s5 = d_merged * ys5 * sg_s * (1.0 - sg_s)
        d_yhg = d_merged * sg_h
        d_ys5 = d_merged * sg_s
        ong, on, rs = _head_rms_fwd(o_b, gn_ref[...])
        sil = _silu(g_hg)
        a = ong * sil
        d_wohg = _bdot_tn(a, d_yhg)
        d_a = _bdot_nt(d_yhg, wohg_ref[...])
        d_g_hg = d_a * ong * _dsilu(g_hg)
        d_o, d_gn = _head_rms_bwd(d_a * sil, on, rs, gn_ref[...])
        return [d_o, d_g_hg, d_gate_hg, d_gate_s5, d_ys5], [d_wout, d_wohg, d_gn]

    rows = [(dh1, D_MODEL, 0), (y_hg, D_MODEL, 0), (y_s5, D_MODEL, 0), (proj, D_MODEL, COL_GH // D_MODEL),
            (proj, D_MODEL, COL_GS // D_MODEL), (o, D_MODEL, 0), (proj, D_MODEL, COL_G // D_MODEL)]
    outs, accs = _rowwise(
        "bwd_merge_hg", fn, S, 128, rows, [hg_norm_g, w_out, w_o_hg], [(D_MODEL, F32)] * 5,
        [((D_MODEL, D_MODEL), F32), ((D_MODEL, D_MODEL), F32), ((1, D_MODEL), F32)], 56)
    return outs, accs


def _stage_bwd_s5_path(d_ys5, ys, glu, proj, w_o_s5, w_glu):
    S = ys.shape[0]

    def fn(rv, cr, out):
        d_ys5_b, ys_b, glu_b, z_s = rv
        wos5_ref, wglu_ref = cr
        ga, gb = glu_b[:, :S5_WIDTH], glu_b[:, S5_WIDTH:]
        sgb, silz = _sigmoid(gb), _silu(z_s)
        ys2 = ga * sgb * silz
        d_wos5 = _bdot_tn(ys2, d_ys5_b)
        d_ys2 = _bdot_nt(d_ys5_b, wos5_ref[...])
        d_ga = d_ys2 * sgb * silz
        d_gb = d_ys2 * ga * sgb * (1.0 - sgb) * silz
        d_z = d_ys2 * ga * sgb * _dsilu(z_s)
        d_glu = jnp.concatenate([d_ga, d_gb], axis=1)
        gl = _gelu(ys_b)
        d_wglu = _bdot_tn(gl, d_glu)
        d_bglu = jnp.sum(d_glu, axis=0, keepdims=True)
        d_gl = _bdot_nt(d_glu, wglu_ref[...])
        return [d_gl * _dgelu(ys_b), d_z], [d_wos5, d_wglu, d_bglu]

    rows = [(d_ys5, D_MODEL, 0), (ys, S5_WIDTH, 0), (glu, D_MODEL, 0), (proj, S5_WIDTH, COL_ZS // S5_WIDTH)]
    outs, accs = _rowwise(
        "bwd_s5_path", fn, S, 256, rows, [w_o_s5, w_glu], [(S5_WIDTH, F32), (S5_WIDTH, F32)],
        [((S5_WIDTH, D_MODEL), F32), ((S5_WIDTH, D_MODEL), F32), ((1, D_MODEL), F32)], 48)
    return outs, accs


def _stage_inproj_bwd(pieces, x, dh1, norm_g, w_in_all):
    S = x.shape[0]

    def fn(rv, cr, out):
        g_ref, w_ref = cr
        x_b, dh1_b = rv[8], rv[9]
        dproj_ref = out[1]
        col = 0
        for v in rv[:8]:
            dproj_ref[:, col:col + v.shape[1]] = v.astype(BF16)
            col += v.shape[1]
        d_u = jnp.zeros((x_b.shape[0], D_MODEL), F32)
        for j in range(N_DEV):
            d_u = d_u + lax.dot_general(dproj_ref[:, j * SHARD_IN:(j + 1) * SHARD_IN], w_ref[j],
                                        (((1,), (1,)), ((), ())), preferred_element_type=F32)
        _, n, r = _rms_fwd(x_b, g_ref[...])
        dx, d_g = _rms_bwd(d_u, n, r, g_ref[...])
        return [dh1_b + dx, None], [d_g]

    rows = [(a, a.shape[1], 0) for a in pieces] + [(x, D_MODEL, 0), (dh1, D_MODEL, 0)]
    (grad_x, dproj), (d_g,) = _rowwise(
        "inproj_bwd", fn, S, 128, rows, [norm_g, w_in_all], [(D_MODEL, F32), (IN_COLS, BF16)],
        [((1, D_MODEL), F32)], 56)
    return grad_x, dproj, d_g


def _grad_w_in(dproj, u):
    S = u.shape[0]
    tm = 512
    n_i = S // tm

    def body(dp_ref, u_ref, out_ref, acc):
        i = pl.program_id(1)

        @pl.when(i == 0)
        def _():
            acc[...] = jnp.zeros(acc.shape, F32)

        acc[...] += lax.dot_general(dp_ref[...], u_ref[...], (((0,), (0,)), ((), ())), preferred_element_type=F32)

        @pl.when(i == n_i - 1)
        def _():
            out_ref[...] = acc[...].T.astype(BF16)

    return _call(
        body, name="grad_w_in", grid=(N_DEV, n_i),
        in_specs=[pl.BlockSpec((tm, SHARD_IN), lambda j, i: (i, j)), pl.BlockSpec((tm, D_MODEL), lambda j, i: (i, 0))],
        out_specs=pl.BlockSpec((None, D_MODEL, SHARD_IN), lambda j, i: (j, 0, 0)),
        out_shape=jax.ShapeDtypeStruct((N_DEV, D_MODEL, SHARD_IN), BF16),
        scratch_shapes=[pltpu.VMEM((SHARD_IN, D_MODEL), F32)],
        compiler_params=_params(32, 2),
    )(dproj, u)


def _chunk_row(shape):
    return lax.broadcasted_iota(jnp.int32, shape, 0) & (HG_CHUNK - 1)


def _chunk_cumsum(x):
    r_in = _chunk_row(x.shape)
    s = 1
    while s < HG_CHUNK:
        x = x + jnp.where(r_in >= s, pltpu.roll(x, s, 0), 0.0)
        s *= 2
    return x


def _chunk_suffix_sum(x):
    n = x.shape[0]
    r_in = _chunk_row(x.shape)
    s = 1
    while s < HG_CHUNK:
        x = x + jnp.where(r_in < HG_CHUNK - s, pltpu.roll(x, n - s, 0), 0.0)
        s *= 2
    return x


def _hgrn_prep(q, fl, lb):
    nc = HG_SUPER // HG_CHUNK
    sig = _sigmoid(fl)
    f = lb + (1.0 - lb) * sig
    k = (1.0 - lb) * (1.0 - sig)
    b = _chunk_cumsum(jnp.log(f))
    b3 = b.reshape(nc, HG_CHUNK, HG_DIM)
    row3 = lax.broadcasted_iota(jnp.int32, b3.shape, 1)
    pick = lambda r: jnp.sum(jnp.where(row3 == r, b3, 0.0), axis=1, keepdims=True)
    b_mid = pick(HG_CHUNK // 2 - 1)
    b_last = pick(HG_CHUNK - 1)
    flat = lambda t: t.reshape(HG_SUPER, HG_DIM)
    e_qa = flat(jnp.exp(b3 - b_mid))
    e_ka = flat(jnp.exp(b_mid - b3))
    e_qd = jnp.exp(b)
    e_kd = flat(jnp.exp(b_last - b3))
    dc = jnp.exp(b_last)
    return sig, f, k, e_qa, e_ka, e_qd, e_kd, dc


def _hgrn_mask():
    r = lax.broadcasted_iota(jnp.int32, (HG_SUPER, HG_SUPER), 0)
    c = lax.broadcasted_iota(jnp.int32, (HG_SUPER, HG_SUPER), 1)
    shift = HG_CHUNK.bit_length() - 1
    return (jnp.right_shift(r, shift) == jnp.right_shift(c, shift)) & (r >= c)


def _hgrn_fwd(proj, lb):
    S = proj.shape[0]
    nb = S // HG_SUPER
    nc = HG_SUPER // HG_CHUNK

    def body(q_ref, f_ref, iv_ref, lb_ref, o_ref, st_ref, state):
        @pl.when(pl.program_id(1) == 0)
        def _():
            state[...] = jnp.zeros(state.shape, F32)

        q, iv = q_ref[...], iv_ref[...]
        _, _, k, e_qa, e_ka, e_qd, e_kd, dc = _hgrn_prep(q, f_ref[...], lb_ref[...])
        scores = jnp.where(_hgrn_mask(), _bdot_nt(q * e_qa, k * e_ka), 0.0)
        o_intra = _bdot(scores, iv)
        qd, kd = q * e_qd, k * e_kd
        for c in range(nc):
            sl = slice(c * HG_CHUNK, (c + 1) * HG_CHUNK)
            st = state[...]
            st_ref[c] = st
            o_ref[sl, :] = o_intra[sl] + _bdot_nt(qd[sl], st)
            state[...] = dc[c] * st + _bdot_tn(iv[sl], kd[sl])

    blk = lambda base: pl.BlockSpec((HG_SUPER, HG_DIM), functools.partial(lambda h, i, b: (i, b + h), b=base // HG_DIM))
    return _call(
        body, name="hgrn_fwd", grid=(HG_HEADS, nb),
        in_specs=[blk(COL_Q), blk(COL_F), blk(COL_I), pl.BlockSpec((1, HG_DIM), lambda h, i: (0, h))],
        out_specs=[pl.BlockSpec((HG_SUPER, HG_DIM), lambda h, i: (i, h)),
                   pl.BlockSpec((None, nc, HG_DIM, HG_DIM), lambda h, i: (h, i, 0, 0))],
        out_shape=[jax.ShapeDtypeStruct((S, D_MODEL), F32),
                   jax.ShapeDtypeStruct((HG_HEADS, S // HG_CHUNK, HG_DIM, HG_DIM), F32)],
        scratch_shapes=[pltpu.VMEM((HG_DIM, HG_DIM), F32)],
        compiler_params=_params(32, 2),
    )(proj, proj, proj, lb)


def _hgrn_bwd(proj, lb, d_o, states):
    S = proj.shape[0]
    nb = S // HG_SUPER
    nc = HG_SUPER // HG_CHUNK

    def body(q_ref, f_ref, iv_ref, lb_ref, do_ref, st_ref, dq_ref, df_ref, div_ref, dlb_ref, dstate):
        @pl.when(pl.program_id(1) == 0)
        def _():
            dstate[...] = jnp.zeros(dstate.shape, F32)
            dlb_ref[...] = jnp.zeros(dlb_ref.shape, F32)

        q, iv, do, lb_v = q_ref[...], iv_ref[...], do_ref[...], lb_ref[...]
        sig, f, k, e_qa, e_ka, e_qd, e_kd, dc = _hgrn_prep(q, f_ref[...], lb_v)
        mask = _hgrn_mask()
        qa, ka, qd, kd = q * e_qa, k * e_ka, q * e_qd, k * e_kd
        scores = jnp.where(mask, _bdot_nt(qa, ka), 0.0)
        d_scores = jnp.where(mask, _bdot_nt(do, iv), 0.0)
        d_iv_intra = _bdot_tn(scores, do)
        d_qa = _bdot(d_scores, ka)
        d_ka = _bdot_tn(d_scores, qa)
        d_qd, d_kd, d_iv, d_last = [None] * nc, [None] * nc, [None] * nc, [None] * nc
        for c in reversed(range(nc)):
            sl = slice(c * HG_CHUNK, (c + 1) * HG_CHUNK)
            st = st_ref[c]
            ds = dstate[...]
            d_qd[c] = _bdot(do[sl], st)
            d_kd[c] = _bdot(iv[sl], ds)
            d_iv[c] = d_iv_intra[sl] + _bdot_nt(kd[sl], ds)
            d_last[c] = (jnp.sum(ds * st, axis=0, keepdims=True) * dc[c]
                         + jnp.sum(d_kd[c] * kd[sl], axis=0, keepdims=True))
            dstate[...] = dc[c] * ds + _bdot_tn(do[sl], qd[sl])
        d_qd = jnp.concatenate(d_qd, axis=0)
        d_kd = jnp.concatenate(d_kd, axis=0)
        d_b = d_qa * qa - d_ka * ka + d_qd * qd - d_kd * kd
        last_rows = jnp.concatenate([jnp.broadcast_to(t, (HG_CHUNK, HG_DIM)) for t in d_last], axis=0)
        d_b = d_b + jnp.where(_chunk_row(d_b.shape) == HG_CHUNK - 1, last_rows, 0.0)
        d_logf = _chunk_suffix_sum(d_b)
        d_k = d_ka * e_ka + d_kd * e_kd
        g_f = d_logf / f
        d_sig = (g_f - d_k) * (1.0 - lb_v)
        dq_ref[...] = d_qa * e_qa + d_qd * e_qd
        df_ref[...] = d_sig * sig * (1.0 - sig)
        div_ref[...] = jnp.concatenate(d_iv, axis=0)
        d_lb = jnp.sum((g_f - d_k) * (1.0 - sig), axis=0, keepdims=True)
        dlb_ref[...] += jnp.broadcast_to(d_lb, dlb_ref.shape)

    rev = lambda i: nb - 1 - i
    blk = lambda base: pl.BlockSpec((HG_SUPER, HG_DIM), functools.partial(lambda h, i, b: (rev(i), b + h), b=base // HG_DIM))
    row_out = pl.BlockSpec((HG_SUPER, HG_DIM), lambda h, i: (rev(i), h))
    dq, df, div, dlb = _call(
        body, name="hgrn_bwd", grid=(HG_HEADS, nb),
        in_specs=[blk(COL_Q), blk(COL_F), blk(COL_I), pl.BlockSpec((1, HG_DIM), lambda h, i: (0, h)),
                  pl.BlockSpec((HG_SUPER, HG_DIM), lambda h, i: (rev(i), h)),
                  pl.BlockSpec((None, nc, HG_DIM, HG_DIM), lambda h, i: (h, rev(i), 0, 0))],
        out_specs=[row_out, row_out, row_out, pl.BlockSpec((8, HG_DIM), lambda h, i: (0, h))],
        out_shape=[jax.ShapeDtypeStruct((S, D_MODEL), F32)] * 3 + [jax.ShapeDtypeStruct((8, D_MODEL), F32)],
        scratch_shapes=[pltpu.VMEM((HG_DIM, HG_DIM), F32)],
        compiler_params=_params(32, 2),
    )(proj, proj, proj, lb, d_o, states)
    return dq, df, div, dlb[0:1]


def _s5_matrices(a_re, a_im, log_dt, b_re, b_im, c_re, c_im, d, seg_len):
    dt = jnp.exp(log_dt)[:, None]
    mag = jnp.exp(a_re * dt)
    lr, li = mag * jnp.cos(a_im * dt), mag * jnp.sin(a_im * dt)
    den = a_re * a_re + a_im * a_im
    nr = lr - 1.0
    sr = (nr * a_re + li * a_im) / den
    si = (li * a_re - nr * a_im) / den
    bbr = sr[..., None] * b_re - si[..., None] * b_im
    bbi = sr[..., None] * b_im + si[..., None] * b_re
    eye = jnp.eye(8, dtype=F32)

    def quad_cols(v):
        return v.reshape(S5_QUADS, 8 * S5_STATE)

    def lam_row(re_part, im_part):
        row = jnp.concatenate([quad_cols(re_part), quad_cols(im_part)], axis=1).reshape(1, S5_COLS)
        return jnp.broadcast_to(row, (S5_SEG, S5_COLS))

    def b_mat(bb):
        t = bb.reshape(S5_QUADS, 8, S5_STATE, S5_CH)
        return jnp.einsum("qgnc,gh->qgchn", t, eye).reshape(S5_QUADS, 8 * S5_CH, 8 * S5_STATE)

    def c_mat(cc):
        t = cc.reshape(S5_QUADS, 8, S5_CH, S5_STATE)
        return jnp.einsum("qgcn,gh->qgnhc", t, eye).reshape(S5_QUADS, 8 * S5_STATE, 8 * S5_CH)

    ang = a_im * dt * seg_len
    magp = jnp.exp(a_re * dt * seg_len)
    lpr, lpi = magp * jnp.cos(ang), magp * jnp.sin(ang)
    return dict(
        lam_r=lam_row(lr, lr), lam_i=lam_row(-li, li), lam_ic=lam_row(li, -li),
        b_q=jnp.concatenate([b_mat(bbr), b_mat(bbi)], axis=2),
        c_q=jnp.concatenate([c_mat(c_re), -c_mat(c_im)], axis=1),
        d_row=d.reshape(1, S5_WIDTH), pow_r=quad_cols(lpr), pow_i=quad_cols(lpi),
    )


def _swap_halves(h):
    half = S5_QCOLS // 2
    parts = []
    for q in range(S5_QUADS):
        parts.append(h[:, q * S5_QCOLS + half:(q + 1) * S5_QCOLS])
        parts.append(h[:, q * S5_QCOLS:q * S5_QCOLS + half])
    return jnp.concatenate(parts, axis=1)


def _s5_fwd_pass(u_perm, mats, h0, with_output):
    S = u_perm.shape[0]
    rows = S5_TILE_STEPS * S5_SEG
    nt = S // rows

    def body(*refs):
        if with_output:
            u_ref, b_ref, lr_ref, li_ref, h0_ref, c_ref, d_ref, y_ref, hinit_ref, hend_ref, xs, hcar = refs
        else:
            u_ref, b_ref, lr_ref, li_ref, h0_ref, hend_ref, xs, hcar = refs

        @pl.when(pl.program_id(0) == 0)
        def _():
            hcar[...] = h0_ref[...]

        if with_output:
            hinit_ref[...] = hcar[...]
        u = u_ref[...]
        ub = u.astype(BF16)
        for q in range(S5_QUADS):
            xs[:, q * S5_QCOLS:(q + 1) * S5_QCOLS] = jnp.dot(ub[:, q * 128:(q + 1) * 128], b_ref[q], preferred_element_type=F32)

        def step(t, h):
            sl = pl.ds(pl.multiple_of(t * S5_SEG, S5_SEG), S5_SEG)
            hn = lr_ref[...] * h + li_ref[...] * _swap_halves(h) + xs[sl, :]
            xs[sl, :] = hn
            return hn

        h = lax.fori_loop(0, S5_TILE_STEPS, step, hcar[...])
        hcar[...] = h
        hend_ref[...] = h
        if with_output:
            ys = [jnp.dot(xs[:, q * S5_QCOLS:(q + 1) * S5_QCOLS].astype(BF16), c_ref[q], preferred_element_type=F32)
                  for q in range(S5_QUADS)]
            y_ref[...] = jnp.concatenate(ys, axis=1) + d_ref[...] * u

    full = lambda a: pl.BlockSpec(a.shape, functools.partial(lambda i, nd: (0,) * nd, nd=a.ndim))
    ins = [u_perm, mats["b_q"], mats["lam_r"], mats["lam_i"], h0]
    in_specs = [pl.BlockSpec((rows, S5_WIDTH), lambda i: (i, 0))] + [full(a) for a in ins[1:]]
    out_specs = [pl.BlockSpec((S5_SEG, S5_COLS), lambda i: (0, 0))]
    out_shape = [jax.ShapeDtypeStruct((S5_SEG, S5_COLS), F32)]
    if with_output:
        ins += [mats["c_q"], mats["d_row"]]
        in_specs += [full(mats["c_q"]), full(mats["d_row"])]
        out_specs = [pl.BlockSpec((rows, S5_WIDTH), lambda i: (i, 0)),
                     pl.BlockSpec((None, S5_SEG, S5_COLS), lambda i: (i, 0, 0))] + out_specs
        out_shape = [jax.ShapeDtypeStruct((S, S5_WIDTH), F32), jax.ShapeDtypeStruct((nt, S5_SEG, S5_COLS), F32)] + out_shape
    return _call(
        body, name="s5_fwd_y" if with_output else "s5_fwd_ends", grid=(nt,), in_specs=in_specs, out_specs=out_specs,
        out_shape=out_shape,
        scratch_shapes=[pltpu.VMEM((rows, S5_COLS), F32), pltpu.VMEM((S5_SEG, S5_COLS), F32)],
        compiler_params=_params(40),
    )(*ins)


def _s5_bwd_ends(dy_perm, mats):
    S = dy_perm.shape[0]
    rows = S5_TILE_STEPS * S5_SEG
    nt = S // rows

    def body(dy_ref, c_ref, lr_ref, lic_ref, gend_ref, gs, gcar):
        @pl.when(pl.program_id(0) == 0)
        def _():
            gcar[...] = jnp.zeros(gcar.shape, F32)

        dyb = dy_ref[...].astype(BF16)
        for q in range(S5_QUADS):
            gs[:, q * S5_QCOLS:(q + 1) * S5_QCOLS] = lax.dot_general(
                dyb[:, q * 128:(q + 1) * 128], c_ref[q], (((1,), (1,)), ((), ())), preferred_element_type=F32)

        def step(k, g):
            t = S5_TILE_STEPS - 1 - k
            sl = pl.ds(pl.multiple_of(t * S5_SEG, S5_SEG), S5_SEG)
            return lr_ref[...] * g + lic_ref[...] * _swap_halves(g) + gs[sl, :]

        g = lax.fori_loop(0, S5_TILE_STEPS, step, gcar[...])
        gcar[...] = g
        gend_ref[...] = g

    full = lambda a: pl.BlockSpec(a.shape, functools.partial(lambda i, nd: (0,) * nd, nd=a.ndim))
    return _call(
        body, name="s5_bwd_ends", grid=(nt,),
        in_specs=[pl.BlockSpec((rows, S5_WIDTH), lambda i: (nt - 1 - i, 0)), full(mats["c_q"]), full(mats["lam_r"]),
                  full(mats["lam_ic"])],
        out_specs=pl.BlockSpec((S5_SEG, S5_COLS), lambda i: (0, 0)),
        out_shape=jax.ShapeDtypeStruct((S5_SEG, S5_COLS), F32),
        scratch_shapes=[pltpu.VMEM((rows, S5_COLS), F32), pltpu.VMEM((S5_SEG, S5_COLS), F32)],
        compiler_params=_params(40),
    )(dy_perm, mats["c_q"], mats["lam_r"], mats["lam_ic"])


def _s5_bwd_full(u_perm, dy_perm, hinit, g0, mats):
    S = u_perm.shape[0]
    rows = S5_TILE_STEPS * S5_SEG
    nt = S // rows

    def body(u_ref, dy_ref, hinit_ref, g0_ref, b_ref, c_ref, lr_ref, li_ref, lic_ref, d_ref,
             du_ref, dp_ref, dq_ref, db_ref, dc_ref, dd_ref, hs, gs, gcar):
        @pl.when(pl.program_id(0) == 0)
        def _():
            gcar[...] = g0_ref[...]
            for ref in (dp_ref, dq_ref, db_ref, dc_ref, dd_ref):
                ref[...] = jnp.zeros(ref.shape, F32)

        u, dy = u_ref[...], dy_ref[...]
        ub, dyb = u.astype(BF16), dy.astype(BF16)
        hs[0:S5_SEG, :] = hinit_ref[...]
        for q in range(S5_QUADS):
            cols = slice(q * S5_QCOLS, (q + 1) * S5_QCOLS)
            hs[S5_SEG:, cols] = jnp.dot(ub[:, q * 128:(q + 1) * 128], b_ref[q], preferred_element_type=F32)
            gs[:, cols] = lax.dot_general(dyb[:, q * 128:(q + 1) * 128], c_ref[q], (((1,), (1,)), ((), ())),
                                          preferred_element_type=F32)

        def fstep(t, h):
            sl = pl.ds(pl.multiple_of((t + 1) * S5_SEG, S5_SEG), S5_SEG)
            hn = lr_ref[...] * h + li_ref[...] * _swap_halves(h) + hs[sl, :]
            hs[sl, :] = hn
            return hn

        lax.fori_loop(0, S5_TILE_STEPS, fstep, hinit_ref[...])

        def bstep(k, g):
            t = S5_TILE_STEPS - 1 - k
            sl = pl.ds(pl.multiple_of(t * S5_SEG, S5_SEG), S5_SEG)
            gn = lr_ref[...] * g + lic_ref[...] * _swap_halves(g) + gs[sl, :]
            gs[sl, :] = gn
            return gn

        gcar[...] = lax.fori_loop(0, S5_TILE_STEPS, bstep, gcar[...])

        half = S5_QCOLS // 2
        dus = []
        for q in range(S5_QUADS):
            cols = slice(q * S5_QCOLS, (q + 1) * S5_QCOLS)

            def astep(t, carry, q=q):
                sl = pl.ds(pl.multiple_of(t * S5_SEG, S5_SEG), S5_SEG)
                g = gs[sl, q * S5_QCOLS:(q + 1) * S5_QCOLS]
                hp = hs[sl, q * S5_QCOLS:(q + 1) * S5_QCOLS]
                hp_sw = jnp.concatenate([hp[:, half:], hp[:, :half]], axis=1)
                return carry[0] + g * hp, carry[1] + g * hp_sw

            zero = jnp.zeros((S5_SEG, S5_QCOLS), F32)
            acc_p, acc_q = lax.fori_loop(0, S5_TILE_STEPS, astep, (zero, zero))
            dp_ref[:, cols] += jnp.sum(acc_p, axis=0, keepdims=True)
            dq_ref[:, cols] += jnp.sum(acc_q, axis=0, keepdims=True)
            gq = gs[:, cols].astype(BF16)
            db_ref[q] += lax.dot_general(ub[:, q * 128:(q + 1) * 128], gq, (((0,), (0,)), ((), ())),
                                         preferred_element_type=F32)
            hq = hs[S5_SEG:, cols].astype(BF16)
            dc_ref[q] += lax.dot_general(dyb[:, q * 128:(q + 1) * 128], hq, (((0,), (0,)), ((), ())),
                                         preferred_element_type=F32)
            dus.append(lax.dot_general(gq, b_ref[q], (((1,), (1,)), ((), ())), preferred_element_type=F32))
        du_ref[...] = jnp.concatenate(dus, axis=1) + d_ref[...] * dy
        dd_ref[...] += jnp.sum(dy * u, axis=0, keepdims=True)

    full = lambda a: pl.BlockSpec(a.shape, functools.partial(lambda i, nd: (0,) * nd, nd=a.ndim))
    rev_rows = pl.BlockSpec((rows, S5_WIDTH), lambda i: (nt - 1 - i, 0))
    consts = [mats["b_q"], mats["c_q"], mats["lam_r"], mats["lam_i"], mats["lam_ic"], mats["d_row"]]
    acc = lambda s: pl.BlockSpec(s, functools.partial(lambda i, nd: (0,) * nd, nd=len(s)))
    acc_shapes = [(1, S5_COLS), (1, S5_COLS), (S5_QUADS, 128, S5_QCOLS), (S5_QUADS, 128, S5_QCOLS), (1, S5_WIDTH)]
    return _call(
        body, name="s5_bwd_full", grid=(nt,),
        in_specs=[rev_rows, rev_rows, pl.BlockSpec((None, S5_SEG, S5_COLS), lambda i: (nt - 1 - i, 0, 0)), full(g0)]
        + [full(a) for a in consts],
        out_specs=[rev_rows] + [acc(s) for s in acc_shapes],
        out_shape=[jax.ShapeDtypeStruct((S, S5_WIDTH), F32)] + [jax.ShapeDtypeStruct(s, F32) for s in acc_shapes],
        scratch_shapes=[pltpu.VMEM((rows + S5_SEG, S5_COLS), F32), pltpu.VMEM((rows, S5_COLS), F32),
                        pltpu.VMEM((S5_SEG, S5_COLS), F32)],
        compiler_params=_params(56),
    )(u_perm, dy_perm, hinit, g0, *consts)


def _cmul(ar, ai, br, bi):
    return ar * br - ai * bi, ar * bi + ai * br


def _split_cols(v):
    t = v.reshape(v.shape[0], S5_QUADS, 2, S5_QCOLS // 2)
    return t[:, :, 0], t[:, :, 1]


def _join_cols(re, im):
    return jnp.stack([re, im], axis=2).reshape(re.shape[0], S5_COLS)


def _segment_starts(ends, pow_r, pow_i, reverse):
    er, ei = _split_cols(ends)
    pi = -pow_i if reverse else pow_i
    order = list(range(S5_SEG))
    if reverse:
        order = order[::-1]
    cr, ci = jnp.zeros_like(er[0]), jnp.zeros_like(ei[0])
    out_r, out_i = [None] * S5_SEG, [None] * S5_SEG
    for j in order:
        out_r[j], out_i[j] = cr, ci
        mr, mi = _cmul(pow_r, pi, cr, ci)
        cr, ci = mr + er[j], mi + ei[j]
    return _join_cols(jnp.stack(out_r), jnp.stack(out_i))


def _to_segments(a):
    S, w = a.shape
    return a.reshape(S5_SEG, S // S5_SEG, w).transpose(1, 0, 2).reshape(S, w)


def _from_segments(a):
    S, w = a.shape
    return a.reshape(S // S5_SEG, S5_SEG, w).transpose(1, 0, 2).reshape(S, w)


def _my_pos():
    return lax.axis_index("x"), lax.axis_index("y"), lax.axis_index("c")


def _flip(pos, k):
    x, y, c = pos
    return (1 - x if k & 4 else x, 1 - y if k & 2 else y, 1 - c if k & 1 else c)


def _index_of(pos):
    return 4 * pos[0] + 2 * pos[1] + pos[2]


def _all_gather(arrays):
    n = len(arrays)

    def body(*refs):
        in_refs, out_refs = refs[:n], refs[n:2 * n]
        send_sems, recv_sems, local_sems = refs[2 * n:]
        me = _my_pos()
        mine = _index_of(me)
        local = [pltpu.make_async_copy(in_refs[a], out_refs[a].at[mine], local_sems.at[a]) for a in range(n)]
        for cp in local:
            cp.start()

        def copy(a, k, src_idx, to):
            return pltpu.make_async_remote_copy(
                src_ref=out_refs[a].at[src_idx], dst_ref=out_refs[a].at[src_idx],
                send_sem=send_sems.at[a * 7 + k - 1], recv_sem=recv_sems.at[a * 7 + k - 1],
                device_id=to, device_id_type=MESH)

        def first(a, k):
            return pltpu.make_async_remote_copy(
                src_ref=in_refs[a], dst_ref=out_refs[a].at[mine],
                send_sem=send_sems.at[a * 7 + k - 1], recv_sem=recv_sems.at[a * 7 + k - 1],
                device_id=_flip(me, k), device_id_type=MESH)

        sent = []
        for a in range(n):
            for k in range(1, 8):
                cp = first(a, k)
                cp.start()
                sent.append(cp)
        for a in range(n):
            for k in range(1, 8):
                copy(a, k, _index_of(_flip(me, k)), me).wait_recv()
        for cp in sent:
            cp.wait_send()
        for cp in local:
            cp.wait()

    any_spec = pl.BlockSpec(memory_space=pl.ANY)
    return _call(
        body, name="all_gather_weights",
        in_specs=[any_spec] * n, out_specs=[any_spec] * n,
        out_shape=[jax.ShapeDtypeStruct((N_DEV,) + a.shape, a.dtype) for a in arrays],
        scratch_shapes=[pltpu.SemaphoreType.DMA((7 * n,)), pltpu.SemaphoreType.DMA((7 * n,)),
                        pltpu.SemaphoreType.DMA((n,))],
    )(*arrays)


def _reduce_scatter(arrays):
    n = len(arrays)
    chunk = 16

    def body(*refs):
        in_refs, out_refs = refs[:n], refs[n:2 * n]
        own = refs[2 * n:3 * n]
        land = refs[3 * n:4 * n]
        send_sems, recv_sems, local_sems = refs[4 * n:]
        me = _my_pos()
        mine = _index_of(me)
        local = [pltpu.make_async_copy(in_refs[a].at[mine], own[a], local_sems.at[a]) for a in range(n)]
        for cp in local:
            cp.start()

        def copy(a, k, to):
            return pltpu.make_async_remote_copy(
                src_ref=in_refs[a].at[_index_of(to)], dst_ref=land[a].at[k - 1],
                send_sem=send_sems.at[a * 7 + k - 1], recv_sem=recv_sems.at[a * 7 + k - 1],
                device_id=to, device_id_type=MESH)

        sent = []
        for a in range(n):
            for k in range(1, 8):
                cp = copy(a, k, _flip(me, k))
                cp.start()
                sent.append(cp)
        for a in range(n):
            local[a].wait()
            for k in range(1, 8):
                copy(a, k, me).wait_recv()
            rows = own[a].shape[0]

            def add_rows(i, carry, a=a):
                sl = pl.ds(pl.multiple_of(i * chunk, chunk), chunk)
                acc = own[a][sl, :].astype(F32)
                for k in range(7):
                    acc = acc + land[a][k, sl, :].astype(F32)
                out_refs[a][sl, :] = acc
                return carry

            lax.fori_loop(0, rows // chunk, add_rows, 0)
        for cp in sent:
            cp.wait_send()

    any_spec = pl.BlockSpec(memory_space=pl.ANY)
    vmem = pl.BlockSpec(memory_space=pltpu.VMEM)
    scratch = [pltpu.VMEM(a.shape[1:], a.dtype) for a in arrays]
    scratch += [pltpu.VMEM((7,) + a.shape[1:], a.dtype) for a in arrays]
    scratch += [pltpu.SemaphoreType.DMA((7 * n,)), pltpu.SemaphoreType.DMA((7 * n,)), pltpu.SemaphoreType.DMA((n,))]
    return _call(
        body, name="reduce_scatter_grads",
        in_specs=[any_spec] * n, out_specs=[vmem] * n,
        out_shape=[jax.ShapeDtypeStruct(a.shape[1:], F32) for a in arrays],
        scratch_shapes=scratch,
        compiler_params=pltpu.CompilerParams(vmem_limit_bytes=56 * MIB),
    )(*arrays)


def _adam_update(g, w, m, v):
    m2 = ADAM_B1 * m + (1.0 - ADAM_B1) * g
    v2 = ADAM_B2 * v + (1.0 - ADAM_B2) * (g * g)
    m_hat = m2 / (1.0 - ADAM_B1 ** ADAM_STEP)
    v_hat = v2 / (1.0 - ADAM_B2 ** ADAM_STEP)
    delta = -ADAM_LR * (m_hat / (jnp.sqrt(v_hat) + ADAM_EPS) + ADAM_WD * w)
    return delta, m2, v2


def _small_allreduce_adam(partial, w, m, v):
    rows = partial.shape[0]

    def body(p_ref, w_ref, m_ref, v_ref, g_ref, d_ref, m2_ref, v2_ref, gath, send_sems, recv_sems):
        me = _my_pos()
        mine = _index_of(me)
        gath[mine] = p_ref[...]
        sent = []
        for k in range(1, 8):
            cp = pltpu.make_async_remote_copy(
                src_ref=p_ref, dst_ref=gath.at[mine], send_sem=send_sems.at[k - 1], recv_sem=recv_sems.at[k - 1],
                device_id=_flip(me, k), device_id_type=MESH)
            cp.start()
            sent.append(cp)
        for k in range(1, 8):
            pltpu.make_async_remote_copy(
                src_ref=p_ref, dst_ref=gath.at[_index_of(_flip(me, k))], send_sem=send_sems.at[k - 1],
                recv_sem=recv_sems.at[k - 1], device_id=me, device_id_type=MESH).wait_recv()
        g = gath[0]
        for dev in range(1, N_DEV):
            g = g + gath[dev]
        delta, m2, v2 = _adam_update(g, w_ref[...], m_ref[...], v_ref[...])
        g_ref[...] = g
        d_ref[...] = delta
        m2_ref[...] = m2
        v2_ref[...] = v2
        for cp in sent:
            cp.wait_send()

    vmem = pl.BlockSpec(memory_space=pltpu.VMEM)
    return _call(
        body, name="small_allreduce_adam", in_specs=[vmem] * 4, out_specs=[vmem] * 4,
        out_shape=[jax.ShapeDtypeStruct((rows, 128), F32)] * 4,
        scratch_shapes=[pltpu.VMEM((N_DEV, rows, 128), F32), pltpu.SemaphoreType.DMA((7,)), pltpu.SemaphoreType.DMA((7,))],
        compiler_params=pltpu.CompilerParams(vmem_limit_bytes=32 * MIB),
    )(partial, w, m, v)


def _adam_rows(g, w, m, v):
    rows, cols = w.shape
    tm = rows if rows <= 256 else 256

    def fn(rv, cr, out):
        return list(_adam_update(*rv)), []

    outs, _ = _rowwise("adamw", fn, rows, tm, [(a, cols, 0) for a in (g, w, m, v)], [], [(cols, F32)] * 3, [], 32)
    return outs


_SMALL = ["norm_g", "hg_lb", "hg_norm_g", "s5_a_re", "s5_a_im", "s5_log_dt", "s5_b_re", "s5_b_im", "s5_c_re",
          "s5_c_im", "s5_d", "b_glu", "ple_norm_g", "final_norm_g"]
_BIG = ["w_in", "w_o_hg", "w_glu", "w_o_s5", "w_out", "w_ple", "w_ple_gate"]
_ORDER = ["norm_g", "w_in", "hg_lb", "hg_norm_g", "w_o_hg", "s5_a_re", "s5_a_im", "s5_log_dt", "s5_b_re", "s5_b_im",
          "s5_c_re", "s5_c_im", "s5_d", "w_glu", "b_glu", "w_o_s5", "w_out", "ple_norm_g", "w_ple", "w_ple_gate",
          "final_norm_g"]


def _pack_small(vals):
    parts = []
    for name in _SMALL:
        flat = vals[name].reshape(-1).astype(F32)
        pad = (-flat.shape[0]) % 1024
        parts.append(jnp.pad(flat, (0, pad)))
    return jnp.concatenate(parts).reshape(-1, 128)


def _unpack_small(packed, like):
    flat = packed.reshape(-1)
    out, off = {}, 0
    for name in _SMALL:
        size = like[name].size
        out[name] = flat[off:off + size].reshape(like[name].shape)
        off += size + (-size) % 1024
    return out


def _col_blocks(full):
    k = full.shape[0]
    return full.reshape(k, N_DEV, 128).transpose(1, 0, 2)


def _from_col_blocks(blocks):
    k = blocks.shape[1]
    return blocks.transpose(1, 0, 2).reshape(k, N_DEV * 128)


def kernel(x, p, norm_g, w_in, hg_lb, hg_norm_g, w_o_hg, s5_a_re, s5_a_im, s5_log_dt, s5_b_re, s5_b_im, s5_c_re, s5_c_im, s5_d, w_glu, b_glu, w_o_s5, w_out, ple_norm_g, w_ple, w_ple_gate, final_norm_g, loss_target, m_norm_g, m_w_in, m_hg_lb, m_hg_norm_g, m_w_o_hg, m_s5_a_re, m_s5_a_im, m_s5_log_dt, m_s5_b_re, m_s5_b_im, m_s5_c_re, m_s5_c_im, m_s5_d, m_w_glu, m_b_glu, m_w_o_s5, m_w_out, m_ple_norm_g, m_w_ple, m_w_ple_gate, m_final_norm_g, v_norm_g, v_w_in, v_hg_lb, v_hg_norm_g, v_w_o_hg, v_s5_a_re, v_s5_a_im, v_s5_log_dt, v_s5_b_re, v_s5_b_im, v_s5_c_re, v_s5_c_im, v_s5_d, v_w_glu, v_b_glu, v_w_o_s5, v_w_out, v_ple_norm_g, v_w_ple, v_w_ple_gate, v_final_norm_g):
    args = dict(locals())
    w = {n: args[n] for n in _ORDER}
    m = {n: args["m_" + n] for n in _ORDER}
    v = {n: args["v_" + n] for n in _ORDER}
    xs = x[0]
    ps = p[0, 0]
    tgt = loss_target[0]
    S = xs.shape[0]

    pack_a = jnp.concatenate([w_o_hg[0], w_out[0], w_ple_gate[0]], axis=0).astype(BF16)
    pack_b = w_in[0].astype(BF16)
    pack_c = jnp.concatenate([w_glu[0], w_o_s5[0], w_ple[0]], axis=0).astype(BF16)
    all_a, all_b, all_c = _all_gather([pack_a, pack_b, pack_c])
    wf_o_hg = all_a[:, 0:128].reshape(D_MODEL, D_MODEL)
    wf_out = all_a[:, 128:256].reshape(D_MODEL, D_MODEL)
    wf_pg = all_a[:, 256:384].reshape(D_MODEL, D_MODEL)
    wf_glu = _from_col_blocks(all_c[:, 0:512])
    wf_o_s5 = _from_col_blocks(all_c[:, 512:1024])
    wf_ple = _from_col_blocks(all_c[:, 1024:1280])

    lb = jax.nn.sigmoid(hg_lb[0:1] - hg_lb[1:2])
    s5_names = ["s5_a_re", "s5_a_im", "s5_log_dt", "s5_b_re", "s5_b_im", "s5_c_re", "s5_c_im", "s5_d"]
    build = lambda *a: _s5_matrices(*a, seg_len=S // S5_SEG)
    mats_f32, mats_vjp = jax.vjp(build, *[w[n][0] for n in s5_names])
    mats = dict(mats_f32, b_q=mats_f32["b_q"].astype(BF16), c_q=mats_f32["c_q"].astype(BF16))
    bias_glu = b_glu

    u, proj = _stage_inproj(xs, norm_g, all_b)
    o, states = _hgrn_fwd(proj, lb)
    u_perm = _to_segments(proj[:, COL_US:COL_US + S5_WIDTH])
    zeros_state = jnp.zeros((S5_SEG, S5_COLS), F32)
    (h_ends,) = _s5_fwd_pass(u_perm, mats, zeros_state, False)
    h0 = _segment_starts(h_ends, mats["pow_r"], mats["pow_i"], False)
    y_perm, h_init, _ = _s5_fwd_pass(u_perm, mats, h0, True)
    ys = _from_segments(y_perm)
    y_hg, y_s5, glu, h1 = _stage_branches(o, proj, ys, xs, hg_norm_g, wf_o_hg, wf_glu, bias_glu, wf_o_s5, wf_out)

    dh1, (loss_acc, d_final_g, d_ple_g, d_w_ple, d_w_pg) = _stage_ple_loss(
        h1, ps, tgt, ple_norm_g, wf_ple, wf_pg, final_norm_g.reshape(1, D_MODEL))
    (d_o, d_g_hg, d_gate_hg, d_gate_s5, d_ys5), (d_w_out, d_w_o_hg, d_hg_norm) = _stage_bwd_merge_hg(
        dh1, y_hg, y_s5, proj, o, hg_norm_g, wf_out, wf_o_hg)
    (d_ys, d_z), (d_w_o_s5, d_w_glu, d_b_glu) = _stage_bwd_s5_path(d_ys5, ys, glu, proj, wf_o_s5, wf_glu)
    dq, df, div, d_lb = _hgrn_bwd(proj, lb, d_o, states)
    dy_perm = _to_segments(d_ys)
    g_ends = _s5_bwd_ends(dy_perm, mats)
    g0 = _segment_starts(g_ends, mats["pow_r"], mats["pow_i"], True)
    du_perm, acc_p, acc_q, d_bq, d_cq_t, d_d = _s5_bwd_full(u_perm, dy_perm, h_init, g0, mats)
    d_us = _from_segments(du_perm)
    grad_x, dproj, d_norm_g = _stage_inproj_bwd([dq, df, div, d_g_hg, d_us, d_z, d_gate_hg, d_gate_s5], xs, dh1,
                                                norm_g, all_b)
    d_w_in_blocks = _grad_w_in(dproj, u)

    p_re, p_im = _split_cols(acc_p)
    q_re, q_im = _split_cols(acc_q)
    d_lam_r = (p_re + p_im)[0]
    d_lam_i = (q_im - q_re)[0]
    zero_row = jnp.zeros((S5_SEG, S5_COLS), F32)
    row_of = lambda re_part, im_part: zero_row.at[0].set(_join_cols(re_part[None], im_part[None])[0])
    zeros_q = jnp.zeros_like(d_lam_r)
    cot = dict(
        lam_r=row_of(d_lam_r, zeros_q), lam_i=row_of(zeros_q, d_lam_i), lam_ic=zero_row,
        b_q=d_bq, c_q=d_cq_t.transpose(0, 2, 1), d_row=d_d,
        pow_r=jnp.zeros_like(mats["pow_r"]), pow_i=jnp.zeros_like(mats["pow_i"]),
    )
    d_s5 = mats_vjp(cot)

    s_lb = lb * (1.0 - lb)
    d_hg_lb = jnp.concatenate([d_lb * s_lb, -d_lb * s_lb], axis=0)
    small_g = dict(norm_g=d_norm_g, hg_lb=d_hg_lb, hg_norm_g=d_hg_norm, b_glu=d_b_glu, ple_norm_g=d_ple_g,
                   final_norm_g=d_final_g)
    for name, g in zip(s5_names, d_s5):
        small_g[name] = g
    pk = lambda d: _pack_small({n: d[n] for n in _SMALL})
    sg, sd, sm, sv = _small_allreduce_adam(pk(small_g), pk(w), pk(m), pk(v))
    like = {n: w[n] for n in _SMALL}
    out_g, out_d, out_m, out_v = (_unpack_small(t, like) for t in (sg, sd, sm, sv))

    rs_a = jnp.concatenate([d_w_o_hg.reshape(N_DEV, 128, D_MODEL), d_w_out.reshape(N_DEV, 128, D_MODEL),
                            d_w_pg.reshape(N_DEV, 128, D_MODEL)], axis=1).astype(BF16)
    rs_c = jnp.concatenate([_col_blocks(d_w_glu), _col_blocks(d_w_o_s5), _col_blocks(d_w_ple)], axis=1).astype(BF16)
    g_a, g_b, g_c = _reduce_scatter([rs_a, d_w_in_blocks, rs_c])
    big_g = dict(w_o_hg=g_a[0:128], w_out=g_a[128:256], w_ple_gate=g_a[256:384], w_in=g_b,
                 w_glu=g_c[0:512], w_o_s5=g_c[512:1024], w_ple=g_c[1024:1280])
    for name in _BIG:
        shape = w[name].shape
        g2 = big_g[name]
        d2, m2, v2 = _adam_rows(g2, w[name][0], m[name][0], v[name][0])
        out_g[name], out_d[name], out_m[name], out_v[name] = (t.reshape(shape) for t in (g2, d2, m2, v2))

    loss = lax.psum(loss_acc[0, 0], ("x", "y", "c"))
    return (loss, grad_x[None], *[out_g[n] for n in _ORDER], *[out_d[n] for n in _ORDER],
            *[out_m[n] for n in _ORDER], *[out_v[n] for n in _ORDER])
```

```python
import functools
import math

import jax
import jax.numpy as jnp
from jax import lax
from jax.experimental import pallas as pl
from jax.experimental.pallas import tpu as pltpu

F32 = jnp.float32
BF16 = jnp.bfloat16

D_MODEL = 1024
N_DEV = 8
IN_COLS = 7168
SHARD_IN = IN_COLS // N_DEV
HG_HEADS = 8
HG_DIM = 128
HG_CHUNK = 64
HG_SUPER = 256
HG_HEADS_PER_STEP = 4
S5_WIDTH = 512
S5_GROUPS = 32
S5_STATE = 64
S5_CH = 16
S5_SEG = 8
S5_QUADS = 4
S5_QCOLS = 1024
S5_COLS = S5_QUADS * S5_QCOLS
S5_TILE_STEPS = 64
NORM_EPS = 1e-6
ADAM_LR = 0.001
ADAM_B1 = 0.9
ADAM_B2 = 0.999
ADAM_EPS = 1e-08
ADAM_WD = 0.01
ADAM_STEP = 10
MIB = 1024 * 1024
MESH = pl.DeviceIdType.MESH

COL_Q, COL_F, COL_I, COL_G, COL_US, COL_ZS, COL_GH, COL_GS = 0, 1024, 2048, 3072, 4096, 4608, 5120, 6144


def _call(body, **kw):
    return pl.pallas_call(body, **kw)


def _params(vmem_mb, n_grid=1):
    return pltpu.CompilerParams(
        dimension_semantics=("arbitrary",) * n_grid, vmem_limit_bytes=vmem_mb * MIB
    )


def _bdot(a, b):
    return jnp.dot(a.astype(BF16), b.astype(BF16), preferred_element_type=F32)


def _bdot_nt(a, b):
    return lax.dot_general(a.astype(BF16), b.astype(BF16), (((1,), (1,)), ((), ())), preferred_element_type=F32)


def _bdot_tn(a, b):
    return lax.dot_general(a.astype(BF16), b.astype(BF16), (((0,), (0,)), ((), ())), preferred_element_type=F32)


def _sigmoid(x):
    return jax.nn.sigmoid(x)


def _silu(x):
    return x * _sigmoid(x)


def _dsilu(x):
    s = _sigmoid(x)
    return s * (1.0 + x * (1.0 - s))


_GELU_C = math.sqrt(2.0 / math.pi)


def _gelu(x):
    return 0.5 * x * (1.0 + jnp.tanh(_GELU_C * (x + 0.044715 * x * x * x)))


def _dgelu(x):
    t = jnp.tanh(_GELU_C * (x + 0.044715 * x * x * x))
    return 0.5 * (1.0 + t) + 0.5 * x * (1.0 - t * t) * _GELU_C * (1.0 + 3.0 * 0.044715 * x * x)


def _rms_fwd(x, g):
    r = lax.rsqrt(jnp.mean(x * x, axis=-1, keepdims=True) + NORM_EPS)
    n = x * r
    return n * g, n, r


def _rms_bwd(dy, n, r, g):
    dn = dy * g
    dx = r * (dn - n * jnp.mean(dn * n, axis=-1, keepdims=True))
    return dx, jnp.sum(dy * n, axis=0, keepdims=True)


def _head_rms_fwd(o, g):
    ns, rs = [], []
    for h in range(HG_HEADS):
        oh = o[:, h * HG_DIM:(h + 1) * HG_DIM]
        r = lax.rsqrt(jnp.mean(oh * oh, axis=-1, keepdims=True) + NORM_EPS)
        ns.append(oh * r)
        rs.append(r)
    n = jnp.concatenate(ns, axis=1)
    return n * g, n, rs


def _head_rms_bwd(dy, n, rs, g):
    dn = dy * g
    dxs = []
    for h in range(HG_HEADS):
        sl = slice(h * HG_DIM, (h + 1) * HG_DIM)
        dxs.append(rs[h] * (dn[:, sl] - n[:, sl] * jnp.mean(dn[:, sl] * n[:, sl], axis=-1, keepdims=True)))
    return jnp.concatenate(dxs, axis=1), jnp.sum(dy * n, axis=0, keepdims=True)


def _rowwise(name, fn, n_rows, tm, rows, consts, out_rows, out_accs, vmem_mb):
    n_r, n_c, n_or, n_oa = len(rows), len(consts), len(out_rows), len(out_accs)

    def body(*refs):
        r_refs = refs[:n_r]
        c_refs = refs[n_r:n_r + n_c]
        or_refs = refs[n_r + n_c:n_r + n_c + n_or]
        oa_refs = refs[n_r + n_c + n_or:]
        outs, accs = fn([r[...] for r in r_refs], c_refs, or_refs)

        if n_oa:
            @pl.when(pl.program_id(0) == 0)
            def _():
                for ref in oa_refs:
                    ref[...] = jnp.zeros(ref.shape, ref.dtype)

        for ref, v in zip(or_refs, outs):
            if v is not None:
                ref[...] = v.astype(ref.dtype)
        for ref, v in zip(oa_refs, accs):
            ref[...] += v.astype(ref.dtype)

    in_specs = [pl.BlockSpec((tm, w), functools.partial(lambda i, c: (i, c), c=cb)) for (_, w, cb) in rows]
    in_specs += [pl.BlockSpec(c.shape, functools.partial(lambda i, nd: (0,) * nd, nd=c.ndim),
                              pipeline_mode=pl.Buffered(1)) for c in consts]
    out_specs = [pl.BlockSpec((tm, w), lambda i: (i, 0)) for (w, _) in out_rows]
    out_specs += [pl.BlockSpec(s, functools.partial(lambda i, nd: (0,) * nd, nd=len(s))) for (s, _) in out_accs]
    out_shape = [jax.ShapeDtypeStruct((n_rows, w), dt) for (w, dt) in out_rows]
    out_shape += [jax.ShapeDtypeStruct(s, dt) for (s, dt) in out_accs]
    res = _call(
        body, name=name, grid=(n_rows // tm,), in_specs=in_specs, out_specs=out_specs, out_shape=out_shape,
        compiler_params=_params(vmem_mb),
    )(*[a for (a, _, _) in rows], *consts)
    return res[:n_or], res[n_or:]


def _stage_inproj(x, norm_g, w_in_all):
    S = x.shape[0]

    def fn(rv, cr, out):
        g_ref, w_ref = cr
        y, _, _ = _rms_fwd(rv[0], g_ref[...])
        ub = y.astype(BF16)
        for j in range(N_DEV):
            out[1][:, j * SHARD_IN:(j + 1) * SHARD_IN] = jnp.dot(ub, w_ref[j], preferred_element_type=F32)
        return [ub, None], []

    (u, proj), _ = _rowwise("inproj", fn, S, 256, [(x, D_MODEL, 0)], [norm_g, w_in_all],
                            [(D_MODEL, BF16), (IN_COLS, F32)], [], 56)
    return u, proj


def _stage_branches(o, proj, ys, x, hg_norm_g, w_o_hg, w_glu, b_glu, w_o_s5, w_out):
    S = x.shape[0]

    def fn(rv, cr, out):
        o_b, g_hg, z_s, gate_hg, gate_s5, ys_b, x_b = rv
        gn_ref, wohg_ref, wglu_ref, bglu_ref, wos5_ref, wout_ref = cr
        on, _, _ = _head_rms_fwd(o_b, gn_ref[...])
        a = on * _silu(g_hg)
        y_hg = jnp.dot(a.astype(BF16), wohg_ref[...], preferred_element_type=F32)
        gl = _gelu(ys_b)
        glu = jnp.dot(gl.astype(BF16), wglu_ref[...], preferred_element_type=F32) + bglu_ref[...]
        ys2 = glu[:, :S5_WIDTH] * _sigmoid(glu[:, S5_WIDTH:]) * _silu(z_s)
        y_s5 = jnp.dot(ys2.astype(BF16), wos5_ref[...], preferred_element_type=F32)
        merged = _sigmoid(gate_hg) * y_hg + _sigmoid(gate_s5) * y_s5
        h1 = x_b + jnp.dot(merged.astype(BF16), wout_ref[...], preferred_element_type=F32)
        return [y_hg, y_s5, glu, h1], []

    rows = [(o, D_MODEL, 0), (proj, D_MODEL, COL_G // D_MODEL), (proj, S5_WIDTH, COL_ZS // S5_WIDTH),
            (proj, D_MODEL, COL_GH // D_MODEL), (proj, D_MODEL, COL_GS // D_MODEL), (ys, S5_WIDTH, 0), (x, D_MODEL, 0)]
    (y_hg, y_s5, glu, h1), _ = _rowwise(
        "branches", fn, S, 256, rows, [hg_norm_g, w_o_hg, w_glu, b_glu, w_o_s5, w_out],
        [(D_MODEL, F32)] * 4, [], 56)
    return y_hg, y_s5, glu, h1


def _stage_ple_loss(h1, p, target, ple_norm_g, w_ple, w_ple_gate, final_norm_g):
    S = h1.shape[0]

    def fn(rv, cr, out):
        h1_b, p_b, t_b = rv
        gp_ref, wple_ref, wpg_ref, gf_ref = cr
        n2g, n2, r2 = _rms_fwd(h1_b, gp_ref[...])
        z = jnp.dot(n2g.astype(BF16), wpg_ref[...], preferred_element_type=F32)
        gate = _sigmoid(z)
        pe = jnp.dot(p_b.astype(BF16), wple_ref[...], preferred_element_type=F32)
        h2 = h1_b + pe * gate
        y, nf, rf = _rms_fwd(h2, gf_ref[...])
        err = y - t_b
        loss_rows = 0.5 * jnp.mean(err * err, axis=-1, keepdims=True)
        loss_inc = jnp.broadcast_to(jnp.sum(loss_rows, axis=0, keepdims=True), (1, 128))
        dy = err * (1.0 / D_MODEL)
        dh2, d_gf = _rms_bwd(dy, nf, rf, gf_ref[...])
        d_pe = dh2 * gate
        dz = dh2 * pe * gate * (1.0 - gate)
        d_wple = _bdot_tn(p_b, d_pe)
        d_wpg = _bdot_tn(n2g, dz)
        dn2g = _bdot_nt(dz, wpg_ref[...])
        dh1n, d_gp = _rms_bwd(dn2g, n2, r2, gp_ref[...])
        return [dh2 + dh1n], [loss_inc, d_gf, d_gp, d_wple, d_wpg]

    (dh1,), accs = _rowwise(
        "ple_loss", fn, S, 256, [(h1, D_MODEL, 0), (p, 256, 0), (target, D_MODEL, 0)],
        [ple_norm_g, w_ple, w_ple_gate, final_norm_g], [(D_MODEL, F32)],
        [((1, 128), F32), ((1, D_MODEL), F32), ((1, D_MODEL), F32), ((256, D_MODEL), F32), ((D_MODEL, D_MODEL), F32)], 56)
    return dh1, accs


def _stage_bwd_merge_hg(dh1, y_hg, y_s5, proj, o, hg_norm_g, w_out, w_o_hg):
    S = dh1.shape[0]

    def fn(rv, cr, out):
        dh1_b, yhg, ys5, gate_hg, gate_s5, o_b, g_hg = rv
        gn_ref, wout_ref, wohg_ref = cr
        sg_h, sg_s = _sigmoid(gate_hg), _sigmoid(gate_s5)
        merged = sg_h * yhg + sg_s * ys5
        d_wout = _bdot_tn(merged, dh1_b)
        d_merged = _bdot_nt(dh1_b, wout_ref[...])
        d_gate_hg = d_merged * yhg * sg_h * (1.0 - sg_h)
        d_gate_s5 = d_merged * ys5 * sg_s * (1.0 - sg_s)
        d_yhg = d_merged * sg_h
        d_ys5 = d_merged * sg_s
        ong, on, rs = _head_rms_fwd(o_b, gn_ref[...])
        sil = _silu(g_hg)
        a = ong * sil
        d_wohg = _bdot_tn(a, d_yhg)
        d_a = _bdot_nt(d_yhg, wohg_ref[...])
        d_g_hg = d_a * ong * _dsilu(g_hg)
        d_o, d_gn = _head_rms_bwd(d_a * sil, on, rs, gn_ref[...])
        return [d_o, d_g_hg, d_gate_hg, d_gate_s5, d_ys5], [d_wout, d_wohg, d_gn]

    rows = [(dh1, D_MODEL, 0), (y_hg, D_MODEL, 0), (y_s5, D_MODEL, 0), (proj, D_MODEL, COL_GH // D_MODEL),
            (proj, D_MODEL, COL_GS // D_MODEL), (o, D_MODEL, 0), (proj, D_MODEL, COL_G // D_MODEL)]
    outs, accs = _rowwise(
        "bwd_merge_hg", fn, S, 256, rows, [hg_norm_g, w_out, w_o_hg], [(D_MODEL, F32)] * 5,
        [((D_MODEL, D_MODEL), F32), ((D_MODEL, D_MODEL), F32), ((1, D_MODEL), F32)], 56)
    return outs, accs


def _stage_bwd_s5_path(d_ys5, ys, glu, proj, w_o_s5, w_glu):
    S = ys.shape[0]

    def fn(rv, cr, out):
        d_ys5_b, ys_b, glu_b, z_s = rv
        wos5_ref, wglu_ref = cr
        ga, gb = glu_b[:, :S5_WIDTH], glu_b[:, S5_WIDTH:]
        sgb, silz = _sigmoid(gb), _silu(z_s)
        ys2 = ga * sgb * silz
        d_wos5 = _bdot_tn(ys2, d_ys5_b)
        d_ys2 = _bdot_nt(d_ys5_b, wos5_ref[...])
        d_ga = d_ys2 * sgb * silz
        d_gb = d_ys2 * ga * sgb * (1.0 - sgb) * silz
        d_z = d_ys2 * ga * sgb * _dsilu(z_s)
        d_glu = jnp.concatenate([d_ga, d_gb], axis=1)
        gl = _gelu(ys_b)
        d_wglu = _bdot_tn(gl, d_glu)
        d_bglu = jnp.sum(d_glu, axis=0, keepdims=True)
        d_gl = _bdot_nt(d_glu, wglu_ref[...])
        return [d_gl * _dgelu(ys_b), d_z], [d_wos5, d_wglu, d_bglu]

    rows = [(d_ys5, D_MODEL, 0), (ys, S5_WIDTH, 0), (glu, D_MODEL, 0), (proj, S5_WIDTH, COL_ZS // S5_WIDTH)]
    outs, accs = _rowwise(
        "bwd_s5_path", fn, S, 256, rows, [w_o_s5, w_glu], [(S5_WIDTH, F32), (S5_WIDTH, F32)],
        [((S5_WIDTH, D_MODEL), F32), ((S5_WIDTH, D_MODEL), F32), ((1, D_MODEL), F32)], 48)
    return outs, accs


def _stage_inproj_bwd(pieces, x, dh1, norm_g, w_in_all):
    S = x.shape[0]

    def fn(rv, cr, out):
        g_ref, w_ref = cr
        x_b, dh1_b = rv[8], rv[9]
        dproj_ref = out[1]
        col = 0
        for v in rv[:8]:
            dproj_ref[:, col:col + v.shape[1]] = v.astype(BF16)
            col += v.shape[1]
        d_u = jnp.zeros((x_b.shape[0], D_MODEL), F32)
        for j in range(N_DEV):
            d_u = d_u + lax.dot_general(dproj_ref[:, j * SHARD_IN:(j + 1) * SHARD_IN], w_ref[j],
                                        (((1,), (1,)), ((), ())), preferred_element_type=F32)
        _, n, r = _rms_fwd(x_b, g_ref[...])
        dx, d_g = _rms_bwd(d_u, n, r, g_ref[...])
        return [dh1_b + dx, None], [d_g]

    rows = [(a, a.shape[1], 0) for a in pieces] + [(x, D_MODEL, 0), (dh1, D_MODEL, 0)]
    (grad_x, dproj), (d_g,) = _rowwise(
        "inproj_bwd", fn, S, 256, rows, [norm_g, w_in_all], [(D_MODEL, F32), (IN_COLS, BF16)],
        [((1, D_MODEL), F32)], 56)
    return grad_x, dproj, d_g


def _grad_w_in(dproj, u):
    S = u.shape[0]
    tm = 512
    n_i = S // tm

    def body(dp_ref, u_ref, out_ref, acc):
        i = pl.program_id(1)

        @pl.when(i == 0)
        def _():
            acc[...] = jnp.zeros(acc.shape, F32)

        acc[...] += lax.dot_general(dp_ref[...], u_ref[...], (((0,), (0,)), ((), ())), preferred_element_type=F32)

        @pl.when(i == n_i - 1)
        def _():
            out_ref[...] = acc[...].T.astype(BF16)

    return _call(
        body, name="grad_w_in", grid=(N_DEV, n_i),
        in_specs=[pl.BlockSpec((tm, SHARD_IN), lambda j, i: (i, j)), pl.BlockSpec((tm, D_MODEL), lambda j, i: (i, 0))],
        out_specs=pl.BlockSpec((None, D_MODEL, SHARD_IN), lambda j, i: (j, 0, 0)),
        out_shape=jax.ShapeDtypeStruct((N_DEV, D_MODEL, SHARD_IN), BF16),
        scratch_shapes=[pltpu.VMEM((SHARD_IN, D_MODEL), F32)],
        compiler_params=_params(32, 2),
    )(dproj, u)


def _chunk_row(shape):
    return lax.broadcasted_iota(jnp.int32, shape, 0) & (HG_CHUNK - 1)


def _chunk_cumsum(x):
    r_in = _chunk_row(x.shape)
    s = 1
    while s < HG_CHUNK:
        x = x + jnp.where(r_in >= s, pltpu.roll(x, s, 0), 0.0)
        s *= 2
    return x


def _chunk_suffix_sum(x):
    n = x.shape[0]
    r_in = _chunk_row(x.shape)
    s = 1
    while s < HG_CHUNK:
        x = x + jnp.where(r_in < HG_CHUNK - s, pltpu.roll(x, n - s, 0), 0.0)
        s *= 2
    return x


def _hgrn_prep(q, fl, lb):
    nc = HG_SUPER // HG_CHUNK
    sig = _sigmoid(fl)
    f = lb + (1.0 - lb) * sig
    k = (1.0 - lb) * (1.0 - sig)
    b = _chunk_cumsum(jnp.log(f))
    b3 = b.reshape(nc, HG_CHUNK, HG_DIM)
    row3 = lax.broadcasted_iota(jnp.int32, b3.shape, 1)
    pick = lambda r: jnp.sum(jnp.where(row3 == r, b3, 0.0), axis=1, keepdims=True)
    b_mid = pick(HG_CHUNK // 2 - 1)
    b_last = pick(HG_CHUNK - 1)
    flat = lambda t: t.reshape(HG_SUPER, HG_DIM)
    e_qa = flat(jnp.exp(b3 - b_mid))
    e_ka = flat(jnp.exp(b_mid - b3))
    e_qd = jnp.exp(b)
    e_kd = flat(jnp.exp(b_last - b3))
    dc = jnp.exp(b_last)
    return sig, f, k, e_qa, e_ka, e_qd, e_kd, dc


def _hgrn_mask():
    r = lax.broadcasted_iota(jnp.int32, (HG_SUPER, HG_SUPER), 0)
    c = lax.broadcasted_iota(jnp.int32, (HG_SUPER, HG_SUPER), 1)
    shift = HG_CHUNK.bit_length() - 1
    return (jnp.right_shift(r, shift) == jnp.right_shift(c, shift)) & (r >= c)


def _hgrn_fwd(proj, lb):
    S = proj.shape[0]
    nb = S // HG_SUPER
    nc = HG_SUPER // HG_CHUNK
    hp = HG_HEADS_PER_STEP
    wide = hp * HG_DIM

    def body(q_ref, f_ref, iv_ref, lb_ref, o_ref, st_ref, state):
        @pl.when(pl.program_id(1) == 0)
        def _():
            state[...] = jnp.zeros(state.shape, F32)

        mask = _hgrn_mask()
        for hh in range(hp):
            lanes = slice(hh * HG_DIM, (hh + 1) * HG_DIM)
            q, iv = q_ref[:, lanes], iv_ref[:, lanes]
            _, _, k, e_qa, e_ka, e_qd, e_kd, dc = _hgrn_prep(q, f_ref[:, lanes], lb_ref[:, lanes])
            scores = jnp.where(mask, _bdot_nt(q * e_qa, k * e_ka), 0.0)
            o_intra = _bdot(scores, iv)
            qd, kd = q * e_qd, k * e_kd
            for c in range(nc):
                sl = slice(c * HG_CHUNK, (c + 1) * HG_CHUNK)
                st = state[hh]
                st_ref[hh, c] = st
                o_ref[sl, lanes] = o_intra[sl] + _bdot_nt(qd[sl], st)
                state[hh] = dc[c] * st + _bdot_tn(iv[sl], kd[sl])

    blk = lambda base: pl.BlockSpec((HG_SUPER, wide), functools.partial(lambda h, i, b: (i, b + h), b=base // wide))
    return _call(
        body, name="hgrn_fwd", grid=(HG_HEADS // hp, nb),
        in_specs=[blk(COL_Q), blk(COL_F), blk(COL_I), pl.BlockSpec((1, wide), lambda h, i: (0, h))],
        out_specs=[pl.BlockSpec((HG_SUPER, wide), lambda h, i: (i, h)),
                   pl.BlockSpec((hp, nc, HG_DIM, HG_DIM), lambda h, i: (h, i, 0, 0))],
        out_shape=[jax.ShapeDtypeStruct((S, D_MODEL), F32),
                   jax.ShapeDtypeStruct((HG_HEADS, S // HG_CHUNK, HG_DIM, HG_DIM), F32)],
        scratch_shapes=[pltpu.VMEM((hp, HG_DIM, HG_DIM), F32)],
        compiler_params=_params(40, 2),
    )(proj, proj, proj, lb)


def _hgrn_bwd(proj, lb, d_o, states):
    S = proj.shape[0]
    nb = S // HG_SUPER
    nc = HG_SUPER // HG_CHUNK
    hp = HG_HEADS_PER_STEP
    wide = hp * HG_DIM

    def body(q_ref, f_ref, iv_ref, lb_ref, do_ref, st_ref, dq_ref, df_ref, div_ref, dlb_ref, dstate):
        @pl.when(pl.program_id(1) == 0)
        def _():
            dstate[...] = jnp.zeros(dstate.shape, F32)
            dlb_ref[...] = jnp.zeros(dlb_ref.shape, F32)

        mask = _hgrn_mask()
        for hh in range(hp):
            lanes = slice(hh * HG_DIM, (hh + 1) * HG_DIM)
            q, iv, do, lb_v = q_ref[:, lanes], iv_ref[:, lanes], do_ref[:, lanes], lb_ref[:, lanes]
            sig, f, k, e_qa, e_ka, e_qd, e_kd, dc = _hgrn_prep(q, f_ref[:, lanes], lb_v)
            qa, ka, qd, kd = q * e_qa, k * e_ka, q * e_qd, k * e_kd
            scores = jnp.where(mask, _bdot_nt(qa, ka), 0.0)
            d_scores = jnp.where(mask, _bdot_nt(do, iv), 0.0)
            d_iv_intra = _bdot_tn(scores, do)
            d_qa = _bdot(d_scores, ka)
            d_ka = _bdot_tn(d_scores, qa)
            d_qd, d_kd, d_last = [None] * nc, [None] * nc, [None] * nc
            for c in reversed(range(nc)):
                sl = slice(c * HG_CHUNK, (c + 1) * HG_CHUNK)
                st = st_ref[hh, c]
                ds = dstate[hh]
                d_qd[c] = _bdot(do[sl], st)
                d_kd[c] = _bdot(iv[sl], ds)
                div_ref[sl, lanes] = d_iv_intra[sl] + _bdot_nt(kd[sl], ds)
                d_last[c] = (jnp.sum(ds * st, axis=0, keepdims=True) * dc[c]
                             + jnp.sum(d_kd[c] * kd[sl], axis=0, keepdims=True))
                dstate[hh] = dc[c] * ds + _bdot_tn(do[sl], qd[sl])
            d_qd = jnp.concatenate(d_qd, axis=0)
            d_kd = jnp.concatenate(d_kd, axis=0)
            d_b = d_qa * qa - d_ka * ka + d_qd * qd - d_kd * kd
            last_rows = jnp.concatenate([jnp.broadcast_to(t, (HG_CHUNK, HG_DIM)) for t in d_last], axis=0)
            d_b = d_b + jnp.where(_chunk_row(d_b.shape) == HG_CHUNK - 1, last_rows, 0.0)
            d_logf = _chunk_suffix_sum(d_b)
            d_k = d_ka * e_ka + d_kd * e_kd
            g_f = d_logf / f
            d_sig = (g_f - d_k) * (1.0 - lb_v)
            dq_ref[:, lanes] = d_qa * e_qa + d_qd * e_qd
            df_ref[:, lanes] = d_sig * sig * (1.0 - sig)
            d_lb = jnp.sum((g_f - d_k) * (1.0 - sig), axis=0, keepdims=True)
            dlb_ref[:, lanes] += jnp.broadcast_to(d_lb, (8, HG_DIM))

    rev = lambda i: nb - 1 - i
    blk = lambda base: pl.BlockSpec((HG_SUPER, wide), functools.partial(lambda h, i, b: (rev(i), b + h), b=base // wide))
    row_out = pl.BlockSpec((HG_SUPER, wide), lambda h, i: (rev(i), h))
    dq, df, div, dlb = _call(
        body, name="hgrn_bwd", grid=(HG_HEADS // hp, nb),
        in_specs=[blk(COL_Q), blk(COL_F), blk(COL_I), pl.BlockSpec((1, wide), lambda h, i: (0, h)),
                  pl.BlockSpec((HG_SUPER, wide), lambda h, i: (rev(i), h)),
                  pl.BlockSpec((hp, nc, HG_DIM, HG_DIM), lambda h, i: (h, rev(i), 0, 0))],
        out_specs=[row_out, row_out, row_out, pl.BlockSpec((8, wide), lambda h, i: (0, h))],
        out_shape=[jax.ShapeDtypeStruct((S, D_MODEL), F32)] * 3 + [jax.ShapeDtypeStruct((8, D_MODEL), F32)],
        scratch_shapes=[pltpu.VMEM((hp, HG_DIM, HG_DIM), F32)],
        compiler_params=_params(40, 2),
    )(proj, proj, proj, lb, d_o, states)
    return dq, df, div, dlb[0:1]


def _s5_matrices(a_re, a_im, log_dt, b_re, b_im, c_re, c_im, d, seg_len):
    dt = jnp.exp(log_dt)[:, None]
    mag = jnp.exp(a_re * dt)
    lr, li = mag * jnp.cos(a_im * dt), mag * jnp.sin(a_im * dt)
    den = a_re * a_re + a_im * a_im
    nr = lr - 1.0
    sr = (nr * a_re + li * a_im) / den
    si = (li * a_re - nr * a_im) / den
    bbr = sr[..., None] * b_re - si[..., None] * b_im
    bbi = sr[..., None] * b_im + si[..., None] * b_re
    eye = jnp.eye(8, dtype=F32)

    def quad_cols(v):
        return v.reshape(S5_QUADS, 8 * S5_STATE)

    def lam_row(re_part, im_part):
        row = jnp.concatenate([quad_cols(re_part), quad_cols(im_part)], axis=1).reshape(1, S5_COLS)
        return jnp.broadcast_to(row, (S5_SEG, S5_COLS))

    def b_mat(bb):
        t = bb.reshape(S5_QUADS, 8, S5_STATE, S5_CH)
        return jnp.einsum("qgnc,gh->qgchn", t, eye).reshape(S5_QUADS, 8 * S5_CH, 8 * S5_STATE)

    def c_mat(cc):
        t = cc.reshape(S5_QUADS, 8, S5_CH, S5_STATE)
        return jnp.einsum("qgcn,gh->qgnhc", t, eye).reshape(S5_QUADS, 8 * S5_STATE, 8 * S5_CH)

    ang = a_im * dt * seg_len
    magp = jnp.exp(a_re * dt * seg_len)
    lpr, lpi = magp * jnp.cos(ang), magp * jnp.sin(ang)
    return dict(
        lam_r=lam_row(lr, lr), lam_i=lam_row(-li, li), lam_ic=lam_row(li, -li),
        b_q=jnp.concatenate([b_mat(bbr), b_mat(bbi)], axis=2),
        c_q=jnp.concatenate([c_mat(c_re), -c_mat(c_im)], axis=1),
        d_row=d.reshape(1, S5_WIDTH), pow_r=quad_cols(lpr), pow_i=quad_cols(lpi),
    )


def _swap_halves(h):
    half = S5_QCOLS // 2
    parts = []
    for q in range(S5_QUADS):
        parts.append(h[:, q * S5_QCOLS + half:(q + 1) * S5_QCOLS])
        parts.append(h[:, q * S5_QCOLS:q * S5_QCOLS + half])
    return jnp.concatenate(parts, axis=1)


def _s5_fwd_pass(u_perm, mats, h0, with_output):
    S = u_perm.shape[0]
    rows = S5_TILE_STEPS * S5_SEG
    nt = S // rows

    def body(*refs):
        if with_output:
            u_ref, b_ref, lr_ref, li_ref, h0_ref, c_ref, d_ref, y_ref, hinit_ref, hend_ref, xs, hcar = refs
        else:
            u_ref, b_ref, lr_ref, li_ref, h0_ref, hend_ref, xs, hcar = refs

        @pl.when(pl.program_id(0) == 0)
        def _():
            hcar[...] = h0_ref[...]

        if with_output:
            hinit_ref[...] = hcar[...]
        u = u_ref[...]
        ub = u.astype(BF16)
        for q in range(S5_QUADS):
            xs[:, q * S5_QCOLS:(q + 1) * S5_QCOLS] = jnp.dot(ub[:, q * 128:(q + 1) * 128], b_ref[q], preferred_element_type=F32)

        def step(t, h):
            sl = pl.ds(pl.multiple_of(t * S5_SEG, S5_SEG), S5_SEG)
            hn = lr_ref[...] * h + li_ref[...] * _swap_halves(h) + xs[sl, :]
            xs[sl, :] = hn
            return hn

        h = lax.fori_loop(0, S5_TILE_STEPS, step, hcar[...])
        hcar[...] = h
        hend_ref[...] = h
        if with_output:
            ys = [jnp.dot(xs[:, q * S5_QCOLS:(q + 1) * S5_QCOLS].astype(BF16), c_ref[q], preferred_element_type=F32)
                  for q in range(S5_QUADS)]
            y_ref[...] = jnp.concatenate(ys, axis=1) + d_ref[...] * u

    full = lambda a: pl.BlockSpec(a.shape, functools.partial(lambda i, nd: (0,) * nd, nd=a.ndim))
    ins = [u_perm, mats["b_q"], mats["lam_r"], mats["lam_i"], h0]
    in_specs = [pl.BlockSpec((rows, S5_WIDTH), lambda i: (i, 0))] + [full(a) for a in ins[1:]]
    out_specs = [pl.BlockSpec((S5_SEG, S5_COLS), lambda i: (0, 0))]
    out_shape = [jax.ShapeDtypeStruct((S5_SEG, S5_COLS), F32)]
    if with_output:
        ins += [mats["c_q"], mats["d_row"]]
        in_specs += [full(mats["c_q"]), full(mats["d_row"])]
        out_specs = [pl.BlockSpec((rows, S5_WIDTH), lambda i: (i, 0)),
                     pl.BlockSpec((None, S5_SEG, S5_COLS), lambda i: (i, 0, 0))] + out_specs
        out_shape = [jax.ShapeDtypeStruct((S, S5_WIDTH), F32), jax.ShapeDtypeStruct((nt, S5_SEG, S5_COLS), F32)] + out_shape
    return _call(
        body, name="s5_fwd_y" if with_output else "s5_fwd_ends", grid=(nt,), in_specs=in_specs, out_specs=out_specs,
        out_shape=out_shape,
        scratch_shapes=[pltpu.VMEM((rows, S5_COLS), F32), pltpu.VMEM((S5_SEG, S5_COLS), F32)],
        compiler_params=_params(40),
    )(*ins)


def _s5_bwd_ends(dy_perm, mats):
    S = dy_perm.shape[0]
    rows = S5_TILE_STEPS * S5_SEG
    nt = S // rows

    def body(dy_ref, c_ref, lr_ref, lic_ref, gend_ref, gs, gcar):
        @pl.when(pl.program_id(0) == 0)
        def _():
            gcar[...] = jnp.zeros(gcar.shape, F32)

        dyb = dy_ref[...].astype(BF16)
        for q in range(S5_QUADS):
            gs[:, q * S5_QCOLS:(q + 1) * S5_QCOLS] = lax.dot_general(
                dyb[:, q * 128:(q + 1) * 128], c_ref[q], (((1,), (1,)), ((), ())), preferred_element_type=F32)

        def step(k, g):
            t = S5_TILE_STEPS - 1 - k
            sl = pl.ds(pl.multiple_of(t * S5_SEG, S5_SEG), S5_SEG)
            return lr_ref[...] * g + lic_ref[...] * _swap_halves(g) + gs[sl, :]

        g = lax.fori_loop(0, S5_TILE_STEPS, step, gcar[...])
        gcar[...] = g
        gend_ref[...] = g

    full = lambda a: pl.BlockSpec(a.shape, functools.partial(lambda i, nd: (0,) * nd, nd=a.ndim))
    return _call(
        body, name="s5_bwd_ends", grid=(nt,),
        in_specs=[pl.BlockSpec((rows, S5_WIDTH), lambda i: (nt - 1 - i, 0)), full(mats["c_q"]), full(mats["lam_r"]),
                  full(mats["lam_ic"])],
        out_specs=pl.BlockSpec((S5_SEG, S5_COLS), lambda i: (0, 0)),
        out_shape=jax.ShapeDtypeStruct((S5_SEG, S5_COLS), F32),
        scratch_shapes=[pltpu.VMEM((rows, S5_COLS), F32), pltpu.VMEM((S5_SEG, S5_COLS), F32)],
        compiler_params=_params(40),
    )(dy_perm, mats["c_q"], mats["lam_r"], mats["lam_ic"])


def _s5_bwd_full(u_perm, dy_perm, hinit, g0, mats):
    S = u_perm.shape[0]
    rows = S5_TILE_STEPS * S5_SEG
    nt = S // rows

    def body(u_ref, dy_ref, hinit_ref, g0_ref, b_ref, c_ref, lr_ref, li_ref, lic_ref, d_ref,
             du_ref, dp_ref, dq_ref, db_ref, dc_ref, dd_ref, hs, gs, gcar):
        @pl.when(pl.program_id(0) == 0)
        def _():
            gcar[...] = g0_ref[...]
            for ref in (dp_ref, dq_ref, db_ref, dc_ref, dd_ref):
                ref[...] = jnp.zeros(ref.shape, F32)

        u, dy = u_ref[...], dy_ref[...]
        ub, dyb = u.astype(BF16), dy.astype(BF16)
        hs[0:S5_SEG, :] = hinit_ref[...]
        for q in range(S5_QUADS):
            cols = slice(q * S5_QCOLS, (q + 1) * S5_QCOLS)
            hs[S5_SEG:, cols] = jnp.dot(ub[:, q * 128:(q + 1) * 128], b_ref[q], preferred_element_type=F32)
            gs[:, cols] = lax.dot_general(dyb[:, q * 128:(q + 1) * 128], c_ref[q], (((1,), (1,)), ((), ())),
                                          preferred_element_type=F32)

        def fstep(t, h):
            sl = pl.ds(pl.multiple_of((t + 1) * S5_SEG, S5_SEG), S5_SEG)
            hn = lr_ref[...] * h + li_ref[...] * _swap_halves(h) + hs[sl, :]
            hs[sl, :] = hn
            return hn

        lax.fori_loop(0, S5_TILE_STEPS, fstep, hinit_ref[...])

        def bstep(k, g):
            t = S5_TILE_STEPS - 1 - k
            sl = pl.ds(pl.multiple_of(t * S5_SEG, S5_SEG), S5_SEG)
            gn = lr_ref[...] * g + lic_ref[...] * _swap_halves(g) + gs[sl, :]
            gs[sl, :] = gn
            return gn

        gcar[...] = lax.fori_loop(0, S5_TILE_STEPS, bstep, gcar[...])

        half = S5_QCOLS // 2
        dus = []
        for q in range(S5_QUADS):
            cols = slice(q * S5_QCOLS, (q + 1) * S5_QCOLS)

            def astep(t, carry, q=q):
                sl = pl.ds(pl.multiple_of(t * S5_SEG, S5_SEG), S5_SEG)
                g = gs[sl, q * S5_QCOLS:(q + 1) * S5_QCOLS]
                hp = hs[sl, q * S5_QCOLS:(q + 1) * S5_QCOLS]
                hp_sw = jnp.concatenate([hp[:, half:], hp[:, :half]], axis=1)
                return carry[0] + g * hp, carry[1] + g * hp_sw

            zero = jnp.zeros((S5_SEG, S5_QCOLS), F32)
            acc_p, acc_q = lax.fori_loop(0, S5_TILE_STEPS, astep, (zero, zero))
            dp_ref[:, cols] += jnp.sum(acc_p, axis=0, keepdims=True)
            dq_ref[:, cols] += jnp.sum(acc_q, axis=0, keepdims=True)
            gq = gs[:, cols].astype(BF16)
            db_ref[q] += lax.dot_general(ub[:, q * 128:(q + 1) * 128], gq, (((0,), (0,)), ((), ())),
                                         preferred_element_type=F32)
            hq = hs[S5_SEG:, cols].astype(BF16)
            dc_ref[q] += lax.dot_general(dyb[:, q * 128:(q + 1) * 128], hq, (((0,), (0,)), ((), ())),
                                         preferred_element_type=F32)
            dus.append(lax.dot_general(gq, b_ref[q], (((1,), (1,)), ((), ())), preferred_element_type=F32))
        du_ref[...] = jnp.concatenate(dus, axis=1) + d_ref[...] * dy
        dd_ref[...] += jnp.sum(dy * u, axis=0, keepdims=True)

    full = lambda a: pl.BlockSpec(a.shape, functools.partial(lambda i, nd: (0,) * nd, nd=a.ndim))
    rev_rows = pl.BlockSpec((rows, S5_WIDTH), lambda i: (nt - 1 - i, 0))
    consts = [mats["b_q"], mats["c_q"], mats["lam_r"], mats["lam_i"], mats["lam_ic"], mats["d_row"]]
    acc = lambda s: pl.BlockSpec(s, functools.partial(lambda i, nd: (0,) * nd, nd=len(s)))
    acc_shapes = [(1, S5_COLS), (1, S5_COLS), (S5_QUADS, 128, S5_QCOLS), (S5_QUADS, 128, S5_QCOLS), (1, S5_WIDTH)]
    return _call(
        body, name="s5_bwd_full", grid=(nt,),
        in_specs=[rev_rows, rev_rows, pl.BlockSpec((None, S5_SEG, S5_COLS), lambda i: (nt - 1 - i, 0, 0)), full(g0)]
        + [full(a) for a in consts],
        out_specs=[rev_rows] + [acc(s) for s in acc_shapes],
        out_shape=[jax.ShapeDtypeStruct((S, S5_WIDTH), F32)] + [jax.ShapeDtypeStruct(s, F32) for s in acc_shapes],
        scratch_shapes=[pltpu.VMEM((rows + S5_SEG, S5_COLS), F32), pltpu.VMEM((rows, S5_COLS), F32),
                        pltpu.VMEM((S5_SEG, S5_COLS), F32)],
        compiler_params=_params(56),
    )(u_perm, dy_perm, hinit, g0, *consts)


def _cmul(ar, ai, br, bi):
    return ar * br - ai * bi, ar * bi + ai * br


def _split_cols(v):
    t = v.reshape(v.shape[0], S5_QUADS, 2, S5_QCOLS // 2)
    return t[:, :, 0], t[:, :, 1]


def _join_cols(re, im):
    return jnp.stack([re, im], axis=2).reshape(re.shape[0], S5_COLS)


def _segment_starts(ends, pow_r, pow_i, reverse):
    er, ei = _split_cols(ends)
    pi = -pow_i if reverse else pow_i
    order = list(range(S5_SEG))
    if reverse:
        order = order[::-1]
    cr, ci = jnp.zeros_like(er[0]), jnp.zeros_like(ei[0])
    out_r, out_i = [None] * S5_SEG, [None] * S5_SEG
    for j in order:
        out_r[j], out_i[j] = cr, ci
        mr, mi = _cmul(pow_r, pi, cr, ci)
        cr, ci = mr + er[j], mi + ei[j]
    return _join_cols(jnp.stack(out_r), jnp.stack(out_i))


def _to_segments(a):
    S, w = a.shape
    return a.reshape(S5_SEG, S // S5_SEG, w).transpose(1, 0, 2).reshape(S, w)


def _from_segments(a):
    S, w = a.shape
    return a.reshape(S // S5_SEG, S5_SEG, w).transpose(1, 0, 2).reshape(S, w)


def _my_pos():
    return lax.axis_index("x"), lax.axis_index("y"), lax.axis_index("c")


def _flip(pos, k):
    x, y, c = pos
    return (1 - x if k & 4 else x, 1 - y if k & 2 else y, 1 - c if k & 1 else c)


def _index_of(pos):
    return 4 * pos[0] + 2 * pos[1] + pos[2]


def _all_gather(arrays):
    n = len(arrays)

    def body(*refs):
        in_refs, out_refs = refs[:n], refs[n:2 * n]
        send_sems, recv_sems, local_sems = refs[2 * n:]
        me = _my_pos()
        mine = _index_of(me)
        local = [pltpu.make_async_copy(in_refs[a], out_refs[a].at[mine], local_sems.at[a]) for a in range(n)]
        for cp in local:
            cp.start()

        def copy(a, k, src_idx, to):
            return pltpu.make_async_remote_copy(
                src_ref=out_refs[a].at[src_idx], dst_ref=out_refs[a].at[src_idx],
                send_sem=send_sems.at[a * 7 + k - 1], recv_sem=recv_sems.at[a * 7 + k - 1],
                device_id=to, device_id_type=MESH)

        def first(a, k):
            return pltpu.make_async_remote_copy(
                src_ref=in_refs[a], dst_ref=out_refs[a].at[mine],
                send_sem=send_sems.at[a * 7 + k - 1], recv_sem=recv_sems.at[a * 7 + k - 1],
                device_id=_flip(me, k), device_id_type=MESH)

        sent = []
        for a in range(n):
            for k in range(1, 8):
                cp = first(a, k)
                cp.start()
                sent.append(cp)
        for a in range(n):
            for k in range(1, 8):
                copy(a, k, _index_of(_flip(me, k)), me).wait_recv()
        for cp in sent:
            cp.wait_send()
        for cp in local:
            cp.wait()

    any_spec = pl.BlockSpec(memory_space=pl.ANY)
    return _call(
        body, name="all_gather_weights",
        in_specs=[any_spec] * n, out_specs=[any_spec] * n,
        out_shape=[jax.ShapeDtypeStruct((N_DEV,) + a.shape, a.dtype) for a in arrays],
        scratch_shapes=[pltpu.SemaphoreType.DMA((7 * n,)), pltpu.SemaphoreType.DMA((7 * n,)),
                        pltpu.SemaphoreType.DMA((n,))],
    )(*arrays)


def _reduce_scatter(arrays):
    n = len(arrays)
    chunk = 16

    def body(*refs):
        in_refs, out_refs = refs[:n], refs[n:2 * n]
        own = refs[2 * n:3 * n]
        land = refs[3 * n:4 * n]
        send_sems, recv_sems, local_sems = refs[4 * n:]
        me = _my_pos()
        mine = _index_of(me)
        local = [pltpu.make_async_copy(in_refs[a].at[mine], own[a], local_sems.at[a]) for a in range(n)]
        for cp in local:
            cp.start()

        def copy(a, k, to):
            return pltpu.make_async_remote_copy(
                src_ref=in_refs[a].at[_index_of(to)], dst_ref=land[a].at[k - 1],
                send_sem=send_sems.at[a * 7 + k - 1], recv_sem=recv_sems.at[a * 7 + k - 1],
                device_id=to, device_id_type=MESH)

        sent = []
        for a in range(n):
            for k in range(1, 8):
                cp = copy(a, k, _flip(me, k))
                cp.start()
                sent.append(cp)
        for a in range(n):
            local[a].wait()
            for k in range(1, 8):
                copy(a, k, me).wait_recv()
            rows = own[a].shape[0]

            def add_rows(i, carry, a=a):
                sl = pl.ds(pl.multiple_of(i * chunk, chunk), chunk)
                acc = own[a][sl, :].astype(F32)
                for k in range(7):
                    acc = acc + land[a][k, sl, :].astype(F32)
                out_refs[a][sl, :] = acc
                return carry

            lax.fori_loop(0, rows // chunk, add_rows, 0)
        for cp in sent:
            cp.wait_send()

    any_spec = pl.BlockSpec(memory_space=pl.ANY)
    vmem = pl.BlockSpec(memory_space=pltpu.VMEM)
    scratch = [pltpu.VMEM(a.shape[1:], a.dtype) for a in arrays]
    scratch += [pltpu.VMEM((7,) + a.shape[1:], a.dtype) for a in arrays]
    scratch += [pltpu.SemaphoreType.DMA((7 * n,)), pltpu.SemaphoreType.DMA((7 * n,)), pltpu.SemaphoreType.DMA((n,))]
    return _call(
        body, name="reduce_scatter_grads",
        in_specs=[any_spec] * n, out_specs=[vmem] * n,
        out_shape=[jax.ShapeDtypeStruct(a.shape[1:], F32) for a in arrays],
        scratch_shapes=scratch,
        compiler_params=pltpu.CompilerParams(vmem_limit_bytes=56 * MIB),
    )(*arrays)


def _adam_update(g, w, m, v):
    m2 = ADAM_B1 * m + (1.0 - ADAM_B1) * g
    v2 = ADAM_B2 * v + (1.0 - ADAM_B2) * (g * g)
    m_hat = m2 / (1.0 - ADAM_B1 ** ADAM_STEP)
    v_hat = v2 / (1.0 - ADAM_B2 ** ADAM_STEP)
    delta = -ADAM_LR * (m_hat / (jnp.sqrt(v_hat) + ADAM_EPS) + ADAM_WD * w)
    return delta, m2, v2


def _small_allreduce_adam(partial, w, m, v):
    rows = partial.shape[0]

    def body(p_ref, w_ref, m_ref, v_ref, g_ref, d_ref, m2_ref, v2_ref, gath, send_sems, recv_sems):
        me = _my_pos()
        mine = _index_of(me)
        gath[mine] = p_ref[...]
        sent = []
        for k in range(1, 8):
            cp = pltpu.make_async_remote_copy(
                src_ref=p_ref, dst_ref=gath.at[mine], send_sem=send_sems.at[k - 1], recv_sem=recv_sems.at[k - 1],
                device_id=_flip(me, k), device_id_type=MESH)
            cp.start()
            sent.append(cp)
        for k in range(1, 8):
            pltpu.make_async_remote_copy(
                src_ref=p_ref, dst_ref=gath.at[_index_of(_flip(me, k))], send_sem=send_sems.at[k - 1],
                recv_sem=recv_sems.at[k - 1], device_id=me, device_id_type=MESH).wait_recv()
        g = gath[0]
        for dev in range(1, N_DEV):
            g = g + gath[dev]
        delta, m2, v2 = _adam_update(g, w_ref[...], m_ref[...], v_ref[...])
        g_ref[...] = g
        d_ref[...] = delta
        m2_ref[...] = m2
        v2_ref[...] = v2
        for cp in sent:
            cp.wait_send()

    vmem = pl.BlockSpec(memory_space=pltpu.VMEM)
    return _call(
        body, name="small_allreduce_adam", in_specs=[vmem] * 4, out_specs=[vmem] * 4,
        out_shape=[jax.ShapeDtypeStruct((rows, 128), F32)] * 4,
        scratch_shapes=[pltpu.VMEM((N_DEV, rows, 128), F32), pltpu.SemaphoreType.DMA((7,)), pltpu.SemaphoreType.DMA((7,))],
        compiler_params=pltpu.CompilerParams(vmem_limit_bytes=32 * MIB),
    )(partial, w, m, v)


def _adam_rows(g, w, m, v):
    rows, cols = w.shape
    tm = rows if rows <= 256 else 256

    def fn(rv, cr, out):
        return list(_adam_update(*rv)), []

    outs, _ = _rowwise("adamw", fn, rows, tm, [(a, cols, 0) for a in (g, w, m, v)], [], [(cols, F32)] * 3, [], 32)
    return outs


_SMALL = ["norm_g", "hg_lb", "hg_norm_g", "s5_a_re", "s5_a_im", "s5_log_dt", "s5_b_re", "s5_b_im", "s5_c_re",
          "s5_c_im", "s5_d", "b_glu", "ple_norm_g", "final_norm_g"]
_BIG = ["w_in", "w_o_hg", "w_glu", "w_o_s5", "w_out", "w_ple", "w_ple_gate"]
_ORDER = ["norm_g", "w_in", "hg_lb", "hg_norm_g", "w_o_hg", "s5_a_re", "s5_a_im", "s5_log_dt", "s5_b_re", "s5_b_im",
          "s5_c_re", "s5_c_im", "s5_d", "w_glu", "b_glu", "w_o_s5", "w_out", "ple_norm_g", "w_ple", "w_ple_gate",
          "final_norm_g"]


def _pack_small(vals):
    parts = []
    for name in _SMALL:
        flat = vals[name].reshape(-1).astype(F32)
        pad = (-flat.shape[0]) % 1024
        parts.append(jnp.pad(flat, (0, pad)))
    return jnp.concatenate(parts).reshape(-1, 128)


def _unpack_small(packed, like):
    flat = packed.reshape(-1)
    out, off = {}, 0
    for name in _SMALL:
        size = like[name].size
        out[name] = flat[off:off + size].reshape(like[name].shape)
        off += size + (-size) % 1024
    return out


def _col_blocks(full):
    k = full.shape[0]
    return full.reshape(k, N_DEV, 128).transpose(1, 0, 2)


def _from_col_blocks(blocks):
    k = blocks.shape[1]
    return blocks.transpose(1, 0, 2).reshape(k, N_DEV * 128)


def kernel(x, p, norm_g, w_in, hg_lb, hg_norm_g, w_o_hg, s5_a_re, s5_a_im, s5_log_dt, s5_b_re, s5_b_im, s5_c_re, s5_c_im, s5_d, w_glu, b_glu, w_o_s5, w_out, ple_norm_g, w_ple, w_ple_gate, final_norm_g, loss_target, m_norm_g, m_w_in, m_hg_lb, m_hg_norm_g, m_w_o_hg, m_s5_a_re, m_s5_a_im, m_s5_log_dt, m_s5_b_re, m_s5_b_im, m_s5_c_re, m_s5_c_im, m_s5_d, m_w_glu, m_b_glu, m_w_o_s5, m_w_out, m_ple_norm_g, m_w_ple, m_w_ple_gate, m_final_norm_g, v_norm_g, v_w_in, v_hg_lb, v_hg_norm_g, v_w_o_hg, v_s5_a_re, v_s5_a_im, v_s5_log_dt, v_s5_b_re, v_s5_b_im, v_s5_c_re, v_s5_c_im, v_s5_d, v_w_glu, v_b_glu, v_w_o_s5, v_w_out, v_ple_norm_g, v_w_ple, v_w_ple_gate, v_final_norm_g):
    args = dict(locals())
    w = {n: args[n] for n in _ORDER}
    m = {n: args["m_" + n] for n in _ORDER}
    v = {n: args["v_" + n] for n in _ORDER}
    xs = x[0]
    ps = p[0, 0]
    tgt = loss_target[0]
    S = xs.shape[0]

    pack_a = jnp.concatenate([w_o_hg[0], w_out[0], w_ple_gate[0]], axis=0).astype(BF16)
    pack_b = w_in[0].astype(BF16)
    pack_c = jnp.concatenate([w_glu[0], w_o_s5[0], w_ple[0]], axis=0).astype(BF16)
    all_a, all_b, all_c = _all_gather([pack_a, pack_b, pack_c])
    wf_o_hg = all_a[:, 0:128].reshape(D_MODEL, D_MODEL)
    wf_out = all_a[:, 128:256].reshape(D_MODEL, D_MODEL)
    wf_pg = all_a[:, 256:384].reshape(D_MODEL, D_MODEL)
    wf_glu = _from_col_blocks(all_c[:, 0:512])
    wf_o_s5 = _from_col_blocks(all_c[:, 512:1024])
    wf_ple = _from_col_blocks(all_c[:, 1024:1280])

    lb = jax.nn.sigmoid(hg_lb[0:1] - hg_lb[1:2])
    s5_names = ["s5_a_re", "s5_a_im", "s5_log_dt", "s5_b_re", "s5_b_im", "s5_c_re", "s5_c_im", "s5_d"]
    build = lambda *a: _s5_matrices(*a, seg_len=S // S5_SEG)
    mats_f32, mats_vjp = jax.vjp(build, *[w[n][0] for n in s5_names])
    mats = dict(mats_f32, b_q=mats_f32["b_q"].astype(BF16), c_q=mats_f32["c_q"].astype(BF16))
    bias_glu = b_glu

    u, proj = _stage_inproj(xs, norm_g, all_b)
    o, states = _hgrn_fwd(proj, lb)
    u_perm = _to_segments(proj[:, COL_US:COL_US + S5_WIDTH])
    zeros_state = jnp.zeros((S5_SEG, S5_COLS), F32)
    (h_ends,) = _s5_fwd_pass(u_perm, mats, zeros_state, False)
    h0 = _segment_starts(h_ends, mats["pow_r"], mats["pow_i"], False)
    y_perm, h_init, _ = _s5_fwd_pass(u_perm, mats, h0, True)
    ys = _from_segments(y_perm)
    y_hg, y_s5, glu, h1 = _stage_branches(o, proj, ys, xs, hg_norm_g, wf_o_hg, wf_glu, bias_glu, wf_o_s5, wf_out)

    dh1, (loss_acc, d_final_g, d_ple_g, d_w_ple, d_w_pg) = _stage_ple_loss(
        h1, ps, tgt, ple_norm_g, wf_ple, wf_pg, final_norm_g.reshape(1, D_MODEL))
    (d_o, d_g_hg, d_gate_hg, d_gate_s5, d_ys5), (d_w_out, d_w_o_hg, d_hg_norm) = _stage_bwd_merge_hg(
        dh1, y_hg, y_s5, proj, o, hg_norm_g, wf_out, wf_o_hg)
    (d_ys, d_z), (d_w_o_s5, d_w_glu, d_b_glu) = _stage_bwd_s5_path(d_ys5, ys, glu, proj, wf_o_s5, wf_glu)
    dq, df, div, d_lb = _hgrn_bwd(proj, lb, d_o, states)
    dy_perm = _to_segments(d_ys)
    g_ends = _s5_bwd_ends(dy_perm, mats)
    g0 = _segment_starts(g_ends, mats["pow_r"], mats["pow_i"], True)
    du_perm, acc_p, acc_q, d_bq, d_cq_t, d_d = _s5_bwd_full(u_perm, dy_perm, h_init, g0, mats)
    d_us = _from_segments(du_perm)
    grad_x, dproj, d_norm_g = _stage_inproj_bwd([dq, df, div, d_g_hg, d_us, d_z, d_gate_hg, d_gate_s5], xs, dh1,
                                                norm_g, all_b)
    d_w_in_blocks = _grad_w_in(dproj, u)

    p_re, p_im = _split_cols(acc_p)
    q_re, q_im = _split_cols(acc_q)
    d_lam_r = (p_re + p_im)[0]
    d_lam_i = (q_im - q_re)[0]
    zero_row = jnp.zeros((S5_SEG, S5_COLS), F32)
    row_of = lambda re_part, im_part: zero_row.at[0].set(_join_cols(re_part[None], im_part[None])[0])
    zeros_q = jnp.zeros_like(d_lam_r)
    cot = dict(
        lam_r=row_of(d_lam_r, zeros_q), lam_i=row_of(zeros_q, d_lam_i), lam_ic=zero_row,
        b_q=d_bq, c_q=d_cq_t.transpose(0, 2, 1), d_row=d_d,
        pow_r=jnp.zeros_like(mats["pow_r"]), pow_i=jnp.zeros_like(mats["pow_i"]),
    )
    d_s5 = mats_vjp(cot)

    s_lb = lb * (1.0 - lb)
    d_hg_lb = jnp.concatenate([d_lb * s_lb, -d_lb * s_lb], axis=0)
    small_g = dict(norm_g=d_norm_g, hg_lb=d_hg_lb, hg_norm_g=d_hg_norm, b_glu=d_b_glu, ple_norm_g=d_ple_g,
                   final_norm_g=d_final_g)
    for name, g in zip(s5_names, d_s5):
        small_g[name] = g
    pk = lambda d: _pack_small({n: d[n] for n in _SMALL})
    sg, sd, sm, sv = _small_allreduce_adam(pk(small_g), pk(w), pk(m), pk(v))
    like = {n: w[n] for n in _SMALL}
    out_g, out_d, out_m, out_v = (_unpack_small(t, like) for t in (sg, sd, sm, sv))

    rs_a = jnp.concatenate([d_w_o_hg.reshape(N_DEV, 128, D_MODEL), d_w_out.reshape(N_DEV, 128, D_MODEL),
                            d_w_pg.reshape(N_DEV, 128, D_MODEL)], axis=1).astype(BF16)
    rs_c = jnp.concatenate([_col_blocks(d_w_glu), _col_blocks(d_w_o_s5), _col_blocks(d_w_ple)], axis=1).astype(BF16)
    g_a, g_b, g_c = _reduce_scatter([rs_a, d_w_in_blocks, rs_c])
    big_g = dict(w_o_hg=g_a[0:128], w_out=g_a[128:256], w_ple_gate=g_a[256:384], w_in=g_b,
                 w_glu=g_c[0:512], w_o_s5=g_c[512:1024], w_ple=g_c[1024:1280])
    for name in _BIG:
        shape = w[name].shape
        g2 = big_g[name]
        d2, m2, v2 = _adam_rows(g2, w[name][0], m[name][0], v[name][0])
        out_g[name], out_d[name], out_m[name], out_v[name] = (t.reshape(shape) for t in (g2, d2, m2, v2))

    loss = lax.psum(loss_acc[0, 0], ("x", "y", "c"))
    return (loss, grad_x[None], *[out_g[n] for n in _ORDER], *[out_d[n] for n in _ORDER],
            *[out_m[n] for n in _ORDER], *[out_v[n] for n in _ORDER])
```

```python
import functools
import math

import jax
import jax.numpy as jnp
from jax import lax
from jax.experimental import pallas as pl
from jax.experimental.pallas import tpu as pltpu

F32 = jnp.float32
BF16 = jnp.bfloat16

D_MODEL = 1024
N_DEV = 8
IN_COLS = 7168
SHARD_IN = IN_COLS // N_DEV
HG_HEADS = 8
HG_DIM = 128
HG_CHUNK = 64
HG_SUPER = 256
HG_HEADS_PER_STEP = 4
S5_WIDTH = 512
S5_GROUPS = 32
S5_STATE = 64
S5_CH = 16
S5_SEG = 8
S5_QUADS = 4
S5_QCOLS = 1024
S5_COLS = S5_QUADS * S5_QCOLS
S5_TILE_STEPS = 64
NORM_EPS = 1e-6
ADAM_LR = 0.001
ADAM_B1 = 0.9
ADAM_B2 = 0.999
ADAM_EPS = 1e-08
ADAM_WD = 0.01
ADAM_STEP = 10
MIB = 1024 * 1024
MESH = pl.DeviceIdType.MESH

COL_Q, COL_F, COL_I, COL_G, COL_US, COL_ZS, COL_GH, COL_GS = 0, 1024, 2048, 3072, 4096, 4608, 5120, 6144


def _call(body, **kw):
    return pl.pallas_call(body, **kw)


def _params(vmem_mb, n_grid=1):
    return pltpu.CompilerParams(
        dimension_semantics=("arbitrary",) * n_grid, vmem_limit_bytes=vmem_mb * MIB
    )


def _bdot(a, b):
    return jnp.dot(a.astype(BF16), b.astype(BF16), preferred_element_type=F32)


def _bdot_nt(a, b):
    return lax.dot_general(a.astype(BF16), b.astype(BF16), (((1,), (1,)), ((), ())), preferred_element_type=F32)


def _bdot_tn(a, b):
    return lax.dot_general(a.astype(BF16), b.astype(BF16), (((0,), (0,)), ((), ())), preferred_element_type=F32)


def _sigmoid(x):
    return jax.nn.sigmoid(x)


def _silu(x):
    return x * _sigmoid(x)


def _dsilu(x):
    s = _sigmoid(x)
    return s * (1.0 + x * (1.0 - s))


_GELU_C = math.sqrt(2.0 / math.pi)


def _gelu(x):
    return 0.5 * x * (1.0 + jnp.tanh(_GELU_C * (x + 0.044715 * x * x * x)))


def _dgelu(x):
    t = jnp.tanh(_GELU_C * (x + 0.044715 * x * x * x))
    return 0.5 * (1.0 + t) + 0.5 * x * (1.0 - t * t) * _GELU_C * (1.0 + 3.0 * 0.044715 * x * x)


def _rms_fwd(x, g):
    r = lax.rsqrt(jnp.mean(x * x, axis=-1, keepdims=True) + NORM_EPS)
    n = x * r
    return n * g, n, r


def _rms_bwd(dy, n, r, g):
    dn = dy * g
    dx = r * (dn - n * jnp.mean(dn * n, axis=-1, keepdims=True))
    return dx, jnp.sum(dy * n, axis=0, keepdims=True)


def _head_rms_fwd(o, g):
    ns, rs = [], []
    for h in range(HG_HEADS):
        oh = o[:, h * HG_DIM:(h + 1) * HG_DIM]
        r = lax.rsqrt(jnp.mean(oh * oh, axis=-1, keepdims=True) + NORM_EPS)
        ns.append(oh * r)
        rs.append(r)
    n = jnp.concatenate(ns, axis=1)
    return n * g, n, rs


def _head_rms_bwd(dy, n, rs, g):
    dn = dy * g
    dxs = []
    for h in range(HG_HEADS):
        sl = slice(h * HG_DIM, (h + 1) * HG_DIM)
        dxs.append(rs[h] * (dn[:, sl] - n[:, sl] * jnp.mean(dn[:, sl] * n[:, sl], axis=-1, keepdims=True)))
    return jnp.concatenate(dxs, axis=1), jnp.sum(dy * n, axis=0, keepdims=True)


def _rowwise(name, fn, n_rows, tm, rows, consts, out_rows, out_accs, vmem_mb):
    n_r, n_c, n_or, n_oa = len(rows), len(consts), len(out_rows), len(out_accs)

    def body(*refs):
        r_refs = refs[:n_r]
        c_refs = refs[n_r:n_r + n_c]
        or_refs = refs[n_r + n_c:n_r + n_c + n_or]
        oa_refs = refs[n_r + n_c + n_or:]
        outs, accs = fn([r[...] for r in r_refs], c_refs, or_refs)

        if n_oa:
            @pl.when(pl.program_id(0) == 0)
            def _():
                for ref in oa_refs:
                    ref[...] = jnp.zeros(ref.shape, ref.dtype)

        for ref, v in zip(or_refs, outs):
            if v is not None:
                ref[...] = v.astype(ref.dtype)
        for ref, v in zip(oa_refs, accs):
            ref[...] += v.astype(ref.dtype)

    in_specs = [pl.BlockSpec((tm, w), functools.partial(lambda i, c: (i, c), c=cb)) for (_, w, cb) in rows]
    in_specs += [pl.BlockSpec(c.shape, functools.partial(lambda i, nd: (0,) * nd, nd=c.ndim),
                              pipeline_mode=pl.Buffered(1)) for c in consts]
    out_specs = [pl.BlockSpec((tm, w), lambda i: (i, 0)) for (w, _) in out_rows]
    out_specs += [pl.BlockSpec(s, functools.partial(lambda i, nd: (0,) * nd, nd=len(s))) for (s, _) in out_accs]
    out_shape = [jax.ShapeDtypeStruct((n_rows, w), dt) for (w, dt) in out_rows]
    out_shape += [jax.ShapeDtypeStruct(s, dt) for (s, dt) in out_accs]
    res = _call(
        body, name=name, grid=(n_rows // tm,), in_specs=in_specs, out_specs=out_specs, out_shape=out_shape,
        compiler_params=_params(vmem_mb),
    )(*[a for (a, _, _) in rows], *consts)
    return res[:n_or], res[n_or:]


def _stage_branches(o, proj, ys, x, hg_norm_g, w_o_hg, w_glu, b_glu, w_o_s5, w_out):
    S = x.shape[0]

    def fn(rv, cr, out):
        o_b, g_hg, z_s, gate_hg, gate_s5, ys_b, x_b = rv
        gn_ref, wohg_ref, wglu_ref, bglu_ref, wos5_ref, wout_ref = cr
        on, _, _ = _head_rms_fwd(o_b, gn_ref[...])
        a = on * _silu(g_hg)
        y_hg = jnp.dot(a.astype(BF16), wohg_ref[...], preferred_element_type=F32)
        gl = _gelu(ys_b)
        glu = jnp.dot(gl.astype(BF16), wglu_ref[...], preferred_element_type=F32) + bglu_ref[...]
        ys2 = glu[:, :S5_WIDTH] * _sigmoid(glu[:, S5_WIDTH:]) * _silu(z_s)
        y_s5 = jnp.dot(ys2.astype(BF16), wos5_ref[...], preferred_element_type=F32)
        merged = _sigmoid(gate_hg) * y_hg + _sigmoid(gate_s5) * y_s5
        h1 = x_b + jnp.dot(merged.astype(BF16), wout_ref[...], preferred_element_type=F32)
        return [y_hg, y_s5, glu, h1], []

    rows = [(o, D_MODEL, 0), (proj, D_MODEL, COL_G // D_MODEL), (proj, S5_WIDTH, COL_ZS // S5_WIDTH),
            (proj, D_MODEL, COL_GH // D_MODEL), (proj, D_MODEL, COL_GS // D_MODEL), (ys, S5_WIDTH, 0), (x, D_MODEL, 0)]
    (y_hg, y_s5, glu, h1), _ = _rowwise(
        "branches", fn, S, 256, rows, [hg_norm_g, w_o_hg, w_glu, b_glu, w_o_s5, w_out],
        [(D_MODEL, F32)] * 4, [], 56)
    return y_hg, y_s5, glu, h1


def _stage_ple_loss(h1, p, target, ple_norm_g, w_ple, w_ple_gate, final_norm_g):
    S = h1.shape[0]

    def fn(rv, cr, out):
        h1_b, p_b, t_b = rv
        gp_ref, wple_ref, wpg_ref, gf_ref = cr
        n2g, n2, r2 = _rms_fwd(h1_b, gp_ref[...])
        z = jnp.dot(n2g.astype(BF16), wpg_ref[...], preferred_element_type=F32)
        gate = _sigmoid(z)
        pe = jnp.dot(p_b.astype(BF16), wple_ref[...], preferred_element_type=F32)
        h2 = h1_b + pe * gate
        y, nf, rf = _rms_fwd(h2, gf_ref[...])
        err = y - t_b
        loss_rows = 0.5 * jnp.mean(err * err, axis=-1, keepdims=True)
        loss_inc = jnp.broadcast_to(jnp.sum(loss_rows, axis=0, keepdims=True), (1, 128))
        dy = err * (1.0 / D_MODEL)
        dh2, d_gf = _rms_bwd(dy, nf, rf, gf_ref[...])
        d_pe = dh2 * gate
        dz = dh2 * pe * gate * (1.0 - gate)
        d_wple = _bdot_tn(p_b, d_pe)
        d_wpg = _bdot_tn(n2g, dz)
        dn2g = _bdot_nt(dz, wpg_ref[...])
        dh1n, d_gp = _rms_bwd(dn2g, n2, r2, gp_ref[...])
        return [dh2 + dh1n], [loss_inc, d_gf, d_gp, d_wple, d_wpg]

    (dh1,), accs = _rowwise(
        "ple_loss", fn, S, 256, [(h1, D_MODEL, 0), (p, 256, 0), (target, D_MODEL, 0)],
        [ple_norm_g, w_ple, w_ple_gate, final_norm_g], [(D_MODEL, F32)],
        [((1, 128), F32), ((1, D_MODEL), F32), ((1, D_MODEL), F32), ((256, D_MODEL), F32), ((D_MODEL, D_MODEL), F32)], 56)
    return dh1, accs


def _stage_bwd_merge_hg(dh1, y_hg, y_s5, proj, o, hg_norm_g, w_out, w_o_hg):
    S = dh1.shape[0]

    def fn(rv, cr, out):
        dh1_b, yhg, ys5, gate_hg, gate_s5, o_b, g_hg = rv
        gn_ref, wout_ref, wohg_ref = cr
        sg_h, sg_s = _sigmoid(gate_hg), _sigmoid(gate_s5)
        merged = sg_h * yhg + sg_s * ys5
        d_wout = _bdot_tn(merged, dh1_b)
        d_merged = _bdot_nt(dh1_b, wout_ref[...])
        d_gate_hg = d_merged * yhg * sg_h * (1.0 - sg_h)
        d_gate_s5 = d_merged * ys5 * sg_s * (1.0 - sg_s)
        d_yhg = d_merged * sg_h
        d_ys5 = d_merged * sg_s
        ong, on, rs = _head_rms_fwd(o_b, gn_ref[...])
        sil = _silu(g_hg)
        a = ong * sil
        d_wohg = _bdot_tn(a, d_yhg)
        d_a = _bdot_nt(d_yhg, wohg_ref[...])
        d_g_hg = d_a * ong * _dsilu(g_hg)
        d_o, d_gn = _head_rms_bwd(d_a * sil, on, rs, gn_ref[...])
        return [d_o, d_g_hg, d_gate_hg, d_gate_s5, d_ys5], [d_wout, d_wohg, d_gn]

    rows = [(dh1, D_MODEL, 0), (y_hg, D_MODEL, 0), (y_s5, D_MODEL, 0), (proj, D_MODEL, COL_GH // D_MODEL),
            (proj, D_MODEL, COL_GS // D_MODEL), (o, D_MODEL, 0), (proj, D_MODEL, COL_G // D_MODEL)]
    outs, accs = _rowwise(
        "bwd_merge_hg", fn, S, 256, rows, [hg_norm_g, w_out, w_o_hg], [(D_MODEL, F32)] * 5,
        [((D_MODEL, D_MODEL), F32), ((D_MODEL, D_MODEL), F32), ((1, D_MODEL), F32)], 56)
    return outs, accs


def _stage_bwd_s5_path(d_ys5, ys, glu, proj, w_o_s5, w_glu):
    S = ys.shape[0]

    def fn(rv, cr, out):
        d_ys5_b, ys_b, glu_b, z_s = rv
        wos5_ref, wglu_ref = cr
        ga, gb = glu_b[:, :S5_WIDTH], glu_b[:, S5_WIDTH:]
        sgb, silz = _sigmoid(gb), _silu(z_s)
        ys2 = ga * sgb * silz
        d_wos5 = _bdot_tn(ys2, d_ys5_b)
        d_ys2 = _bdot_nt(d_ys5_b, wos5_ref[...])
        d_ga = d_ys2 * sgb * silz
        d_gb = d_ys2 * ga * sgb * (1.0 - sgb) * silz
        d_z = d_ys2 * ga * sgb * _dsilu(z_s)
        d_glu = jnp.concatenate([d_ga, d_gb], axis=1)
        gl = _gelu(ys_b)
        d_wglu = _bdot_tn(gl, d_glu)
        d_bglu = jnp.sum(d_glu, axis=0, keepdims=True)
        d_gl = _bdot_nt(d_glu, wglu_ref[...])
        return [d_gl * _dgelu(ys_b), d_z], [d_wos5, d_wglu, d_bglu]

    rows = [(d_ys5, D_MODEL, 0), (ys, S5_WIDTH, 0), (glu, D_MODEL, 0), (proj, S5_WIDTH, COL_ZS // S5_WIDTH)]
    outs, accs = _rowwise(
        "bwd_s5_path", fn, S, 256, rows, [w_o_s5, w_glu], [(S5_WIDTH, F32), (S5_WIDTH, F32)],
        [((S5_WIDTH, D_MODEL), F32), ((S5_WIDTH, D_MODEL), F32), ((1, D_MODEL), F32)], 48)
    return outs, accs


def _stage_inproj_bwd(pieces, x, dh1, norm_g, w_in_all):
    S = x.shape[0]

    def fn(rv, cr, out):
        g_ref, w_ref = cr
        x_b, dh1_b = rv[8], rv[9]
        dproj_ref = out[1]
        col = 0
        for v in rv[:8]:
            dproj_ref[:, col:col + v.shape[1]] = v.astype(BF16)
            col += v.shape[1]
        d_u = jnp.zeros((x_b.shape[0], D_MODEL), F32)
        for j in range(N_DEV):
            d_u = d_u + lax.dot_general(dproj_ref[:, j * SHARD_IN:(j + 1) * SHARD_IN], w_ref[j],
                                        (((1,), (1,)), ((), ())), preferred_element_type=F32)
        _, n, r = _rms_fwd(x_b, g_ref[...])
        dx, d_g = _rms_bwd(d_u, n, r, g_ref[...])
        return [dh1_b + dx, None], [d_g]

    rows = [(a, a.shape[1], 0) for a in pieces] + [(x, D_MODEL, 0), (dh1, D_MODEL, 0)]
    (grad_x, dproj), (d_g,) = _rowwise(
        "inproj_bwd", fn, S, 256, rows, [norm_g, w_in_all], [(D_MODEL, F32), (IN_COLS, BF16)],
        [((1, D_MODEL), F32)], 56)
    return grad_x, dproj, d_g


def _grad_w_in(dproj, u):
    S = u.shape[0]
    tm = 512
    n_i = S // tm

    def body(dp_ref, u_ref, out_ref, acc):
        i = pl.program_id(1)

        @pl.when(i == 0)
        def _():
            acc[...] = jnp.zeros(acc.shape, F32)

        acc[...] += lax.dot_general(dp_ref[...], u_ref[...], (((0,), (0,)), ((), ())), preferred_element_type=F32)

        @pl.when(i == n_i - 1)
        def _():
            out_ref[...] = acc[...].T.astype(BF16)

    return _call(
        body, name="grad_w_in", grid=(N_DEV, n_i),
        in_specs=[pl.BlockSpec((tm, SHARD_IN), lambda j, i: (i, j)), pl.BlockSpec((tm, D_MODEL), lambda j, i: (i, 0))],
        out_specs=pl.BlockSpec((None, D_MODEL, SHARD_IN), lambda j, i: (j, 0, 0)),
        out_shape=jax.ShapeDtypeStruct((N_DEV, D_MODEL, SHARD_IN), BF16),
        scratch_shapes=[pltpu.VMEM((SHARD_IN, D_MODEL), F32)],
        compiler_params=_params(32, 2),
    )(dproj, u)


def _chunk_row(shape):
    return lax.broadcasted_iota(jnp.int32, shape, 0) & (HG_CHUNK - 1)


def _chunk_cumsum(x):
    r_in = _chunk_row(x.shape)
    s = 1
    while s < HG_CHUNK:
        x = x + jnp.where(r_in >= s, pltpu.roll(x, s, 0), 0.0)
        s *= 2
    return x


def _chunk_suffix_sum(x):
    n = x.shape[0]
    r_in = _chunk_row(x.shape)
    s = 1
    while s < HG_CHUNK:
        x = x + jnp.where(r_in < HG_CHUNK - s, pltpu.roll(x, n - s, 0), 0.0)
        s *= 2
    return x


def _hgrn_prep(q, fl, lb):
    nc = HG_SUPER // HG_CHUNK
    sig = _sigmoid(fl)
    f = lb + (1.0 - lb) * sig
    k = (1.0 - lb) * (1.0 - sig)
    b = _chunk_cumsum(jnp.log(f))
    b3 = b.reshape(nc, HG_CHUNK, HG_DIM)
    row3 = lax.broadcasted_iota(jnp.int32, b3.shape, 1)
    pick = lambda r: jnp.sum(jnp.where(row3 == r, b3, 0.0), axis=1, keepdims=True)
    b_mid = pick(HG_CHUNK // 2 - 1)
    b_last = pick(HG_CHUNK - 1)
    flat = lambda t: t.reshape(HG_SUPER, HG_DIM)
    e_qa = flat(jnp.exp(b3 - b_mid))
    e_ka = flat(jnp.exp(b_mid - b3))
    e_qd = jnp.exp(b)
    e_kd = flat(jnp.exp(b_last - b3))
    dc = jnp.exp(b_last)
    return sig, f, k, e_qa, e_ka, e_qd, e_kd, dc


def _hgrn_mask():
    r = lax.broadcasted_iota(jnp.int32, (HG_SUPER, HG_SUPER), 0)
    c = lax.broadcasted_iota(jnp.int32, (HG_SUPER, HG_SUPER), 1)
    shift = HG_CHUNK.bit_length() - 1
    return (jnp.right_shift(r, shift) == jnp.right_shift(c, shift)) & (r >= c)


def _hgrn_fwd(proj, lb):
    S = proj.shape[0]
    nb = S // HG_SUPER
    nc = HG_SUPER // HG_CHUNK
    hp = HG_HEADS_PER_STEP
    wide = hp * HG_DIM

    def body(q_ref, f_ref, iv_ref, lb_ref, o_ref, st_ref, state):
        @pl.when(pl.program_id(1) == 0)
        def _():
            state[...] = jnp.zeros(state.shape, F32)

        mask = _hgrn_mask()
        for hh in range(hp):
            lanes = slice(hh * HG_DIM, (hh + 1) * HG_DIM)
            q, iv = q_ref[:, lanes], iv_ref[:, lanes]
            _, _, k, e_qa, e_ka, e_qd, e_kd, dc = _hgrn_prep(q, f_ref[:, lanes], lb_ref[:, lanes])
            scores = jnp.where(mask, _bdot_nt(q * e_qa, k * e_ka), 0.0)
            o_intra = _bdot(scores, iv)
            qd, kd = q * e_qd, k * e_kd
            for c in range(nc):
                sl = slice(c * HG_CHUNK, (c + 1) * HG_CHUNK)
                st = state[hh]
                st_ref[hh, c] = st
                o_ref[sl, lanes] = o_intra[sl] + _bdot_nt(qd[sl], st)
                state[hh] = dc[c] * st + _bdot_tn(iv[sl], kd[sl])

    blk = lambda base: pl.BlockSpec((HG_SUPER, wide), functools.partial(lambda h, i, b: (i, b + h), b=base // wide))
    return _call(
        body, name="hgrn_fwd", grid=(HG_HEADS // hp, nb),
        in_specs=[blk(COL_Q), blk(COL_F), blk(COL_I), pl.BlockSpec((1, wide), lambda h, i: (0, h))],
        out_specs=[pl.BlockSpec((HG_SUPER, wide), lambda h, i: (i, h)),
                   pl.BlockSpec((hp, nc, HG_DIM, HG_DIM), lambda h, i: (h, i, 0, 0))],
        out_shape=[jax.ShapeDtypeStruct((S, D_MODEL), F32),
                   jax.ShapeDtypeStruct((HG_HEADS, S // HG_CHUNK, HG_DIM, HG_DIM), F32)],
        scratch_shapes=[pltpu.VMEM((hp, HG_DIM, HG_DIM), F32)],
        compiler_params=_params(40, 2),
    )(proj, proj, proj, lb)


def _hgrn_bwd(proj, lb, d_o, states):
    S = proj.shape[0]
    nb = S // HG_SUPER
    nc = HG_SUPER // HG_CHUNK
    hp = HG_HEADS_PER_STEP
    wide = hp * HG_DIM

    def body(q_ref, f_ref, iv_ref, lb_ref, do_ref, st_ref, dq_ref, df_ref, div_ref, dlb_ref, dstate):
        @pl.when(pl.program_id(1) == 0)
        def _():
            dstate[...] = jnp.zeros(dstate.shape, F32)
            dlb_ref[...] = jnp.zeros(dlb_ref.shape, F32)

        mask = _hgrn_mask()
        for hh in range(hp):
            lanes = slice(hh * HG_DIM, (hh + 1) * HG_DIM)
            q, iv, do, lb_v = q_ref[:, lanes], iv_ref[:, lanes], do_ref[:, lanes], lb_ref[:, lanes]
            sig, f, k, e_qa, e_ka, e_qd, e_kd, dc = _hgrn_prep(q, f_ref[:, lanes], lb_v)
            qa, ka, qd, kd = q * e_qa, k * e_ka, q * e_qd, k * e_kd
            scores = jnp.where(mask, _bdot_nt(qa, ka), 0.0)
            d_scores = jnp.where(mask, _bdot_nt(do, iv), 0.0)
            d_iv_intra = _bdot_tn(scores, do)
            d_qa = _bdot(d_scores, ka)
            d_ka = _bdot_tn(d_scores, qa)
            d_qd, d_kd, d_last = [None] * nc, [None] * nc, [None] * nc
            for c in reversed(range(nc)):
                sl = slice(c * HG_CHUNK, (c + 1) * HG_CHUNK)
                st = st_ref[hh, c]
                ds = dstate[hh]
                d_qd[c] = _bdot(do[sl], st)
                d_kd[c] = _bdot(iv[sl], ds)
                div_ref[sl, lanes] = d_iv_intra[sl] + _bdot_nt(kd[sl], ds)
                d_last[c] = (jnp.sum(ds * st, axis=0, keepdims=True) * dc[c]
                             + jnp.sum(d_kd[c] * kd[sl], axis=0, keepdims=True))
                dstate[hh] = dc[c] * ds + _bdot_tn(do[sl], qd[sl])
            d_qd = jnp.concatenate(d_qd, axis=0)
            d_kd = jnp.concatenate(d_kd, axis=0)
            d_b = d_qa * qa - d_ka * ka + d_qd * qd - d_kd * kd
            last_rows = jnp.concatenate([jnp.broadcast_to(t, (HG_CHUNK, HG_DIM)) for t in d_last], axis=0)
            d_b = d_b + jnp.where(_chunk_row(d_b.shape) == HG_CHUNK - 1, last_rows, 0.0)
            d_logf = _chunk_suffix_sum(d_b)
            d_k = d_ka * e_ka + d_kd * e_kd
            g_f = d_logf / f
            d_sig = (g_f - d_k) * (1.0 - lb_v)
            dq_ref[:, lanes] = d_qa * e_qa + d_qd * e_qd
            df_ref[:, lanes] = d_sig * sig * (1.0 - sig)
            d_lb = jnp.sum((g_f - d_k) * (1.0 - sig), axis=0, keepdims=True)
            dlb_ref[:, lanes] += jnp.broadcast_to(d_lb, (8, HG_DIM))

    rev = lambda i: nb - 1 - i
    blk = lambda base: pl.BlockSpec((HG_SUPER, wide), functools.partial(lambda h, i, b: (rev(i), b + h), b=base // wide))
    row_out = pl.BlockSpec((HG_SUPER, wide), lambda h, i: (rev(i), h))
    dq, df, div, dlb = _call(
        body, name="hgrn_bwd", grid=(HG_HEADS // hp, nb),
        in_specs=[blk(COL_Q), blk(COL_F), blk(COL_I), pl.BlockSpec((1, wide), lambda h, i: (0, h)),
                  pl.BlockSpec((HG_SUPER, wide), lambda h, i: (rev(i), h)),
                  pl.BlockSpec((hp, nc, HG_DIM, HG_DIM), lambda h, i: (h, rev(i), 0, 0))],
        out_specs=[row_out, row_out, row_out, pl.BlockSpec((8, wide), lambda h, i: (0, h))],
        out_shape=[jax.ShapeDtypeStruct((S, D_MODEL), F32)] * 3 + [jax.ShapeDtypeStruct((8, D_MODEL), F32)],
        scratch_shapes=[pltpu.VMEM((hp, HG_DIM, HG_DIM), F32)],
        compiler_params=_params(40, 2),
    )(proj, proj, proj, lb, d_o, states)
    return dq, df, div, dlb[0:1]


def _s5_matrices(a_re, a_im, log_dt, b_re, b_im, c_re, c_im, d, seg_len):
    dt = jnp.exp(log_dt)[:, None]
    mag = jnp.exp(a_re * dt)
    lr, li = mag * jnp.cos(a_im * dt), mag * jnp.sin(a_im * dt)
    den = a_re * a_re + a_im * a_im
    nr = lr - 1.0
    sr = (nr * a_re + li * a_im) / den
    si = (li * a_re - nr * a_im) / den
    bbr = sr[..., None] * b_re - si[..., None] * b_im
    bbi = sr[..., None] * b_im + si[..., None] * b_re
    eye = jnp.eye(8, dtype=F32)

    def quad_cols(v):
        return v.reshape(S5_QUADS, 8 * S5_STATE)

    def lam_row(re_part, im_part):
        row = jnp.concatenate([quad_cols(re_part), quad_cols(im_part)], axis=1).reshape(1, S5_COLS)
        return jnp.broadcast_to(row, (S5_SEG, S5_COLS))

    def b_mat(bb):
        t = bb.reshape(S5_QUADS, 8, S5_STATE, S5_CH)
        return jnp.einsum("qgnc,gh->qgchn", t, eye).reshape(S5_QUADS, 8 * S5_CH, 8 * S5_STATE)

    def c_mat(cc):
        t = cc.reshape(S5_QUADS, 8, S5_CH, S5_STATE)
        return jnp.einsum("qgcn,gh->qgnhc", t, eye).reshape(S5_QUADS, 8 * S5_STATE, 8 * S5_CH)

    ang = a_im * dt * seg_len
    magp = jnp.exp(a_re * dt * seg_len)
    lpr, lpi = magp * jnp.cos(ang), magp * jnp.sin(ang)
    return dict(
        lam_r=lam_row(lr, lr), lam_i=lam_row(-li, li), lam_ic=lam_row(li, -li),
        b_q=jnp.concatenate([b_mat(bbr), b_mat(bbi)], axis=2),
        c_q=jnp.concatenate([c_mat(c_re), -c_mat(c_im)], axis=1),
        d_row=d.reshape(1, S5_WIDTH), pow_r=quad_cols(lpr), pow_i=quad_cols(lpi),
    )


def _swap_halves(h):
    half = S5_QCOLS // 2
    parts = []
    for q in range(S5_QUADS):
        parts.append(h[:, q * S5_QCOLS + half:(q + 1) * S5_QCOLS])
        parts.append(h[:, q * S5_QCOLS:q * S5_QCOLS + half])
    return jnp.concatenate(parts, axis=1)


def _s5_fwd_pass(u_perm, mats, h0, with_output):
    S = u_perm.shape[0]
    rows = S5_TILE_STEPS * S5_SEG
    nt = S // rows

    def body(*refs):
        if with_output:
            u_ref, b_ref, lr_ref, li_ref, h0_ref, c_ref, d_ref, y_ref, hinit_ref, hend_ref, xs, hcar = refs
        else:
            u_ref, b_ref, lr_ref, li_ref, h0_ref, hend_ref, xs, hcar = refs

        @pl.when(pl.program_id(0) == 0)
        def _():
            hcar[...] = h0_ref[...]

        if with_output:
            hinit_ref[...] = hcar[...]
        u = u_ref[...]
        ub = u.astype(BF16)
        for q in range(S5_QUADS):
            xs[:, q * S5_QCOLS:(q + 1) * S5_QCOLS] = jnp.dot(ub[:, q * 128:(q + 1) * 128], b_ref[q], preferred_element_type=F32)

        def step(t, h):
            sl = pl.ds(pl.multiple_of(t * S5_SEG, S5_SEG), S5_SEG)
            hn = lr_ref[...] * h + li_ref[...] * _swap_halves(h) + xs[sl, :]
            xs[sl, :] = hn
            return hn

        h = lax.fori_loop(0, S5_TILE_STEPS, step, hcar[...])
        hcar[...] = h
        hend_ref[...] = h
        if with_output:
            ys = [jnp.dot(xs[:, q * S5_QCOLS:(q + 1) * S5_QCOLS].astype(BF16), c_ref[q], preferred_element_type=F32)
                  for q in range(S5_QUADS)]
            y_ref[...] = jnp.concatenate(ys, axis=1) + d_ref[...] * u

    full = lambda a: pl.BlockSpec(a.shape, functools.partial(lambda i, nd: (0,) * nd, nd=a.ndim))
    ins = [u_perm, mats["b_q"], mats["lam_r"], mats["lam_i"], h0]
    in_specs = [pl.BlockSpec((rows, S5_WIDTH), lambda i: (i, 0))] + [full(a) for a in ins[1:]]
    out_specs = [pl.BlockSpec((S5_SEG, S5_COLS), lambda i: (0, 0))]
    out_shape = [jax.ShapeDtypeStruct((S5_SEG, S5_COLS), F32)]
    if with_output:
        ins += [mats["c_q"], mats["d_row"]]
        in_specs += [full(mats["c_q"]), full(mats["d_row"])]
        out_specs = [pl.BlockSpec((rows, S5_WIDTH), lambda i: (i, 0)),
                     pl.BlockSpec((None, S5_SEG, S5_COLS), lambda i: (i, 0, 0))] + out_specs
        out_shape = [jax.ShapeDtypeStruct((S, S5_WIDTH), F32), jax.ShapeDtypeStruct((nt, S5_SEG, S5_COLS), F32)] + out_shape
    return _call(
        body, name="s5_fwd_y" if with_output else "s5_fwd_ends", grid=(nt,), in_specs=in_specs, out_specs=out_specs,
        out_shape=out_shape,
        scratch_shapes=[pltpu.VMEM((rows, S5_COLS), F32), pltpu.VMEM((S5_SEG, S5_COLS), F32)],
        compiler_params=_params(40),
    )(*ins)


def _s5_bwd_ends(dy_perm, mats):
    S = dy_perm.shape[0]
    rows = S5_TILE_STEPS * S5_SEG
    nt = S // rows

    def body(dy_ref, c_ref, lr_ref, lic_ref, gend_ref, gs, gcar):
        @pl.when(pl.program_id(0) == 0)
        def _():
            gcar[...] = jnp.zeros(gcar.shape, F32)

        dyb = dy_ref[...].astype(BF16)
        for q in range(S5_QUADS):
            gs[:, q * S5_QCOLS:(q + 1) * S5_QCOLS] = lax.dot_general(
                dyb[:, q * 128:(q + 1) * 128], c_ref[q], (((1,), (1,)), ((), ())), preferred_element_type=F32)

        def step(k, g):
            t = S5_TILE_STEPS - 1 - k
            sl = pl.ds(pl.multiple_of(t * S5_SEG, S5_SEG), S5_SEG)
            return lr_ref[...] * g + lic_ref[...] * _swap_halves(g) + gs[sl, :]

        g = lax.fori_loop(0, S5_TILE_STEPS, step, gcar[...])
        gcar[...] = g
        gend_ref[...] = g

    full = lambda a: pl.BlockSpec(a.shape, functools.partial(lambda i, nd: (0,) * nd, nd=a.ndim))
    return _call(
        body, name="s5_bwd_ends", grid=(nt,),
        in_specs=[pl.BlockSpec((rows, S5_WIDTH), lambda i: (nt - 1 - i, 0)), full(mats["c_q"]), full(mats["lam_r"]),
                  full(mats["lam_ic"])],
        out_specs=pl.BlockSpec((S5_SEG, S5_COLS), lambda i: (0, 0)),
        out_shape=jax.ShapeDtypeStruct((S5_SEG, S5_COLS), F32),
        scratch_shapes=[pltpu.VMEM((rows, S5_COLS), F32), pltpu.VMEM((S5_SEG, S5_COLS), F32)],
        compiler_params=_params(40),
    )(dy_perm, mats["c_q"], mats["lam_r"], mats["lam_ic"])


def _s5_bwd_full(u_perm, dy_perm, hinit, g0, mats):
    S = u_perm.shape[0]
    rows = S5_TILE_STEPS * S5_SEG
    nt = S // rows

    def body(u_ref, dy_ref, hinit_ref, g0_ref, b_ref, c_ref, lr_ref, li_ref, lic_ref, d_ref,
             du_ref, dp_ref, dq_ref, db_ref, dc_ref, dd_ref, hs, gs, gcar):
        @pl.when(pl.program_id(0) == 0)
        def _():
            gcar[...] = g0_ref[...]
            for ref in (dp_ref, dq_ref, db_ref, dc_ref, dd_ref):
                ref[...] = jnp.zeros(ref.shape, F32)

        u, dy = u_ref[...], dy_ref[...]
        ub, dyb = u.astype(BF16), dy.astype(BF16)
        hs[0:S5_SEG, :] = hinit_ref[...]
        for q in range(S5_QUADS):
            cols = slice(q * S5_QCOLS, (q + 1) * S5_QCOLS)
            hs[S5_SEG:, cols] = jnp.dot(ub[:, q * 128:(q + 1) * 128], b_ref[q], preferred_element_type=F32)
            gs[:, cols] = lax.dot_general(dyb[:, q * 128:(q + 1) * 128], c_ref[q], (((1,), (1,)), ((), ())),
                                          preferred_element_type=F32)

        def fstep(t, h):
            sl = pl.ds(pl.multiple_of((t + 1) * S5_SEG, S5_SEG), S5_SEG)
            hn = lr_ref[...] * h + li_ref[...] * _swap_halves(h) + hs[sl, :]
            hs[sl, :] = hn
            return hn

        lax.fori_loop(0, S5_TILE_STEPS, fstep, hinit_ref[...])

        def bstep(k, g):
            t = S5_TILE_STEPS - 1 - k
            sl = pl.ds(pl.multiple_of(t * S5_SEG, S5_SEG), S5_SEG)
            gn = lr_ref[...] * g + lic_ref[...] * _swap_halves(g) + gs[sl, :]
            gs[sl, :] = gn
            return gn

        gcar[...] = lax.fori_loop(0, S5_TILE_STEPS, bstep, gcar[...])

        half = S5_QCOLS // 2
        dus = []
        for q in range(S5_QUADS):
            cols = slice(q * S5_QCOLS, (q + 1) * S5_QCOLS)

            def astep(t, carry, q=q):
                sl = pl.ds(pl.multiple_of(t * S5_SEG, S5_SEG), S5_SEG)
                g = gs[sl, q * S5_QCOLS:(q + 1) * S5_QCOLS]
                hp = hs[sl, q * S5_QCOLS:(q + 1) * S5_QCOLS]
                hp_sw = jnp.concatenate([hp[:, half:], hp[:, :half]], axis=1)
                return carry[0] + g * hp, carry[1] + g * hp_sw

            zero = jnp.zeros((S5_SEG, S5_QCOLS), F32)
            acc_p, acc_q = lax.fori_loop(0, S5_TILE_STEPS, astep, (zero, zero))
            dp_ref[:, cols] += jnp.sum(acc_p, axis=0, keepdims=True)
            dq_ref[:, cols] += jnp.sum(acc_q, axis=0, keepdims=True)
            gq = gs[:, cols].astype(BF16)
            db_ref[q] += lax.dot_general(ub[:, q * 128:(q + 1) * 128], gq, (((0,), (0,)), ((), ())),
                                         preferred_element_type=F32)
            hq = hs[S5_SEG:, cols].astype(BF16)
            dc_ref[q] += lax.dot_general(dyb[:, q * 128:(q + 1) * 128], hq, (((0,), (0,)), ((), ())),
                                         preferred_element_type=F32)
            dus.append(lax.dot_general(gq, b_ref[q], (((1,), (1,)), ((), ())), preferred_element_type=F32))
        du_ref[...] = jnp.concatenate(dus, axis=1) + d_ref[...] * dy
        dd_ref[...] += jnp.sum(dy * u, axis=0, keepdims=True)

    full = lambda a: pl.BlockSpec(a.shape, functools.partial(lambda i, nd: (0,) * nd, nd=a.ndim))
    rev_rows = pl.BlockSpec((rows, S5_WIDTH), lambda i: (nt - 1 - i, 0))
    consts = [mats["b_q"], mats["c_q"], mats["lam_r"], mats["lam_i"], mats["lam_ic"], mats["d_row"]]
    acc = lambda s: pl.BlockSpec(s, functools.partial(lambda i, nd: (0,) * nd, nd=len(s)))
    acc_shapes = [(1, S5_COLS), (1, S5_COLS), (S5_QUADS, 128, S5_QCOLS), (S5_QUADS, 128, S5_QCOLS), (1, S5_WIDTH)]
    return _call(
        body, name="s5_bwd_full", grid=(nt,),
        in_specs=[rev_rows, rev_rows, pl.BlockSpec((None, S5_SEG, S5_COLS), lambda i: (nt - 1 - i, 0, 0)), full(g0)]
        + [full(a) for a in consts],
        out_specs=[rev_rows] + [acc(s) for s in acc_shapes],
        out_shape=[jax.ShapeDtypeStruct((S, S5_WIDTH), F32)] + [jax.ShapeDtypeStruct(s, F32) for s in acc_shapes],
        scratch_shapes=[pltpu.VMEM((rows + S5_SEG, S5_COLS), F32), pltpu.VMEM((rows, S5_COLS), F32),
                        pltpu.VMEM((S5_SEG, S5_COLS), F32)],
        compiler_params=_params(56),
    )(u_perm, dy_perm, hinit, g0, *consts)


def _cmul(ar, ai, br, bi):
    return ar * br - ai * bi, ar * bi + ai * br


def _split_cols(v):
    t = v.reshape(v.shape[0], S5_QUADS, 2, S5_QCOLS // 2)
    return t[:, :, 0], t[:, :, 1]


def _join_cols(re, im):
    return jnp.stack([re, im], axis=2).reshape(re.shape[0], S5_COLS)


def _segment_starts(ends, pow_r, pow_i, reverse):
    er, ei = _split_cols(ends)
    pi = -pow_i if reverse else pow_i
    order = list(range(S5_SEG))
    if reverse:
        order = order[::-1]
    cr, ci = jnp.zeros_like(er[0]), jnp.zeros_like(ei[0])
    out_r, out_i = [None] * S5_SEG, [None] * S5_SEG
    for j in order:
        out_r[j], out_i[j] = cr, ci
        mr, mi = _cmul(pow_r, pi, cr, ci)
        cr, ci = mr + er[j], mi + ei[j]
    return _join_cols(jnp.stack(out_r), jnp.stack(out_i))


def _to_segments(a):
    S, w = a.shape
    return a.reshape(S5_SEG, S // S5_SEG, w).transpose(1, 0, 2).reshape(S, w)


def _from_segments(a):
    S, w = a.shape
    return a.reshape(S // S5_SEG, S5_SEG, w).transpose(1, 0, 2).reshape(S, w)


def _my_pos():
    return lax.axis_index("x"), lax.axis_index("y"), lax.axis_index("c")


def _flip(pos, k):
    x, y, c = pos
    return (1 - x if k & 4 else x, 1 - y if k & 2 else y, 1 - c if k & 1 else c)


def _index_of(pos):
    return 4 * pos[0] + 2 * pos[1] + pos[2]


_GATHER_FLIPS = (0, 1, 4, 5, 2, 3, 6, 7)


def _inproj_gather(x, norm_g, pack_a, pack_b, pack_c):
    S = x.shape[0]
    tm = 512
    n_i = S // tm
    order = jnp.stack([_index_of(_flip(_my_pos(), k)) for k in _GATHER_FLIPS]).astype(jnp.int32)

    def body(order_ref, x_ref, g_ref, pa_ref, pb_ref, pc_ref, u_ref, proj_ref, oa_ref, ob_ref, oc_ref,
             wv, u_scr, send_sems, recv_sems, local_sems):
        s, i = pl.program_id(0), pl.program_id(1)
        me = _my_pos()
        mine = _index_of(me)
        sibling = _flip(me, 1)
        srcs = (pb_ref, pa_ref, pc_ref)
        dsts = (wv, oa_ref, oc_ref)

        def direct(a, k):
            return pltpu.make_async_remote_copy(
                src_ref=srcs[a], dst_ref=dsts[a].at[mine], send_sem=send_sems.at[a * 8 + k],
                recv_sem=recv_sems.at[a * 8 + k], device_id=_flip(me, k), device_id_type=MESH)

        def passed_on(a, k):
            slot = _index_of(_flip(me, k))
            return pltpu.make_async_remote_copy(
                src_ref=dsts[a].at[slot], dst_ref=dsts[a].at[slot], send_sem=send_sems.at[a * 8 + (k | 1)],
                recv_sem=recv_sems.at[a * 8 + (k | 1)], device_id=sibling, device_id_type=MESH)

        def arrival(a, k):
            slot = _index_of(_flip(me, k))
            pltpu.make_async_remote_copy(
                src_ref=dsts[a].at[slot], dst_ref=dsts[a].at[slot], send_sem=send_sems.at[a * 8 + k],
                recv_sem=recv_sems.at[a * 8 + k], device_id=me, device_id_type=MESH).wait_recv()

        def own_copy(a):
            return pltpu.make_async_copy(srcs[a], dsts[a].at[mine], local_sems.at[a])

        first = (s == 0) & (i == 0)

        @pl.when(first)
        def _():
            for a in range(3):
                own_copy(a).start()
            for a in range(3):
                for k in (1, 4, 2, 6):
                    direct(a, k).start()
            own_copy(0).wait()

        for idx, k in enumerate(_GATHER_FLIPS):
            if idx == 0:
                continue

            @pl.when((s == idx) & (i == 0))
            def _(k=k):
                arrival(0, k)
                if k in (4, 2, 6):
                    passed_on(0, k).start()

        @pl.when(s == 0)
        def _():
            y, _, _ = _rms_fwd(x_ref[...], g_ref[...])
            u_scr[pl.ds(pl.multiple_of(i * tm, tm), tm), :] = y.astype(BF16)

        ub = u_scr[pl.ds(pl.multiple_of(i * tm, tm), tm), :]
        proj_ref[...] = jnp.dot(ub, wv[order_ref[s]], preferred_element_type=F32)

        @pl.when((s == N_DEV - 1) & (i == n_i - 1))
        def _():
            for a in (1, 2):
                for k in (4, 2, 6):
                    arrival(a, k)
                    passed_on(a, k).start()
            for a in (1, 2):
                for k in (1, 5, 3, 7):
                    arrival(a, k)
                own_copy(a).wait()
            for a in range(3):
                for k in (1, 4, 2, 6):
                    direct(a, k).wait_send()
                for k in (4, 2, 6):
                    passed_on(a, k).wait_send()
            outs = [pltpu.make_async_copy(wv, ob_ref, local_sems.at[3]),
                    pltpu.make_async_copy(u_scr, u_ref, local_sems.at[4])]
            for cp in outs:
                cp.start()
            for cp in outs:
                cp.wait()

    any_spec = pl.BlockSpec(memory_space=pl.ANY)
    vmem = pl.BlockSpec(memory_space=pltpu.VMEM)
    grid_spec = pltpu.PrefetchScalarGridSpec(
        num_scalar_prefetch=1, grid=(N_DEV, n_i),
        in_specs=[pl.BlockSpec((tm, D_MODEL), lambda s, i, o: (jnp.where(s == 0, i, 0), 0)),
                  pl.BlockSpec((1, D_MODEL), lambda s, i, o: (0, 0)), any_spec, vmem, any_spec],
        out_specs=[any_spec, pl.BlockSpec((tm, SHARD_IN), lambda s, i, o: (i, o[s])), any_spec, any_spec, any_spec],
        scratch_shapes=[pltpu.VMEM((N_DEV,) + pack_b.shape, BF16), pltpu.VMEM((S, D_MODEL), BF16),
                        pltpu.SemaphoreType.DMA((24,)), pltpu.SemaphoreType.DMA((24,)), pltpu.SemaphoreType.DMA((5,))],
    )
    return _call(
        body, name="inproj_gather", grid_spec=grid_spec,
        out_shape=[jax.ShapeDtypeStruct((S, D_MODEL), BF16), jax.ShapeDtypeStruct((S, IN_COLS), F32),
                   jax.ShapeDtypeStruct((N_DEV,) + pack_a.shape, BF16), jax.ShapeDtypeStruct((N_DEV,) + pack_b.shape, BF16),
                   jax.ShapeDtypeStruct((N_DEV,) + pack_c.shape, BF16)],
        compiler_params=_params(56, 2),
    )(order, x, norm_g, pack_a, pack_b, pack_c)


def _reduce_scatter(arrays):
    n = len(arrays)
    chunk = 16

    def body(*refs):
        in_refs, out_refs = refs[:n], refs[n:2 * n]
        own = refs[2 * n:3 * n]
        land = refs[3 * n:4 * n]
        send_sems, recv_sems, local_sems = refs[4 * n:]
        me = _my_pos()
        mine = _index_of(me)
        local = [pltpu.make_async_copy(in_refs[a].at[mine], own[a], local_sems.at[a]) for a in range(n)]
        for cp in local:
            cp.start()

        def copy(a, k, to):
            return pltpu.make_async_remote_copy(
                src_ref=in_refs[a].at[_index_of(to)], dst_ref=land[a].at[k - 1],
                send_sem=send_sems.at[a * 7 + k - 1], recv_sem=recv_sems.at[a * 7 + k - 1],
                device_id=to, device_id_type=MESH)

        sent = []
        for a in range(n):
            for k in range(1, 8):
                cp = copy(a, k, _flip(me, k))
                cp.start()
                sent.append(cp)
        for a in range(n):
            local[a].wait()
            for k in range(1, 8):
                copy(a, k, me).wait_recv()
            rows = own[a].shape[0]

            def add_rows(i, carry, a=a):
                sl = pl.ds(pl.multiple_of(i * chunk, chunk), chunk)
                acc = own[a][sl, :].astype(F32)
                for k in range(7):
                    acc = acc + land[a][k, sl, :].astype(F32)
                out_refs[a][sl, :] = acc
                return carry

            lax.fori_loop(0, rows // chunk, add_rows, 0)
        for cp in sent:
            cp.wait_send()

    any_spec = pl.BlockSpec(memory_space=pl.ANY)
    vmem = pl.BlockSpec(memory_space=pltpu.VMEM)
    scratch = [pltpu.VMEM(a.shape[1:], a.dtype) for a in arrays]
    scratch += [pltpu.VMEM((7,) + a.shape[1:], a.dtype) for a in arrays]
    scratch += [pltpu.SemaphoreType.DMA((7 * n,)), pltpu.SemaphoreType.DMA((7 * n,)), pltpu.SemaphoreType.DMA((n,))]
    return _call(
        body, name="reduce_scatter_grads",
        in_specs=[any_spec] * n, out_specs=[vmem] * n,
        out_shape=[jax.ShapeDtypeStruct(a.shape[1:], F32) for a in arrays],
        scratch_shapes=scratch,
        compiler_params=pltpu.CompilerParams(vmem_limit_bytes=56 * MIB),
    )(*arrays)


def _adam_update(g, w, m, v):
    m2 = ADAM_B1 * m + (1.0 - ADAM_B1) * g
    v2 = ADAM_B2 * v + (1.0 - ADAM_B2) * (g * g)
    m_hat = m2 / (1.0 - ADAM_B1 ** ADAM_STEP)
    v_hat = v2 / (1.0 - ADAM_B2 ** ADAM_STEP)
    delta = -ADAM_LR * (m_hat / (jnp.sqrt(v_hat) + ADAM_EPS) + ADAM_WD * w)
    return delta, m2, v2


def _small_allreduce_adam(partial, w, m, v):
    rows = partial.shape[0]

    def body(p_ref, w_ref, m_ref, v_ref, g_ref, d_ref, m2_ref, v2_ref, gath, send_sems, recv_sems):
        me = _my_pos()
        mine = _index_of(me)
        gath[mine] = p_ref[...]
        sent = []
        for k in range(1, 8):
            cp = pltpu.make_async_remote_copy(
                src_ref=p_ref, dst_ref=gath.at[mine], send_sem=send_sems.at[k - 1], recv_sem=recv_sems.at[k - 1],
                device_id=_flip(me, k), device_id_type=MESH)
            cp.start()
            sent.append(cp)
        for k in range(1, 8):
            pltpu.make_async_remote_copy(
                src_ref=p_ref, dst_ref=gath.at[_index_of(_flip(me, k))], send_sem=send_sems.at[k - 1],
                recv_sem=recv_sems.at[k - 1], device_id=me, device_id_type=MESH).wait_recv()
        g = gath[0]
        for dev in range(1, N_DEV):
            g = g + gath[dev]
        delta, m2, v2 = _adam_update(g, w_ref[...], m_ref[...], v_ref[...])
        g_ref[...] = g
        d_ref[...] = delta
        m2_ref[...] = m2
        v2_ref[...] = v2
        for cp in sent:
            cp.wait_send()

    vmem = pl.BlockSpec(memory_space=pltpu.VMEM)
    return _call(
        body, name="small_allreduce_adam", in_specs=[vmem] * 4, out_specs=[vmem] * 4,
        out_shape=[jax.ShapeDtypeStruct((rows, 128), F32)] * 4,
        scratch_shapes=[pltpu.VMEM((N_DEV, rows, 128), F32), pltpu.SemaphoreType.DMA((7,)), pltpu.SemaphoreType.DMA((7,))],
        compiler_params=pltpu.CompilerParams(vmem_limit_bytes=32 * MIB),
    )(partial, w, m, v)


def _adam_rows(g, w, m, v):
    rows, cols = w.shape
    tm = rows if rows <= 256 else 256

    def fn(rv, cr, out):
        return list(_adam_update(*rv)), []

    outs, _ = _rowwise("adamw", fn, rows, tm, [(a, cols, 0) for a in (g, w, m, v)], [], [(cols, F32)] * 3, [], 32)
    return outs


_SMALL = ["norm_g", "hg_lb", "hg_norm_g", "s5_a_re", "s5_a_im", "s5_log_dt", "s5_b_re", "s5_b_im", "s5_c_re",
          "s5_c_im", "s5_d", "b_glu", "ple_norm_g", "final_norm_g"]
_BIG = ["w_in", "w_o_hg", "w_glu", "w_o_s5", "w_out", "w_ple", "w_ple_gate"]
_ORDER = ["norm_g", "w_in", "hg_lb", "hg_norm_g", "w_o_hg", "s5_a_re", "s5_a_im", "s5_log_dt", "s5_b_re", "s5_b_im",
          "s5_c_re", "s5_c_im", "s5_d", "w_glu", "b_glu", "w_o_s5", "w_out", "ple_norm_g", "w_ple", "w_ple_gate",
          "final_norm_g"]


def _pack_small(vals):
    parts = []
    for name in _SMALL:
        flat = vals[name].reshape(-1).astype(F32)
        pad = (-flat.shape[0]) % 1024
        parts.append(jnp.pad(flat, (0, pad)))
    return jnp.concatenate(parts).reshape(-1, 128)


def _unpack_small(packed, like):
    flat = packed.reshape(-1)
    out, off = {}, 0
    for name in _SMALL:
        size = like[name].size
        out[name] = flat[off:off + size].reshape(like[name].shape)
        off += size + (-size) % 1024
    return out


def _col_blocks(full):
    k = full.shape[0]
    return full.reshape(k, N_DEV, 128).transpose(1, 0, 2)


def _from_col_blocks(blocks):
    k = blocks.shape[1]
    return blocks.transpose(1, 0, 2).reshape(k, N_DEV * 128)


def kernel(x, p, norm_g, w_in, hg_lb, hg_norm_g, w_o_hg, s5_a_re, s5_a_im, s5_log_dt, s5_b_re, s5_b_im, s5_c_re, s5_c_im, s5_d, w_glu, b_glu, w_o_s5, w_out, ple_norm_g, w_ple, w_ple_gate, final_norm_g, loss_target, m_norm_g, m_w_in, m_hg_lb, m_hg_norm_g, m_w_o_hg, m_s5_a_re, m_s5_a_im, m_s5_log_dt, m_s5_b_re, m_s5_b_im, m_s5_c_re, m_s5_c_im, m_s5_d, m_w_glu, m_b_glu, m_w_o_s5, m_w_out, m_ple_norm_g, m_w_ple, m_w_ple_gate, m_final_norm_g, v_norm_g, v_w_in, v_hg_lb, v_hg_norm_g, v_w_o_hg, v_s5_a_re, v_s5_a_im, v_s5_log_dt, v_s5_b_re, v_s5_b_im, v_s5_c_re, v_s5_c_im, v_s5_d, v_w_glu, v_b_glu, v_w_o_s5, v_w_out, v_ple_norm_g, v_w_ple, v_w_ple_gate, v_final_norm_g):
    args = dict(locals())
    w = {n: args[n] for n in _ORDER}
    m = {n: args["m_" + n] for n in _ORDER}
    v = {n: args["v_" + n] for n in _ORDER}
    xs = x[0]
    ps = p[0, 0]
    tgt = loss_target[0]
    S = xs.shape[0]

    pack_a = jnp.concatenate([w_o_hg[0], w_out[0], w_ple_gate[0]], axis=0).astype(BF16)
    pack_b = w_in[0].astype(BF16)
    pack_c = jnp.concatenate([w_glu[0], w_o_s5[0], w_ple[0]], axis=0).astype(BF16)
    u, proj, all_a, all_b, all_c = _inproj_gather(xs, norm_g, pack_a, pack_b, pack_c)
    wf_o_hg = all_a[:, 0:128].reshape(D_MODEL, D_MODEL)
    wf_out = all_a[:, 128:256].reshape(D_MODEL, D_MODEL)
    wf_pg = all_a[:, 256:384].reshape(D_MODEL, D_MODEL)
    wf_glu = _from_col_blocks(all_c[:, 0:512])
    wf_o_s5 = _from_col_blocks(all_c[:, 512:1024])
    wf_ple = _from_col_blocks(all_c[:, 1024:1280])

    lb = jax.nn.sigmoid(hg_lb[0:1] - hg_lb[1:2])
    s5_names = ["s5_a_re", "s5_a_im", "s5_log_dt", "s5_b_re", "s5_b_im", "s5_c_re", "s5_c_im", "s5_d"]
    build = lambda *a: _s5_matrices(*a, seg_len=S // S5_SEG)
    mats_f32, mats_vjp = jax.vjp(build, *[w[n][0] for n in s5_names])
    mats = dict(mats_f32, b_q=mats_f32["b_q"].astype(BF16), c_q=mats_f32["c_q"].astype(BF16))
    bias_glu = b_glu

    o, states = _hgrn_fwd(proj, lb)
    u_perm = _to_segments(proj[:, COL_US:COL_US + S5_WIDTH])
    zeros_state = jnp.zeros((S5_SEG, S5_COLS), F32)
    (h_ends,) = _s5_fwd_pass(u_perm, mats, zeros_state, False)
    h0 = _segment_starts(h_ends, mats["pow_r"], mats["pow_i"], False)
    y_perm, h_init, _ = _s5_fwd_pass(u_perm, mats, h0, True)
    ys = _from_segments(y_perm)
    y_hg, y_s5, glu, h1 = _stage_branches(o, proj, ys, xs, hg_norm_g, wf_o_hg, wf_glu, bias_glu, wf_o_s5, wf_out)

    dh1, (loss_acc, d_final_g, d_ple_g, d_w_ple, d_w_pg) = _stage_ple_loss(
        h1, ps, tgt, ple_norm_g, wf_ple, wf_pg, final_norm_g.reshape(1, D_MODEL))
    (d_o, d_g_hg, d_gate_hg, d_gate_s5, d_ys5), (d_w_out, d_w_o_hg, d_hg_norm) = _stage_bwd_merge_hg(
        dh1, y_hg, y_s5, proj, o, hg_norm_g, wf_out, wf_o_hg)
    (d_ys, d_z), (d_w_o_s5, d_w_glu, d_b_glu) = _stage_bwd_s5_path(d_ys5, ys, glu, proj, wf_o_s5, wf_glu)
    dq, df, div, d_lb = _hgrn_bwd(proj, lb, d_o, states)
    dy_perm = _to_segments(d_ys)
    g_ends = _s5_bwd_ends(dy_perm, mats)
    g0 = _segment_starts(g_ends, mats["pow_r"], mats["pow_i"], True)
    du_perm, acc_p, acc_q, d_bq, d_cq_t, d_d = _s5_bwd_full(u_perm, dy_perm, h_init, g0, mats)
    d_us = _from_segments(du_perm)
    grad_x, dproj, d_norm_g = _stage_inproj_bwd([dq, df, div, d_g_hg, d_us, d_z, d_gate_hg, d_gate_s5], xs, dh1,
                                                norm_g, all_b)
    d_w_in_blocks = _grad_w_in(dproj, u)

    p_re, p_im = _split_cols(acc_p)
    q_re, q_im = _split_cols(acc_q)
    d_lam_r = (p_re + p_im)[0]
    d_lam_i = (q_im - q_re)[0]
    zero_row = jnp.zeros((S5_SEG, S5_COLS), F32)
    row_of = lambda re_part, im_part: zero_row.at[0].set(_join_cols(re_part[None], im_part[None])[0])
    zeros_q = jnp.zeros_like(d_lam_r)
    cot = dict(
        lam_r=row_of(d_lam_r, zeros_q), lam_i=row_of(zeros_q, d_lam_i), lam_ic=zero_row,
        b_q=d_bq, c_q=d_cq_t.transpose(0, 2, 1), d_row=d_d,
        pow_r=jnp.zeros_like(mats["pow_r"]), pow_i=jnp.zeros_like(mats["pow_i"]),
    )
    d_s5 = mats_vjp(cot)

    s_lb = lb * (1.0 - lb)
    d_hg_lb = jnp.concatenate([d_lb * s_lb, -d_lb * s_lb], axis=0)
    small_g = dict(norm_g=d_norm_g, hg_lb=d_hg_lb, hg_norm_g=d_hg_norm, b_glu=d_b_glu, ple_norm_g=d_ple_g,
                   final_norm_g=d_final_g)
    for name, g in zip(s5_names, d_s5):
        small_g[name] = g
    pk = lambda d: _pack_small({n: d[n] for n in _SMALL})
    sg, sd, sm, sv = _small_allreduce_adam(pk(small_g), pk(w), pk(m), pk(v))
    like = {n: w[n] for n in _SMALL}
    out_g, out_d, out_m, out_v = (_unpack_small(t, like) for t in (sg, sd, sm, sv))

    rs_a = jnp.concatenate([d_w_o_hg.reshape(N_DEV, 128, D_MODEL), d_w_out.reshape(N_DEV, 128, D_MODEL),
                            d_w_pg.reshape(N_DEV, 128, D_MODEL)], axis=1).astype(BF16)
    rs_c = jnp.concatenate([_col_blocks(d_w_glu), _col_blocks(d_w_o_s5), _col_blocks(d_w_ple)], axis=1).astype(BF16)
    g_a, g_b, g_c = _reduce_scatter([rs_a, d_w_in_blocks, rs_c])
    big_g = dict(w_o_hg=g_a[0:128], w_out=g_a[128:256], w_ple_gate=g_a[256:384], w_in=g_b,
                 w_glu=g_c[0:512], w_o_s5=g_c[512:1024], w_ple=g_c[1024:1280])
    for name in _BIG:
        shape = w[name].shape
        g2 = big_g[name]
        d2, m2, v2 = _adam_rows(g2, w[name][0], m[name][0], v[name][0])
        out_g[name], out_d[name], out_m[name], out_v[name] = (t.reshape(shape) for t in (g2, d2, m2, v2))

    loss = lax.psum(loss_acc[0, 0], ("x", "y", "c"))
    return (loss, grad_x[None], *[out_g[n] for n in _ORDER], *[out_d[n] for n in _ORDER],
            *[out_m[n] for n in _ORDER], *[out_v[n] for n in _ORDER])
```

```python
import functools
import math

import jax
import jax.numpy as jnp
from jax import lax
from jax.experimental import pallas as pl
from jax.experimental.pallas import tpu as pltpu

F32 = jnp.float32
BF16 = jnp.bfloat16

D_MODEL = 1024
N_DEV = 8
IN_COLS = 7168
SHARD_IN = IN_COLS // N_DEV
HG_HEADS = 8
HG_DIM = 128
HG_CHUNK = 64
HG_SUPER = 256
HG_HEADS_PER_STEP = 4
S5_WIDTH = 512
S5_GROUPS = 32
S5_STATE = 64
S5_CH = 16
S5_SEG = 8
S5_QUADS = 4
S5_QCOLS = 1024
S5_COLS = S5_QUADS * S5_QCOLS
S5_TILE_STEPS = 64
NORM_EPS = 1e-6
ADAM_LR = 0.001
ADAM_B1 = 0.9
ADAM_B2 = 0.999
ADAM_EPS = 1e-08
ADAM_WD = 0.01
ADAM_STEP = 10
MIB = 1024 * 1024
MESH = pl.DeviceIdType.MESH

COL_Q, COL_F, COL_I, COL_G, COL_US, COL_ZS, COL_GH, COL_GS = 0, 1024, 2048, 3072, 4096, 4608, 5120, 6144


def _call(body, **kw):
    return pl.pallas_call(body, **kw)


def _params(vmem_mb, n_grid=1):
    return pltpu.CompilerParams(
        dimension_semantics=("arbitrary",) * n_grid, vmem_limit_bytes=vmem_mb * MIB
    )


def _bdot(a, b):
    return jnp.dot(a.astype(BF16), b.astype(BF16), preferred_element_type=F32)


def _bdot_nt(a, b):
    return lax.dot_general(a.astype(BF16), b.astype(BF16), (((1,), (1,)), ((), ())), preferred_element_type=F32)


def _bdot_tn(a, b):
    return lax.dot_general(a.astype(BF16), b.astype(BF16), (((0,), (0,)), ((), ())), preferred_element_type=F32)


def _sigmoid(x):
    return jax.nn.sigmoid(x)


def _silu(x):
    return x * _sigmoid(x)


def _dsilu(x):
    s = _sigmoid(x)
    return s * (1.0 + x * (1.0 - s))


_GELU_C = math.sqrt(2.0 / math.pi)


def _gelu(x):
    return 0.5 * x * (1.0 + jnp.tanh(_GELU_C * (x + 0.044715 * x * x * x)))


def _dgelu(x):
    t = jnp.tanh(_GELU_C * (x + 0.044715 * x * x * x))
    return 0.5 * (1.0 + t) + 0.5 * x * (1.0 - t * t) * _GELU_C * (1.0 + 3.0 * 0.044715 * x * x)


def _rms_fwd(x, g):
    r = lax.rsqrt(jnp.mean(x * x, axis=-1, keepdims=True) + NORM_EPS)
    n = x * r
    return n * g, n, r


def _rms_bwd(dy, n, r, g):
    dn = dy * g
    dx = r * (dn - n * jnp.mean(dn * n, axis=-1, keepdims=True))
    return dx, jnp.sum(dy * n, axis=0, keepdims=True)


def _head_rms_fwd(o, g):
    ns, rs = [], []
    for h in range(HG_HEADS):
        oh = o[:, h * HG_DIM:(h + 1) * HG_DIM]
        r = lax.rsqrt(jnp.mean(oh * oh, axis=-1, keepdims=True) + NORM_EPS)
        ns.append(oh * r)
        rs.append(r)
    n = jnp.concatenate(ns, axis=1)
    return n * g, n, rs


def _head_rms_bwd(dy, n, rs, g):
    dn = dy * g
    dxs = []
    for h in range(HG_HEADS):
        sl = slice(h * HG_DIM, (h + 1) * HG_DIM)
        dxs.append(rs[h] * (dn[:, sl] - n[:, sl] * jnp.mean(dn[:, sl] * n[:, sl], axis=-1, keepdims=True)))
    return jnp.concatenate(dxs, axis=1), jnp.sum(dy * n, axis=0, keepdims=True)


def _rowwise(name, fn, n_rows, tm, rows, consts, out_rows, out_accs, vmem_mb):
    n_r, n_c, n_or, n_oa = len(rows), len(consts), len(out_rows), len(out_accs)

    def body(*refs):
        r_refs = refs[:n_r]
        c_refs = refs[n_r:n_r + n_c]
        or_refs = refs[n_r + n_c:n_r + n_c + n_or]
        oa_refs = refs[n_r + n_c + n_or:]
        outs, accs = fn([r[...] for r in r_refs], c_refs, or_refs)

        if n_oa:
            @pl.when(pl.program_id(0) == 0)
            def _():
                for ref in oa_refs:
                    ref[...] = jnp.zeros(ref.shape, ref.dtype)

        for ref, v in zip(or_refs, outs):
            if v is not None:
                ref[...] = v.astype(ref.dtype)
        for ref, v in zip(oa_refs, accs):
            ref[...] += v.astype(ref.dtype)

    in_specs = [pl.BlockSpec((tm, w), functools.partial(lambda i, c: (i, c), c=cb)) for (_, w, cb) in rows]
    in_specs += [pl.BlockSpec(c.shape, functools.partial(lambda i, nd: (0,) * nd, nd=c.ndim),
                              pipeline_mode=pl.Buffered(1)) for c in consts]
    out_specs = [pl.BlockSpec((tm, w), lambda i: (i, 0)) for (w, _) in out_rows]
    out_specs += [pl.BlockSpec(s, functools.partial(lambda i, nd: (0,) * nd, nd=len(s))) for (s, _) in out_accs]
    out_shape = [jax.ShapeDtypeStruct((n_rows, w), dt) for (w, dt) in out_rows]
    out_shape += [jax.ShapeDtypeStruct(s, dt) for (s, dt) in out_accs]
    res = _call(
        body, name=name, grid=(n_rows // tm,), in_specs=in_specs, out_specs=out_specs, out_shape=out_shape,
        compiler_params=_params(vmem_mb),
    )(*[a for (a, _, _) in rows], *consts)
    return res[:n_or], res[n_or:]


def _stage_branches(o, proj, ys, x, hg_norm_g, w_o_hg, w_glu, b_glu, w_o_s5, w_out):
    S = x.shape[0]

    def fn(rv, cr, out):
        o_b, g_hg, z_s, gate_hg, gate_s5, ys_b, x_b = rv
        gn_ref, wohg_ref, wglu_ref, bglu_ref, wos5_ref, wout_ref = cr
        on, _, _ = _head_rms_fwd(o_b, gn_ref[...])
        a = on * _silu(g_hg)
        y_hg = jnp.dot(a.astype(BF16), wohg_ref[...], preferred_element_type=F32)
        gl = _gelu(ys_b)
        glu = jnp.dot(gl.astype(BF16), wglu_ref[...], preferred_element_type=F32) + bglu_ref[...]
        ys2 = glu[:, :S5_WIDTH] * _sigmoid(glu[:, S5_WIDTH:]) * _silu(z_s)
        y_s5 = jnp.dot(ys2.astype(BF16), wos5_ref[...], preferred_element_type=F32)
        merged = _sigmoid(gate_hg) * y_hg + _sigmoid(gate_s5) * y_s5
        h1 = x_b + jnp.dot(merged.astype(BF16), wout_ref[...], preferred_element_type=F32)
        return [y_hg, y_s5, glu, h1], []

    rows = [(o, D_MODEL, 0), (proj, D_MODEL, COL_G // D_MODEL), (proj, S5_WIDTH, COL_ZS // S5_WIDTH),
            (proj, D_MODEL, COL_GH // D_MODEL), (proj, D_MODEL, COL_GS // D_MODEL), (ys, S5_WIDTH, 0), (x, D_MODEL, 0)]
    (y_hg, y_s5, glu, h1), _ = _rowwise(
        "branches", fn, S, 256, rows, [hg_norm_g, w_o_hg, w_glu, b_glu, w_o_s5, w_out],
        [(D_MODEL, F32)] * 4, [], 56)
    return y_hg, y_s5, glu, h1


def _stage_ple_loss(h1, p, target, ple_norm_g, w_ple, w_ple_gate, final_norm_g):
    S = h1.shape[0]

    def fn(rv, cr, out):
        h1_b, p_b, t_b = rv
        gp_ref, wple_ref, wpg_ref, gf_ref = cr
        n2g, n2, r2 = _rms_fwd(h1_b, gp_ref[...])
        z = jnp.dot(n2g.astype(BF16), wpg_ref[...], preferred_element_type=F32)
        gate = _sigmoid(z)
        pe = jnp.dot(p_b.astype(BF16), wple_ref[...], preferred_element_type=F32)
        h2 = h1_b + pe * gate
        y, nf, rf = _rms_fwd(h2, gf_ref[...])
        err = y - t_b
        loss_rows = 0.5 * jnp.mean(err * err, axis=-1, keepdims=True)
        loss_inc = jnp.broadcast_to(jnp.sum(loss_rows, axis=0, keepdims=True), (1, 128))
        dy = err * (1.0 / D_MODEL)
        dh2, d_gf = _rms_bwd(dy, nf, rf, gf_ref[...])
        d_pe = dh2 * gate
        dz = dh2 * pe * gate * (1.0 - gate)
        d_wple = _bdot_tn(p_b, d_pe)
        d_wpg = _bdot_tn(n2g, dz)
        dn2g = _bdot_nt(dz, wpg_ref[...])
        dh1n, d_gp = _rms_bwd(dn2g, n2, r2, gp_ref[...])
        return [dh2 + dh1n], [loss_inc, d_gf, d_gp, d_wple, d_wpg]

    (dh1,), accs = _rowwise(
        "ple_loss", fn, S, 256, [(h1, D_MODEL, 0), (p, 256, 0), (target, D_MODEL, 0)],
        [ple_norm_g, w_ple, w_ple_gate, final_norm_g], [(D_MODEL, F32)],
        [((1, 128), F32), ((1, D_MODEL), F32), ((1, D_MODEL), F32), ((256, D_MODEL), F32), ((D_MODEL, D_MODEL), F32)], 56)
    return dh1, accs


def _stage_bwd_merge_hg(dh1, y_hg, y_s5, proj, o, hg_norm_g, w_out, w_o_hg):
    S = dh1.shape[0]

    def fn(rv, cr, out):
        dh1_b, yhg, ys5, gate_hg, gate_s5, o_b, g_hg = rv
        gn_ref, wout_ref, wohg_ref = cr
        sg_h, sg_s = _sigmoid(gate_hg), _sigmoid(gate_s5)
        merged = sg_h * yhg + sg_s * ys5
        d_wout = _bdot_tn(merged, dh1_b)
        d_merged = _bdot_nt(dh1_b, wout_ref[...])
        d_gate_hg = d_merged * yhg * sg_h * (1.0 - sg_h)
        d_gate_s5 = d_merged * ys5 * sg_s * (1.0 - sg_s)
        d_yhg = d_merged * sg_h
        d_ys5 = d_merged * sg_s
        ong, on, rs = _head_rms_fwd(o_b, gn_ref[...])
        sil = _silu(g_hg)
        a = ong * sil
        d_wohg = _bdot_tn(a, d_yhg)
        d_a = _bdot_nt(d_yhg, wohg_ref[...])
        d_g_hg = d_a * ong * _dsilu(g_hg)
        d_o, d_gn = _head_rms_bwd(d_a * sil, on, rs, gn_ref[...])
        return [d_o, d_g_hg, d_gate_hg, d_gate_s5, d_ys5], [d_wout, d_wohg, d_gn]

    rows = [(dh1, D_MODEL, 0), (y_hg, D_MODEL, 0), (y_s5, D_MODEL, 0), (proj, D_MODEL, COL_GH // D_MODEL),
            (proj, D_MODEL, COL_GS // D_MODEL), (o, D_MODEL, 0), (proj, D_MODEL, COL_G // D_MODEL)]
    outs, accs = _rowwise(
        "bwd_merge_hg", fn, S, 256, rows, [hg_norm_g, w_out, w_o_hg], [(D_MODEL, F32)] * 5,
        [((D_MODEL, D_MODEL), F32), ((D_MODEL, D_MODEL), F32), ((1, D_MODEL), F32)], 56)
    return outs, accs


def _stage_bwd_s5_path(d_ys5, ys, glu, proj, w_o_s5, w_glu):
    S = ys.shape[0]

    def fn(rv, cr, out):
        d_ys5_b, ys_b, glu_b, z_s = rv
        wos5_ref, wglu_ref = cr
        ga, gb = glu_b[:, :S5_WIDTH], glu_b[:, S5_WIDTH:]
        sgb, silz = _sigmoid(gb), _silu(z_s)
        ys2 = ga * sgb * silz
        d_wos5 = _bdot_tn(ys2, d_ys5_b)
        d_ys2 = _bdot_nt(d_ys5_b, wos5_ref[...])
        d_ga = d_ys2 * sgb * silz
        d_gb = d_ys2 * ga * sgb * (1.0 - sgb) * silz
        d_z = d_ys2 * ga * sgb * _dsilu(z_s)
        d_glu = jnp.concatenate([d_ga, d_gb], axis=1)
        gl = _gelu(ys_b)
        d_wglu = _bdot_tn(gl, d_glu)
        d_bglu = jnp.sum(d_glu, axis=0, keepdims=True)
        d_gl = _bdot_nt(d_glu, wglu_ref[...])
        return [d_gl * _dgelu(ys_b), d_z], [d_wos5, d_wglu, d_bglu]

    rows = [(d_ys5, D_MODEL, 0), (ys, S5_WIDTH, 0), (glu, D_MODEL, 0), (proj, S5_WIDTH, COL_ZS // S5_WIDTH)]
    outs, accs = _rowwise(
        "bwd_s5_path", fn, S, 256, rows, [w_o_s5, w_glu], [(S5_WIDTH, F32), (S5_WIDTH, F32)],
        [((S5_WIDTH, D_MODEL), F32), ((S5_WIDTH, D_MODEL), F32), ((1, D_MODEL), F32)], 48)
    return outs, accs


def _stage_inproj_bwd(pieces, x, dh1, norm_g, w_in_all):
    S = x.shape[0]

    def fn(rv, cr, out):
        g_ref, w_ref = cr
        x_b, dh1_b = rv[8], rv[9]
        dproj_ref = out[1]
        col = 0
        for v in rv[:8]:
            dproj_ref[:, col:col + v.shape[1]] = v.astype(BF16)
            col += v.shape[1]
        d_u = jnp.zeros((x_b.shape[0], D_MODEL), F32)
        for j in range(N_DEV):
            d_u = d_u + lax.dot_general(dproj_ref[:, j * SHARD_IN:(j + 1) * SHARD_IN], w_ref[j],
                                        (((1,), (1,)), ((), ())), preferred_element_type=F32)
        _, n, r = _rms_fwd(x_b, g_ref[...])
        dx, d_g = _rms_bwd(d_u, n, r, g_ref[...])
        return [dh1_b + dx, None], [d_g]

    rows = [(a, a.shape[1], 0) for a in pieces] + [(x, D_MODEL, 0), (dh1, D_MODEL, 0)]
    (grad_x, dproj), (d_g,) = _rowwise(
        "inproj_bwd", fn, S, 256, rows, [norm_g, w_in_all], [(D_MODEL, F32), (IN_COLS, BF16)],
        [((1, D_MODEL), F32)], 56)
    return grad_x, dproj, d_g


def _chunk_row(shape):
    return lax.broadcasted_iota(jnp.int32, shape, 0) & (HG_CHUNK - 1)


def _chunk_cumsum(x):
    r_in = _chunk_row(x.shape)
    s = 1
    while s < HG_CHUNK:
        x = x + jnp.where(r_in >= s, pltpu.roll(x, s, 0), 0.0)
        s *= 2
    return x


def _chunk_suffix_sum(x):
    n = x.shape[0]
    r_in = _chunk_row(x.shape)
    s = 1
    while s < HG_CHUNK:
        x = x + jnp.where(r_in < HG_CHUNK - s, pltpu.roll(x, n - s, 0), 0.0)
        s *= 2
    return x


def _hgrn_prep(q, fl, lb):
    nc = HG_SUPER // HG_CHUNK
    sig = _sigmoid(fl)
    f = lb + (1.0 - lb) * sig
    k = (1.0 - lb) * (1.0 - sig)
    b = _chunk_cumsum(jnp.log(f))
    b3 = b.reshape(nc, HG_CHUNK, HG_DIM)
    row3 = lax.broadcasted_iota(jnp.int32, b3.shape, 1)
    pick = lambda r: jnp.sum(jnp.where(row3 == r, b3, 0.0), axis=1, keepdims=True)
    b_mid = pick(HG_CHUNK // 2 - 1)
    b_last = pick(HG_CHUNK - 1)
    flat = lambda t: t.reshape(HG_SUPER, HG_DIM)
    e_qa = flat(jnp.exp(b3 - b_mid))
    e_ka = flat(jnp.exp(b_mid - b3))
    e_qd = jnp.exp(b)
    e_kd = flat(jnp.exp(b_last - b3))
    dc = jnp.exp(b_last)
    return sig, f, k, e_qa, e_ka, e_qd, e_kd, dc


def _hgrn_mask():
    r = lax.broadcasted_iota(jnp.int32, (HG_SUPER, HG_SUPER), 0)
    c = lax.broadcasted_iota(jnp.int32, (HG_SUPER, HG_SUPER), 1)
    shift = HG_CHUNK.bit_length() - 1
    return (jnp.right_shift(r, shift) == jnp.right_shift(c, shift)) & (r >= c)


def _hgrn_fwd(proj, lb):
    S = proj.shape[0]
    nb = S // HG_SUPER
    nc = HG_SUPER // HG_CHUNK
    hp = HG_HEADS_PER_STEP
    wide = hp * HG_DIM

    def body(q_ref, f_ref, iv_ref, lb_ref, o_ref, st_ref, state):
        @pl.when(pl.program_id(1) == 0)
        def _():
            state[...] = jnp.zeros(state.shape, F32)

        mask = _hgrn_mask()
        for hh in range(hp):
            lanes = slice(hh * HG_DIM, (hh + 1) * HG_DIM)
            q, iv = q_ref[:, lanes], iv_ref[:, lanes]
            _, _, k, e_qa, e_ka, e_qd, e_kd, dc = _hgrn_prep(q, f_ref[:, lanes], lb_ref[:, lanes])
            scores = jnp.where(mask, _bdot_nt(q * e_qa, k * e_ka), 0.0)
            o_intra = _bdot(scores, iv)
            qd, kd = q * e_qd, k * e_kd
            for c in range(nc):
                sl = slice(c * HG_CHUNK, (c + 1) * HG_CHUNK)
                st = state[hh]
                st_ref[hh, c] = st
                o_ref[sl, lanes] = o_intra[sl] + _bdot_nt(qd[sl], st)
                state[hh] = dc[c] * st + _bdot_tn(iv[sl], kd[sl])

    blk = lambda base: pl.BlockSpec((HG_SUPER, wide), functools.partial(lambda h, i, b: (i, b + h), b=base // wide))
    return _call(
        body, name="hgrn_fwd", grid=(HG_HEADS // hp, nb),
        in_specs=[blk(COL_Q), blk(COL_F), blk(COL_I), pl.BlockSpec((1, wide), lambda h, i: (0, h))],
        out_specs=[pl.BlockSpec((HG_SUPER, wide), lambda h, i: (i, h)),
                   pl.BlockSpec((hp, nc, HG_DIM, HG_DIM), lambda h, i: (h, i, 0, 0))],
        out_shape=[jax.ShapeDtypeStruct((S, D_MODEL), F32),
                   jax.ShapeDtypeStruct((HG_HEADS, S // HG_CHUNK, HG_DIM, HG_DIM), F32)],
        scratch_shapes=[pltpu.VMEM((hp, HG_DIM, HG_DIM), F32)],
        compiler_params=_params(40, 2),
    )(proj, proj, proj, lb)


def _hgrn_bwd(proj, lb, d_o, states):
    S = proj.shape[0]
    nb = S // HG_SUPER
    nc = HG_SUPER // HG_CHUNK
    hp = HG_HEADS_PER_STEP
    wide = hp * HG_DIM

    def body(q_ref, f_ref, iv_ref, lb_ref, do_ref, st_ref, dq_ref, df_ref, div_ref, dlb_ref, dstate):
        @pl.when(pl.program_id(1) == 0)
        def _():
            dstate[...] = jnp.zeros(dstate.shape, F32)
            dlb_ref[...] = jnp.zeros(dlb_ref.shape, F32)

        mask = _hgrn_mask()
        for hh in range(hp):
            lanes = slice(hh * HG_DIM, (hh + 1) * HG_DIM)
            q, iv, do, lb_v = q_ref[:, lanes], iv_ref[:, lanes], do_ref[:, lanes], lb_ref[:, lanes]
            sig, f, k, e_qa, e_ka, e_qd, e_kd, dc = _hgrn_prep(q, f_ref[:, lanes], lb_v)
            qa, ka, qd, kd = q * e_qa, k * e_ka, q * e_qd, k * e_kd
            scores = jnp.where(mask, _bdot_nt(qa, ka), 0.0)
            d_scores = jnp.where(mask, _bdot_nt(do, iv), 0.0)
            d_iv_intra = _bdot_tn(scores, do)
            d_qa = _bdot(d_scores, ka)
            d_ka = _bdot_tn(d_scores, qa)
            d_qd, d_kd, d_last = [None] * nc, [None] * nc, [None] * nc
            for c in reversed(range(nc)):
                sl = slice(c * HG_CHUNK, (c + 1) * HG_CHUNK)
                st = st_ref[hh, c]
                ds = dstate[hh]
                d_qd[c] = _bdot(do[sl], st)
                d_kd[c] = _bdot(iv[sl], ds)
                div_ref[sl, lanes] = d_iv_intra[sl] + _bdot_nt(kd[sl], ds)
                d_last[c] = (jnp.sum(ds * st, axis=0, keepdims=True) * dc[c]
                             + jnp.sum(d_kd[c] * kd[sl], axis=0, keepdims=True))
                dstate[hh] = dc[c] * ds + _bdot_tn(do[sl], qd[sl])
            d_qd = jnp.concatenate(d_qd, axis=0)
            d_kd = jnp.concatenate(d_kd, axis=0)
            d_b = d_qa * qa - d_ka * ka + d_qd * qd - d_kd * kd
            last_rows = jnp.concatenate([jnp.broadcast_to(t, (HG_CHUNK, HG_DIM)) for t in d_last], axis=0)
            d_b = d_b + jnp.where(_chunk_row(d_b.shape) == HG_CHUNK - 1, last_rows, 0.0)
            d_logf = _chunk_suffix_sum(d_b)
            d_k = d_ka * e_ka + d_kd * e_kd
            g_f = d_logf / f
            d_sig = (g_f - d_k) * (1.0 - lb_v)
            dq_ref[:, lanes] = d_qa * e_qa + d_qd * e_qd
            df_ref[:, lanes] = d_sig * sig * (1.0 - sig)
            d_lb = jnp.sum((g_f - d_k) * (1.0 - sig), axis=0, keepdims=True)
            dlb_ref[:, lanes] += jnp.broadcast_to(d_lb, (8, HG_DIM))

    rev = lambda i: nb - 1 - i
    blk = lambda base: pl.BlockSpec((HG_SUPER, wide), functools.partial(lambda h, i, b: (rev(i), b + h), b=base // wide))
    row_out = pl.BlockSpec((HG_SUPER, wide), lambda h, i: (rev(i), h))
    dq, df, div, dlb = _call(
        body, name="hgrn_bwd", grid=(HG_HEADS // hp, nb),
        in_specs=[blk(COL_Q), blk(COL_F), blk(COL_I), pl.BlockSpec((1, wide), lambda h, i: (0, h)),
                  pl.BlockSpec((HG_SUPER, wide), lambda h, i: (rev(i), h)),
                  pl.BlockSpec((hp, nc, HG_DIM, HG_DIM), lambda h, i: (h, rev(i), 0, 0))],
        out_specs=[row_out, row_out, row_out, pl.BlockSpec((8, wide), lambda h, i: (0, h))],
        out_shape=[jax.ShapeDtypeStruct((S, D_MODEL), F32)] * 3 + [jax.ShapeDtypeStruct((8, D_MODEL), F32)],
        scratch_shapes=[pltpu.VMEM((hp, HG_DIM, HG_DIM), F32)],
        compiler_params=_params(40, 2),
    )(proj, proj, proj, lb, d_o, states)
    return dq, df, div, dlb[0:1]


def _s5_matrices(a_re, a_im, log_dt, b_re, b_im, c_re, c_im, d, seg_len):
    dt = jnp.exp(log_dt)[:, None]
    mag = jnp.exp(a_re * dt)
    lr, li = mag * jnp.cos(a_im * dt), mag * jnp.sin(a_im * dt)
    den = a_re * a_re + a_im * a_im
    nr = lr - 1.0
    sr = (nr * a_re + li * a_im) / den
    si = (li * a_re - nr * a_im) / den
    bbr = sr[..., None] * b_re - si[..., None] * b_im
    bbi = sr[..., None] * b_im + si[..., None] * b_re
    eye = jnp.eye(8, dtype=F32)

    def quad_cols(v):
        return v.reshape(S5_QUADS, 8 * S5_STATE)

    def lam_row(re_part, im_part):
        row = jnp.concatenate([quad_cols(re_part), quad_cols(im_part)], axis=1).reshape(1, S5_COLS)
        return jnp.broadcast_to(row, (S5_SEG, S5_COLS))

    def b_mat(bb):
        t = bb.reshape(S5_QUADS, 8, S5_STATE, S5_CH)
        return jnp.einsum("qgnc,gh->qgchn", t, eye).reshape(S5_QUADS, 8 * S5_CH, 8 * S5_STATE)

    def c_mat(cc):
        t = cc.reshape(S5_QUADS, 8, S5_CH, S5_STATE)
        return jnp.einsum("qgcn,gh->qgnhc", t, eye).reshape(S5_QUADS, 8 * S5_STATE, 8 * S5_CH)

    ang = a_im * dt * seg_len
    magp = jnp.exp(a_re * dt * seg_len)
    lpr, lpi = magp * jnp.cos(ang), magp * jnp.sin(ang)
    return dict(
        lam_r=lam_row(lr, lr), lam_i=lam_row(-li, li), lam_ic=lam_row(li, -li),
        b_q=jnp.concatenate([b_mat(bbr), b_mat(bbi)], axis=2),
        c_q=jnp.concatenate([c_mat(c_re), -c_mat(c_im)], axis=1),
        d_row=d.reshape(1, S5_WIDTH), pow_r=quad_cols(lpr), pow_i=quad_cols(lpi),
    )


def _swap_halves(h):
    half = S5_QCOLS // 2
    parts = []
    for q in range(S5_QUADS):
        parts.append(h[:, q * S5_QCOLS + half:(q + 1) * S5_QCOLS])
        parts.append(h[:, q * S5_QCOLS:q * S5_QCOLS + half])
    return jnp.concatenate(parts, axis=1)


def _s5_fwd_pass(u_perm, mats, h0, with_output):
    S = u_perm.shape[0]
    rows = S5_TILE_STEPS * S5_SEG
    nt = S // rows

    def body(*refs):
        if with_output:
            u_ref, b_ref, lr_ref, li_ref, h0_ref, c_ref, d_ref, y_ref, hinit_ref, hend_ref, xs, hcar = refs
        else:
            u_ref, b_ref, lr_ref, li_ref, h0_ref, hend_ref, xs, hcar = refs

        @pl.when(pl.program_id(0) == 0)
        def _():
            hcar[...] = h0_ref[...]

        if with_output:
            hinit_ref[...] = hcar[...]
        u = u_ref[...]
        ub = u.astype(BF16)
        for q in range(S5_QUADS):
            xs[:, q * S5_QCOLS:(q + 1) * S5_QCOLS] = jnp.dot(ub[:, q * 128:(q + 1) * 128], b_ref[q], preferred_element_type=F32)

        def step(t, h):
            sl = pl.ds(pl.multiple_of(t * S5_SEG, S5_SEG), S5_SEG)
            hn = lr_ref[...] * h + li_ref[...] * _swap_halves(h) + xs[sl, :]
            xs[sl, :] = hn
            return hn

        h = lax.fori_loop(0, S5_TILE_STEPS, step, hcar[...])
        hcar[...] = h
        hend_ref[...] = h
        if with_output:
            ys = [jnp.dot(xs[:, q * S5_QCOLS:(q + 1) * S5_QCOLS].astype(BF16), c_ref[q], preferred_element_type=F32)
                  for q in range(S5_QUADS)]
            y_ref[...] = jnp.concatenate(ys, axis=1) + d_ref[...] * u

    full = lambda a: pl.BlockSpec(a.shape, functools.partial(lambda i, nd: (0,) * nd, nd=a.ndim))
    ins = [u_perm, mats["b_q"], mats["lam_r"], mats["lam_i"], h0]
    in_specs = [pl.BlockSpec((rows, S5_WIDTH), lambda i: (i, 0))] + [full(a) for a in ins[1:]]
    out_specs = [pl.BlockSpec((S5_SEG, S5_COLS), lambda i: (0, 0))]
    out_shape = [jax.ShapeDtypeStruct((S5_SEG, S5_COLS), F32)]
    if with_output:
        ins += [mats["c_q"], mats["d_row"]]
        in_specs += [full(mats["c_q"]), full(mats["d_row"])]
        out_specs = [pl.BlockSpec((rows, S5_WIDTH), lambda i: (i, 0)),
                     pl.BlockSpec((None, S5_SEG, S5_COLS), lambda i: (i, 0, 0))] + out_specs
        out_shape = [jax.ShapeDtypeStruct((S, S5_WIDTH), F32), jax.ShapeDtypeStruct((nt, S5_SEG, S5_COLS), F32)] + out_shape
    return _call(
        body, name="s5_fwd_y" if with_output else "s5_fwd_ends", grid=(nt,), in_specs=in_specs, out_specs=out_specs,
        out_shape=out_shape,
        scratch_shapes=[pltpu.VMEM((rows, S5_COLS), F32), pltpu.VMEM((S5_SEG, S5_COLS), F32)],
        compiler_params=_params(40),
    )(*ins)


def _s5_bwd_ends(dy_perm, mats):
    S = dy_perm.shape[0]
    rows = S5_TILE_STEPS * S5_SEG
    nt = S // rows

    def body(dy_ref, c_ref, lr_ref, lic_ref, gend_ref, gs, gcar):
        @pl.when(pl.program_id(0) == 0)
        def _():
            gcar[...] = jnp.zeros(gcar.shape, F32)

        dyb = dy_ref[...].astype(BF16)
        for q in range(S5_QUADS):
            gs[:, q * S5_QCOLS:(q + 1) * S5_QCOLS] = lax.dot_general(
                dyb[:, q * 128:(q + 1) * 128], c_ref[q], (((1,), (1,)), ((), ())), preferred_element_type=F32)

        def step(k, g):
            t = S5_TILE_STEPS - 1 - k
            sl = pl.ds(pl.multiple_of(t * S5_SEG, S5_SEG), S5_SEG)
            return lr_ref[...] * g + lic_ref[...] * _swap_halves(g) + gs[sl, :]

        g = lax.fori_loop(0, S5_TILE_STEPS, step, gcar[...])
        gcar[...] = g
        gend_ref[...] = g

    full = lambda a: pl.BlockSpec(a.shape, functools.partial(lambda i, nd: (0,) * nd, nd=a.ndim))
    return _call(
        body, name="s5_bwd_ends", grid=(nt,),
        in_specs=[pl.BlockSpec((rows, S5_WIDTH), lambda i: (nt - 1 - i, 0)), full(mats["c_q"]), full(mats["lam_r"]),
                  full(mats["lam_ic"])],
        out_specs=pl.BlockSpec((S5_SEG, S5_COLS), lambda i: (0, 0)),
        out_shape=jax.ShapeDtypeStruct((S5_SEG, S5_COLS), F32),
        scratch_shapes=[pltpu.VMEM((rows, S5_COLS), F32), pltpu.VMEM((S5_SEG, S5_COLS), F32)],
        compiler_params=_params(40),
    )(dy_perm, mats["c_q"], mats["lam_r"], mats["lam_ic"])


def _s5_bwd_full(u_perm, dy_perm, hinit, g0, mats):
    S = u_perm.shape[0]
    rows = S5_TILE_STEPS * S5_SEG
    nt = S // rows

    def body(u_ref, dy_ref, hinit_ref, g0_ref, b_ref, c_ref, lr_ref, li_ref, lic_ref, d_ref,
             du_ref, dp_ref, dq_ref, db_ref, dc_ref, dd_ref, hs, gs, gcar):
        @pl.when(pl.program_id(0) == 0)
        def _():
            gcar[...] = g0_ref[...]
            for ref in (dp_ref, dq_ref, db_ref, dc_ref, dd_ref):
                ref[...] = jnp.zeros(ref.shape, F32)

        u, dy = u_ref[...], dy_ref[...]
        ub, dyb = u.astype(BF16), dy.astype(BF16)
        hs[0:S5_SEG, :] = hinit_ref[...]
        for q in range(S5_QUADS):
            cols = slice(q * S5_QCOLS, (q + 1) * S5_QCOLS)
            hs[S5_SEG:, cols] = jnp.dot(ub[:, q * 128:(q + 1) * 128], b_ref[q], preferred_element_type=F32)
            gs[:, cols] = lax.dot_general(dyb[:, q * 128:(q + 1) * 128], c_ref[q], (((1,), (1,)), ((), ())),
                                          preferred_element_type=F32)

        def fstep(t, h):
            sl = pl.ds(pl.multiple_of((t + 1) * S5_SEG, S5_SEG), S5_SEG)
            hn = lr_ref[...] * h + li_ref[...] * _swap_halves(h) + hs[sl, :]
            hs[sl, :] = hn
            return hn

        lax.fori_loop(0, S5_TILE_STEPS, fstep, hinit_ref[...])

        def bstep(k, g):
            t = S5_TILE_STEPS - 1 - k
            sl = pl.ds(pl.multiple_of(t * S5_SEG, S5_SEG), S5_SEG)
            gn = lr_ref[...] * g + lic_ref[...] * _swap_halves(g) + gs[sl, :]
            gs[sl, :] = gn
            return gn

        gcar[...] = lax.fori_loop(0, S5_TILE_STEPS, bstep, gcar[...])

        half = S5_QCOLS // 2
        dus = []
        for q in range(S5_QUADS):
            cols = slice(q * S5_QCOLS, (q + 1) * S5_QCOLS)

            def astep(t, carry, q=q):
                sl = pl.ds(pl.multiple_of(t * S5_SEG, S5_SEG), S5_SEG)
                g = gs[sl, q * S5_QCOLS:(q + 1) * S5_QCOLS]
                hp = hs[sl, q * S5_QCOLS:(q + 1) * S5_QCOLS]
                hp_sw = jnp.concatenate([hp[:, half:], hp[:, :half]], axis=1)
                return carry[0] + g * hp, carry[1] + g * hp_sw

            zero = jnp.zeros((S5_SEG, S5_QCOLS), F32)
            acc_p, acc_q = lax.fori_loop(0, S5_TILE_STEPS, astep, (zero, zero))
            dp_ref[:, cols] += jnp.sum(acc_p, axis=0, keepdims=True)
            dq_ref[:, cols] += jnp.sum(acc_q, axis=0, keepdims=True)
            gq = gs[:, cols].astype(BF16)
            db_ref[q] += lax.dot_general(ub[:, q * 128:(q + 1) * 128], gq, (((0,), (0,)), ((), ())),
                                         preferred_element_type=F32)
            hq = hs[S5_SEG:, cols].astype(BF16)
            dc_ref[q] += lax.dot_general(dyb[:, q * 128:(q + 1) * 128], hq, (((0,), (0,)), ((), ())),
                                         preferred_element_type=F32)
            dus.append(lax.dot_general(gq, b_ref[q], (((1,), (1,)), ((), ())), preferred_element_type=F32))
        du_ref[...] = jnp.concatenate(dus, axis=1) + d_ref[...] * dy
        dd_ref[...] += jnp.sum(dy * u, axis=0, keepdims=True)

    full = lambda a: pl.BlockSpec(a.shape, functools.partial(lambda i, nd: (0,) * nd, nd=a.ndim))
    rev_rows = pl.BlockSpec((rows, S5_WIDTH), lambda i: (nt - 1 - i, 0))
    consts = [mats["b_q"], mats["c_q"], mats["lam_r"], mats["lam_i"], mats["lam_ic"], mats["d_row"]]
    acc = lambda s: pl.BlockSpec(s, functools.partial(lambda i, nd: (0,) * nd, nd=len(s)))
    acc_shapes = [(1, S5_COLS), (1, S5_COLS), (S5_QUADS, 128, S5_QCOLS), (S5_QUADS, 128, S5_QCOLS), (1, S5_WIDTH)]
    return _call(
        body, name="s5_bwd_full", grid=(nt,),
        in_specs=[rev_rows, rev_rows, pl.BlockSpec((None, S5_SEG, S5_COLS), lambda i: (nt - 1 - i, 0, 0)), full(g0)]
        + [full(a) for a in consts],
        out_specs=[rev_rows] + [acc(s) for s in acc_shapes],
        out_shape=[jax.ShapeDtypeStruct((S, S5_WIDTH), F32)] + [jax.ShapeDtypeStruct(s, F32) for s in acc_shapes],
        scratch_shapes=[pltpu.VMEM((rows + S5_SEG, S5_COLS), F32), pltpu.VMEM((rows, S5_COLS), F32),
                        pltpu.VMEM((S5_SEG, S5_COLS), F32)],
        compiler_params=_params(56),
    )(u_perm, dy_perm, hinit, g0, *consts)


def _cmul(ar, ai, br, bi):
    return ar * br - ai * bi, ar * bi + ai * br


def _split_cols(v):
    t = v.reshape(v.shape[0], S5_QUADS, 2, S5_QCOLS // 2)
    return t[:, :, 0], t[:, :, 1]


def _join_cols(re, im):
    return jnp.stack([re, im], axis=2).reshape(re.shape[0], S5_COLS)


def _segment_starts(ends, pow_r, pow_i, reverse):
    er, ei = _split_cols(ends)
    pi = -pow_i if reverse else pow_i
    order = list(range(S5_SEG))
    if reverse:
        order = order[::-1]
    cr, ci = jnp.zeros_like(er[0]), jnp.zeros_like(ei[0])
    out_r, out_i = [None] * S5_SEG, [None] * S5_SEG
    for j in order:
        out_r[j], out_i[j] = cr, ci
        mr, mi = _cmul(pow_r, pi, cr, ci)
        cr, ci = mr + er[j], mi + ei[j]
    return _join_cols(jnp.stack(out_r), jnp.stack(out_i))


def _to_segments(a):
    S, w = a.shape
    return a.reshape(S5_SEG, S // S5_SEG, w).transpose(1, 0, 2).reshape(S, w)


def _from_segments(a):
    S, w = a.shape
    return a.reshape(S // S5_SEG, S5_SEG, w).transpose(1, 0, 2).reshape(S, w)


def _my_pos():
    return lax.axis_index("x"), lax.axis_index("y"), lax.axis_index("c")


def _flip(pos, k):
    x, y, c = pos
    return (1 - x if k & 4 else x, 1 - y if k & 2 else y, 1 - c if k & 1 else c)


def _index_of(pos):
    return 4 * pos[0] + 2 * pos[1] + pos[2]


_GATHER_FLIPS = (0, 1, 4, 5, 2, 3, 6, 7)


def _inproj_gather(x, norm_g, pack_a, pack_b, pack_c):
    S = x.shape[0]
    tm = 512
    n_i = S // tm
    order = jnp.stack([_index_of(_flip(_my_pos(), k)) for k in _GATHER_FLIPS]).astype(jnp.int32)

    def body(order_ref, x_ref, g_ref, pa_ref, pb_ref, pc_ref, u_ref, proj_ref, oa_ref, ob_ref, oc_ref,
             wv, u_scr, send_sems, recv_sems, local_sems):
        s, i = pl.program_id(0), pl.program_id(1)
        me = _my_pos()
        mine = _index_of(me)
        sibling = _flip(me, 1)
        srcs = (pb_ref, pa_ref, pc_ref)
        dsts = (wv, oa_ref, oc_ref)

        def direct(a, k):
            return pltpu.make_async_remote_copy(
                src_ref=srcs[a], dst_ref=dsts[a].at[mine], send_sem=send_sems.at[a * 8 + k],
                recv_sem=recv_sems.at[a * 8 + k], device_id=_flip(me, k), device_id_type=MESH)

        def passed_on(a, k):
            slot = _index_of(_flip(me, k))
            return pltpu.make_async_remote_copy(
                src_ref=dsts[a].at[slot], dst_ref=dsts[a].at[slot], send_sem=send_sems.at[a * 8 + (k | 1)],
                recv_sem=recv_sems.at[a * 8 + (k | 1)], device_id=sibling, device_id_type=MESH)

        def arrival(a, k):
            slot = _index_of(_flip(me, k))
            pltpu.make_async_remote_copy(
                src_ref=dsts[a].at[slot], dst_ref=dsts[a].at[slot], send_sem=send_sems.at[a * 8 + k],
                recv_sem=recv_sems.at[a * 8 + k], device_id=me, device_id_type=MESH).wait_recv()

        def own_copy(a):
            return pltpu.make_async_copy(srcs[a], dsts[a].at[mine], local_sems.at[a])

        first = (s == 0) & (i == 0)

        @pl.when(first)
        def _():
            for a in range(3):
                own_copy(a).start()
            for a in range(3):
                for k in (1, 4, 2, 6):
                    direct(a, k).start()
            own_copy(0).wait()

        for idx, k in enumerate(_GATHER_FLIPS):
            if idx == 0:
                continue

            @pl.when((s == idx) & (i == 0))
            def _(k=k):
                arrival(0, k)
                if k in (4, 2, 6):
                    passed_on(0, k).start()

        @pl.when(s == 0)
        def _():
            y, _, _ = _rms_fwd(x_ref[...], g_ref[...])
            u_scr[pl.ds(pl.multiple_of(i * tm, tm), tm), :] = y.astype(BF16)

        ub = u_scr[pl.ds(pl.multiple_of(i * tm, tm), tm), :]
        proj_ref[...] = jnp.dot(ub, wv[order_ref[s]], preferred_element_type=F32)

        @pl.when((s == N_DEV - 1) & (i == n_i - 1))
        def _():
            for a in (1, 2):
                for k in (4, 2, 6):
                    arrival(a, k)
                    passed_on(a, k).start()
            for a in (1, 2):
                for k in (1, 5, 3, 7):
                    arrival(a, k)
                own_copy(a).wait()
            for a in range(3):
                for k in (1, 4, 2, 6):
                    direct(a, k).wait_send()
                for k in (4, 2, 6):
                    passed_on(a, k).wait_send()
            outs = [pltpu.make_async_copy(wv, ob_ref, local_sems.at[3]),
                    pltpu.make_async_copy(u_scr, u_ref, local_sems.at[4])]
            for cp in outs:
                cp.start()
            for cp in outs:
                cp.wait()

    any_spec = pl.BlockSpec(memory_space=pl.ANY)
    vmem = pl.BlockSpec(memory_space=pltpu.VMEM)
    grid_spec = pltpu.PrefetchScalarGridSpec(
        num_scalar_prefetch=1, grid=(N_DEV, n_i),
        in_specs=[pl.BlockSpec((tm, D_MODEL), lambda s, i, o: (jnp.where(s == 0, i, 0), 0)),
                  pl.BlockSpec((1, D_MODEL), lambda s, i, o: (0, 0)), any_spec, vmem, any_spec],
        out_specs=[any_spec, pl.BlockSpec((tm, SHARD_IN), lambda s, i, o: (i, o[s])), any_spec, any_spec, any_spec],
        scratch_shapes=[pltpu.VMEM((N_DEV,) + pack_b.shape, BF16), pltpu.VMEM((S, D_MODEL), BF16),
                        pltpu.SemaphoreType.DMA((24,)), pltpu.SemaphoreType.DMA((24,)), pltpu.SemaphoreType.DMA((5,))],
    )
    return _call(
        body, name="inproj_gather", grid_spec=grid_spec,
        out_shape=[jax.ShapeDtypeStruct((S, D_MODEL), BF16), jax.ShapeDtypeStruct((S, IN_COLS), F32),
                   jax.ShapeDtypeStruct((N_DEV,) + pack_a.shape, BF16), jax.ShapeDtypeStruct((N_DEV,) + pack_b.shape, BF16),
                   jax.ShapeDtypeStruct((N_DEV,) + pack_c.shape, BF16)],
        compiler_params=_params(56, 2),
    )(order, x, norm_g, pack_a, pack_b, pack_c)


_SCATTER_FLIPS = (7, 6, 5, 4, 3, 2, 1, 0)
_N_CHIPS = 4


def _for_row_chunks(n_rows, chunk, fn):
    def step(c, carry):
        fn(pl.ds(pl.multiple_of(c * chunk, chunk), chunk))
        return carry

    lax.fori_loop(0, n_rows // chunk, step, 0)


def _grad_w_in_scatter(dproj, u, rs_a, rs_c):
    S = u.shape[0]
    tm = 512
    n_i = S // tm
    order = jnp.stack([_index_of(_flip(_my_pos(), k)) for k in _SCATTER_FLIPS]).astype(jnp.int32)
    shapes = ((D_MODEL, SHARD_IN), rs_a.shape[1:], rs_c.shape[1:])
    row_chunk = 128

    def body(order_ref, dp_ref, u_ref, ra_ref, rc_ref, gb_ref, ga_ref, gc_ref, acc, sib_b, d2d_b, send_b, ici_b,
             mine_a, sib_a, ici_a, mine_c, sib_c, ici_c, send_sems, recv_sems, local_sems):
        s, i = pl.program_id(0), pl.program_id(1)
        me = _my_pos()
        sibling = _flip(me, 1)
        sib = (sib_b, sib_a, sib_c)
        ici = (ici_b, ici_a, ici_c)
        outs = (gb_ref, ga_ref, gc_ref)

        def to_sibling(arr, m, src):
            return pltpu.make_async_remote_copy(
                src_ref=src, dst_ref=sib[arr].at[m], send_sem=send_sems.at[arr * 7 + m],
                recv_sem=recv_sems.at[arr * 7 + m], device_id=sibling, device_id_type=MESH)

        def over_ici(arr, m, src):
            return pltpu.make_async_remote_copy(
                src_ref=src, dst_ref=ici[arr].at[m], send_sem=send_sems.at[arr * 7 + 4 + m],
                recv_sem=recv_sems.at[arr * 7 + 4 + m], device_id=_flip(me, 6 - 2 * m), device_id_type=MESH)

        def from_sibling(arr, m):
            to_sibling(arr, m, sib[arr].at[m]).wait_recv()

        def from_ici(arr, m):
            over_ici(arr, m, ici[arr].at[m]).wait_recv()

        small = ((1, ra_ref, mine_a), (2, rc_ref, mine_c))

        def local_copy(arr, src, mine, m):
            return pltpu.make_async_copy(src.at[_index_of(_flip(me, 6 - 2 * m))], mine.at[m],
                                         local_sems.at[(arr - 1) * _N_CHIPS + m])

        @pl.when((s == 0) & (i == 0))
        def _():
            for arr, src, mine in small:
                for m in range(_N_CHIPS):
                    to_sibling(arr, m, src.at[_index_of(_flip(me, 7 - 2 * m))]).start()
                    local_copy(arr, src, mine, m).start()

        @pl.when((s == 1) & (i == 0))
        def _():
            for arr, src, mine in small:
                rows, chunk = shapes[arr][0], 16
                for m in range(_N_CHIPS):
                    local_copy(arr, src, mine, m).wait()
                    from_sibling(arr, m)
                    if m < _N_CHIPS - 1:
                        def add(sl, arr=arr, mine=mine, m=m):
                            mine[m, sl, :] = (mine[m, sl, :].astype(F32) + sib[arr][m, sl, :].astype(F32)).astype(BF16)

                        _for_row_chunks(rows, chunk, add)
                        over_ici(arr, m, mine.at[m]).start()
                    else:
                        def keep(sl, arr=arr, mine=mine, m=m):
                            outs[arr][sl, :] = mine[m, sl, :].astype(F32) + sib[arr][m, sl, :].astype(F32)

                        _for_row_chunks(rows, chunk, keep)

        @pl.when(i == 0)
        def _():
            acc[...] = jnp.zeros(acc.shape, F32)

        acc[...] += lax.dot_general(dp_ref[...], u_ref[...], (((0,), (0,)), ((), ())), preferred_element_type=F32)

        def block_rows(c):
            return acc[:, c * row_chunk:(c + 1) * row_chunk].T

        for m in range(_N_CHIPS):
            @pl.when((s == 2 * m) & (i == n_i - 1))
            def _(m=m):
                if m > 0:
                    to_sibling(0, m - 1, d2d_b).wait_send()
                for c in range(D_MODEL // row_chunk):
                    d2d_b[c * row_chunk:(c + 1) * row_chunk, :] = block_rows(c).astype(BF16)
                to_sibling(0, m, d2d_b).start()

            @pl.when((s == 2 * m + 1) & (i == n_i - 1))
            def _(m=m):
                from_sibling(0, m)
                slot = m % 2
                if m == 2:
                    over_ici(0, 0, send_b.at[0]).wait_send()
                for c in range(D_MODEL // row_chunk):
                    rows = slice(c * row_chunk, (c + 1) * row_chunk)
                    total = block_rows(c) + sib_b[m, rows, :].astype(F32)
                    if m < _N_CHIPS - 1:
                        send_b[slot, rows, :] = total.astype(BF16)
                    else:
                        gb_ref[rows, :] = total
                if m < _N_CHIPS - 1:
                    over_ici(0, m, send_b.at[slot]).start()

        @pl.when((s == N_DEV - 1) & (i == n_i - 1))
        def _():
            for arr in range(3):
                for m in range(_N_CHIPS - 1):
                    from_ici(arr, m)
                rows = shapes[arr][0]

                def add(sl, arr=arr):
                    outs[arr][sl, :] = (outs[arr][sl, :] + ici[arr][0, sl, :].astype(F32)
                                        + ici[arr][1, sl, :].astype(F32) + ici[arr][2, sl, :].astype(F32))

                _for_row_chunks(rows, 16, add)
            to_sibling(0, _N_CHIPS - 1, d2d_b).wait_send()
            over_ici(0, 1, send_b.at[1]).wait_send()
            over_ici(0, 2, send_b.at[0]).wait_send()
            for arr, src, mine in small:
                for m in range(_N_CHIPS):
                    to_sibling(arr, m, src.at[0]).wait_send()
                for m in range(_N_CHIPS - 1):
                    over_ici(arr, m, mine.at[m]).wait_send()

    any_spec = pl.BlockSpec(memory_space=pl.ANY)
    vmem = pl.BlockSpec(memory_space=pltpu.VMEM)
    half = lambda shp, n: pltpu.VMEM((n,) + tuple(shp), BF16)
    grid_spec = pltpu.PrefetchScalarGridSpec(
        num_scalar_prefetch=1, grid=(N_DEV, n_i),
        in_specs=[pl.BlockSpec((tm, SHARD_IN), lambda s, i, o: (i, o[s])),
                  pl.BlockSpec((tm, D_MODEL), lambda s, i, o: (i, 0)), any_spec, any_spec],
        out_specs=[vmem, vmem, vmem],
        scratch_shapes=[
            pltpu.VMEM((SHARD_IN, D_MODEL), F32), half(shapes[0], _N_CHIPS), pltpu.VMEM(shapes[0], BF16),
            half(shapes[0], 2), half(shapes[0], _N_CHIPS - 1),
            half(shapes[1], _N_CHIPS), half(shapes[1], _N_CHIPS), half(shapes[1], _N_CHIPS - 1),
            half(shapes[2], _N_CHIPS), half(shapes[2], _N_CHIPS), half(shapes[2], _N_CHIPS - 1),
            pltpu.SemaphoreType.DMA((21,)), pltpu.SemaphoreType.DMA((21,)), pltpu.SemaphoreType.DMA((2 * _N_CHIPS,))],
    )
    return _call(
        body, name="grad_w_in_scatter", grid_spec=grid_spec,
        out_shape=[jax.ShapeDtypeStruct(shp, F32) for shp in shapes],
        compiler_params=_params(58, 2),
    )(order, dproj, u, rs_a, rs_c)


def _adam_update(g, w, m, v):
    m2 = ADAM_B1 * m + (1.0 - ADAM_B1) * g
    v2 = ADAM_B2 * v + (1.0 - ADAM_B2) * (g * g)
    m_hat = m2 / (1.0 - ADAM_B1 ** ADAM_STEP)
    v_hat = v2 / (1.0 - ADAM_B2 ** ADAM_STEP)
    delta = -ADAM_LR * (m_hat / (jnp.sqrt(v_hat) + ADAM_EPS) + ADAM_WD * w)
    return delta, m2, v2


def _small_allreduce_adam(partial, w, m, v):
    rows = partial.shape[0]

    def body(p_ref, w_ref, m_ref, v_ref, g_ref, d_ref, m2_ref, v2_ref, gath, send_sems, recv_sems):
        me = _my_pos()
        mine = _index_of(me)
        gath[mine] = p_ref[...]
        sent = []
        for k in range(1, 8):
            cp = pltpu.make_async_remote_copy(
                src_ref=p_ref, dst_ref=gath.at[mine], send_sem=send_sems.at[k - 1], recv_sem=recv_sems.at[k - 1],
                device_id=_flip(me, k), device_id_type=MESH)
            cp.start()
            sent.append(cp)
        for k in range(1, 8):
            pltpu.make_async_remote_copy(
                src_ref=p_ref, dst_ref=gath.at[_index_of(_flip(me, k))], send_sem=send_sems.at[k - 1],
                recv_sem=recv_sems.at[k - 1], device_id=me, device_id_type=MESH).wait_recv()
        g = gath[0]
        for dev in range(1, N_DEV):
            g = g + gath[dev]
        delta, m2, v2 = _adam_update(g, w_ref[...], m_ref[...], v_ref[...])
        g_ref[...] = g
        d_ref[...] = delta
        m2_ref[...] = m2
        v2_ref[...] = v2
        for cp in sent:
            cp.wait_send()

    vmem = pl.BlockSpec(memory_space=pltpu.VMEM)
    return _call(
        body, name="small_allreduce_adam", in_specs=[vmem] * 4, out_specs=[vmem] * 4,
        out_shape=[jax.ShapeDtypeStruct((rows, 128), F32)] * 4,
        scratch_shapes=[pltpu.VMEM((N_DEV, rows, 128), F32), pltpu.SemaphoreType.DMA((7,)), pltpu.SemaphoreType.DMA((7,))],
        compiler_params=pltpu.CompilerParams(vmem_limit_bytes=32 * MIB),
    )(partial, w, m, v)


def _adam_rows(g, w, m, v):
    rows, cols = w.shape
    tm = rows if rows <= 256 else 256

    def fn(rv, cr, out):
        return list(_adam_update(*rv)), []

    outs, _ = _rowwise("adamw", fn, rows, tm, [(a, cols, 0) for a in (g, w, m, v)], [], [(cols, F32)] * 3, [], 32)
    return outs


_SMALL = ["norm_g", "hg_lb", "hg_norm_g", "s5_a_re", "s5_a_im", "s5_log_dt", "s5_b_re", "s5_b_im", "s5_c_re",
          "s5_c_im", "s5_d", "b_glu", "ple_norm_g", "final_norm_g"]
_BIG = ["w_in", "w_o_hg", "w_glu", "w_o_s5", "w_out", "w_ple", "w_ple_gate"]
_ORDER = ["norm_g", "w_in", "hg_lb", "hg_norm_g", "w_o_hg", "s5_a_re", "s5_a_im", "s5_log_dt", "s5_b_re", "s5_b_im",
          "s5_c_re", "s5_c_im", "s5_d", "w_glu", "b_glu", "w_o_s5", "w_out", "ple_norm_g", "w_ple", "w_ple_gate",
          "final_norm_g"]


def _pack_small(vals):
    parts = []
    for name in _SMALL:
        flat = vals[name].reshape(-1).astype(F32)
        pad = (-flat.shape[0]) % 1024
        parts.append(jnp.pad(flat, (0, pad)))
    return jnp.concatenate(parts).reshape(-1, 128)


def _unpack_small(packed, like):
    flat = packed.reshape(-1)
    out, off = {}, 0
    for name in _SMALL:
        size = like[name].size
        out[name] = flat[off:off + size].reshape(like[name].shape)
        off += size + (-size) % 1024
    return out


def _col_blocks(full):
    k = full.shape[0]
    return full.reshape(k, N_DEV, 128).transpose(1, 0, 2)


def _from_col_blocks(blocks):
    k = blocks.shape[1]
    return blocks.transpose(1, 0, 2).reshape(k, N_DEV * 128)


def kernel(x, p, norm_g, w_in, hg_lb, hg_norm_g, w_o_hg, s5_a_re, s5_a_im, s5_log_dt, s5_b_re, s5_b_im, s5_c_re, s5_c_im, s5_d, w_glu, b_glu, w_o_s5, w_out, ple_norm_g, w_ple, w_ple_gate, final_norm_g, loss_target, m_norm_g, m_w_in, m_hg_lb, m_hg_norm_g, m_w_o_hg, m_s5_a_re, m_s5_a_im, m_s5_log_dt, m_s5_b_re, m_s5_b_im, m_s5_c_re, m_s5_c_im, m_s5_d, m_w_glu, m_b_glu, m_w_o_s5, m_w_out, m_ple_norm_g, m_w_ple, m_w_ple_gate, m_final_norm_g, v_norm_g, v_w_in, v_hg_lb, v_hg_norm_g, v_w_o_hg, v_s5_a_re, v_s5_a_im, v_s5_log_dt, v_s5_b_re, v_s5_b_im, v_s5_c_re, v_s5_c_im, v_s5_d, v_w_glu, v_b_glu, v_w_o_s5, v_w_out, v_ple_norm_g, v_w_ple, v_w_ple_gate, v_final_norm_g):
    args = dict(locals())
    w = {n: args[n] for n in _ORDER}
    m = {n: args["m_" + n] for n in _ORDER}
    v = {n: args["v_" + n] for n in _ORDER}
    xs = x[0]
    ps = p[0, 0]
    tgt = loss_target[0]
    S = xs.shape[0]

    pack_a = jnp.concatenate([w_o_hg[0], w_out[0], w_ple_gate[0]], axis=0).astype(BF16)
    pack_b = w_in[0].astype(BF16)
    pack_c = jnp.concatenate([w_glu[0], w_o_s5[0], w_ple[0]], axis=0).astype(BF16)
    u, proj, all_a, all_b, all_c = _inproj_gather(xs, norm_g, pack_a, pack_b, pack_c)
    wf_o_hg = all_a[:, 0:128].reshape(D_MODEL, D_MODEL)
    wf_out = all_a[:, 128:256].reshape(D_MODEL, D_MODEL)
    wf_pg = all_a[:, 256:384].reshape(D_MODEL, D_MODEL)
    wf_glu = _from_col_blocks(all_c[:, 0:512])
    wf_o_s5 = _from_col_blocks(all_c[:, 512:1024])
    wf_ple = _from_col_blocks(all_c[:, 1024:1280])

    lb = jax.nn.sigmoid(hg_lb[0:1] - hg_lb[1:2])
    s5_names = ["s5_a_re", "s5_a_im", "s5_log_dt", "s5_b_re", "s5_b_im", "s5_c_re", "s5_c_im", "s5_d"]
    build = lambda *a: _s5_matrices(*a, seg_len=S // S5_SEG)
    mats_f32, mats_vjp = jax.vjp(build, *[w[n][0] for n in s5_names])
    mats = dict(mats_f32, b_q=mats_f32["b_q"].astype(BF16), c_q=mats_f32["c_q"].astype(BF16))
    bias_glu = b_glu

    o, states = _hgrn_fwd(proj, lb)
    u_perm = _to_segments(proj[:, COL_US:COL_US + S5_WIDTH])
    zeros_state = jnp.zeros((S5_SEG, S5_COLS), F32)
    (h_ends,) = _s5_fwd_pass(u_perm, mats, zeros_state, False)
    h0 = _segment_starts(h_ends, mats["pow_r"], mats["pow_i"], False)
    y_perm, h_init, _ = _s5_fwd_pass(u_perm, mats, h0, True)
    ys = _from_segments(y_perm)
    y_hg, y_s5, glu, h1 = _stage_branches(o, proj, ys, xs, hg_norm_g, wf_o_hg, wf_glu, bias_glu, wf_o_s5, wf_out)

    dh1, (loss_acc, d_final_g, d_ple_g, d_w_ple, d_w_pg) = _stage_ple_loss(
        h1, ps, tgt, ple_norm_g, wf_ple, wf_pg, final_norm_g.reshape(1, D_MODEL))
    (d_o, d_g_hg, d_gate_hg, d_gate_s5, d_ys5), (d_w_out, d_w_o_hg, d_hg_norm) = _stage_bwd_merge_hg(
        dh1, y_hg, y_s5, proj, o, hg_norm_g, wf_out, wf_o_hg)
    (d_ys, d_z), (d_w_o_s5, d_w_glu, d_b_glu) = _stage_bwd_s5_path(d_ys5, ys, glu, proj, wf_o_s5, wf_glu)
    dq, df, div, d_lb = _hgrn_bwd(proj, lb, d_o, states)
    dy_perm = _to_segments(d_ys)
    g_ends = _s5_bwd_ends(dy_perm, mats)
    g0 = _segment_starts(g_ends, mats["pow_r"], mats["pow_i"], True)
    du_perm, acc_p, acc_q, d_bq, d_cq_t, d_d = _s5_bwd_full(u_perm, dy_perm, h_init, g0, mats)
    d_us = _from_segments(du_perm)
    grad_x, dproj, d_norm_g = _stage_inproj_bwd([dq, df, div, d_g_hg, d_us, d_z, d_gate_hg, d_gate_s5], xs, dh1,
                                                norm_g, all_b)

    p_re, p_im = _split_cols(acc_p)
    q_re, q_im = _split_cols(acc_q)
    d_lam_r = (p_re + p_im)[0]
    d_lam_i = (q_im - q_re)[0]
    zero_row = jnp.zeros((S5_SEG, S5_COLS), F32)
    row_of = lambda re_part, im_part: zero_row.at[0].set(_join_cols(re_part[None], im_part[None])[0])
    zeros_q = jnp.zeros_like(d_lam_r)
    cot = dict(
        lam_r=row_of(d_lam_r, zeros_q), lam_i=row_of(zeros_q, d_lam_i), lam_ic=zero_row,
        b_q=d_bq, c_q=d_cq_t.transpose(0, 2, 1), d_row=d_d,
        pow_r=jnp.zeros_like(mats["pow_r"]), pow_i=jnp.zeros_like(mats["pow_i"]),
    )
    d_s5 = mats_vjp(cot)

    s_lb = lb * (1.0 - lb)
    d_hg_lb = jnp.concatenate([d_lb * s_lb, -d_lb * s_lb], axis=0)
    small_g = dict(norm_g=d_norm_g, hg_lb=d_hg_lb, hg_norm_g=d_hg_norm, b_glu=d_b_glu, ple_norm_g=d_ple_g,
                   final_norm_g=d_final_g)
    for name, g in zip(s5_names, d_s5):
        small_g[name] = g
    pk = lambda d: _pack_small({n: d[n] for n in _SMALL})
    sg, sd, sm, sv = _small_allreduce_adam(pk(small_g), pk(w), pk(m), pk(v))
    like = {n: w[n] for n in _SMALL}
    out_g, out_d, out_m, out_v = (_unpack_small(t, like) for t in (sg, sd, sm, sv))

    rs_a = jnp.concatenate([d_w_o_hg.reshape(N_DEV, 128, D_MODEL), d_w_out.reshape(N_DEV, 128, D_MODEL),
                            d_w_pg.reshape(N_DEV, 128, D_MODEL)], axis=1).astype(BF16)
    rs_c = jnp.concatenate([_col_blocks(d_w_glu), _col_blocks(d_w_o_s5), _col_blocks(d_w_ple)], axis=1).astype(BF16)
    g_b, g_a, g_c = _grad_w_in_scatter(dproj, u, rs_a, rs_c)
    big_g = dict(w_o_hg=g_a[0:128], w_out=g_a[128:256], w_ple_gate=g_a[256:384], w_in=g_b,
                 w_glu=g_c[0:512], w_o_s5=g_c[512:1024], w_ple=g_c[1024:1280])
    for name in _BIG:
        shape = w[name].shape
        g2 = big_g[name]
        d2, m2, v2 = _adam_rows(g2, w[name][0], m[name][0], v[name][0])
        out_g[name], out_d[name], out_m[name], out_v[name] = (t.reshape(shape) for t in (g2, d2, m2, v2))

    loss = lax.psum(loss_acc[0, 0], ("x", "y", "c"))
    return (loss, grad_x[None], *[out_g[n] for n in _ORDER], *[out_d[n] for n in _ORDER],
            *[out_m[n] for n in _ORDER], *[out_v[n] for n in _ORDER])
```

```python
import functools
import math

import jax
import jax.numpy as jnp
from jax import lax
from jax.experimental import pallas as pl
from jax.experimental.pallas import tpu as pltpu

F32 = jnp.float32
BF16 = jnp.bfloat16

D_MODEL = 1024
N_DEV = 8
IN_COLS = 7168
SHARD_IN = IN_COLS // N_DEV
HG_HEADS = 8
HG_DIM = 128
HG_CHUNK = 64
HG_SUPER = 256
HG_HEADS_PER_STEP = 4
S5_WIDTH = 512
S5_GROUPS = 32
S5_STATE = 64
S5_CH = 16
S5_SEG = 8
S5_QUADS = 4
S5_QCOLS = 1024
S5_COLS = S5_QUADS * S5_QCOLS
S5_TILE_STEPS = 64
NORM_EPS = 1e-6
ADAM_LR = 0.001
ADAM_B1 = 0.9
ADAM_B2 = 0.999
ADAM_EPS = 1e-08
ADAM_WD = 0.01
ADAM_STEP = 10
MIB = 1024 * 1024
MESH = pl.DeviceIdType.MESH

COL_Q, COL_F, COL_I, COL_G, COL_US, COL_ZS, COL_GH, COL_GS = 0, 1024, 2048, 3072, 4096, 4608, 5120, 6144


def _call(body, **kw):
    return pl.pallas_call(body, **kw)


def _params(vmem_mb, n_grid=1):
    return pltpu.CompilerParams(
        dimension_semantics=("arbitrary",) * n_grid, vmem_limit_bytes=vmem_mb * MIB
    )


def _bdot(a, b):
    return jnp.dot(a.astype(BF16), b.astype(BF16), preferred_element_type=F32)


def _bdot_nt(a, b):
    return lax.dot_general(a.astype(BF16), b.astype(BF16), (((1,), (1,)), ((), ())), preferred_element_type=F32)


def _bdot_tn(a, b):
    return lax.dot_general(a.astype(BF16), b.astype(BF16), (((0,), (0,)), ((), ())), preferred_element_type=F32)


def _sigmoid(x):
    return jax.nn.sigmoid(x)


def _silu(x):
    return x * _sigmoid(x)


def _dsilu(x):
    s = _sigmoid(x)
    return s * (1.0 + x * (1.0 - s))


_GELU_C = math.sqrt(2.0 / math.pi)


def _gelu(x):
    return 0.5 * x * (1.0 + jnp.tanh(_GELU_C * (x + 0.044715 * x * x * x)))


def _dgelu(x):
    t = jnp.tanh(_GELU_C * (x + 0.044715 * x * x * x))
    return 0.5 * (1.0 + t) + 0.5 * x * (1.0 - t * t) * _GELU_C * (1.0 + 3.0 * 0.044715 * x * x)


def _rms_fwd(x, g):
    r = lax.rsqrt(jnp.mean(x * x, axis=-1, keepdims=True) + NORM_EPS)
    n = x * r
    return n * g, n, r


def _rms_bwd(dy, n, r, g):
    dn = dy * g
    dx = r * (dn - n * jnp.mean(dn * n, axis=-1, keepdims=True))
    return dx, jnp.sum(dy * n, axis=0, keepdims=True)


def _head_rms_fwd(o, g):
    ns, rs = [], []
    for h in range(HG_HEADS):
        oh = o[:, h * HG_DIM:(h + 1) * HG_DIM]
        r = lax.rsqrt(jnp.mean(oh * oh, axis=-1, keepdims=True) + NORM_EPS)
        ns.append(oh * r)
        rs.append(r)
    n = jnp.concatenate(ns, axis=1)
    return n * g, n, rs


def _head_rms_bwd(dy, n, rs, g):
    dn = dy * g
    dxs = []
    for h in range(HG_HEADS):
        sl = slice(h * HG_DIM, (h + 1) * HG_DIM)
        dxs.append(rs[h] * (dn[:, sl] - n[:, sl] * jnp.mean(dn[:, sl] * n[:, sl], axis=-1, keepdims=True)))
    return jnp.concatenate(dxs, axis=1), jnp.sum(dy * n, axis=0, keepdims=True)


def _rowwise(name, fn, n_rows, tm, rows, consts, out_rows, out_accs, vmem_mb):
    n_r, n_c, n_or, n_oa = len(rows), len(consts), len(out_rows), len(out_accs)

    def body(*refs):
        r_refs = refs[:n_r]
        c_refs = refs[n_r:n_r + n_c]
        or_refs = refs[n_r + n_c:n_r + n_c + n_or]
        oa_refs = refs[n_r + n_c + n_or:]
        outs, accs = fn([r[...] for r in r_refs], c_refs, or_refs)

        if n_oa:
            @pl.when(pl.program_id(0) == 0)
            def _():
                for ref in oa_refs:
                    ref[...] = jnp.zeros(ref.shape, ref.dtype)

        for ref, v in zip(or_refs, outs):
            if v is not None:
                ref[...] = v.astype(ref.dtype)
        for ref, v in zip(oa_refs, accs):
            ref[...] += v.astype(ref.dtype)

    in_specs = [pl.BlockSpec((tm, w), functools.partial(lambda i, c: (i, c), c=cb)) for (_, w, cb) in rows]
    in_specs += [pl.BlockSpec(c.shape, functools.partial(lambda i, nd: (0,) * nd, nd=c.ndim),
                              pipeline_mode=pl.Buffered(1)) for c in consts]
    out_specs = [pl.BlockSpec((tm, w), lambda i: (i, 0)) for (w, _) in out_rows]
    out_specs += [pl.BlockSpec(s, functools.partial(lambda i, nd: (0,) * nd, nd=len(s))) for (s, _) in out_accs]
    out_shape = [jax.ShapeDtypeStruct((n_rows, w), dt) for (w, dt) in out_rows]
    out_shape += [jax.ShapeDtypeStruct(s, dt) for (s, dt) in out_accs]
    res = _call(
        body, name=name, grid=(n_rows // tm,), in_specs=in_specs, out_specs=out_specs, out_shape=out_shape,
        compiler_params=_params(vmem_mb),
    )(*[a for (a, _, _) in rows], *consts)
    return res[:n_or], res[n_or:]


def _stage_branches(o, proj, ys, x, hg_norm_g, w_o_hg, w_glu, b_glu, w_o_s5, w_out):
    S = x.shape[0]

    def fn(rv, cr, out):
        o_b, g_hg, z_s, gate_hg, gate_s5, ys_b, x_b = rv
        gn_ref, wohg_ref, wglu_ref, bglu_ref, wos5_ref, wout_ref = cr
        on, _, _ = _head_rms_fwd(o_b, gn_ref[...])
        a = on * _silu(g_hg)
        y_hg = jnp.dot(a.astype(BF16), wohg_ref[...], preferred_element_type=F32)
        gl = _gelu(ys_b)
        glu = jnp.dot(gl.astype(BF16), wglu_ref[...], preferred_element_type=F32) + bglu_ref[...]
        ys2 = glu[:, :S5_WIDTH] * _sigmoid(glu[:, S5_WIDTH:]) * _silu(z_s)
        y_s5 = jnp.dot(ys2.astype(BF16), wos5_ref[...], preferred_element_type=F32)
        merged = _sigmoid(gate_hg) * y_hg + _sigmoid(gate_s5) * y_s5
        h1 = x_b + jnp.dot(merged.astype(BF16), wout_ref[...], preferred_element_type=F32)
        return [y_hg, y_s5, glu, h1], []

    rows = [(o, D_MODEL, 0), (proj, D_MODEL, COL_G // D_MODEL), (proj, S5_WIDTH, COL_ZS // S5_WIDTH),
            (proj, D_MODEL, COL_GH // D_MODEL), (proj, D_MODEL, COL_GS // D_MODEL), (ys, S5_WIDTH, 0), (x, D_MODEL, 0)]
    (y_hg, y_s5, glu, h1), _ = _rowwise(
        "branches", fn, S, 256, rows, [hg_norm_g, w_o_hg, w_glu, b_glu, w_o_s5, w_out],
        [(D_MODEL, F32)] * 4, [], 56)
    return y_hg, y_s5, glu, h1


def _stage_ple_loss(h1, p, target, ple_norm_g, w_ple, w_ple_gate, final_norm_g):
    S = h1.shape[0]

    def fn(rv, cr, out):
        h1_b, p_b, t_b = rv
        gp_ref, wple_ref, wpg_ref, gf_ref = cr
        n2g, n2, r2 = _rms_fwd(h1_b, gp_ref[...])
        z = jnp.dot(n2g.astype(BF16), wpg_ref[...], preferred_element_type=F32)
        gate = _sigmoid(z)
        pe = jnp.dot(p_b.astype(BF16), wple_ref[...], preferred_element_type=F32)
        h2 = h1_b + pe * gate
        y, nf, rf = _rms_fwd(h2, gf_ref[...])
        err = y - t_b
        loss_rows = 0.5 * jnp.mean(err * err, axis=-1, keepdims=True)
        loss_inc = jnp.broadcast_to(jnp.sum(loss_rows, axis=0, keepdims=True), (1, 128))
        dy = err * (1.0 / D_MODEL)
        dh2, d_gf = _rms_bwd(dy, nf, rf, gf_ref[...])
        d_pe = dh2 * gate
        dz = dh2 * pe * gate * (1.0 - gate)
        d_wple = _bdot_tn(p_b, d_pe)
        d_wpg = _bdot_tn(n2g, dz)
        dn2g = _bdot_nt(dz, wpg_ref[...])
        dh1n, d_gp = _rms_bwd(dn2g, n2, r2, gp_ref[...])
        return [dh2 + dh1n], [loss_inc, d_gf, d_gp, d_wple, d_wpg]

    (dh1,), accs = _rowwise(
        "ple_loss", fn, S, 256, [(h1, D_MODEL, 0), (p, 256, 0), (target, D_MODEL, 0)],
        [ple_norm_g, w_ple, w_ple_gate, final_norm_g], [(D_MODEL, F32)],
        [((1, 128), F32), ((1, D_MODEL), F32), ((1, D_MODEL), F32), ((256, D_MODEL), F32), ((D_MODEL, D_MODEL), F32)], 56)
    return dh1, accs


def _stage_bwd_merge_hg(dh1, y_hg, y_s5, proj, o, hg_norm_g, w_out, w_o_hg):
    S = dh1.shape[0]

    def fn(rv, cr, out):
        dh1_b, yhg, ys5, gate_hg, gate_s5, o_b, g_hg = rv
        gn_ref, wout_ref, wohg_ref = cr
        sg_h, sg_s = _sigmoid(gate_hg), _sigmoid(gate_s5)
        merged = sg_h * yhg + sg_s * ys5
        d_wout = _bdot_tn(merged, dh1_b)
        d_merged = _bdot_nt(dh1_b, wout_ref[...])
        d_gate_hg = d_merged * yhg * sg_h * (1.0 - sg_h)
        d_gate_s5 = d_merged * ys5 * sg_s * (1.0 - sg_s)
        d_yhg = d_merged * sg_h
        d_ys5 = d_merged * sg_s
        ong, on, rs = _head_rms_fwd(o_b, gn_ref[...])
        sil = _silu(g_hg)
        a = ong * sil
        d_wohg = _bdot_tn(a, d_yhg)
        d_a = _bdot_nt(d_yhg, wohg_ref[...])
        d_g_hg = d_a * ong * _dsilu(g_hg)
        d_o, d_gn = _head_rms_bwd(d_a * sil, on, rs, gn_ref[...])
        return [d_o, d_g_hg, d_gate_hg, d_gate_s5, d_ys5], [d_wout, d_wohg, d_gn]

    rows = [(dh1, D_MODEL, 0), (y_hg, D_MODEL, 0), (y_s5, D_MODEL, 0), (proj, D_MODEL, COL_GH // D_MODEL),
            (proj, D_MODEL, COL_GS // D_MODEL), (o, D_MODEL, 0), (proj, D_MODEL, COL_G // D_MODEL)]
    outs, accs = _rowwise(
        "bwd_merge_hg", fn, S, 256, rows, [hg_norm_g, w_out, w_o_hg], [(D_MODEL, BF16)] * 5,
        [((D_MODEL, D_MODEL), F32), ((D_MODEL, D_MODEL), F32), ((1, D_MODEL), F32)], 56)
    return outs, accs


def _stage_bwd_s5_path(d_ys5, ys, glu, proj, w_o_s5, w_glu):
    S = ys.shape[0]

    def fn(rv, cr, out):
        d_ys5_b, ys_b, glu_b, z_s = rv
        wos5_ref, wglu_ref = cr
        ga, gb = glu_b[:, :S5_WIDTH], glu_b[:, S5_WIDTH:]
        sgb, silz = _sigmoid(gb), _silu(z_s)
        ys2 = ga * sgb * silz
        d_wos5 = _bdot_tn(ys2, d_ys5_b)
        d_ys2 = _bdot_nt(d_ys5_b, wos5_ref[...])
        d_ga = d_ys2 * sgb * silz
        d_gb = d_ys2 * ga * sgb * (1.0 - sgb) * silz
        d_z = d_ys2 * ga * sgb * _dsilu(z_s)
        d_glu = jnp.concatenate([d_ga, d_gb], axis=1)
        gl = _gelu(ys_b)
        d_wglu = _bdot_tn(gl, d_glu)
        d_bglu = jnp.sum(d_glu, axis=0, keepdims=True)
        d_gl = _bdot_nt(d_glu, wglu_ref[...])
        return [d_gl * _dgelu(ys_b), d_z], [d_wos5, d_wglu, d_bglu]

    rows = [(d_ys5, D_MODEL, 0), (ys, S5_WIDTH, 0), (glu, D_MODEL, 0), (proj, S5_WIDTH, COL_ZS // S5_WIDTH)]
    outs, accs = _rowwise(
        "bwd_s5_path", fn, S, 256, rows, [w_o_s5, w_glu], [(S5_WIDTH, F32), (S5_WIDTH, BF16)],
        [((S5_WIDTH, D_MODEL), F32), ((S5_WIDTH, D_MODEL), F32), ((1, D_MODEL), F32)], 48)
    return outs, accs


def _stage_inproj_bwd(pieces, x, dh1, norm_g, w_in_all):
    S = x.shape[0]

    def fn(rv, cr, out):
        g_ref, w_ref = cr
        x_b, dh1_b = rv[8], rv[9]
        dproj_ref = out[1]
        col = 0
        for v in rv[:8]:
            dproj_ref[:, col:col + v.shape[1]] = v.astype(BF16)
            col += v.shape[1]
        d_u = jnp.zeros((x_b.shape[0], D_MODEL), F32)
        for j in range(N_DEV):
            d_u = d_u + lax.dot_general(dproj_ref[:, j * SHARD_IN:(j + 1) * SHARD_IN], w_ref[j],
                                        (((1,), (1,)), ((), ())), preferred_element_type=F32)
        _, n, r = _rms_fwd(x_b, g_ref[...])
        dx, d_g = _rms_bwd(d_u, n, r, g_ref[...])
        return [dh1_b + dx, None], [d_g]

    rows = [(a, a.shape[1], 0) for a in pieces] + [(x, D_MODEL, 0), (dh1, D_MODEL, 0)]
    (grad_x, dproj), (d_g,) = _rowwise(
        "inproj_bwd", fn, S, 256, rows, [norm_g, w_in_all], [(D_MODEL, F32), (IN_COLS, BF16)],
        [((1, D_MODEL), F32)], 56)
    return grad_x, dproj, d_g


def _chunk_row(shape):
    return lax.broadcasted_iota(jnp.int32, shape, 0) & (HG_CHUNK - 1)


def _chunk_cumsum(x):
    r_in = _chunk_row(x.shape)
    s = 1
    while s < HG_CHUNK:
        x = x + jnp.where(r_in >= s, pltpu.roll(x, s, 0), 0.0)
        s *= 2
    return x


def _chunk_suffix_sum(x):
    n = x.shape[0]
    r_in = _chunk_row(x.shape)
    s = 1
    while s < HG_CHUNK:
        x = x + jnp.where(r_in < HG_CHUNK - s, pltpu.roll(x, n - s, 0), 0.0)
        s *= 2
    return x


def _hgrn_prep(q, fl, lb):
    nc = HG_SUPER // HG_CHUNK
    sig = _sigmoid(fl)
    f = lb + (1.0 - lb) * sig
    k = (1.0 - lb) * (1.0 - sig)
    b = _chunk_cumsum(jnp.log(f))
    b3 = b.reshape(nc, HG_CHUNK, HG_DIM)
    row3 = lax.broadcasted_iota(jnp.int32, b3.shape, 1)
    pick = lambda r: jnp.sum(jnp.where(row3 == r, b3, 0.0), axis=1, keepdims=True)
    b_mid = pick(HG_CHUNK // 2 - 1)
    b_last = pick(HG_CHUNK - 1)
    flat = lambda t: t.reshape(HG_SUPER, HG_DIM)
    e_qa = flat(jnp.exp(b3 - b_mid))
    e_ka = flat(jnp.exp(b_mid - b3))
    e_qd = jnp.exp(b)
    e_kd = flat(jnp.exp(b_last - b3))
    dc = jnp.exp(b_last)
    return sig, f, k, e_qa, e_ka, e_qd, e_kd, dc


def _hgrn_mask():
    r = lax.broadcasted_iota(jnp.int32, (HG_SUPER, HG_SUPER), 0)
    c = lax.broadcasted_iota(jnp.int32, (HG_SUPER, HG_SUPER), 1)
    shift = HG_CHUNK.bit_length() - 1
    return (jnp.right_shift(r, shift) == jnp.right_shift(c, shift)) & (r >= c)


def _hgrn_fwd(proj, lb):
    S = proj.shape[0]
    nb = S // HG_SUPER
    nc = HG_SUPER // HG_CHUNK
    hp = HG_HEADS_PER_STEP
    wide = hp * HG_DIM

    def body(q_ref, f_ref, iv_ref, lb_ref, o_ref, st_ref, state):
        @pl.when(pl.program_id(1) == 0)
        def _():
            state[...] = jnp.zeros(state.shape, F32)

        mask = _hgrn_mask()
        for hh in range(hp):
            lanes = slice(hh * HG_DIM, (hh + 1) * HG_DIM)
            q, iv = q_ref[:, lanes], iv_ref[:, lanes]
            _, _, k, e_qa, e_ka, e_qd, e_kd, dc = _hgrn_prep(q, f_ref[:, lanes], lb_ref[:, lanes])
            scores = jnp.where(mask, _bdot_nt(q * e_qa, k * e_ka), 0.0)
            o_intra = _bdot(scores, iv)
            qd, kd = q * e_qd, k * e_kd
            for c in range(nc):
                sl = slice(c * HG_CHUNK, (c + 1) * HG_CHUNK)
                st = state[hh]
                st_ref[hh, c] = st
                o_ref[sl, lanes] = o_intra[sl] + _bdot_nt(qd[sl], st)
                state[hh] = dc[c] * st + _bdot_tn(iv[sl], kd[sl])

    blk = lambda base: pl.BlockSpec((HG_SUPER, wide), functools.partial(lambda h, i, b: (i, b + h), b=base // wide))
    return _call(
        body, name="hgrn_fwd", grid=(HG_HEADS // hp, nb),
        in_specs=[blk(COL_Q), blk(COL_F), blk(COL_I), pl.BlockSpec((1, wide), lambda h, i: (0, h))],
        out_specs=[pl.BlockSpec((HG_SUPER, wide), lambda h, i: (i, h)),
                   pl.BlockSpec((hp, nc, HG_DIM, HG_DIM), lambda h, i: (h, i, 0, 0))],
        out_shape=[jax.ShapeDtypeStruct((S, D_MODEL), F32),
                   jax.ShapeDtypeStruct((HG_HEADS, S // HG_CHUNK, HG_DIM, HG_DIM), F32)],
        scratch_shapes=[pltpu.VMEM((hp, HG_DIM, HG_DIM), F32)],
        compiler_params=_params(40, 2),
    )(proj, proj, proj, lb)


def _hgrn_bwd(proj, lb, d_o, states):
    S = proj.shape[0]
    nb = S // HG_SUPER
    nc = HG_SUPER // HG_CHUNK
    hp = HG_HEADS_PER_STEP
    wide = hp * HG_DIM

    def body(q_ref, f_ref, iv_ref, lb_ref, do_ref, st_ref, dq_ref, df_ref, div_ref, dlb_ref, dstate):
        @pl.when(pl.program_id(1) == 0)
        def _():
            dstate[...] = jnp.zeros(dstate.shape, F32)
            dlb_ref[...] = jnp.zeros(dlb_ref.shape, F32)

        mask = _hgrn_mask()
        for hh in range(hp):
            lanes = slice(hh * HG_DIM, (hh + 1) * HG_DIM)
            q, iv, do, lb_v = q_ref[:, lanes], iv_ref[:, lanes], do_ref[:, lanes], lb_ref[:, lanes]
            sig, f, k, e_qa, e_ka, e_qd, e_kd, dc = _hgrn_prep(q, f_ref[:, lanes], lb_v)
            qa, ka, qd, kd = q * e_qa, k * e_ka, q * e_qd, k * e_kd
            scores = jnp.where(mask, _bdot_nt(qa, ka), 0.0)
            d_scores = jnp.where(mask, _bdot_nt(do, iv), 0.0)
            d_iv_intra = _bdot_tn(scores, do)
            d_qa = _bdot(d_scores, ka)
            d_ka = _bdot_tn(d_scores, qa)
            d_qd, d_kd, d_last = [None] * nc, [None] * nc, [None] * nc
            for c in reversed(range(nc)):
                sl = slice(c * HG_CHUNK, (c + 1) * HG_CHUNK)
                st = st_ref[hh, c]
                ds = dstate[hh]
                d_qd[c] = _bdot(do[sl], st)
                d_kd[c] = _bdot(iv[sl], ds)
                div_ref[sl, lanes] = (d_iv_intra[sl] + _bdot_nt(kd[sl], ds)).astype(div_ref.dtype)
                d_last[c] = (jnp.sum(ds * st, axis=0, keepdims=True) * dc[c]
                             + jnp.sum(d_kd[c] * kd[sl], axis=0, keepdims=True))
                dstate[hh] = dc[c] * ds + _bdot_tn(do[sl], qd[sl])
            d_qd = jnp.concatenate(d_qd, axis=0)
            d_kd = jnp.concatenate(d_kd, axis=0)
            d_b = d_qa * qa - d_ka * ka + d_qd * qd - d_kd * kd
            last_rows = jnp.concatenate([jnp.broadcast_to(t, (HG_CHUNK, HG_DIM)) for t in d_last], axis=0)
            d_b = d_b + jnp.where(_chunk_row(d_b.shape) == HG_CHUNK - 1, last_rows, 0.0)
            d_logf = _chunk_suffix_sum(d_b)
            d_k = d_ka * e_ka + d_kd * e_kd
            g_f = d_logf / f
            d_sig = (g_f - d_k) * (1.0 - lb_v)
            dq_ref[:, lanes] = (d_qa * e_qa + d_qd * e_qd).astype(dq_ref.dtype)
            df_ref[:, lanes] = (d_sig * sig * (1.0 - sig)).astype(df_ref.dtype)
            d_lb = jnp.sum((g_f - d_k) * (1.0 - sig), axis=0, keepdims=True)
            dlb_ref[:, lanes] += jnp.broadcast_to(d_lb, (8, HG_DIM))

    rev = lambda i: nb - 1 - i
    blk = lambda base: pl.BlockSpec((HG_SUPER, wide), functools.partial(lambda h, i, b: (rev(i), b + h), b=base // wide))
    row_out = pl.BlockSpec((HG_SUPER, wide), lambda h, i: (rev(i), h))
    dq, df, div, dlb = _call(
        body, name="hgrn_bwd", grid=(HG_HEADS // hp, nb),
        in_specs=[blk(COL_Q), blk(COL_F), blk(COL_I), pl.BlockSpec((1, wide), lambda h, i: (0, h)),
                  pl.BlockSpec((HG_SUPER, wide), lambda h, i: (rev(i), h)),
                  pl.BlockSpec((hp, nc, HG_DIM, HG_DIM), lambda h, i: (h, rev(i), 0, 0))],
        out_specs=[row_out, row_out, row_out, pl.BlockSpec((8, wide), lambda h, i: (0, h))],
        out_shape=[jax.ShapeDtypeStruct((S, D_MODEL), BF16)] * 3 + [jax.ShapeDtypeStruct((8, D_MODEL), F32)],
        scratch_shapes=[pltpu.VMEM((hp, HG_DIM, HG_DIM), F32)],
        compiler_params=_params(40, 2),
    )(proj, proj, proj, lb, d_o, states)
    return dq, df, div, dlb[0:1]


def _s5_matrices(a_re, a_im, log_dt, b_re, b_im, c_re, c_im, d, seg_len):
    dt = jnp.exp(log_dt)[:, None]
    mag = jnp.exp(a_re * dt)
    lr, li = mag * jnp.cos(a_im * dt), mag * jnp.sin(a_im * dt)
    den = a_re * a_re + a_im * a_im
    nr = lr - 1.0
    sr = (nr * a_re + li * a_im) / den
    si = (li * a_re - nr * a_im) / den
    bbr = sr[..., None] * b_re - si[..., None] * b_im
    bbi = sr[..., None] * b_im + si[..., None] * b_re
    eye = jnp.eye(8, dtype=F32)

    def quad_cols(v):
        return v.reshape(S5_QUADS, 8 * S5_STATE)

    def lam_row(re_part, im_part):
        row = jnp.concatenate([quad_cols(re_part), quad_cols(im_part)], axis=1).reshape(1, S5_COLS)
        return jnp.broadcast_to(row, (S5_SEG, S5_COLS))

    def b_mat(bb):
        t = bb.reshape(S5_QUADS, 8, S5_STATE, S5_CH)
        return jnp.einsum("qgnc,gh->qgchn", t, eye).reshape(S5_QUADS, 8 * S5_CH, 8 * S5_STATE)

    def c_mat(cc):
        t = cc.reshape(S5_QUADS, 8, S5_CH, S5_STATE)
        return jnp.einsum("qgcn,gh->qgnhc", t, eye).reshape(S5_QUADS, 8 * S5_STATE, 8 * S5_CH)

    ang = a_im * dt * seg_len
    magp = jnp.exp(a_re * dt * seg_len)
    lpr, lpi = magp * jnp.cos(ang), magp * jnp.sin(ang)
    return dict(
        lam_r=lam_row(lr, lr), lam_i=lam_row(-li, li), lam_ic=lam_row(li, -li),
        b_q=jnp.concatenate([b_mat(bbr), b_mat(bbi)], axis=2),
        c_q=jnp.concatenate([c_mat(c_re), -c_mat(c_im)], axis=1),
        d_row=d.reshape(1, S5_WIDTH), pow_r=quad_cols(lpr), pow_i=quad_cols(lpi),
    )


def _swap_halves(h):
    half = S5_QCOLS // 2
    parts = []
    for q in range(S5_QUADS):
        parts.append(h[:, q * S5_QCOLS + half:(q + 1) * S5_QCOLS])
        parts.append(h[:, q * S5_QCOLS:q * S5_QCOLS + half])
    return jnp.concatenate(parts, axis=1)


def _s5_fwd_pass(u_perm, mats, h0, with_output):
    S = u_perm.shape[0]
    rows = S5_TILE_STEPS * S5_SEG
    nt = S // rows

    def body(*refs):
        if with_output:
            u_ref, b_ref, lr_ref, li_ref, h0_ref, c_ref, d_ref, y_ref, hinit_ref, hend_ref, xs, hcar = refs
        else:
            u_ref, b_ref, lr_ref, li_ref, h0_ref, hend_ref, xs, hcar = refs

        @pl.when(pl.program_id(0) == 0)
        def _():
            hcar[...] = h0_ref[...]

        if with_output:
            hinit_ref[...] = hcar[...]
        u = u_ref[...]
        ub = u.astype(BF16)
        for q in range(S5_QUADS):
            xs[:, q * S5_QCOLS:(q + 1) * S5_QCOLS] = jnp.dot(ub[:, q * 128:(q + 1) * 128], b_ref[q], preferred_element_type=F32)

        def step(t, h):
            sl = pl.ds(pl.multiple_of(t * S5_SEG, S5_SEG), S5_SEG)
            hn = lr_ref[...] * h + li_ref[...] * _swap_halves(h) + xs[sl, :]
            xs[sl, :] = hn
            return hn

        h = lax.fori_loop(0, S5_TILE_STEPS, step, hcar[...])
        hcar[...] = h
        hend_ref[...] = h
        if with_output:
            ys = [jnp.dot(xs[:, q * S5_QCOLS:(q + 1) * S5_QCOLS].astype(BF16), c_ref[q], preferred_element_type=F32)
                  for q in range(S5_QUADS)]
            y_ref[...] = jnp.concatenate(ys, axis=1) + d_ref[...] * u

    full = lambda a: pl.BlockSpec(a.shape, functools.partial(lambda i, nd: (0,) * nd, nd=a.ndim))
    ins = [u_perm, mats["b_q"], mats["lam_r"], mats["lam_i"], h0]
    in_specs = [pl.BlockSpec((rows, S5_WIDTH), lambda i: (i, 0))] + [full(a) for a in ins[1:]]
    out_specs = [pl.BlockSpec((S5_SEG, S5_COLS), lambda i: (0, 0))]
    out_shape = [jax.ShapeDtypeStruct((S5_SEG, S5_COLS), F32)]
    if with_output:
        ins += [mats["c_q"], mats["d_row"]]
        in_specs += [full(mats["c_q"]), full(mats["d_row"])]
        out_specs = [pl.BlockSpec((rows, S5_WIDTH), lambda i: (i, 0)),
                     pl.BlockSpec((None, S5_SEG, S5_COLS), lambda i: (i, 0, 0))] + out_specs
        out_shape = [jax.ShapeDtypeStruct((S, S5_WIDTH), F32), jax.ShapeDtypeStruct((nt, S5_SEG, S5_COLS), F32)] + out_shape
    return _call(
        body, name="s5_fwd_y" if with_output else "s5_fwd_ends", grid=(nt,), in_specs=in_specs, out_specs=out_specs,
        out_shape=out_shape,
        scratch_shapes=[pltpu.VMEM((rows, S5_COLS), F32), pltpu.VMEM((S5_SEG, S5_COLS), F32)],
        compiler_params=_params(40),
    )(*ins)


def _s5_bwd_ends(dy_perm, mats):
    S = dy_perm.shape[0]
    rows = S5_TILE_STEPS * S5_SEG
    nt = S // rows

    def body(dy_ref, c_ref, lr_ref, lic_ref, gend_ref, gs, gcar):
        @pl.when(pl.program_id(0) == 0)
        def _():
            gcar[...] = jnp.zeros(gcar.shape, F32)

        dyb = dy_ref[...].astype(BF16)
        for q in range(S5_QUADS):
            gs[:, q * S5_QCOLS:(q + 1) * S5_QCOLS] = lax.dot_general(
                dyb[:, q * 128:(q + 1) * 128], c_ref[q], (((1,), (1,)), ((), ())), preferred_element_type=F32)

        def step(k, g):
            t = S5_TILE_STEPS - 1 - k
            sl = pl.ds(pl.multiple_of(t * S5_SEG, S5_SEG), S5_SEG)
            return lr_ref[...] * g + lic_ref[...] * _swap_halves(g) + gs[sl, :]

        g = lax.fori_loop(0, S5_TILE_STEPS, step, gcar[...])
        gcar[...] = g
        gend_ref[...] = g

    full = lambda a: pl.BlockSpec(a.shape, functools.partial(lambda i, nd: (0,) * nd, nd=a.ndim))
    return _call(
        body, name="s5_bwd_ends", grid=(nt,),
        in_specs=[pl.BlockSpec((rows, S5_WIDTH), lambda i: (nt - 1 - i, 0)), full(mats["c_q"]), full(mats["lam_r"]),
                  full(mats["lam_ic"])],
        out_specs=pl.BlockSpec((S5_SEG, S5_COLS), lambda i: (0, 0)),
        out_shape=jax.ShapeDtypeStruct((S5_SEG, S5_COLS), F32),
        scratch_shapes=[pltpu.VMEM((rows, S5_COLS), F32), pltpu.VMEM((S5_SEG, S5_COLS), F32)],
        compiler_params=_params(40),
    )(dy_perm, mats["c_q"], mats["lam_r"], mats["lam_ic"])


def _s5_bwd_full(u_perm, dy_perm, hinit, g0, mats):
    S = u_perm.shape[0]
    rows = S5_TILE_STEPS * S5_SEG
    nt = S // rows

    def body(u_ref, dy_ref, hinit_ref, g0_ref, b_ref, c_ref, lr_ref, li_ref, lic_ref, d_ref,
             du_ref, dp_ref, dq_ref, db_ref, dc_ref, dd_ref, hs, gs, gcar):
        @pl.when(pl.program_id(0) == 0)
        def _():
            gcar[...] = g0_ref[...]
            for ref in (dp_ref, dq_ref, db_ref, dc_ref, dd_ref):
                ref[...] = jnp.zeros(ref.shape, F32)

        u, dy = u_ref[...], dy_ref[...]
        ub, dyb = u.astype(BF16), dy.astype(BF16)
        hs[0:S5_SEG, :] = hinit_ref[...]
        for q in range(S5_QUADS):
            cols = slice(q * S5_QCOLS, (q + 1) * S5_QCOLS)
            hs[S5_SEG:, cols] = jnp.dot(ub[:, q * 128:(q + 1) * 128], b_ref[q], preferred_element_type=F32)
            gs[:, cols] = lax.dot_general(dyb[:, q * 128:(q + 1) * 128], c_ref[q], (((1,), (1,)), ((), ())),
                                          preferred_element_type=F32)

        def fstep(t, h):
            sl = pl.ds(pl.multiple_of((t + 1) * S5_SEG, S5_SEG), S5_SEG)
            hn = lr_ref[...] * h + li_ref[...] * _swap_halves(h) + hs[sl, :]
            hs[sl, :] = hn
            return hn

        lax.fori_loop(0, S5_TILE_STEPS, fstep, hinit_ref[...])

        def bstep(k, g):
            t = S5_TILE_STEPS - 1 - k
            sl = pl.ds(pl.multiple_of(t * S5_SEG, S5_SEG), S5_SEG)
            gn = lr_ref[...] * g + lic_ref[...] * _swap_halves(g) + gs[sl, :]
            gs[sl, :] = gn
            return gn

        gcar[...] = lax.fori_loop(0, S5_TILE_STEPS, bstep, gcar[...])

        half = S5_QCOLS // 2
        dus = []
        for q in range(S5_QUADS):
            cols = slice(q * S5_QCOLS, (q + 1) * S5_QCOLS)

            def astep(t, carry, q=q):
                sl = pl.ds(pl.multiple_of(t * S5_SEG, S5_SEG), S5_SEG)
                g = gs[sl, q * S5_QCOLS:(q + 1) * S5_QCOLS]
                hp = hs[sl, q * S5_QCOLS:(q + 1) * S5_QCOLS]
                hp_sw = jnp.concatenate([hp[:, half:], hp[:, :half]], axis=1)
                return carry[0] + g * hp, carry[1] + g * hp_sw

            zero = jnp.zeros((S5_SEG, S5_QCOLS), F32)
            acc_p, acc_q = lax.fori_loop(0, S5_TILE_STEPS, astep, (zero, zero))
            dp_ref[:, cols] += jnp.sum(acc_p, axis=0, keepdims=True)
            dq_ref[:, cols] += jnp.sum(acc_q, axis=0, keepdims=True)
            gq = gs[:, cols].astype(BF16)
            db_ref[q] += lax.dot_general(ub[:, q * 128:(q + 1) * 128], gq, (((0,), (0,)), ((), ())),
                                         preferred_element_type=F32)
            hq = hs[S5_SEG:, cols].astype(BF16)
            dc_ref[q] += lax.dot_general(dyb[:, q * 128:(q + 1) * 128], hq, (((0,), (0,)), ((), ())),
                                         preferred_element_type=F32)
            dus.append(lax.dot_general(gq, b_ref[q], (((1,), (1,)), ((), ())), preferred_element_type=F32))
        du_ref[...] = (jnp.concatenate(dus, axis=1) + d_ref[...] * dy).astype(du_ref.dtype)
        dd_ref[...] += jnp.sum(dy * u, axis=0, keepdims=True)

    full = lambda a: pl.BlockSpec(a.shape, functools.partial(lambda i, nd: (0,) * nd, nd=a.ndim))
    rev_rows = pl.BlockSpec((rows, S5_WIDTH), lambda i: (nt - 1 - i, 0))
    consts = [mats["b_q"], mats["c_q"], mats["lam_r"], mats["lam_i"], mats["lam_ic"], mats["d_row"]]
    acc = lambda s: pl.BlockSpec(s, functools.partial(lambda i, nd: (0,) * nd, nd=len(s)))
    acc_shapes = [(1, S5_COLS), (1, S5_COLS), (S5_QUADS, 128, S5_QCOLS), (S5_QUADS, 128, S5_QCOLS), (1, S5_WIDTH)]
    return _call(
        body, name="s5_bwd_full", grid=(nt,),
        in_specs=[rev_rows, rev_rows, pl.BlockSpec((None, S5_SEG, S5_COLS), lambda i: (nt - 1 - i, 0, 0)), full(g0)]
        + [full(a) for a in consts],
        out_specs=[rev_rows] + [acc(s) for s in acc_shapes],
        out_shape=[jax.ShapeDtypeStruct((S, S5_WIDTH), BF16)] + [jax.ShapeDtypeStruct(s, F32) for s in acc_shapes],
        scratch_shapes=[pltpu.VMEM((rows + S5_SEG, S5_COLS), F32), pltpu.VMEM((rows, S5_COLS), F32),
                        pltpu.VMEM((S5_SEG, S5_COLS), F32)],
        compiler_params=_params(56),
    )(u_perm, dy_perm, hinit, g0, *consts)


def _cmul(ar, ai, br, bi):
    return ar * br - ai * bi, ar * bi + ai * br


def _split_cols(v):
    t = v.reshape(v.shape[0], S5_QUADS, 2, S5_QCOLS // 2)
    return t[:, :, 0], t[:, :, 1]


def _join_cols(re, im):
    return jnp.stack([re, im], axis=2).reshape(re.shape[0], S5_COLS)


def _segment_starts(ends, pow_r, pow_i, reverse):
    er, ei = _split_cols(ends)
    pi = -pow_i if reverse else pow_i
    order = list(range(S5_SEG))
    if reverse:
        order = order[::-1]
    cr, ci = jnp.zeros_like(er[0]), jnp.zeros_like(ei[0])
    out_r, out_i = [None] * S5_SEG, [None] * S5_SEG
    for j in order:
        out_r[j], out_i[j] = cr, ci
        mr, mi = _cmul(pow_r, pi, cr, ci)
        cr, ci = mr + er[j], mi + ei[j]
    return _join_cols(jnp.stack(out_r), jnp.stack(out_i))


def _to_segments(a):
    S, w = a.shape
    return a.reshape(S5_SEG, S // S5_SEG, w).transpose(1, 0, 2).reshape(S, w)


def _from_segments(a):
    S, w = a.shape
    return a.reshape(S // S5_SEG, S5_SEG, w).transpose(1, 0, 2).reshape(S, w)


def _my_pos():
    return lax.axis_index("x"), lax.axis_index("y"), lax.axis_index("c")


def _flip(pos, k):
    x, y, c = pos
    return (1 - x if k & 4 else x, 1 - y if k & 2 else y, 1 - c if k & 1 else c)


def _index_of(pos):
    return 4 * pos[0] + 2 * pos[1] + pos[2]


_GATHER_FLIPS = (0, 1, 4, 5, 2, 3, 6, 7)


def _inproj_gather(x, norm_g, pack_a, pack_b, pack_c):
    S = x.shape[0]
    tm = 512
    n_i = S // tm
    order = jnp.stack([_index_of(_flip(_my_pos(), k)) for k in _GATHER_FLIPS]).astype(jnp.int32)

    def body(order_ref, x_ref, g_ref, pa_ref, pb_ref, pc_ref, u_ref, proj_ref, oa_ref, ob_ref, oc_ref,
             wv, u_scr, send_sems, recv_sems, local_sems):
        s, i = pl.program_id(0), pl.program_id(1)
        me = _my_pos()
        mine = _index_of(me)
        sibling = _flip(me, 1)
        srcs = (pb_ref, pa_ref, pc_ref)
        dsts = (wv, oa_ref, oc_ref)

        def direct(a, k):
            return pltpu.make_async_remote_copy(
                src_ref=srcs[a], dst_ref=dsts[a].at[mine], send_sem=send_sems.at[a * 8 + k],
                recv_sem=recv_sems.at[a * 8 + k], device_id=_flip(me, k), device_id_type=MESH)

        def passed_on(a, k):
            slot = _index_of(_flip(me, k))
            return pltpu.make_async_remote_copy(
                src_ref=dsts[a].at[slot], dst_ref=dsts[a].at[slot], send_sem=send_sems.at[a * 8 + (k | 1)],
                recv_sem=recv_sems.at[a * 8 + (k | 1)], device_id=sibling, device_id_type=MESH)

        def arrival(a, k):
            slot = _index_of(_flip(me, k))
            pltpu.make_async_remote_copy(
                src_ref=dsts[a].at[slot], dst_ref=dsts[a].at[slot], send_sem=send_sems.at[a * 8 + k],
                recv_sem=recv_sems.at[a * 8 + k], device_id=me, device_id_type=MESH).wait_recv()

        def own_copy(a):
            return pltpu.make_async_copy(srcs[a], dsts[a].at[mine], local_sems.at[a])

        def keep(idx):
            slot = _index_of(_flip(me, _GATHER_FLIPS[idx]))
            return pltpu.make_async_copy(wv.at[slot], ob_ref.at[slot], local_sems.at[3 + idx])

        keep_u = pltpu.make_async_copy(u_scr, u_ref, local_sems.at[3 + N_DEV])

        first = (s == 0) & (i == 0)

        @pl.when(first)
        def _():
            for a in range(3):
                own_copy(a).start()
            for a in range(3):
                for k in (1, 4, 2, 6):
                    direct(a, k).start()
            own_copy(0).wait()
            keep(0).start()

        for idx, k in enumerate(_GATHER_FLIPS):
            if idx == 0:
                continue

            @pl.when((s == idx) & (i == 0))
            def _(idx=idx, k=k):
                arrival(0, k)
                if k in (4, 2, 6):
                    passed_on(0, k).start()
                keep(idx).start()
                if idx == 1:
                    keep_u.start()

        @pl.when(s == 0)
        def _():
            y, _, _ = _rms_fwd(x_ref[...], g_ref[...])
            u_scr[pl.ds(pl.multiple_of(i * tm, tm), tm), :] = y.astype(BF16)

        ub = u_scr[pl.ds(pl.multiple_of(i * tm, tm), tm), :]
        proj_ref[...] = jnp.dot(ub, wv[order_ref[s]], preferred_element_type=F32)

        @pl.when((s == N_DEV - 1) & (i == n_i - 1))
        def _():
            for a in (1, 2):
                for k in (4, 2, 6):
                    arrival(a, k)
                    passed_on(a, k).start()
            for a in (1, 2):
                for k in (1, 5, 3, 7):
                    arrival(a, k)
                own_copy(a).wait()
            for a in range(3):
                for k in (1, 4, 2, 6):
                    direct(a, k).wait_send()
                for k in (4, 2, 6):
                    passed_on(a, k).wait_send()
            for idx in range(N_DEV):
                keep(idx).wait()
            keep_u.wait()

    any_spec = pl.BlockSpec(memory_space=pl.ANY)
    vmem = pl.BlockSpec(memory_space=pltpu.VMEM)
    grid_spec = pltpu.PrefetchScalarGridSpec(
        num_scalar_prefetch=1, grid=(N_DEV, n_i),
        in_specs=[pl.BlockSpec((tm, D_MODEL), lambda s, i, o: (jnp.where(s == 0, i, 0), 0)),
                  pl.BlockSpec((1, D_MODEL), lambda s, i, o: (0, 0)), any_spec, vmem, any_spec],
        out_specs=[any_spec, pl.BlockSpec((tm, SHARD_IN), lambda s, i, o: (i, o[s])), any_spec, any_spec, any_spec],
        scratch_shapes=[pltpu.VMEM((N_DEV,) + pack_b.shape, BF16), pltpu.VMEM((S, D_MODEL), BF16),
                        pltpu.SemaphoreType.DMA((24,)), pltpu.SemaphoreType.DMA((24,)), pltpu.SemaphoreType.DMA((4 + N_DEV,))],
    )
    return _call(
        body, name="inproj_gather", grid_spec=grid_spec,
        out_shape=[jax.ShapeDtypeStruct((S, D_MODEL), BF16), jax.ShapeDtypeStruct((S, IN_COLS), F32),
                   jax.ShapeDtypeStruct((N_DEV,) + pack_a.shape, BF16), jax.ShapeDtypeStruct((N_DEV,) + pack_b.shape, BF16),
                   jax.ShapeDtypeStruct((N_DEV,) + pack_c.shape, BF16)],
        compiler_params=_params(56, 2),
    )(order, x, norm_g, pack_a, pack_b, pack_c)


_SCATTER_FLIPS = (7, 6, 5, 4, 3, 2, 1, 0)
_N_CHIPS = 4


def _for_row_chunks(n_rows, chunk, fn):
    def step(c, carry):
        fn(pl.ds(pl.multiple_of(c * chunk, chunk), chunk))
        return carry

    lax.fori_loop(0, n_rows // chunk, step, 0)


def _grad_w_in_scatter(dproj, u, rs_a, rs_c, small_partial):
    S = u.shape[0]
    tm = 512
    n_i = S // tm
    order = jnp.stack([_index_of(_flip(_my_pos(), k)) for k in _SCATTER_FLIPS]).astype(jnp.int32)
    shapes = ((D_MODEL, SHARD_IN), rs_a.shape[1:], rs_c.shape[1:])
    row_chunk = 128

    def body(order_ref, dp_ref, u_ref, ra_ref, rc_ref, p_ref, gb_ref, ga_ref, gc_ref, gs_ref, acc, sib_b, d2d_b,
             send_b, ici_b, mine_a, sib_a, ici_a, mine_c, sib_c, ici_c, gath, send_sems, recv_sems, local_sems):
        s, i = pl.program_id(0), pl.program_id(1)
        me = _my_pos()
        mine_idx = _index_of(me)
        sibling = _flip(me, 1)

        def small_to(k):
            return pltpu.make_async_remote_copy(
                src_ref=p_ref, dst_ref=gath.at[mine_idx], send_sem=send_sems.at[21 + k - 1],
                recv_sem=recv_sems.at[21 + k - 1], device_id=_flip(me, k), device_id_type=MESH)

        def small_from(k):
            pltpu.make_async_remote_copy(
                src_ref=p_ref, dst_ref=gath.at[_index_of(_flip(me, k))], send_sem=send_sems.at[21 + k - 1],
                recv_sem=recv_sems.at[21 + k - 1], device_id=me, device_id_type=MESH).wait_recv()
        sib = (sib_b, sib_a, sib_c)
        ici = (ici_b, ici_a, ici_c)
        outs = (gb_ref, ga_ref, gc_ref)

        def to_sibling(arr, m, src):
            return pltpu.make_async_remote_copy(
                src_ref=src, dst_ref=sib[arr].at[m], send_sem=send_sems.at[arr * 7 + m],
                recv_sem=recv_sems.at[arr * 7 + m], device_id=sibling, device_id_type=MESH)

        def over_ici(arr, m, src):
            return pltpu.make_async_remote_copy(
                src_ref=src, dst_ref=ici[arr].at[m], send_sem=send_sems.at[arr * 7 + 4 + m],
                recv_sem=recv_sems.at[arr * 7 + 4 + m], device_id=_flip(me, 6 - 2 * m), device_id_type=MESH)

        def from_sibling(arr, m):
            to_sibling(arr, m, sib[arr].at[m]).wait_recv()

        def from_ici(arr, m):
            over_ici(arr, m, ici[arr].at[m]).wait_recv()

        small = ((1, ra_ref, mine_a), (2, rc_ref, mine_c))

        def local_copy(arr, src, mine, m):
            return pltpu.make_async_copy(src.at[_index_of(_flip(me, 6 - 2 * m))], mine.at[m],
                                         local_sems.at[(arr - 1) * _N_CHIPS + m])

        @pl.when((s == 0) & (i == 0))
        def _():
            gath[mine_idx] = p_ref[...]
            for k in range(1, N_DEV):
                small_to(k).start()
            for arr, src, mine in small:
                for m in range(_N_CHIPS):
                    to_sibling(arr, m, src.at[_index_of(_flip(me, 7 - 2 * m))]).start()
                    local_copy(arr, src, mine, m).start()

        @pl.when((s == 1) & (i == 0))
        def _():
            for arr, src, mine in small:
                rows, chunk = shapes[arr][0], 16
                for m in range(_N_CHIPS):
                    local_copy(arr, src, mine, m).wait()
                    from_sibling(arr, m)
                    if m < _N_CHIPS - 1:
                        def add(sl, arr=arr, mine=mine, m=m):
                            mine[m, sl, :] = (mine[m, sl, :].astype(F32) + sib[arr][m, sl, :].astype(F32)).astype(BF16)

                        _for_row_chunks(rows, chunk, add)
                        over_ici(arr, m, mine.at[m]).start()
                    else:
                        def keep(sl, arr=arr, mine=mine, m=m):
                            outs[arr][sl, :] = mine[m, sl, :].astype(F32) + sib[arr][m, sl, :].astype(F32)

                        _for_row_chunks(rows, chunk, keep)

        @pl.when(i == 0)
        def _():
            acc[...] = jnp.zeros(acc.shape, F32)

        acc[...] += lax.dot_general(dp_ref[...], u_ref[...], (((0,), (0,)), ((), ())), preferred_element_type=F32)

        def block_rows(c):
            return acc[:, c * row_chunk:(c + 1) * row_chunk].T

        for m in range(_N_CHIPS):
            @pl.when((s == 2 * m) & (i == n_i - 1))
            def _(m=m):
                if m > 0:
                    to_sibling(0, m - 1, d2d_b).wait_send()
                for c in range(D_MODEL // row_chunk):
                    d2d_b[c * row_chunk:(c + 1) * row_chunk, :] = block_rows(c).astype(BF16)
                to_sibling(0, m, d2d_b).start()

            @pl.when((s == 2 * m + 1) & (i == n_i - 1))
            def _(m=m):
                from_sibling(0, m)
                slot = m % 2
                if m == 2:
                    over_ici(0, 0, send_b.at[0]).wait_send()
                for c in range(D_MODEL // row_chunk):
                    rows = slice(c * row_chunk, (c + 1) * row_chunk)
                    total = block_rows(c) + sib_b[m, rows, :].astype(F32)
                    if m < _N_CHIPS - 1:
                        send_b[slot, rows, :] = total.astype(BF16)
                    else:
                        gb_ref[rows, :] = total
                if m < _N_CHIPS - 1:
                    over_ici(0, m, send_b.at[slot]).start()

        @pl.when((s == N_DEV - 1) & (i == n_i - 1))
        def _():
            for arr in range(3):
                for m in range(_N_CHIPS - 1):
                    from_ici(arr, m)
                rows = shapes[arr][0]

                def add(sl, arr=arr):
                    outs[arr][sl, :] = (outs[arr][sl, :] + ici[arr][0, sl, :].astype(F32)
                                        + ici[arr][1, sl, :].astype(F32) + ici[arr][2, sl, :].astype(F32))

                _for_row_chunks(rows, 16, add)
            for k in range(1, N_DEV):
                small_from(k)
            total = gath[0]
            for dev in range(1, N_DEV):
                total = total + gath[dev]
            gs_ref[...] = total
            for k in range(1, N_DEV):
                small_to(k).wait_send()
            to_sibling(0, _N_CHIPS - 1, d2d_b).wait_send()
            over_ici(0, 1, send_b.at[1]).wait_send()
            over_ici(0, 2, send_b.at[0]).wait_send()
            for arr, src, mine in small:
                for m in range(_N_CHIPS):
                    to_sibling(arr, m, src.at[0]).wait_send()
                for m in range(_N_CHIPS - 1):
                    over_ici(arr, m, mine.at[m]).wait_send()

    any_spec = pl.BlockSpec(memory_space=pl.ANY)
    vmem = pl.BlockSpec(memory_space=pltpu.VMEM)
    half = lambda shp, n: pltpu.VMEM((n,) + tuple(shp), BF16)
    grid_spec = pltpu.PrefetchScalarGridSpec(
        num_scalar_prefetch=1, grid=(N_DEV, n_i),
        in_specs=[pl.BlockSpec((tm, SHARD_IN), lambda s, i, o: (i, o[s])),
                  pl.BlockSpec((tm, D_MODEL), lambda s, i, o: (i, 0)), any_spec, any_spec, vmem],
        out_specs=[vmem, vmem, vmem, vmem],
        scratch_shapes=[
            pltpu.VMEM((SHARD_IN, D_MODEL), F32), half(shapes[0], _N_CHIPS), pltpu.VMEM(shapes[0], BF16),
            half(shapes[0], 2), half(shapes[0], _N_CHIPS - 1),
            half(shapes[1], _N_CHIPS), half(shapes[1], _N_CHIPS), half(shapes[1], _N_CHIPS - 1),
            half(shapes[2], _N_CHIPS), half(shapes[2], _N_CHIPS), half(shapes[2], _N_CHIPS - 1),
            pltpu.VMEM((N_DEV,) + small_partial.shape, F32),
            pltpu.SemaphoreType.DMA((28,)), pltpu.SemaphoreType.DMA((28,)), pltpu.SemaphoreType.DMA((2 * _N_CHIPS,))],
    )
    return _call(
        body, name="grad_w_in_scatter", grid_spec=grid_spec,
        out_shape=[jax.ShapeDtypeStruct(shp, F32) for shp in shapes] + [jax.ShapeDtypeStruct(small_partial.shape, F32)],
        compiler_params=_params(60, 2),
    )(order, dproj, u, rs_a, rs_c, small_partial)


def _adam_update(g, w, m, v):
    m2 = ADAM_B1 * m + (1.0 - ADAM_B1) * g
    v2 = ADAM_B2 * v + (1.0 - ADAM_B2) * (g * g)
    m_hat = m2 / (1.0 - ADAM_B1 ** ADAM_STEP)
    v_hat = v2 / (1.0 - ADAM_B2 ** ADAM_STEP)
    delta = -ADAM_LR * (m_hat / (jnp.sqrt(v_hat) + ADAM_EPS) + ADAM_WD * w)
    return delta, m2, v2


def _adam_rows(g, w, m, v):
    rows, cols = w.shape
    tm = rows if rows % 256 else 256

    def fn(rv, cr, out):
        return list(_adam_update(*rv)), []

    outs, _ = _rowwise("adamw", fn, rows, tm, [(a, cols, 0) for a in (g, w, m, v)], [], [(cols, F32)] * 3, [], 32)
    return outs


_SMALL = ["norm_g", "hg_lb", "hg_norm_g", "s5_a_re", "s5_a_im", "s5_log_dt", "s5_b_re", "s5_b_im", "s5_c_re",
          "s5_c_im", "s5_d", "b_glu", "ple_norm_g", "final_norm_g"]
_BIG = ["w_in", "w_o_hg", "w_glu", "w_o_s5", "w_out", "w_ple", "w_ple_gate"]
_ORDER = ["norm_g", "w_in", "hg_lb", "hg_norm_g", "w_o_hg", "s5_a_re", "s5_a_im", "s5_log_dt", "s5_b_re", "s5_b_im",
          "s5_c_re", "s5_c_im", "s5_d", "w_glu", "b_glu", "w_o_s5", "w_out", "ple_norm_g", "w_ple", "w_ple_gate",
          "final_norm_g"]


def _pack_small(vals):
    parts = []
    for name in _SMALL:
        flat = vals[name].reshape(-1).astype(F32)
        pad = (-flat.shape[0]) % 1024
        parts.append(jnp.pad(flat, (0, pad)))
    return jnp.concatenate(parts).reshape(-1, 128)


def _unpack_small(packed, like):
    flat = packed.reshape(-1)
    out, off = {}, 0
    for name in _SMALL:
        size = like[name].size
        out[name] = flat[off:off + size].reshape(like[name].shape)
        off += size + (-size) % 1024
    return out


def _col_blocks(full):
    k = full.shape[0]
    return full.reshape(k, N_DEV, 128).transpose(1, 0, 2)


def _from_col_blocks(blocks):
    k = blocks.shape[1]
    return blocks.transpose(1, 0, 2).reshape(k, N_DEV * 128)


def kernel(x, p, norm_g, w_in, hg_lb, hg_norm_g, w_o_hg, s5_a_re, s5_a_im, s5_log_dt, s5_b_re, s5_b_im, s5_c_re, s5_c_im, s5_d, w_glu, b_glu, w_o_s5, w_out, ple_norm_g, w_ple, w_ple_gate, final_norm_g, loss_target, m_norm_g, m_w_in, m_hg_lb, m_hg_norm_g, m_w_o_hg, m_s5_a_re, m_s5_a_im, m_s5_log_dt, m_s5_b_re, m_s5_b_im, m_s5_c_re, m_s5_c_im, m_s5_d, m_w_glu, m_b_glu, m_w_o_s5, m_w_out, m_ple_norm_g, m_w_ple, m_w_ple_gate, m_final_norm_g, v_norm_g, v_w_in, v_hg_lb, v_hg_norm_g, v_w_o_hg, v_s5_a_re, v_s5_a_im, v_s5_log_dt, v_s5_b_re, v_s5_b_im, v_s5_c_re, v_s5_c_im, v_s5_d, v_w_glu, v_b_glu, v_w_o_s5, v_w_out, v_ple_norm_g, v_w_ple, v_w_ple_gate, v_final_norm_g):
    args = dict(locals())
    w = {n: args[n] for n in _ORDER}
    m = {n: args["m_" + n] for n in _ORDER}
    v = {n: args["v_" + n] for n in _ORDER}
    xs = x[0]
    ps = p[0, 0]
    tgt = loss_target[0]
    S = xs.shape[0]

    pack_a = jnp.concatenate([w_o_hg[0], w_out[0], w_ple_gate[0]], axis=0).astype(BF16)
    pack_b = w_in[0].astype(BF16)
    pack_c = jnp.concatenate([w_glu[0], w_o_s5[0], w_ple[0]], axis=0).astype(BF16)
    u, proj, all_a, all_b, all_c = _inproj_gather(xs, norm_g, pack_a, pack_b, pack_c)
    wf_o_hg = all_a[:, 0:128].reshape(D_MODEL, D_MODEL)
    wf_out = all_a[:, 128:256].reshape(D_MODEL, D_MODEL)
    wf_pg = all_a[:, 256:384].reshape(D_MODEL, D_MODEL)
    wf_glu = _from_col_blocks(all_c[:, 0:512])
    wf_o_s5 = _from_col_blocks(all_c[:, 512:1024])
    wf_ple = _from_col_blocks(all_c[:, 1024:1280])

    lb = jax.nn.sigmoid(hg_lb[0:1] - hg_lb[1:2])
    s5_names = ["s5_a_re", "s5_a_im", "s5_log_dt", "s5_b_re", "s5_b_im", "s5_c_re", "s5_c_im", "s5_d"]
    build = lambda *a: _s5_matrices(*a, seg_len=S // S5_SEG)
    mats_f32, mats_vjp = jax.vjp(build, *[w[n][0] for n in s5_names])
    mats = dict(mats_f32, b_q=mats_f32["b_q"].astype(BF16), c_q=mats_f32["c_q"].astype(BF16))
    bias_glu = b_glu

    o, states = _hgrn_fwd(proj, lb)
    u_perm = _to_segments(proj[:, COL_US:COL_US + S5_WIDTH])
    zeros_state = jnp.zeros((S5_SEG, S5_COLS), F32)
    (h_ends,) = _s5_fwd_pass(u_perm, mats, zeros_state, False)
    h0 = _segment_starts(h_ends, mats["pow_r"], mats["pow_i"], False)
    y_perm, h_init, _ = _s5_fwd_pass(u_perm, mats, h0, True)
    ys = _from_segments(y_perm)
    y_hg, y_s5, glu, h1 = _stage_branches(o, proj, ys, xs, hg_norm_g, wf_o_hg, wf_glu, bias_glu, wf_o_s5, wf_out)

    dh1, (loss_acc, d_final_g, d_ple_g, d_w_ple, d_w_pg) = _stage_ple_loss(
        h1, ps, tgt, ple_norm_g, wf_ple, wf_pg, final_norm_g.reshape(1, D_MODEL))
    (d_o, d_g_hg, d_gate_hg, d_gate_s5, d_ys5), (d_w_out, d_w_o_hg, d_hg_norm) = _stage_bwd_merge_hg(
        dh1, y_hg, y_s5, proj, o, hg_norm_g, wf_out, wf_o_hg)
    (d_ys, d_z), (d_w_o_s5, d_w_glu, d_b_glu) = _stage_bwd_s5_path(d_ys5, ys, glu, proj, wf_o_s5, wf_glu)
    dq, df, div, d_lb = _hgrn_bwd(proj, lb, d_o, states)
    dy_perm = _to_segments(d_ys)
    g_ends = _s5_bwd_ends(dy_perm, mats)
    g0 = _segment_starts(g_ends, mats["pow_r"], mats["pow_i"], True)
    du_perm, acc_p, acc_q, d_bq, d_cq_t, d_d = _s5_bwd_full(u_perm, dy_perm, h_init, g0, mats)
    d_us = _from_segments(du_perm)
    grad_x, dproj, d_norm_g = _stage_inproj_bwd([dq, df, div, d_g_hg, d_us, d_z, d_gate_hg, d_gate_s5], xs, dh1,
                                                norm_g, all_b)

    p_re, p_im = _split_cols(acc_p)
    q_re, q_im = _split_cols(acc_q)
    d_lam_r = (p_re + p_im)[0]
    d_lam_i = (q_im - q_re)[0]
    zero_row = jnp.zeros((S5_SEG, S5_COLS), F32)
    row_of = lambda re_part, im_part: zero_row.at[0].set(_join_cols(re_part[None], im_part[None])[0])
    zeros_q = jnp.zeros_like(d_lam_r)
    cot = dict(
        lam_r=row_of(d_lam_r, zeros_q), lam_i=row_of(zeros_q, d_lam_i), lam_ic=zero_row,
        b_q=d_bq, c_q=d_cq_t.transpose(0, 2, 1), d_row=d_d,
        pow_r=jnp.zeros_like(mats["pow_r"]), pow_i=jnp.zeros_like(mats["pow_i"]),
    )
    d_s5 = mats_vjp(cot)

    s_lb = lb * (1.0 - lb)
    d_hg_lb = jnp.concatenate([d_lb * s_lb, -d_lb * s_lb], axis=0)
    small_g = dict(norm_g=d_norm_g, hg_lb=d_hg_lb, hg_norm_g=d_hg_norm, b_glu=d_b_glu, ple_norm_g=d_ple_g,
                   final_norm_g=d_final_g)
    for name, g in zip(s5_names, d_s5):
        small_g[name] = g
    pk = lambda d: _pack_small({n: d[n] for n in _SMALL})
    rs_a = jnp.concatenate([d_w_o_hg.reshape(N_DEV, 128, D_MODEL), d_w_out.reshape(N_DEV, 128, D_MODEL),
                            d_w_pg.reshape(N_DEV, 128, D_MODEL)], axis=1).astype(BF16)
    rs_c = jnp.concatenate([_col_blocks(d_w_glu), _col_blocks(d_w_o_s5), _col_blocks(d_w_ple)], axis=1).astype(BF16)
    g_b, g_a, g_c, sg = _grad_w_in_scatter(dproj, u, rs_a, rs_c, pk(small_g))
    sd, sm, sv = _adam_rows(sg, pk(w), pk(m), pk(v))
    like = {n: w[n] for n in _SMALL}
    out_g, out_d, out_m, out_v = (_unpack_small(t, like) for t in (sg, sd, sm, sv))
    big_g = dict(w_o_hg=g_a[0:128], w_out=g_a[128:256], w_ple_gate=g_a[256:384], w_in=g_b,
                 w_glu=g_c[0:512], w_o_s5=g_c[512:1024], w_ple=g_c[1024:1280])
    for name in _BIG:
        shape = w[name].shape
        g2 = big_g[name]
        d2, m2, v2 = _adam_rows(g2, w[name][0], m[name][0], v[name][0])
        out_g[name], out_d[name], out_m[name], out_v[name] = (t.reshape(shape) for t in (g2, d2, m2, v2))

    loss = lax.psum(loss_acc[0, 0], ("x", "y", "c"))
    return (loss, grad_x[None], *[out_g[n] for n in _ORDER], *[out_d[n] for n in _ORDER],
            *[out_m[n] for n in _ORDER], *[out_v[n] for n in _ORDER])
```

```python
import functools
import math

import jax
import jax.numpy as jnp
from jax import lax
from jax.experimental import pallas as pl
from jax.experimental.pallas import tpu as pltpu

F32 = jnp.float32
BF16 = jnp.bfloat16

D_MODEL = 1024
N_DEV = 8
IN_COLS = 7168
SHARD_IN = IN_COLS // N_DEV
HG_HEADS = 8
HG_DIM = 128
HG_CHUNK = 64
HG_SUPER = 256
HG_HEADS_PER_STEP = 4
S5_WIDTH = 512
S5_GROUPS = 32
S5_STATE = 64
S5_CH = 16
S5_SEG = 8
S5_QUADS = 4
S5_QCOLS = 1024
S5_COLS = S5_QUADS * S5_QCOLS
S5_TILE_STEPS = 64
S5_UNROLL = 8
NORM_EPS = 1e-6
ADAM_LR = 0.001
ADAM_B1 = 0.9
ADAM_B2 = 0.999
ADAM_EPS = 1e-08
ADAM_WD = 0.01
ADAM_STEP = 10
MIB = 1024 * 1024
MESH = pl.DeviceIdType.MESH

COL_Q, COL_F, COL_I, COL_G, COL_US, COL_ZS, COL_GH, COL_GS = 0, 1024, 2048, 3072, 4096, 4608, 5120, 6144


def _call(body, **kw):
    return pl.pallas_call(body, **kw)


def _params(vmem_mb, n_grid=1):
    return pltpu.CompilerParams(
        dimension_semantics=("arbitrary",) * n_grid, vmem_limit_bytes=vmem_mb * MIB
    )


def _bdot(a, b):
    return jnp.dot(a.astype(BF16), b.astype(BF16), preferred_element_type=F32)


def _bdot_nt(a, b):
    return lax.dot_general(a.astype(BF16), b.astype(BF16), (((1,), (1,)), ((), ())), preferred_element_type=F32)


def _bdot_tn(a, b):
    return lax.dot_general(a.astype(BF16), b.astype(BF16), (((0,), (0,)), ((), ())), preferred_element_type=F32)


def _sigmoid(x):
    return jax.nn.sigmoid(x)


def _silu(x):
    return x * _sigmoid(x)


def _dsilu(x):
    s = _sigmoid(x)
    return s * (1.0 + x * (1.0 - s))


_GELU_C = math.sqrt(2.0 / math.pi)


def _gelu(x):
    return 0.5 * x * (1.0 + jnp.tanh(_GELU_C * (x + 0.044715 * x * x * x)))


def _dgelu(x):
    t = jnp.tanh(_GELU_C * (x + 0.044715 * x * x * x))
    return 0.5 * (1.0 + t) + 0.5 * x * (1.0 - t * t) * _GELU_C * (1.0 + 3.0 * 0.044715 * x * x)


def _rms_fwd(x, g):
    r = lax.rsqrt(jnp.mean(x * x, axis=-1, keepdims=True) + NORM_EPS)
    n = x * r
    return n * g, n, r


def _rms_bwd(dy, n, r, g):
    dn = dy * g
    dx = r * (dn - n * jnp.mean(dn * n, axis=-1, keepdims=True))
    return dx, jnp.sum(dy * n, axis=0, keepdims=True)


def _head_rms_fwd(o, g):
    ns, rs = [], []
    for h in range(HG_HEADS):
        oh = o[:, h * HG_DIM:(h + 1) * HG_DIM]
        r = lax.rsqrt(jnp.mean(oh * oh, axis=-1, keepdims=True) + NORM_EPS)
        ns.append(oh * r)
        rs.append(r)
    n = jnp.concatenate(ns, axis=1)
    return n * g, n, rs


def _head_rms_bwd(dy, n, rs, g):
    dn = dy * g
    dxs = []
    for h in range(HG_HEADS):
        sl = slice(h * HG_DIM, (h + 1) * HG_DIM)
        dxs.append(rs[h] * (dn[:, sl] - n[:, sl] * jnp.mean(dn[:, sl] * n[:, sl], axis=-1, keepdims=True)))
    return jnp.concatenate(dxs, axis=1), jnp.sum(dy * n, axis=0, keepdims=True)


def _rowwise(name, fn, n_rows, tm, rows, consts, out_rows, out_accs, vmem_mb, parts=1):
    n_r, n_c, n_or, n_oa = len(rows), len(consts), len(out_rows), len(out_accs)
    tp = tm // parts

    def body(*refs):
        r_refs = refs[:n_r]
        c_refs = refs[n_r:n_r + n_c]
        or_refs = refs[n_r + n_c:n_r + n_c + n_or]
        oa_refs = refs[n_r + n_c + n_or:]

        if n_oa:
            @pl.when(pl.program_id(0) == 0)
            def _():
                for ref in oa_refs:
                    ref[...] = jnp.zeros(ref.shape, ref.dtype)

        for part in range(parts):
            sl = slice(part * tp, (part + 1) * tp)
            outs, accs = fn([r[sl, :] for r in r_refs], c_refs, [o.at[sl, :] for o in or_refs])
            for ref, v in zip(or_refs, outs):
                if v is not None:
                    ref[sl, :] = v.astype(ref.dtype)
            for ref, v in zip(oa_refs, accs):
                ref[...] += v.astype(ref.dtype)

    in_specs = [pl.BlockSpec((tm, w), functools.partial(lambda i, c: (i, c), c=cb)) for (_, w, cb) in rows]
    in_specs += [pl.BlockSpec(c.shape, functools.partial(lambda i, nd: (0,) * nd, nd=c.ndim),
                              pipeline_mode=pl.Buffered(1)) for c in consts]
    out_specs = [pl.BlockSpec((tm, w), lambda i: (i, 0)) for (w, _) in out_rows]
    out_specs += [pl.BlockSpec(s, functools.partial(lambda i, nd: (0,) * nd, nd=len(s))) for (s, _) in out_accs]
    out_shape = [jax.ShapeDtypeStruct((n_rows, w), dt) for (w, dt) in out_rows]
    out_shape += [jax.ShapeDtypeStruct(s, dt) for (s, dt) in out_accs]
    res = _call(
        body, name=name, grid=(n_rows // tm,), in_specs=in_specs, out_specs=out_specs, out_shape=out_shape,
        compiler_params=_params(vmem_mb),
    )(*[a for (a, _, _) in rows], *consts)
    return res[:n_or], res[n_or:]


def _stage_branches(o, proj, ys, x, hg_norm_g, w_o_hg, w_glu, b_glu, w_o_s5, w_out):
    S = x.shape[0]

    def fn(rv, cr, out):
        o_b, g_hg, z_s, gate_hg, gate_s5, ys_b, x_b = rv
        gn_ref, wohg_ref, wglu_ref, bglu_ref, wos5_ref, wout_ref = cr
        on, _, _ = _head_rms_fwd(o_b, gn_ref[...])
        a = on * _silu(g_hg)
        y_hg = jnp.dot(a.astype(BF16), wohg_ref[...], preferred_element_type=F32)
        gl = _gelu(ys_b)
        glu = jnp.dot(gl.astype(BF16), wglu_ref[...], preferred_element_type=F32) + bglu_ref[...]
        ys2 = glu[:, :S5_WIDTH] * _sigmoid(glu[:, S5_WIDTH:]) * _silu(z_s)
        y_s5 = jnp.dot(ys2.astype(BF16), wos5_ref[...], preferred_element_type=F32)
        merged = _sigmoid(gate_hg) * y_hg + _sigmoid(gate_s5) * y_s5
        h1 = x_b + jnp.dot(merged.astype(BF16), wout_ref[...], preferred_element_type=F32)
        return [y_hg, y_s5, glu, h1], []

    rows = [(o, D_MODEL, 0), (proj, D_MODEL, COL_G // D_MODEL), (proj, S5_WIDTH, COL_ZS // S5_WIDTH),
            (proj, D_MODEL, COL_GH // D_MODEL), (proj, D_MODEL, COL_GS // D_MODEL), (ys, S5_WIDTH, 0), (x, D_MODEL, 0)]
    (y_hg, y_s5, glu, h1), _ = _rowwise(
        "branches", fn, S, 256, rows, [hg_norm_g, w_o_hg, w_glu, b_glu, w_o_s5, w_out],
        [(D_MODEL, F32)] * 4, [], 56)
    return y_hg, y_s5, glu, h1


def _stage_ple_loss(h1, p, target, ple_norm_g, w_ple, w_ple_gate, final_norm_g):
    S = h1.shape[0]

    def fn(rv, cr, out):
        h1_b, p_b, t_b = rv
        gp_ref, wple_ref, wpg_ref, gf_ref = cr
        n2g, n2, r2 = _rms_fwd(h1_b, gp_ref[...])
        z = jnp.dot(n2g.astype(BF16), wpg_ref[...], preferred_element_type=F32)
        gate = _sigmoid(z)
        pe = jnp.dot(p_b.astype(BF16), wple_ref[...], preferred_element_type=F32)
        h2 = h1_b + pe * gate
        y, nf, rf = _rms_fwd(h2, gf_ref[...])
        err = y - t_b
        loss_rows = 0.5 * jnp.mean(err * err, axis=-1, keepdims=True)
        loss_inc = jnp.broadcast_to(jnp.sum(loss_rows, axis=0, keepdims=True), (1, 128))
        dy = err * (1.0 / D_MODEL)
        dh2, d_gf = _rms_bwd(dy, nf, rf, gf_ref[...])
        d_pe = dh2 * gate
        dz = dh2 * pe * gate * (1.0 - gate)
        d_wple = _bdot_tn(p_b, d_pe)
        d_wpg = _bdot_tn(n2g, dz)
        dn2g = _bdot_nt(dz, wpg_ref[...])
        dh1n, d_gp = _rms_bwd(dn2g, n2, r2, gp_ref[...])
        return [dh2 + dh1n], [loss_inc, d_gf, d_gp, d_wple, d_wpg]

    (dh1,), accs = _rowwise(
        "ple_loss", fn, S, 512, [(h1, D_MODEL, 0), (p, 256, 0), (target, D_MODEL, 0)],
        [ple_norm_g, w_ple, w_ple_gate, final_norm_g], [(D_MODEL, F32)],
        [((1, 128), F32), ((1, D_MODEL), F32), ((1, D_MODEL), F32), ((256, D_MODEL), F32), ((D_MODEL, D_MODEL), F32)], 56,
        parts=2)
    return dh1, accs


def _stage_bwd_merge(dh1, y_hg, y_s5, proj, w_out):
    S = dh1.shape[0]

    def fn(rv, cr, out):
        dh1_b, yhg, ys5, gate_hg, gate_s5 = rv
        (wout_ref,) = cr
        sg_h, sg_s = _sigmoid(gate_hg), _sigmoid(gate_s5)
        merged = sg_h * yhg + sg_s * ys5
        d_wout = _bdot_tn(merged, dh1_b)
        d_merged = _bdot_nt(dh1_b, wout_ref[...])
        d_gate_hg = d_merged * yhg * sg_h * (1.0 - sg_h)
        d_gate_s5 = d_merged * ys5 * sg_s * (1.0 - sg_s)
        return [d_gate_hg, d_gate_s5, d_merged * sg_h, d_merged * sg_s], [d_wout]

    rows = [(dh1, D_MODEL, 0), (y_hg, D_MODEL, 0), (y_s5, D_MODEL, 0), (proj, D_MODEL, COL_GH // D_MODEL),
            (proj, D_MODEL, COL_GS // D_MODEL)]
    outs, accs = _rowwise("bwd_merge", fn, S, 512, rows, [w_out], [(D_MODEL, BF16)] * 4,
                          [((D_MODEL, D_MODEL), F32)], 56, parts=2)
    return outs, accs


def _stage_bwd_hg_path(d_yhg, o, proj, hg_norm_g, w_o_hg):
    S = o.shape[0]

    def fn(rv, cr, out):
        d_yhg_b, o_b, g_hg = rv
        gn_ref, wohg_ref = cr
        ong, on, rs = _head_rms_fwd(o_b, gn_ref[...])
        sil = _silu(g_hg)
        d_wohg = _bdot_tn(ong * sil, d_yhg_b)
        d_a = _bdot_nt(d_yhg_b, wohg_ref[...])
        d_g_hg = d_a * ong * _dsilu(g_hg)
        d_o, d_gn = _head_rms_bwd(d_a * sil, on, rs, gn_ref[...])
        return [d_o, d_g_hg], [d_wohg, d_gn]

    rows = [(d_yhg, D_MODEL, 0), (o, D_MODEL, 0), (proj, D_MODEL, COL_G // D_MODEL)]
    outs, accs = _rowwise("bwd_hg_path", fn, S, 512, rows, [hg_norm_g, w_o_hg], [(D_MODEL, BF16)] * 2,
                          [((D_MODEL, D_MODEL), F32), ((1, D_MODEL), F32)], 56, parts=2)
    return outs, accs


def _stage_bwd_s5_path(d_ys5, ys, glu, proj, w_o_s5, w_glu):
    S = ys.shape[0]

    def fn(rv, cr, out):
        d_ys5_b, ys_b, glu_b, z_s = rv
        wos5_ref, wglu_ref = cr
        ga, gb = glu_b[:, :S5_WIDTH], glu_b[:, S5_WIDTH:]
        sgb, silz = _sigmoid(gb), _silu(z_s)
        ys2 = ga * sgb * silz
        d_wos5 = _bdot_tn(ys2, d_ys5_b)
        d_ys2 = _bdot_nt(d_ys5_b, wos5_ref[...])
        d_ga = d_ys2 * sgb * silz
        d_gb = d_ys2 * ga * sgb * (1.0 - sgb) * silz
        d_z = d_ys2 * ga * sgb * _dsilu(z_s)
        d_glu = jnp.concatenate([d_ga, d_gb], axis=1)
        gl = _gelu(ys_b)
        d_wglu = _bdot_tn(gl, d_glu)
        d_bglu = jnp.sum(d_glu, axis=0, keepdims=True)
        d_gl = _bdot_nt(d_glu, wglu_ref[...])
        return [d_gl * _dgelu(ys_b), d_z], [d_wos5, d_wglu, d_bglu]

    rows = [(d_ys5, D_MODEL, 0), (ys, S5_WIDTH, 0), (glu, D_MODEL, 0), (proj, S5_WIDTH, COL_ZS // S5_WIDTH)]
    outs, accs = _rowwise(
        "bwd_s5_path", fn, S, 512, rows, [w_o_s5, w_glu], [(S5_WIDTH, F32), (S5_WIDTH, BF16)],
        [((S5_WIDTH, D_MODEL), F32), ((S5_WIDTH, D_MODEL), F32), ((1, D_MODEL), F32)], 48, parts=2)
    return outs, accs


def _stage_inproj_bwd(pieces, x, dh1, norm_g, w_in_all):
    S = x.shape[0]

    def fn(rv, cr, out):
        g_ref, w_ref = cr
        x_b, dh1_b = rv[8], rv[9]
        dproj_ref = out[1]
        col = 0
        for v in rv[:8]:
            dproj_ref[:, col:col + v.shape[1]] = v.astype(BF16)
            col += v.shape[1]
        d_u = jnp.zeros((x_b.shape[0], D_MODEL), F32)
        for j in range(N_DEV):
            d_u = d_u + lax.dot_general(dproj_ref[:, j * SHARD_IN:(j + 1) * SHARD_IN], w_ref[j],
                                        (((1,), (1,)), ((), ())), preferred_element_type=F32)
        _, n, r = _rms_fwd(x_b, g_ref[...])
        dx, d_g = _rms_bwd(d_u, n, r, g_ref[...])
        return [dh1_b + dx, None], [d_g]

    rows = [(a, a.shape[1], 0) for a in pieces] + [(x, D_MODEL, 0), (dh1, D_MODEL, 0)]
    (grad_x, dproj), (d_g,) = _rowwise(
        "inproj_bwd", fn, S, 256, rows, [norm_g, w_in_all], [(D_MODEL, F32), (IN_COLS, BF16)],
        [((1, D_MODEL), F32)], 56)
    return grad_x, dproj, d_g


def _chunk_row(shape):
    return lax.broadcasted_iota(jnp.int32, shape, 0) & (HG_CHUNK - 1)


def _chunk_cumsum(x):
    r_in = _chunk_row(x.shape)
    s = 1
    while s < HG_CHUNK:
        x = x + jnp.where(r_in >= s, pltpu.roll(x, s, 0), 0.0)
        s *= 2
    return x


def _chunk_suffix_sum(x):
    n = x.shape[0]
    r_in = _chunk_row(x.shape)
    s = 1
    while s < HG_CHUNK:
        x = x + jnp.where(r_in < HG_CHUNK - s, pltpu.roll(x, n - s, 0), 0.0)
        s *= 2
    return x


def _hgrn_prep(q, fl, lb):
    nc = HG_SUPER // HG_CHUNK
    sig = _sigmoid(fl)
    f = lb + (1.0 - lb) * sig
    k = (1.0 - lb) * (1.0 - sig)
    b = _chunk_cumsum(jnp.log(f))
    b3 = b.reshape(nc, HG_CHUNK, HG_DIM)
    row3 = lax.broadcasted_iota(jnp.int32, b3.shape, 1)
    pick = lambda r: jnp.sum(jnp.where(row3 == r, b3, 0.0), axis=1, keepdims=True)
    b_mid = pick(HG_CHUNK // 2 - 1)
    b_last = pick(HG_CHUNK - 1)
    flat = lambda t: t.reshape(HG_SUPER, HG_DIM)
    e_qa = flat(jnp.exp(b3 - b_mid))
    e_ka = flat(jnp.exp(b_mid - b3))
    e_qd = jnp.exp(b)
    e_kd = flat(jnp.exp(b_last - b3))
    dc = jnp.exp(b_last)
    return sig, f, k, e_qa, e_ka, e_qd, e_kd, dc


def _hgrn_mask():
    r = lax.broadcasted_iota(jnp.int32, (HG_SUPER, HG_SUPER), 0)
    c = lax.broadcasted_iota(jnp.int32, (HG_SUPER, HG_SUPER), 1)
    shift = HG_CHUNK.bit_length() - 1
    return (jnp.right_shift(r, shift) == jnp.right_shift(c, shift)) & (r >= c)


def _hgrn_fwd(proj, lb):
    S = proj.shape[0]
    nb = S // HG_SUPER
    nc = HG_SUPER // HG_CHUNK
    hp = HG_HEADS_PER_STEP
    wide = hp * HG_DIM

    def body(q_ref, f_ref, iv_ref, lb_ref, o_ref, st_ref, state):
        @pl.when(pl.program_id(1) == 0)
        def _():
            state[...] = jnp.zeros(state.shape, F32)

        mask = _hgrn_mask()
        for hh in range(hp):
            lanes = slice(hh * HG_DIM, (hh + 1) * HG_DIM)
            q, iv = q_ref[:, lanes], iv_ref[:, lanes]
            _, _, k, e_qa, e_ka, e_qd, e_kd, dc = _hgrn_prep(q, f_ref[:, lanes], lb_ref[:, lanes])
            scores = jnp.where(mask, _bdot_nt(q * e_qa, k * e_ka), 0.0)
            o_intra = _bdot(scores, iv)
            qd, kd = q * e_qd, k * e_kd
            for c in range(nc):
                sl = slice(c * HG_CHUNK, (c + 1) * HG_CHUNK)
                st = state[hh]
                st_ref[hh, c] = st
                o_ref[sl, lanes] = o_intra[sl] + _bdot_nt(qd[sl], st)
                state[hh] = dc[c] * st + _bdot_tn(iv[sl], kd[sl])

    blk = lambda base: pl.BlockSpec((HG_SUPER, wide), functools.partial(lambda h, i, b: (i, b + h), b=base // wide))
    return _call(
        body, name="hgrn_fwd", grid=(HG_HEADS // hp, nb),
        in_specs=[blk(COL_Q), blk(COL_F), blk(COL_I), pl.BlockSpec((1, wide), lambda h, i: (0, h))],
        out_specs=[pl.BlockSpec((HG_SUPER, wide), lambda h, i: (i, h)),
                   pl.BlockSpec((hp, nc, HG_DIM, HG_DIM), lambda h, i: (h, i, 0, 0))],
        out_shape=[jax.ShapeDtypeStruct((S, D_MODEL), F32),
                   jax.ShapeDtypeStruct((HG_HEADS, S // HG_CHUNK, HG_DIM, HG_DIM), F32)],
        scratch_shapes=[pltpu.VMEM((hp, HG_DIM, HG_DIM), F32)],
        compiler_params=_params(40, 2),
    )(proj, proj, proj, lb)


def _hgrn_bwd(proj, lb, d_o, states):
    S = proj.shape[0]
    nb = S // HG_SUPER
    nc = HG_SUPER // HG_CHUNK
    hp = HG_HEADS_PER_STEP
    wide = hp * HG_DIM

    def body(q_ref, f_ref, iv_ref, lb_ref, do_ref, st_ref, dq_ref, df_ref, div_ref, dlb_ref, dstate):
        @pl.when(pl.program_id(1) == 0)
        def _():
            dstate[...] = jnp.zeros(dstate.shape, F32)
            dlb_ref[...] = jnp.zeros(dlb_ref.shape, F32)

        mask = _hgrn_mask()
        for hh in range(hp):
            lanes = slice(hh * HG_DIM, (hh + 1) * HG_DIM)
            q, iv, do, lb_v = q_ref[:, lanes], iv_ref[:, lanes], do_ref[:, lanes], lb_ref[:, lanes]
            sig, f, k, e_qa, e_ka, e_qd, e_kd, dc = _hgrn_prep(q, f_ref[:, lanes], lb_v)
            qa, ka, qd, kd = q * e_qa, k * e_ka, q * e_qd, k * e_kd
            scores = jnp.where(mask, _bdot_nt(qa, ka), 0.0)
            d_scores = jnp.where(mask, _bdot_nt(do, iv), 0.0)
            d_iv_intra = _bdot_tn(scores, do)
            d_qa = _bdot(d_scores, ka)
            d_ka = _bdot_tn(d_scores, qa)
            d_qd, d_kd, d_last = [None] * nc, [None] * nc, [None] * nc
            for c in reversed(range(nc)):
                sl = slice(c * HG_CHUNK, (c + 1) * HG_CHUNK)
                st = st_ref[hh, c]
                ds = dstate[hh]
                d_qd[c] = _bdot(do[sl], st)
                d_kd[c] = _bdot(iv[sl], ds)
                div_ref[sl, lanes] = (d_iv_intra[sl] + _bdot_nt(kd[sl], ds)).astype(div_ref.dtype)
                d_last[c] = (jnp.sum(ds * st, axis=0, keepdims=True) * dc[c]
                             + jnp.sum(d_kd[c] * kd[sl], axis=0, keepdims=True))
                dstate[hh] = dc[c] * ds + _bdot_tn(do[sl], qd[sl])
            d_qd = jnp.concatenate(d_qd, axis=0)
            d_kd = jnp.concatenate(d_kd, axis=0)
            d_b = d_qa * qa - d_ka * ka + d_qd * qd - d_kd * kd
            last_rows = jnp.concatenate([jnp.broadcast_to(t, (HG_CHUNK, HG_DIM)) for t in d_last], axis=0)
            d_b = d_b + jnp.where(_chunk_row(d_b.shape) == HG_CHUNK - 1, last_rows, 0.0)
            d_logf = _chunk_suffix_sum(d_b)
            d_k = d_ka * e_ka + d_kd * e_kd
            g_f = d_logf / f
            d_sig = (g_f - d_k) * (1.0 - lb_v)
            dq_ref[:, lanes] = (d_qa * e_qa + d_qd * e_qd).astype(dq_ref.dtype)
            df_ref[:, lanes] = (d_sig * sig * (1.0 - sig)).astype(df_ref.dtype)
            d_lb = jnp.sum((g_f - d_k) * (1.0 - sig), axis=0, keepdims=True)
            dlb_ref[:, lanes] += jnp.broadcast_to(d_lb, (8, HG_DIM))

    rev = lambda i: nb - 1 - i
    blk = lambda base: pl.BlockSpec((HG_SUPER, wide), functools.partial(lambda h, i, b: (rev(i), b + h), b=base // wide))
    row_out = pl.BlockSpec((HG_SUPER, wide), lambda h, i: (rev(i), h))
    dq, df, div, dlb = _call(
        body, name="hgrn_bwd", grid=(HG_HEADS // hp, nb),
        in_specs=[blk(COL_Q), blk(COL_F), blk(COL_I), pl.BlockSpec((1, wide), lambda h, i: (0, h)),
                  pl.BlockSpec((HG_SUPER, wide), lambda h, i: (rev(i), h)),
                  pl.BlockSpec((hp, nc, HG_DIM, HG_DIM), lambda h, i: (h, rev(i), 0, 0))],
        out_specs=[row_out, row_out, row_out, pl.BlockSpec((8, wide), lambda h, i: (0, h))],
        out_shape=[jax.ShapeDtypeStruct((S, D_MODEL), BF16)] * 3 + [jax.ShapeDtypeStruct((8, D_MODEL), F32)],
        scratch_shapes=[pltpu.VMEM((hp, HG_DIM, HG_DIM), F32)],
        compiler_params=_params(40, 2),
    )(proj, proj, proj, lb, d_o, states)
    return dq, df, div, dlb[0:1]


def _s5_matrices(a_re, a_im, log_dt, b_re, b_im, c_re, c_im, d, seg_len):
    dt = jnp.exp(log_dt)[:, None]
    mag = jnp.exp(a_re * dt)
    lr, li = mag * jnp.cos(a_im * dt), mag * jnp.sin(a_im * dt)
    den = a_re * a_re + a_im * a_im
    nr = lr - 1.0
    sr = (nr * a_re + li * a_im) / den
    si = (li * a_re - nr * a_im) / den
    bbr = sr[..., None] * b_re - si[..., None] * b_im
    bbi = sr[..., None] * b_im + si[..., None] * b_re
    eye = jnp.eye(8, dtype=F32)

    def quad_cols(v):
        return v.reshape(S5_QUADS, 8 * S5_STATE)

    def lam_row(re_part, im_part):
        row = jnp.concatenate([quad_cols(re_part), quad_cols(im_part)], axis=1).reshape(1, S5_COLS)
        return jnp.broadcast_to(row, (S5_SEG, S5_COLS))

    def b_mat(bb):
        t = bb.reshape(S5_QUADS, 8, S5_STATE, S5_CH)
        return jnp.einsum("qgnc,gh->qgchn", t, eye).reshape(S5_QUADS, 8 * S5_CH, 8 * S5_STATE)

    def c_mat(cc):
        t = cc.reshape(S5_QUADS, 8, S5_CH, S5_STATE)
        return jnp.einsum("qgcn,gh->qgnhc", t, eye).reshape(S5_QUADS, 8 * S5_STATE, 8 * S5_CH)

    ang = a_im * dt * seg_len
    magp = jnp.exp(a_re * dt * seg_len)
    lpr, lpi = magp * jnp.cos(ang), magp * jnp.sin(ang)
    return dict(
        lam_r=lam_row(lr, lr), lam_i=lam_row(-li, li),
        b_q=jnp.concatenate([b_mat(bbr), b_mat(bbi)], axis=2),
        c_q=jnp.concatenate([c_mat(c_re), -c_mat(c_im)], axis=1),
        d_row=d.reshape(1, S5_WIDTH), pow_r=quad_cols(lpr), pow_i=quad_cols(lpi),
    )


def _s5_parts(v):
    half = S5_QCOLS // 2
    return tuple(v[:, k * half:(k + 1) * half] for k in range(2 * S5_QUADS))


def _s5_advance(parts, lr_ref, li_ref, x_ref, sl, conj):
    half = S5_QCOLS // 2
    out = []
    for q in range(S5_QUADS):
        re_c = slice(q * S5_QCOLS, q * S5_QCOLS + half)
        im_c = slice(q * S5_QCOLS + half, (q + 1) * S5_QCOLS)
        lr, li = lr_ref[:, re_c], li_ref[:, im_c]
        hr, hi = parts[2 * q], parts[2 * q + 1]
        if conj:
            out += [lr * hr + li * hi + x_ref[sl, re_c], lr * hi - li * hr + x_ref[sl, im_c]]
        else:
            out += [lr * hr - li * hi + x_ref[sl, re_c], lr * hi + li * hr + x_ref[sl, im_c]]
    return tuple(out)


def _scan_loop(step, init):
    def trip(o, carry):
        for j in range(S5_UNROLL):
            carry = step(o * S5_UNROLL + j, carry)
        return carry

    return lax.fori_loop(0, S5_TILE_STEPS // S5_UNROLL, trip, init)


def _s5_store(ref, sl, parts):
    half = S5_QCOLS // 2
    for k, v in enumerate(parts):
        ref[sl, k * half:(k + 1) * half] = v


def _s5_fwd_pass(u_perm, mats, h0, with_output):
    S = u_perm.shape[0]
    rows = S5_TILE_STEPS * S5_SEG
    nt = S // rows

    def body(*refs):
        if with_output:
            u_ref, b_ref, lr_ref, li_ref, h0_ref, c_ref, d_ref, y_ref, hinit_ref, hend_ref, xs, hcar = refs
        else:
            u_ref, b_ref, lr_ref, li_ref, h0_ref, hend_ref, xs, hcar = refs

        @pl.when(pl.program_id(0) == 0)
        def _():
            hcar[...] = h0_ref[...]

        if with_output:
            hinit_ref[...] = hcar[...]
        u = u_ref[...]
        ub = u.astype(BF16)
        for q in range(S5_QUADS):
            xs[:, q * S5_QCOLS:(q + 1) * S5_QCOLS] = jnp.dot(ub[:, q * 128:(q + 1) * 128], b_ref[q], preferred_element_type=F32)

        def step(t, h):
            sl = pl.ds(pl.multiple_of(t * S5_SEG, S5_SEG), S5_SEG)
            hn = _s5_advance(h, lr_ref, li_ref, xs, sl, False)
            _s5_store(xs, sl, hn)
            return hn

        h = _scan_loop(step, _s5_parts(hcar[...]))
        _s5_store(hcar, slice(None), h)
        _s5_store(hend_ref, slice(None), h)
        if with_output:
            ys = [jnp.dot(xs[:, q * S5_QCOLS:(q + 1) * S5_QCOLS].astype(BF16), c_ref[q], preferred_element_type=F32)
                  for q in range(S5_QUADS)]
            y_ref[...] = jnp.concatenate(ys, axis=1) + d_ref[...] * u

    full = lambda a: pl.BlockSpec(a.shape, functools.partial(lambda i, nd: (0,) * nd, nd=a.ndim))
    ins = [u_perm, mats["b_q"], mats["lam_r"], mats["lam_i"], h0]
    in_specs = [pl.BlockSpec((rows, S5_WIDTH), lambda i: (i, 0))] + [full(a) for a in ins[1:]]
    out_specs = [pl.BlockSpec((S5_SEG, S5_COLS), lambda i: (0, 0))]
    out_shape = [jax.ShapeDtypeStruct((S5_SEG, S5_COLS), F32)]
    if with_output:
        ins += [mats["c_q"], mats["d_row"]]
        in_specs += [full(mats["c_q"]), full(mats["d_row"])]
        out_specs = [pl.BlockSpec((rows, S5_WIDTH), lambda i: (i, 0)),
                     pl.BlockSpec((None, S5_SEG, S5_COLS), lambda i: (i, 0, 0))] + out_specs
        out_shape = [jax.ShapeDtypeStruct((S, S5_WIDTH), F32), jax.ShapeDtypeStruct((nt, S5_SEG, S5_COLS), F32)] + out_shape
    return _call(
        body, name="s5_fwd_y" if with_output else "s5_fwd_ends", grid=(nt,), in_specs=in_specs, out_specs=out_specs,
        out_shape=out_shape,
        scratch_shapes=[pltpu.VMEM((rows, S5_COLS), F32), pltpu.VMEM((S5_SEG, S5_COLS), F32)],
        compiler_params=_params(40),
    )(*ins)


def _s5_bwd_ends(dy_perm, mats):
    S = dy_perm.shape[0]
    rows = S5_TILE_STEPS * S5_SEG
    nt = S // rows

    def body(dy_ref, c_ref, lr_ref, li_ref, gend_ref, gs, gcar):
        @pl.when(pl.program_id(0) == 0)
        def _():
            gcar[...] = jnp.zeros(gcar.shape, F32)

        dyb = dy_ref[...].astype(BF16)
        for q in range(S5_QUADS):
            gs[:, q * S5_QCOLS:(q + 1) * S5_QCOLS] = lax.dot_general(
                dyb[:, q * 128:(q + 1) * 128], c_ref[q], (((1,), (1,)), ((), ())), preferred_element_type=F32)

        def step(k, g):
            t = S5_TILE_STEPS - 1 - k
            sl = pl.ds(pl.multiple_of(t * S5_SEG, S5_SEG), S5_SEG)
            return _s5_advance(g, lr_ref, li_ref, gs, sl, True)

        g = _scan_loop(step, _s5_parts(gcar[...]))
        _s5_store(gcar, slice(None), g)
        _s5_store(gend_ref, slice(None), g)

    full = lambda a: pl.BlockSpec(a.shape, functools.partial(lambda i, nd: (0,) * nd, nd=a.ndim))
    return _call(
        body, name="s5_bwd_ends", grid=(nt,),
        in_specs=[pl.BlockSpec((rows, S5_WIDTH), lambda i: (nt - 1 - i, 0)), full(mats["c_q"]), full(mats["lam_r"]),
                  full(mats["lam_i"])],
        out_specs=pl.BlockSpec((S5_SEG, S5_COLS), lambda i: (0, 0)),
        out_shape=jax.ShapeDtypeStruct((S5_SEG, S5_COLS), F32),
        scratch_shapes=[pltpu.VMEM((rows, S5_COLS), F32), pltpu.VMEM((S5_SEG, S5_COLS), F32)],
        compiler_params=_params(40),
    )(dy_perm, mats["c_q"], mats["lam_r"], mats["lam_i"])


def _s5_bwd_full(u_perm, dy_perm, hinit, g0, mats):
    S = u_perm.shape[0]
    rows = S5_TILE_STEPS * S5_SEG
    nt = S // rows

    def body(u_ref, dy_ref, hinit_ref, g0_ref, b_ref, c_ref, lr_ref, li_ref, d_ref,
             du_ref, dp_ref, dq_ref, db_ref, dc_ref, dd_ref, hs, gs, gcar):
        @pl.when(pl.program_id(0) == 0)
        def _():
            gcar[...] = g0_ref[...]
            for ref in (dp_ref, dq_ref, db_ref, dc_ref, dd_ref):
                ref[...] = jnp.zeros(ref.shape, F32)

        u, dy = u_ref[...], dy_ref[...]
        ub, dyb = u.astype(BF16), dy.astype(BF16)
        hs[0:S5_SEG, :] = hinit_ref[...]
        for q in range(S5_QUADS):
            cols = slice(q * S5_QCOLS, (q + 1) * S5_QCOLS)
            hs[S5_SEG:, cols] = jnp.dot(ub[:, q * 128:(q + 1) * 128], b_ref[q], preferred_element_type=F32)
            gs[:, cols] = lax.dot_general(dyb[:, q * 128:(q + 1) * 128], c_ref[q], (((1,), (1,)), ((), ())),
                                          preferred_element_type=F32)

        def fstep(t, h):
            sl = pl.ds(pl.multiple_of((t + 1) * S5_SEG, S5_SEG), S5_SEG)
            hn = _s5_advance(h, lr_ref, li_ref, hs, sl, False)
            _s5_store(hs, sl, hn)
            return hn

        _scan_loop(fstep, _s5_parts(hinit_ref[...]))

        def bstep(k, g):
            t = S5_TILE_STEPS - 1 - k
            sl = pl.ds(pl.multiple_of(t * S5_SEG, S5_SEG), S5_SEG)
            gn = _s5_advance(g, lr_ref, li_ref, gs, sl, True)
            _s5_store(gs, sl, gn)
            return gn

        _s5_store(gcar, slice(None), _scan_loop(bstep, _s5_parts(gcar[...])))

        half = S5_QCOLS // 2
        dus = []
        for q in range(S5_QUADS):
            cols = slice(q * S5_QCOLS, (q + 1) * S5_QCOLS)

            def astep(t, carry, q=q):
                sl = pl.ds(pl.multiple_of(t * S5_SEG, S5_SEG), S5_SEG)
                g = gs[sl, q * S5_QCOLS:(q + 1) * S5_QCOLS]
                hp = hs[sl, q * S5_QCOLS:(q + 1) * S5_QCOLS]
                hp_sw = jnp.concatenate([hp[:, half:], hp[:, :half]], axis=1)
                return carry[0] + g * hp, carry[1] + g * hp_sw

            zero = jnp.zeros((S5_SEG, S5_QCOLS), F32)
            acc_p, acc_q = _scan_loop(astep, (zero, zero))
            dp_ref[:, cols] += jnp.sum(acc_p, axis=0, keepdims=True)
            dq_ref[:, cols] += jnp.sum(acc_q, axis=0, keepdims=True)
            gq = gs[:, cols].astype(BF16)
            db_ref[q] += lax.dot_general(ub[:, q * 128:(q + 1) * 128], gq, (((0,), (0,)), ((), ())),
                                         preferred_element_type=F32)
            hq = hs[S5_SEG:, cols].astype(BF16)
            dc_ref[q] += lax.dot_general(dyb[:, q * 128:(q + 1) * 128], hq, (((0,), (0,)), ((), ())),
                                         preferred_element_type=F32)
            dus.append(lax.dot_general(gq, b_ref[q], (((1,), (1,)), ((), ())), preferred_element_type=F32))
        du_ref[...] = (jnp.concatenate(dus, axis=1) + d_ref[...] * dy).astype(du_ref.dtype)
        dd_ref[...] += jnp.sum(dy * u, axis=0, keepdims=True)

    full = lambda a: pl.BlockSpec(a.shape, functools.partial(lambda i, nd: (0,) * nd, nd=a.ndim))
    rev_rows = pl.BlockSpec((rows, S5_WIDTH), lambda i: (nt - 1 - i, 0))
    consts = [mats["b_q"], mats["c_q"], mats["lam_r"], mats["lam_i"], mats["d_row"]]
    acc = lambda s: pl.BlockSpec(s, functools.partial(lambda i, nd: (0,) * nd, nd=len(s)))
    acc_shapes = [(1, S5_COLS), (1, S5_COLS), (S5_QUADS, 128, S5_QCOLS), (S5_QUADS, 128, S5_QCOLS), (1, S5_WIDTH)]
    return _call(
        body, name="s5_bwd_full", grid=(nt,),
        in_specs=[rev_rows, rev_rows, pl.BlockSpec((None, S5_SEG, S5_COLS), lambda i: (nt - 1 - i, 0, 0)), full(g0)]
        + [full(a) for a in consts],
        out_specs=[rev_rows] + [acc(s) for s in acc_shapes],
        out_shape=[jax.ShapeDtypeStruct((S, S5_WIDTH), BF16)] + [jax.ShapeDtypeStruct(s, F32) for s in acc_shapes],
        scratch_shapes=[pltpu.VMEM((rows + S5_SEG, S5_COLS), F32), pltpu.VMEM((rows, S5_COLS), F32),
                        pltpu.VMEM((S5_SEG, S5_COLS), F32)],
        compiler_params=_params(56),
    )(u_perm, dy_perm, hinit, g0, *consts)


def _cmul(ar, ai, br, bi):
    return ar * br - ai * bi, ar * bi + ai * br


def _split_cols(v):
    t = v.reshape(v.shape[0], S5_QUADS, 2, S5_QCOLS // 2)
    return t[:, :, 0], t[:, :, 1]


def _join_cols(re, im):
    return jnp.stack([re, im], axis=2).reshape(re.shape[0], S5_COLS)


def _segment_starts(ends, pow_r, pow_i, reverse):
    er, ei = _split_cols(ends)
    pi = -pow_i if reverse else pow_i
    order = list(range(S5_SEG))
    if reverse:
        order = order[::-1]
    cr, ci = jnp.zeros_like(er[0]), jnp.zeros_like(ei[0])
    out_r, out_i = [None] * S5_SEG, [None] * S5_SEG
    for j in order:
        out_r[j], out_i[j] = cr, ci
        mr, mi = _cmul(pow_r, pi, cr, ci)
        cr, ci = mr + er[j], mi + ei[j]
    return _join_cols(jnp.stack(out_r), jnp.stack(out_i))


def _to_segments(a):
    S, w = a.shape
    return a.reshape(S5_SEG, S // S5_SEG, w).transpose(1, 0, 2).reshape(S, w)


def _from_segments(a):
    S, w = a.shape
    return a.reshape(S // S5_SEG, S5_SEG, w).transpose(1, 0, 2).reshape(S, w)


def _my_pos():
    return lax.axis_index("x"), lax.axis_index("y"), lax.axis_index("c")


def _flip(pos, k):
    x, y, c = pos
    return (1 - x if k & 4 else x, 1 - y if k & 2 else y, 1 - c if k & 1 else c)


def _index_of(pos):
    return 4 * pos[0] + 2 * pos[1] + pos[2]


_GATHER_FLIPS = (0, 1, 4, 5, 2, 3, 6, 7)


def _inproj_gather(x, norm_g, pack_a, pack_b, pack_c):
    S = x.shape[0]
    tm = 512
    n_i = S // tm
    order = jnp.stack([_index_of(_flip(_my_pos(), k)) for k in _GATHER_FLIPS]).astype(jnp.int32)

    def body(order_ref, x_ref, g_ref, pa_ref, pb_ref, pc_ref, u_ref, proj_ref, oa_ref, ob_ref, oc_ref,
             wv, u_scr, send_sems, recv_sems, local_sems):
        s, i = pl.program_id(0), pl.program_id(1)
        me = _my_pos()
        mine = _index_of(me)
        sibling = _flip(me, 1)
        srcs = (pb_ref, pa_ref, pc_ref)
        dsts = (wv, oa_ref, oc_ref)

        def direct(a, k):
            return pltpu.make_async_remote_copy(
                src_ref=srcs[a], dst_ref=dsts[a].at[mine], send_sem=send_sems.at[a * 8 + k],
                recv_sem=recv_sems.at[a * 8 + k], device_id=_flip(me, k), device_id_type=MESH)

        def passed_on(a, k):
            slot = _index_of(_flip(me, k))
            return pltpu.make_async_remote_copy(
                src_ref=dsts[a].at[slot], dst_ref=dsts[a].at[slot], send_sem=send_sems.at[a * 8 + (k | 1)],
                recv_sem=recv_sems.at[a * 8 + (k | 1)], device_id=sibling, device_id_type=MESH)

        def arrival(a, k):
            slot = _index_of(_flip(me, k))
            pltpu.make_async_remote_copy(
                src_ref=dsts[a].at[slot], dst_ref=dsts[a].at[slot], send_sem=send_sems.at[a * 8 + k],
                recv_sem=recv_sems.at[a * 8 + k], device_id=me, device_id_type=MESH).wait_recv()

        def own_copy(a):
            return pltpu.make_async_copy(srcs[a], dsts[a].at[mine], local_sems.at[a])

        def keep(idx):
            slot = _index_of(_flip(me, _GATHER_FLIPS[idx]))
            return pltpu.make_async_copy(wv.at[slot], ob_ref.at[slot], local_sems.at[3 + idx])

        keep_u = pltpu.make_async_copy(u_scr, u_ref, local_sems.at[3 + N_DEV])

        first = (s == 0) & (i == 0)

        @pl.when(first)
        def _():
            for a in range(3):
                own_copy(a).start()
            for k in (1, 4, 2):
                direct(0, k).start()
            own_copy(0).wait()
            keep(0).start()

        for idx, k in enumerate(_GATHER_FLIPS):
            if idx == 0:
                continue

            @pl.when((s == idx) & (i == 0))
            def _(idx=idx, k=k):
                arrival(0, k)
                if k in (4, 2, 6):
                    passed_on(0, k).start()
                keep(idx).start()
                if idx == 1:
                    keep_u.start()
                    direct(0, 6).start()
                if idx == 2:
                    for a in (1, 2):
                        for k in (1, 4, 2, 6):
                            direct(a, k).start()

        @pl.when(s == 0)
        def _():
            y, _, _ = _rms_fwd(x_ref[...], g_ref[...])
            u_scr[pl.ds(pl.multiple_of(i * tm, tm), tm), :] = y.astype(BF16)

        ub = u_scr[pl.ds(pl.multiple_of(i * tm, tm), tm), :]
        proj_ref[...] = jnp.dot(ub, wv[order_ref[s]], preferred_element_type=F32)

        @pl.when((s == N_DEV - 1) & (i == n_i - 1))
        def _():
            for a in (1, 2):
                for k in (4, 2, 6):
                    arrival(a, k)
                    passed_on(a, k).start()
            for a in (1, 2):
                for k in (1, 5, 3, 7):
                    arrival(a, k)
                own_copy(a).wait()
            for a in range(3):
                for k in (1, 4, 2, 6):
                    direct(a, k).wait_send()
                for k in (4, 2, 6):
                    passed_on(a, k).wait_send()
            for idx in range(N_DEV):
                keep(idx).wait()
            keep_u.wait()

    any_spec = pl.BlockSpec(memory_space=pl.ANY)
    vmem = pl.BlockSpec(memory_space=pltpu.VMEM)
    grid_spec = pltpu.PrefetchScalarGridSpec(
        num_scalar_prefetch=1, grid=(N_DEV, n_i),
        in_specs=[pl.BlockSpec((tm, D_MODEL), lambda s, i, o: (jnp.where(s == 0, i, 0), 0)),
                  pl.BlockSpec((1, D_MODEL), lambda s, i, o: (0, 0)), any_spec, vmem, any_spec],
        out_specs=[any_spec, pl.BlockSpec((tm, SHARD_IN), lambda s, i, o: (i, o[s])), any_spec, any_spec, any_spec],
        scratch_shapes=[pltpu.VMEM((N_DEV,) + pack_b.shape, BF16), pltpu.VMEM((S, D_MODEL), BF16),
                        pltpu.SemaphoreType.DMA((24,)), pltpu.SemaphoreType.DMA((24,)), pltpu.SemaphoreType.DMA((4 + N_DEV,))],
    )
    return _call(
        body, name="inproj_gather", grid_spec=grid_spec,
        out_shape=[jax.ShapeDtypeStruct((S, D_MODEL), BF16), jax.ShapeDtypeStruct((S, IN_COLS), F32),
                   jax.ShapeDtypeStruct((N_DEV,) + pack_a.shape, BF16), jax.ShapeDtypeStruct((N_DEV,) + pack_b.shape, BF16),
                   jax.ShapeDtypeStruct((N_DEV,) + pack_c.shape, BF16)],
        compiler_params=_params(56, 2),
    )(order, x, norm_g, pack_a, pack_b, pack_c)


_SCATTER_FLIPS = (7, 6, 5, 4, 3, 2, 1, 0)
_N_CHIPS = 4


def _for_row_chunks(n_rows, chunk, fn):
    def step(c, carry):
        fn(pl.ds(pl.multiple_of(c * chunk, chunk), chunk))
        return carry

    lax.fori_loop(0, n_rows // chunk, step, 0)


def _grad_w_in_scatter(dproj, u, rs_a, rs_c, small_partial):
    S = u.shape[0]
    tm = 512
    n_i = S // tm
    order = jnp.stack([_index_of(_flip(_my_pos(), k)) for k in _SCATTER_FLIPS]).astype(jnp.int32)
    shapes = ((D_MODEL, SHARD_IN), rs_a.shape[1:], rs_c.shape[1:])
    row_chunk = 128

    def body(order_ref, dp_ref, u_ref, ra_ref, rc_ref, p_ref, gb_ref, ga_ref, gc_ref, gs_ref, acc, sib_b, d2d_b,
             send_b, ici_b, mine_a, sib_a, ici_a, mine_c, sib_c, ici_c, gath, send_sems, recv_sems, local_sems):
        s, i = pl.program_id(0), pl.program_id(1)
        me = _my_pos()
        mine_idx = _index_of(me)
        sibling = _flip(me, 1)

        def small_to(k):
            return pltpu.make_async_remote_copy(
                src_ref=p_ref, dst_ref=gath.at[mine_idx], send_sem=send_sems.at[21 + k - 1],
                recv_sem=recv_sems.at[21 + k - 1], device_id=_flip(me, k), device_id_type=MESH)

        def small_from(k):
            pltpu.make_async_remote_copy(
                src_ref=p_ref, dst_ref=gath.at[_index_of(_flip(me, k))], send_sem=send_sems.at[21 + k - 1],
                recv_sem=recv_sems.at[21 + k - 1], device_id=me, device_id_type=MESH).wait_recv()
        sib = (sib_b, sib_a, sib_c)
        ici = (ici_b, ici_a, ici_c)
        outs = (gb_ref, ga_ref, gc_ref)

        def to_sibling(arr, m, src):
            return pltpu.make_async_remote_copy(
                src_ref=src, dst_ref=sib[arr].at[m], send_sem=send_sems.at[arr * 7 + m],
                recv_sem=recv_sems.at[arr * 7 + m], device_id=sibling, device_id_type=MESH)

        def over_ici(arr, m, src):
            return pltpu.make_async_remote_copy(
                src_ref=src, dst_ref=ici[arr].at[m], send_sem=send_sems.at[arr * 7 + 4 + m],
                recv_sem=recv_sems.at[arr * 7 + 4 + m], device_id=_flip(me, 6 - 2 * m), device_id_type=MESH)

        def from_sibling(arr, m):
            to_sibling(arr, m, sib[arr].at[m]).wait_recv()

        def from_ici(arr, m):
            over_ici(arr, m, ici[arr].at[m]).wait_recv()

        small = ((1, ra_ref, mine_a), (2, rc_ref, mine_c))

        def local_copy(arr, src, mine, m):
            return pltpu.make_async_copy(src.at[_index_of(_flip(me, 6 - 2 * m))], mine.at[m],
                                         local_sems.at[(arr - 1) * _N_CHIPS + m])

        @pl.when((s == 0) & (i == 0))
        def _():
            gath[mine_idx] = p_ref[...]
            for k in range(1, N_DEV):
                small_to(k).start()
            for arr, src, mine in small:
                for m in range(_N_CHIPS):
                    to_sibling(arr, m, src.at[_index_of(_flip(me, 7 - 2 * m))]).start()
                    local_copy(arr, src, mine, m).start()

        @pl.when((s == 1) & (i == 0))
        def _():
            for arr, src, mine in small:
                rows, chunk = shapes[arr][0], 16
                for m in range(_N_CHIPS):
                    local_copy(arr, src, mine, m).wait()
                    from_sibling(arr, m)
                    if m < _N_CHIPS - 1:
                        def add(sl, arr=arr, mine=mine, m=m):
                            mine[m, sl, :] = (mine[m, sl, :].astype(F32) + sib[arr][m, sl, :].astype(F32)).astype(BF16)

                        _for_row_chunks(rows, chunk, add)
                        over_ici(arr, m, mine.at[m]).start()
                    else:
                        def keep(sl, arr=arr, mine=mine, m=m):
                            outs[arr][sl, :] = mine[m, sl, :].astype(F32) + sib[arr][m, sl, :].astype(F32)

                        _for_row_chunks(rows, chunk, keep)

        @pl.when(i == 0)
        def _():
            acc[...] = jnp.zeros(acc.shape, F32)

        acc[...] += lax.dot_general(dp_ref[...], u_ref[...], (((0,), (0,)), ((), ())), preferred_element_type=F32)

        def block_rows(c):
            return acc[:, c * row_chunk:(c + 1) * row_chunk].T

        for m in range(_N_CHIPS):
            @pl.when((s == 2 * m) & (i == n_i - 1))
            def _(m=m):
                if m > 0:
                    to_sibling(0, m - 1, d2d_b).wait_send()
                for c in range(D_MODEL // row_chunk):
                    d2d_b[c * row_chunk:(c + 1) * row_chunk, :] = block_rows(c).astype(BF16)
                to_sibling(0, m, d2d_b).start()

            @pl.when((s == 2 * m + 1) & (i == n_i - 1))
            def _(m=m):
                from_sibling(0, m)
                slot = m % 2
                if m == 2:
                    over_ici(0, 0, send_b.at[0]).wait_send()
                for c in range(D_MODEL // row_chunk):
                    rows = slice(c * row_chunk, (c + 1) * row_chunk)
                    total = block_rows(c) + sib_b[m, rows, :].astype(F32)
                    if m < _N_CHIPS - 1:
                        send_b[slot, rows, :] = total.astype(BF16)
                    else:
                        gb_ref[rows, :] = total
                if m < _N_CHIPS - 1:
                    over_ici(0, m, send_b.at[slot]).start()

        @pl.when((s == N_DEV - 1) & (i == n_i - 1))
        def _():
            for arr in range(3):
                for m in range(_N_CHIPS - 1):
                    from_ici(arr, m)
                rows = shapes[arr][0]

                def add(sl, arr=arr):
                    outs[arr][sl, :] = (outs[arr][sl, :] + ici[arr][0, sl, :].astype(F32)
                                        + ici[arr][1, sl, :].astype(F32) + ici[arr][2, sl, :].astype(F32))

                _for_row_chunks(rows, 16, add)
            for k in range(1, N_DEV):
                small_from(k)
            total = gath[0]
            for dev in range(1, N_DEV):
                total = total + gath[dev]
            gs_ref[...] = total
            for k in range(1, N_DEV):
                small_to(k).wait_send()
            to_sibling(0, _N_CHIPS - 1, d2d_b).wait_send()
            over_ici(0, 1, send_b.at[1]).wait_send()
            over_ici(0, 2, send_b.at[0]).wait_send()
            for arr, src, mine in small:
                for m in range(_N_CHIPS):
                    to_sibling(arr, m, src.at[0]).wait_send()
                for m in range(_N_CHIPS - 1):
                    over_ici(arr, m, mine.at[m]).wait_send()

    any_spec = pl.BlockSpec(memory_space=pl.ANY)
    vmem = pl.BlockSpec(memory_space=pltpu.VMEM)
    half = lambda shp, n: pltpu.VMEM((n,) + tuple(shp), BF16)
    grid_spec = pltpu.PrefetchScalarGridSpec(
        num_scalar_prefetch=1, grid=(N_DEV, n_i),
        in_specs=[pl.BlockSpec((tm, SHARD_IN), lambda s, i, o: (i, o[s])),
                  pl.BlockSpec((tm, D_MODEL), lambda s, i, o: (i, 0)), any_spec, any_spec, vmem],
        out_specs=[vmem, vmem, vmem, vmem],
        scratch_shapes=[
            pltpu.VMEM((SHARD_IN, D_MODEL), F32), half(shapes[0], _N_CHIPS), pltpu.VMEM(shapes[0], BF16),
            half(shapes[0], 2), half(shapes[0], _N_CHIPS - 1),
            half(shapes[1], _N_CHIPS), half(shapes[1], _N_CHIPS), half(shapes[1], _N_CHIPS - 1),
            half(shapes[2], _N_CHIPS), half(shapes[2], _N_CHIPS), half(shapes[2], _N_CHIPS - 1),
            pltpu.VMEM((N_DEV,) + small_partial.shape, F32),
            pltpu.SemaphoreType.DMA((28,)), pltpu.SemaphoreType.DMA((28,)), pltpu.SemaphoreType.DMA((2 * _N_CHIPS,))],
    )
    return _call(
        body, name="grad_w_in_scatter", grid_spec=grid_spec,
        out_shape=[jax.ShapeDtypeStruct(shp, F32) for shp in shapes] + [jax.ShapeDtypeStruct(small_partial.shape, F32)],
        compiler_params=_params(60, 2),
    )(order, dproj, u, rs_a, rs_c, small_partial)


def _adam_update(g, w, m, v):
    m2 = ADAM_B1 * m + (1.0 - ADAM_B1) * g
    v2 = ADAM_B2 * v + (1.0 - ADAM_B2) * (g * g)
    m_hat = m2 / (1.0 - ADAM_B1 ** ADAM_STEP)
    v_hat = v2 / (1.0 - ADAM_B2 ** ADAM_STEP)
    delta = -ADAM_LR * (m_hat / (jnp.sqrt(v_hat) + ADAM_EPS) + ADAM_WD * w)
    return delta, m2, v2


def _adam_rows(g, w, m, v):
    rows, cols = w.shape
    tm = rows if rows % 256 else 256

    def fn(rv, cr, out):
        return list(_adam_update(*rv)), []

    outs, _ = _rowwise("adamw", fn, rows, tm, [(a, cols, 0) for a in (g, w, m, v)], [], [(cols, F32)] * 3, [], 32)
    return outs


_SMALL = ["norm_g", "hg_lb", "hg_norm_g", "s5_a_re", "s5_a_im", "s5_log_dt", "s5_b_re", "s5_b_im", "s5_c_re",
          "s5_c_im", "s5_d", "b_glu", "ple_norm_g", "final_norm_g"]
_BIG = ["w_in", "w_o_hg", "w_glu", "w_o_s5", "w_out", "w_ple", "w_ple_gate"]
_ORDER = ["norm_g", "w_in", "hg_lb", "hg_norm_g", "w_o_hg", "s5_a_re", "s5_a_im", "s5_log_dt", "s5_b_re", "s5_b_im",
          "s5_c_re", "s5_c_im", "s5_d", "w_glu", "b_glu", "w_o_s5", "w_out", "ple_norm_g", "w_ple", "w_ple_gate",
          "final_norm_g"]


def _pack_small(vals):
    parts = []
    for name in _SMALL:
        flat = vals[name].reshape(-1).astype(F32)
        pad = (-flat.shape[0]) % 1024
        parts.append(jnp.pad(flat, (0, pad)))
    return jnp.concatenate(parts).reshape(-1, 128)


def _unpack_small(packed, like):
    flat = packed.reshape(-1)
    out, off = {}, 0
    for name in _SMALL:
        size = like[name].size
        out[name] = flat[off:off + size].reshape(like[name].shape)
        off += size + (-size) % 1024
    return out


def _col_blocks(full):
    k = full.shape[0]
    return full.reshape(k, N_DEV, 128).transpose(1, 0, 2)


def _from_col_blocks(blocks):
    k = blocks.shape[1]
    return blocks.transpose(1, 0, 2).reshape(k, N_DEV * 128)


def kernel(x, p, norm_g, w_in, hg_lb, hg_norm_g, w_o_hg, s5_a_re, s5_a_im, s5_log_dt, s5_b_re, s5_b_im, s5_c_re, s5_c_im, s5_d, w_glu, b_glu, w_o_s5, w_out, ple_norm_g, w_ple, w_ple_gate, final_norm_g, loss_target, m_norm_g, m_w_in, m_hg_lb, m_hg_norm_g, m_w_o_hg, m_s5_a_re, m_s5_a_im, m_s5_log_dt, m_s5_b_re, m_s5_b_im, m_s5_c_re, m_s5_c_im, m_s5_d, m_w_glu, m_b_glu, m_w_o_s5, m_w_out, m_ple_norm_g, m_w_ple, m_w_ple_gate, m_final_norm_g, v_norm_g, v_w_in, v_hg_lb, v_hg_norm_g, v_w_o_hg, v_s5_a_re, v_s5_a_im, v_s5_log_dt, v_s5_b_re, v_s5_b_im, v_s5_c_re, v_s5_c_im, v_s5_d, v_w_glu, v_b_glu, v_w_o_s5, v_w_out, v_ple_norm_g, v_w_ple, v_w_ple_gate, v_final_norm_g):
    args = dict(locals())
    w = {n: args[n] for n in _ORDER}
    m = {n: args["m_" + n] for n in _ORDER}
    v = {n: args["v_" + n] for n in _ORDER}
    xs = x[0]
    ps = p[0, 0]
    tgt = loss_target[0]
    S = xs.shape[0]

    pack_a = jnp.concatenate([w_o_hg[0], w_out[0], w_ple_gate[0]], axis=0).astype(BF16)
    pack_b = w_in[0].astype(BF16)
    pack_c = jnp.concatenate([w_glu[0], w_o_s5[0], w_ple[0]], axis=0).astype(BF16)
    u, proj, all_a, all_b, all_c = _inproj_gather(xs, norm_g, pack_a, pack_b, pack_c)
    wf_o_hg = all_a[:, 0:128].reshape(D_MODEL, D_MODEL)
    wf_out = all_a[:, 128:256].reshape(D_MODEL, D_MODEL)
    wf_pg = all_a[:, 256:384].reshape(D_MODEL, D_MODEL)
    wf_glu = _from_col_blocks(all_c[:, 0:512])
    wf_o_s5 = _from_col_blocks(all_c[:, 512:1024])
    wf_ple = _from_col_blocks(all_c[:, 1024:1280])

    lb = jax.nn.sigmoid(hg_lb[0:1] - hg_lb[1:2])
    s5_names = ["s5_a_re", "s5_a_im", "s5_log_dt", "s5_b_re", "s5_b_im", "s5_c_re", "s5_c_im", "s5_d"]
    build = lambda *a: _s5_matrices(*a, seg_len=S // S5_SEG)
    mats_f32, mats_vjp = jax.vjp(build, *[w[n][0] for n in s5_names])
    mats = dict(mats_f32, b_q=mats_f32["b_q"].astype(BF16), c_q=mats_f32["c_q"].astype(BF16))
    bias_glu = b_glu

    o, states = _hgrn_fwd(proj, lb)
    u_perm = _to_segments(proj[:, COL_US:COL_US + S5_WIDTH])
    zeros_state = jnp.zeros((S5_SEG, S5_COLS), F32)
    (h_ends,) = _s5_fwd_pass(u_perm, mats, zeros_state, False)
    h0 = _segment_starts(h_ends, mats["pow_r"], mats["pow_i"], False)
    y_perm, h_init, _ = _s5_fwd_pass(u_perm, mats, h0, True)
    ys = _from_segments(y_perm)
    y_hg, y_s5, glu, h1 = _stage_branches(o, proj, ys, xs, hg_norm_g, wf_o_hg, wf_glu, bias_glu, wf_o_s5, wf_out)

    dh1, (loss_acc, d_final_g, d_ple_g, d_w_ple, d_w_pg) = _stage_ple_loss(
        h1, ps, tgt, ple_norm_g, wf_ple, wf_pg, final_norm_g.reshape(1, D_MODEL))
    (d_gate_hg, d_gate_s5, d_yhg, d_ys5), (d_w_out,) = _stage_bwd_merge(dh1, y_hg, y_s5, proj, wf_out)
    (d_o, d_g_hg), (d_w_o_hg, d_hg_norm) = _stage_bwd_hg_path(d_yhg, o, proj, hg_norm_g, wf_o_hg)
    (d_ys, d_z), (d_w_o_s5, d_w_glu, d_b_glu) = _stage_bwd_s5_path(d_ys5, ys, glu, proj, wf_o_s5, wf_glu)
    dq, df, div, d_lb = _hgrn_bwd(proj, lb, d_o, states)
    dy_perm = _to_segments(d_ys)
    g_ends = _s5_bwd_ends(dy_perm, mats)
    g0 = _segment_starts(g_ends, mats["pow_r"], mats["pow_i"], True)
    du_perm, acc_p, acc_q, d_bq, d_cq_t, d_d = _s5_bwd_full(u_perm, dy_perm, h_init, g0, mats)
    d_us = _from_segments(du_perm)
    grad_x, dproj, d_norm_g = _stage_inproj_bwd([dq, df, div, d_g_hg, d_us, d_z, d_gate_hg, d_gate_s5], xs, dh1,
                                                norm_g, all_b)

    p_re, p_im = _split_cols(acc_p)
    q_re, q_im = _split_cols(acc_q)
    d_lam_r = (p_re + p_im)[0]
    d_lam_i = (q_im - q_re)[0]
    zero_row = jnp.zeros((S5_SEG, S5_COLS), F32)
    row_of = lambda re_part, im_part: zero_row.at[0].set(_join_cols(re_part[None], im_part[None])[0])
    zeros_q = jnp.zeros_like(d_lam_r)
    cot = dict(
        lam_r=row_of(d_lam_r, zeros_q), lam_i=row_of(zeros_q, d_lam_i),
        b_q=d_bq, c_q=d_cq_t.transpose(0, 2, 1), d_row=d_d,
        pow_r=jnp.zeros_like(mats["pow_r"]), pow_i=jnp.zeros_like(mats["pow_i"]),
    )
    d_s5 = mats_vjp(cot)

    s_lb = lb * (1.0 - lb)
    d_hg_lb = jnp.concatenate([d_lb * s_lb, -d_lb * s_lb], axis=0)
    small_g = dict(norm_g=d_norm_g, hg_lb=d_hg_lb, hg_norm_g=d_hg_norm, b_glu=d_b_glu, ple_norm_g=d_ple_g,
                   final_norm_g=d_final_g)
    for name, g in zip(s5_names, d_s5):
        small_g[name] = g
    pk = lambda d: _pack_small({n: d[n] for n in _SMALL})
    rs_a = jnp.concatenate([d_w_o_hg.reshape(N_DEV, 128, D_MODEL), d_w_out.reshape(N_DEV, 128, D_MODEL),
                            d_w_pg.reshape(N_DEV, 128, D_MODEL)], axis=1).astype(BF16)
    rs_c = jnp.concatenate([_col_blocks(d_w_glu), _col_blocks(d_w_o_s5), _col_blocks(d_w_ple)], axis=1).astype(BF16)
    g_b, g_a, g_c, sg = _grad_w_in_scatter(dproj, u, rs_a, rs_c, pk(small_g))
    sd, sm, sv = _adam_rows(sg, pk(w), pk(m), pk(v))
    like = {n: w[n] for n in _SMALL}
    out_g, out_d, out_m, out_v = (_unpack_small(t, like) for t in (sg, sd, sm, sv))
    big_g = dict(w_o_hg=g_a[0:128], w_out=g_a[128:256], w_ple_gate=g_a[256:384], w_in=g_b,
                 w_glu=g_c[0:512], w_o_s5=g_c[512:1024], w_ple=g_c[1024:1280])
    for name in _BIG:
        shape = w[name].shape
        g2 = big_g[name]
        d2, m2, v2 = _adam_rows(g2, w[name][0], m[name][0], v[name][0])
        out_g[name], out_d[name], out_m[name], out_v[name] = (t.reshape(shape) for t in (g2, d2, m2, v2))

    loss = lax.psum(loss_acc[0, 0], ("x", "y", "c"))
    return (loss, grad_x[None], *[out_g[n] for n in _ORDER], *[out_d[n] for n in _ORDER],
            *[out_m[n] for n in _ORDER], *[out_v[n] for n in _ORDER])
```

```python
import functools
import math

import jax
import jax.numpy as jnp
from jax import lax
from jax.experimental import pallas as pl
from jax.experimental.pallas import tpu as pltpu

F32 = jnp.float32
BF16 = jnp.bfloat16

D_MODEL = 1024
N_DEV = 8
IN_COLS = 7168
SHARD_IN = IN_COLS // N_DEV
HG_HEADS = 8
HG_DIM = 128
HG_CHUNK = 64
HG_SUPER = 256
HG_HEADS_PER_STEP = 8
S5_WIDTH = 512
S5_GROUPS = 32
S5_STATE = 64
S5_CH = 16
S5_SEG = 8
S5_QUADS = 4
S5_QCOLS = 1024
S5_COLS = S5_QUADS * S5_QCOLS
S5_TILE_STEPS = 64
S5_UNROLL = 8
NORM_EPS = 1e-6
ADAM_LR = 0.001
ADAM_B1 = 0.9
ADAM_B2 = 0.999
ADAM_EPS = 1e-08
ADAM_WD = 0.01
ADAM_STEP = 10
MIB = 1024 * 1024
MESH = pl.DeviceIdType.MESH

COL_Q, COL_F, COL_I, COL_G, COL_US, COL_ZS, COL_GH, COL_GS = 0, 1024, 2048, 3072, 4096, 4608, 5120, 6144


def _call(body, **kw):
    return pl.pallas_call(body, **kw)


def _params(vmem_mb, n_grid=1):
    return pltpu.CompilerParams(
        dimension_semantics=("arbitrary",) * n_grid, vmem_limit_bytes=vmem_mb * MIB
    )


def _bdot(a, b):
    return jnp.dot(a.astype(BF16), b.astype(BF16), preferred_element_type=F32)


def _bdot_nt(a, b):
    return lax.dot_general(a.astype(BF16), b.astype(BF16), (((1,), (1,)), ((), ())), preferred_element_type=F32)


def _bdot_tn(a, b):
    return lax.dot_general(a.astype(BF16), b.astype(BF16), (((0,), (0,)), ((), ())), preferred_element_type=F32)


def _sigmoid(x):
    return jax.nn.sigmoid(x)


def _silu(x):
    return x * _sigmoid(x)


def _dsilu(x):
    s = _sigmoid(x)
    return s * (1.0 + x * (1.0 - s))


_GELU_C = math.sqrt(2.0 / math.pi)


def _gelu(x):
    return 0.5 * x * (1.0 + jnp.tanh(_GELU_C * (x + 0.044715 * x * x * x)))


def _dgelu(x):
    t = jnp.tanh(_GELU_C * (x + 0.044715 * x * x * x))
    return 0.5 * (1.0 + t) + 0.5 * x * (1.0 - t * t) * _GELU_C * (1.0 + 3.0 * 0.044715 * x * x)


def _rms_fwd(x, g):
    r = lax.rsqrt(jnp.mean(x * x, axis=-1, keepdims=True) + NORM_EPS)
    n = x * r
    return n * g, n, r


def _rms_bwd(dy, n, r, g):
    dn = dy * g
    dx = r * (dn - n * jnp.mean(dn * n, axis=-1, keepdims=True))
    return dx, jnp.sum(dy * n, axis=0, keepdims=True)


def _head_rms_fwd(o, g):
    ns, rs = [], []
    for h in range(HG_HEADS):
        oh = o[:, h * HG_DIM:(h + 1) * HG_DIM]
        r = lax.rsqrt(jnp.mean(oh * oh, axis=-1, keepdims=True) + NORM_EPS)
        ns.append(oh * r)
        rs.append(r)
    n = jnp.concatenate(ns, axis=1)
    return n * g, n, rs


def _head_rms_bwd(dy, n, rs, g):
    dn = dy * g
    dxs = []
    for h in range(HG_HEADS):
        sl = slice(h * HG_DIM, (h + 1) * HG_DIM)
        dxs.append(rs[h] * (dn[:, sl] - n[:, sl] * jnp.mean(dn[:, sl] * n[:, sl], axis=-1, keepdims=True)))
    return jnp.concatenate(dxs, axis=1), jnp.sum(dy * n, axis=0, keepdims=True)


def _rowwise(name, fn, n_rows, tm, rows, consts, out_rows, out_accs, vmem_mb, parts=1):
    n_r, n_c, n_or, n_oa = len(rows), len(consts), len(out_rows), len(out_accs)
    tp = tm // parts

    def body(*refs):
        r_refs = refs[:n_r]
        c_refs = refs[n_r:n_r + n_c]
        or_refs = refs[n_r + n_c:n_r + n_c + n_or]
        oa_refs = refs[n_r + n_c + n_or:]

        if n_oa:
            @pl.when(pl.program_id(0) == 0)
            def _():
                for ref in oa_refs:
                    ref[...] = jnp.zeros(ref.shape, ref.dtype)

        for part in range(parts):
            sl = slice(part * tp, (part + 1) * tp)
            outs, accs = fn([r[sl, :] for r in r_refs], c_refs, [o.at[sl, :] for o in or_refs])
            for ref, v in zip(or_refs, outs):
                if v is not None:
                    ref[sl, :] = v.astype(ref.dtype)
            for ref, v in zip(oa_refs, accs):
                ref[...] += v.astype(ref.dtype)

    in_specs = [pl.BlockSpec((tm, w), functools.partial(lambda i, c: (i, c), c=cb)) for (_, w, cb) in rows]
    in_specs += [pl.BlockSpec(c.shape, functools.partial(lambda i, nd: (0,) * nd, nd=c.ndim),
                              pipeline_mode=pl.Buffered(1)) for c in consts]
    out_specs = [pl.BlockSpec((tm, w), lambda i: (i, 0)) for (w, _) in out_rows]
    out_specs += [pl.BlockSpec(s, functools.partial(lambda i, nd: (0,) * nd, nd=len(s))) for (s, _) in out_accs]
    out_shape = [jax.ShapeDtypeStruct((n_rows, w), dt) for (w, dt) in out_rows]
    out_shape += [jax.ShapeDtypeStruct(s, dt) for (s, dt) in out_accs]
    res = _call(
        body, name=name, grid=(n_rows // tm,), in_specs=in_specs, out_specs=out_specs, out_shape=out_shape,
        compiler_params=_params(vmem_mb),
    )(*[a for (a, _, _) in rows], *consts)
    return res[:n_or], res[n_or:]


def _stage_branches(o, proj, ys, x, hg_norm_g, w_o_hg, w_glu, b_glu, w_o_s5, w_out):
    S = x.shape[0]

    def fn(rv, cr, out):
        o_b, g_hg, z_s, gate_hg, gate_s5, ys_b, x_b = rv
        gn_ref, wohg_ref, wglu_ref, bglu_ref, wos5_ref, wout_ref = cr
        on, _, _ = _head_rms_fwd(o_b, gn_ref[...])
        a = on * _silu(g_hg)
        y_hg = jnp.dot(a.astype(BF16), wohg_ref[...], preferred_element_type=F32)
        gl = _gelu(ys_b)
        glu = jnp.dot(gl.astype(BF16), wglu_ref[...], preferred_element_type=F32) + bglu_ref[...]
        ys2 = glu[:, :S5_WIDTH] * _sigmoid(glu[:, S5_WIDTH:]) * _silu(z_s)
        y_s5 = jnp.dot(ys2.astype(BF16), wos5_ref[...], preferred_element_type=F32)
        merged = _sigmoid(gate_hg) * y_hg + _sigmoid(gate_s5) * y_s5
        h1 = x_b + jnp.dot(merged.astype(BF16), wout_ref[...], preferred_element_type=F32)
        return [y_hg, y_s5, glu, h1], []

    rows = [(o, D_MODEL, 0), (proj, D_MODEL, COL_G // D_MODEL), (proj, S5_WIDTH, COL_ZS // S5_WIDTH),
            (proj, D_MODEL, COL_GH // D_MODEL), (proj, D_MODEL, COL_GS // D_MODEL), (ys, S5_WIDTH, 0), (x, D_MODEL, 0)]
    (y_hg, y_s5, glu, h1), _ = _rowwise(
        "branches", fn, S, 256, rows, [hg_norm_g, w_o_hg, w_glu, b_glu, w_o_s5, w_out],
        [(D_MODEL, F32)] * 4, [], 56)
    return y_hg, y_s5, glu, h1


def _stage_ple_loss(h1, p, target, ple_norm_g, w_ple, w_ple_gate, final_norm_g):
    S = h1.shape[0]

    def fn(rv, cr, out):
        h1_b, p_b, t_b = rv
        gp_ref, wple_ref, wpg_ref, gf_ref = cr
        n2g, n2, r2 = _rms_fwd(h1_b, gp_ref[...])
        z = jnp.dot(n2g.astype(BF16), wpg_ref[...], preferred_element_type=F32)
        gate = _sigmoid(z)
        pe = jnp.dot(p_b.astype(BF16), wple_ref[...], preferred_element_type=F32)
        h2 = h1_b + pe * gate
        y, nf, rf = _rms_fwd(h2, gf_ref[...])
        err = y - t_b
        loss_rows = 0.5 * jnp.mean(err * err, axis=-1, keepdims=True)
        loss_inc = jnp.broadcast_to(jnp.sum(loss_rows, axis=0, keepdims=True), (1, 128))
        dy = err * (1.0 / D_MODEL)
        dh2, d_gf = _rms_bwd(dy, nf, rf, gf_ref[...])
        d_pe = dh2 * gate
        dz = dh2 * pe * gate * (1.0 - gate)
        d_wple = _bdot_tn(p_b, d_pe)
        d_wpg = _bdot_tn(n2g, dz)
        dn2g = _bdot_nt(dz, wpg_ref[...])
        dh1n, d_gp = _rms_bwd(dn2g, n2, r2, gp_ref[...])
        return [dh2 + dh1n], [loss_inc, d_gf, d_gp, d_wple, d_wpg]

    (dh1,), accs = _rowwise(
        "ple_loss", fn, S, 512, [(h1, D_MODEL, 0), (p, 256, 0), (target, D_MODEL, 0)],
        [ple_norm_g, w_ple, w_ple_gate, final_norm_g], [(D_MODEL, F32)],
        [((1, 128), F32), ((1, D_MODEL), F32), ((1, D_MODEL), F32), ((256, D_MODEL), F32), ((D_MODEL, D_MODEL), F32)], 56,
        parts=2)
    return dh1, accs


def _stage_bwd_merge(dh1, y_hg, y_s5, proj, w_out):
    S = dh1.shape[0]

    def fn(rv, cr, out):
        dh1_b, yhg, ys5, gate_hg, gate_s5 = rv
        (wout_ref,) = cr
        sg_h, sg_s = _sigmoid(gate_hg), _sigmoid(gate_s5)
        merged = sg_h * yhg + sg_s * ys5
        d_wout = _bdot_tn(merged, dh1_b)
        d_merged = _bdot_nt(dh1_b, wout_ref[...])
        d_gate_hg = d_merged * yhg * sg_h * (1.0 - sg_h)
        d_gate_s5 = d_merged * ys5 * sg_s * (1.0 - sg_s)
        return [d_gate_hg, d_gate_s5, d_merged * sg_h, d_merged * sg_s], [d_wout]

    rows = [(dh1, D_MODEL, 0), (y_hg, D_MODEL, 0), (y_s5, D_MODEL, 0), (proj, D_MODEL, COL_GH // D_MODEL),
            (proj, D_MODEL, COL_GS // D_MODEL)]
    outs, accs = _rowwise("bwd_merge", fn, S, 512, rows, [w_out], [(D_MODEL, BF16)] * 4,
                          [((D_MODEL, D_MODEL), F32)], 56, parts=2)
    return outs, accs


def _stage_bwd_hg_path(d_yhg, o, proj, hg_norm_g, w_o_hg):
    S = o.shape[0]

    def fn(rv, cr, out):
        d_yhg_b, o_b, g_hg = rv
        gn_ref, wohg_ref = cr
        ong, on, rs = _head_rms_fwd(o_b, gn_ref[...])
        sil = _silu(g_hg)
        d_wohg = _bdot_tn(ong * sil, d_yhg_b)
        d_a = _bdot_nt(d_yhg_b, wohg_ref[...])
        d_g_hg = d_a * ong * _dsilu(g_hg)
        d_o, d_gn = _head_rms_bwd(d_a * sil, on, rs, gn_ref[...])
        return [d_o, d_g_hg], [d_wohg, d_gn]

    rows = [(d_yhg, D_MODEL, 0), (o, D_MODEL, 0), (proj, D_MODEL, COL_G // D_MODEL)]
    outs, accs = _rowwise("bwd_hg_path", fn, S, 512, rows, [hg_norm_g, w_o_hg], [(D_MODEL, BF16)] * 2,
                          [((D_MODEL, D_MODEL), F32), ((1, D_MODEL), F32)], 56, parts=2)
    return outs, accs


def _stage_bwd_s5_path(d_ys5, ys, glu, proj, w_o_s5, w_glu):
    S = ys.shape[0]

    def fn(rv, cr, out):
        d_ys5_b, ys_b, glu_b, z_s = rv
        wos5_ref, wglu_ref = cr
        ga, gb = glu_b[:, :S5_WIDTH], glu_b[:, S5_WIDTH:]
        sgb, silz = _sigmoid(gb), _silu(z_s)
        ys2 = ga * sgb * silz
        d_wos5 = _bdot_tn(ys2, d_ys5_b)
        d_ys2 = _bdot_nt(d_ys5_b, wos5_ref[...])
        d_ga = d_ys2 * sgb * silz
        d_gb = d_ys2 * ga * sgb * (1.0 - sgb) * silz
        d_z = d_ys2 * ga * sgb * _dsilu(z_s)
        d_glu = jnp.concatenate([d_ga, d_gb], axis=1)
        gl = _gelu(ys_b)
        d_wglu = _bdot_tn(gl, d_glu)
        d_bglu = jnp.sum(d_glu, axis=0, keepdims=True)
        d_gl = _bdot_nt(d_glu, wglu_ref[...])
        return [d_gl * _dgelu(ys_b), d_z], [d_wos5, d_wglu, d_bglu]

    rows = [(d_ys5, D_MODEL, 0), (ys, S5_WIDTH, 0), (glu, D_MODEL, 0), (proj, S5_WIDTH, COL_ZS // S5_WIDTH)]
    outs, accs = _rowwise(
        "bwd_s5_path", fn, S, 512, rows, [w_o_s5, w_glu], [(S5_WIDTH, F32), (S5_WIDTH, BF16)],
        [((S5_WIDTH, D_MODEL), F32), ((S5_WIDTH, D_MODEL), F32), ((1, D_MODEL), F32)], 48, parts=2)
    return outs, accs


def _stage_inproj_bwd(pieces, x, dh1, norm_g, w_in_all):
    S = x.shape[0]

    def fn(rv, cr, out):
        g_ref, w_ref = cr
        x_b, dh1_b = rv[8], rv[9]
        dproj_ref = out[1]
        col = 0
        for v in rv[:8]:
            dproj_ref[:, col:col + v.shape[1]] = v.astype(BF16)
            col += v.shape[1]
        d_u = jnp.zeros((x_b.shape[0], D_MODEL), F32)
        for j in range(N_DEV):
            d_u = d_u + lax.dot_general(dproj_ref[:, j * SHARD_IN:(j + 1) * SHARD_IN], w_ref[j],
                                        (((1,), (1,)), ((), ())), preferred_element_type=F32)
        _, n, r = _rms_fwd(x_b, g_ref[...])
        dx, d_g = _rms_bwd(d_u, n, r, g_ref[...])
        return [dh1_b + dx, None], [d_g]

    rows = [(a, a.shape[1], 0) for a in pieces] + [(x, D_MODEL, 0), (dh1, D_MODEL, 0)]
    (grad_x, dproj), (d_g,) = _rowwise(
        "inproj_bwd", fn, S, 256, rows, [norm_g, w_in_all], [(D_MODEL, F32), (IN_COLS, BF16)],
        [((1, D_MODEL), F32)], 56)
    return grad_x, dproj, d_g


def _chunk_row(shape):
    return lax.broadcasted_iota(jnp.int32, shape, 0) & (HG_CHUNK - 1)


def _chunk_cumsum(x):
    r_in = _chunk_row(x.shape)
    s = 1
    while s < HG_CHUNK:
        x = x + jnp.where(r_in >= s, pltpu.roll(x, s, 0), 0.0)
        s *= 2
    return x


def _chunk_suffix_sum(x):
    n = x.shape[0]
    r_in = _chunk_row(x.shape)
    s = 1
    while s < HG_CHUNK:
        x = x + jnp.where(r_in < HG_CHUNK - s, pltpu.roll(x, n - s, 0), 0.0)
        s *= 2
    return x


def _hgrn_prep(q, fl, lb):
    nc = HG_SUPER // HG_CHUNK
    sig = _sigmoid(fl)
    f = lb + (1.0 - lb) * sig
    k = (1.0 - lb) * (1.0 - sig)
    b = _chunk_cumsum(jnp.log(f))
    b3 = b.reshape(nc, HG_CHUNK, HG_DIM)
    row3 = lax.broadcasted_iota(jnp.int32, b3.shape, 1)
    pick = lambda r: jnp.sum(jnp.where(row3 == r, b3, 0.0), axis=1, keepdims=True)
    b_mid = pick(HG_CHUNK // 2 - 1)
    b_last = pick(HG_CHUNK - 1)
    flat = lambda t: t.reshape(HG_SUPER, HG_DIM)
    e_qa = flat(jnp.exp(b3 - b_mid))
    e_ka = flat(jnp.exp(b_mid - b3))
    e_qd = jnp.exp(b)
    e_kd = flat(jnp.exp(b_last - b3))
    dc = jnp.exp(b_last)
    return sig, f, k, e_qa, e_ka, e_qd, e_kd, dc


def _hgrn_mask():
    r = lax.broadcasted_iota(jnp.int32, (HG_SUPER, HG_SUPER), 0)
    c = lax.broadcasted_iota(jnp.int32, (HG_SUPER, HG_SUPER), 1)
    shift = HG_CHUNK.bit_length() - 1
    return (jnp.right_shift(r, shift) == jnp.right_shift(c, shift)) & (r >= c)


def _hgrn_fwd(proj, lb):
    S = proj.shape[0]
    nb = S // HG_SUPER
    nc = HG_SUPER // HG_CHUNK
    hp = HG_HEADS_PER_STEP
    wide = hp * HG_DIM

    def body(q_ref, f_ref, iv_ref, lb_ref, o_ref, st_ref, state):
        @pl.when(pl.program_id(1) == 0)
        def _():
            state[...] = jnp.zeros(state.shape, F32)

        mask = _hgrn_mask()
        for hh in range(hp):
            lanes = slice(hh * HG_DIM, (hh + 1) * HG_DIM)
            q, iv = q_ref[:, lanes], iv_ref[:, lanes]
            _, _, k, e_qa, e_ka, e_qd, e_kd, dc = _hgrn_prep(q, f_ref[:, lanes], lb_ref[:, lanes])
            scores = jnp.where(mask, _bdot_nt(q * e_qa, k * e_ka), 0.0)
            o_intra = _bdot(scores, iv)
            qd, kd = q * e_qd, k * e_kd
            for c in range(nc):
                sl = slice(c * HG_CHUNK, (c + 1) * HG_CHUNK)
                st = state[hh]
                st_ref[hh, c] = st
                o_ref[sl, lanes] = o_intra[sl] + _bdot_nt(qd[sl], st)
                state[hh] = dc[c] * st + _bdot_tn(iv[sl], kd[sl])

    blk = lambda base: pl.BlockSpec((HG_SUPER, wide), functools.partial(lambda h, i, b: (i, b + h), b=base // wide))
    return _call(
        body, name="hgrn_fwd", grid=(HG_HEADS // hp, nb),
        in_specs=[blk(COL_Q), blk(COL_F), blk(COL_I), pl.BlockSpec((1, wide), lambda h, i: (0, h))],
        out_specs=[pl.BlockSpec((HG_SUPER, wide), lambda h, i: (i, h)),
                   pl.BlockSpec((hp, nc, HG_DIM, HG_DIM), lambda h, i: (h, i, 0, 0))],
        out_shape=[jax.ShapeDtypeStruct((S, D_MODEL), F32),
                   jax.ShapeDtypeStruct((HG_HEADS, S // HG_CHUNK, HG_DIM, HG_DIM), F32)],
        scratch_shapes=[pltpu.VMEM((hp, HG_DIM, HG_DIM), F32)],
        compiler_params=_params(40, 2),
    )(proj, proj, proj, lb)


def _hgrn_bwd(proj, lb, d_o, states):
    S = proj.shape[0]
    nb = S // HG_SUPER
    nc = HG_SUPER // HG_CHUNK
    hp = HG_HEADS_PER_STEP
    wide = hp * HG_DIM

    def body(q_ref, f_ref, iv_ref, lb_ref, do_ref, st_ref, dq_ref, df_ref, div_ref, dlb_ref, dstate):
        @pl.when(pl.program_id(1) == 0)
        def _():
            dstate[...] = jnp.zeros(dstate.shape, F32)
            dlb_ref[...] = jnp.zeros(dlb_ref.shape, F32)

        mask = _hgrn_mask()
        for hh in range(hp):
            lanes = slice(hh * HG_DIM, (hh + 1) * HG_DIM)
            q, iv, do, lb_v = q_ref[:, lanes], iv_ref[:, lanes], do_ref[:, lanes], lb_ref[:, lanes]
            sig, f, k, e_qa, e_ka, e_qd, e_kd, dc = _hgrn_prep(q, f_ref[:, lanes], lb_v)
            qa, ka, qd, kd = q * e_qa, k * e_ka, q * e_qd, k * e_kd
            scores = jnp.where(mask, _bdot_nt(qa, ka), 0.0)
            d_scores = jnp.where(mask, _bdot_nt(do, iv), 0.0)
            d_iv_intra = _bdot_tn(scores, do)
            d_qa = _bdot(d_scores, ka)
            d_ka = _bdot_tn(d_scores, qa)
            d_qd, d_kd, d_last = [None] * nc, [None] * nc, [None] * nc
            for c in reversed(range(nc)):
                sl = slice(c * HG_CHUNK, (c + 1) * HG_CHUNK)
                st = st_ref[hh, c]
                ds = dstate[hh]
                d_qd[c] = _bdot(do[sl], st)
                d_kd[c] = _bdot(iv[sl], ds)
                div_ref[sl, lanes] = (d_iv_intra[sl] + _bdot_nt(kd[sl], ds)).astype(div_ref.dtype)
                d_last[c] = (jnp.sum(ds * st, axis=0, keepdims=True) * dc[c]
                             + jnp.sum(d_kd[c] * kd[sl], axis=0, keepdims=True))
                dstate[hh] = dc[c] * ds + _bdot_tn(do[sl], qd[sl])
            d_qd = jnp.concatenate(d_qd, axis=0)
            d_kd = jnp.concatenate(d_kd, axis=0)
            d_b = d_qa * qa - d_ka * ka + d_qd * qd - d_kd * kd
            last_rows = jnp.concatenate([jnp.broadcast_to(t, (HG_CHUNK, HG_DIM)) for t in d_last], axis=0)
            d_b = d_b + jnp.where(_chunk_row(d_b.shape) == HG_CHUNK - 1, last_rows, 0.0)
            d_logf = _chunk_suffix_sum(d_b)
            d_k = d_ka * e_ka + d_kd * e_kd
            g_f = d_logf / f
            d_sig = (g_f - d_k) * (1.0 - lb_v)
            dq_ref[:, lanes] = (d_qa * e_qa + d_qd * e_qd).astype(dq_ref.dtype)
            df_ref[:, lanes] = (d_sig * sig * (1.0 - sig)).astype(df_ref.dtype)
            d_lb = jnp.sum((g_f - d_k) * (1.0 - sig), axis=0, keepdims=True)
            dlb_ref[:, lanes] += jnp.broadcast_to(d_lb, (8, HG_DIM))

    rev = lambda i: nb - 1 - i
    blk = lambda base: pl.BlockSpec((HG_SUPER, wide), functools.partial(lambda h, i, b: (rev(i), b + h), b=base // wide))
    row_out = pl.BlockSpec((HG_SUPER, wide), lambda h, i: (rev(i), h))
    dq, df, div, dlb = _call(
        body, name="hgrn_bwd", grid=(HG_HEADS // hp, nb),
        in_specs=[blk(COL_Q), blk(COL_F), blk(COL_I), pl.BlockSpec((1, wide), lambda h, i: (0, h)),
                  pl.BlockSpec((HG_SUPER, wide), lambda h, i: (rev(i), h)),
                  pl.BlockSpec((hp, nc, HG_DIM, HG_DIM), lambda h, i: (h, rev(i), 0, 0))],
        out_specs=[row_out, row_out, row_out, pl.BlockSpec((8, wide), lambda h, i: (0, h))],
        out_shape=[jax.ShapeDtypeStruct((S, D_MODEL), BF16)] * 3 + [jax.ShapeDtypeStruct((8, D_MODEL), F32)],
        scratch_shapes=[pltpu.VMEM((hp, HG_DIM, HG_DIM), F32)],
        compiler_params=_params(40, 2),
    )(proj, proj, proj, lb, d_o, states)
    return dq, df, div, dlb[0:1]


def _s5_matrices(a_re, a_im, log_dt, b_re, b_im, c_re, c_im, d, seg_len):
    dt = jnp.exp(log_dt)[:, None]
    mag = jnp.exp(a_re * dt)
    lr, li = mag * jnp.cos(a_im * dt), mag * jnp.sin(a_im * dt)
    den = a_re * a_re + a_im * a_im
    nr = lr - 1.0
    sr = (nr * a_re + li * a_im) / den
    si = (li * a_re - nr * a_im) / den
    bbr = sr[..., None] * b_re - si[..., None] * b_im
    bbi = sr[..., None] * b_im + si[..., None] * b_re
    eye = jnp.eye(8, dtype=F32)

    def quad_cols(v):
        return v.reshape(S5_QUADS, 8 * S5_STATE)

    def lam_row(re_part, im_part):
        row = jnp.concatenate([quad_cols(re_part), quad_cols(im_part)], axis=1).reshape(1, S5_COLS)
        return jnp.broadcast_to(row, (S5_SEG, S5_COLS))

    def b_mat(bb):
        t = bb.reshape(S5_QUADS, 8, S5_STATE, S5_CH)
        return jnp.einsum("qgnc,gh->qgchn", t, eye).reshape(S5_QUADS, 8 * S5_CH, 8 * S5_STATE)

    def c_mat(cc):
        t = cc.reshape(S5_QUADS, 8, S5_CH, S5_STATE)
        return jnp.einsum("qgcn,gh->qgnhc", t, eye).reshape(S5_QUADS, 8 * S5_STATE, 8 * S5_CH)

    ang = a_im * dt * seg_len
    magp = jnp.exp(a_re * dt * seg_len)
    lpr, lpi = magp * jnp.cos(ang), magp * jnp.sin(ang)
    return dict(
        lam_r=lam_row(lr, lr), lam_i=lam_row(-li, li),
        b_q=jnp.concatenate([b_mat(bbr), b_mat(bbi)], axis=2),
        c_q=jnp.concatenate([c_mat(c_re), -c_mat(c_im)], axis=1),
        d_row=d.reshape(1, S5_WIDTH), pow_r=quad_cols(lpr), pow_i=quad_cols(lpi),
    )


def _s5_parts(v):
    half = S5_QCOLS // 2
    return tuple(v[:, k * half:(k + 1) * half] for k in range(2 * S5_QUADS))


def _s5_advance(parts, lr_ref, li_ref, x_ref, sl, conj):
    half = S5_QCOLS // 2
    out = []
    for q in range(S5_QUADS):
        re_c = slice(q * S5_QCOLS, q * S5_QCOLS + half)
        im_c = slice(q * S5_QCOLS + half, (q + 1) * S5_QCOLS)
        lr, li = lr_ref[:, re_c], li_ref[:, im_c]
        hr, hi = parts[2 * q], parts[2 * q + 1]
        if conj:
            out += [lr * hr + li * hi + x_ref[sl, re_c], lr * hi - li * hr + x_ref[sl, im_c]]
        else:
            out += [lr * hr - li * hi + x_ref[sl, re_c], lr * hi + li * hr + x_ref[sl, im_c]]
    return tuple(out)


def _scan_loop(step, init):
    def trip(o, carry):
        for j in range(S5_UNROLL):
            carry = step(o * S5_UNROLL + j, carry)
        return carry

    return lax.fori_loop(0, S5_TILE_STEPS // S5_UNROLL, trip, init)


def _s5_store(ref, sl, parts):
    half = S5_QCOLS // 2
    for k, v in enumerate(parts):
        ref[sl, k * half:(k + 1) * half] = v


def _s5_fwd_pass(u_perm, mats, h0, with_output):
    S = u_perm.shape[0]
    rows = S5_TILE_STEPS * S5_SEG
    nt = S // rows

    def body(*refs):
        if with_output:
            u_ref, b_ref, lr_ref, li_ref, h0_ref, c_ref, d_ref, y_ref, hinit_ref, hend_ref, xs, hcar = refs
        else:
            u_ref, b_ref, lr_ref, li_ref, h0_ref, hend_ref, xs, hcar = refs

        @pl.when(pl.program_id(0) == 0)
        def _():
            hcar[...] = h0_ref[...]

        if with_output:
            hinit_ref[...] = hcar[...]
        u = u_ref[...]
        ub = u.astype(BF16)
        for q in range(S5_QUADS):
            xs[:, q * S5_QCOLS:(q + 1) * S5_QCOLS] = jnp.dot(ub[:, q * 128:(q + 1) * 128], b_ref[q], preferred_element_type=F32)

        def step(t, h):
            sl = pl.ds(pl.multiple_of(t * S5_SEG, S5_SEG), S5_SEG)
            hn = _s5_advance(h, lr_ref, li_ref, xs, sl, False)
            _s5_store(xs, sl, hn)
            return hn

        h = _scan_loop(step, _s5_parts(hcar[...]))
        _s5_store(hcar, slice(None), h)
        _s5_store(hend_ref, slice(None), h)
        if with_output:
            ys = [jnp.dot(xs[:, q * S5_QCOLS:(q + 1) * S5_QCOLS].astype(BF16), c_ref[q], preferred_element_type=F32)
                  for q in range(S5_QUADS)]
            y_ref[...] = jnp.concatenate(ys, axis=1) + d_ref[...] * u

    full = lambda a: pl.BlockSpec(a.shape, functools.partial(lambda i, nd: (0,) * nd, nd=a.ndim))
    ins = [u_perm, mats["b_q"], mats["lam_r"], mats["lam_i"], h0]
    in_specs = [pl.BlockSpec((rows, S5_WIDTH), lambda i: (i, 0))] + [full(a) for a in ins[1:]]
    out_specs = [pl.BlockSpec((S5_SEG, S5_COLS), lambda i: (0, 0))]
    out_shape = [jax.ShapeDtypeStruct((S5_SEG, S5_COLS), F32)]
    if with_output:
        ins += [mats["c_q"], mats["d_row"]]
        in_specs += [full(mats["c_q"]), full(mats["d_row"])]
        out_specs = [pl.BlockSpec((rows, S5_WIDTH), lambda i: (i, 0)),
                     pl.BlockSpec((None, S5_SEG, S5_COLS), lambda i: (i, 0, 0))] + out_specs
        out_shape = [jax.ShapeDtypeStruct((S, S5_WIDTH), F32), jax.ShapeDtypeStruct((nt, S5_SEG, S5_COLS), F32)] + out_shape
    return _call(
        body, name="s5_fwd_y" if with_output else "s5_fwd_ends", grid=(nt,), in_specs=in_specs, out_specs=out_specs,
        out_shape=out_shape,
        scratch_shapes=[pltpu.VMEM((rows, S5_COLS), F32), pltpu.VMEM((S5_SEG, S5_COLS), F32)],
        compiler_params=_params(40),
    )(*ins)


def _s5_bwd_ends(dy_perm, mats):
    S = dy_perm.shape[0]
    rows = S5_TILE_STEPS * S5_SEG
    nt = S // rows

    def body(dy_ref, c_ref, lr_ref, li_ref, gend_ref, gs, gcar):
        @pl.when(pl.program_id(0) == 0)
        def _():
            gcar[...] = jnp.zeros(gcar.shape, F32)

        dyb = dy_ref[...].astype(BF16)
        for q in range(S5_QUADS):
            gs[:, q * S5_QCOLS:(q + 1) * S5_QCOLS] = lax.dot_general(
                dyb[:, q * 128:(q + 1) * 128], c_ref[q], (((1,), (1,)), ((), ())), preferred_element_type=F32)

        def step(k, g):
            t = S5_TILE_STEPS - 1 - k
            sl = pl.ds(pl.multiple_of(t * S5_SEG, S5_SEG), S5_SEG)
            return _s5_advance(g, lr_ref, li_ref, gs, sl, True)

        g = _scan_loop(step, _s5_parts(gcar[...]))
        _s5_store(gcar, slice(None), g)
        _s5_store(gend_ref, slice(None), g)

    full = lambda a: pl.BlockSpec(a.shape, functools.partial(lambda i, nd: (0,) * nd, nd=a.ndim))
    return _call(
        body, name="s5_bwd_ends", grid=(nt,),
        in_specs=[pl.BlockSpec((rows, S5_WIDTH), lambda i: (nt - 1 - i, 0)), full(mats["c_q"]), full(mats["lam_r"]),
                  full(mats["lam_i"])],
        out_specs=pl.BlockSpec((S5_SEG, S5_COLS), lambda i: (0, 0)),
        out_shape=jax.ShapeDtypeStruct((S5_SEG, S5_COLS), F32),
        scratch_shapes=[pltpu.VMEM((rows, S5_COLS), F32), pltpu.VMEM((S5_SEG, S5_COLS), F32)],
        compiler_params=_params(40),
    )(dy_perm, mats["c_q"], mats["lam_r"], mats["lam_i"])


def _s5_bwd_full(u_perm, dy_perm, hinit, g0, mats):
    S = u_perm.shape[0]
    rows = S5_TILE_STEPS * S5_SEG
    nt = S // rows

    def body(u_ref, dy_ref, hinit_ref, g0_ref, b_ref, c_ref, lr_ref, li_ref, d_ref,
             du_ref, dp_ref, dq_ref, db_ref, dc_ref, dd_ref, hs, gs, gcar):
        @pl.when(pl.program_id(0) == 0)
        def _():
            gcar[...] = g0_ref[...]
            for ref in (dp_ref, dq_ref, db_ref, dc_ref, dd_ref):
                ref[...] = jnp.zeros(ref.shape, F32)

        u, dy = u_ref[...], dy_ref[...]
        ub, dyb = u.astype(BF16), dy.astype(BF16)
        hs[0:S5_SEG, :] = hinit_ref[...]
        for q in range(S5_QUADS):
            cols = slice(q * S5_QCOLS, (q + 1) * S5_QCOLS)
            hs[S5_SEG:, cols] = jnp.dot(ub[:, q * 128:(q + 1) * 128], b_ref[q], preferred_element_type=F32)
            gs[:, cols] = lax.dot_general(dyb[:, q * 128:(q + 1) * 128], c_ref[q], (((1,), (1,)), ((), ())),
                                          preferred_element_type=F32)

        def fstep(t, h):
            sl = pl.ds(pl.multiple_of((t + 1) * S5_SEG, S5_SEG), S5_SEG)
            hn = _s5_advance(h, lr_ref, li_ref, hs, sl, False)
            _s5_store(hs, sl, hn)
            return hn

        _scan_loop(fstep, _s5_parts(hinit_ref[...]))

        def bstep(k, g):
            t = S5_TILE_STEPS - 1 - k
            sl = pl.ds(pl.multiple_of(t * S5_SEG, S5_SEG), S5_SEG)
            gn = _s5_advance(g, lr_ref, li_ref, gs, sl, True)
            _s5_store(gs, sl, gn)
            return gn

        _s5_store(gcar, slice(None), _scan_loop(bstep, _s5_parts(gcar[...])))

        half = S5_QCOLS // 2
        dus = []
        for q in range(S5_QUADS):
            cols = slice(q * S5_QCOLS, (q + 1) * S5_QCOLS)

            def astep(t, carry, q=q):
                sl = pl.ds(pl.multiple_of(t * S5_SEG, S5_SEG), S5_SEG)
                g = gs[sl, q * S5_QCOLS:(q + 1) * S5_QCOLS]
                hp = hs[sl, q * S5_QCOLS:(q + 1) * S5_QCOLS]
                hp_sw = jnp.concatenate([hp[:, half:], hp[:, :half]], axis=1)
                return carry[0] + g * hp, carry[1] + g * hp_sw

            zero = jnp.zeros((S5_SEG, S5_QCOLS), F32)
            acc_p, acc_q = _scan_loop(astep, (zero, zero))
            dp_ref[:, cols] += jnp.sum(acc_p, axis=0, keepdims=True)
            dq_ref[:, cols] += jnp.sum(acc_q, axis=0, keepdims=True)
            gq = gs[:, cols].astype(BF16)
            db_ref[q] += lax.dot_general(ub[:, q * 128:(q + 1) * 128], gq, (((0,), (0,)), ((), ())),
                                         preferred_element_type=F32)
            hq = hs[S5_SEG:, cols].astype(BF16)
            dc_ref[q] += lax.dot_general(dyb[:, q * 128:(q + 1) * 128], hq, (((0,), (0,)), ((), ())),
                                         preferred_element_type=F32)
            dus.append(lax.dot_general(gq, b_ref[q], (((1,), (1,)), ((), ())), preferred_element_type=F32))
        du_ref[...] = (jnp.concatenate(dus, axis=1) + d_ref[...] * dy).astype(du_ref.dtype)
        dd_ref[...] += jnp.sum(dy * u, axis=0, keepdims=True)

    full = lambda a: pl.BlockSpec(a.shape, functools.partial(lambda i, nd: (0,) * nd, nd=a.ndim))
    rev_rows = pl.BlockSpec((rows, S5_WIDTH), lambda i: (nt - 1 - i, 0))
    consts = [mats["b_q"], mats["c_q"], mats["lam_r"], mats["lam_i"], mats["d_row"]]
    acc = lambda s: pl.BlockSpec(s, functools.partial(lambda i, nd: (0,) * nd, nd=len(s)))
    acc_shapes = [(1, S5_COLS), (1, S5_COLS), (S5_QUADS, 128, S5_QCOLS), (S5_QUADS, 128, S5_QCOLS), (1, S5_WIDTH)]
    return _call(
        body, name="s5_bwd_full", grid=(nt,),
        in_specs=[rev_rows, rev_rows, pl.BlockSpec((None, S5_SEG, S5_COLS), lambda i: (nt - 1 - i, 0, 0)), full(g0)]
        + [full(a) for a in consts],
        out_specs=[rev_rows] + [acc(s) for s in acc_shapes],
        out_shape=[jax.ShapeDtypeStruct((S, S5_WIDTH), BF16)] + [jax.ShapeDtypeStruct(s, F32) for s in acc_shapes],
        scratch_shapes=[pltpu.VMEM((rows + S5_SEG, S5_COLS), F32), pltpu.VMEM((rows, S5_COLS), F32),
                        pltpu.VMEM((S5_SEG, S5_COLS), F32)],
        compiler_params=_params(56),
    )(u_perm, dy_perm, hinit, g0, *consts)


def _cmul(ar, ai, br, bi):
    return ar * br - ai * bi, ar * bi + ai * br


def _split_cols(v):
    t = v.reshape(v.shape[0], S5_QUADS, 2, S5_QCOLS // 2)
    return t[:, :, 0], t[:, :, 1]


def _join_cols(re, im):
    return jnp.stack([re, im], axis=2).reshape(re.shape[0], S5_COLS)


def _segment_starts(ends, pow_r, pow_i, reverse):
    er, ei = _split_cols(ends)
    pi = -pow_i if reverse else pow_i
    order = list(range(S5_SEG))
    if reverse:
        order = order[::-1]
    cr, ci = jnp.zeros_like(er[0]), jnp.zeros_like(ei[0])
    out_r, out_i = [None] * S5_SEG, [None] * S5_SEG
    for j in order:
        out_r[j], out_i[j] = cr, ci
        mr, mi = _cmul(pow_r, pi, cr, ci)
        cr, ci = mr + er[j], mi + ei[j]
    return _join_cols(jnp.stack(out_r), jnp.stack(out_i))


def _to_segments(a):
    S, w = a.shape
    return a.reshape(S5_SEG, S // S5_SEG, w).transpose(1, 0, 2).reshape(S, w)


def _from_segments(a):
    S, w = a.shape
    return a.reshape(S // S5_SEG, S5_SEG, w).transpose(1, 0, 2).reshape(S, w)


def _my_pos():
    return lax.axis_index("x"), lax.axis_index("y"), lax.axis_index("c")


def _flip(pos, k):
    x, y, c = pos
    return (1 - x if k & 4 else x, 1 - y if k & 2 else y, 1 - c if k & 1 else c)


def _index_of(pos):
    return 4 * pos[0] + 2 * pos[1] + pos[2]


_GATHER_FLIPS = (0, 1, 4, 5, 2, 3, 6, 7)


def _inproj_gather(x, norm_g, pack_a, pack_b, pack_c):
    S = x.shape[0]
    tm = min(S, 1024)
    n_i = S // tm
    order = jnp.stack([_index_of(_flip(_my_pos(), k)) for k in _GATHER_FLIPS]).astype(jnp.int32)

    def body(order_ref, x_ref, g_ref, pa_ref, pb_ref, pc_ref, u_ref, proj_ref, oa_ref, ob_ref, oc_ref,
             wv, u_scr, send_sems, recv_sems, local_sems):
        s, i = pl.program_id(0), pl.program_id(1)
        me = _my_pos()
        mine = _index_of(me)
        sibling = _flip(me, 1)
        srcs = (pb_ref, pa_ref, pc_ref)
        dsts = (wv, oa_ref, oc_ref)

        def direct(a, k):
            return pltpu.make_async_remote_copy(
                src_ref=srcs[a], dst_ref=dsts[a].at[mine], send_sem=send_sems.at[a * 8 + k],
                recv_sem=recv_sems.at[a * 8 + k], device_id=_flip(me, k), device_id_type=MESH)

        def passed_on(a, k):
            slot = _index_of(_flip(me, k))
            return pltpu.make_async_remote_copy(
                src_ref=dsts[a].at[slot], dst_ref=dsts[a].at[slot], send_sem=send_sems.at[a * 8 + (k | 1)],
                recv_sem=recv_sems.at[a * 8 + (k | 1)], device_id=sibling, device_id_type=MESH)

        def arrival(a, k):
            slot = _index_of(_flip(me, k))
            pltpu.make_async_remote_copy(
                src_ref=dsts[a].at[slot], dst_ref=dsts[a].at[slot], send_sem=send_sems.at[a * 8 + k],
                recv_sem=recv_sems.at[a * 8 + k], device_id=me, device_id_type=MESH).wait_recv()

        def own_copy(a):
            return pltpu.make_async_copy(srcs[a], dsts[a].at[mine], local_sems.at[a])

        def keep(idx):
            slot = _index_of(_flip(me, _GATHER_FLIPS[idx]))
            return pltpu.make_async_copy(wv.at[slot], ob_ref.at[slot], local_sems.at[3 + idx])

        keep_u = pltpu.make_async_copy(u_scr, u_ref, local_sems.at[3 + N_DEV])

        first = (s == 0) & (i == 0)

        @pl.when(first)
        def _():
            for a in range(3):
                own_copy(a).start()
            for k in (1, 4, 2):
                direct(0, k).start()
            own_copy(0).wait()
            keep(0).start()

        for idx, k in enumerate(_GATHER_FLIPS):
            if idx == 0:
                continue

            @pl.when((s == idx) & (i == 0))
            def _(idx=idx, k=k):
                arrival(0, k)
                if k in (4, 2, 6):
                    passed_on(0, k).start()
                keep(idx).start()
                if idx == 1:
                    keep_u.start()
                    direct(0, 6).start()
                if idx == 2:
                    for a in (1, 2):
                        for k in (1, 4, 2, 6):
                            direct(a, k).start()

        @pl.when(s == 0)
        def _():
            y, _, _ = _rms_fwd(x_ref[...], g_ref[...])
            u_scr[pl.ds(pl.multiple_of(i * tm, tm), tm), :] = y.astype(BF16)

        ub = u_scr[pl.ds(pl.multiple_of(i * tm, tm), tm), :]
        proj_ref[...] = jnp.dot(ub, wv[order_ref[s]], preferred_element_type=F32)

        @pl.when((s == N_DEV - 1) & (i == n_i - 1))
        def _():
            for a in (1, 2):
                for k in (4, 2, 6):
                    arrival(a, k)
                    passed_on(a, k).start()
            for a in (1, 2):
                for k in (1, 5, 3, 7):
                    arrival(a, k)
                own_copy(a).wait()
            for a in range(3):
                for k in (1, 4, 2, 6):
                    direct(a, k).wait_send()
                for k in (4, 2, 6):
                    passed_on(a, k).wait_send()
            for idx in range(N_DEV):
                keep(idx).wait()
            keep_u.wait()

    any_spec = pl.BlockSpec(memory_space=pl.ANY)
    vmem = pl.BlockSpec(memory_space=pltpu.VMEM)
    grid_spec = pltpu.PrefetchScalarGridSpec(
        num_scalar_prefetch=1, grid=(N_DEV, n_i),
        in_specs=[pl.BlockSpec((tm, D_MODEL), lambda s, i, o: (jnp.where(s == 0, i, 0), 0)),
                  pl.BlockSpec((1, D_MODEL), lambda s, i, o: (0, 0)), any_spec, vmem, any_spec],
        out_specs=[any_spec, pl.BlockSpec((tm, SHARD_IN), lambda s, i, o: (i, o[s])), any_spec, any_spec, any_spec],
        scratch_shapes=[pltpu.VMEM((N_DEV,) + pack_b.shape, BF16), pltpu.VMEM((S, D_MODEL), BF16),
                        pltpu.SemaphoreType.DMA((24,)), pltpu.SemaphoreType.DMA((24,)), pltpu.SemaphoreType.DMA((4 + N_DEV,))],
    )
    return _call(
        body, name="inproj_gather", grid_spec=grid_spec,
        out_shape=[jax.ShapeDtypeStruct((S, D_MODEL), BF16), jax.ShapeDtypeStruct((S, IN_COLS), F32),
                   jax.ShapeDtypeStruct((N_DEV,) + pack_a.shape, BF16), jax.ShapeDtypeStruct((N_DEV,) + pack_b.shape, BF16),
                   jax.ShapeDtypeStruct((N_DEV,) + pack_c.shape, BF16)],
        compiler_params=_params(56, 2),
    )(order, x, norm_g, pack_a, pack_b, pack_c)


_SCATTER_FLIPS = (7, 6, 5, 4, 3, 2, 1, 0)
_N_CHIPS = 4


def _for_row_chunks(n_rows, chunk, fn):
    def step(c, carry):
        fn(pl.ds(pl.multiple_of(c * chunk, chunk), chunk))
        return carry

    lax.fori_loop(0, n_rows // chunk, step, 0)


def _grad_w_in_scatter(dproj, u, rs_a, rs_c, small_partial):
    S = u.shape[0]
    tm = min(S, 1024)
    n_i = S // tm
    order = jnp.stack([_index_of(_flip(_my_pos(), k)) for k in _SCATTER_FLIPS]).astype(jnp.int32)
    shapes = ((D_MODEL, SHARD_IN), rs_a.shape[1:], rs_c.shape[1:])
    row_chunk = 128

    def body(order_ref, dp_ref, u_ref, ra_ref, rc_ref, p_ref, gb_ref, ga_ref, gc_ref, gs_ref, acc, sib_b, d2d_b,
             send_b, ici_b, mine_a, sib_a, ici_a, mine_c, sib_c, ici_c, gath, send_sems, recv_sems, local_sems):
        s, i = pl.program_id(0), pl.program_id(1)
        me = _my_pos()
        mine_idx = _index_of(me)
        sibling = _flip(me, 1)

        def small_to(k):
            return pltpu.make_async_remote_copy(
                src_ref=p_ref, dst_ref=gath.at[mine_idx], send_sem=send_sems.at[21 + k - 1],
                recv_sem=recv_sems.at[21 + k - 1], device_id=_flip(me, k), device_id_type=MESH)

        def small_from(k):
            pltpu.make_async_remote_copy(
                src_ref=p_ref, dst_ref=gath.at[_index_of(_flip(me, k))], send_sem=send_sems.at[21 + k - 1],
                recv_sem=recv_sems.at[21 + k - 1], device_id=me, device_id_type=MESH).wait_recv()
        sib = (sib_b, sib_a, sib_c)
        ici = (ici_b, ici_a, ici_c)
        outs = (gb_ref, ga_ref, gc_ref)

        def to_sibling(arr, m, src):
            return pltpu.make_async_remote_copy(
                src_ref=src, dst_ref=sib[arr].at[m], send_sem=send_sems.at[arr * 7 + m],
                recv_sem=recv_sems.at[arr * 7 + m], device_id=sibling, device_id_type=MESH)

        def over_ici(arr, m, src):
            return pltpu.make_async_remote_copy(
                src_ref=src, dst_ref=ici[arr].at[m], send_sem=send_sems.at[arr * 7 + 4 + m],
                recv_sem=recv_sems.at[arr * 7 + 4 + m], device_id=_flip(me, 6 - 2 * m), device_id_type=MESH)

        def from_sibling(arr, m):
            to_sibling(arr, m, sib[arr].at[m]).wait_recv()

        def from_ici(arr, m):
            over_ici(arr, m, ici[arr].at[m]).wait_recv()

        small = ((1, ra_ref, mine_a), (2, rc_ref, mine_c))

        def local_copy(arr, src, mine, m):
            return pltpu.make_async_copy(src.at[_index_of(_flip(me, 6 - 2 * m))], mine.at[m],
                                         local_sems.at[(arr - 1) * _N_CHIPS + m])

        @pl.when((s == 0) & (i == 0))
        def _():
            gath[mine_idx] = p_ref[...]
            for k in range(1, N_DEV):
                small_to(k).start()
            for arr, src, mine in small:
                for m in range(_N_CHIPS):
                    to_sibling(arr, m, src.at[_index_of(_flip(me, 7 - 2 * m))]).start()
                    local_copy(arr, src, mine, m).start()

        @pl.when((s == 1) & (i == 0))
        def _():
            for arr, src, mine in small:
                rows, chunk = shapes[arr][0], 16
                for m in range(_N_CHIPS):
                    local_copy(arr, src, mine, m).wait()
                    from_sibling(arr, m)
                    if m < _N_CHIPS - 1:
                        def add(sl, arr=arr, mine=mine, m=m):
                            mine[m, sl, :] = (mine[m, sl, :].astype(F32) + sib[arr][m, sl, :].astype(F32)).astype(BF16)

                        _for_row_chunks(rows, chunk, add)
                        over_ici(arr, m, mine.at[m]).start()
                    else:
                        def keep(sl, arr=arr, mine=mine, m=m):
                            outs[arr][sl, :] = mine[m, sl, :].astype(F32) + sib[arr][m, sl, :].astype(F32)

                        _for_row_chunks(rows, chunk, keep)

        @pl.when(i == 0)
        def _():
            acc[...] = jnp.zeros(acc.shape, F32)

        acc[...] += lax.dot_general(dp_ref[...], u_ref[...], (((0,), (0,)), ((), ())), preferred_element_type=F32)

        def block_rows(c):
            return acc[:, c * row_chunk:(c + 1) * row_chunk].T

        for m in range(_N_CHIPS):
            @pl.when((s == 2 * m) & (i == n_i - 1))
            def _(m=m):
                if m > 0:
                    to_sibling(0, m - 1, d2d_b).wait_send()
                for c in range(D_MODEL // row_chunk):
                    d2d_b[c * row_chunk:(c + 1) * row_chunk, :] = block_rows(c).astype(BF16)
                to_sibling(0, m, d2d_b).start()

            @pl.when((s == 2 * m + 1) & (i == n_i - 1))
            def _(m=m):
                from_sibling(0, m)
                slot = m % 2
                if m == 2:
                    over_ici(0, 0, send_b.at[0]).wait_send()
                for c in range(D_MODEL // row_chunk):
                    rows = slice(c * row_chunk, (c + 1) * row_chunk)
                    total = block_rows(c) + sib_b[m, rows, :].astype(F32)
                    if m < _N_CHIPS - 1:
                        send_b[slot, rows, :] = total.astype(BF16)
                    else:
                        gb_ref[rows, :] = total
                if m < _N_CHIPS - 1:
                    over_ici(0, m, send_b.at[slot]).start()

        @pl.when((s == N_DEV - 1) & (i == n_i - 1))
        def _():
            for arr in range(3):
                for m in range(_N_CHIPS - 1):
                    from_ici(arr, m)
                rows = shapes[arr][0]

                def add(sl, arr=arr):
                    outs[arr][sl, :] = (outs[arr][sl, :] + ici[arr][0, sl, :].astype(F32)
                                        + ici[arr][1, sl, :].astype(F32) + ici[arr][2, sl, :].astype(F32))

                _for_row_chunks(rows, 16, add)
            for k in range(1, N_DEV):
                small_from(k)
            total = gath[0]
            for dev in range(1, N_DEV):
                total = total + gath[dev]
            gs_ref[...] = total
            for k in range(1, N_DEV):
                small_to(k).wait_send()
            to_sibling(0, _N_CHIPS - 1, d2d_b).wait_send()
            over_ici(0, 1, send_b.at[1]).wait_send()
            over_ici(0, 2, send_b.at[0]).wait_send()
            for arr, src, mine in small:
                for m in range(_N_CHIPS):
                    to_sibling(arr, m, src.at[0]).wait_send()
                for m in range(_N_CHIPS - 1):
                    over_ici(arr, m, mine.at[m]).wait_send()

    any_spec = pl.BlockSpec(memory_space=pl.ANY)
    vmem = pl.BlockSpec(memory_space=pltpu.VMEM)
    half = lambda shp, n: pltpu.VMEM((n,) + tuple(shp), BF16)
    grid_spec = pltpu.PrefetchScalarGridSpec(
        num_scalar_prefetch=1, grid=(N_DEV, n_i),
        in_specs=[pl.BlockSpec((tm, SHARD_IN), lambda s, i, o: (i, o[s])),
                  pl.BlockSpec((tm, D_MODEL), lambda s, i, o: (i, 0)), any_spec, any_spec, vmem],
        out_specs=[vmem, vmem, vmem, vmem],
        scratch_shapes=[
            pltpu.VMEM((SHARD_IN, D_MODEL), F32), half(shapes[0], _N_CHIPS), pltpu.VMEM(shapes[0], BF16),
            half(shapes[0], 2), half(shapes[0], _N_CHIPS - 1),
            half(shapes[1], _N_CHIPS), half(shapes[1], _N_CHIPS), half(shapes[1], _N_CHIPS - 1),
            half(shapes[2], _N_CHIPS), half(shapes[2], _N_CHIPS), half(shapes[2], _N_CHIPS - 1),
            pltpu.VMEM((N_DEV,) + small_partial.shape, F32),
            pltpu.SemaphoreType.DMA((28,)), pltpu.SemaphoreType.DMA((28,)), pltpu.SemaphoreType.DMA((2 * _N_CHIPS,))],
    )
    return _call(
        body, name="grad_w_in_scatter", grid_spec=grid_spec,
        out_shape=[jax.ShapeDtypeStruct(shp, F32) for shp in shapes] + [jax.ShapeDtypeStruct(small_partial.shape, F32)],
        compiler_params=_params(60, 2),
    )(order, dproj, u, rs_a, rs_c, small_partial)


def _adam_update(g, w, m, v):
    m2 = ADAM_B1 * m + (1.0 - ADAM_B1) * g
    v2 = ADAM_B2 * v + (1.0 - ADAM_B2) * (g * g)
    m_hat = m2 / (1.0 - ADAM_B1 ** ADAM_STEP)
    v_hat = v2 / (1.0 - ADAM_B2 ** ADAM_STEP)
    delta = -ADAM_LR * (m_hat / (jnp.sqrt(v_hat) + ADAM_EPS) + ADAM_WD * w)
    return delta, m2, v2


def _adam_rows(g, w, m, v):
    rows, cols = w.shape
    tm = rows if rows % 256 else 256

    def fn(rv, cr, out):
        return list(_adam_update(*rv)), []

    outs, _ = _rowwise("adamw", fn, rows, tm, [(a, cols, 0) for a in (g, w, m, v)], [], [(cols, F32)] * 3, [], 32)
    return outs


_SMALL = ["norm_g", "hg_lb", "hg_norm_g", "s5_a_re", "s5_a_im", "s5_log_dt", "s5_b_re", "s5_b_im", "s5_c_re",
          "s5_c_im", "s5_d", "b_glu", "ple_norm_g", "final_norm_g"]
_BIG = ["w_in", "w_o_hg", "w_glu", "w_o_s5", "w_out", "w_ple", "w_ple_gate"]
_ORDER = ["norm_g", "w_in", "hg_lb", "hg_norm_g", "w_o_hg", "s5_a_re", "s5_a_im", "s5_log_dt", "s5_b_re", "s5_b_im",
          "s5_c_re", "s5_c_im", "s5_d", "w_glu", "b_glu", "w_o_s5", "w_out", "ple_norm_g", "w_ple", "w_ple_gate",
          "final_norm_g"]


def _pack_small(vals, tail=None):
    parts = []
    for name in _SMALL:
        flat = vals[name].reshape(-1).astype(F32)
        pad = (-flat.shape[0]) % 1024
        parts.append(jnp.pad(flat, (0, pad)))
    tail = jnp.zeros((0,), F32) if tail is None else tail.reshape(-1).astype(F32)
    parts.append(jnp.pad(tail, (0, 1024 - tail.shape[0])))
    return jnp.concatenate(parts).reshape(-1, 128)


def _unpack_small(packed, like):
    flat = packed.reshape(-1)
    out, off = {}, 0
    for name in _SMALL:
        size = like[name].size
        out[name] = flat[off:off + size].reshape(like[name].shape)
        off += size + (-size) % 1024
    return out


def _col_blocks(full):
    k = full.shape[0]
    return full.reshape(k, N_DEV, 128).transpose(1, 0, 2)


def _from_col_blocks(blocks):
    k = blocks.shape[1]
    return blocks.transpose(1, 0, 2).reshape(k, N_DEV * 128)


def kernel(x, p, norm_g, w_in, hg_lb, hg_norm_g, w_o_hg, s5_a_re, s5_a_im, s5_log_dt, s5_b_re, s5_b_im, s5_c_re, s5_c_im, s5_d, w_glu, b_glu, w_o_s5, w_out, ple_norm_g, w_ple, w_ple_gate, final_norm_g, loss_target, m_norm_g, m_w_in, m_hg_lb, m_hg_norm_g, m_w_o_hg, m_s5_a_re, m_s5_a_im, m_s5_log_dt, m_s5_b_re, m_s5_b_im, m_s5_c_re, m_s5_c_im, m_s5_d, m_w_glu, m_b_glu, m_w_o_s5, m_w_out, m_ple_norm_g, m_w_ple, m_w_ple_gate, m_final_norm_g, v_norm_g, v_w_in, v_hg_lb, v_hg_norm_g, v_w_o_hg, v_s5_a_re, v_s5_a_im, v_s5_log_dt, v_s5_b_re, v_s5_b_im, v_s5_c_re, v_s5_c_im, v_s5_d, v_w_glu, v_b_glu, v_w_o_s5, v_w_out, v_ple_norm_g, v_w_ple, v_w_ple_gate, v_final_norm_g):
    args = dict(locals())
    w = {n: args[n] for n in _ORDER}
    m = {n: args["m_" + n] for n in _ORDER}
    v = {n: args["v_" + n] for n in _ORDER}
    xs = x[0]
    ps = p[0, 0]
    tgt = loss_target[0]
    S = xs.shape[0]

    pack_a = jnp.concatenate([w_o_hg[0], w_out[0], w_ple_gate[0]], axis=0).astype(BF16)
    pack_b = w_in[0].astype(BF16)
    pack_c = jnp.concatenate([w_glu[0], w_o_s5[0], w_ple[0]], axis=0).astype(BF16)
    u, proj, all_a, all_b, all_c = _inproj_gather(xs, norm_g, pack_a, pack_b, pack_c)
    wf_o_hg = all_a[:, 0:128].reshape(D_MODEL, D_MODEL)
    wf_out = all_a[:, 128:256].reshape(D_MODEL, D_MODEL)
    wf_pg = all_a[:, 256:384].reshape(D_MODEL, D_MODEL)
    wf_glu = _from_col_blocks(all_c[:, 0:512])
    wf_o_s5 = _from_col_blocks(all_c[:, 512:1024])
    wf_ple = _from_col_blocks(all_c[:, 1024:1280])

    lb = jax.nn.sigmoid(hg_lb[0:1] - hg_lb[1:2])
    s5_names = ["s5_a_re", "s5_a_im", "s5_log_dt", "s5_b_re", "s5_b_im", "s5_c_re", "s5_c_im", "s5_d"]
    build = lambda *a: _s5_matrices(*a, seg_len=S // S5_SEG)
    mats_f32, mats_vjp = jax.vjp(build, *[w[n][0] for n in s5_names])
    mats = dict(mats_f32, b_q=mats_f32["b_q"].astype(BF16), c_q=mats_f32["c_q"].astype(BF16))
    bias_glu = b_glu

    o, states = _hgrn_fwd(proj, lb)
    u_perm = _to_segments(proj[:, COL_US:COL_US + S5_WIDTH])
    zeros_state = jnp.zeros((S5_SEG, S5_COLS), F32)
    (h_ends,) = _s5_fwd_pass(u_perm, mats, zeros_state, False)
    h0 = _segment_starts(h_ends, mats["pow_r"], mats["pow_i"], False)
    y_perm, h_init, _ = _s5_fwd_pass(u_perm, mats, h0, True)
    ys = _from_segments(y_perm)
    y_hg, y_s5, glu, h1 = _stage_branches(o, proj, ys, xs, hg_norm_g, wf_o_hg, wf_glu, bias_glu, wf_o_s5, wf_out)

    dh1, (loss_acc, d_final_g, d_ple_g, d_w_ple, d_w_pg) = _stage_ple_loss(
        h1, ps, tgt, ple_norm_g, wf_ple, wf_pg, final_norm_g.reshape(1, D_MODEL))
    (d_gate_hg, d_gate_s5, d_yhg, d_ys5), (d_w_out,) = _stage_bwd_merge(dh1, y_hg, y_s5, proj, wf_out)
    (d_o, d_g_hg), (d_w_o_hg, d_hg_norm) = _stage_bwd_hg_path(d_yhg, o, proj, hg_norm_g, wf_o_hg)
    (d_ys, d_z), (d_w_o_s5, d_w_glu, d_b_glu) = _stage_bwd_s5_path(d_ys5, ys, glu, proj, wf_o_s5, wf_glu)
    dq, df, div, d_lb = _hgrn_bwd(proj, lb, d_o, states)
    dy_perm = _to_segments(d_ys)
    g_ends = _s5_bwd_ends(dy_perm, mats)
    g0 = _segment_starts(g_ends, mats["pow_r"], mats["pow_i"], True)
    du_perm, acc_p, acc_q, d_bq, d_cq_t, d_d = _s5_bwd_full(u_perm, dy_perm, h_init, g0, mats)
    d_us = _from_segments(du_perm)
    grad_x, dproj, d_norm_g = _stage_inproj_bwd([dq, df, div, d_g_hg, d_us, d_z, d_gate_hg, d_gate_s5], xs, dh1,
                                                norm_g, all_b)

    p_re, p_im = _split_cols(acc_p)
    q_re, q_im = _split_cols(acc_q)
    d_lam_r = (p_re + p_im)[0]
    d_lam_i = (q_im - q_re)[0]
    zero_row = jnp.zeros((S5_SEG, S5_COLS), F32)
    row_of = lambda re_part, im_part: zero_row.at[0].set(_join_cols(re_part[None], im_part[None])[0])
    zeros_q = jnp.zeros_like(d_lam_r)
    cot = dict(
        lam_r=row_of(d_lam_r, zeros_q), lam_i=row_of(zeros_q, d_lam_i),
        b_q=d_bq, c_q=d_cq_t.transpose(0, 2, 1), d_row=d_d,
        pow_r=jnp.zeros_like(mats["pow_r"]), pow_i=jnp.zeros_like(mats["pow_i"]),
    )
    d_s5 = mats_vjp(cot)

    s_lb = lb * (1.0 - lb)
    d_hg_lb = jnp.concatenate([d_lb * s_lb, -d_lb * s_lb], axis=0)
    small_g = dict(norm_g=d_norm_g, hg_lb=d_hg_lb, hg_norm_g=d_hg_norm, b_glu=d_b_glu, ple_norm_g=d_ple_g,
                   final_norm_g=d_final_g)
    for name, g in zip(s5_names, d_s5):
        small_g[name] = g
    pk = lambda d: _pack_small({n: d[n] for n in _SMALL})
    rs_a = jnp.concatenate([d_w_o_hg.reshape(N_DEV, 128, D_MODEL), d_w_out.reshape(N_DEV, 128, D_MODEL),
                            d_w_pg.reshape(N_DEV, 128, D_MODEL)], axis=1).astype(BF16)
    rs_c = jnp.concatenate([_col_blocks(d_w_glu), _col_blocks(d_w_o_s5), _col_blocks(d_w_ple)], axis=1).astype(BF16)
    partial = _pack_small({n: small_g[n] for n in _SMALL}, tail=loss_acc[0, 0:1])
    g_b, g_a, g_c, sg = _grad_w_in_scatter(dproj, u, rs_a, rs_c, partial)
    sd, sm, sv = _adam_rows(sg, pk(w), pk(m), pk(v))
    like = {n: w[n] for n in _SMALL}
    out_g, out_d, out_m, out_v = (_unpack_small(t, like) for t in (sg, sd, sm, sv))
    big_g = dict(w_o_hg=g_a[0:128], w_out=g_a[128:256], w_ple_gate=g_a[256:384], w_in=g_b,
                 w_glu=g_c[0:512], w_o_s5=g_c[512:1024], w_ple=g_c[1024:1280])
    for name in _BIG:
        shape = w[name].shape
        g2 = big_g[name]
        d2, m2, v2 = _adam_rows(g2, w[name][0], m[name][0], v[name][0])
        out_g[name], out_d[name], out_m[name], out_v[name] = (t.reshape(shape) for t in (g2, d2, m2, v2))

    loss = sg[sg.shape[0] - 8, 0]
    return (loss, grad_x[None], *[out_g[n] for n in _ORDER], *[out_d[n] for n in _ORDER],
            *[out_m[n] for n in _ORDER], *[out_v[n] for n in _ORDER])
```

```python
import functools
import math

import jax
import jax.numpy as jnp
from jax import lax
from jax.experimental import pallas as pl
from jax.experimental.pallas import tpu as pltpu

F32 = jnp.float32
BF16 = jnp.bfloat16

D_MODEL = 1024
N_DEV = 8
IN_COLS = 7168
SHARD_IN = IN_COLS // N_DEV
HG_HEADS = 8
HG_DIM = 128
HG_CHUNK = 64
HG_SUPER = 128
HG_HEADS_PER_STEP = 8
S5_WIDTH = 512
S5_GROUPS = 32
S5_STATE = 64
S5_CH = 16
S5_SEG = 8
S5_QUADS = 4
S5_QCOLS = 1024
S5_COLS = S5_QUADS * S5_QCOLS
S5_TILE_STEPS = 64
S5_UNROLL = 8
NORM_EPS = 1e-6
ADAM_LR = 0.001
ADAM_B1 = 0.9
ADAM_B2 = 0.999
ADAM_EPS = 1e-08
ADAM_WD = 0.01
ADAM_STEP = 10
MIB = 1024 * 1024
MESH = pl.DeviceIdType.MESH

COL_Q, COL_F, COL_I, COL_G, COL_US, COL_ZS, COL_GH, COL_GS = 0, 1024, 2048, 3072, 4096, 4608, 5120, 6144


def _call(body, **kw):
    return pl.pallas_call(body, **kw)


def _params(vmem_mb, n_grid=1):
    return pltpu.CompilerParams(
        dimension_semantics=("arbitrary",) * n_grid, vmem_limit_bytes=vmem_mb * MIB
    )


def _bdot(a, b):
    return jnp.dot(a.astype(BF16), b.astype(BF16), preferred_element_type=F32)


def _bdot_nt(a, b):
    return lax.dot_general(a.astype(BF16), b.astype(BF16), (((1,), (1,)), ((), ())), preferred_element_type=F32)


def _bdot_tn(a, b):
    return lax.dot_general(a.astype(BF16), b.astype(BF16), (((0,), (0,)), ((), ())), preferred_element_type=F32)


def _sigmoid(x):
    return jax.nn.sigmoid(x)


def _silu(x):
    return x * _sigmoid(x)


def _dsilu(x):
    s = _sigmoid(x)
    return s * (1.0 + x * (1.0 - s))


_GELU_C = math.sqrt(2.0 / math.pi)


def _gelu(x):
    return 0.5 * x * (1.0 + jnp.tanh(_GELU_C * (x + 0.044715 * x * x * x)))


def _dgelu(x):
    t = jnp.tanh(_GELU_C * (x + 0.044715 * x * x * x))
    return 0.5 * (1.0 + t) + 0.5 * x * (1.0 - t * t) * _GELU_C * (1.0 + 3.0 * 0.044715 * x * x)


def _rms_fwd(x, g):
    r = lax.rsqrt(jnp.mean(x * x, axis=-1, keepdims=True) + NORM_EPS)
    n = x * r
    return n * g, n, r


def _rms_bwd(dy, n, r, g):
    dn = dy * g
    dx = r * (dn - n * jnp.mean(dn * n, axis=-1, keepdims=True))
    return dx, jnp.sum(dy * n, axis=0, keepdims=True)


def _head_rms_fwd(o, g):
    ns, rs = [], []
    for h in range(HG_HEADS):
        oh = o[:, h * HG_DIM:(h + 1) * HG_DIM]
        r = lax.rsqrt(jnp.mean(oh * oh, axis=-1, keepdims=True) + NORM_EPS)
        ns.append(oh * r)
        rs.append(r)
    n = jnp.concatenate(ns, axis=1)
    return n * g, n, rs


def _head_rms_bwd(dy, n, rs, g):
    dn = dy * g
    dxs = []
    for h in range(HG_HEADS):
        sl = slice(h * HG_DIM, (h + 1) * HG_DIM)
        dxs.append(rs[h] * (dn[:, sl] - n[:, sl] * jnp.mean(dn[:, sl] * n[:, sl], axis=-1, keepdims=True)))
    return jnp.concatenate(dxs, axis=1), jnp.sum(dy * n, axis=0, keepdims=True)


def _rowwise(name, fn, n_rows, tm, rows, consts, out_rows, out_accs, vmem_mb, parts=1):
    n_r, n_c, n_or, n_oa = len(rows), len(consts), len(out_rows), len(out_accs)
    tp = tm // parts

    def body(*refs):
        r_refs = refs[:n_r]
        c_refs = refs[n_r:n_r + n_c]
        or_refs = refs[n_r + n_c:n_r + n_c + n_or]
        oa_refs = refs[n_r + n_c + n_or:]

        if n_oa:
            @pl.when(pl.program_id(0) == 0)
            def _():
                for ref in oa_refs:
                    ref[...] = jnp.zeros(ref.shape, ref.dtype)

        for part in range(parts):
            sl = slice(part * tp, (part + 1) * tp)
            outs, accs = fn([r[sl, :] for r in r_refs], c_refs, [o.at[sl, :] for o in or_refs])
            for ref, v in zip(or_refs, outs):
                if v is not None:
                    ref[sl, :] = v.astype(ref.dtype)
            for ref, v in zip(oa_refs, accs):
                ref[...] += v.astype(ref.dtype)

    in_specs = [pl.BlockSpec((tm, w), functools.partial(lambda i, c: (i, c), c=cb)) for (_, w, cb) in rows]
    in_specs += [pl.BlockSpec(c.shape, functools.partial(lambda i, nd: (0,) * nd, nd=c.ndim),
                              pipeline_mode=pl.Buffered(1)) for c in consts]
    out_specs = [pl.BlockSpec((tm, w), lambda i: (i, 0)) for (w, _) in out_rows]
    out_specs += [pl.BlockSpec(s, functools.partial(lambda i, nd: (0,) * nd, nd=len(s))) for (s, _) in out_accs]
    out_shape = [jax.ShapeDtypeStruct((n_rows, w), dt) for (w, dt) in out_rows]
    out_shape += [jax.ShapeDtypeStruct(s, dt) for (s, dt) in out_accs]
    res = _call(
        body, name=name, grid=(n_rows // tm,), in_specs=in_specs, out_specs=out_specs, out_shape=out_shape,
        compiler_params=_params(vmem_mb),
    )(*[a for (a, _, _) in rows], *consts)
    return res[:n_or], res[n_or:]


def _stage_branches(o, proj, ys, x, hg_norm_g, w_o_hg, w_glu, b_glu, w_o_s5, w_out):
    S = x.shape[0]

    def fn(rv, cr, out):
        o_b, g_hg, z_s, gate_hg, gate_s5, ys_b, x_b = rv
        gn_ref, wohg_ref, wglu_ref, bglu_ref, wos5_ref, wout_ref = cr
        on, _, _ = _head_rms_fwd(o_b, gn_ref[...])
        a = on * _silu(g_hg)
        y_hg = jnp.dot(a.astype(BF16), wohg_ref[...], preferred_element_type=F32)
        gl = _gelu(ys_b)
        glu = jnp.dot(gl.astype(BF16), wglu_ref[...], preferred_element_type=F32) + bglu_ref[...]
        ys2 = glu[:, :S5_WIDTH] * _sigmoid(glu[:, S5_WIDTH:]) * _silu(z_s)
        y_s5 = jnp.dot(ys2.astype(BF16), wos5_ref[...], preferred_element_type=F32)
        merged = _sigmoid(gate_hg) * y_hg + _sigmoid(gate_s5) * y_s5
        h1 = x_b + jnp.dot(merged.astype(BF16), wout_ref[...], preferred_element_type=F32)
        return [y_hg, y_s5, glu, h1], []

    rows = [(o, D_MODEL, 0), (proj, D_MODEL, COL_G // D_MODEL), (proj, S5_WIDTH, COL_ZS // S5_WIDTH),
            (proj, D_MODEL, COL_GH // D_MODEL), (proj, D_MODEL, COL_GS // D_MODEL), (ys, S5_WIDTH, 0), (x, D_MODEL, 0)]
    (y_hg, y_s5, glu, h1), _ = _rowwise(
        "branches", fn, S, 256, rows, [hg_norm_g, w_o_hg, w_glu, b_glu, w_o_s5, w_out],
        [(D_MODEL, F32)] * 4, [], 56)
    return y_hg, y_s5, glu, h1


def _stage_ple_loss(h1, p, target, ple_norm_g, w_ple, w_ple_gate, final_norm_g):
    S = h1.shape[0]

    def fn(rv, cr, out):
        h1_b, p_b, t_b = rv
        gp_ref, wple_ref, wpg_ref, gf_ref = cr
        n2g, n2, r2 = _rms_fwd(h1_b, gp_ref[...])
        z = jnp.dot(n2g.astype(BF16), wpg_ref[...], preferred_element_type=F32)
        gate = _sigmoid(z)
        pe = jnp.dot(p_b.astype(BF16), wple_ref[...], preferred_element_type=F32)
        h2 = h1_b + pe * gate
        y, nf, rf = _rms_fwd(h2, gf_ref[...])
        err = y - t_b
        loss_rows = 0.5 * jnp.mean(err * err, axis=-1, keepdims=True)
        loss_inc = jnp.broadcast_to(jnp.sum(loss_rows, axis=0, keepdims=True), (1, 128))
        dy = err * (1.0 / D_MODEL)
        dh2, d_gf = _rms_bwd(dy, nf, rf, gf_ref[...])
        d_pe = dh2 * gate
        dz = dh2 * pe * gate * (1.0 - gate)
        d_wple = _bdot_tn(p_b, d_pe)
        d_wpg = _bdot_tn(n2g, dz)
        dn2g = _bdot_nt(dz, wpg_ref[...])
        dh1n, d_gp = _rms_bwd(dn2g, n2, r2, gp_ref[...])
        return [dh2 + dh1n], [loss_inc, d_gf, d_gp, d_wple, d_wpg]

    (dh1,), accs = _rowwise(
        "ple_loss", fn, S, 512, [(h1, D_MODEL, 0), (p, 256, 0), (target, D_MODEL, 0)],
        [ple_norm_g, w_ple, w_ple_gate, final_norm_g], [(D_MODEL, F32)],
        [((1, 128), F32), ((1, D_MODEL), F32), ((1, D_MODEL), F32), ((256, D_MODEL), F32), ((D_MODEL, D_MODEL), F32)], 56,
        parts=2)
    return dh1, accs


def _stage_bwd_merge(dh1, y_hg, y_s5, proj, w_out):
    S = dh1.shape[0]

    def fn(rv, cr, out):
        dh1_b, yhg, ys5, gate_hg, gate_s5 = rv
        (wout_ref,) = cr
        sg_h, sg_s = _sigmoid(gate_hg), _sigmoid(gate_s5)
        merged = sg_h * yhg + sg_s * ys5
        d_wout = _bdot_tn(merged, dh1_b)
        d_merged = _bdot_nt(dh1_b, wout_ref[...])
        d_gate_hg = d_merged * yhg * sg_h * (1.0 - sg_h)
        d_gate_s5 = d_merged * ys5 * sg_s * (1.0 - sg_s)
        return [d_gate_hg, d_gate_s5, d_merged * sg_h, d_merged * sg_s], [d_wout]

    rows = [(dh1, D_MODEL, 0), (y_hg, D_MODEL, 0), (y_s5, D_MODEL, 0), (proj, D_MODEL, COL_GH // D_MODEL),
            (proj, D_MODEL, COL_GS // D_MODEL)]
    outs, accs = _rowwise("bwd_merge", fn, S, 512, rows, [w_out], [(D_MODEL, BF16)] * 4,
                          [((D_MODEL, D_MODEL), F32)], 56, parts=2)
    return outs, accs


def _stage_bwd_hg_path(d_yhg, o, proj, hg_norm_g, w_o_hg):
    S = o.shape[0]

    def fn(rv, cr, out):
        d_yhg_b, o_b, g_hg = rv
        gn_ref, wohg_ref = cr
        ong, on, rs = _head_rms_fwd(o_b, gn_ref[...])
        sil = _silu(g_hg)
        d_wohg = _bdot_tn(ong * sil, d_yhg_b)
        d_a = _bdot_nt(d_yhg_b, wohg_ref[...])
        d_g_hg = d_a * ong * _dsilu(g_hg)
        d_o, d_gn = _head_rms_bwd(d_a * sil, on, rs, gn_ref[...])
        return [d_o, d_g_hg], [d_wohg, d_gn]

    rows = [(d_yhg, D_MODEL, 0), (o, D_MODEL, 0), (proj, D_MODEL, COL_G // D_MODEL)]
    outs, accs = _rowwise("bwd_hg_path", fn, S, 512, rows, [hg_norm_g, w_o_hg], [(D_MODEL, BF16)] * 2,
                          [((D_MODEL, D_MODEL), F32), ((1, D_MODEL), F32)], 56, parts=2)
    return outs, accs


def _stage_bwd_s5_path(d_ys5, ys, glu, proj, w_o_s5, w_glu):
    S = ys.shape[0]

    def fn(rv, cr, out):
        d_ys5_b, ys_b, glu_b, z_s = rv
        wos5_ref, wglu_ref = cr
        ga, gb = glu_b[:, :S5_WIDTH], glu_b[:, S5_WIDTH:]
        sgb, silz = _sigmoid(gb), _silu(z_s)
        ys2 = ga * sgb * silz
        d_wos5 = _bdot_tn(ys2, d_ys5_b)
        d_ys2 = _bdot_nt(d_ys5_b, wos5_ref[...])
        d_ga = d_ys2 * sgb * silz
        d_gb = d_ys2 * ga * sgb * (1.0 - sgb) * silz
        d_z = d_ys2 * ga * sgb * _dsilu(z_s)
        d_glu = jnp.concatenate([d_ga, d_gb], axis=1)
        gl = _gelu(ys_b)
        d_wglu = _bdot_tn(gl, d_glu)
        d_bglu = jnp.sum(d_glu, axis=0, keepdims=True)
        d_gl = _bdot_nt(d_glu, wglu_ref[...])
        return [d_gl * _dgelu(ys_b), d_z], [d_wos5, d_wglu, d_bglu]

    rows = [(d_ys5, D_MODEL, 0), (ys, S5_WIDTH, 0), (glu, D_MODEL, 0), (proj, S5_WIDTH, COL_ZS // S5_WIDTH)]
    outs, accs = _rowwise(
        "bwd_s5_path", fn, S, 512, rows, [w_o_s5, w_glu], [(S5_WIDTH, F32), (S5_WIDTH, BF16)],
        [((S5_WIDTH, D_MODEL), F32), ((S5_WIDTH, D_MODEL), F32), ((1, D_MODEL), F32)], 48, parts=2)
    return outs, accs


def _stage_inproj_bwd(pieces, x, dh1, norm_g, w_in_all):
    S = x.shape[0]

    def fn(rv, cr, out):
        g_ref, w_ref = cr
        x_b, dh1_b = rv[8], rv[9]
        dproj_ref = out[1]
        col = 0
        for v in rv[:8]:
            dproj_ref[:, col:col + v.shape[1]] = v.astype(BF16)
            col += v.shape[1]
        d_u = jnp.zeros((x_b.shape[0], D_MODEL), F32)
        for j in range(N_DEV):
            d_u = d_u + lax.dot_general(dproj_ref[:, j * SHARD_IN:(j + 1) * SHARD_IN], w_ref[j],
                                        (((1,), (1,)), ((), ())), preferred_element_type=F32)
        _, n, r = _rms_fwd(x_b, g_ref[...])
        dx, d_g = _rms_bwd(d_u, n, r, g_ref[...])
        return [dh1_b + dx, None], [d_g]

    rows = [(a, a.shape[1], 0) for a in pieces] + [(x, D_MODEL, 0), (dh1, D_MODEL, 0)]
    (grad_x, dproj), (d_g,) = _rowwise(
        "inproj_bwd", fn, S, 256, rows, [norm_g, w_in_all], [(D_MODEL, F32), (IN_COLS, BF16)],
        [((1, D_MODEL), F32)], 56)
    return grad_x, dproj, d_g


def _chunk_row(shape):
    return lax.broadcasted_iota(jnp.int32, shape, 0) & (HG_CHUNK - 1)


def _chunk_cumsum(x):
    r_in = _chunk_row(x.shape)
    s = 1
    while s < HG_CHUNK:
        x = x + jnp.where(r_in >= s, pltpu.roll(x, s, 0), 0.0)
        s *= 2
    return x


def _chunk_suffix_sum(x):
    n = x.shape[0]
    r_in = _chunk_row(x.shape)
    s = 1
    while s < HG_CHUNK:
        x = x + jnp.where(r_in < HG_CHUNK - s, pltpu.roll(x, n - s, 0), 0.0)
        s *= 2
    return x


def _hgrn_prep(q, fl, lb):
    nc = HG_SUPER // HG_CHUNK
    sig = _sigmoid(fl)
    f = lb + (1.0 - lb) * sig
    k = (1.0 - lb) * (1.0 - sig)
    b = _chunk_cumsum(jnp.log(f))
    b3 = b.reshape(nc, HG_CHUNK, HG_DIM)
    row3 = lax.broadcasted_iota(jnp.int32, b3.shape, 1)
    pick = lambda r: jnp.sum(jnp.where(row3 == r, b3, 0.0), axis=1, keepdims=True)
    b_mid = pick(HG_CHUNK // 2 - 1)
    b_last = pick(HG_CHUNK - 1)
    flat = lambda t: t.reshape(HG_SUPER, HG_DIM)
    e_qa = flat(jnp.exp(b3 - b_mid))
    e_ka = flat(jnp.exp(b_mid - b3))
    e_qd = jnp.exp(b)
    e_kd = flat(jnp.exp(b_last - b3))
    dc = jnp.exp(b_last)
    return sig, f, k, e_qa, e_ka, e_qd, e_kd, dc


def _hgrn_mask():
    r = lax.broadcasted_iota(jnp.int32, (HG_SUPER, HG_SUPER), 0)
    c = lax.broadcasted_iota(jnp.int32, (HG_SUPER, HG_SUPER), 1)
    shift = HG_CHUNK.bit_length() - 1
    return (jnp.right_shift(r, shift) == jnp.right_shift(c, shift)) & (r >= c)


def _hgrn_fwd(proj, lb):
    S = proj.shape[0]
    nb = S // HG_SUPER
    nc = HG_SUPER // HG_CHUNK
    hp = HG_HEADS_PER_STEP
    wide = hp * HG_DIM

    def body(q_ref, f_ref, iv_ref, lb_ref, o_ref, st_ref, state):
        @pl.when(pl.program_id(1) == 0)
        def _():
            state[...] = jnp.zeros(state.shape, F32)

        mask = _hgrn_mask()
        for hh in range(hp):
            lanes = slice(hh * HG_DIM, (hh + 1) * HG_DIM)
            q, iv = q_ref[:, lanes], iv_ref[:, lanes]
            _, _, k, e_qa, e_ka, e_qd, e_kd, dc = _hgrn_prep(q, f_ref[:, lanes], lb_ref[:, lanes])
            scores = jnp.where(mask, _bdot_nt(q * e_qa, k * e_ka), 0.0)
            o_intra = _bdot(scores, iv)
            qd, kd = q * e_qd, k * e_kd
            for c in range(nc):
                sl = slice(c * HG_CHUNK, (c + 1) * HG_CHUNK)
                st = state[hh]
                st_ref[hh, c] = st
                o_ref[sl, lanes] = o_intra[sl] + _bdot_nt(qd[sl], st)
                state[hh] = dc[c] * st + _bdot_tn(iv[sl], kd[sl])

    blk = lambda base: pl.BlockSpec((HG_SUPER, wide), functools.partial(lambda h, i, b: (i, b + h), b=base // wide))
    return _call(
        body, name="hgrn_fwd", grid=(HG_HEADS // hp, nb),
        in_specs=[blk(COL_Q), blk(COL_F), blk(COL_I), pl.BlockSpec((1, wide), lambda h, i: (0, h))],
        out_specs=[pl.BlockSpec((HG_SUPER, wide), lambda h, i: (i, h)),
                   pl.BlockSpec((hp, nc, HG_DIM, HG_DIM), lambda h, i: (h, i, 0, 0))],
        out_shape=[jax.ShapeDtypeStruct((S, D_MODEL), F32),
                   jax.ShapeDtypeStruct((HG_HEADS, S // HG_CHUNK, HG_DIM, HG_DIM), F32)],
        scratch_shapes=[pltpu.VMEM((hp, HG_DIM, HG_DIM), F32)],
        compiler_params=_params(40, 2),
    )(proj, proj, proj, lb)


def _hgrn_bwd(proj, lb, d_o, states):
    S = proj.shape[0]
    nb = S // HG_SUPER
    nc = HG_SUPER // HG_CHUNK
    hp = HG_HEADS_PER_STEP
    wide = hp * HG_DIM

    def body(q_ref, f_ref, iv_ref, lb_ref, do_ref, st_ref, dq_ref, df_ref, div_ref, dlb_ref, dstate):
        @pl.when(pl.program_id(1) == 0)
        def _():
            dstate[...] = jnp.zeros(dstate.shape, F32)
            dlb_ref[...] = jnp.zeros(dlb_ref.shape, F32)

        mask = _hgrn_mask()
        for hh in range(hp):
            lanes = slice(hh * HG_DIM, (hh + 1) * HG_DIM)
            q, iv, do, lb_v = q_ref[:, lanes], iv_ref[:, lanes], do_ref[:, lanes], lb_ref[:, lanes]
            sig, f, k, e_qa, e_ka, e_qd, e_kd, dc = _hgrn_prep(q, f_ref[:, lanes], lb_v)
            qa, ka, qd, kd = q * e_qa, k * e_ka, q * e_qd, k * e_kd
            scores = jnp.where(mask, _bdot_nt(qa, ka), 0.0)
            d_scores = jnp.where(mask, _bdot_nt(do, iv), 0.0)
            d_iv_intra = _bdot_tn(scores, do)
            d_qa = _bdot(d_scores, ka)
            d_ka = _bdot_tn(d_scores, qa)
            d_qd, d_kd, d_last = [None] * nc, [None] * nc, [None] * nc
            for c in reversed(range(nc)):
                sl = slice(c * HG_CHUNK, (c + 1) * HG_CHUNK)
                st = st_ref[hh, c]
                ds = dstate[hh]
                d_qd[c] = _bdot(do[sl], st)
                d_kd[c] = _bdot(iv[sl], ds)
                div_ref[sl, lanes] = (d_iv_intra[sl] + _bdot_nt(kd[sl], ds)).astype(div_ref.dtype)
                d_last[c] = (jnp.sum(ds * st, axis=0, keepdims=True) * dc[c]
                             + jnp.sum(d_kd[c] * kd[sl], axis=0, keepdims=True))
                dstate[hh] = dc[c] * ds + _bdot_tn(do[sl], qd[sl])
            d_qd = jnp.concatenate(d_qd, axis=0)
            d_kd = jnp.concatenate(d_kd, axis=0)
            d_b = d_qa * qa - d_ka * ka + d_qd * qd - d_kd * kd
            last_rows = jnp.concatenate([jnp.broadcast_to(t, (HG_CHUNK, HG_DIM)) for t in d_last], axis=0)
            d_b = d_b + jnp.where(_chunk_row(d_b.shape) == HG_CHUNK - 1, last_rows, 0.0)
            d_logf = _chunk_suffix_sum(d_b)
            d_k = d_ka * e_ka + d_kd * e_kd
            g_f = d_logf / f
            d_sig = (g_f - d_k) * (1.0 - lb_v)
            dq_ref[:, lanes] = (d_qa * e_qa + d_qd * e_qd).astype(dq_ref.dtype)
            df_ref[:, lanes] = (d_sig * sig * (1.0 - sig)).astype(df_ref.dtype)
            d_lb = jnp.sum((g_f - d_k) * (1.0 - sig), axis=0, keepdims=True)
            dlb_ref[:, lanes] += jnp.broadcast_to(d_lb, (8, HG_DIM))

    rev = lambda i: nb - 1 - i
    blk = lambda base: pl.BlockSpec((HG_SUPER, wide), functools.partial(lambda h, i, b: (rev(i), b + h), b=base // wide))
    row_out = pl.BlockSpec((HG_SUPER, wide), lambda h, i: (rev(i), h))
    dq, df, div, dlb = _call(
        body, name="hgrn_bwd", grid=(HG_HEADS // hp, nb),
        in_specs=[blk(COL_Q), blk(COL_F), blk(COL_I), pl.BlockSpec((1, wide), lambda h, i: (0, h)),
                  pl.BlockSpec((HG_SUPER, wide), lambda h, i: (rev(i), h)),
                  pl.BlockSpec((hp, nc, HG_DIM, HG_DIM), lambda h, i: (h, rev(i), 0, 0))],
        out_specs=[row_out, row_out, row_out, pl.BlockSpec((8, wide), lambda h, i: (0, h))],
        out_shape=[jax.ShapeDtypeStruct((S, D_MODEL), BF16)] * 3 + [jax.ShapeDtypeStruct((8, D_MODEL), F32)],
        scratch_shapes=[pltpu.VMEM((hp, HG_DIM, HG_DIM), F32)],
        compiler_params=_params(40, 2),
    )(proj, proj, proj, lb, d_o, states)
    return dq, df, div, dlb[0:1]


def _s5_matrices(a_re, a_im, log_dt, b_re, b_im, c_re, c_im, d, seg_len):
    dt = jnp.exp(log_dt)[:, None]
    mag = jnp.exp(a_re * dt)
    lr, li = mag * jnp.cos(a_im * dt), mag * jnp.sin(a_im * dt)
    den = a_re * a_re + a_im * a_im
    nr = lr - 1.0
    sr = (nr * a_re + li * a_im) / den
    si = (li * a_re - nr * a_im) / den
    bbr = sr[..., None] * b_re - si[..., None] * b_im
    bbi = sr[..., None] * b_im + si[..., None] * b_re
    eye = jnp.eye(8, dtype=F32)

    def quad_cols(v):
        return v.reshape(S5_QUADS, 8 * S5_STATE)

    def lam_row(re_part, im_part):
        row = jnp.concatenate([quad_cols(re_part), quad_cols(im_part)], axis=1).reshape(1, S5_COLS)
        return jnp.broadcast_to(row, (S5_SEG, S5_COLS))

    def b_mat(bb):
        t = bb.reshape(S5_QUADS, 8, S5_STATE, S5_CH)
        return jnp.einsum("qgnc,gh->qgchn", t, eye).reshape(S5_QUADS, 8 * S5_CH, 8 * S5_STATE)

    def c_mat(cc):
        t = cc.reshape(S5_QUADS, 8, S5_CH, S5_STATE)
        return jnp.einsum("qgcn,gh->qgnhc", t, eye).reshape(S5_QUADS, 8 * S5_STATE, 8 * S5_CH)

    ang = a_im * dt * seg_len
    magp = jnp.exp(a_re * dt * seg_len)
    lpr, lpi = magp * jnp.cos(ang), magp * jnp.sin(ang)
    return dict(
        lam_r=lam_row(lr, lr), lam_i=lam_row(-li, li),
        b_q=jnp.concatenate([b_mat(bbr), b_mat(bbi)], axis=2),
        c_q=jnp.concatenate([c_mat(c_re), -c_mat(c_im)], axis=1),
        d_row=d.reshape(1, S5_WIDTH), pow_r=quad_cols(lpr), pow_i=quad_cols(lpi),
    )


def _s5_parts(v):
    half = S5_QCOLS // 2
    return tuple(v[:, k * half:(k + 1) * half] for k in range(2 * S5_QUADS))


def _s5_advance(parts, lr_ref, li_ref, x_ref, sl, conj):
    half = S5_QCOLS // 2
    out = []
    for q in range(S5_QUADS):
        re_c = slice(q * S5_QCOLS, q * S5_QCOLS + half)
        im_c = slice(q * S5_QCOLS + half, (q + 1) * S5_QCOLS)
        lr, li = lr_ref[:, re_c], li_ref[:, im_c]
        hr, hi = parts[2 * q], parts[2 * q + 1]
        if conj:
            out += [lr * hr + li * hi + x_ref[sl, re_c], lr * hi - li * hr + x_ref[sl, im_c]]
        else:
            out += [lr * hr - li * hi + x_ref[sl, re_c], lr * hi + li * hr + x_ref[sl, im_c]]
    return tuple(out)


def _scan_loop(step, init):
    def trip(o, carry):
        for j in range(S5_UNROLL):
            carry = step(o * S5_UNROLL + j, carry)
        return carry

    return lax.fori_loop(0, S5_TILE_STEPS // S5_UNROLL, trip, init)


def _s5_store(ref, sl, parts):
    half = S5_QCOLS // 2
    for k, v in enumerate(parts):
        ref[sl, k * half:(k + 1) * half] = v


def _s5_fwd_pass(u_perm, mats, h0, with_output):
    S = u_perm.shape[0]
    rows = S5_TILE_STEPS * S5_SEG
    nt = S // rows

    def body(*refs):
        if with_output:
            u_ref, b_ref, lr_ref, li_ref, h0_ref, c_ref, d_ref, y_ref, hinit_ref, hend_ref, xs, hcar = refs
        else:
            u_ref, b_ref, lr_ref, li_ref, h0_ref, hend_ref, xs, hcar = refs

        @pl.when(pl.program_id(0) == 0)
        def _():
            hcar[...] = h0_ref[...]

        if with_output:
            hinit_ref[...] = hcar[...]
        u = u_ref[...]
        ub = u.astype(BF16)
        for q in range(S5_QUADS):
            xs[:, q * S5_QCOLS:(q + 1) * S5_QCOLS] = jnp.dot(ub[:, q * 128:(q + 1) * 128], b_ref[q], preferred_element_type=F32)

        def step(t, h):
            sl = pl.ds(pl.multiple_of(t * S5_SEG, S5_SEG), S5_SEG)
            hn = _s5_advance(h, lr_ref, li_ref, xs, sl, False)
            _s5_store(xs, sl, hn)
            return hn

        h = _scan_loop(step, _s5_parts(hcar[...]))
        _s5_store(hcar, slice(None), h)
        _s5_store(hend_ref, slice(None), h)
        if with_output:
            ys = [jnp.dot(xs[:, q * S5_QCOLS:(q + 1) * S5_QCOLS].astype(BF16), c_ref[q], preferred_element_type=F32)
                  for q in range(S5_QUADS)]
            y_ref[...] = jnp.concatenate(ys, axis=1) + d_ref[...] * u

    full = lambda a: pl.BlockSpec(a.shape, functools.partial(lambda i, nd: (0,) * nd, nd=a.ndim))
    ins = [u_perm, mats["b_q"], mats["lam_r"], mats["lam_i"], h0]
    in_specs = [pl.BlockSpec((rows, S5_WIDTH), lambda i: (i, 0))] + [full(a) for a in ins[1:]]
    out_specs = [pl.BlockSpec((S5_SEG, S5_COLS), lambda i: (0, 0))]
    out_shape = [jax.ShapeDtypeStruct((S5_SEG, S5_COLS), F32)]
    if with_output:
        ins += [mats["c_q"], mats["d_row"]]
        in_specs += [full(mats["c_q"]), full(mats["d_row"])]
        out_specs = [pl.BlockSpec((rows, S5_WIDTH), lambda i: (i, 0)),
                     pl.BlockSpec((None, S5_SEG, S5_COLS), lambda i: (i, 0, 0))] + out_specs
        out_shape = [jax.ShapeDtypeStruct((S, S5_WIDTH), F32), jax.ShapeDtypeStruct((nt, S5_SEG, S5_COLS), F32)] + out_shape
    return _call(
        body, name="s5_fwd_y" if with_output else "s5_fwd_ends", grid=(nt,), in_specs=in_specs, out_specs=out_specs,
        out_shape=out_shape,
        scratch_shapes=[pltpu.VMEM((rows, S5_COLS), F32), pltpu.VMEM((S5_SEG, S5_COLS), F32)],
        compiler_params=_params(40),
    )(*ins)


def _s5_bwd_ends(dy_perm, mats):
    S = dy_perm.shape[0]
    rows = S5_TILE_STEPS * S5_SEG
    nt = S // rows

    def body(dy_ref, c_ref, lr_ref, li_ref, gend_ref, gs, gcar):
        @pl.when(pl.program_id(0) == 0)
        def _():
            gcar[...] = jnp.zeros(gcar.shape, F32)

        dyb = dy_ref[...].astype(BF16)
        for q in range(S5_QUADS):
            gs[:, q * S5_QCOLS:(q + 1) * S5_QCOLS] = lax.dot_general(
                dyb[:, q * 128:(q + 1) * 128], c_ref[q], (((1,), (1,)), ((), ())), preferred_element_type=F32)

        def step(k, g):
            t = S5_TILE_STEPS - 1 - k
            sl = pl.ds(pl.multiple_of(t * S5_SEG, S5_SEG), S5_SEG)
            return _s5_advance(g, lr_ref, li_ref, gs, sl, True)

        g = _scan_loop(step, _s5_parts(gcar[...]))
        _s5_store(gcar, slice(None), g)
        _s5_store(gend_ref, slice(None), g)

    full = lambda a: pl.BlockSpec(a.shape, functools.partial(lambda i, nd: (0,) * nd, nd=a.ndim))
    return _call(
        body, name="s5_bwd_ends", grid=(nt,),
        in_specs=[pl.BlockSpec((rows, S5_WIDTH), lambda i: (nt - 1 - i, 0)), full(mats["c_q"]), full(mats["lam_r"]),
                  full(mats["lam_i"])],
        out_specs=pl.BlockSpec((S5_SEG, S5_COLS), lambda i: (0, 0)),
        out_shape=jax.ShapeDtypeStruct((S5_SEG, S5_COLS), F32),
        scratch_shapes=[pltpu.VMEM((rows, S5_COLS), F32), pltpu.VMEM((S5_SEG, S5_COLS), F32)],
        compiler_params=_params(40),
    )(dy_perm, mats["c_q"], mats["lam_r"], mats["lam_i"])


def _s5_bwd_full(u_perm, dy_perm, hinit, g0, mats):
    S = u_perm.shape[0]
    rows = S5_TILE_STEPS * S5_SEG
    nt = S // rows

    def body(u_ref, dy_ref, hinit_ref, g0_ref, b_ref, c_ref, lr_ref, li_ref, d_ref,
             du_ref, dp_ref, dq_ref, db_ref, dc_ref, dd_ref, hs, gs, gcar):
        @pl.when(pl.program_id(0) == 0)
        def _():
            gcar[...] = g0_ref[...]
            for ref in (dp_ref, dq_ref, db_ref, dc_ref, dd_ref):
                ref[...] = jnp.zeros(ref.shape, F32)

        u, dy = u_ref[...], dy_ref[...]
        ub, dyb = u.astype(BF16), dy.astype(BF16)
        hs[0:S5_SEG, :] = hinit_ref[...]
        for q in range(S5_QUADS):
            cols = slice(q * S5_QCOLS, (q + 1) * S5_QCOLS)
            hs[S5_SEG:, cols] = jnp.dot(ub[:, q * 128:(q + 1) * 128], b_ref[q], preferred_element_type=F32)
            gs[:, cols] = lax.dot_general(dyb[:, q * 128:(q + 1) * 128], c_ref[q], (((1,), (1,)), ((), ())),
                                          preferred_element_type=F32)

        def fstep(t, h):
            sl = pl.ds(pl.multiple_of((t + 1) * S5_SEG, S5_SEG), S5_SEG)
            hn = _s5_advance(h, lr_ref, li_ref, hs, sl, False)
            _s5_store(hs, sl, hn)
            return hn

        _scan_loop(fstep, _s5_parts(hinit_ref[...]))

        def bstep(k, g):
            t = S5_TILE_STEPS - 1 - k
            sl = pl.ds(pl.multiple_of(t * S5_SEG, S5_SEG), S5_SEG)
            gn = _s5_advance(g, lr_ref, li_ref, gs, sl, True)
            _s5_store(gs, sl, gn)
            return gn

        _s5_store(gcar, slice(None), _scan_loop(bstep, _s5_parts(gcar[...])))

        half = S5_QCOLS // 2
        dus = []
        for q in range(S5_QUADS):
            cols = slice(q * S5_QCOLS, (q + 1) * S5_QCOLS)

            def astep(t, carry, q=q):
                sl = pl.ds(pl.multiple_of(t * S5_SEG, S5_SEG), S5_SEG)
                g = gs[sl, q * S5_QCOLS:(q + 1) * S5_QCOLS]
                hp = hs[sl, q * S5_QCOLS:(q + 1) * S5_QCOLS]
                hp_sw = jnp.concatenate([hp[:, half:], hp[:, :half]], axis=1)
                return carry[0] + g * hp, carry[1] + g * hp_sw

            zero = jnp.zeros((S5_SEG, S5_QCOLS), F32)
            acc_p, acc_q = _scan_loop(astep, (zero, zero))
            dp_ref[:, cols] += jnp.sum(acc_p, axis=0, keepdims=True)
            dq_ref[:, cols] += jnp.sum(acc_q, axis=0, keepdims=True)
            gq = gs[:, cols].astype(BF16)
            db_ref[q] += lax.dot_general(ub[:, q * 128:(q + 1) * 128], gq, (((0,), (0,)), ((), ())),
                                         preferred_element_type=F32)
            hq = hs[S5_SEG:, cols].astype(BF16)
            dc_ref[q] += lax.dot_general(dyb[:, q * 128:(q + 1) * 128], hq, (((0,), (0,)), ((), ())),
                                         preferred_element_type=F32)
            dus.append(lax.dot_general(gq, b_ref[q], (((1,), (1,)), ((), ())), preferred_element_type=F32))
        du_ref[...] = (jnp.concatenate(dus, axis=1) + d_ref[...] * dy).astype(du_ref.dtype)
        dd_ref[...] += jnp.sum(dy * u, axis=0, keepdims=True)

    full = lambda a: pl.BlockSpec(a.shape, functools.partial(lambda i, nd: (0,) * nd, nd=a.ndim))
    rev_rows = pl.BlockSpec((rows, S5_WIDTH), lambda i: (nt - 1 - i, 0))
    consts = [mats["b_q"], mats["c_q"], mats["lam_r"], mats["lam_i"], mats["d_row"]]
    acc = lambda s: pl.BlockSpec(s, functools.partial(lambda i, nd: (0,) * nd, nd=len(s)))
    acc_shapes = [(1, S5_COLS), (1, S5_COLS), (S5_QUADS, 128, S5_QCOLS), (S5_QUADS, 128, S5_QCOLS), (1, S5_WIDTH)]
    return _call(
        body, name="s5_bwd_full", grid=(nt,),
        in_specs=[rev_rows, rev_rows, pl.BlockSpec((None, S5_SEG, S5_COLS), lambda i: (nt - 1 - i, 0, 0)), full(g0)]
        + [full(a) for a in consts],
        out_specs=[rev_rows] + [acc(s) for s in acc_shapes],
        out_shape=[jax.ShapeDtypeStruct((S, S5_WIDTH), BF16)] + [jax.ShapeDtypeStruct(s, F32) for s in acc_shapes],
        scratch_shapes=[pltpu.VMEM((rows + S5_SEG, S5_COLS), F32), pltpu.VMEM((rows, S5_COLS), F32),
                        pltpu.VMEM((S5_SEG, S5_COLS), F32)],
        compiler_params=_params(56),
    )(u_perm, dy_perm, hinit, g0, *consts)


def _cmul(ar, ai, br, bi):
    return ar * br - ai * bi, ar * bi + ai * br


def _split_cols(v):
    t = v.reshape(v.shape[0], S5_QUADS, 2, S5_QCOLS // 2)
    return t[:, :, 0], t[:, :, 1]


def _join_cols(re, im):
    return jnp.stack([re, im], axis=2).reshape(re.shape[0], S5_COLS)


def _segment_starts(ends, pow_r, pow_i, reverse):
    er, ei = _split_cols(ends)
    pi = -pow_i if reverse else pow_i
    order = list(range(S5_SEG))
    if reverse:
        order = order[::-1]
    cr, ci = jnp.zeros_like(er[0]), jnp.zeros_like(ei[0])
    out_r, out_i = [None] * S5_SEG, [None] * S5_SEG
    for j in order:
        out_r[j], out_i[j] = cr, ci
        mr, mi = _cmul(pow_r, pi, cr, ci)
        cr, ci = mr + er[j], mi + ei[j]
    return _join_cols(jnp.stack(out_r), jnp.stack(out_i))


def _to_segments(a):
    S, w = a.shape
    return a.reshape(S5_SEG, S // S5_SEG, w).transpose(1, 0, 2).reshape(S, w)


def _from_segments(a):
    S, w = a.shape
    return a.reshape(S // S5_SEG, S5_SEG, w).transpose(1, 0, 2).reshape(S, w)


def _my_pos():
    return lax.axis_index("x"), lax.axis_index("y"), lax.axis_index("c")


def _flip(pos, k):
    x, y, c = pos
    return (1 - x if k & 4 else x, 1 - y if k & 2 else y, 1 - c if k & 1 else c)


def _index_of(pos):
    return 4 * pos[0] + 2 * pos[1] + pos[2]


_GATHER_FLIPS = (0, 1, 4, 5, 2, 3, 6, 7)


def _inproj_gather(x, norm_g, pack_a, pack_b, pack_c):
    S = x.shape[0]
    tm = min(S, 1024)
    n_i = S // tm
    order = jnp.stack([_index_of(_flip(_my_pos(), k)) for k in _GATHER_FLIPS]).astype(jnp.int32)

    def body(order_ref, x_ref, g_ref, pa_ref, pb_ref, pc_ref, u_ref, proj_ref, oa_ref, ob_ref, oc_ref,
             wv, u_scr, send_sems, recv_sems, local_sems):
        s, i = pl.program_id(0), pl.program_id(1)
        me = _my_pos()
        mine = _index_of(me)
        sibling = _flip(me, 1)
        srcs = (pb_ref, pa_ref, pc_ref)
        dsts = (wv, oa_ref, oc_ref)

        def direct(a, k):
            return pltpu.make_async_remote_copy(
                src_ref=srcs[a], dst_ref=dsts[a].at[mine], send_sem=send_sems.at[a * 8 + k],
                recv_sem=recv_sems.at[a * 8 + k], device_id=_flip(me, k), device_id_type=MESH)

        def passed_on(a, k):
            slot = _index_of(_flip(me, k))
            return pltpu.make_async_remote_copy(
                src_ref=dsts[a].at[slot], dst_ref=dsts[a].at[slot], send_sem=send_sems.at[a * 8 + (k | 1)],
                recv_sem=recv_sems.at[a * 8 + (k | 1)], device_id=sibling, device_id_type=MESH)

        def arrival(a, k):
            slot = _index_of(_flip(me, k))
            pltpu.make_async_remote_copy(
                src_ref=dsts[a].at[slot], dst_ref=dsts[a].at[slot], send_sem=send_sems.at[a * 8 + k],
                recv_sem=recv_sems.at[a * 8 + k], device_id=me, device_id_type=MESH).wait_recv()

        def own_copy(a):
            return pltpu.make_async_copy(srcs[a], dsts[a].at[mine], local_sems.at[a])

        def keep(idx):
            slot = _index_of(_flip(me, _GATHER_FLIPS[idx]))
            return pltpu.make_async_copy(wv.at[slot], ob_ref.at[slot], local_sems.at[3 + idx])

        keep_u = pltpu.make_async_copy(u_scr, u_ref, local_sems.at[3 + N_DEV])

        first = (s == 0) & (i == 0)

        @pl.when(first)
        def _():
            for a in range(3):
                own_copy(a).start()
            for k in (1, 4, 2):
                direct(0, k).start()
            own_copy(0).wait()
            keep(0).start()

        for idx, k in enumerate(_GATHER_FLIPS):
            if idx == 0:
                continue

            @pl.when((s == idx) & (i == 0))
            def _(idx=idx, k=k):
                arrival(0, k)
                if k in (4, 2, 6):
                    passed_on(0, k).start()
                keep(idx).start()
                if idx == 1:
                    keep_u.start()
                    direct(0, 6).start()
                if idx == 2:
                    for a in (1, 2):
                        for k in (1, 4, 2, 6):
                            direct(a, k).start()

        @pl.when(s == 0)
        def _():
            y, _, _ = _rms_fwd(x_ref[...], g_ref[...])
            u_scr[pl.ds(pl.multiple_of(i * tm, tm), tm), :] = y.astype(BF16)

        ub = u_scr[pl.ds(pl.multiple_of(i * tm, tm), tm), :]
        proj_ref[...] = jnp.dot(ub, wv[order_ref[s]], preferred_element_type=F32)

        @pl.when((s == N_DEV - 1) & (i == n_i - 1))
        def _():
            for a in (1, 2):
                for k in (4, 2, 6):
                    arrival(a, k)
                    passed_on(a, k).start()
            for a in (1, 2):
                for k in (1, 5, 3, 7):
                    arrival(a, k)
                own_copy(a).wait()
            for a in range(3):
                for k in (1, 4, 2, 6):
                    direct(a, k).wait_send()
                for k in (4, 2, 6):
                    passed_on(a, k).wait_send()
            for idx in range(N_DEV):
                keep(idx).wait()
            keep_u.wait()

    any_spec = pl.BlockSpec(memory_space=pl.ANY)
    vmem = pl.BlockSpec(memory_space=pltpu.VMEM)
    grid_spec = pltpu.PrefetchScalarGridSpec(
        num_scalar_prefetch=1, grid=(N_DEV, n_i),
        in_specs=[pl.BlockSpec((tm, D_MODEL), lambda s, i, o: (jnp.where(s == 0, i, 0), 0)),
                  pl.BlockSpec((1, D_MODEL), lambda s, i, o: (0, 0)), any_spec, vmem, any_spec],
        out_specs=[any_spec, pl.BlockSpec((tm, SHARD_IN), lambda s, i, o: (i, o[s])), any_spec, any_spec, any_spec],
        scratch_shapes=[pltpu.VMEM((N_DEV,) + pack_b.shape, BF16), pltpu.VMEM((S, D_MODEL), BF16),
                        pltpu.SemaphoreType.DMA((24,)), pltpu.SemaphoreType.DMA((24,)), pltpu.SemaphoreType.DMA((4 + N_DEV,))],
    )
    return _call(
        body, name="inproj_gather", grid_spec=grid_spec,
        out_shape=[jax.ShapeDtypeStruct((S, D_MODEL), BF16), jax.ShapeDtypeStruct((S, IN_COLS), F32),
                   jax.ShapeDtypeStruct((N_DEV,) + pack_a.shape, BF16), jax.ShapeDtypeStruct((N_DEV,) + pack_b.shape, BF16),
                   jax.ShapeDtypeStruct((N_DEV,) + pack_c.shape, BF16)],
        compiler_params=_params(56, 2),
    )(order, x, norm_g, pack_a, pack_b, pack_c)


_SCATTER_FLIPS = (7, 6, 5, 4, 3, 2, 1, 0)
_N_CHIPS = 4


def _for_row_chunks(n_rows, chunk, fn):
    def step(c, carry):
        fn(pl.ds(pl.multiple_of(c * chunk, chunk), chunk))
        return carry

    lax.fori_loop(0, n_rows // chunk, step, 0)


def _grad_w_in_scatter(dproj, u, rs_a, rs_c, small_partial):
    S = u.shape[0]
    tm = min(S, 1024)
    n_i = S // tm
    order = jnp.stack([_index_of(_flip(_my_pos(), k)) for k in _SCATTER_FLIPS]).astype(jnp.int32)
    shapes = ((D_MODEL, SHARD_IN), rs_a.shape[1:], rs_c.shape[1:])
    row_chunk = 128

    def body(order_ref, dp_ref, u_ref, ra_ref, rc_ref, p_ref, gb_ref, ga_ref, gc_ref, gs_ref, acc, sib_b, d2d_b,
             send_b, ici_b, mine_a, sib_a, ici_a, mine_c, sib_c, ici_c, gath, send_sems, recv_sems, local_sems):
        s, i = pl.program_id(0), pl.program_id(1)
        me = _my_pos()
        mine_idx = _index_of(me)
        sibling = _flip(me, 1)

        def small_to(k):
            return pltpu.make_async_remote_copy(
                src_ref=p_ref, dst_ref=gath.at[mine_idx], send_sem=send_sems.at[21 + k - 1],
                recv_sem=recv_sems.at[21 + k - 1], device_id=_flip(me, k), device_id_type=MESH)

        def small_from(k):
            pltpu.make_async_remote_copy(
                src_ref=p_ref, dst_ref=gath.at[_index_of(_flip(me, k))], send_sem=send_sems.at[21 + k - 1],
                recv_sem=recv_sems.at[21 + k - 1], device_id=me, device_id_type=MESH).wait_recv()
        sib = (sib_b, sib_a, sib_c)
        ici = (ici_b, ici_a, ici_c)
        outs = (gb_ref, ga_ref, gc_ref)

        def to_sibling(arr, m, src):
            return pltpu.make_async_remote_copy(
                src_ref=src, dst_ref=sib[arr].at[m], send_sem=send_sems.at[arr * 7 + m],
                recv_sem=recv_sems.at[arr * 7 + m], device_id=sibling, device_id_type=MESH)

        def over_ici(arr, m, src):
            return pltpu.make_async_remote_copy(
                src_ref=src, dst_ref=ici[arr].at[m], send_sem=send_sems.at[arr * 7 + 4 + m],
                recv_sem=recv_sems.at[arr * 7 + 4 + m], device_id=_flip(me, 6 - 2 * m), device_id_type=MESH)

        def from_sibling(arr, m):
            to_sibling(arr, m, sib[arr].at[m]).wait_recv()

        def from_ici(arr, m):
            over_ici(arr, m, ici[arr].at[m]).wait_recv()

        small = ((1, ra_ref, mine_a), (2, rc_ref, mine_c))

        def local_copy(arr, src, mine, m):
            return pltpu.make_async_copy(src.at[_index_of(_flip(me, 6 - 2 * m))], mine.at[m],
                                         local_sems.at[(arr - 1) * _N_CHIPS + m])

        @pl.when((s == 0) & (i == 0))
        def _():
            gath[mine_idx] = p_ref[...]
            for k in range(1, N_DEV):
                small_to(k).start()
            for arr, src, mine in small:
                for m in range(_N_CHIPS):
                    to_sibling(arr, m, src.at[_index_of(_flip(me, 7 - 2 * m))]).start()
                    local_copy(arr, src, mine, m).start()

        @pl.when((s == 1) & (i == 0))
        def _():
            for arr, src, mine in small:
                rows, chunk = shapes[arr][0], 16
                for m in range(_N_CHIPS):
                    local_copy(arr, src, mine, m).wait()
                    from_sibling(arr, m)
                    if m < _N_CHIPS - 1:
                        def add(sl, arr=arr, mine=mine, m=m):
                            mine[m, sl, :] = (mine[m, sl, :].astype(F32) + sib[arr][m, sl, :].astype(F32)).astype(BF16)

                        _for_row_chunks(rows, chunk, add)
                        over_ici(arr, m, mine.at[m]).start()
                    else:
                        def keep(sl, arr=arr, mine=mine, m=m):
                            outs[arr][sl, :] = mine[m, sl, :].astype(F32) + sib[arr][m, sl, :].astype(F32)

                        _for_row_chunks(rows, chunk, keep)

        @pl.when(i == 0)
        def _():
            acc[...] = jnp.zeros(acc.shape, F32)

        acc[...] += lax.dot_general(dp_ref[...], u_ref[...], (((0,), (0,)), ((), ())), preferred_element_type=F32)

        def block_rows(c):
            return acc[:, c * row_chunk:(c + 1) * row_chunk].T

        for m in range(_N_CHIPS):
            @pl.when((s == 2 * m) & (i == n_i - 1))
            def _(m=m):
                if m > 0:
                    to_sibling(0, m - 1, d2d_b).wait_send()
                for c in range(D_MODEL // row_chunk):
                    d2d_b[c * row_chunk:(c + 1) * row_chunk, :] = block_rows(c).astype(BF16)
                to_sibling(0, m, d2d_b).start()

            @pl.when((s == 2 * m + 1) & (i == n_i - 1))
            def _(m=m):
                from_sibling(0, m)
                slot = m % 2
                if m == 2:
                    over_ici(0, 0, send_b.at[0]).wait_send()
                for c in range(D_MODEL // row_chunk):
                    rows = slice(c * row_chunk, (c + 1) * row_chunk)
                    total = block_rows(c) + sib_b[m, rows, :].astype(F32)
                    if m < _N_CHIPS - 1:
                        send_b[slot, rows, :] = total.astype(BF16)
                    else:
                        gb_ref[rows, :] = total
                if m < _N_CHIPS - 1:
                    over_ici(0, m, send_b.at[slot]).start()

        @pl.when((s == N_DEV - 1) & (i == n_i - 1))
        def _():
            for arr in range(3):
                for m in range(_N_CHIPS - 1):
                    from_ici(arr, m)
                rows = shapes[arr][0]

                def add(sl, arr=arr):
                    outs[arr][sl, :] = (outs[arr][sl, :] + ici[arr][0, sl, :].astype(F32)
                                        + ici[arr][1, sl, :].astype(F32) + ici[arr][2, sl, :].astype(F32))

                _for_row_chunks(rows, 16, add)
            for k in range(1, N_DEV):
                small_from(k)
            total = gath[0]
            for dev in range(1, N_DEV):
                total = total + gath[dev]
            gs_ref[...] = total
            for k in range(1, N_DEV):
                small_to(k).wait_send()
            to_sibling(0, _N_CHIPS - 1, d2d_b).wait_send()
            over_ici(0, 1, send_b.at[1]).wait_send()
            over_ici(0, 2, send_b.at[0]).wait_send()
            for arr, src, mine in small:
                for m in range(_N_CHIPS):
                    to_sibling(arr, m, src.at[0]).wait_send()
                for m in range(_N_CHIPS - 1):
                    over_ici(arr, m, mine.at[m]).wait_send()

    any_spec = pl.BlockSpec(memory_space=pl.ANY)
    vmem = pl.BlockSpec(memory_space=pltpu.VMEM)
    half = lambda shp, n: pltpu.VMEM((n,) + tuple(shp), BF16)
    grid_spec = pltpu.PrefetchScalarGridSpec(
        num_scalar_prefetch=1, grid=(N_DEV, n_i),
        in_specs=[pl.BlockSpec((tm, SHARD_IN), lambda s, i, o: (i, o[s])),
                  pl.BlockSpec((tm, D_MODEL), lambda s, i, o: (i, 0)), any_spec, any_spec, vmem],
        out_specs=[vmem, vmem, vmem, vmem],
        scratch_shapes=[
            pltpu.VMEM((SHARD_IN, D_MODEL), F32), half(shapes[0], _N_CHIPS), pltpu.VMEM(shapes[0], BF16),
            half(shapes[0], 2), half(shapes[0], _N_CHIPS - 1),
            half(shapes[1], _N_CHIPS), half(shapes[1], _N_CHIPS), half(shapes[1], _N_CHIPS - 1),
            half(shapes[2], _N_CHIPS), half(shapes[2], _N_CHIPS), half(shapes[2], _N_CHIPS - 1),
            pltpu.VMEM((N_DEV,) + small_partial.shape, F32),
            pltpu.SemaphoreType.DMA((28,)), pltpu.SemaphoreType.DMA((28,)), pltpu.SemaphoreType.DMA((2 * _N_CHIPS,))],
    )
    return _call(
        body, name="grad_w_in_scatter", grid_spec=grid_spec,
        out_shape=[jax.ShapeDtypeStruct(shp, F32) for shp in shapes] + [jax.ShapeDtypeStruct(small_partial.shape, F32)],
        compiler_params=_params(60, 2),
    )(order, dproj, u, rs_a, rs_c, small_partial)


def _adam_update(g, w, m, v):
    m2 = ADAM_B1 * m + (1.0 - ADAM_B1) * g
    v2 = ADAM_B2 * v + (1.0 - ADAM_B2) * (g * g)
    m_hat = m2 / (1.0 - ADAM_B1 ** ADAM_STEP)
    v_hat = v2 / (1.0 - ADAM_B2 ** ADAM_STEP)
    delta = -ADAM_LR * (m_hat / (jnp.sqrt(v_hat) + ADAM_EPS) + ADAM_WD * w)
    return delta, m2, v2


def _adam_rows(g, w, m, v):
    rows, cols = w.shape
    tm = rows if rows % 256 else 256

    def fn(rv, cr, out):
        return list(_adam_update(*rv)), []

    outs, _ = _rowwise("adamw", fn, rows, tm, [(a, cols, 0) for a in (g, w, m, v)], [], [(cols, F32)] * 3, [], 32)
    return outs


_SMALL = ["norm_g", "hg_lb", "hg_norm_g", "s5_a_re", "s5_a_im", "s5_log_dt", "s5_b_re", "s5_b_im", "s5_c_re",
          "s5_c_im", "s5_d", "b_glu", "ple_norm_g", "final_norm_g"]
_BIG = ["w_in", "w_o_hg", "w_glu", "w_o_s5", "w_out", "w_ple", "w_ple_gate"]
_ORDER = ["norm_g", "w_in", "hg_lb", "hg_norm_g", "w_o_hg", "s5_a_re", "s5_a_im", "s5_log_dt", "s5_b_re", "s5_b_im",
          "s5_c_re", "s5_c_im", "s5_d", "w_glu", "b_glu", "w_o_s5", "w_out", "ple_norm_g", "w_ple", "w_ple_gate",
          "final_norm_g"]


def _pack_small(vals, tail=None):
    parts = []
    for name in _SMALL:
        flat = vals[name].reshape(-1).astype(F32)
        pad = (-flat.shape[0]) % 1024
        parts.append(jnp.pad(flat, (0, pad)))
    tail = jnp.zeros((0,), F32) if tail is None else tail.reshape(-1).astype(F32)
    parts.append(jnp.pad(tail, (0, 1024 - tail.shape[0])))
    return jnp.concatenate(parts).reshape(-1, 128)


def _unpack_small(packed, like):
    flat = packed.reshape(-1)
    out, off = {}, 0
    for name in _SMALL:
        size = like[name].size
        out[name] = flat[off:off + size].reshape(like[name].shape)
        off += size + (-size) % 1024
    return out


def _col_blocks(full):
    k = full.shape[0]
    return full.reshape(k, N_DEV, 128).transpose(1, 0, 2)


def _from_col_blocks(blocks):
    k = blocks.shape[1]
    return blocks.transpose(1, 0, 2).reshape(k, N_DEV * 128)


def kernel(x, p, norm_g, w_in, hg_lb, hg_norm_g, w_o_hg, s5_a_re, s5_a_im, s5_log_dt, s5_b_re, s5_b_im, s5_c_re, s5_c_im, s5_d, w_glu, b_glu, w_o_s5, w_out, ple_norm_g, w_ple, w_ple_gate, final_norm_g, loss_target, m_norm_g, m_w_in, m_hg_lb, m_hg_norm_g, m_w_o_hg, m_s5_a_re, m_s5_a_im, m_s5_log_dt, m_s5_b_re, m_s5_b_im, m_s5_c_re, m_s5_c_im, m_s5_d, m_w_glu, m_b_glu, m_w_o_s5, m_w_out, m_ple_norm_g, m_w_ple, m_w_ple_gate, m_final_norm_g, v_norm_g, v_w_in, v_hg_lb, v_hg_norm_g, v_w_o_hg, v_s5_a_re, v_s5_a_im, v_s5_log_dt, v_s5_b_re, v_s5_b_im, v_s5_c_re, v_s5_c_im, v_s5_d, v_w_glu, v_b_glu, v_w_o_s5, v_w_out, v_ple_norm_g, v_w_ple, v_w_ple_gate, v_final_norm_g):
    args = dict(locals())
    w = {n: args[n] for n in _ORDER}
    m = {n: args["m_" + n] for n in _ORDER}
    v = {n: args["v_" + n] for n in _ORDER}
    xs = x[0]
    ps = p[0, 0]
    tgt = loss_target[0]
    S = xs.shape[0]

    pack_a = jnp.concatenate([w_o_hg[0], w_out[0], w_ple_gate[0]], axis=0).astype(BF16)
    pack_b = w_in[0].astype(BF16)
    pack_c = jnp.concatenate([w_glu[0], w_o_s5[0], w_ple[0]], axis=0).astype(BF16)
    u, proj, all_a, all_b, all_c = _inproj_gather(xs, norm_g, pack_a, pack_b, pack_c)
    wf_o_hg = all_a[:, 0:128].reshape(D_MODEL, D_MODEL)
    wf_out = all_a[:, 128:256].reshape(D_MODEL, D_MODEL)
    wf_pg = all_a[:, 256:384].reshape(D_MODEL, D_MODEL)
    wf_glu = _from_col_blocks(all_c[:, 0:512])
    wf_o_s5 = _from_col_blocks(all_c[:, 512:1024])
    wf_ple = _from_col_blocks(all_c[:, 1024:1280])

    lb = jax.nn.sigmoid(hg_lb[0:1] - hg_lb[1:2])
    s5_names = ["s5_a_re", "s5_a_im", "s5_log_dt", "s5_b_re", "s5_b_im", "s5_c_re", "s5_c_im", "s5_d"]
    build = lambda *a: _s5_matrices(*a, seg_len=S // S5_SEG)
    mats_f32, mats_vjp = jax.vjp(build, *[w[n][0] for n in s5_names])
    mats = dict(mats_f32, b_q=mats_f32["b_q"].astype(BF16), c_q=mats_f32["c_q"].astype(BF16))
    bias_glu = b_glu

    o, states = _hgrn_fwd(proj, lb)
    u_perm = _to_segments(proj[:, COL_US:COL_US + S5_WIDTH])
    zeros_state = jnp.zeros((S5_SEG, S5_COLS), F32)
    (h_ends,) = _s5_fwd_pass(u_perm, mats, zeros_state, False)
    h0 = _segment_starts(h_ends, mats["pow_r"], mats["pow_i"], False)
    y_perm, h_init, _ = _s5_fwd_pass(u_perm, mats, h0, True)
    ys = _from_segments(y_perm)
    y_hg, y_s5, glu, h1 = _stage_branches(o, proj, ys, xs, hg_norm_g, wf_o_hg, wf_glu, bias_glu, wf_o_s5, wf_out)

    dh1, (loss_acc, d_final_g, d_ple_g, d_w_ple, d_w_pg) = _stage_ple_loss(
        h1, ps, tgt, ple_norm_g, wf_ple, wf_pg, final_norm_g.reshape(1, D_MODEL))
    (d_gate_hg, d_gate_s5, d_yhg, d_ys5), (d_w_out,) = _stage_bwd_merge(dh1, y_hg, y_s5, proj, wf_out)
    (d_o, d_g_hg), (d_w_o_hg, d_hg_norm) = _stage_bwd_hg_path(d_yhg, o, proj, hg_norm_g, wf_o_hg)
    (d_ys, d_z), (d_w_o_s5, d_w_glu, d_b_glu) = _stage_bwd_s5_path(d_ys5, ys, glu, proj, wf_o_s5, wf_glu)
    dq, df, div, d_lb = _hgrn_bwd(proj, lb, d_o, states)
    dy_perm = _to_segments(d_ys)
    g_ends = _s5_bwd_ends(dy_perm, mats)
    g0 = _segment_starts(g_ends, mats["pow_r"], mats["pow_i"], True)
    du_perm, acc_p, acc_q, d_bq, d_cq_t, d_d = _s5_bwd_full(u_perm, dy_perm, h_init, g0, mats)
    d_us = _from_segments(du_perm)
    grad_x, dproj, d_norm_g = _stage_inproj_bwd([dq, df, div, d_g_hg, d_us, d_z, d_gate_hg, d_gate_s5], xs, dh1,
                                                norm_g, all_b)

    p_re, p_im = _split_cols(acc_p)
    q_re, q_im = _split_cols(acc_q)
    d_lam_r = (p_re + p_im)[0]
    d_lam_i = (q_im - q_re)[0]
    zero_row = jnp.zeros((S5_SEG, S5_COLS), F32)
    row_of = lambda re_part, im_part: zero_row.at[0].set(_join_cols(re_part[None], im_part[None])[0])
    zeros_q = jnp.zeros_like(d_lam_r)
    cot = dict(
        lam_r=row_of(d_lam_r, zeros_q), lam_i=row_of(zeros_q, d_lam_i),
        b_q=d_bq, c_q=d_cq_t.transpose(0, 2, 1), d_row=d_d,
        pow_r=jnp.zeros_like(mats["pow_r"]), pow_i=jnp.zeros_like(mats["pow_i"]),
    )
    d_s5 = mats_vjp(cot)

    s_lb = lb * (1.0 - lb)
    d_hg_lb = jnp.concatenate([d_lb * s_lb, -d_lb * s_lb], axis=0)
    small_g = dict(norm_g=d_norm_g, hg_lb=d_hg_lb, hg_norm_g=d_hg_norm, b_glu=d_b_glu, ple_norm_g=d_ple_g,
                   final_norm_g=d_final_g)
    for name, g in zip(s5_names, d_s5):
        small_g[name] = g
    pk = lambda d: _pack_small({n: d[n] for n in _SMALL})
    rs_a = jnp.concatenate([d_w_o_hg.reshape(N_DEV, 128, D_MODEL), d_w_out.reshape(N_DEV, 128, D_MODEL),
                            d_w_pg.reshape(N_DEV, 128, D_MODEL)], axis=1).astype(BF16)
    rs_c = jnp.concatenate([_col_blocks(d_w_glu), _col_blocks(d_w_o_s5), _col_blocks(d_w_ple)], axis=1).astype(BF16)
    partial = _pack_small({n: small_g[n] for n in _SMALL}, tail=loss_acc[0, 0:1])
    g_b, g_a, g_c, sg = _grad_w_in_scatter(dproj, u, rs_a, rs_c, partial)
    sd, sm, sv = _adam_rows(sg, pk(w), pk(m), pk(v))
    like = {n: w[n] for n in _SMALL}
    out_g, out_d, out_m, out_v = (_unpack_small(t, like) for t in (sg, sd, sm, sv))
    big_g = dict(w_o_hg=g_a[0:128], w_out=g_a[128:256], w_ple_gate=g_a[256:384], w_in=g_b,
                 w_glu=g_c[0:512], w_o_s5=g_c[512:1024], w_ple=g_c[1024:1280])
    for name in _BIG:
        shape = w[name].shape
        g2 = big_g[name]
        d2, m2, v2 = _adam_rows(g2, w[name][0], m[name][0], v[name][0])
        out_g[name], out_d[name], out_m[name], out_v[name] = (t.reshape(shape) for t in (g2, d2, m2, v2))

    loss = sg[sg.shape[0] - 8, 0]
    return (loss, grad_x[None], *[out_g[n] for n in _ORDER], *[out_d[n] for n in _ORDER],
            *[out_m[n] for n in _ORDER], *[out_v[n] for n in _ORDER])
```

```python
import functools
import math

import jax
import jax.numpy as jnp
from jax import lax
from jax.experimental import pallas as pl
from jax.experimental.pallas import tpu as pltpu

F32 = jnp.float32
BF16 = jnp.bfloat16

D_MODEL = 1024
N_DEV = 8
IN_COLS = 7168
SHARD_IN = IN_COLS // N_DEV
HG_HEADS = 8
HG_DIM = 128
HG_CHUNK = 64
HG_SUPER_FWD = 256
HG_SUPER_BWD = 128
HG_HEADS_PER_STEP = 8
S5_WIDTH = 512
S5_GROUPS = 32
S5_STATE = 64
S5_CH = 16
S5_SEG = 8
S5_QUADS = 4
S5_QCOLS = 1024
S5_COLS = S5_QUADS * S5_QCOLS
S5_TILE_STEPS = 64
S5_UNROLL = 8
NORM_EPS = 1e-6
ADAM_LR = 0.001
ADAM_B1 = 0.9
ADAM_B2 = 0.999
ADAM_EPS = 1e-08
ADAM_WD = 0.01
ADAM_STEP = 10
MIB = 1024 * 1024
MESH = pl.DeviceIdType.MESH

COL_Q, COL_F, COL_I, COL_G, COL_US, COL_ZS, COL_GH, COL_GS = 0, 1024, 2048, 3072, 4096, 4608, 5120, 6144


def _call(body, **kw):
    return pl.pallas_call(body, **kw)


def _params(vmem_mb, n_grid=1):
    return pltpu.CompilerParams(
        dimension_semantics=("arbitrary",) * n_grid, vmem_limit_bytes=vmem_mb * MIB
    )


def _bdot(a, b):
    return jnp.dot(a.astype(BF16), b.astype(BF16), preferred_element_type=F32)


def _bdot_nt(a, b):
    return lax.dot_general(a.astype(BF16), b.astype(BF16), (((1,), (1,)), ((), ())), preferred_element_type=F32)


def _bdot_tn(a, b):
    return lax.dot_general(a.astype(BF16), b.astype(BF16), (((0,), (0,)), ((), ())), preferred_element_type=F32)


def _sigmoid(x):
    return jax.nn.sigmoid(x)


def _silu(x):
    return x * _sigmoid(x)


def _dsilu(x):
    s = _sigmoid(x)
    return s * (1.0 + x * (1.0 - s))


_GELU_C = math.sqrt(2.0 / math.pi)


def _gelu(x):
    return 0.5 * x * (1.0 + jnp.tanh(_GELU_C * (x + 0.044715 * x * x * x)))


def _dgelu(x):
    t = jnp.tanh(_GELU_C * (x + 0.044715 * x * x * x))
    return 0.5 * (1.0 + t) + 0.5 * x * (1.0 - t * t) * _GELU_C * (1.0 + 3.0 * 0.044715 * x * x)


def _rms_fwd(x, g):
    r = lax.rsqrt(jnp.mean(x * x, axis=-1, keepdims=True) + NORM_EPS)
    n = x * r
    return n * g, n, r


def _rms_bwd(dy, n, r, g):
    dn = dy * g
    dx = r * (dn - n * jnp.mean(dn * n, axis=-1, keepdims=True))
    return dx, jnp.sum(dy * n, axis=0, keepdims=True)


def _head_rms_fwd(o, g):
    ns, rs = [], []
    for h in range(HG_HEADS):
        oh = o[:, h * HG_DIM:(h + 1) * HG_DIM]
        r = lax.rsqrt(jnp.mean(oh * oh, axis=-1, keepdims=True) + NORM_EPS)
        ns.append(oh * r)
        rs.append(r)
    n = jnp.concatenate(ns, axis=1)
    return n * g, n, rs


def _head_rms_bwd(dy, n, rs, g):
    dn = dy * g
    dxs = []
    for h in range(HG_HEADS):
        sl = slice(h * HG_DIM, (h + 1) * HG_DIM)
        dxs.append(rs[h] * (dn[:, sl] - n[:, sl] * jnp.mean(dn[:, sl] * n[:, sl], axis=-1, keepdims=True)))
    return jnp.concatenate(dxs, axis=1), jnp.sum(dy * n, axis=0, keepdims=True)


def _rowwise(name, fn, n_rows, tm, rows, consts, out_rows, out_accs, vmem_mb, parts=1):
    n_r, n_c, n_or, n_oa = len(rows), len(consts), len(out_rows), len(out_accs)
    tp = tm // parts

    def body(*refs):
        r_refs = refs[:n_r]
        c_refs = refs[n_r:n_r + n_c]
        or_refs = refs[n_r + n_c:n_r + n_c + n_or]
        oa_refs = refs[n_r + n_c + n_or:]

        if n_oa:
            @pl.when(pl.program_id(0) == 0)
            def _():
                for ref in oa_refs:
                    ref[...] = jnp.zeros(ref.shape, ref.dtype)

        for part in range(parts):
            sl = slice(part * tp, (part + 1) * tp)
            outs, accs = fn([r[sl, :] for r in r_refs], c_refs, [o.at[sl, :] for o in or_refs])
            for ref, v in zip(or_refs, outs):
                if v is not None:
                    ref[sl, :] = v.astype(ref.dtype)
            for ref, v in zip(oa_refs, accs):
                ref[...] += v.astype(ref.dtype)

    in_specs = [pl.BlockSpec((tm, w), functools.partial(lambda i, c: (i, c), c=cb)) for (_, w, cb) in rows]
    in_specs += [pl.BlockSpec(c.shape, functools.partial(lambda i, nd: (0,) * nd, nd=c.ndim),
                              pipeline_mode=pl.Buffered(1)) for c in consts]
    out_specs = [pl.BlockSpec((tm, w), lambda i: (i, 0)) for (w, _) in out_rows]
    out_specs += [pl.BlockSpec(s, functools.partial(lambda i, nd: (0,) * nd, nd=len(s))) for (s, _) in out_accs]
    out_shape = [jax.ShapeDtypeStruct((n_rows, w), dt) for (w, dt) in out_rows]
    out_shape += [jax.ShapeDtypeStruct(s, dt) for (s, dt) in out_accs]
    res = _call(
        body, name=name, grid=(n_rows // tm,), in_specs=in_specs, out_specs=out_specs, out_shape=out_shape,
        compiler_params=_params(vmem_mb),
    )(*[a for (a, _, _) in rows], *consts)
    return res[:n_or], res[n_or:]


def _stage_branches(o, proj, ys, x, hg_norm_g, w_o_hg, w_glu, b_glu, w_o_s5, w_out):
    S = x.shape[0]

    def fn(rv, cr, out):
        o_b, g_hg, z_s, gate_hg, gate_s5, ys_b, x_b = rv
        gn_ref, wohg_ref, wglu_ref, bglu_ref, wos5_ref, wout_ref = cr
        on, _, _ = _head_rms_fwd(o_b, gn_ref[...])
        a = on * _silu(g_hg)
        y_hg = jnp.dot(a.astype(BF16), wohg_ref[...], preferred_element_type=F32)
        gl = _gelu(ys_b)
        glu = jnp.dot(gl.astype(BF16), wglu_ref[...], preferred_element_type=F32) + bglu_ref[...]
        ys2 = glu[:, :S5_WIDTH] * _sigmoid(glu[:, S5_WIDTH:]) * _silu(z_s)
        y_s5 = jnp.dot(ys2.astype(BF16), wos5_ref[...], preferred_element_type=F32)
        merged = _sigmoid(gate_hg) * y_hg + _sigmoid(gate_s5) * y_s5
        h1 = x_b + jnp.dot(merged.astype(BF16), wout_ref[...], preferred_element_type=F32)
        return [y_hg, y_s5, glu, h1], []

    rows = [(o, D_MODEL, 0), (proj, D_MODEL, COL_G // D_MODEL), (proj, S5_WIDTH, COL_ZS // S5_WIDTH),
            (proj, D_MODEL, COL_GH // D_MODEL), (proj, D_MODEL, COL_GS // D_MODEL), (ys, S5_WIDTH, 0), (x, D_MODEL, 0)]
    (y_hg, y_s5, glu, h1), _ = _rowwise(
        "branches", fn, S, 256, rows, [hg_norm_g, w_o_hg, w_glu, b_glu, w_o_s5, w_out],
        [(D_MODEL, F32)] * 4, [], 56)
    return y_hg, y_s5, glu, h1


def _stage_ple_loss(h1, p, target, ple_norm_g, w_ple, w_ple_gate, final_norm_g):
    S = h1.shape[0]

    def fn(rv, cr, out):
        h1_b, p_b, t_b = rv
        gp_ref, wple_ref, wpg_ref, gf_ref = cr
        n2g, n2, r2 = _rms_fwd(h1_b, gp_ref[...])
        z = jnp.dot(n2g.astype(BF16), wpg_ref[...], preferred_element_type=F32)
        gate = _sigmoid(z)
        pe = jnp.dot(p_b.astype(BF16), wple_ref[...], preferred_element_type=F32)
        h2 = h1_b + pe * gate
        y, nf, rf = _rms_fwd(h2, gf_ref[...])
        err = y - t_b
        loss_rows = 0.5 * jnp.mean(err * err, axis=-1, keepdims=True)
        loss_inc = jnp.broadcast_to(jnp.sum(loss_rows, axis=0, keepdims=True), (1, 128))
        dy = err * (1.0 / D_MODEL)
        dh2, d_gf = _rms_bwd(dy, nf, rf, gf_ref[...])
        d_pe = dh2 * gate
        dz = dh2 * pe * gate * (1.0 - gate)
        d_wple = _bdot_tn(p_b, d_pe)
        d_wpg = _bdot_tn(n2g, dz)
        dn2g = _bdot_nt(dz, wpg_ref[...])
        dh1n, d_gp = _rms_bwd(dn2g, n2, r2, gp_ref[...])
        return [dh2 + dh1n], [loss_inc, d_gf, d_gp, d_wple, d_wpg]

    (dh1,), accs = _rowwise(
        "ple_loss", fn, S, 512, [(h1, D_MODEL, 0), (p, 256, 0), (target, D_MODEL, 0)],
        [ple_norm_g, w_ple, w_ple_gate, final_norm_g], [(D_MODEL, F32)],
        [((1, 128), F32), ((1, D_MODEL), F32), ((1, D_MODEL), F32), ((256, D_MODEL), F32), ((D_MODEL, D_MODEL), F32)], 56,
        parts=2)
    return dh1, accs


def _stage_bwd_merge(dh1, y_hg, y_s5, proj, w_out):
    S = dh1.shape[0]

    def fn(rv, cr, out):
        dh1_b, yhg, ys5, gate_hg, gate_s5 = rv
        (wout_ref,) = cr
        sg_h, sg_s = _sigmoid(gate_hg), _sigmoid(gate_s5)
        merged = sg_h * yhg + sg_s * ys5
        d_wout = _bdot_tn(merged, dh1_b)
        d_merged = _bdot_nt(dh1_b, wout_ref[...])
        d_gate_hg = d_merged * yhg * sg_h * (1.0 - sg_h)
        d_gate_s5 = d_merged * ys5 * sg_s * (1.0 - sg_s)
        return [d_gate_hg, d_gate_s5, d_merged * sg_h, d_merged * sg_s], [d_wout]

    rows = [(dh1, D_MODEL, 0), (y_hg, D_MODEL, 0), (y_s5, D_MODEL, 0), (proj, D_MODEL, COL_GH // D_MODEL),
            (proj, D_MODEL, COL_GS // D_MODEL)]
    outs, accs = _rowwise("bwd_merge", fn, S, 512, rows, [w_out], [(D_MODEL, BF16)] * 4,
                          [((D_MODEL, D_MODEL), F32)], 56, parts=2)
    return outs, accs


def _stage_bwd_hg_path(d_yhg, o, proj, hg_norm_g, w_o_hg):
    S = o.shape[0]

    def fn(rv, cr, out):
        d_yhg_b, o_b, g_hg = rv
        gn_ref, wohg_ref = cr
        ong, on, rs = _head_rms_fwd(o_b, gn_ref[...])
        sil = _silu(g_hg)
        d_wohg = _bdot_tn(ong * sil, d_yhg_b)
        d_a = _bdot_nt(d_yhg_b, wohg_ref[...])
        d_g_hg = d_a * ong * _dsilu(g_hg)
        d_o, d_gn = _head_rms_bwd(d_a * sil, on, rs, gn_ref[...])
        return [d_o, d_g_hg], [d_wohg, d_gn]

    rows = [(d_yhg, D_MODEL, 0), (o, D_MODEL, 0), (proj, D_MODEL, COL_G // D_MODEL)]
    outs, accs = _rowwise("bwd_hg_path", fn, S, 512, rows, [hg_norm_g, w_o_hg], [(D_MODEL, BF16)] * 2,
                          [((D_MODEL, D_MODEL), F32), ((1, D_MODEL), F32)], 56, parts=2)
    return outs, accs


def _stage_bwd_s5_path(d_ys5, ys, glu, proj, w_o_s5, w_glu):
    S = ys.shape[0]

    def fn(rv, cr, out):
        d_ys5_b, ys_b, glu_b, z_s = rv
        wos5_ref, wglu_ref = cr
        ga, gb = glu_b[:, :S5_WIDTH], glu_b[:, S5_WIDTH:]
        sgb, silz = _sigmoid(gb), _silu(z_s)
        ys2 = ga * sgb * silz
        d_wos5 = _bdot_tn(ys2, d_ys5_b)
        d_ys2 = _bdot_nt(d_ys5_b, wos5_ref[...])
        d_ga = d_ys2 * sgb * silz
        d_gb = d_ys2 * ga * sgb * (1.0 - sgb) * silz
        d_z = d_ys2 * ga * sgb * _dsilu(z_s)
        d_glu = jnp.concatenate([d_ga, d_gb], axis=1)
        gl = _gelu(ys_b)
        d_wglu = _bdot_tn(gl, d_glu)
        d_bglu = jnp.sum(d_glu, axis=0, keepdims=True)
        d_gl = _bdot_nt(d_glu, wglu_ref[...])
        return [d_gl * _dgelu(ys_b), d_z], [d_wos5, d_wglu, d_bglu]

    rows = [(d_ys5, D_MODEL, 0), (ys, S5_WIDTH, 0), (glu, D_MODEL, 0), (proj, S5_WIDTH, COL_ZS // S5_WIDTH)]
    outs, accs = _rowwise(
        "bwd_s5_path", fn, S, 512, rows, [w_o_s5, w_glu], [(S5_WIDTH, F32), (S5_WIDTH, BF16)],
        [((S5_WIDTH, D_MODEL), F32), ((S5_WIDTH, D_MODEL), F32), ((1, D_MODEL), F32)], 48, parts=2)
    return outs, accs


def _stage_inproj_bwd(pieces, x, dh1, norm_g, w_in_all):
    S = x.shape[0]

    def fn(rv, cr, out):
        g_ref, w_ref = cr
        x_b, dh1_b = rv[8], rv[9]
        dproj_ref = out[1]
        col = 0
        for v in rv[:8]:
            dproj_ref[:, col:col + v.shape[1]] = v.astype(BF16)
            col += v.shape[1]
        d_u = jnp.zeros((x_b.shape[0], D_MODEL), F32)
        for j in range(N_DEV):
            d_u = d_u + lax.dot_general(dproj_ref[:, j * SHARD_IN:(j + 1) * SHARD_IN], w_ref[j],
                                        (((1,), (1,)), ((), ())), preferred_element_type=F32)
        _, n, r = _rms_fwd(x_b, g_ref[...])
        dx, d_g = _rms_bwd(d_u, n, r, g_ref[...])
        return [dh1_b + dx, None], [d_g]

    rows = [(a, a.shape[1], 0) for a in pieces] + [(x, D_MODEL, 0), (dh1, D_MODEL, 0)]
    (grad_x, dproj), (d_g,) = _rowwise(
        "inproj_bwd", fn, S, 256, rows, [norm_g, w_in_all], [(D_MODEL, F32), (IN_COLS, BF16)],
        [((1, D_MODEL), F32)], 56)
    return grad_x, dproj, d_g


def _chunk_row(shape):
    return lax.broadcasted_iota(jnp.int32, shape, 0) & (HG_CHUNK - 1)


def _chunk_cumsum(x):
    r_in = _chunk_row(x.shape)
    s = 1
    while s < HG_CHUNK:
        x = x + jnp.where(r_in >= s, pltpu.roll(x, s, 0), 0.0)
        s *= 2
    return x


def _chunk_suffix_sum(x):
    n = x.shape[0]
    r_in = _chunk_row(x.shape)
    s = 1
    while s < HG_CHUNK:
        x = x + jnp.where(r_in < HG_CHUNK - s, pltpu.roll(x, n - s, 0), 0.0)
        s *= 2
    return x


def _hgrn_prep(q, fl, lb):
    sup = q.shape[0]
    nc = sup // HG_CHUNK
    sig = _sigmoid(fl)
    f = lb + (1.0 - lb) * sig
    k = (1.0 - lb) * (1.0 - sig)
    b = _chunk_cumsum(jnp.log(f))
    b3 = b.reshape(nc, HG_CHUNK, HG_DIM)
    row3 = lax.broadcasted_iota(jnp.int32, b3.shape, 1)
    pick = lambda r: jnp.sum(jnp.where(row3 == r, b3, 0.0), axis=1, keepdims=True)
    b_mid = pick(HG_CHUNK // 2 - 1)
    b_last = pick(HG_CHUNK - 1)
    flat = lambda t: t.reshape(sup, HG_DIM)
    e_qa = flat(jnp.exp(b3 - b_mid))
    e_ka = flat(jnp.exp(b_mid - b3))
    e_qd = jnp.exp(b)
    e_kd = flat(jnp.exp(b_last - b3))
    dc = jnp.exp(b_last)
    return sig, f, k, e_qa, e_ka, e_qd, e_kd, dc


def _hgrn_mask(sup):
    r = lax.broadcasted_iota(jnp.int32, (sup, sup), 0)
    c = lax.broadcasted_iota(jnp.int32, (sup, sup), 1)
    shift = HG_CHUNK.bit_length() - 1
    return (jnp.right_shift(r, shift) == jnp.right_shift(c, shift)) & (r >= c)


def _hgrn_fwd(proj, lb):
    S = proj.shape[0]
    sup = HG_SUPER_FWD
    nb = S // sup
    nc = sup // HG_CHUNK
    hp = HG_HEADS_PER_STEP
    wide = hp * HG_DIM

    def body(q_ref, f_ref, iv_ref, lb_ref, o_ref, st_ref, state):
        @pl.when(pl.program_id(1) == 0)
        def _():
            state[...] = jnp.zeros(state.shape, F32)

        mask = _hgrn_mask(sup)
        for hh in range(hp):
            lanes = slice(hh * HG_DIM, (hh + 1) * HG_DIM)
            q, iv = q_ref[:, lanes], iv_ref[:, lanes]
            _, _, k, e_qa, e_ka, e_qd, e_kd, dc = _hgrn_prep(q, f_ref[:, lanes], lb_ref[:, lanes])
            scores = jnp.where(mask, _bdot_nt(q * e_qa, k * e_ka), 0.0)
            o_intra = _bdot(scores, iv)
            qd, kd = q * e_qd, k * e_kd
            for c in range(nc):
                sl = slice(c * HG_CHUNK, (c + 1) * HG_CHUNK)
                st = state[hh]
                st_ref[hh, c] = st
                o_ref[sl, lanes] = o_intra[sl] + _bdot_nt(qd[sl], st)
                state[hh] = dc[c] * st + _bdot_tn(iv[sl], kd[sl])

    blk = lambda base: pl.BlockSpec((sup, wide), functools.partial(lambda h, i, b: (i, b + h), b=base // wide))
    return _call(
        body, name="hgrn_fwd", grid=(HG_HEADS // hp, nb),
        in_specs=[blk(COL_Q), blk(COL_F), blk(COL_I), pl.BlockSpec((1, wide), lambda h, i: (0, h))],
        out_specs=[pl.BlockSpec((sup, wide), lambda h, i: (i, h)),
                   pl.BlockSpec((hp, nc, HG_DIM, HG_DIM), lambda h, i: (h, i, 0, 0))],
        out_shape=[jax.ShapeDtypeStruct((S, D_MODEL), F32),
                   jax.ShapeDtypeStruct((HG_HEADS, S // HG_CHUNK, HG_DIM, HG_DIM), F32)],
        scratch_shapes=[pltpu.VMEM((hp, HG_DIM, HG_DIM), F32)],
        compiler_params=_params(40, 2),
    )(proj, proj, proj, lb)


def _hgrn_bwd(proj, lb, d_o, states):
    S = proj.shape[0]
    sup = HG_SUPER_BWD
    nb = S // sup
    nc = sup // HG_CHUNK
    hp = HG_HEADS_PER_STEP
    wide = hp * HG_DIM

    def body(q_ref, f_ref, iv_ref, lb_ref, do_ref, st_ref, dq_ref, df_ref, div_ref, dlb_ref, dstate):
        @pl.when(pl.program_id(1) == 0)
        def _():
            dstate[...] = jnp.zeros(dstate.shape, F32)
            dlb_ref[...] = jnp.zeros(dlb_ref.shape, F32)

        mask = _hgrn_mask(sup)
        for hh in range(hp):
            lanes = slice(hh * HG_DIM, (hh + 1) * HG_DIM)
            q, iv, do, lb_v = q_ref[:, lanes], iv_ref[:, lanes], do_ref[:, lanes], lb_ref[:, lanes]
            sig, f, k, e_qa, e_ka, e_qd, e_kd, dc = _hgrn_prep(q, f_ref[:, lanes], lb_v)
            qa, ka, qd, kd = q * e_qa, k * e_ka, q * e_qd, k * e_kd
            scores = jnp.where(mask, _bdot_nt(qa, ka), 0.0)
            d_scores = jnp.where(mask, _bdot_nt(do, iv), 0.0)
            d_iv_intra = _bdot_tn(scores, do)
            d_qa = _bdot(d_scores, ka)
            d_ka = _bdot_tn(d_scores, qa)
            d_qd, d_kd, d_last = [None] * nc, [None] * nc, [None] * nc
            for c in reversed(range(nc)):
                sl = slice(c * HG_CHUNK, (c + 1) * HG_CHUNK)
                st = st_ref[hh, c]
                ds = dstate[hh]
                d_qd[c] = _bdot(do[sl], st)
                d_kd[c] = _bdot(iv[sl], ds)
                div_ref[sl, lanes] = (d_iv_intra[sl] + _bdot_nt(kd[sl], ds)).astype(div_ref.dtype)
                d_last[c] = (jnp.sum(ds * st, axis=0, keepdims=True) * dc[c]
                             + jnp.sum(d_kd[c] * kd[sl], axis=0, keepdims=True))
                dstate[hh] = dc[c] * ds + _bdot_tn(do[sl], qd[sl])
            d_qd = jnp.concatenate(d_qd, axis=0)
            d_kd = jnp.concatenate(d_kd, axis=0)
            d_b = d_qa * qa - d_ka * ka + d_qd * qd - d_kd * kd
            last_rows = jnp.concatenate([jnp.broadcast_to(t, (HG_CHUNK, HG_DIM)) for t in d_last], axis=0)
            d_b = d_b + jnp.where(_chunk_row(d_b.shape) == HG_CHUNK - 1, last_rows, 0.0)
            d_logf = _chunk_suffix_sum(d_b)
            d_k = d_ka * e_ka + d_kd * e_kd
            g_f = d_logf / f
            d_sig = (g_f - d_k) * (1.0 - lb_v)
            dq_ref[:, lanes] = (d_qa * e_qa + d_qd * e_qd).astype(dq_ref.dtype)
            df_ref[:, lanes] = (d_sig * sig * (1.0 - sig)).astype(df_ref.dtype)
            d_lb = jnp.sum((g_f - d_k) * (1.0 - sig), axis=0, keepdims=True)
            dlb_ref[:, lanes] += jnp.broadcast_to(d_lb, (8, HG_DIM))

    rev = lambda i: nb - 1 - i
    blk = lambda base: pl.BlockSpec((sup, wide), functools.partial(lambda h, i, b: (rev(i), b + h), b=base // wide))
    row_out = pl.BlockSpec((sup, wide), lambda h, i: (rev(i), h))
    dq, df, div, dlb = _call(
        body, name="hgrn_bwd", grid=(HG_HEADS // hp, nb),
        in_specs=[blk(COL_Q), blk(COL_F), blk(COL_I), pl.BlockSpec((1, wide), lambda h, i: (0, h)),
                  pl.BlockSpec((sup, wide), lambda h, i: (rev(i), h)),
                  pl.BlockSpec((hp, nc, HG_DIM, HG_DIM), lambda h, i: (h, rev(i), 0, 0))],
        out_specs=[row_out, row_out, row_out, pl.BlockSpec((8, wide), lambda h, i: (0, h))],
        out_shape=[jax.ShapeDtypeStruct((S, D_MODEL), BF16)] * 3 + [jax.ShapeDtypeStruct((8, D_MODEL), F32)],
        scratch_shapes=[pltpu.VMEM((hp, HG_DIM, HG_DIM), F32)],
        compiler_params=_params(40, 2),
    )(proj, proj, proj, lb, d_o, states)
    return dq, df, div, dlb[0:1]


def _s5_matrices(a_re, a_im, log_dt, b_re, b_im, c_re, c_im, d, seg_len):
    dt = jnp.exp(log_dt)[:, None]
    mag = jnp.exp(a_re * dt)
    lr, li = mag * jnp.cos(a_im * dt), mag * jnp.sin(a_im * dt)
    den = a_re * a_re + a_im * a_im
    nr = lr - 1.0
    sr = (nr * a_re + li * a_im) / den
    si = (li * a_re - nr * a_im) / den
    bbr = sr[..., None] * b_re - si[..., None] * b_im
    bbi = sr[..., None] * b_im + si[..., None] * b_re
    eye = jnp.eye(8, dtype=F32)

    def quad_cols(v):
        return v.reshape(S5_QUADS, 8 * S5_STATE)

    def lam_row(re_part, im_part):
        row = jnp.concatenate([quad_cols(re_part), quad_cols(im_part)], axis=1).reshape(1, S5_COLS)
        return jnp.broadcast_to(row, (S5_SEG, S5_COLS))

    def b_mat(bb):
        t = bb.reshape(S5_QUADS, 8, S5_STATE, S5_CH)
        return jnp.einsum("qgnc,gh->qgchn", t, eye).reshape(S5_QUADS, 8 * S5_CH, 8 * S5_STATE)

    def c_mat(cc):
        t = cc.reshape(S5_QUADS, 8, S5_CH, S5_STATE)
        return jnp.einsum("qgcn,gh->qgnhc", t, eye).reshape(S5_QUADS, 8 * S5_STATE, 8 * S5_CH)

    ang = a_im * dt * seg_len
    magp = jnp.exp(a_re * dt * seg_len)
    lpr, lpi = magp * jnp.cos(ang), magp * jnp.sin(ang)
    return dict(
        lam_r=lam_row(lr, lr), lam_i=lam_row(-li, li),
        b_q=jnp.concatenate([b_mat(bbr), b_mat(bbi)], axis=2),
        c_q=jnp.concatenate([c_mat(c_re), -c_mat(c_im)], axis=1),
        d_row=d.reshape(1, S5_WIDTH), pow_r=quad_cols(lpr), pow_i=quad_cols(lpi),
    )


def _s5_parts(v):
    half = S5_QCOLS // 2
    return tuple(v[:, k * half:(k + 1) * half] for k in range(2 * S5_QUADS))


def _s5_advance(parts, lr_ref, li_ref, x_ref, sl, conj):
    half = S5_QCOLS // 2
    out = []
    for q in range(S5_QUADS):
        re_c = slice(q * S5_QCOLS, q * S5_QCOLS + half)
        im_c = slice(q * S5_QCOLS + half, (q + 1) * S5_QCOLS)
        lr, li = lr_ref[:, re_c], li_ref[:, im_c]
        hr, hi = parts[2 * q], parts[2 * q + 1]
        if conj:
            out += [lr * hr + li * hi + x_ref[sl, re_c], lr * hi - li * hr + x_ref[sl, im_c]]
        else:
            out += [lr * hr - li * hi + x_ref[sl, re_c], lr * hi + li * hr + x_ref[sl, im_c]]
    return tuple(out)


def _scan_loop(step, init):
    def trip(o, carry):
        for j in range(S5_UNROLL):
            carry = step(o * S5_UNROLL + j, carry)
        return carry

    return lax.fori_loop(0, S5_TILE_STEPS // S5_UNROLL, trip, init)


def _s5_store(ref, sl, parts):
    half = S5_QCOLS // 2
    for k, v in enumerate(parts):
        ref[sl, k * half:(k + 1) * half] = v


def _s5_fwd_pass(u_perm, mats, h0, with_output):
    S = u_perm.shape[0]
    rows = S5_TILE_STEPS * S5_SEG
    nt = S // rows

    def body(*refs):
        if with_output:
            u_ref, b_ref, lr_ref, li_ref, h0_ref, c_ref, d_ref, y_ref, hinit_ref, hend_ref, xs, hcar = refs
        else:
            u_ref, b_ref, lr_ref, li_ref, h0_ref, hend_ref, xs, hcar = refs

        @pl.when(pl.program_id(0) == 0)
        def _():
            hcar[...] = h0_ref[...]

        if with_output:
            hinit_ref[...] = hcar[...]
        u = u_ref[...]
        ub = u.astype(BF16)
        for q in range(S5_QUADS):
            xs[:, q * S5_QCOLS:(q + 1) * S5_QCOLS] = jnp.dot(ub[:, q * 128:(q + 1) * 128], b_ref[q], preferred_element_type=F32)

        def step(t, h):
            sl = pl.ds(pl.multiple_of(t * S5_SEG, S5_SEG), S5_SEG)
            hn = _s5_advance(h, lr_ref, li_ref, xs, sl, False)
            _s5_store(xs, sl, hn)
            return hn

        h = _scan_loop(step, _s5_parts(hcar[...]))
        _s5_store(hcar, slice(None), h)
        _s5_store(hend_ref, slice(None), h)
        if with_output:
            ys = [jnp.dot(xs[:, q * S5_QCOLS:(q + 1) * S5_QCOLS].astype(BF16), c_ref[q], preferred_element_type=F32)
                  for q in range(S5_QUADS)]
            y_ref[...] = jnp.concatenate(ys, axis=1) + d_ref[...] * u

    full = lambda a: pl.BlockSpec(a.shape, functools.partial(lambda i, nd: (0,) * nd, nd=a.ndim))
    ins = [u_perm, mats["b_q"], mats["lam_r"], mats["lam_i"], h0]
    in_specs = [pl.BlockSpec((rows, S5_WIDTH), lambda i: (i, 0))] + [full(a) for a in ins[1:]]
    out_specs = [pl.BlockSpec((S5_SEG, S5_COLS), lambda i: (0, 0))]
    out_shape = [jax.ShapeDtypeStruct((S5_SEG, S5_COLS), F32)]
    if with_output:
        ins += [mats["c_q"], mats["d_row"]]
        in_specs += [full(mats["c_q"]), full(mats["d_row"])]
        out_specs = [pl.BlockSpec((rows, S5_WIDTH), lambda i: (i, 0)),
                     pl.BlockSpec((None, S5_SEG, S5_COLS), lambda i: (i, 0, 0))] + out_specs
        out_shape = [jax.ShapeDtypeStruct((S, S5_WIDTH), F32), jax.ShapeDtypeStruct((nt, S5_SEG, S5_COLS), F32)] + out_shape
    return _call(
        body, name="s5_fwd_y" if with_output else "s5_fwd_ends", grid=(nt,), in_specs=in_specs, out_specs=out_specs,
        out_shape=out_shape,
        scratch_shapes=[pltpu.VMEM((rows, S5_COLS), F32), pltpu.VMEM((S5_SEG, S5_COLS), F32)],
        compiler_params=_params(40),
    )(*ins)


def _s5_bwd_ends(dy_perm, mats):
    S = dy_perm.shape[0]
    rows = S5_TILE_STEPS * S5_SEG
    nt = S // rows

    def body(dy_ref, c_ref, lr_ref, li_ref, gend_ref, gs, gcar):
        @pl.when(pl.program_id(0) == 0)
        def _():
            gcar[...] = jnp.zeros(gcar.shape, F32)

        dyb = dy_ref[...].astype(BF16)
        for q in range(S5_QUADS):
            gs[:, q * S5_QCOLS:(q + 1) * S5_QCOLS] = lax.dot_general(
                dyb[:, q * 128:(q + 1) * 128], c_ref[q], (((1,), (1,)), ((), ())), preferred_element_type=F32)

        def step(k, g):
            t = S5_TILE_STEPS - 1 - k
            sl = pl.ds(pl.multiple_of(t * S5_SEG, S5_SEG), S5_SEG)
            return _s5_advance(g, lr_ref, li_ref, gs, sl, True)

        g = _scan_loop(step, _s5_parts(gcar[...]))
        _s5_store(gcar, slice(None), g)
        _s5_store(gend_ref, slice(None), g)

    full = lambda a: pl.BlockSpec(a.shape, functools.partial(lambda i, nd: (0,) * nd, nd=a.ndim))
    return _call(
        body, name="s5_bwd_ends", grid=(nt,),
        in_specs=[pl.BlockSpec((rows, S5_WIDTH), lambda i: (nt - 1 - i, 0)), full(mats["c_q"]), full(mats["lam_r"]),
                  full(mats["lam_i"])],
        out_specs=pl.BlockSpec((S5_SEG, S5_COLS), lambda i: (0, 0)),
        out_shape=jax.ShapeDtypeStruct((S5_SEG, S5_COLS), F32),
        scratch_shapes=[pltpu.VMEM((rows, S5_COLS), F32), pltpu.VMEM((S5_SEG, S5_COLS), F32)],
        compiler_params=_params(40),
    )(dy_perm, mats["c_q"], mats["lam_r"], mats["lam_i"])


def _s5_bwd_full(u_perm, dy_perm, hinit, g0, mats):
    S = u_perm.shape[0]
    rows = S5_TILE_STEPS * S5_SEG
    nt = S // rows

    def body(u_ref, dy_ref, hinit_ref, g0_ref, b_ref, c_ref, lr_ref, li_ref, d_ref,
             du_ref, dp_ref, dq_ref, db_ref, dc_ref, dd_ref, hs, gs, gcar):
        @pl.when(pl.program_id(0) == 0)
        def _():
            gcar[...] = g0_ref[...]
            for ref in (dp_ref, dq_ref, db_ref, dc_ref, dd_ref):
                ref[...] = jnp.zeros(ref.shape, F32)

        u, dy = u_ref[...], dy_ref[...]
        ub, dyb = u.astype(BF16), dy.astype(BF16)
        hs[0:S5_SEG, :] = hinit_ref[...]
        for q in range(S5_QUADS):
            cols = slice(q * S5_QCOLS, (q + 1) * S5_QCOLS)
            hs[S5_SEG:, cols] = jnp.dot(ub[:, q * 128:(q + 1) * 128], b_ref[q], preferred_element_type=F32)
            gs[:, cols] = lax.dot_general(dyb[:, q * 128:(q + 1) * 128], c_ref[q], (((1,), (1,)), ((), ())),
                                          preferred_element_type=F32)

        def fstep(t, h):
            sl = pl.ds(pl.multiple_of((t + 1) * S5_SEG, S5_SEG), S5_SEG)
            hn = _s5_advance(h, lr_ref, li_ref, hs, sl, False)
            _s5_store(hs, sl, hn)
            return hn

        _scan_loop(fstep, _s5_parts(hinit_ref[...]))

        def bstep(k, g):
            t = S5_TILE_STEPS - 1 - k
            sl = pl.ds(pl.multiple_of(t * S5_SEG, S5_SEG), S5_SEG)
            gn = _s5_advance(g, lr_ref, li_ref, gs, sl, True)
            _s5_store(gs, sl, gn)
            return gn

        _s5_store(gcar, slice(None), _scan_loop(bstep, _s5_parts(gcar[...])))

        half = S5_QCOLS // 2
        dus = []
        for q in range(S5_QUADS):
            cols = slice(q * S5_QCOLS, (q + 1) * S5_QCOLS)

            def astep(t, carry, q=q):
                sl = pl.ds(pl.multiple_of(t * S5_SEG, S5_SEG), S5_SEG)
                g = gs[sl, q * S5_QCOLS:(q + 1) * S5_QCOLS]
                hp = hs[sl, q * S5_QCOLS:(q + 1) * S5_QCOLS]
                hp_sw = jnp.concatenate([hp[:, half:], hp[:, :half]], axis=1)
                return carry[0] + g * hp, carry[1] + g * hp_sw

            zero = jnp.zeros((S5_SEG, S5_QCOLS), F32)
            acc_p, acc_q = _scan_loop(astep, (zero, zero))
            dp_ref[:, cols] += jnp.sum(acc_p, axis=0, keepdims=True)
            dq_ref[:, cols] += jnp.sum(acc_q, axis=0, keepdims=True)
            gq = gs[:, cols].astype(BF16)
            db_ref[q] += lax.dot_general(ub[:, q * 128:(q + 1) * 128], gq, (((0,), (0,)), ((), ())),
                                         preferred_element_type=F32)
            hq = hs[S5_SEG:, cols].astype(BF16)
            dc_ref[q] += lax.dot_general(dyb[:, q * 128:(q + 1) * 128], hq, (((0,), (0,)), ((), ())),
                                         preferred_element_type=F32)
            dus.append(lax.dot_general(gq, b_ref[q], (((1,), (1,)), ((), ())), preferred_element_type=F32))
        du_ref[...] = (jnp.concatenate(dus, axis=1) + d_ref[...] * dy).astype(du_ref.dtype)
        dd_ref[...] += jnp.sum(dy * u, axis=0, keepdims=True)

    full = lambda a: pl.BlockSpec(a.shape, functools.partial(lambda i, nd: (0,) * nd, nd=a.ndim))
    rev_rows = pl.BlockSpec((rows, S5_WIDTH), lambda i: (nt - 1 - i, 0))
    consts = [mats["b_q"], mats["c_q"], mats["lam_r"], mats["lam_i"], mats["d_row"]]
    acc = lambda s: pl.BlockSpec(s, functools.partial(lambda i, nd: (0,) * nd, nd=len(s)))
    acc_shapes = [(1, S5_COLS), (1, S5_COLS), (S5_QUADS, 128, S5_QCOLS), (S5_QUADS, 128, S5_QCOLS), (1, S5_WIDTH)]
    return _call(
        body, name="s5_bwd_full", grid=(nt,),
        in_specs=[rev_rows, rev_rows, pl.BlockSpec((None, S5_SEG, S5_COLS), lambda i: (nt - 1 - i, 0, 0)), full(g0)]
        + [full(a) for a in consts],
        out_specs=[rev_rows] + [acc(s) for s in acc_shapes],
        out_shape=[jax.ShapeDtypeStruct((S, S5_WIDTH), BF16)] + [jax.ShapeDtypeStruct(s, F32) for s in acc_shapes],
        scratch_shapes=[pltpu.VMEM((rows + S5_SEG, S5_COLS), F32), pltpu.VMEM((rows, S5_COLS), F32),
                        pltpu.VMEM((S5_SEG, S5_COLS), F32)],
        compiler_params=_params(56),
    )(u_perm, dy_perm, hinit, g0, *consts)


def _cmul(ar, ai, br, bi):
    return ar * br - ai * bi, ar * bi + ai * br


def _split_cols(v):
    t = v.reshape(v.shape[0], S5_QUADS, 2, S5_QCOLS // 2)
    return t[:, :, 0], t[:, :, 1]


def _join_cols(re, im):
    return jnp.stack([re, im], axis=2).reshape(re.shape[0], S5_COLS)


def _segment_starts(ends, pow_r, pow_i, reverse):
    er, ei = _split_cols(ends)
    pi = -pow_i if reverse else pow_i
    order = list(range(S5_SEG))
    if reverse:
        order = order[::-1]
    cr, ci = jnp.zeros_like(er[0]), jnp.zeros_like(ei[0])
    out_r, out_i = [None] * S5_SEG, [None] * S5_SEG
    for j in order:
        out_r[j], out_i[j] = cr, ci
        mr, mi = _cmul(pow_r, pi, cr, ci)
        cr, ci = mr + er[j], mi + ei[j]
    return _join_cols(jnp.stack(out_r), jnp.stack(out_i))


def _to_segments(a):
    S, w = a.shape
    return a.reshape(S5_SEG, S // S5_SEG, w).transpose(1, 0, 2).reshape(S, w)


def _from_segments(a):
    S, w = a.shape
    return a.reshape(S // S5_SEG, S5_SEG, w).transpose(1, 0, 2).reshape(S, w)


def _my_pos():
    return lax.axis_index("x"), lax.axis_index("y"), lax.axis_index("c")


def _flip(pos, k):
    x, y, c = pos
    return (1 - x if k & 4 else x, 1 - y if k & 2 else y, 1 - c if k & 1 else c)


def _index_of(pos):
    return 4 * pos[0] + 2 * pos[1] + pos[2]


_GATHER_FLIPS = (0, 1, 4, 5, 2, 3, 6, 7)


def _inproj_gather(x, norm_g, pack_a, pack_b, pack_c):
    S = x.shape[0]
    tm = min(S, 1024)
    n_i = S // tm
    order = jnp.stack([_index_of(_flip(_my_pos(), k)) for k in _GATHER_FLIPS]).astype(jnp.int32)

    def body(order_ref, x_ref, g_ref, pa_ref, pb_ref, pc_ref, ut_ref, proj_ref, oa_ref, ob_ref, oc_ref,
             wv, u_scr, send_sems, recv_sems, local_sems):
        s, i = pl.program_id(0), pl.program_id(1)
        me = _my_pos()
        mine = _index_of(me)
        sibling = _flip(me, 1)
        srcs = (pb_ref, pa_ref, pc_ref)
        dsts = (wv, oa_ref, oc_ref)

        def direct(a, k):
            return pltpu.make_async_remote_copy(
                src_ref=srcs[a], dst_ref=dsts[a].at[mine], send_sem=send_sems.at[a * 8 + k],
                recv_sem=recv_sems.at[a * 8 + k], device_id=_flip(me, k), device_id_type=MESH)

        def passed_on(a, k):
            slot = _index_of(_flip(me, k))
            return pltpu.make_async_remote_copy(
                src_ref=dsts[a].at[slot], dst_ref=dsts[a].at[slot], send_sem=send_sems.at[a * 8 + (k | 1)],
                recv_sem=recv_sems.at[a * 8 + (k | 1)], device_id=sibling, device_id_type=MESH)

        def arrival(a, k):
            slot = _index_of(_flip(me, k))
            pltpu.make_async_remote_copy(
                src_ref=dsts[a].at[slot], dst_ref=dsts[a].at[slot], send_sem=send_sems.at[a * 8 + k],
                recv_sem=recv_sems.at[a * 8 + k], device_id=me, device_id_type=MESH).wait_recv()

        def own_copy(a):
            return pltpu.make_async_copy(srcs[a], dsts[a].at[mine], local_sems.at[a])

        def keep(idx):
            slot = _index_of(_flip(me, _GATHER_FLIPS[idx]))
            return pltpu.make_async_copy(wv.at[slot], ob_ref.at[slot], local_sems.at[3 + idx])


        first = (s == 0) & (i == 0)

        @pl.when(first)
        def _():
            for a in range(3):
                own_copy(a).start()
            for k in (1, 4, 2):
                direct(0, k).start()
            own_copy(0).wait()
            keep(0).start()

        for idx, k in enumerate(_GATHER_FLIPS):
            if idx == 0:
                continue

            @pl.when((s == idx) & (i == 0))
            def _(idx=idx, k=k):
                arrival(0, k)
                if k in (4, 2, 6):
                    passed_on(0, k).start()
                keep(idx).start()
                if idx == 1:
                    direct(0, 6).start()
                if idx == 2:
                    for a in (1, 2):
                        for k in (1, 4, 2, 6):
                            direct(a, k).start()

        @pl.when(s == 0)
        def _():
            y, _, _ = _rms_fwd(x_ref[...], g_ref[...])
            u_scr[pl.ds(pl.multiple_of(i * tm, tm), tm), :] = y.astype(BF16)
            ut_ref[...] = y.T.astype(BF16)

        ub = u_scr[pl.ds(pl.multiple_of(i * tm, tm), tm), :]
        proj_ref[...] = jnp.dot(ub, wv[order_ref[s]], preferred_element_type=F32)

        @pl.when((s == N_DEV - 1) & (i == n_i - 1))
        def _():
            for a in (1, 2):
                for k in (4, 2, 6):
                    arrival(a, k)
                    passed_on(a, k).start()
            for a in (1, 2):
                for k in (1, 5, 3, 7):
                    arrival(a, k)
                own_copy(a).wait()
            for a in range(3):
                for k in (1, 4, 2, 6):
                    direct(a, k).wait_send()
                for k in (4, 2, 6):
                    passed_on(a, k).wait_send()
            for idx in range(N_DEV):
                keep(idx).wait()

    any_spec = pl.BlockSpec(memory_space=pl.ANY)
    vmem = pl.BlockSpec(memory_space=pltpu.VMEM)
    grid_spec = pltpu.PrefetchScalarGridSpec(
        num_scalar_prefetch=1, grid=(N_DEV, n_i),
        in_specs=[pl.BlockSpec((tm, D_MODEL), lambda s, i, o: (jnp.where(s == 0, i, 0), 0)),
                  pl.BlockSpec((1, D_MODEL), lambda s, i, o: (0, 0)), any_spec, vmem, any_spec],
        out_specs=[pl.BlockSpec((D_MODEL, tm), lambda s, i, o: (0, jnp.where(s == 0, i, n_i - 1))),
                   pl.BlockSpec((tm, SHARD_IN), lambda s, i, o: (i, o[s])), any_spec, any_spec, any_spec],
        scratch_shapes=[pltpu.VMEM((N_DEV,) + pack_b.shape, BF16), pltpu.VMEM((S, D_MODEL), BF16),
                        pltpu.SemaphoreType.DMA((24,)), pltpu.SemaphoreType.DMA((24,)), pltpu.SemaphoreType.DMA((3 + N_DEV,))],
    )
    return _call(
        body, name="inproj_gather", grid_spec=grid_spec,
        out_shape=[jax.ShapeDtypeStruct((D_MODEL, S), BF16), jax.ShapeDtypeStruct((S, IN_COLS), F32),
                   jax.ShapeDtypeStruct((N_DEV,) + pack_a.shape, BF16), jax.ShapeDtypeStruct((N_DEV,) + pack_b.shape, BF16),
                   jax.ShapeDtypeStruct((N_DEV,) + pack_c.shape, BF16)],
        compiler_params=_params(56, 2),
    )(order, x, norm_g, pack_a, pack_b, pack_c)


_SCATTER_FLIPS = (7, 6, 5, 4, 3, 2, 1, 0)
_N_CHIPS = 4


def _for_row_chunks(n_rows, chunk, fn):
    def step(c, carry):
        fn(pl.ds(pl.multiple_of(c * chunk, chunk), chunk))
        return carry

    lax.fori_loop(0, n_rows // chunk, step, 0)


def _grad_w_in_scatter(dproj, u_t, rs_a, rs_c, small_partial):
    S = u_t.shape[1]
    tm = min(S, 1024)
    n_i = S // tm
    order = jnp.stack([_index_of(_flip(_my_pos(), k)) for k in _SCATTER_FLIPS]).astype(jnp.int32)
    shapes = ((D_MODEL, SHARD_IN), rs_a.shape[1:], rs_c.shape[1:])
    row_chunk = 128

    def body(order_ref, dp_ref, ut_ref, ra_ref, rc_ref, p_ref, gb_ref, ga_ref, gc_ref, gs_ref, acc, sib_b, d2d_b,
             send_b, ici_b, mine_a, sib_a, ici_a, mine_c, sib_c, ici_c, gath, send_sems, recv_sems, local_sems):
        s, i = pl.program_id(0), pl.program_id(1)
        me = _my_pos()
        mine_idx = _index_of(me)
        sibling = _flip(me, 1)

        def small_to(k):
            return pltpu.make_async_remote_copy(
                src_ref=p_ref, dst_ref=gath.at[mine_idx], send_sem=send_sems.at[21 + k - 1],
                recv_sem=recv_sems.at[21 + k - 1], device_id=_flip(me, k), device_id_type=MESH)

        def small_from(k):
            pltpu.make_async_remote_copy(
                src_ref=p_ref, dst_ref=gath.at[_index_of(_flip(me, k))], send_sem=send_sems.at[21 + k - 1],
                recv_sem=recv_sems.at[21 + k - 1], device_id=me, device_id_type=MESH).wait_recv()
        sib = (sib_b, sib_a, sib_c)
        ici = (ici_b, ici_a, ici_c)
        outs = (gb_ref, ga_ref, gc_ref)

        def to_sibling(arr, m, src):
            return pltpu.make_async_remote_copy(
                src_ref=src, dst_ref=sib[arr].at[m], send_sem=send_sems.at[arr * 7 + m],
                recv_sem=recv_sems.at[arr * 7 + m], device_id=sibling, device_id_type=MESH)

        def over_ici(arr, m, src):
            return pltpu.make_async_remote_copy(
                src_ref=src, dst_ref=ici[arr].at[m], send_sem=send_sems.at[arr * 7 + 4 + m],
                recv_sem=recv_sems.at[arr * 7 + 4 + m], device_id=_flip(me, 6 - 2 * m), device_id_type=MESH)

        def from_sibling(arr, m):
            to_sibling(arr, m, sib[arr].at[m]).wait_recv()

        def from_ici(arr, m):
            over_ici(arr, m, ici[arr].at[m]).wait_recv()

        small = ((1, ra_ref, mine_a), (2, rc_ref, mine_c))

        def local_copy(arr, src, mine, m):
            return pltpu.make_async_copy(src.at[_index_of(_flip(me, 6 - 2 * m))], mine.at[m],
                                         local_sems.at[(arr - 1) * _N_CHIPS + m])

        @pl.when((s == 0) & (i == 0))
        def _():
            gath[mine_idx] = p_ref[...]
            for k in range(1, N_DEV):
                small_to(k).start()
            for arr, src, mine in small:
                for m in range(_N_CHIPS):
                    to_sibling(arr, m, src.at[_index_of(_flip(me, 7 - 2 * m))]).start()
                    local_copy(arr, src, mine, m).start()

        @pl.when((s == 1) & (i == 0))
        def _():
            for arr, src, mine in small:
                rows, chunk = shapes[arr][0], 16
                for m in range(_N_CHIPS):
                    local_copy(arr, src, mine, m).wait()
                    from_sibling(arr, m)
                    if m < _N_CHIPS - 1:
                        def add(sl, arr=arr, mine=mine, m=m):
                            mine[m, sl, :] = (mine[m, sl, :].astype(F32) + sib[arr][m, sl, :].astype(F32)).astype(BF16)

                        _for_row_chunks(rows, chunk, add)
                        over_ici(arr, m, mine.at[m]).start()
                    else:
                        def keep(sl, arr=arr, mine=mine, m=m):
                            outs[arr][sl, :] = mine[m, sl, :].astype(F32) + sib[arr][m, sl, :].astype(F32)

                        _for_row_chunks(rows, chunk, keep)

        @pl.when(i == 0)
        def _():
            acc[...] = jnp.zeros(acc.shape, F32)

        acc[...] += jnp.dot(ut_ref[...], dp_ref[...], preferred_element_type=F32)

        def block_rows(c):
            return acc[c * row_chunk:(c + 1) * row_chunk, :]

        for m in range(_N_CHIPS):
            @pl.when((s == 2 * m) & (i == n_i - 1))
            def _(m=m):
                if m > 0:
                    to_sibling(0, m - 1, d2d_b).wait_send()
                for c in range(D_MODEL // row_chunk):
                    d2d_b[c * row_chunk:(c + 1) * row_chunk, :] = block_rows(c).astype(BF16)
                to_sibling(0, m, d2d_b).start()

            @pl.when((s == 2 * m + 1) & (i == n_i - 1))
            def _(m=m):
                from_sibling(0, m)
                slot = m % 2
                if m == 2:
                    over_ici(0, 0, send_b.at[0]).wait_send()
                for c in range(D_MODEL // row_chunk):
                    rows = slice(c * row_chunk, (c + 1) * row_chunk)
                    total = block_rows(c) + sib_b[m, rows, :].astype(F32)
                    if m < _N_CHIPS - 1:
                        send_b[slot, rows, :] = total.astype(BF16)
                    else:
                        gb_ref[rows, :] = total
                if m < _N_CHIPS - 1:
                    over_ici(0, m, send_b.at[slot]).start()

        @pl.when((s == N_DEV - 1) & (i == n_i - 1))
        def _():
            for arr in range(3):
                for m in range(_N_CHIPS - 1):
                    from_ici(arr, m)
                rows = shapes[arr][0]

                def add(sl, arr=arr):
                    outs[arr][sl, :] = (outs[arr][sl, :] + ici[arr][0, sl, :].astype(F32)
                                        + ici[arr][1, sl, :].astype(F32) + ici[arr][2, sl, :].astype(F32))

                _for_row_chunks(rows, 16, add)
            for k in range(1, N_DEV):
                small_from(k)
            total = gath[0]
            for dev in range(1, N_DEV):
                total = total + gath[dev]
            gs_ref[...] = total
            for k in range(1, N_DEV):
                small_to(k).wait_send()
            to_sibling(0, _N_CHIPS - 1, d2d_b).wait_send()
            over_ici(0, 1, send_b.at[1]).wait_send()
            over_ici(0, 2, send_b.at[0]).wait_send()
            for arr, src, mine in small:
                for m in range(_N_CHIPS):
                    to_sibling(arr, m, src.at[0]).wait_send()
                for m in range(_N_CHIPS - 1):
                    over_ici(arr, m, mine.at[m]).wait_send()

    any_spec = pl.BlockSpec(memory_space=pl.ANY)
    vmem = pl.BlockSpec(memory_space=pltpu.VMEM)
    half = lambda shp, n: pltpu.VMEM((n,) + tuple(shp), BF16)
    grid_spec = pltpu.PrefetchScalarGridSpec(
        num_scalar_prefetch=1, grid=(N_DEV, n_i),
        in_specs=[pl.BlockSpec((tm, SHARD_IN), lambda s, i, o: (i, o[s])),
                  pl.BlockSpec((D_MODEL, tm), lambda s, i, o: (0, i)), any_spec, any_spec, vmem],
        out_specs=[vmem, vmem, vmem, vmem],
        scratch_shapes=[
            pltpu.VMEM((D_MODEL, SHARD_IN), F32), half(shapes[0], _N_CHIPS), pltpu.VMEM(shapes[0], BF16),
            half(shapes[0], 2), half(shapes[0], _N_CHIPS - 1),
            half(shapes[1], _N_CHIPS), half(shapes[1], _N_CHIPS), half(shapes[1], _N_CHIPS - 1),
            half(shapes[2], _N_CHIPS), half(shapes[2], _N_CHIPS), half(shapes[2], _N_CHIPS - 1),
            pltpu.VMEM((N_DEV,) + small_partial.shape, F32),
            pltpu.SemaphoreType.DMA((28,)), pltpu.SemaphoreType.DMA((28,)), pltpu.SemaphoreType.DMA((2 * _N_CHIPS,))],
    )
    return _call(
        body, name="grad_w_in_scatter", grid_spec=grid_spec,
        out_shape=[jax.ShapeDtypeStruct(shp, F32) for shp in shapes] + [jax.ShapeDtypeStruct(small_partial.shape, F32)],
        compiler_params=_params(60, 2),
    )(order, dproj, u_t, rs_a, rs_c, small_partial)


def _adam_update(g, w, m, v):
    m2 = ADAM_B1 * m + (1.0 - ADAM_B1) * g
    v2 = ADAM_B2 * v + (1.0 - ADAM_B2) * (g * g)
    m_hat = m2 / (1.0 - ADAM_B1 ** ADAM_STEP)
    v_hat = v2 / (1.0 - ADAM_B2 ** ADAM_STEP)
    delta = -ADAM_LR * (m_hat / (jnp.sqrt(v_hat) + ADAM_EPS) + ADAM_WD * w)
    return delta, m2, v2


def _adam_rows(g, w, m, v):
    rows, cols = w.shape
    tm = rows if rows % 256 else 256

    def fn(rv, cr, out):
        return list(_adam_update(*rv)), []

    outs, _ = _rowwise("adamw", fn, rows, tm, [(a, cols, 0) for a in (g, w, m, v)], [], [(cols, F32)] * 3, [], 32)
    return outs


_SMALL = ["norm_g", "hg_lb", "hg_norm_g", "s5_a_re", "s5_a_im", "s5_log_dt", "s5_b_re", "s5_b_im", "s5_c_re",
          "s5_c_im", "s5_d", "b_glu", "ple_norm_g", "final_norm_g"]
_BIG = ["w_in", "w_o_hg", "w_glu", "w_o_s5", "w_out", "w_ple", "w_ple_gate"]
_ORDER = ["norm_g", "w_in", "hg_lb", "hg_norm_g", "w_o_hg", "s5_a_re", "s5_a_im", "s5_log_dt", "s5_b_re", "s5_b_im",
          "s5_c_re", "s5_c_im", "s5_d", "w_glu", "b_glu", "w_o_s5", "w_out", "ple_norm_g", "w_ple", "w_ple_gate",
          "final_norm_g"]


def _pack_small(vals, tail=None):
    parts = []
    for name in _SMALL:
        flat = vals[name].reshape(-1).astype(F32)
        pad = (-flat.shape[0]) % 1024
        parts.append(jnp.pad(flat, (0, pad)))
    tail = jnp.zeros((0,), F32) if tail is None else tail.reshape(-1).astype(F32)
    parts.append(jnp.pad(tail, (0, 1024 - tail.shape[0])))
    return jnp.concatenate(parts).reshape(-1, 128)


def _unpack_small(packed, like):
    flat = packed.reshape(-1)
    out, off = {}, 0
    for name in _SMALL:
        size = like[name].size
        out[name] = flat[off:off + size].reshape(like[name].shape)
        off += size + (-size) % 1024
    return out


def _col_blocks(full):
    k = full.shape[0]
    return full.reshape(k, N_DEV, 128).transpose(1, 0, 2)


def _from_col_blocks(blocks):
    k = blocks.shape[1]
    return blocks.transpose(1, 0, 2).reshape(k, N_DEV * 128)


def kernel(x, p, norm_g, w_in, hg_lb, hg_norm_g, w_o_hg, s5_a_re, s5_a_im, s5_log_dt, s5_b_re, s5_b_im, s5_c_re, s5_c_im, s5_d, w_glu, b_glu, w_o_s5, w_out, ple_norm_g, w_ple, w_ple_gate, final_norm_g, loss_target, m_norm_g, m_w_in, m_hg_lb, m_hg_norm_g, m_w_o_hg, m_s5_a_re, m_s5_a_im, m_s5_log_dt, m_s5_b_re, m_s5_b_im, m_s5_c_re, m_s5_c_im, m_s5_d, m_w_glu, m_b_glu, m_w_o_s5, m_w_out, m_ple_norm_g, m_w_ple, m_w_ple_gate, m_final_norm_g, v_norm_g, v_w_in, v_hg_lb, v_hg_norm_g, v_w_o_hg, v_s5_a_re, v_s5_a_im, v_s5_log_dt, v_s5_b_re, v_s5_b_im, v_s5_c_re, v_s5_c_im, v_s5_d, v_w_glu, v_b_glu, v_w_o_s5, v_w_out, v_ple_norm_g, v_w_ple, v_w_ple_gate, v_final_norm_g):
    args = dict(locals())
    w = {n: args[n] for n in _ORDER}
    m = {n: args["m_" + n] for n in _ORDER}
    v = {n: args["v_" + n] for n in _ORDER}
    xs = x[0]
    ps = p[0, 0]
    tgt = loss_target[0]
    S = xs.shape[0]

    pack_a = jnp.concatenate([w_o_hg[0], w_out[0], w_ple_gate[0]], axis=0).astype(BF16)
    pack_b = w_in[0].astype(BF16)
    pack_c = jnp.concatenate([w_glu[0], w_o_s5[0], w_ple[0]], axis=0).astype(BF16)
    u_t, proj, all_a, all_b, all_c = _inproj_gather(xs, norm_g, pack_a, pack_b, pack_c)
    wf_o_hg = all_a[:, 0:128].reshape(D_MODEL, D_MODEL)
    wf_out = all_a[:, 128:256].reshape(D_MODEL, D_MODEL)
    wf_pg = all_a[:, 256:384].reshape(D_MODEL, D_MODEL)
    wf_glu = _from_col_blocks(all_c[:, 0:512])
    wf_o_s5 = _from_col_blocks(all_c[:, 512:1024])
    wf_ple = _from_col_blocks(all_c[:, 1024:1280])

    lb = jax.nn.sigmoid(hg_lb[0:1] - hg_lb[1:2])
    s5_names = ["s5_a_re", "s5_a_im", "s5_log_dt", "s5_b_re", "s5_b_im", "s5_c_re", "s5_c_im", "s5_d"]
    build = lambda *a: _s5_matrices(*a, seg_len=S // S5_SEG)
    mats_f32, mats_vjp = jax.vjp(build, *[w[n][0] for n in s5_names])
    mats = dict(mats_f32, b_q=mats_f32["b_q"].astype(BF16), c_q=mats_f32["c_q"].astype(BF16))
    bias_glu = b_glu

    o, states = _hgrn_fwd(proj, lb)
    u_perm = _to_segments(proj[:, COL_US:COL_US + S5_WIDTH])
    zeros_state = jnp.zeros((S5_SEG, S5_COLS), F32)
    (h_ends,) = _s5_fwd_pass(u_perm, mats, zeros_state, False)
    h0 = _segment_starts(h_ends, mats["pow_r"], mats["pow_i"], False)
    y_perm, h_init, _ = _s5_fwd_pass(u_perm, mats, h0, True)
    ys = _from_segments(y_perm)
    y_hg, y_s5, glu, h1 = _stage_branches(o, proj, ys, xs, hg_norm_g, wf_o_hg, wf_glu, bias_glu, wf_o_s5, wf_out)

    dh1, (loss_acc, d_final_g, d_ple_g, d_w_ple, d_w_pg) = _stage_ple_loss(
        h1, ps, tgt, ple_norm_g, wf_ple, wf_pg, final_norm_g.reshape(1, D_MODEL))
    (d_gate_hg, d_gate_s5, d_yhg, d_ys5), (d_w_out,) = _stage_bwd_merge(dh1, y_hg, y_s5, proj, wf_out)
    (d_o, d_g_hg), (d_w_o_hg, d_hg_norm) = _stage_bwd_hg_path(d_yhg, o, proj, hg_norm_g, wf_o_hg)
    (d_ys, d_z), (d_w_o_s5, d_w_glu, d_b_glu) = _stage_bwd_s5_path(d_ys5, ys, glu, proj, wf_o_s5, wf_glu)
    dq, df, div, d_lb = _hgrn_bwd(proj, lb, d_o, states)
    dy_perm = _to_segments(d_ys)
    g_ends = _s5_bwd_ends(dy_perm, mats)
    g0 = _segment_starts(g_ends, mats["pow_r"], mats["pow_i"], True)
    du_perm, acc_p, acc_q, d_bq, d_cq_t, d_d = _s5_bwd_full(u_perm, dy_perm, h_init, g0, mats)
    d_us = _from_segments(du_perm)
    grad_x, dproj, d_norm_g = _stage_inproj_bwd([dq, df, div, d_g_hg, d_us, d_z, d_gate_hg, d_gate_s5], xs, dh1,
                                                norm_g, all_b)

    p_re, p_im = _split_cols(acc_p)
    q_re, q_im = _split_cols(acc_q)
    d_lam_r = (p_re + p_im)[0]
    d_lam_i = (q_im - q_re)[0]
    zero_row = jnp.zeros((S5_SEG, S5_COLS), F32)
    row_of = lambda re_part, im_part: zero_row.at[0].set(_join_cols(re_part[None], im_part[None])[0])
    zeros_q = jnp.zeros_like(d_lam_r)
    cot = dict(
        lam_r=row_of(d_lam_r, zeros_q), lam_i=row_of(zeros_q, d_lam_i),
        b_q=d_bq, c_q=d_cq_t.transpose(0, 2, 1), d_row=d_d,
        pow_r=jnp.zeros_like(mats["pow_r"]), pow_i=jnp.zeros_like(mats["pow_i"]),
    )
    d_s5 = mats_vjp(cot)

    s_lb = lb * (1.0 - lb)
    d_hg_lb = jnp.concatenate([d_lb * s_lb, -d_lb * s_lb], axis=0)
    small_g = dict(norm_g=d_norm_g, hg_lb=d_hg_lb, hg_norm_g=d_hg_norm, b_glu=d_b_glu, ple_norm_g=d_ple_g,
                   final_norm_g=d_final_g)
    for name, g in zip(s5_names, d_s5):
        small_g[name] = g
    pk = lambda d: _pack_small({n: d[n] for n in _SMALL})
    rs_a = jnp.concatenate([d_w_o_hg.reshape(N_DEV, 128, D_MODEL), d_w_out.reshape(N_DEV, 128, D_MODEL),
                            d_w_pg.reshape(N_DEV, 128, D_MODEL)], axis=1).astype(BF16)
    rs_c = jnp.concatenate([_col_blocks(d_w_glu), _col_blocks(d_w_o_s5), _col_blocks(d_w_ple)], axis=1).astype(BF16)
    partial = _pack_small({n: small_g[n] for n in _SMALL}, tail=loss_acc[0, 0:1])
    g_b, g_a, g_c, sg = _grad_w_in_scatter(dproj, u_t, rs_a, rs_c, partial)
    sd, sm, sv = _adam_rows(sg, pk(w), pk(m), pk(v))
    like = {n: w[n] for n in _SMALL}
    out_g, out_d, out_m, out_v = (_unpack_small(t, like) for t in (sg, sd, sm, sv))
    big_g = dict(w_o_hg=g_a[0:128], w_out=g_a[128:256], w_ple_gate=g_a[256:384], w_in=g_b,
                 w_glu=g_c[0:512], w_o_s5=g_c[512:1024], w_ple=g_c[1024:1280])
    for name in _BIG:
        shape = w[name].shape
        g2 = big_g[name]
        d2, m2, v2 = _adam_rows(g2, w[name][0], m[name][0], v[name][0])
        out_g[name], out_d[name], out_m[name], out_v[name] = (t.reshape(shape) for t in (g2, d2, m2, v2))

    loss = sg[sg.shape[0] - 8, 0]
    return (loss, grad_x[None], *[out_g[n] for n in _ORDER], *[out_d[n] for n in _ORDER],
            *[out_m[n] for n in _ORDER], *[out_v[n] for n in _ORDER])
```

```python
import functools
import math

import jax
import jax.numpy as jnp
from jax import lax
from jax.experimental import pallas as pl
from jax.experimental.pallas import tpu as pltpu

F32 = jnp.float32
BF16 = jnp.bfloat16

D_MODEL = 1024
N_DEV = 8
IN_COLS = 7168
SHARD_IN = IN_COLS // N_DEV
HG_HEADS = 8
HG_DIM = 128
HG_CHUNK = 64
HG_SUPER_FWD = 256
HG_SUPER_BWD = 128
HG_HEADS_PER_STEP = 8
S5_WIDTH = 512
S5_GROUPS = 32
S5_STATE = 64
S5_CH = 16
S5_SEG = 8
S5_QUADS = 4
S5_QCOLS = 1024
S5_COLS = S5_QUADS * S5_QCOLS
S5_TILE_STEPS = 64
S5_UNROLL = 8
NORM_EPS = 1e-6
ADAM_LR = 0.001
ADAM_B1 = 0.9
ADAM_B2 = 0.999
ADAM_EPS = 1e-08
ADAM_WD = 0.01
ADAM_STEP = 10
MIB = 1024 * 1024
MESH = pl.DeviceIdType.MESH

COL_Q, COL_F, COL_I, COL_G, COL_US, COL_ZS, COL_GH, COL_GS = 0, 1024, 2048, 3072, 4096, 4608, 5120, 6144


def _call(body, **kw):
    return pl.pallas_call(body, **kw)


def _params(vmem_mb, n_grid=1):
    return pltpu.CompilerParams(
        dimension_semantics=("arbitrary",) * n_grid, vmem_limit_bytes=vmem_mb * MIB
    )


def _bdot(a, b):
    return jnp.dot(a.astype(BF16), b.astype(BF16), preferred_element_type=F32)


def _bdot_nt(a, b):
    return lax.dot_general(a.astype(BF16), b.astype(BF16), (((1,), (1,)), ((), ())), preferred_element_type=F32)


def _bdot_tn(a, b):
    return lax.dot_general(a.astype(BF16), b.astype(BF16), (((0,), (0,)), ((), ())), preferred_element_type=F32)


def _sigmoid(x):
    return jax.nn.sigmoid(x)


def _silu(x):
    return x * _sigmoid(x)


def _dsilu(x):
    s = _sigmoid(x)
    return s * (1.0 + x * (1.0 - s))


_GELU_C = math.sqrt(2.0 / math.pi)


def _gelu(x):
    return 0.5 * x * (1.0 + jnp.tanh(_GELU_C * (x + 0.044715 * x * x * x)))


def _dgelu(x):
    t = jnp.tanh(_GELU_C * (x + 0.044715 * x * x * x))
    return 0.5 * (1.0 + t) + 0.5 * x * (1.0 - t * t) * _GELU_C * (1.0 + 3.0 * 0.044715 * x * x)


def _rms_fwd(x, g):
    r = lax.rsqrt(jnp.mean(x * x, axis=-1, keepdims=True) + NORM_EPS)
    n = x * r
    return n * g, n, r


def _rms_bwd(dy, n, r, g):
    dn = dy * g
    dx = r * (dn - n * jnp.mean(dn * n, axis=-1, keepdims=True))
    return dx, jnp.sum(dy * n, axis=0, keepdims=True)


def _head_rms_fwd(o, g):
    ns, rs = [], []
    for h in range(HG_HEADS):
        oh = o[:, h * HG_DIM:(h + 1) * HG_DIM]
        r = lax.rsqrt(jnp.mean(oh * oh, axis=-1, keepdims=True) + NORM_EPS)
        ns.append(oh * r)
        rs.append(r)
    n = jnp.concatenate(ns, axis=1)
    return n * g, n, rs


def _head_rms_bwd(dy, n, rs, g):
    dn = dy * g
    dxs = []
    for h in range(HG_HEADS):
        sl = slice(h * HG_DIM, (h + 1) * HG_DIM)
        dxs.append(rs[h] * (dn[:, sl] - n[:, sl] * jnp.mean(dn[:, sl] * n[:, sl], axis=-1, keepdims=True)))
    return jnp.concatenate(dxs, axis=1), jnp.sum(dy * n, axis=0, keepdims=True)


def _rowwise(name, fn, n_rows, tm, rows, consts, out_rows, out_accs, vmem_mb, parts=1):
    n_r, n_c, n_or, n_oa = len(rows), len(consts), len(out_rows), len(out_accs)
    tp = tm // parts

    def body(*refs):
        r_refs = refs[:n_r]
        c_refs = refs[n_r:n_r + n_c]
        or_refs = refs[n_r + n_c:n_r + n_c + n_or]
        oa_refs = refs[n_r + n_c + n_or:]

        if n_oa:
            @pl.when(pl.program_id(0) == 0)
            def _():
                for ref in oa_refs:
                    ref[...] = jnp.zeros(ref.shape, ref.dtype)

        for part in range(parts):
            sl = slice(part * tp, (part + 1) * tp)
            outs, accs = fn([r[sl, :] for r in r_refs], c_refs, [o.at[sl, :] for o in or_refs])
            for ref, v in zip(or_refs, outs):
                if v is not None:
                    ref[sl, :] = v.astype(ref.dtype)
            for ref, v in zip(oa_refs, accs):
                ref[...] += v.astype(ref.dtype)

    in_specs = [pl.BlockSpec((tm, w), functools.partial(lambda i, c: (i, c), c=cb)) for (_, w, cb) in rows]
    in_specs += [pl.BlockSpec(c.shape, functools.partial(lambda i, nd: (0,) * nd, nd=c.ndim),
                              pipeline_mode=pl.Buffered(1)) for c in consts]
    out_specs = [pl.BlockSpec((tm, w), lambda i: (i, 0)) for (w, _) in out_rows]
    out_specs += [pl.BlockSpec(s, functools.partial(lambda i, nd: (0,) * nd, nd=len(s))) for (s, _) in out_accs]
    out_shape = [jax.ShapeDtypeStruct((n_rows, w), dt) for (w, dt) in out_rows]
    out_shape += [jax.ShapeDtypeStruct(s, dt) for (s, dt) in out_accs]
    res = _call(
        body, name=name, grid=(n_rows // tm,), in_specs=in_specs, out_specs=out_specs, out_shape=out_shape,
        compiler_params=_params(vmem_mb),
    )(*[a for (a, _, _) in rows], *consts)
    return res[:n_or], res[n_or:]


def _stage_branches(o, proj, ys, x, hg_norm_g, w_o_hg, w_glu, b_glu, w_o_s5, w_out):
    S = x.shape[0]

    def fn(rv, cr, out):
        o_b, g_hg, z_s, gate_hg, gate_s5, ys_b, x_b = rv
        gn_ref, wohg_ref, wglu_ref, bglu_ref, wos5_ref, wout_ref = cr
        on, _, _ = _head_rms_fwd(o_b, gn_ref[...])
        a = on * _silu(g_hg)
        y_hg = jnp.dot(a.astype(BF16), wohg_ref[...], preferred_element_type=F32)
        gl = _gelu(ys_b)
        glu = jnp.dot(gl.astype(BF16), wglu_ref[...], preferred_element_type=F32) + bglu_ref[...]
        ys2 = glu[:, :S5_WIDTH] * _sigmoid(glu[:, S5_WIDTH:]) * _silu(z_s)
        y_s5 = jnp.dot(ys2.astype(BF16), wos5_ref[...], preferred_element_type=F32)
        merged = _sigmoid(gate_hg) * y_hg + _sigmoid(gate_s5) * y_s5
        h1 = x_b + jnp.dot(merged.astype(BF16), wout_ref[...], preferred_element_type=F32)
        return [y_hg, y_s5, glu, h1], []

    rows = [(o, D_MODEL, 0), (proj, D_MODEL, COL_G // D_MODEL), (proj, S5_WIDTH, COL_ZS // S5_WIDTH),
            (proj, D_MODEL, COL_GH // D_MODEL), (proj, D_MODEL, COL_GS // D_MODEL), (ys, S5_WIDTH, 0), (x, D_MODEL, 0)]
    (y_hg, y_s5, glu, h1), _ = _rowwise(
        "branches", fn, S, 256, rows, [hg_norm_g, w_o_hg, w_glu, b_glu, w_o_s5, w_out],
        [(D_MODEL, F32)] * 4, [], 56)
    return y_hg, y_s5, glu, h1


def _stage_ple_loss(h1, p, target, ple_norm_g, w_ple, w_ple_gate, final_norm_g):
    S = h1.shape[0]

    def fn(rv, cr, out):
        h1_b, p_b, t_b = rv
        gp_ref, wple_ref, wpg_ref, gf_ref = cr
        n2g, n2, r2 = _rms_fwd(h1_b, gp_ref[...])
        z = jnp.dot(n2g.astype(BF16), wpg_ref[...], preferred_element_type=F32)
        gate = _sigmoid(z)
        pe = jnp.dot(p_b.astype(BF16), wple_ref[...], preferred_element_type=F32)
        h2 = h1_b + pe * gate
        y, nf, rf = _rms_fwd(h2, gf_ref[...])
        err = y - t_b
        loss_rows = 0.5 * jnp.mean(err * err, axis=-1, keepdims=True)
        loss_inc = jnp.broadcast_to(jnp.sum(loss_rows, axis=0, keepdims=True), (1, 128))
        dy = err * (1.0 / D_MODEL)
        dh2, d_gf = _rms_bwd(dy, nf, rf, gf_ref[...])
        d_pe = dh2 * gate
        dz = dh2 * pe * gate * (1.0 - gate)
        d_wple = _bdot_tn(p_b, d_pe)
        d_wpg = _bdot_tn(n2g, dz)
        dn2g = _bdot_nt(dz, wpg_ref[...])
        dh1n, d_gp = _rms_bwd(dn2g, n2, r2, gp_ref[...])
        return [dh2 + dh1n], [loss_inc, d_gf, d_gp, d_wple, d_wpg]

    (dh1,), accs = _rowwise(
        "ple_loss", fn, S, 512, [(h1, D_MODEL, 0), (p, 256, 0), (target, D_MODEL, 0)],
        [ple_norm_g, w_ple, w_ple_gate, final_norm_g], [(D_MODEL, F32)],
        [((1, 128), F32), ((1, D_MODEL), F32), ((1, D_MODEL), F32), ((256, D_MODEL), F32), ((D_MODEL, D_MODEL), F32)], 56,
        parts=2)
    return dh1, accs


def _stage_bwd_merge(dh1, y_hg, y_s5, proj, w_out):
    S = dh1.shape[0]

    def fn(rv, cr, out):
        dh1_b, yhg, ys5, gate_hg, gate_s5 = rv
        (wout_ref,) = cr
        sg_h, sg_s = _sigmoid(gate_hg), _sigmoid(gate_s5)
        merged = sg_h * yhg + sg_s * ys5
        d_wout = _bdot_tn(merged, dh1_b)
        d_merged = _bdot_nt(dh1_b, wout_ref[...])
        d_gate_hg = d_merged * yhg * sg_h * (1.0 - sg_h)
        d_gate_s5 = d_merged * ys5 * sg_s * (1.0 - sg_s)
        return [d_gate_hg, d_gate_s5, d_merged * sg_h, d_merged * sg_s], [d_wout]

    rows = [(dh1, D_MODEL, 0), (y_hg, D_MODEL, 0), (y_s5, D_MODEL, 0), (proj, D_MODEL, COL_GH // D_MODEL),
            (proj, D_MODEL, COL_GS // D_MODEL)]
    outs, accs = _rowwise("bwd_merge", fn, S, 512, rows, [w_out], [(D_MODEL, BF16)] * 4,
                          [((D_MODEL, D_MODEL), F32)], 56, parts=2)
    return outs, accs


def _stage_bwd_hg_path(d_yhg, o, proj, hg_norm_g, w_o_hg):
    S = o.shape[0]

    def fn(rv, cr, out):
        d_yhg_b, o_b, g_hg = rv
        gn_ref, wohg_ref = cr
        ong, on, rs = _head_rms_fwd(o_b, gn_ref[...])
        sil = _silu(g_hg)
        d_wohg = _bdot_tn(ong * sil, d_yhg_b)
        d_a = _bdot_nt(d_yhg_b, wohg_ref[...])
        d_g_hg = d_a * ong * _dsilu(g_hg)
        d_o, d_gn = _head_rms_bwd(d_a * sil, on, rs, gn_ref[...])
        return [d_o, d_g_hg], [d_wohg, d_gn]

    rows = [(d_yhg, D_MODEL, 0), (o, D_MODEL, 0), (proj, D_MODEL, COL_G // D_MODEL)]
    outs, accs = _rowwise("bwd_hg_path", fn, S, 512, rows, [hg_norm_g, w_o_hg], [(D_MODEL, BF16)] * 2,
                          [((D_MODEL, D_MODEL), F32), ((1, D_MODEL), F32)], 56, parts=2)
    return outs, accs


def _stage_bwd_s5_path(d_ys5, ys, glu, proj, w_o_s5, w_glu):
    S = ys.shape[0]

    def fn(rv, cr, out):
        d_ys5_b, ys_b, glu_b, z_s = rv
        wos5_ref, wglu_ref = cr
        ga, gb = glu_b[:, :S5_WIDTH], glu_b[:, S5_WIDTH:]
        sgb, silz = _sigmoid(gb), _silu(z_s)
        ys2 = ga * sgb * silz
        d_wos5 = _bdot_tn(ys2, d_ys5_b)
        d_ys2 = _bdot_nt(d_ys5_b, wos5_ref[...])
        d_ga = d_ys2 * sgb * silz
        d_gb = d_ys2 * ga * sgb * (1.0 - sgb) * silz
        d_z = d_ys2 * ga * sgb * _dsilu(z_s)
        d_glu = jnp.concatenate([d_ga, d_gb], axis=1)
        gl = _gelu(ys_b)
        d_wglu = _bdot_tn(gl, d_glu)
        d_bglu = jnp.sum(d_glu, axis=0, keepdims=True)
        d_gl = _bdot_nt(d_glu, wglu_ref[...])
        return [d_gl * _dgelu(ys_b), d_z], [d_wos5, d_wglu, d_bglu]

    rows = [(d_ys5, D_MODEL, 0), (ys, S5_WIDTH, 0), (glu, D_MODEL, 0), (proj, S5_WIDTH, COL_ZS // S5_WIDTH)]
    outs, accs = _rowwise(
        "bwd_s5_path", fn, S, 512, rows, [w_o_s5, w_glu], [(S5_WIDTH, F32), (S5_WIDTH, BF16)],
        [((S5_WIDTH, D_MODEL), F32), ((S5_WIDTH, D_MODEL), F32), ((1, D_MODEL), F32)], 48, parts=2)
    return outs, accs


def _stage_inproj_bwd(pieces, x, dh1, norm_g, w_in_all):
    S = x.shape[0]

    def fn(rv, cr, out):
        g_ref, w_ref = cr
        x_b, dh1_b = rv[8], rv[9]
        dproj_ref = out[1]
        col = 0
        for v in rv[:8]:
            dproj_ref[:, col:col + v.shape[1]] = v.astype(BF16)
            col += v.shape[1]
        d_u = jnp.zeros((x_b.shape[0], D_MODEL), F32)
        for j in range(N_DEV):
            d_u = d_u + lax.dot_general(dproj_ref[:, j * SHARD_IN:(j + 1) * SHARD_IN], w_ref[j],
                                        (((1,), (1,)), ((), ())), preferred_element_type=F32)
        _, n, r = _rms_fwd(x_b, g_ref[...])
        dx, d_g = _rms_bwd(d_u, n, r, g_ref[...])
        return [dh1_b + dx, None], [d_g]

    rows = [(a, a.shape[1], 0) for a in pieces] + [(x, D_MODEL, 0), (dh1, D_MODEL, 0)]
    (grad_x, dproj), (d_g,) = _rowwise(
        "inproj_bwd", fn, S, 256, rows, [norm_g, w_in_all], [(D_MODEL, F32), (IN_COLS, BF16)],
        [((1, D_MODEL), F32)], 56)
    return grad_x, dproj, d_g


def _chunk_row(shape):
    return lax.broadcasted_iota(jnp.int32, shape, 0) & (HG_CHUNK - 1)


def _chunk_cumsum(x):
    r_in = _chunk_row(x.shape)
    s = 1
    while s < HG_CHUNK:
        x = x + jnp.where(r_in >= s, pltpu.roll(x, s, 0), 0.0)
        s *= 2
    return x


def _chunk_suffix_sum(x):
    n = x.shape[0]
    r_in = _chunk_row(x.shape)
    s = 1
    while s < HG_CHUNK:
        x = x + jnp.where(r_in < HG_CHUNK - s, pltpu.roll(x, n - s, 0), 0.0)
        s *= 2
    return x


def _hgrn_prep(q, fl, lb):
    sup = q.shape[0]
    nc = sup // HG_CHUNK
    sig = _sigmoid(fl)
    f = lb + (1.0 - lb) * sig
    k = (1.0 - lb) * (1.0 - sig)
    b = _chunk_cumsum(jnp.log(f))
    b3 = b.reshape(nc, HG_CHUNK, HG_DIM)
    row3 = lax.broadcasted_iota(jnp.int32, b3.shape, 1)
    pick = lambda r: jnp.sum(jnp.where(row3 == r, b3, 0.0), axis=1, keepdims=True)
    b_mid = pick(HG_CHUNK // 2 - 1)
    b_last = pick(HG_CHUNK - 1)
    flat = lambda t: t.reshape(sup, HG_DIM)
    e_qa = flat(jnp.exp(b3 - b_mid))
    e_ka = flat(jnp.exp(b_mid - b3))
    e_qd = jnp.exp(b)
    e_kd = flat(jnp.exp(b_last - b3))
    dc = jnp.exp(b_last)
    return sig, f, k, e_qa, e_ka, e_qd, e_kd, dc


def _hgrn_mask(sup):
    r = lax.broadcasted_iota(jnp.int32, (sup, sup), 0)
    c = lax.broadcasted_iota(jnp.int32, (sup, sup), 1)
    shift = HG_CHUNK.bit_length() - 1
    return (jnp.right_shift(r, shift) == jnp.right_shift(c, shift)) & (r >= c)


def _hgrn_fwd(proj, lb):
    S = proj.shape[0]
    sup = HG_SUPER_FWD
    nb = S // sup
    nc = sup // HG_CHUNK
    hp = HG_HEADS_PER_STEP
    wide = hp * HG_DIM

    def body(q_ref, f_ref, iv_ref, lb_ref, o_ref, st_ref, state):
        @pl.when(pl.program_id(1) == 0)
        def _():
            state[...] = jnp.zeros(state.shape, F32)

        mask = _hgrn_mask(sup)
        for hh in range(hp):
            lanes = slice(hh * HG_DIM, (hh + 1) * HG_DIM)
            q, iv = q_ref[:, lanes], iv_ref[:, lanes]
            _, _, k, e_qa, e_ka, e_qd, e_kd, dc = _hgrn_prep(q, f_ref[:, lanes], lb_ref[:, lanes])
            scores = jnp.where(mask, _bdot_nt(q * e_qa, k * e_ka), 0.0)
            o_intra = _bdot(scores, iv)
            qd, kd = q * e_qd, k * e_kd
            for c in range(nc):
                sl = slice(c * HG_CHUNK, (c + 1) * HG_CHUNK)
                st = state[hh]
                st_ref[hh, c] = st
                o_ref[sl, lanes] = o_intra[sl] + _bdot_nt(qd[sl], st)
                state[hh] = dc[c] * st + _bdot_tn(iv[sl], kd[sl])

    blk = lambda base: pl.BlockSpec((sup, wide), functools.partial(lambda h, i, b: (i, b + h), b=base // wide))
    return _call(
        body, name="hgrn_fwd", grid=(HG_HEADS // hp, nb),
        in_specs=[blk(COL_Q), blk(COL_F), blk(COL_I), pl.BlockSpec((1, wide), lambda h, i: (0, h))],
        out_specs=[pl.BlockSpec((sup, wide), lambda h, i: (i, h)),
                   pl.BlockSpec((hp, nc, HG_DIM, HG_DIM), lambda h, i: (h, i, 0, 0))],
        out_shape=[jax.ShapeDtypeStruct((S, D_MODEL), F32),
                   jax.ShapeDtypeStruct((HG_HEADS, S // HG_CHUNK, HG_DIM, HG_DIM), F32)],
        scratch_shapes=[pltpu.VMEM((hp, HG_DIM, HG_DIM), F32)],
        compiler_params=_params(40, 2),
    )(proj, proj, proj, lb)


def _hgrn_bwd(proj, lb, d_o, states):
    S = proj.shape[0]
    sup = HG_SUPER_BWD
    nb = S // sup
    nc = sup // HG_CHUNK
    hp = HG_HEADS_PER_STEP
    wide = hp * HG_DIM

    def body(q_ref, f_ref, iv_ref, lb_ref, do_ref, st_ref, dq_ref, df_ref, div_ref, dlb_ref, dstate):
        @pl.when(pl.program_id(1) == 0)
        def _():
            dstate[...] = jnp.zeros(dstate.shape, F32)
            dlb_ref[...] = jnp.zeros(dlb_ref.shape, F32)

        mask = _hgrn_mask(sup)
        for hh in range(hp):
            lanes = slice(hh * HG_DIM, (hh + 1) * HG_DIM)
            q, iv, do, lb_v = q_ref[:, lanes], iv_ref[:, lanes], do_ref[:, lanes], lb_ref[:, lanes]
            sig, f, k, e_qa, e_ka, e_qd, e_kd, dc = _hgrn_prep(q, f_ref[:, lanes], lb_v)
            qa, ka, qd, kd = q * e_qa, k * e_ka, q * e_qd, k * e_kd
            scores = jnp.where(mask, _bdot_nt(qa, ka), 0.0)
            d_scores = jnp.where(mask, _bdot_nt(do, iv), 0.0)
            d_iv_intra = _bdot_tn(scores, do)
            d_qa = _bdot(d_scores, ka)
            d_ka = _bdot_tn(d_scores, qa)
            d_qd, d_kd, d_last = [None] * nc, [None] * nc, [None] * nc
            for c in reversed(range(nc)):
                sl = slice(c * HG_CHUNK, (c + 1) * HG_CHUNK)
                st = st_ref[hh, c]
                ds = dstate[hh]
                d_qd[c] = _bdot(do[sl], st)
                d_kd[c] = _bdot(iv[sl], ds)
                div_ref[sl, lanes] = (d_iv_intra[sl] + _bdot_nt(kd[sl], ds)).astype(div_ref.dtype)
                d_last[c] = (jnp.sum(ds * st, axis=0, keepdims=True) * dc[c]
                             + jnp.sum(d_kd[c] * kd[sl], axis=0, keepdims=True))
                dstate[hh] = dc[c] * ds + _bdot_tn(do[sl], qd[sl])
            d_qd = jnp.concatenate(d_qd, axis=0)
            d_kd = jnp.concatenate(d_kd, axis=0)
            d_b = d_qa * qa - d_ka * ka + d_qd * qd - d_kd * kd
            last_rows = jnp.concatenate([jnp.broadcast_to(t, (HG_CHUNK, HG_DIM)) for t in d_last], axis=0)
            d_b = d_b + jnp.where(_chunk_row(d_b.shape) == HG_CHUNK - 1, last_rows, 0.0)
            d_logf = _chunk_suffix_sum(d_b)
            d_k = d_ka * e_ka + d_kd * e_kd
            g_f = d_logf / f
            d_sig = (g_f - d_k) * (1.0 - lb_v)
            dq_ref[:, lanes] = (d_qa * e_qa + d_qd * e_qd).astype(dq_ref.dtype)
            df_ref[:, lanes] = (d_sig * sig * (1.0 - sig)).astype(df_ref.dtype)
            d_lb = jnp.sum((g_f - d_k) * (1.0 - sig), axis=0, keepdims=True)
            dlb_ref[:, lanes] += jnp.broadcast_to(d_lb, (8, HG_DIM))

    rev = lambda i: nb - 1 - i
    blk = lambda base: pl.BlockSpec((sup, wide), functools.partial(lambda h, i, b: (rev(i), b + h), b=base // wide))
    row_out = pl.BlockSpec((sup, wide), lambda h, i: (rev(i), h))
    dq, df, div, dlb = _call(
        body, name="hgrn_bwd", grid=(HG_HEADS // hp, nb),
        in_specs=[blk(COL_Q), blk(COL_F), blk(COL_I), pl.BlockSpec((1, wide), lambda h, i: (0, h)),
                  pl.BlockSpec((sup, wide), lambda h, i: (rev(i), h)),
                  pl.BlockSpec((hp, nc, HG_DIM, HG_DIM), lambda h, i: (h, rev(i), 0, 0))],
        out_specs=[row_out, row_out, row_out, pl.BlockSpec((8, wide), lambda h, i: (0, h))],
        out_shape=[jax.ShapeDtypeStruct((S, D_MODEL), BF16)] * 3 + [jax.ShapeDtypeStruct((8, D_MODEL), F32)],
        scratch_shapes=[pltpu.VMEM((hp, HG_DIM, HG_DIM), F32)],
        compiler_params=_params(40, 2),
    )(proj, proj, proj, lb, d_o, states)
    return dq, df, div, dlb[0:1]


def _s5_matrices(a_re, a_im, log_dt, b_re, b_im, c_re, c_im, d, seg_len):
    dt = jnp.exp(log_dt)[:, None]
    mag = jnp.exp(a_re * dt)
    lr, li = mag * jnp.cos(a_im * dt), mag * jnp.sin(a_im * dt)
    den = a_re * a_re + a_im * a_im
    nr = lr - 1.0
    sr = (nr * a_re + li * a_im) / den
    si = (li * a_re - nr * a_im) / den
    bbr = sr[..., None] * b_re - si[..., None] * b_im
    bbi = sr[..., None] * b_im + si[..., None] * b_re
    eye = jnp.eye(8, dtype=F32)

    def quad_cols(v):
        return v.reshape(S5_QUADS, 8 * S5_STATE)

    def lam_row(re_part, im_part):
        row = jnp.concatenate([quad_cols(re_part), quad_cols(im_part)], axis=1).reshape(1, S5_COLS)
        return jnp.broadcast_to(row, (S5_SEG, S5_COLS))

    def b_mat(bb):
        t = bb.reshape(S5_QUADS, 8, S5_STATE, S5_CH)
        return jnp.einsum("qgnc,gh->qgchn", t, eye).reshape(S5_QUADS, 8 * S5_CH, 8 * S5_STATE)

    def c_mat(cc):
        t = cc.reshape(S5_QUADS, 8, S5_CH, S5_STATE)
        return jnp.einsum("qgcn,gh->qgnhc", t, eye).reshape(S5_QUADS, 8 * S5_STATE, 8 * S5_CH)

    ang = a_im * dt * seg_len
    magp = jnp.exp(a_re * dt * seg_len)
    lpr, lpi = magp * jnp.cos(ang), magp * jnp.sin(ang)
    return dict(
        lam_r=lam_row(lr, lr), lam_i=lam_row(-li, li),
        b_q=jnp.concatenate([b_mat(bbr), b_mat(bbi)], axis=2),
        c_q=jnp.concatenate([c_mat(c_re), -c_mat(c_im)], axis=1),
        d_row=d.reshape(1, S5_WIDTH), pow_r=quad_cols(lpr), pow_i=quad_cols(lpi),
    )


def _s5_parts(v):
    half = S5_QCOLS // 2
    return tuple(v[:, k * half:(k + 1) * half] for k in range(2 * S5_QUADS))


def _s5_advance(parts, lr_ref, li_ref, x_ref, sl, conj):
    half = S5_QCOLS // 2
    out = []
    for q in range(S5_QUADS):
        re_c = slice(q * S5_QCOLS, q * S5_QCOLS + half)
        im_c = slice(q * S5_QCOLS + half, (q + 1) * S5_QCOLS)
        lr, li = lr_ref[:, re_c], li_ref[:, im_c]
        hr, hi = parts[2 * q], parts[2 * q + 1]
        if conj:
            out += [lr * hr + li * hi + x_ref[sl, re_c], lr * hi - li * hr + x_ref[sl, im_c]]
        else:
            out += [lr * hr - li * hi + x_ref[sl, re_c], lr * hi + li * hr + x_ref[sl, im_c]]
    return tuple(out)


def _scan_loop(step, init):
    def trip(o, carry):
        for j in range(S5_UNROLL):
            carry = step(o * S5_UNROLL + j, carry)
        return carry

    return lax.fori_loop(0, S5_TILE_STEPS // S5_UNROLL, trip, init)


def _s5_store(ref, sl, parts):
    half = S5_QCOLS // 2
    for k, v in enumerate(parts):
        ref[sl, k * half:(k + 1) * half] = v


def _s5_fwd_pass(u_perm, mats, h0, with_output):
    S = u_perm.shape[0]
    rows = S5_TILE_STEPS * S5_SEG
    nt = S // rows

    def body(*refs):
        if with_output:
            u_ref, b_ref, lr_ref, li_ref, h0_ref, c_ref, d_ref, y_ref, hinit_ref, hend_ref, xs, hcar = refs
        else:
            u_ref, b_ref, lr_ref, li_ref, h0_ref, hend_ref, xs, hcar = refs

        @pl.when(pl.program_id(0) == 0)
        def _():
            hcar[...] = h0_ref[...]

        if with_output:
            hinit_ref[...] = hcar[...]
        u = u_ref[...]
        ub = u.astype(BF16)
        for q in range(S5_QUADS):
            xs[:, q * S5_QCOLS:(q + 1) * S5_QCOLS] = jnp.dot(ub[:, q * 128:(q + 1) * 128], b_ref[q], preferred_element_type=F32)

        def step(t, h):
            sl = pl.ds(pl.multiple_of(t * S5_SEG, S5_SEG), S5_SEG)
            hn = _s5_advance(h, lr_ref, li_ref, xs, sl, False)
            _s5_store(xs, sl, hn)
            return hn

        h = _scan_loop(step, _s5_parts(hcar[...]))
        _s5_store(hcar, slice(None), h)
        _s5_store(hend_ref, slice(None), h)
        if with_output:
            ys = [jnp.dot(xs[:, q * S5_QCOLS:(q + 1) * S5_QCOLS].astype(BF16), c_ref[q], preferred_element_type=F32)
                  for q in range(S5_QUADS)]
            y_ref[...] = jnp.concatenate(ys, axis=1) + d_ref[...] * u

    full = lambda a: pl.BlockSpec(a.shape, functools.partial(lambda i, nd: (0,) * nd, nd=a.ndim))
    ins = [u_perm, mats["b_q"], mats["lam_r"], mats["lam_i"], h0]
    in_specs = [pl.BlockSpec((rows, S5_WIDTH), lambda i: (i, 0))] + [full(a) for a in ins[1:]]
    out_specs = [pl.BlockSpec((S5_SEG, S5_COLS), lambda i: (0, 0))]
    out_shape = [jax.ShapeDtypeStruct((S5_SEG, S5_COLS), F32)]
    if with_output:
        ins += [mats["c_q"], mats["d_row"]]
        in_specs += [full(mats["c_q"]), full(mats["d_row"])]
        out_specs = [pl.BlockSpec((rows, S5_WIDTH), lambda i: (i, 0)),
                     pl.BlockSpec((None, S5_SEG, S5_COLS), lambda i: (i, 0, 0))] + out_specs
        out_shape = [jax.ShapeDtypeStruct((S, S5_WIDTH), F32), jax.ShapeDtypeStruct((nt, S5_SEG, S5_COLS), F32)] + out_shape
    return _call(
        body, name="s5_fwd_y" if with_output else "s5_fwd_ends", grid=(nt,), in_specs=in_specs, out_specs=out_specs,
        out_shape=out_shape,
        scratch_shapes=[pltpu.VMEM((rows, S5_COLS), F32), pltpu.VMEM((S5_SEG, S5_COLS), F32)],
        compiler_params=_params(40),
    )(*ins)


def _s5_bwd_ends(dy_perm, mats):
    S = dy_perm.shape[0]
    rows = S5_TILE_STEPS * S5_SEG
    nt = S // rows

    def body(dy_ref, c_ref, lr_ref, li_ref, gend_ref, gs, gcar):
        @pl.when(pl.program_id(0) == 0)
        def _():
            gcar[...] = jnp.zeros(gcar.shape, F32)

        dyb = dy_ref[...].astype(BF16)
        for q in range(S5_QUADS):
            gs[:, q * S5_QCOLS:(q + 1) * S5_QCOLS] = lax.dot_general(
                dyb[:, q * 128:(q + 1) * 128], c_ref[q], (((1,), (1,)), ((), ())), preferred_element_type=F32)

        def step(k, g):
            t = S5_TILE_STEPS - 1 - k
            sl = pl.ds(pl.multiple_of(t * S5_SEG, S5_SEG), S5_SEG)
            return _s5_advance(g, lr_ref, li_ref, gs, sl, True)

        g = _scan_loop(step, _s5_parts(gcar[...]))
        _s5_store(gcar, slice(None), g)
        _s5_store(gend_ref, slice(None), g)

    full = lambda a: pl.BlockSpec(a.shape, functools.partial(lambda i, nd: (0,) * nd, nd=a.ndim))
    return _call(
        body, name="s5_bwd_ends", grid=(nt,),
        in_specs=[pl.BlockSpec((rows, S5_WIDTH), lambda i: (nt - 1 - i, 0)), full(mats["c_q"]), full(mats["lam_r"]),
                  full(mats["lam_i"])],
        out_specs=pl.BlockSpec((S5_SEG, S5_COLS), lambda i: (0, 0)),
        out_shape=jax.ShapeDtypeStruct((S5_SEG, S5_COLS), F32),
        scratch_shapes=[pltpu.VMEM((rows, S5_COLS), F32), pltpu.VMEM((S5_SEG, S5_COLS), F32)],
        compiler_params=_params(40),
    )(dy_perm, mats["c_q"], mats["lam_r"], mats["lam_i"])


def _s5_bwd_full(u_perm, dy_perm, hinit, g0, mats):
    S = u_perm.shape[0]
    rows = S5_TILE_STEPS * S5_SEG
    nt = S // rows

    def body(u_ref, dy_ref, hinit_ref, g0_ref, b_ref, c_ref, lr_ref, li_ref, d_ref,
             du_ref, dp_ref, dq_ref, db_ref, dc_ref, dd_ref, hs, gs, gcar):
        @pl.when(pl.program_id(0) == 0)
        def _():
            gcar[...] = g0_ref[...]
            for ref in (dp_ref, dq_ref, db_ref, dc_ref, dd_ref):
                ref[...] = jnp.zeros(ref.shape, F32)

        u, dy = u_ref[...], dy_ref[...]
        ub, dyb = u.astype(BF16), dy.astype(BF16)
        hs[0:S5_SEG, :] = hinit_ref[...]
        for q in range(S5_QUADS):
            cols = slice(q * S5_QCOLS, (q + 1) * S5_QCOLS)
            hs[S5_SEG:, cols] = jnp.dot(ub[:, q * 128:(q + 1) * 128], b_ref[q], preferred_element_type=F32)
            gs[:, cols] = lax.dot_general(dyb[:, q * 128:(q + 1) * 128], c_ref[q], (((1,), (1,)), ((), ())),
                                          preferred_element_type=F32)

        def fstep(t, h):
            sl = pl.ds(pl.multiple_of((t + 1) * S5_SEG, S5_SEG), S5_SEG)
            hn = _s5_advance(h, lr_ref, li_ref, hs, sl, False)
            _s5_store(hs, sl, hn)
            return hn

        _scan_loop(fstep, _s5_parts(hinit_ref[...]))

        def bstep(k, g):
            t = S5_TILE_STEPS - 1 - k
            sl = pl.ds(pl.multiple_of(t * S5_SEG, S5_SEG), S5_SEG)
            gn = _s5_advance(g, lr_ref, li_ref, gs, sl, True)
            _s5_store(gs, sl, gn)
            return gn

        _s5_store(gcar, slice(None), _scan_loop(bstep, _s5_parts(gcar[...])))

        half = S5_QCOLS // 2
        dus = []
        for q in range(S5_QUADS):
            cols = slice(q * S5_QCOLS, (q + 1) * S5_QCOLS)

            def astep(t, carry, q=q):
                sl = pl.ds(pl.multiple_of(t * S5_SEG, S5_SEG), S5_SEG)
                g = gs[sl, q * S5_QCOLS:(q + 1) * S5_QCOLS]
                hp = hs[sl, q * S5_QCOLS:(q + 1) * S5_QCOLS]
                hp_sw = jnp.concatenate([hp[:, half:], hp[:, :half]], axis=1)
                return carry[0] + g * hp, carry[1] + g * hp_sw

            zero = jnp.zeros((S5_SEG, S5_QCOLS), F32)
            acc_p, acc_q = _scan_loop(astep, (zero, zero))
            dp_ref[:, cols] += jnp.sum(acc_p, axis=0, keepdims=True)
            dq_ref[:, cols] += jnp.sum(acc_q, axis=0, keepdims=True)
            gq = gs[:, cols].astype(BF16)
            db_ref[q] += lax.dot_general(ub[:, q * 128:(q + 1) * 128], gq, (((0,), (0,)), ((), ())),
                                         preferred_element_type=F32)
            hq = hs[S5_SEG:, cols].astype(BF16)
            dc_ref[q] += lax.dot_general(dyb[:, q * 128:(q + 1) * 128], hq, (((0,), (0,)), ((), ())),
                                         preferred_element_type=F32)
            dus.append(lax.dot_general(gq, b_ref[q], (((1,), (1,)), ((), ())), preferred_element_type=F32))
        du_ref[...] = (jnp.concatenate(dus, axis=1) + d_ref[...] * dy).astype(du_ref.dtype)
        dd_ref[...] += jnp.sum(dy * u, axis=0, keepdims=True)

    full = lambda a: pl.BlockSpec(a.shape, functools.partial(lambda i, nd: (0,) * nd, nd=a.ndim))
    rev_rows = pl.BlockSpec((rows, S5_WIDTH), lambda i: (nt - 1 - i, 0))
    consts = [mats["b_q"], mats["c_q"], mats["lam_r"], mats["lam_i"], mats["d_row"]]
    acc = lambda s: pl.BlockSpec(s, functools.partial(lambda i, nd: (0,) * nd, nd=len(s)))
    acc_shapes = [(1, S5_COLS), (1, S5_COLS), (S5_QUADS, 128, S5_QCOLS), (S5_QUADS, 128, S5_QCOLS), (1, S5_WIDTH)]
    return _call(
        body, name="s5_bwd_full", grid=(nt,),
        in_specs=[rev_rows, rev_rows, pl.BlockSpec((None, S5_SEG, S5_COLS), lambda i: (nt - 1 - i, 0, 0)), full(g0)]
        + [full(a) for a in consts],
        out_specs=[rev_rows] + [acc(s) for s in acc_shapes],
        out_shape=[jax.ShapeDtypeStruct((S, S5_WIDTH), BF16)] + [jax.ShapeDtypeStruct(s, F32) for s in acc_shapes],
        scratch_shapes=[pltpu.VMEM((rows + S5_SEG, S5_COLS), F32), pltpu.VMEM((rows, S5_COLS), F32),
                        pltpu.VMEM((S5_SEG, S5_COLS), F32)],
        compiler_params=_params(56),
    )(u_perm, dy_perm, hinit, g0, *consts)


def _cmul(ar, ai, br, bi):
    return ar * br - ai * bi, ar * bi + ai * br


def _split_cols(v):
    t = v.reshape(v.shape[0], S5_QUADS, 2, S5_QCOLS // 2)
    return t[:, :, 0], t[:, :, 1]


def _join_cols(re, im):
    return jnp.stack([re, im], axis=2).reshape(re.shape[0], S5_COLS)


def _segment_starts(ends, pow_r, pow_i, reverse):
    er, ei = _split_cols(ends)
    pi = -pow_i if reverse else pow_i
    order = list(range(S5_SEG))
    if reverse:
        order = order[::-1]
    cr, ci = jnp.zeros_like(er[0]), jnp.zeros_like(ei[0])
    out_r, out_i = [None] * S5_SEG, [None] * S5_SEG
    for j in order:
        out_r[j], out_i[j] = cr, ci
        mr, mi = _cmul(pow_r, pi, cr, ci)
        cr, ci = mr + er[j], mi + ei[j]
    return _join_cols(jnp.stack(out_r), jnp.stack(out_i))


def _to_segments(a):
    S, w = a.shape
    return a.reshape(S5_SEG, S // S5_SEG, w).transpose(1, 0, 2).reshape(S, w)


def _from_segments(a):
    S, w = a.shape
    return a.reshape(S // S5_SEG, S5_SEG, w).transpose(1, 0, 2).reshape(S, w)


def _my_pos():
    return lax.axis_index("x"), lax.axis_index("y"), lax.axis_index("c")


def _flip(pos, k):
    x, y, c = pos
    return (1 - x if k & 4 else x, 1 - y if k & 2 else y, 1 - c if k & 1 else c)


def _index_of(pos):
    return 4 * pos[0] + 2 * pos[1] + pos[2]


_GATHER_FLIPS = (0, 1, 4, 5, 2, 3, 6, 7)


def _inproj_gather(x, norm_g, pack_a, pack_b, pack_c):
    S = x.shape[0]
    tm = min(S, 1024)
    n_i = S // tm
    order = jnp.stack([_index_of(_flip(_my_pos(), k)) for k in _GATHER_FLIPS]).astype(jnp.int32)

    def body(order_ref, x_ref, g_ref, pa_ref, pb_ref, pc_ref, ut_ref, proj_ref, oa_ref, ob_ref, oc_ref,
             wv, u_scr, send_sems, recv_sems, local_sems):
        s, i = pl.program_id(0), pl.program_id(1)
        me = _my_pos()
        mine = _index_of(me)
        sibling = _flip(me, 1)
        srcs = (pb_ref, pa_ref, pc_ref)
        dsts = (wv, oa_ref, oc_ref)

        def direct(a, k):
            return pltpu.make_async_remote_copy(
                src_ref=srcs[a], dst_ref=dsts[a].at[mine], send_sem=send_sems.at[a * 8 + k],
                recv_sem=recv_sems.at[a * 8 + k], device_id=_flip(me, k), device_id_type=MESH)

        def passed_on(a, k):
            slot = _index_of(_flip(me, k))
            return pltpu.make_async_remote_copy(
                src_ref=dsts[a].at[slot], dst_ref=dsts[a].at[slot], send_sem=send_sems.at[a * 8 + (k | 1)],
                recv_sem=recv_sems.at[a * 8 + (k | 1)], device_id=sibling, device_id_type=MESH)

        def arrival(a, k):
            slot = _index_of(_flip(me, k))
            pltpu.make_async_remote_copy(
                src_ref=dsts[a].at[slot], dst_ref=dsts[a].at[slot], send_sem=send_sems.at[a * 8 + k],
                recv_sem=recv_sems.at[a * 8 + k], device_id=me, device_id_type=MESH).wait_recv()

        def own_copy(a):
            return pltpu.make_async_copy(srcs[a], dsts[a].at[mine], local_sems.at[a])

        def keep(idx):
            slot = _index_of(_flip(me, _GATHER_FLIPS[idx]))
            return pltpu.make_async_copy(wv.at[slot], ob_ref.at[slot], local_sems.at[3 + idx])


        first = (s == 0) & (i == 0)

        @pl.when(first)
        def _():
            for a in range(3):
                own_copy(a).start()
            for k in (1, 4, 2):
                direct(0, k).start()
            own_copy(0).wait()
            keep(0).start()

        for idx, k in enumerate(_GATHER_FLIPS):
            if idx == 0:
                continue

            @pl.when((s == idx) & (i == 0))
            def _(idx=idx, k=k):
                arrival(0, k)
                if k in (4, 2, 6):
                    passed_on(0, k).start()
                keep(idx).start()
                if idx == 1:
                    direct(0, 6).start()
                if idx == 2:
                    for a in (1, 2):
                        for k in (1, 4, 2, 6):
                            direct(a, k).start()

        @pl.when(s == 0)
        def _():
            y, _, _ = _rms_fwd(x_ref[...], g_ref[...])
            u_scr[pl.ds(pl.multiple_of(i * tm, tm), tm), :] = y.astype(BF16)
            ut_ref[...] = y.T.astype(BF16)

        ub = u_scr[pl.ds(pl.multiple_of(i * tm, tm), tm), :]
        proj_ref[...] = jnp.dot(ub, wv[order_ref[s]], preferred_element_type=F32)

        @pl.when((s == N_DEV - 1) & (i == n_i - 1))
        def _():
            for a in (1, 2):
                for k in (4, 2, 6):
                    arrival(a, k)
                    passed_on(a, k).start()
            for a in (1, 2):
                for k in (1, 5, 3, 7):
                    arrival(a, k)
                own_copy(a).wait()
            for a in range(3):
                for k in (1, 4, 2, 6):
                    direct(a, k).wait_send()
                for k in (4, 2, 6):
                    passed_on(a, k).wait_send()
            for idx in range(N_DEV):
                keep(idx).wait()

    any_spec = pl.BlockSpec(memory_space=pl.ANY)
    vmem = pl.BlockSpec(memory_space=pltpu.VMEM)
    grid_spec = pltpu.PrefetchScalarGridSpec(
        num_scalar_prefetch=1, grid=(N_DEV, n_i),
        in_specs=[pl.BlockSpec((tm, D_MODEL), lambda s, i, o: (jnp.where(s == 0, i, 0), 0)),
                  pl.BlockSpec((1, D_MODEL), lambda s, i, o: (0, 0)), any_spec, vmem, any_spec],
        out_specs=[pl.BlockSpec((D_MODEL, tm), lambda s, i, o: (0, jnp.where(s == 0, i, n_i - 1))),
                   pl.BlockSpec((tm, SHARD_IN), lambda s, i, o: (i, o[s])), any_spec, any_spec, any_spec],
        scratch_shapes=[pltpu.VMEM((N_DEV,) + pack_b.shape, BF16), pltpu.VMEM((S, D_MODEL), BF16),
                        pltpu.SemaphoreType.DMA((24,)), pltpu.SemaphoreType.DMA((24,)), pltpu.SemaphoreType.DMA((3 + N_DEV,))],
    )
    return _call(
        body, name="inproj_gather", grid_spec=grid_spec,
        out_shape=[jax.ShapeDtypeStruct((D_MODEL, S), BF16), jax.ShapeDtypeStruct((S, IN_COLS), F32),
                   jax.ShapeDtypeStruct((N_DEV,) + pack_a.shape, BF16), jax.ShapeDtypeStruct((N_DEV,) + pack_b.shape, BF16),
                   jax.ShapeDtypeStruct((N_DEV,) + pack_c.shape, BF16)],
        compiler_params=_params(56, 2),
    )(order, x, norm_g, pack_a, pack_b, pack_c)


_SCATTER_FLIPS = (7, 6, 5, 4, 3, 2, 1, 0)
_N_CHIPS = 4


def _for_row_chunks(n_rows, chunk, fn):
    def step(c, carry):
        fn(pl.ds(pl.multiple_of(c * chunk, chunk), chunk))
        return carry

    lax.fori_loop(0, n_rows // chunk, step, 0)


def _grad_w_in_scatter(dproj, u_t, rs_a, rs_c, small_partial):
    S = u_t.shape[1]
    tm = min(S, 1024)
    n_i = S // tm
    order = jnp.stack([_index_of(_flip(_my_pos(), k)) for k in _SCATTER_FLIPS]).astype(jnp.int32)
    shapes = ((D_MODEL, SHARD_IN), rs_a.shape[1:], rs_c.shape[1:])
    row_chunk = 128

    def body(order_ref, dp_ref, ut_ref, ra_ref, rc_ref, p_ref, gb_ref, ga_ref, gc_ref, gs_ref, acc, sib_b, d2d_b,
             send_b, ici_b, mine_a, sib_a, ici_a, mine_c, sib_c, ici_c, gath, send_sems, recv_sems, local_sems):
        s, i = pl.program_id(0), pl.program_id(1)
        me = _my_pos()
        sibling = _flip(me, 1)

        my_chip = 2 * me[0] + me[1]

        small_d2d = pltpu.make_async_remote_copy(
            src_ref=p_ref, dst_ref=gath.at[_N_CHIPS], send_sem=send_sems.at[21], recv_sem=recv_sems.at[21],
            device_id=sibling, device_id_type=MESH)

        def small_ici(m):
            return pltpu.make_async_remote_copy(
                src_ref=gath.at[my_chip], dst_ref=gath.at[my_chip], send_sem=send_sems.at[22 + m],
                recv_sem=recv_sems.at[22 + m], device_id=_flip(me, 6 - 2 * m), device_id_type=MESH)
        sib = (sib_b, sib_a, sib_c)
        ici = (ici_b, ici_a, ici_c)
        outs = (gb_ref, ga_ref, gc_ref)

        def to_sibling(arr, m, src):
            return pltpu.make_async_remote_copy(
                src_ref=src, dst_ref=sib[arr].at[m], send_sem=send_sems.at[arr * 7 + m],
                recv_sem=recv_sems.at[arr * 7 + m], device_id=sibling, device_id_type=MESH)

        def over_ici(arr, m, src):
            return pltpu.make_async_remote_copy(
                src_ref=src, dst_ref=ici[arr].at[m], send_sem=send_sems.at[arr * 7 + 4 + m],
                recv_sem=recv_sems.at[arr * 7 + 4 + m], device_id=_flip(me, 6 - 2 * m), device_id_type=MESH)

        def from_sibling(arr, m):
            to_sibling(arr, m, sib[arr].at[m]).wait_recv()

        def from_ici(arr, m):
            over_ici(arr, m, ici[arr].at[m]).wait_recv()

        small = ((1, ra_ref, mine_a), (2, rc_ref, mine_c))

        def local_copy(arr, src, mine, m):
            return pltpu.make_async_copy(src.at[_index_of(_flip(me, 6 - 2 * m))], mine.at[m],
                                         local_sems.at[(arr - 1) * _N_CHIPS + m])

        @pl.when((s == 0) & (i == 0))
        def _():
            small_d2d.start()
            for arr, src, mine in small:
                for m in range(_N_CHIPS):
                    to_sibling(arr, m, src.at[_index_of(_flip(me, 7 - 2 * m))]).start()
                    local_copy(arr, src, mine, m).start()

        @pl.when((s == 1) & (i == 0))
        def _():
            small_d2d.wait_recv()
            gath[my_chip] = p_ref[...] + gath[_N_CHIPS]
            for m in range(_N_CHIPS - 1):
                small_ici(m).start()
            for arr, src, mine in small:
                rows, chunk = shapes[arr][0], 16
                for m in range(_N_CHIPS):
                    local_copy(arr, src, mine, m).wait()
                    from_sibling(arr, m)
                    if m < _N_CHIPS - 1:
                        def add(sl, arr=arr, mine=mine, m=m):
                            mine[m, sl, :] = (mine[m, sl, :].astype(F32) + sib[arr][m, sl, :].astype(F32)).astype(BF16)

                        _for_row_chunks(rows, chunk, add)
                        over_ici(arr, m, mine.at[m]).start()
                    else:
                        def keep(sl, arr=arr, mine=mine, m=m):
                            outs[arr][sl, :] = mine[m, sl, :].astype(F32) + sib[arr][m, sl, :].astype(F32)

                        _for_row_chunks(rows, chunk, keep)

        @pl.when(i == 0)
        def _():
            acc[...] = jnp.zeros(acc.shape, F32)

        acc[...] += jnp.dot(ut_ref[...], dp_ref[...], preferred_element_type=F32)

        def block_rows(c):
            return acc[c * row_chunk:(c + 1) * row_chunk, :]

        for m in range(_N_CHIPS):
            @pl.when((s == 2 * m) & (i == n_i - 1))
            def _(m=m):
                if m > 0:
                    to_sibling(0, m - 1, d2d_b).wait_send()
                for c in range(D_MODEL // row_chunk):
                    d2d_b[c * row_chunk:(c + 1) * row_chunk, :] = block_rows(c).astype(BF16)
                to_sibling(0, m, d2d_b).start()

            @pl.when((s == 2 * m + 1) & (i == n_i - 1))
            def _(m=m):
                from_sibling(0, m)
                slot = m % 2
                if m == 2:
                    over_ici(0, 0, send_b.at[0]).wait_send()
                for c in range(D_MODEL // row_chunk):
                    rows = slice(c * row_chunk, (c + 1) * row_chunk)
                    total = block_rows(c) + sib_b[m, rows, :].astype(F32)
                    if m < _N_CHIPS - 1:
                        send_b[slot, rows, :] = total.astype(BF16)
                    else:
                        gb_ref[rows, :] = total
                if m < _N_CHIPS - 1:
                    over_ici(0, m, send_b.at[slot]).start()

        @pl.when((s == N_DEV - 1) & (i == n_i - 1))
        def _():
            for arr in range(3):
                for m in range(_N_CHIPS - 1):
                    from_ici(arr, m)
                rows = shapes[arr][0]

                def add(sl, arr=arr):
                    outs[arr][sl, :] = (outs[arr][sl, :] + ici[arr][0, sl, :].astype(F32)
                                        + ici[arr][1, sl, :].astype(F32) + ici[arr][2, sl, :].astype(F32))

                _for_row_chunks(rows, 16, add)
            for m in range(_N_CHIPS - 1):
                small_ici(m).wait_recv()
            gs_ref[...] = (gath[0] + gath[1]) + (gath[2] + gath[3])
            small_d2d.wait_send()
            for m in range(_N_CHIPS - 1):
                small_ici(m).wait_send()
            to_sibling(0, _N_CHIPS - 1, d2d_b).wait_send()
            over_ici(0, 1, send_b.at[1]).wait_send()
            over_ici(0, 2, send_b.at[0]).wait_send()
            for arr, src, mine in small:
                for m in range(_N_CHIPS):
                    to_sibling(arr, m, src.at[0]).wait_send()
                for m in range(_N_CHIPS - 1):
                    over_ici(arr, m, mine.at[m]).wait_send()

    any_spec = pl.BlockSpec(memory_space=pl.ANY)
    vmem = pl.BlockSpec(memory_space=pltpu.VMEM)
    half = lambda shp, n: pltpu.VMEM((n,) + tuple(shp), BF16)
    grid_spec = pltpu.PrefetchScalarGridSpec(
        num_scalar_prefetch=1, grid=(N_DEV, n_i),
        in_specs=[pl.BlockSpec((tm, SHARD_IN), lambda s, i, o: (i, o[s])),
                  pl.BlockSpec((D_MODEL, tm), lambda s, i, o: (0, i)), any_spec, any_spec, vmem],
        out_specs=[vmem, vmem, vmem, vmem],
        scratch_shapes=[
            pltpu.VMEM((D_MODEL, SHARD_IN), F32), half(shapes[0], _N_CHIPS), pltpu.VMEM(shapes[0], BF16),
            half(shapes[0], 2), half(shapes[0], _N_CHIPS - 1),
            half(shapes[1], _N_CHIPS), half(shapes[1], _N_CHIPS), half(shapes[1], _N_CHIPS - 1),
            half(shapes[2], _N_CHIPS), half(shapes[2], _N_CHIPS), half(shapes[2], _N_CHIPS - 1),
            pltpu.VMEM((_N_CHIPS + 1,) + small_partial.shape, F32),
            pltpu.SemaphoreType.DMA((25,)), pltpu.SemaphoreType.DMA((25,)), pltpu.SemaphoreType.DMA((2 * _N_CHIPS,))],
    )
    return _call(
        body, name="grad_w_in_scatter", grid_spec=grid_spec,
        out_shape=[jax.ShapeDtypeStruct(shp, F32) for shp in shapes] + [jax.ShapeDtypeStruct(small_partial.shape, F32)],
        compiler_params=_params(60, 2),
    )(order, dproj, u_t, rs_a, rs_c, small_partial)


def _adam_update(g, w, m, v):
    m2 = ADAM_B1 * m + (1.0 - ADAM_B1) * g
    v2 = ADAM_B2 * v + (1.0 - ADAM_B2) * (g * g)
    m_hat = m2 / (1.0 - ADAM_B1 ** ADAM_STEP)
    v_hat = v2 / (1.0 - ADAM_B2 ** ADAM_STEP)
    delta = -ADAM_LR * (m_hat / (jnp.sqrt(v_hat) + ADAM_EPS) + ADAM_WD * w)
    return delta, m2, v2


def _adam_rows(g, w, m, v):
    rows, cols = w.shape
    tm = rows if rows % 256 else 256

    def fn(rv, cr, out):
        return list(_adam_update(*rv)), []

    outs, _ = _rowwise("adamw", fn, rows, tm, [(a, cols, 0) for a in (g, w, m, v)], [], [(cols, F32)] * 3, [], 32)
    return outs


_SMALL = ["norm_g", "hg_lb", "hg_norm_g", "s5_a_re", "s5_a_im", "s5_log_dt", "s5_b_re", "s5_b_im", "s5_c_re",
          "s5_c_im", "s5_d", "b_glu", "ple_norm_g", "final_norm_g"]
_BIG = ["w_in", "w_o_hg", "w_glu", "w_o_s5", "w_out", "w_ple", "w_ple_gate"]
_ORDER = ["norm_g", "w_in", "hg_lb", "hg_norm_g", "w_o_hg", "s5_a_re", "s5_a_im", "s5_log_dt", "s5_b_re", "s5_b_im",
          "s5_c_re", "s5_c_im", "s5_d", "w_glu", "b_glu", "w_o_s5", "w_out", "ple_norm_g", "w_ple", "w_ple_gate",
          "final_norm_g"]


def _pack_small(vals, tail=None):
    parts = []
    for name in _SMALL:
        flat = vals[name].reshape(-1).astype(F32)
        pad = (-flat.shape[0]) % 1024
        parts.append(jnp.pad(flat, (0, pad)))
    tail = jnp.zeros((0,), F32) if tail is None else tail.reshape(-1).astype(F32)
    parts.append(jnp.pad(tail, (0, 1024 - tail.shape[0])))
    return jnp.concatenate(parts).reshape(-1, 128)


def _unpack_small(packed, like):
    flat = packed.reshape(-1)
    out, off = {}, 0
    for name in _SMALL:
        size = like[name].size
        out[name] = flat[off:off + size].reshape(like[name].shape)
        off += size + (-size) % 1024
    return out


def _col_blocks(full):
    k = full.shape[0]
    return full.reshape(k, N_DEV, 128).transpose(1, 0, 2)


def _from_col_blocks(blocks):
    k = blocks.shape[1]
    return blocks.transpose(1, 0, 2).reshape(k, N_DEV * 128)


def kernel(x, p, norm_g, w_in, hg_lb, hg_norm_g, w_o_hg, s5_a_re, s5_a_im, s5_log_dt, s5_b_re, s5_b_im, s5_c_re, s5_c_im, s5_d, w_glu, b_glu, w_o_s5, w_out, ple_norm_g, w_ple, w_ple_gate, final_norm_g, loss_target, m_norm_g, m_w_in, m_hg_lb, m_hg_norm_g, m_w_o_hg, m_s5_a_re, m_s5_a_im, m_s5_log_dt, m_s5_b_re, m_s5_b_im, m_s5_c_re, m_s5_c_im, m_s5_d, m_w_glu, m_b_glu, m_w_o_s5, m_w_out, m_ple_norm_g, m_w_ple, m_w_ple_gate, m_final_norm_g, v_norm_g, v_w_in, v_hg_lb, v_hg_norm_g, v_w_o_hg, v_s5_a_re, v_s5_a_im, v_s5_log_dt, v_s5_b_re, v_s5_b_im, v_s5_c_re, v_s5_c_im, v_s5_d, v_w_glu, v_b_glu, v_w_o_s5, v_w_out, v_ple_norm_g, v_w_ple, v_w_ple_gate, v_final_norm_g):
    args = dict(locals())
    w = {n: args[n] for n in _ORDER}
    m = {n: args["m_" + n] for n in _ORDER}
    v = {n: args["v_" + n] for n in _ORDER}
    xs = x[0]
    ps = p[0, 0]
    tgt = loss_target[0]
    S = xs.shape[0]

    pack_a = jnp.concatenate([w_o_hg[0], w_out[0], w_ple_gate[0]], axis=0).astype(BF16)
    pack_b = w_in[0].astype(BF16)
    pack_c = jnp.concatenate([w_glu[0], w_o_s5[0], w_ple[0]], axis=0).astype(BF16)
    u_t, proj, all_a, all_b, all_c = _inproj_gather(xs, norm_g, pack_a, pack_b, pack_c)
    wf_o_hg = all_a[:, 0:128].reshape(D_MODEL, D_MODEL)
    wf_out = all_a[:, 128:256].reshape(D_MODEL, D_MODEL)
    wf_pg = all_a[:, 256:384].reshape(D_MODEL, D_MODEL)
    wf_glu = _from_col_blocks(all_c[:, 0:512])
    wf_o_s5 = _from_col_blocks(all_c[:, 512:1024])
    wf_ple = _from_col_blocks(all_c[:, 1024:1280])

    lb = jax.nn.sigmoid(hg_lb[0:1] - hg_lb[1:2])
    s5_names = ["s5_a_re", "s5_a_im", "s5_log_dt", "s5_b_re", "s5_b_im", "s5_c_re", "s5_c_im", "s5_d"]
    build = lambda *a: _s5_matrices(*a, seg_len=S // S5_SEG)
    mats_f32, mats_vjp = jax.vjp(build, *[w[n][0] for n in s5_names])
    mats = dict(mats_f32, b_q=mats_f32["b_q"].astype(BF16), c_q=mats_f32["c_q"].astype(BF16))
    bias_glu = b_glu

    o, states = _hgrn_fwd(proj, lb)
    u_perm = _to_segments(proj[:, COL_US:COL_US + S5_WIDTH])
    zeros_state = jnp.zeros((S5_SEG, S5_COLS), F32)
    (h_ends,) = _s5_fwd_pass(u_perm, mats, zeros_state, False)
    h0 = _segment_starts(h_ends, mats["pow_r"], mats["pow_i"], False)
    y_perm, h_init, _ = _s5_fwd_pass(u_perm, mats, h0, True)
    ys = _from_segments(y_perm)
    y_hg, y_s5, glu, h1 = _stage_branches(o, proj, ys, xs, hg_norm_g, wf_o_hg, wf_glu, bias_glu, wf_o_s5, wf_out)

    dh1, (loss_acc, d_final_g, d_ple_g, d_w_ple, d_w_pg) = _stage_ple_loss(
        h1, ps, tgt, ple_norm_g, wf_ple, wf_pg, final_norm_g.reshape(1, D_MODEL))
    (d_gate_hg, d_gate_s5, d_yhg, d_ys5), (d_w_out,) = _stage_bwd_merge(dh1, y_hg, y_s5, proj, wf_out)
    (d_o, d_g_hg), (d_w_o_hg, d_hg_norm) = _stage_bwd_hg_path(d_yhg, o, proj, hg_norm_g, wf_o_hg)
    (d_ys, d_z), (d_w_o_s5, d_w_glu, d_b_glu) = _stage_bwd_s5_path(d_ys5, ys, glu, proj, wf_o_s5, wf_glu)
    dq, df, div, d_lb = _hgrn_bwd(proj, lb, d_o, states)
    dy_perm = _to_segments(d_ys)
    g_ends = _s5_bwd_ends(dy_perm, mats)
    g0 = _segment_starts(g_ends, mats["pow_r"], mats["pow_i"], True)
    du_perm, acc_p, acc_q, d_bq, d_cq_t, d_d = _s5_bwd_full(u_perm, dy_perm, h_init, g0, mats)
    d_us = _from_segments(du_perm)
    grad_x, dproj, d_norm_g = _stage_inproj_bwd([dq, df, div, d_g_hg, d_us, d_z, d_gate_hg, d_gate_s5], xs, dh1,
                                                norm_g, all_b)

    p_re, p_im = _split_cols(acc_p)
    q_re, q_im = _split_cols(acc_q)
    d_lam_r = (p_re + p_im)[0]
    d_lam_i = (q_im - q_re)[0]
    zero_row = jnp.zeros((S5_SEG, S5_COLS), F32)
    row_of = lambda re_part, im_part: zero_row.at[0].set(_join_cols(re_part[None], im_part[None])[0])
    zeros_q = jnp.zeros_like(d_lam_r)
    cot = dict(
        lam_r=row_of(d_lam_r, zeros_q), lam_i=row_of(zeros_q, d_lam_i),
        b_q=d_bq, c_q=d_cq_t.transpose(0, 2, 1), d_row=d_d,
        pow_r=jnp.zeros_like(mats["pow_r"]), pow_i=jnp.zeros_like(mats["pow_i"]),
    )
    d_s5 = mats_vjp(cot)

    s_lb = lb * (1.0 - lb)
    d_hg_lb = jnp.concatenate([d_lb * s_lb, -d_lb * s_lb], axis=0)
    small_g = dict(norm_g=d_norm_g, hg_lb=d_hg_lb, hg_norm_g=d_hg_norm, b_glu=d_b_glu, ple_norm_g=d_ple_g,
                   final_norm_g=d_final_g)
    for name, g in zip(s5_names, d_s5):
        small_g[name] = g
    pk = lambda d: _pack_small({n: d[n] for n in _SMALL})
    rs_a = jnp.concatenate([d_w_o_hg.reshape(N_DEV, 128, D_MODEL), d_w_out.reshape(N_DEV, 128, D_MODEL),
                            d_w_pg.reshape(N_DEV, 128, D_MODEL)], axis=1).astype(BF16)
    rs_c = jnp.concatenate([_col_blocks(d_w_glu), _col_blocks(d_w_o_s5), _col_blocks(d_w_ple)], axis=1).astype(BF16)
    partial = _pack_small({n: small_g[n] for n in _SMALL}, tail=loss_acc[0, 0:1])
    g_b, g_a, g_c, sg = _grad_w_in_scatter(dproj, u_t, rs_a, rs_c, partial)
    sd, sm, sv = _adam_rows(sg, pk(w), pk(m), pk(v))
    like = {n: w[n] for n in _SMALL}
    out_g, out_d, out_m, out_v = (_unpack_small(t, like) for t in (sg, sd, sm, sv))
    big_g = dict(w_o_hg=g_a[0:128], w_out=g_a[128:256], w_ple_gate=g_a[256:384], w_in=g_b,
                 w_glu=g_c[0:512], w_o_s5=g_c[512:1024], w_ple=g_c[1024:1280])
    for name in _BIG:
        shape = w[name].shape
        g2 = big_g[name]
        d2, m2, v2 = _adam_rows(g2, w[name][0], m[name][0], v[name][0])
        out_g[name], out_d[name], out_m[name], out_v[name] = (t.reshape(shape) for t in (g2, d2, m2, v2))

    loss = sg[sg.shape[0] - 8, 0]
    return (loss, grad_x[None], *[out_g[n] for n in _ORDER], *[out_d[n] for n in _ORDER],
            *[out_m[n] for n in _ORDER], *[out_v[n] for n in _ORDER])
```

```python
import functools
import math

import jax
import jax.numpy as jnp
from jax import lax
from jax.experimental import pallas as pl
from jax.experimental.pallas import tpu as pltpu

F32 = jnp.float32
BF16 = jnp.bfloat16

D_MODEL = 1024
N_DEV = 8
IN_COLS = 7168
SHARD_IN = IN_COLS // N_DEV
HG_HEADS = 8
HG_DIM = 128
HG_CHUNK = 64
HG_SUPER_FWD = 256
HG_SUPER_BWD = 128
HG_HEADS_PER_STEP = 8
S5_WIDTH = 512
S5_GROUPS = 32
S5_STATE = 64
S5_CH = 16
S5_SEG = 8
S5_QUADS = 4
S5_QCOLS = 1024
S5_COLS = S5_QUADS * S5_QCOLS
S5_TILE_STEPS = 64
S5_UNROLL = 8
NORM_EPS = 1e-6
ADAM_LR = 0.001
ADAM_B1 = 0.9
ADAM_B2 = 0.999
ADAM_EPS = 1e-08
ADAM_WD = 0.01
ADAM_STEP = 10
MIB = 1024 * 1024
MESH = pl.DeviceIdType.MESH

COL_Q, COL_F, COL_I, COL_G, COL_US, COL_ZS, COL_GH, COL_GS = 0, 1024, 2048, 3072, 4096, 4608, 5120, 6144


def _call(body, **kw):
    return pl.pallas_call(body, **kw)


def _params(vmem_mb, n_grid=1):
    return pltpu.CompilerParams(
        dimension_semantics=("arbitrary",) * n_grid, vmem_limit_bytes=vmem_mb * MIB
    )


def _bdot(a, b):
    return jnp.dot(a.astype(BF16), b.astype(BF16), preferred_element_type=F32)


def _bdot_nt(a, b):
    return lax.dot_general(a.astype(BF16), b.astype(BF16), (((1,), (1,)), ((), ())), preferred_element_type=F32)


def _bdot_tn(a, b):
    return lax.dot_general(a.astype(BF16), b.astype(BF16), (((0,), (0,)), ((), ())), preferred_element_type=F32)


def _sigmoid(x):
    return jax.nn.sigmoid(x)


def _silu(x):
    return x * _sigmoid(x)


def _dsilu(x):
    s = _sigmoid(x)
    return s * (1.0 + x * (1.0 - s))


_GELU_C = math.sqrt(2.0 / math.pi)


def _gelu(x):
    return 0.5 * x * (1.0 + jnp.tanh(_GELU_C * (x + 0.044715 * x * x * x)))


def _dgelu(x):
    t = jnp.tanh(_GELU_C * (x + 0.044715 * x * x * x))
    return 0.5 * (1.0 + t) + 0.5 * x * (1.0 - t * t) * _GELU_C * (1.0 + 3.0 * 0.044715 * x * x)


def _rms_fwd(x, g):
    r = lax.rsqrt(jnp.mean(x * x, axis=-1, keepdims=True) + NORM_EPS)
    n = x * r
    return n * g, n, r


def _rms_bwd(dy, n, r, g):
    dn = dy * g
    dx = r * (dn - n * jnp.mean(dn * n, axis=-1, keepdims=True))
    return dx, jnp.sum(dy * n, axis=0, keepdims=True)


def _head_rms_fwd(o, g):
    ns, rs = [], []
    for h in range(HG_HEADS):
        oh = o[:, h * HG_DIM:(h + 1) * HG_DIM]
        r = lax.rsqrt(jnp.mean(oh * oh, axis=-1, keepdims=True) + NORM_EPS)
        ns.append(oh * r)
        rs.append(r)
    n = jnp.concatenate(ns, axis=1)
    return n * g, n, rs


def _head_rms_bwd(dy, n, rs, g):
    dn = dy * g
    dxs = []
    for h in range(HG_HEADS):
        sl = slice(h * HG_DIM, (h + 1) * HG_DIM)
        dxs.append(rs[h] * (dn[:, sl] - n[:, sl] * jnp.mean(dn[:, sl] * n[:, sl], axis=-1, keepdims=True)))
    return jnp.concatenate(dxs, axis=1), jnp.sum(dy * n, axis=0, keepdims=True)


def _rowwise(name, fn, n_rows, tm, rows, consts, out_rows, out_accs, vmem_mb, parts=1):
    n_r, n_c, n_or, n_oa = len(rows), len(consts), len(out_rows), len(out_accs)
    tp = tm // parts

    def body(*refs):
        r_refs = refs[:n_r]
        c_refs = refs[n_r:n_r + n_c]
        or_refs = refs[n_r + n_c:n_r + n_c + n_or]
        oa_refs = refs[n_r + n_c + n_or:]

        if n_oa:
            @pl.when(pl.program_id(0) == 0)
            def _():
                for ref in oa_refs:
                    ref[...] = jnp.zeros(ref.shape, ref.dtype)

        for part in range(parts):
            sl = slice(part * tp, (part + 1) * tp)
            outs, accs = fn([r[sl, :] for r in r_refs], c_refs, [o.at[sl, :] for o in or_refs])
            for ref, v in zip(or_refs, outs):
                if v is not None:
                    ref[sl, :] = v.astype(ref.dtype)
            for ref, v in zip(oa_refs, accs):
                ref[...] += v.astype(ref.dtype)

    in_specs = [pl.BlockSpec((tm, w), functools.partial(lambda i, c: (i, c), c=cb)) for (_, w, cb) in rows]
    in_specs += [pl.BlockSpec(c.shape, functools.partial(lambda i, nd: (0,) * nd, nd=c.ndim),
                              pipeline_mode=pl.Buffered(1)) for c in consts]
    out_specs = [pl.BlockSpec((tm, w), lambda i: (i, 0)) for (w, _) in out_rows]
    out_specs += [pl.BlockSpec(s, functools.partial(lambda i, nd: (0,) * nd, nd=len(s))) for (s, _) in out_accs]
    out_shape = [jax.ShapeDtypeStruct((n_rows, w), dt) for (w, dt) in out_rows]
    out_shape += [jax.ShapeDtypeStruct(s, dt) for (s, dt) in out_accs]
    res = _call(
        body, name=name, grid=(n_rows // tm,), in_specs=in_specs, out_specs=out_specs, out_shape=out_shape,
        compiler_params=_params(vmem_mb),
    )(*[a for (a, _, _) in rows], *consts)
    return res[:n_or], res[n_or:]


def _stage_branches(o, proj, ys, x, hg_norm_g, w_o_hg, w_glu, b_glu, w_o_s5, w_out):
    S = x.shape[0]

    def fn(rv, cr, out):
        o_b, g_hg, z_s, gate_hg, gate_s5, ys_b, x_b = rv
        gn_ref, wohg_ref, wglu_ref, bglu_ref, wos5_ref, wout_ref = cr
        on, _, _ = _head_rms_fwd(o_b, gn_ref[...])
        a = on * _silu(g_hg)
        y_hg = jnp.dot(a.astype(BF16), wohg_ref[...], preferred_element_type=F32)
        gl = _gelu(ys_b)
        glu = jnp.dot(gl.astype(BF16), wglu_ref[...], preferred_element_type=F32) + bglu_ref[...]
        ys2 = glu[:, :S5_WIDTH] * _sigmoid(glu[:, S5_WIDTH:]) * _silu(z_s)
        y_s5 = jnp.dot(ys2.astype(BF16), wos5_ref[...], preferred_element_type=F32)
        merged = _sigmoid(gate_hg) * y_hg + _sigmoid(gate_s5) * y_s5
        h1 = x_b + jnp.dot(merged.astype(BF16), wout_ref[...], preferred_element_type=F32)
        return [y_hg, y_s5, glu, h1], []

    rows = [(o, D_MODEL, 0), (proj, D_MODEL, COL_G // D_MODEL), (proj, S5_WIDTH, COL_ZS // S5_WIDTH),
            (proj, D_MODEL, COL_GH // D_MODEL), (proj, D_MODEL, COL_GS // D_MODEL), (ys, S5_WIDTH, 0), (x, D_MODEL, 0)]
    (y_hg, y_s5, glu, h1), _ = _rowwise(
        "branches", fn, S, 256, rows, [hg_norm_g, w_o_hg, w_glu, b_glu, w_o_s5, w_out],
        [(D_MODEL, F32)] * 4, [], 56)
    return y_hg, y_s5, glu, h1


def _stage_ple_loss(h1, p, target, ple_norm_g, w_ple, w_ple_gate, final_norm_g):
    S = h1.shape[0]

    def fn(rv, cr, out):
        h1_b, p_b, t_b = rv
        gp_ref, wple_ref, wpg_ref, gf_ref = cr
        n2g, n2, r2 = _rms_fwd(h1_b, gp_ref[...])
        z = jnp.dot(n2g.astype(BF16), wpg_ref[...], preferred_element_type=F32)
        gate = _sigmoid(z)
        pe = jnp.dot(p_b.astype(BF16), wple_ref[...], preferred_element_type=F32)
        h2 = h1_b + pe * gate
        y, nf, rf = _rms_fwd(h2, gf_ref[...])
        err = y - t_b
        loss_rows = 0.5 * jnp.mean(err * err, axis=-1, keepdims=True)
        loss_inc = jnp.broadcast_to(jnp.sum(loss_rows, axis=0, keepdims=True), (1, 128))
        dy = err * (1.0 / D_MODEL)
        dh2, d_gf = _rms_bwd(dy, nf, rf, gf_ref[...])
        d_pe = dh2 * gate
        dz = dh2 * pe * gate * (1.0 - gate)
        d_wple = _bdot_tn(p_b, d_pe)
        d_wpg = _bdot_tn(n2g, dz)
        dn2g = _bdot_nt(dz, wpg_ref[...])
        dh1n, d_gp = _rms_bwd(dn2g, n2, r2, gp_ref[...])
        return [dh2 + dh1n], [loss_inc, d_gf, d_gp, d_wple, d_wpg]

    (dh1,), accs = _rowwise(
        "ple_loss", fn, S, 512, [(h1, D_MODEL, 0), (p, 256, 0), (target, D_MODEL, 0)],
        [ple_norm_g, w_ple, w_ple_gate, final_norm_g], [(D_MODEL, F32)],
        [((1, 128), F32), ((1, D_MODEL), F32), ((1, D_MODEL), F32), ((256, D_MODEL), F32), ((D_MODEL, D_MODEL), F32)], 56,
        parts=2)
    return dh1, accs


def _stage_bwd_merge(dh1, y_hg, y_s5, proj, w_out):
    S = dh1.shape[0]

    def fn(rv, cr, out):
        dh1_b, yhg, ys5, gate_hg, gate_s5 = rv
        (wout_ref,) = cr
        sg_h, sg_s = _sigmoid(gate_hg), _sigmoid(gate_s5)
        merged = sg_h * yhg + sg_s * ys5
        d_wout = _bdot_tn(merged, dh1_b)
        d_merged = _bdot_nt(dh1_b, wout_ref[...])
        d_gate_hg = d_merged * yhg * sg_h * (1.0 - sg_h)
        d_gate_s5 = d_merged * ys5 * sg_s * (1.0 - sg_s)
        return [d_gate_hg, d_gate_s5, d_merged * sg_h, d_merged * sg_s], [d_wout]

    rows = [(dh1, D_MODEL, 0), (y_hg, D_MODEL, 0), (y_s5, D_MODEL, 0), (proj, D_MODEL, COL_GH // D_MODEL),
            (proj, D_MODEL, COL_GS // D_MODEL)]
    outs, accs = _rowwise("bwd_merge", fn, S, 512, rows, [w_out], [(D_MODEL, BF16)] * 4,
                          [((D_MODEL, D_MODEL), F32)], 56, parts=2)
    return outs, accs


def _stage_bwd_hg_path(d_yhg, o, proj, hg_norm_g, w_o_hg):
    S = o.shape[0]

    def fn(rv, cr, out):
        d_yhg_b, o_b, g_hg = rv
        gn_ref, wohg_ref = cr
        ong, on, rs = _head_rms_fwd(o_b, gn_ref[...])
        sil = _silu(g_hg)
        d_wohg = _bdot_tn(ong * sil, d_yhg_b)
        d_a = _bdot_nt(d_yhg_b, wohg_ref[...])
        d_g_hg = d_a * ong * _dsilu(g_hg)
        d_o, d_gn = _head_rms_bwd(d_a * sil, on, rs, gn_ref[...])
        return [d_o, d_g_hg], [d_wohg, d_gn]

    rows = [(d_yhg, D_MODEL, 0), (o, D_MODEL, 0), (proj, D_MODEL, COL_G // D_MODEL)]
    outs, accs = _rowwise("bwd_hg_path", fn, S, 512, rows, [hg_norm_g, w_o_hg], [(D_MODEL, BF16)] * 2,
                          [((D_MODEL, D_MODEL), F32), ((1, D_MODEL), F32)], 56, parts=2)
    return outs, accs


def _stage_bwd_s5_path(d_ys5, ys, glu, proj, w_o_s5, w_glu):
    S = ys.shape[0]

    def fn(rv, cr, out):
        d_ys5_b, ys_b, glu_b, z_s = rv
        wos5_ref, wglu_ref = cr
        ga, gb = glu_b[:, :S5_WIDTH], glu_b[:, S5_WIDTH:]
        sgb, silz = _sigmoid(gb), _silu(z_s)
        ys2 = ga * sgb * silz
        d_wos5 = _bdot_tn(ys2, d_ys5_b)
        d_ys2 = _bdot_nt(d_ys5_b, wos5_ref[...])
        d_ga = d_ys2 * sgb * silz
        d_gb = d_ys2 * ga * sgb * (1.0 - sgb) * silz
        d_z = d_ys2 * ga * sgb * _dsilu(z_s)
        d_glu = jnp.concatenate([d_ga, d_gb], axis=1)
        gl = _gelu(ys_b)
        d_wglu = _bdot_tn(gl, d_glu)
        d_bglu = jnp.sum(d_glu, axis=0, keepdims=True)
        d_gl = _bdot_nt(d_glu, wglu_ref[...])
        return [d_gl * _dgelu(ys_b), d_z], [d_wos5, d_wglu, d_bglu]

    rows = [(d_ys5, D_MODEL, 0), (ys, S5_WIDTH, 0), (glu, D_MODEL, 0), (proj, S5_WIDTH, COL_ZS // S5_WIDTH)]
    outs, accs = _rowwise(
        "bwd_s5_path", fn, S, 512, rows, [w_o_s5, w_glu], [(S5_WIDTH, F32), (S5_WIDTH, BF16)],
        [((S5_WIDTH, D_MODEL), F32), ((S5_WIDTH, D_MODEL), F32), ((1, D_MODEL), F32)], 48, parts=2)
    return outs, accs


def _stage_inproj_bwd(pieces, x, dh1, norm_g, w_in_all):
    S = x.shape[0]

    def fn(rv, cr, out):
        g_ref, w_ref = cr
        x_b, dh1_b = rv[8], rv[9]
        dproj_ref = out[1]
        col = 0
        for v in rv[:8]:
            dproj_ref[:, col:col + v.shape[1]] = v.astype(BF16)
            col += v.shape[1]
        d_u = jnp.zeros((x_b.shape[0], D_MODEL), F32)
        for j in range(N_DEV):
            d_u = d_u + lax.dot_general(dproj_ref[:, j * SHARD_IN:(j + 1) * SHARD_IN], w_ref[j],
                                        (((1,), (1,)), ((), ())), preferred_element_type=F32)
        _, n, r = _rms_fwd(x_b, g_ref[...])
        dx, d_g = _rms_bwd(d_u, n, r, g_ref[...])
        return [dh1_b + dx, None], [d_g]

    rows = [(a, a.shape[1], 0) for a in pieces] + [(x, D_MODEL, 0), (dh1, D_MODEL, 0)]
    (grad_x, dproj), (d_g,) = _rowwise(
        "inproj_bwd", fn, S, 256, rows, [norm_g, w_in_all], [(D_MODEL, F32), (IN_COLS, BF16)],
        [((1, D_MODEL), F32)], 56)
    return grad_x, dproj, d_g


def _chunk_row(shape):
    return lax.broadcasted_iota(jnp.int32, shape, 0) & (HG_CHUNK - 1)


def _chunk_cumsum(x):
    r_in = _chunk_row(x.shape)
    s = 1
    while s < HG_CHUNK:
        x = x + jnp.where(r_in >= s, pltpu.roll(x, s, 0), 0.0)
        s *= 2
    return x


def _chunk_suffix_sum(x):
    n = x.shape[0]
    r_in = _chunk_row(x.shape)
    s = 1
    while s < HG_CHUNK:
        x = x + jnp.where(r_in < HG_CHUNK - s, pltpu.roll(x, n - s, 0), 0.0)
        s *= 2
    return x


def _hgrn_prep(q, fl, lb):
    sup = q.shape[0]
    nc = sup // HG_CHUNK
    sig = _sigmoid(fl)
    f = lb + (1.0 - lb) * sig
    k = (1.0 - lb) * (1.0 - sig)
    b = _chunk_cumsum(jnp.log(f))
    b3 = b.reshape(nc, HG_CHUNK, HG_DIM)
    row3 = lax.broadcasted_iota(jnp.int32, b3.shape, 1)
    pick = lambda r: jnp.sum(jnp.where(row3 == r, b3, 0.0), axis=1, keepdims=True)
    b_mid = pick(HG_CHUNK // 2 - 1)
    b_last = pick(HG_CHUNK - 1)
    flat = lambda t: t.reshape(sup, HG_DIM)
    e_qa = flat(jnp.exp(b3 - b_mid))
    e_ka = flat(jnp.exp(b_mid - b3))
    e_qd = jnp.exp(b)
    e_kd = flat(jnp.exp(b_last - b3))
    dc = jnp.exp(b_last)
    return sig, f, k, e_qa, e_ka, e_qd, e_kd, dc


def _hgrn_mask(sup):
    r = lax.broadcasted_iota(jnp.int32, (sup, sup), 0)
    c = lax.broadcasted_iota(jnp.int32, (sup, sup), 1)
    shift = HG_CHUNK.bit_length() - 1
    return (jnp.right_shift(r, shift) == jnp.right_shift(c, shift)) & (r >= c)


def _hgrn_fwd(proj, lb):
    S = proj.shape[0]
    sup = HG_SUPER_FWD
    nb = S // sup
    nc = sup // HG_CHUNK
    hp = HG_HEADS_PER_STEP
    wide = hp * HG_DIM

    def body(q_ref, f_ref, iv_ref, lb_ref, o_ref, st_ref, state):
        @pl.when(pl.program_id(1) == 0)
        def _():
            state[...] = jnp.zeros(state.shape, F32)

        mask = _hgrn_mask(sup)
        for hh in range(hp):
            lanes = slice(hh * HG_DIM, (hh + 1) * HG_DIM)
            q, iv = q_ref[:, lanes], iv_ref[:, lanes]
            _, _, k, e_qa, e_ka, e_qd, e_kd, dc = _hgrn_prep(q, f_ref[:, lanes], lb_ref[:, lanes])
            scores = jnp.where(mask, _bdot_nt(q * e_qa, k * e_ka), 0.0)
            o_intra = _bdot(scores, iv)
            qd, kd = q * e_qd, k * e_kd
            for c in range(nc):
                sl = slice(c * HG_CHUNK, (c + 1) * HG_CHUNK)
                st = state[hh]
                st_ref[hh, c] = st
                o_ref[sl, lanes] = o_intra[sl] + _bdot_nt(qd[sl], st)
                state[hh] = dc[c] * st + _bdot_tn(iv[sl], kd[sl])

    blk = lambda base: pl.BlockSpec((sup, wide), functools.partial(lambda h, i, b: (i, b + h), b=base // wide))
    return _call(
        body, name="hgrn_fwd", grid=(HG_HEADS // hp, nb),
        in_specs=[blk(COL_Q), blk(COL_F), blk(COL_I), pl.BlockSpec((1, wide), lambda h, i: (0, h))],
        out_specs=[pl.BlockSpec((sup, wide), lambda h, i: (i, h)),
                   pl.BlockSpec((hp, nc, HG_DIM, HG_DIM), lambda h, i: (h, i, 0, 0))],
        out_shape=[jax.ShapeDtypeStruct((S, D_MODEL), F32),
                   jax.ShapeDtypeStruct((HG_HEADS, S // HG_CHUNK, HG_DIM, HG_DIM), F32)],
        scratch_shapes=[pltpu.VMEM((hp, HG_DIM, HG_DIM), F32)],
        compiler_params=_params(40, 2),
    )(proj, proj, proj, lb)


def _hgrn_bwd(proj, lb, d_o, states):
    S = proj.shape[0]
    sup = HG_SUPER_BWD
    nb = S // sup
    nc = sup // HG_CHUNK
    hp = HG_HEADS_PER_STEP
    wide = hp * HG_DIM

    def body(q_ref, f_ref, iv_ref, lb_ref, do_ref, st_ref, dq_ref, df_ref, div_ref, dlb_ref, dstate):
        @pl.when(pl.program_id(1) == 0)
        def _():
            dstate[...] = jnp.zeros(dstate.shape, F32)
            dlb_ref[...] = jnp.zeros(dlb_ref.shape, F32)

        mask = _hgrn_mask(sup)
        for hh in range(hp):
            lanes = slice(hh * HG_DIM, (hh + 1) * HG_DIM)
            q, iv, do, lb_v = q_ref[:, lanes], iv_ref[:, lanes], do_ref[:, lanes], lb_ref[:, lanes]
            sig, f, k, e_qa, e_ka, e_qd, e_kd, dc = _hgrn_prep(q, f_ref[:, lanes], lb_v)
            qa, ka, qd, kd = q * e_qa, k * e_ka, q * e_qd, k * e_kd
            scores = jnp.where(mask, _bdot_nt(qa, ka), 0.0)
            d_scores = jnp.where(mask, _bdot_nt(do, iv), 0.0)
            d_iv_intra = _bdot_tn(scores, do)
            d_qa = _bdot(d_scores, ka)
            d_ka = _bdot_tn(d_scores, qa)
            d_qd, d_kd, d_last = [None] * nc, [None] * nc, [None] * nc
            for c in reversed(range(nc)):
                sl = slice(c * HG_CHUNK, (c + 1) * HG_CHUNK)
                st = st_ref[hh, c]
                ds = dstate[hh]
                d_qd[c] = _bdot(do[sl], st)
                d_kd[c] = _bdot(iv[sl], ds)
                div_ref[sl, lanes] = (d_iv_intra[sl] + _bdot_nt(kd[sl], ds)).astype(div_ref.dtype)
                d_last[c] = (jnp.sum(ds * st, axis=0, keepdims=True) * dc[c]
                             + jnp.sum(d_kd[c] * kd[sl], axis=0, keepdims=True))
                dstate[hh] = dc[c] * ds + _bdot_tn(do[sl], qd[sl])
            d_qd = jnp.concatenate(d_qd, axis=0)
            d_kd = jnp.concatenate(d_kd, axis=0)
            d_b = d_qa * qa - d_ka * ka + d_qd * qd - d_kd * kd
            last_rows = jnp.concatenate([jnp.broadcast_to(t, (HG_CHUNK, HG_DIM)) for t in d_last], axis=0)
            d_b = d_b + jnp.where(_chunk_row(d_b.shape) == HG_CHUNK - 1, last_rows, 0.0)
            d_logf = _chunk_suffix_sum(d_b)
            d_k = d_ka * e_ka + d_kd * e_kd
            g_f = d_logf / f
            d_sig = (g_f - d_k) * (1.0 - lb_v)
            dq_ref[:, lanes] = (d_qa * e_qa + d_qd * e_qd).astype(dq_ref.dtype)
            df_ref[:, lanes] = (d_sig * sig * (1.0 - sig)).astype(df_ref.dtype)
            d_lb = jnp.sum((g_f - d_k) * (1.0 - sig), axis=0, keepdims=True)
            dlb_ref[:, lanes] += jnp.broadcast_to(d_lb, (8, HG_DIM))

    rev = lambda i: nb - 1 - i
    blk = lambda base: pl.BlockSpec((sup, wide), functools.partial(lambda h, i, b: (rev(i), b + h), b=base // wide))
    row_out = pl.BlockSpec((sup, wide), lambda h, i: (rev(i), h))
    dq, df, div, dlb = _call(
        body, name="hgrn_bwd", grid=(HG_HEADS // hp, nb),
        in_specs=[blk(COL_Q), blk(COL_F), blk(COL_I), pl.BlockSpec((1, wide), lambda h, i: (0, h)),
                  pl.BlockSpec((sup, wide), lambda h, i: (rev(i), h)),
                  pl.BlockSpec((hp, nc, HG_DIM, HG_DIM), lambda h, i: (h, rev(i), 0, 0))],
        out_specs=[row_out, row_out, row_out, pl.BlockSpec((8, wide), lambda h, i: (0, h))],
        out_shape=[jax.ShapeDtypeStruct((S, D_MODEL), BF16)] * 3 + [jax.ShapeDtypeStruct((8, D_MODEL), F32)],
        scratch_shapes=[pltpu.VMEM((hp, HG_DIM, HG_DIM), F32)],
        compiler_params=_params(40, 2),
    )(proj, proj, proj, lb, d_o, states)
    return dq, df, div, dlb[0:1]


def _s5_matrices(a_re, a_im, log_dt, b_re, b_im, c_re, c_im, d, seg_len):
    dt = jnp.exp(log_dt)[:, None]
    mag = jnp.exp(a_re * dt)
    lr, li = mag * jnp.cos(a_im * dt), mag * jnp.sin(a_im * dt)
    den = a_re * a_re + a_im * a_im
    nr = lr - 1.0
    sr = (nr * a_re + li * a_im) / den
    si = (li * a_re - nr * a_im) / den
    bbr = sr[..., None] * b_re - si[..., None] * b_im
    bbi = sr[..., None] * b_im + si[..., None] * b_re
    eye = jnp.eye(8, dtype=F32)

    def quad_cols(v):
        return v.reshape(S5_QUADS, 8 * S5_STATE)

    def lam_row(re_part, im_part):
        row = jnp.concatenate([quad_cols(re_part), quad_cols(im_part)], axis=1).reshape(1, S5_COLS)
        return jnp.broadcast_to(row, (S5_SEG, S5_COLS))

    def b_mat(bb):
        t = bb.reshape(S5_QUADS, 8, S5_STATE, S5_CH)
        return jnp.einsum("qgnc,gh->qgchn", t, eye).reshape(S5_QUADS, 8 * S5_CH, 8 * S5_STATE)

    def c_mat(cc):
        t = cc.reshape(S5_QUADS, 8, S5_CH, S5_STATE)
        return jnp.einsum("qgcn,gh->qgnhc", t, eye).reshape(S5_QUADS, 8 * S5_STATE, 8 * S5_CH)

    ang = a_im * dt * seg_len
    magp = jnp.exp(a_re * dt * seg_len)
    lpr, lpi = magp * jnp.cos(ang), magp * jnp.sin(ang)
    return dict(
        lam_r=lam_row(lr, lr), lam_i=lam_row(-li, li),
        b_q=jnp.concatenate([b_mat(bbr), b_mat(bbi)], axis=2),
        c_q=jnp.concatenate([c_mat(c_re), -c_mat(c_im)], axis=1),
        d_row=d.reshape(1, S5_WIDTH), pow_r=quad_cols(lpr), pow_i=quad_cols(lpi),
    )


def _s5_parts(v):
    half = S5_QCOLS // 2
    return tuple(v[:, k * half:(k + 1) * half] for k in range(2 * S5_QUADS))


def _s5_advance(parts, lr_ref, li_ref, x_ref, sl, conj):
    half = S5_QCOLS // 2
    out = []
    for q in range(S5_QUADS):
        re_c = slice(q * S5_QCOLS, q * S5_QCOLS + half)
        im_c = slice(q * S5_QCOLS + half, (q + 1) * S5_QCOLS)
        lr, li = lr_ref[:, re_c], li_ref[:, im_c]
        hr, hi = parts[2 * q], parts[2 * q + 1]
        if conj:
            out += [lr * hr + li * hi + x_ref[sl, re_c], lr * hi - li * hr + x_ref[sl, im_c]]
        else:
            out += [lr * hr - li * hi + x_ref[sl, re_c], lr * hi + li * hr + x_ref[sl, im_c]]
    return tuple(out)


def _scan_loop(step, init):
    def trip(o, carry):
        for j in range(S5_UNROLL):
            carry = step(o * S5_UNROLL + j, carry)
        return carry

    return lax.fori_loop(0, S5_TILE_STEPS // S5_UNROLL, trip, init)


def _s5_store(ref, sl, parts):
    half = S5_QCOLS // 2
    for k, v in enumerate(parts):
        ref[sl, k * half:(k + 1) * half] = v


def _s5_fwd_pass(u_perm, mats, h0, with_output):
    S = u_perm.shape[0]
    rows = S5_TILE_STEPS * S5_SEG
    nt = S // rows

    def body(*refs):
        if with_output:
            u_ref, b_ref, lr_ref, li_ref, h0_ref, c_ref, d_ref, y_ref, hinit_ref, hend_ref, xs, hcar = refs
        else:
            u_ref, b_ref, lr_ref, li_ref, h0_ref, hend_ref, xs, hcar = refs

        @pl.when(pl.program_id(0) == 0)
        def _():
            hcar[...] = h0_ref[...]

        if with_output:
            hinit_ref[...] = hcar[...]
        u = u_ref[...]
        ub = u.astype(BF16)
        for q in range(S5_QUADS):
            xs[:, q * S5_QCOLS:(q + 1) * S5_QCOLS] = jnp.dot(ub[:, q * 128:(q + 1) * 128], b_ref[q], preferred_element_type=F32)

        def step(t, h):
            sl = pl.ds(pl.multiple_of(t * S5_SEG, S5_SEG), S5_SEG)
            hn = _s5_advance(h, lr_ref, li_ref, xs, sl, False)
            _s5_store(xs, sl, hn)
            return hn

        h = _scan_loop(step, _s5_parts(hcar[...]))
        _s5_store(hcar, slice(None), h)
        _s5_store(hend_ref, slice(None), h)
        if with_output:
            ys = [jnp.dot(xs[:, q * S5_QCOLS:(q + 1) * S5_QCOLS].astype(BF16), c_ref[q], preferred_element_type=F32)
                  for q in range(S5_QUADS)]
            y_ref[...] = jnp.concatenate(ys, axis=1) + d_ref[...] * u

    full = lambda a: pl.BlockSpec(a.shape, functools.partial(lambda i, nd: (0,) * nd, nd=a.ndim))
    ins = [u_perm, mats["b_q"], mats["lam_r"], mats["lam_i"], h0]
    in_specs = [pl.BlockSpec((rows, S5_WIDTH), lambda i: (i, 0))] + [full(a) for a in ins[1:]]
    out_specs = [pl.BlockSpec((S5_SEG, S5_COLS), lambda i: (0, 0))]
    out_shape = [jax.ShapeDtypeStruct((S5_SEG, S5_COLS), F32)]
    if with_output:
        ins += [mats["c_q"], mats["d_row"]]
        in_specs += [full(mats["c_q"]), full(mats["d_row"])]
        out_specs = [pl.BlockSpec((rows, S5_WIDTH), lambda i: (i, 0)),
                     pl.BlockSpec((None, S5_SEG, S5_COLS), lambda i: (i, 0, 0))] + out_specs
        out_shape = [jax.ShapeDtypeStruct((S, S5_WIDTH), F32), jax.ShapeDtypeStruct((nt, S5_SEG, S5_COLS), F32)] + out_shape
    return _call(
        body, name="s5_fwd_y" if with_output else "s5_fwd_ends", grid=(nt,), in_specs=in_specs, out_specs=out_specs,
        out_shape=out_shape,
        scratch_shapes=[pltpu.VMEM((rows, S5_COLS), F32), pltpu.VMEM((S5_SEG, S5_COLS), F32)],
        compiler_params=_params(40),
    )(*ins)


def _s5_bwd_ends(dy_perm, mats):
    S = dy_perm.shape[0]
    rows = S5_TILE_STEPS * S5_SEG
    nt = S // rows

    def body(dy_ref, c_ref, lr_ref, li_ref, gend_ref, gs, gcar):
        @pl.when(pl.program_id(0) == 0)
        def _():
            gcar[...] = jnp.zeros(gcar.shape, F32)

        dyb = dy_ref[...].astype(BF16)
        for q in range(S5_QUADS):
            gs[:, q * S5_QCOLS:(q + 1) * S5_QCOLS] = lax.dot_general(
                dyb[:, q * 128:(q + 1) * 128], c_ref[q], (((1,), (1,)), ((), ())), preferred_element_type=F32)

        def step(k, g):
            t = S5_TILE_STEPS - 1 - k
            sl = pl.ds(pl.multiple_of(t * S5_SEG, S5_SEG), S5_SEG)
            return _s5_advance(g, lr_ref, li_ref, gs, sl, True)

        g = _scan_loop(step, _s5_parts(gcar[...]))
        _s5_store(gcar, slice(None), g)
        _s5_store(gend_ref, slice(None), g)

    full = lambda a: pl.BlockSpec(a.shape, functools.partial(lambda i, nd: (0,) * nd, nd=a.ndim))
    return _call(
        body, name="s5_bwd_ends", grid=(nt,),
        in_specs=[pl.BlockSpec((rows, S5_WIDTH), lambda i: (nt - 1 - i, 0)), full(mats["c_q"]), full(mats["lam_r"]),
                  full(mats["lam_i"])],
        out_specs=pl.BlockSpec((S5_SEG, S5_COLS), lambda i: (0, 0)),
        out_shape=jax.ShapeDtypeStruct((S5_SEG, S5_COLS), F32),
        scratch_shapes=[pltpu.VMEM((rows, S5_COLS), F32), pltpu.VMEM((S5_SEG, S5_COLS), F32)],
        compiler_params=_params(40),
    )(dy_perm, mats["c_q"], mats["lam_r"], mats["lam_i"])


def _s5_bwd_full(u_perm, dy_perm, hinit, g0, mats):
    S = u_perm.shape[0]
    rows = S5_TILE_STEPS * S5_SEG
    nt = S // rows

    def body(u_ref, dy_ref, hinit_ref, g0_ref, b_ref, c_ref, lr_ref, li_ref, d_ref,
             du_ref, dp_ref, dq_ref, db_ref, dc_ref, dd_ref, hs, gs, gcar):
        @pl.when(pl.program_id(0) == 0)
        def _():
            gcar[...] = g0_ref[...]
            for ref in (dp_ref, dq_ref, db_ref, dc_ref, dd_ref):
                ref[...] = jnp.zeros(ref.shape, F32)

        u, dy = u_ref[...], dy_ref[...]
        ub, dyb = u.astype(BF16), dy.astype(BF16)
        hs[0:S5_SEG, :] = hinit_ref[...]
        for q in range(S5_QUADS):
            cols = slice(q * S5_QCOLS, (q + 1) * S5_QCOLS)
            hs[S5_SEG:, cols] = jnp.dot(ub[:, q * 128:(q + 1) * 128], b_ref[q], preferred_element_type=F32)
            gs[:, cols] = lax.dot_general(dyb[:, q * 128:(q + 1) * 128], c_ref[q], (((1,), (1,)), ((), ())),
                                          preferred_element_type=F32)

        def fstep(t, h):
            sl = pl.ds(pl.multiple_of((t + 1) * S5_SEG, S5_SEG), S5_SEG)
            hn = _s5_advance(h, lr_ref, li_ref, hs, sl, False)
            _s5_store(hs, sl, hn)
            return hn

        _scan_loop(fstep, _s5_parts(hinit_ref[...]))

        def bstep(k, g):
            t = S5_TILE_STEPS - 1 - k
            sl = pl.ds(pl.multiple_of(t * S5_SEG, S5_SEG), S5_SEG)
            gn = _s5_advance(g, lr_ref, li_ref, gs, sl, True)
            _s5_store(gs, sl, gn)
            return gn

        _s5_store(gcar, slice(None), _scan_loop(bstep, _s5_parts(gcar[...])))

        half = S5_QCOLS // 2
        dus = []
        for q in range(S5_QUADS):
            cols = slice(q * S5_QCOLS, (q + 1) * S5_QCOLS)

            def astep(t, carry, q=q):
                sl = pl.ds(pl.multiple_of(t * S5_SEG, S5_SEG), S5_SEG)
                g = gs[sl, q * S5_QCOLS:(q + 1) * S5_QCOLS]
                hp = hs[sl, q * S5_QCOLS:(q + 1) * S5_QCOLS]
                hp_sw = jnp.concatenate([hp[:, half:], hp[:, :half]], axis=1)
                return carry[0] + g * hp, carry[1] + g * hp_sw

            zero = jnp.zeros((S5_SEG, S5_QCOLS), F32)
            acc_p, acc_q = _scan_loop(astep, (zero, zero))
            dp_ref[:, cols] += jnp.sum(acc_p, axis=0, keepdims=True)
            dq_ref[:, cols] += jnp.sum(acc_q, axis=0, keepdims=True)
            gq = gs[:, cols].astype(BF16)
            db_ref[q] += lax.dot_general(ub[:, q * 128:(q + 1) * 128], gq, (((0,), (0,)), ((), ())),
                                         preferred_element_type=F32)
            hq = hs[S5_SEG:, cols].astype(BF16)
            dc_ref[q] += lax.dot_general(dyb[:, q * 128:(q + 1) * 128], hq, (((0,), (0,)), ((), ())),
                                         preferred_element_type=F32)
            dus.append(lax.dot_general(gq, b_ref[q], (((1,), (1,)), ((), ())), preferred_element_type=F32))
        du_ref[...] = (jnp.concatenate(dus, axis=1) + d_ref[...] * dy).astype(du_ref.dtype)
        dd_ref[...] += jnp.sum(dy * u, axis=0, keepdims=True)

    full = lambda a: pl.BlockSpec(a.shape, functools.partial(lambda i, nd: (0,) * nd, nd=a.ndim))
    rev_rows = pl.BlockSpec((rows, S5_WIDTH), lambda i: (nt - 1 - i, 0))
    consts = [mats["b_q"], mats["c_q"], mats["lam_r"], mats["lam_i"], mats["d_row"]]
    acc = lambda s: pl.BlockSpec(s, functools.partial(lambda i, nd: (0,) * nd, nd=len(s)))
    acc_shapes = [(1, S5_COLS), (1, S5_COLS), (S5_QUADS, 128, S5_QCOLS), (S5_QUADS, 128, S5_QCOLS), (1, S5_WIDTH)]
    return _call(
        body, name="s5_bwd_full", grid=(nt,),
        in_specs=[rev_rows, rev_rows, pl.BlockSpec((None, S5_SEG, S5_COLS), lambda i: (nt - 1 - i, 0, 0)), full(g0)]
        + [full(a) for a in consts],
        out_specs=[rev_rows] + [acc(s) for s in acc_shapes],
        out_shape=[jax.ShapeDtypeStruct((S, S5_WIDTH), BF16)] + [jax.ShapeDtypeStruct(s, F32) for s in acc_shapes],
        scratch_shapes=[pltpu.VMEM((rows + S5_SEG, S5_COLS), F32), pltpu.VMEM((rows, S5_COLS), F32),
                        pltpu.VMEM((S5_SEG, S5_COLS), F32)],
        compiler_params=_params(56),
    )(u_perm, dy_perm, hinit, g0, *consts)


def _cmul(ar, ai, br, bi):
    return ar * br - ai * bi, ar * bi + ai * br


def _split_cols(v):
    t = v.reshape(v.shape[0], S5_QUADS, 2, S5_QCOLS // 2)
    return t[:, :, 0], t[:, :, 1]


def _join_cols(re, im):
    return jnp.stack([re, im], axis=2).reshape(re.shape[0], S5_COLS)


def _segment_starts(ends, pow_r, pow_i, reverse):
    er, ei = _split_cols(ends)
    pi = -pow_i if reverse else pow_i
    order = list(range(S5_SEG))
    if reverse:
        order = order[::-1]
    cr, ci = jnp.zeros_like(er[0]), jnp.zeros_like(ei[0])
    out_r, out_i = [None] * S5_SEG, [None] * S5_SEG
    for j in order:
        out_r[j], out_i[j] = cr, ci
        mr, mi = _cmul(pow_r, pi, cr, ci)
        cr, ci = mr + er[j], mi + ei[j]
    return _join_cols(jnp.stack(out_r), jnp.stack(out_i))


def _to_segments(a):
    S, w = a.shape
    return a.reshape(S5_SEG, S // S5_SEG, w).transpose(1, 0, 2).reshape(S, w)


def _from_segments(a):
    S, w = a.shape
    return a.reshape(S // S5_SEG, S5_SEG, w).transpose(1, 0, 2).reshape(S, w)


def _my_pos():
    return lax.axis_index("x"), lax.axis_index("y"), lax.axis_index("c")


def _flip(pos, k):
    x, y, c = pos
    return (1 - x if k & 4 else x, 1 - y if k & 2 else y, 1 - c if k & 1 else c)


def _index_of(pos):
    return 4 * pos[0] + 2 * pos[1] + pos[2]


_GATHER_FLIPS = (0, 1, 4, 5, 2, 3, 6, 7)


def _inproj_gather(x, norm_g, pack_b, others):
    S = x.shape[0]
    tm = min(S, 1024)
    n_i = S // tm
    order = jnp.stack([_index_of(_flip(_my_pos(), k)) for k in _GATHER_FLIPS]).astype(jnp.int32)

    n_o = len(others)
    n_arr = 1 + n_o

    def body(order_ref, x_ref, g_ref, pb_ref, *rest):
        other_in, (ut_ref, proj_ref, ob_ref), other_out = rest[:n_o], rest[n_o:n_o + 3], rest[n_o + 3:2 * n_o + 3]
        wv, u_scr, send_sems, recv_sems, local_sems = rest[2 * n_o + 3:]
        s, i = pl.program_id(0), pl.program_id(1)
        me = _my_pos()
        mine = _index_of(me)
        sibling = _flip(me, 1)
        srcs = (pb_ref,) + tuple(other_in)
        dsts = (wv,) + tuple(other_out)

        def direct(a, k):
            return pltpu.make_async_remote_copy(
                src_ref=srcs[a], dst_ref=dsts[a].at[mine], send_sem=send_sems.at[a * 8 + k],
                recv_sem=recv_sems.at[a * 8 + k], device_id=_flip(me, k), device_id_type=MESH)

        def passed_on(a, k):
            slot = _index_of(_flip(me, k))
            return pltpu.make_async_remote_copy(
                src_ref=dsts[a].at[slot], dst_ref=dsts[a].at[slot], send_sem=send_sems.at[a * 8 + (k | 1)],
                recv_sem=recv_sems.at[a * 8 + (k | 1)], device_id=sibling, device_id_type=MESH)

        def arrival(a, k):
            slot = _index_of(_flip(me, k))
            pltpu.make_async_remote_copy(
                src_ref=dsts[a].at[slot], dst_ref=dsts[a].at[slot], send_sem=send_sems.at[a * 8 + k],
                recv_sem=recv_sems.at[a * 8 + k], device_id=me, device_id_type=MESH).wait_recv()

        def own_copy(a):
            return pltpu.make_async_copy(srcs[a], dsts[a].at[mine], local_sems.at[a])

        def keep(idx):
            slot = _index_of(_flip(me, _GATHER_FLIPS[idx]))
            return pltpu.make_async_copy(wv.at[slot], ob_ref.at[slot], local_sems.at[n_arr + idx])


        first = (s == 0) & (i == 0)

        @pl.when(first)
        def _():
            for a in range(n_arr):
                own_copy(a).start()
            for k in (1, 4, 2):
                direct(0, k).start()
            own_copy(0).wait()
            keep(0).start()

        for idx, k in enumerate(_GATHER_FLIPS):
            if idx == 0:
                continue

            @pl.when((s == idx) & (i == 0))
            def _(idx=idx, k=k):
                arrival(0, k)
                if k in (4, 2, 6):
                    passed_on(0, k).start()
                keep(idx).start()
                if idx == 1:
                    direct(0, 6).start()
                if idx == 2:
                    for a in range(1, n_arr):
                        for k in (1, 4, 2, 6):
                            direct(a, k).start()

        @pl.when(s == 0)
        def _():
            y, _, _ = _rms_fwd(x_ref[...], g_ref[...])
            u_scr[pl.ds(pl.multiple_of(i * tm, tm), tm), :] = y.astype(BF16)
            ut_ref[...] = y.T.astype(BF16)

        ub = u_scr[pl.ds(pl.multiple_of(i * tm, tm), tm), :]
        proj_ref[...] = jnp.dot(ub, wv[order_ref[s]], preferred_element_type=F32)

        @pl.when((s == N_DEV - 1) & (i == n_i - 1))
        def _():
            for a in range(1, n_arr):
                for k in (4, 2, 6):
                    arrival(a, k)
                    passed_on(a, k).start()
            for a in range(1, n_arr):
                for k in (1, 5, 3, 7):
                    arrival(a, k)
                own_copy(a).wait()
            for a in range(n_arr):
                for k in (1, 4, 2, 6):
                    direct(a, k).wait_send()
                for k in (4, 2, 6):
                    passed_on(a, k).wait_send()
            for idx in range(N_DEV):
                keep(idx).wait()

    any_spec = pl.BlockSpec(memory_space=pl.ANY)
    vmem = pl.BlockSpec(memory_space=pltpu.VMEM)
    grid_spec = pltpu.PrefetchScalarGridSpec(
        num_scalar_prefetch=1, grid=(N_DEV, n_i),
        in_specs=[pl.BlockSpec((tm, D_MODEL), lambda s, i, o: (jnp.where(s == 0, i, 0), 0)),
                  pl.BlockSpec((1, D_MODEL), lambda s, i, o: (0, 0)), vmem] + [any_spec] * n_o,
        out_specs=[pl.BlockSpec((D_MODEL, tm), lambda s, i, o: (0, jnp.where(s == 0, i, n_i - 1))),
                   pl.BlockSpec((tm, SHARD_IN), lambda s, i, o: (i, o[s])), any_spec] + [any_spec] * n_o,
        scratch_shapes=[pltpu.VMEM((N_DEV,) + pack_b.shape, BF16), pltpu.VMEM((S, D_MODEL), BF16),
                        pltpu.SemaphoreType.DMA((8 * n_arr,)), pltpu.SemaphoreType.DMA((8 * n_arr,)),
                        pltpu.SemaphoreType.DMA((n_arr + N_DEV,))],
    )
    res = _call(
        body, name="inproj_gather", grid_spec=grid_spec,
        out_shape=[jax.ShapeDtypeStruct((D_MODEL, S), BF16), jax.ShapeDtypeStruct((S, IN_COLS), F32),
                   jax.ShapeDtypeStruct((N_DEV,) + pack_b.shape, BF16)]
        + [jax.ShapeDtypeStruct((N_DEV,) + a.shape, BF16) for a in others],
        compiler_params=_params(56, 2),
    )(order, x, norm_g, pack_b, *others)
    return res[0], res[1], res[2], list(res[3:])


_SCATTER_FLIPS = (7, 6, 5, 4, 3, 2, 1, 0)
_N_CHIPS = 4


def _for_row_chunks(n_rows, chunk, fn):
    def step(c, carry):
        fn(pl.ds(pl.multiple_of(c * chunk, chunk), chunk))
        return carry

    lax.fori_loop(0, n_rows // chunk, step, 0)


def _grad_w_in_scatter(dproj, u_t, others, small_partial):
    S = u_t.shape[1]
    tm = min(S, 1024)
    n_i = S // tm
    order = jnp.stack([_index_of(_flip(_my_pos(), k)) for k in _SCATTER_FLIPS]).astype(jnp.int32)
    shapes = ((D_MODEL, SHARD_IN),) + tuple(a.shape[1:] for a in others)
    n_o = len(others)
    sem_small = 7 * (n_o + 1)
    row_chunk = 128

    def body(order_ref, dp_ref, ut_ref, *rest):
        other_in, p_ref, gb_ref = rest[:n_o], rest[n_o], rest[n_o + 1]
        other_out, gs_ref = rest[n_o + 2:2 * n_o + 2], rest[2 * n_o + 2]
        acc, sib_b, d2d_b, send_b, ici_b = rest[2 * n_o + 3:2 * n_o + 8]
        trios = rest[2 * n_o + 8:5 * n_o + 8]
        gath, send_sems, recv_sems, local_sems = rest[5 * n_o + 8:]
        s, i = pl.program_id(0), pl.program_id(1)
        me = _my_pos()
        sibling = _flip(me, 1)

        my_chip = 2 * me[0] + me[1]

        small_d2d = pltpu.make_async_remote_copy(
            src_ref=p_ref, dst_ref=gath.at[_N_CHIPS], send_sem=send_sems.at[sem_small], recv_sem=recv_sems.at[sem_small],
            device_id=sibling, device_id_type=MESH)

        def small_ici(m):
            return pltpu.make_async_remote_copy(
                src_ref=gath.at[my_chip], dst_ref=gath.at[my_chip], send_sem=send_sems.at[sem_small + 1 + m],
                recv_sem=recv_sems.at[sem_small + 1 + m], device_id=_flip(me, 6 - 2 * m), device_id_type=MESH)
        sib = (sib_b,) + tuple(trios[1::3])
        ici = (ici_b,) + tuple(trios[2::3])
        outs = (gb_ref,) + tuple(other_out)

        def to_sibling(arr, m, src):
            return pltpu.make_async_remote_copy(
                src_ref=src, dst_ref=sib[arr].at[m], send_sem=send_sems.at[arr * 7 + m],
                recv_sem=recv_sems.at[arr * 7 + m], device_id=sibling, device_id_type=MESH)

        def over_ici(arr, m, src):
            return pltpu.make_async_remote_copy(
                src_ref=src, dst_ref=ici[arr].at[m], send_sem=send_sems.at[arr * 7 + 4 + m],
                recv_sem=recv_sems.at[arr * 7 + 4 + m], device_id=_flip(me, 6 - 2 * m), device_id_type=MESH)

        def from_sibling(arr, m):
            to_sibling(arr, m, sib[arr].at[m]).wait_recv()

        def from_ici(arr, m):
            over_ici(arr, m, ici[arr].at[m]).wait_recv()

        small = tuple((a + 1, other_in[a], trios[3 * a]) for a in range(n_o))

        def local_copy(arr, src, mine, m):
            return pltpu.make_async_copy(src.at[_index_of(_flip(me, 6 - 2 * m))], mine.at[m],
                                         local_sems.at[(arr - 1) * _N_CHIPS + m])

        @pl.when((s == 0) & (i == 0))
        def _():
            small_d2d.start()
            for arr, src, mine in small:
                for m in range(_N_CHIPS):
                    to_sibling(arr, m, src.at[_index_of(_flip(me, 7 - 2 * m))]).start()
                    local_copy(arr, src, mine, m).start()

        @pl.when((s == 1) & (i == 0))
        def _():
            small_d2d.wait_recv()
            gath[my_chip] = p_ref[...] + gath[_N_CHIPS]
            for m in range(_N_CHIPS - 1):
                small_ici(m).start()
            for arr, src, mine in small:
                rows, chunk = shapes[arr][0], 16
                for m in range(_N_CHIPS):
                    local_copy(arr, src, mine, m).wait()
                    from_sibling(arr, m)
                    if m < _N_CHIPS - 1:
                        def add(sl, arr=arr, mine=mine, m=m):
                            mine[m, sl, :] = (mine[m, sl, :].astype(F32) + sib[arr][m, sl, :].astype(F32)).astype(BF16)

                        _for_row_chunks(rows, chunk, add)
                        over_ici(arr, m, mine.at[m]).start()
                    else:
                        def keep(sl, arr=arr, mine=mine, m=m):
                            outs[arr][sl, :] = mine[m, sl, :].astype(F32) + sib[arr][m, sl, :].astype(F32)

                        _for_row_chunks(rows, chunk, keep)

        @pl.when(i == 0)
        def _():
            acc[...] = jnp.zeros(acc.shape, F32)

        acc[...] += jnp.dot(ut_ref[...], dp_ref[...], preferred_element_type=F32)

        def block_rows(c):
            return acc[c * row_chunk:(c + 1) * row_chunk, :]

        for m in range(_N_CHIPS):
            @pl.when((s == 2 * m) & (i == n_i - 1))
            def _(m=m):
                if m > 0:
                    to_sibling(0, m - 1, d2d_b).wait_send()
                for c in range(D_MODEL // row_chunk):
                    d2d_b[c * row_chunk:(c + 1) * row_chunk, :] = block_rows(c).astype(BF16)
                to_sibling(0, m, d2d_b).start()

            @pl.when((s == 2 * m + 1) & (i == n_i - 1))
            def _(m=m):
                from_sibling(0, m)
                slot = m % 2
                if m == 2:
                    over_ici(0, 0, send_b.at[0]).wait_send()
                for c in range(D_MODEL // row_chunk):
                    rows = slice(c * row_chunk, (c + 1) * row_chunk)
                    total = block_rows(c) + sib_b[m, rows, :].astype(F32)
                    if m < _N_CHIPS - 1:
                        send_b[slot, rows, :] = total.astype(BF16)
                    else:
                        gb_ref[rows, :] = total
                if m < _N_CHIPS - 1:
                    over_ici(0, m, send_b.at[slot]).start()

        @pl.when((s == N_DEV - 1) & (i == n_i - 1))
        def _():
            for arr in range(n_o + 1):
                for m in range(_N_CHIPS - 1):
                    from_ici(arr, m)
                rows = shapes[arr][0]

                def add(sl, arr=arr):
                    outs[arr][sl, :] = (outs[arr][sl, :] + ici[arr][0, sl, :].astype(F32)
                                        + ici[arr][1, sl, :].astype(F32) + ici[arr][2, sl, :].astype(F32))

                _for_row_chunks(rows, 16, add)
            for m in range(_N_CHIPS - 1):
                small_ici(m).wait_recv()
            gs_ref[...] = (gath[0] + gath[1]) + (gath[2] + gath[3])
            small_d2d.wait_send()
            for m in range(_N_CHIPS - 1):
                small_ici(m).wait_send()
            to_sibling(0, _N_CHIPS - 1, d2d_b).wait_send()
            over_ici(0, 1, send_b.at[1]).wait_send()
            over_ici(0, 2, send_b.at[0]).wait_send()
            for arr, src, mine in small:
                for m in range(_N_CHIPS):
                    to_sibling(arr, m, src.at[0]).wait_send()
                for m in range(_N_CHIPS - 1):
                    over_ici(arr, m, mine.at[m]).wait_send()

    any_spec = pl.BlockSpec(memory_space=pl.ANY)
    vmem = pl.BlockSpec(memory_space=pltpu.VMEM)
    half = lambda shp, n: pltpu.VMEM((n,) + tuple(shp), BF16)
    grid_spec = pltpu.PrefetchScalarGridSpec(
        num_scalar_prefetch=1, grid=(N_DEV, n_i),
        in_specs=[pl.BlockSpec((tm, SHARD_IN), lambda s, i, o: (i, o[s])),
                  pl.BlockSpec((D_MODEL, tm), lambda s, i, o: (0, i))] + [any_spec] * n_o + [vmem],
        out_specs=[vmem] * (n_o + 2),
        scratch_shapes=[
            pltpu.VMEM((D_MODEL, SHARD_IN), F32), half(shapes[0], _N_CHIPS), pltpu.VMEM(shapes[0], BF16),
            half(shapes[0], 2), half(shapes[0], _N_CHIPS - 1)]
        + [half(shp, n) for shp in shapes[1:] for n in (_N_CHIPS, _N_CHIPS, _N_CHIPS - 1)]
        + [pltpu.VMEM((_N_CHIPS + 1,) + small_partial.shape, F32),
           pltpu.SemaphoreType.DMA((sem_small + _N_CHIPS,)), pltpu.SemaphoreType.DMA((sem_small + _N_CHIPS,)),
           pltpu.SemaphoreType.DMA((n_o * _N_CHIPS,))],
    )
    res = _call(
        body, name="grad_w_in_scatter", grid_spec=grid_spec,
        out_shape=[jax.ShapeDtypeStruct(shp, F32) for shp in shapes] + [jax.ShapeDtypeStruct(small_partial.shape, F32)],
        compiler_params=_params(60, 2),
    )(order, dproj, u_t, *others, small_partial)
    return res[0], list(res[1:n_o + 1]), res[n_o + 1]


def _adam_update(g, w, m, v):
    m2 = ADAM_B1 * m + (1.0 - ADAM_B1) * g
    v2 = ADAM_B2 * v + (1.0 - ADAM_B2) * (g * g)
    m_hat = m2 / (1.0 - ADAM_B1 ** ADAM_STEP)
    v_hat = v2 / (1.0 - ADAM_B2 ** ADAM_STEP)
    delta = -ADAM_LR * (m_hat / (jnp.sqrt(v_hat) + ADAM_EPS) + ADAM_WD * w)
    return delta, m2, v2


def _adam_rows(g, w, m, v):
    rows, cols = w.shape
    tm = rows if rows % 256 else 256

    def fn(rv, cr, out):
        return list(_adam_update(*rv)), []

    outs, _ = _rowwise("adamw", fn, rows, tm, [(a, cols, 0) for a in (g, w, m, v)], [], [(cols, F32)] * 3, [], 32)
    return outs


_SMALL = ["norm_g", "hg_lb", "hg_norm_g", "s5_a_re", "s5_a_im", "s5_log_dt", "s5_b_re", "s5_b_im", "s5_c_re",
          "s5_c_im", "s5_d", "b_glu", "ple_norm_g", "final_norm_g"]
_BIG = ["w_in", "w_o_hg", "w_glu", "w_o_s5", "w_out", "w_ple", "w_ple_gate"]
_ORDER = ["norm_g", "w_in", "hg_lb", "hg_norm_g", "w_o_hg", "s5_a_re", "s5_a_im", "s5_log_dt", "s5_b_re", "s5_b_im",
          "s5_c_re", "s5_c_im", "s5_d", "w_glu", "b_glu", "w_o_s5", "w_out", "ple_norm_g", "w_ple", "w_ple_gate",
          "final_norm_g"]


def _pack_small(vals, tail=None):
    parts = []
    for name in _SMALL:
        flat = vals[name].reshape(-1).astype(F32)
        pad = (-flat.shape[0]) % 1024
        parts.append(jnp.pad(flat, (0, pad)))
    tail = jnp.zeros((0,), F32) if tail is None else tail.reshape(-1).astype(F32)
    parts.append(jnp.pad(tail, (0, 1024 - tail.shape[0])))
    return jnp.concatenate(parts).reshape(-1, 128)


def _unpack_small(packed, like):
    flat = packed.reshape(-1)
    out, off = {}, 0
    for name in _SMALL:
        size = like[name].size
        out[name] = flat[off:off + size].reshape(like[name].shape)
        off += size + (-size) % 1024
    return out


def _col_blocks(full):
    k = full.shape[0]
    return full.reshape(k, N_DEV, 128).transpose(1, 0, 2)


def _from_col_blocks(blocks):
    k = blocks.shape[1]
    return blocks.transpose(1, 0, 2).reshape(k, N_DEV * 128)


def kernel(x, p, norm_g, w_in, hg_lb, hg_norm_g, w_o_hg, s5_a_re, s5_a_im, s5_log_dt, s5_b_re, s5_b_im, s5_c_re, s5_c_im, s5_d, w_glu, b_glu, w_o_s5, w_out, ple_norm_g, w_ple, w_ple_gate, final_norm_g, loss_target, m_norm_g, m_w_in, m_hg_lb, m_hg_norm_g, m_w_o_hg, m_s5_a_re, m_s5_a_im, m_s5_log_dt, m_s5_b_re, m_s5_b_im, m_s5_c_re, m_s5_c_im, m_s5_d, m_w_glu, m_b_glu, m_w_o_s5, m_w_out, m_ple_norm_g, m_w_ple, m_w_ple_gate, m_final_norm_g, v_norm_g, v_w_in, v_hg_lb, v_hg_norm_g, v_w_o_hg, v_s5_a_re, v_s5_a_im, v_s5_log_dt, v_s5_b_re, v_s5_b_im, v_s5_c_re, v_s5_c_im, v_s5_d, v_w_glu, v_b_glu, v_w_o_s5, v_w_out, v_ple_norm_g, v_w_ple, v_w_ple_gate, v_final_norm_g):
    args = dict(locals())
    w = {n: args[n] for n in _ORDER}
    m = {n: args["m_" + n] for n in _ORDER}
    v = {n: args["v_" + n] for n in _ORDER}
    xs = x[0]
    ps = p[0, 0]
    tgt = loss_target[0]
    S = xs.shape[0]

    others = ["w_o_hg", "w_out", "w_ple_gate", "w_glu", "w_o_s5", "w_ple"]
    u_t, proj, all_b, gathered = _inproj_gather(xs, norm_g, w_in[0].astype(BF16), [w[n][0].astype(BF16) for n in others])
    wf_o_hg, wf_out, wf_pg = (g.reshape(D_MODEL, D_MODEL) for g in gathered[:3])
    wf_glu, wf_o_s5, wf_ple = (_from_col_blocks(g) for g in gathered[3:])

    lb = jax.nn.sigmoid(hg_lb[0:1] - hg_lb[1:2])
    s5_names = ["s5_a_re", "s5_a_im", "s5_log_dt", "s5_b_re", "s5_b_im", "s5_c_re", "s5_c_im", "s5_d"]
    build = lambda *a: _s5_matrices(*a, seg_len=S // S5_SEG)
    mats_f32, mats_vjp = jax.vjp(build, *[w[n][0] for n in s5_names])
    mats = dict(mats_f32, b_q=mats_f32["b_q"].astype(BF16), c_q=mats_f32["c_q"].astype(BF16))
    bias_glu = b_glu

    o, states = _hgrn_fwd(proj, lb)
    u_perm = _to_segments(proj[:, COL_US:COL_US + S5_WIDTH])
    zeros_state = jnp.zeros((S5_SEG, S5_COLS), F32)
    (h_ends,) = _s5_fwd_pass(u_perm, mats, zeros_state, False)
    h0 = _segment_starts(h_ends, mats["pow_r"], mats["pow_i"], False)
    y_perm, h_init, _ = _s5_fwd_pass(u_perm, mats, h0, True)
    ys = _from_segments(y_perm)
    y_hg, y_s5, glu, h1 = _stage_branches(o, proj, ys, xs, hg_norm_g, wf_o_hg, wf_glu, bias_glu, wf_o_s5, wf_out)

    dh1, (loss_acc, d_final_g, d_ple_g, d_w_ple, d_w_pg) = _stage_ple_loss(
        h1, ps, tgt, ple_norm_g, wf_ple, wf_pg, final_norm_g.reshape(1, D_MODEL))
    (d_gate_hg, d_gate_s5, d_yhg, d_ys5), (d_w_out,) = _stage_bwd_merge(dh1, y_hg, y_s5, proj, wf_out)
    (d_o, d_g_hg), (d_w_o_hg, d_hg_norm) = _stage_bwd_hg_path(d_yhg, o, proj, hg_norm_g, wf_o_hg)
    (d_ys, d_z), (d_w_o_s5, d_w_glu, d_b_glu) = _stage_bwd_s5_path(d_ys5, ys, glu, proj, wf_o_s5, wf_glu)
    dq, df, div, d_lb = _hgrn_bwd(proj, lb, d_o, states)
    dy_perm = _to_segments(d_ys)
    g_ends = _s5_bwd_ends(dy_perm, mats)
    g0 = _segment_starts(g_ends, mats["pow_r"], mats["pow_i"], True)
    du_perm, acc_p, acc_q, d_bq, d_cq_t, d_d = _s5_bwd_full(u_perm, dy_perm, h_init, g0, mats)
    d_us = _from_segments(du_perm)
    grad_x, dproj, d_norm_g = _stage_inproj_bwd([dq, df, div, d_g_hg, d_us, d_z, d_gate_hg, d_gate_s5], xs, dh1,
                                                norm_g, all_b)

    p_re, p_im = _split_cols(acc_p)
    q_re, q_im = _split_cols(acc_q)
    d_lam_r = (p_re + p_im)[0]
    d_lam_i = (q_im - q_re)[0]
    zero_row = jnp.zeros((S5_SEG, S5_COLS), F32)
    row_of = lambda re_part, im_part: zero_row.at[0].set(_join_cols(re_part[None], im_part[None])[0])
    zeros_q = jnp.zeros_like(d_lam_r)
    cot = dict(
        lam_r=row_of(d_lam_r, zeros_q), lam_i=row_of(zeros_q, d_lam_i),
        b_q=d_bq, c_q=d_cq_t.transpose(0, 2, 1), d_row=d_d,
        pow_r=jnp.zeros_like(mats["pow_r"]), pow_i=jnp.zeros_like(mats["pow_i"]),
    )
    d_s5 = mats_vjp(cot)

    s_lb = lb * (1.0 - lb)
    d_hg_lb = jnp.concatenate([d_lb * s_lb, -d_lb * s_lb], axis=0)
    small_g = dict(norm_g=d_norm_g, hg_lb=d_hg_lb, hg_norm_g=d_hg_norm, b_glu=d_b_glu, ple_norm_g=d_ple_g,
                   final_norm_g=d_final_g)
    for name, g in zip(s5_names, d_s5):
        small_g[name] = g
    pk = lambda d: _pack_small({n: d[n] for n in _SMALL})
    blocks = [t.reshape(N_DEV, 128, D_MODEL) for t in (d_w_o_hg, d_w_out, d_w_pg)]
    blocks += [_col_blocks(t) for t in (d_w_glu, d_w_o_s5, d_w_ple)]
    partial = _pack_small({n: small_g[n] for n in _SMALL}, tail=loss_acc[0, 0:1])
    g_w_in, g_others, sg = _grad_w_in_scatter(dproj, u_t, [t.astype(BF16) for t in blocks], partial)
    sd, sm, sv = _adam_rows(sg, pk(w), pk(m), pk(v))
    like = {n: w[n] for n in _SMALL}
    out_g, out_d, out_m, out_v = (_unpack_small(t, like) for t in (sg, sd, sm, sv))
    big_g = dict(zip(others, g_others), w_in=g_w_in)
    for name in _BIG:
        shape = w[name].shape
        g2 = big_g[name]
        d2, m2, v2 = _adam_rows(g2, w[name][0], m[name][0], v[name][0])
        out_g[name], out_d[name], out_m[name], out_v[name] = (t.reshape(shape) for t in (g2, d2, m2, v2))

    loss = sg[sg.shape[0] - 8, 0]
    return (loss, grad_x[None], *[out_g[n] for n in _ORDER], *[out_d[n] for n in _ORDER],
            *[out_m[n] for n in _ORDER], *[out_v[n] for n in _ORDER])
```

```python
import functools
import math

import jax
import jax.numpy as jnp
from jax import lax
from jax.experimental import pallas as pl
from jax.experimental.pallas import tpu as pltpu

F32 = jnp.float32
BF16 = jnp.bfloat16

D_MODEL = 1024
N_DEV = 8
IN_COLS = 7168
SHARD_IN = IN_COLS // N_DEV
HG_HEADS = 8
HG_DIM = 128
HG_CHUNK = 64
HG_SUPER_FWD = 256
HG_SUPER_BWD = 128
HG_HEADS_PER_STEP = 8
S5_WIDTH = 512
S5_GROUPS = 32
S5_STATE = 64
S5_CH = 16
S5_SEG = 8
S5_QUADS = 4
S5_QCOLS = 1024
S5_COLS = S5_QUADS * S5_QCOLS
S5_TILE_STEPS = 64
S5_UNROLL = 8
NORM_EPS = 1e-6
ADAM_LR = 0.001
ADAM_B1 = 0.9
ADAM_B2 = 0.999
ADAM_EPS = 1e-08
ADAM_WD = 0.01
ADAM_STEP = 10
MIB = 1024 * 1024
MESH = pl.DeviceIdType.MESH

COL_Q, COL_F, COL_I, COL_G, COL_US, COL_ZS, COL_GH, COL_GS = 0, 1024, 2048, 3072, 4096, 4608, 5120, 6144


def _call(body, **kw):
    return pl.pallas_call(body, **kw)


def _params(vmem_mb, n_grid=1):
    return pltpu.CompilerParams(
        dimension_semantics=("arbitrary",) * n_grid, vmem_limit_bytes=vmem_mb * MIB
    )


def _bdot(a, b):
    return jnp.dot(a.astype(BF16), b.astype(BF16), preferred_element_type=F32)


def _bdot_nt(a, b):
    return lax.dot_general(a.astype(BF16), b.astype(BF16), (((1,), (1,)), ((), ())), preferred_element_type=F32)


def _bdot_tn(a, b):
    return lax.dot_general(a.astype(BF16), b.astype(BF16), (((0,), (0,)), ((), ())), preferred_element_type=F32)


def _sigmoid(x):
    return jax.nn.sigmoid(x)


def _silu(x):
    return x * _sigmoid(x)


def _dsilu(x):
    s = _sigmoid(x)
    return s * (1.0 + x * (1.0 - s))


_GELU_C = math.sqrt(2.0 / math.pi)


def _gelu(x):
    return 0.5 * x * (1.0 + jnp.tanh(_GELU_C * (x + 0.044715 * x * x * x)))


def _dgelu(x):
    t = jnp.tanh(_GELU_C * (x + 0.044715 * x * x * x))
    return 0.5 * (1.0 + t) + 0.5 * x * (1.0 - t * t) * _GELU_C * (1.0 + 3.0 * 0.044715 * x * x)


def _rms_fwd(x, g):
    r = lax.rsqrt(jnp.mean(x * x, axis=-1, keepdims=True) + NORM_EPS)
    n = x * r
    return n * g, n, r


def _rms_bwd(dy, n, r, g):
    dn = dy * g
    dx = r * (dn - n * jnp.mean(dn * n, axis=-1, keepdims=True))
    return dx, jnp.sum(dy * n, axis=0, keepdims=True)


def _head_rms_fwd(o, g):
    ns, rs = [], []
    for h in range(HG_HEADS):
        oh = o[:, h * HG_DIM:(h + 1) * HG_DIM]
        r = lax.rsqrt(jnp.mean(oh * oh, axis=-1, keepdims=True) + NORM_EPS)
        ns.append(oh * r)
        rs.append(r)
    n = jnp.concatenate(ns, axis=1)
    return n * g, n, rs


def _head_rms_bwd(dy, n, rs, g):
    dn = dy * g
    dxs = []
    for h in range(HG_HEADS):
        sl = slice(h * HG_DIM, (h + 1) * HG_DIM)
        dxs.append(rs[h] * (dn[:, sl] - n[:, sl] * jnp.mean(dn[:, sl] * n[:, sl], axis=-1, keepdims=True)))
    return jnp.concatenate(dxs, axis=1), jnp.sum(dy * n, axis=0, keepdims=True)


def _rowwise(name, fn, n_rows, tm, rows, consts, out_rows, out_accs, vmem_mb, parts=1, into=None):
    n_r, n_c, n_or, n_oa = len(rows), len(consts), len(out_rows), len(out_accs)
    n_or += into is not None
    tp = tm // parts

    def body(*refs):
        r_refs = refs[:n_r]
        c_refs = refs[n_r:n_r + n_c]
        n_in = n_r + n_c + (into is not None)
        or_refs = refs[n_in:n_in + n_or]
        oa_refs = refs[n_in + n_or:]

        if n_oa:
            @pl.when(pl.program_id(0) == 0)
            def _():
                for ref in oa_refs:
                    ref[...] = jnp.zeros(ref.shape, ref.dtype)

        for part in range(parts):
            sl = slice(part * tp, (part + 1) * tp)
            outs, accs = fn([r[sl, :] for r in r_refs], c_refs, [o.at[sl, :] for o in or_refs])
            for ref, v in zip(or_refs, outs):
                if v is not None:
                    ref[sl, :] = v.astype(ref.dtype)
            for ref, v in zip(oa_refs, accs):
                ref[...] += v.astype(ref.dtype)

    in_specs = [pl.BlockSpec((tm, w), functools.partial(lambda i, c: (i, c), c=cb)) for (_, w, cb) in rows]
    in_specs += [pl.BlockSpec(c.shape, functools.partial(lambda i, nd: (0,) * nd, nd=c.ndim),
                              pipeline_mode=pl.Buffered(1)) for c in consts]
    out_specs = [pl.BlockSpec((tm, w), lambda i: (i, 0)) for (w, _) in out_rows]
    out_shape = [jax.ShapeDtypeStruct((n_rows, w), dt) for (w, dt) in out_rows]
    operands = [a for (a, _, _) in rows] + list(consts)
    aliases = {}
    if into is not None:
        buf, w, cb = into
        in_specs.append(pl.BlockSpec(memory_space=pl.ANY))
        out_specs.append(pl.BlockSpec((tm, w), functools.partial(lambda i, c: (i, c), c=cb)))
        out_shape.append(jax.ShapeDtypeStruct(buf.shape, buf.dtype))
        aliases = {len(operands): len(out_shape) - 1}
        operands.append(buf)
    out_specs += [pl.BlockSpec(s, functools.partial(lambda i, nd: (0,) * nd, nd=len(s))) for (s, _) in out_accs]
    out_shape += [jax.ShapeDtypeStruct(s, dt) for (s, dt) in out_accs]
    res = _call(
        body, name=name, grid=(n_rows // tm,), in_specs=in_specs, out_specs=out_specs, out_shape=out_shape,
        input_output_aliases=aliases, compiler_params=_params(vmem_mb),
    )(*operands)
    return res[:n_or], res[n_or:]


def _stage_branches(o, proj, ys, x, hg_norm_g, w_o_hg, w_glu, b_glu, w_o_s5, w_out):
    S = x.shape[0]

    def fn(rv, cr, out):
        o_b, g_hg, z_s, gate_hg, gate_s5, ys_b, x_b = rv
        gn_ref, wohg_ref, wglu_ref, bglu_ref, wos5_ref, wout_ref = cr
        on, _, _ = _head_rms_fwd(o_b, gn_ref[...])
        a = on * _silu(g_hg)
        y_hg = jnp.dot(a.astype(BF16), wohg_ref[...], preferred_element_type=F32)
        gl = _gelu(ys_b)
        glu = jnp.dot(gl.astype(BF16), wglu_ref[...], preferred_element_type=F32) + bglu_ref[...]
        ys2 = glu[:, :S5_WIDTH] * _sigmoid(glu[:, S5_WIDTH:]) * _silu(z_s)
        y_s5 = jnp.dot(ys2.astype(BF16), wos5_ref[...], preferred_element_type=F32)
        merged = _sigmoid(gate_hg) * y_hg + _sigmoid(gate_s5) * y_s5
        h1 = x_b + jnp.dot(merged.astype(BF16), wout_ref[...], preferred_element_type=F32)
        return [y_hg, y_s5, glu, h1], []

    rows = [(o, D_MODEL, 0), (proj, D_MODEL, COL_G // D_MODEL), (proj, S5_WIDTH, COL_ZS // S5_WIDTH),
            (proj, D_MODEL, COL_GH // D_MODEL), (proj, D_MODEL, COL_GS // D_MODEL), (ys, S5_WIDTH, 0), (x, D_MODEL, 0)]
    (y_hg, y_s5, glu, h1), _ = _rowwise(
        "branches", fn, S, 256, rows, [hg_norm_g, w_o_hg, w_glu, b_glu, w_o_s5, w_out],
        [(D_MODEL, F32)] * 4, [], 56)
    return y_hg, y_s5, glu, h1


def _stage_ple_loss(h1, p, target, ple_norm_g, w_ple, w_ple_gate, final_norm_g):
    S = h1.shape[0]

    def fn(rv, cr, out):
        h1_b, p_b, t_b = rv
        gp_ref, wple_ref, wpg_ref, gf_ref = cr
        n2g, n2, r2 = _rms_fwd(h1_b, gp_ref[...])
        z = jnp.dot(n2g.astype(BF16), wpg_ref[...], preferred_element_type=F32)
        gate = _sigmoid(z)
        pe = jnp.dot(p_b.astype(BF16), wple_ref[...], preferred_element_type=F32)
        h2 = h1_b + pe * gate
        y, nf, rf = _rms_fwd(h2, gf_ref[...])
        err = y - t_b
        loss_rows = 0.5 * jnp.mean(err * err, axis=-1, keepdims=True)
        loss_inc = jnp.broadcast_to(jnp.sum(loss_rows, axis=0, keepdims=True), (1, 128))
        dy = err * (1.0 / D_MODEL)
        dh2, d_gf = _rms_bwd(dy, nf, rf, gf_ref[...])
        d_pe = dh2 * gate
        dz = dh2 * pe * gate * (1.0 - gate)
        d_wple = _bdot_tn(p_b, d_pe)
        d_wpg = _bdot_tn(n2g, dz)
        dn2g = _bdot_nt(dz, wpg_ref[...])
        dh1n, d_gp = _rms_bwd(dn2g, n2, r2, gp_ref[...])
        return [dh2 + dh1n], [loss_inc, d_gf, d_gp, d_wple, d_wpg]

    (dh1,), accs = _rowwise(
        "ple_loss", fn, S, 512, [(h1, D_MODEL, 0), (p, 256, 0), (target, D_MODEL, 0)],
        [ple_norm_g, w_ple, w_ple_gate, final_norm_g], [(D_MODEL, F32)],
        [((1, 128), F32), ((1, D_MODEL), F32), ((1, D_MODEL), F32), ((256, D_MODEL), F32), ((D_MODEL, D_MODEL), F32)], 56,
        parts=2)
    return dh1, accs


def _stage_bwd_merge(dh1, y_hg, y_s5, proj, w_out, dproj):
    S = dh1.shape[0]

    def fn(rv, cr, out):
        dh1_b, yhg, ys5, gate_hg, gate_s5 = rv
        (wout_ref,) = cr
        sg_h, sg_s = _sigmoid(gate_hg), _sigmoid(gate_s5)
        merged = sg_h * yhg + sg_s * ys5
        d_wout = _bdot_tn(merged, dh1_b)
        d_merged = _bdot_nt(dh1_b, wout_ref[...])
        d_gate_hg = d_merged * yhg * sg_h * (1.0 - sg_h)
        d_gate_s5 = d_merged * ys5 * sg_s * (1.0 - sg_s)
        return [d_gate_s5, d_merged * sg_h, d_merged * sg_s, d_gate_hg], [d_wout]

    rows = [(dh1, D_MODEL, 0), (y_hg, D_MODEL, 0), (y_s5, D_MODEL, 0), (proj, D_MODEL, COL_GH // D_MODEL),
            (proj, D_MODEL, COL_GS // D_MODEL)]
    outs, accs = _rowwise("bwd_merge", fn, S, 512, rows, [w_out], [(D_MODEL, BF16)] * 3,
                          [((D_MODEL, D_MODEL), F32)], 56, parts=2, into=(dproj, D_MODEL, COL_GH // D_MODEL))
    return outs, accs


def _stage_bwd_hg_path(d_yhg, o, proj, hg_norm_g, w_o_hg, dproj):
    S = o.shape[0]

    def fn(rv, cr, out):
        d_yhg_b, o_b, g_hg = rv
        gn_ref, wohg_ref = cr
        ong, on, rs = _head_rms_fwd(o_b, gn_ref[...])
        sil = _silu(g_hg)
        d_wohg = _bdot_tn(ong * sil, d_yhg_b)
        d_a = _bdot_nt(d_yhg_b, wohg_ref[...])
        d_g_hg = d_a * ong * _dsilu(g_hg)
        d_o, d_gn = _head_rms_bwd(d_a * sil, on, rs, gn_ref[...])
        return [d_o, d_g_hg], [d_wohg, d_gn]

    rows = [(d_yhg, D_MODEL, 0), (o, D_MODEL, 0), (proj, D_MODEL, COL_G // D_MODEL)]
    outs, accs = _rowwise("bwd_hg_path", fn, S, 512, rows, [hg_norm_g, w_o_hg], [(D_MODEL, BF16)],
                          [((D_MODEL, D_MODEL), F32), ((1, D_MODEL), F32)], 56, parts=2,
                          into=(dproj, D_MODEL, COL_G // D_MODEL))
    return outs, accs


def _stage_bwd_s5_path(d_ys5, ys, glu, proj, w_o_s5, w_glu, dproj):
    S = ys.shape[0]

    def fn(rv, cr, out):
        d_ys5_b, ys_b, glu_b, z_s = rv
        wos5_ref, wglu_ref = cr
        ga, gb = glu_b[:, :S5_WIDTH], glu_b[:, S5_WIDTH:]
        sgb, silz = _sigmoid(gb), _silu(z_s)
        ys2 = ga * sgb * silz
        d_wos5 = _bdot_tn(ys2, d_ys5_b)
        d_ys2 = _bdot_nt(d_ys5_b, wos5_ref[...])
        d_ga = d_ys2 * sgb * silz
        d_gb = d_ys2 * ga * sgb * (1.0 - sgb) * silz
        d_z = d_ys2 * ga * sgb * _dsilu(z_s)
        d_glu = jnp.concatenate([d_ga, d_gb], axis=1)
        gl = _gelu(ys_b)
        d_wglu = _bdot_tn(gl, d_glu)
        d_bglu = jnp.sum(d_glu, axis=0, keepdims=True)
        d_gl = _bdot_nt(d_glu, wglu_ref[...])
        return [d_gl * _dgelu(ys_b), d_z], [d_wos5, d_wglu, d_bglu]

    rows = [(d_ys5, D_MODEL, 0), (ys, S5_WIDTH, 0), (glu, D_MODEL, 0), (proj, S5_WIDTH, COL_ZS // S5_WIDTH)]
    outs, accs = _rowwise(
        "bwd_s5_path", fn, S, 512, rows, [w_o_s5, w_glu], [(S5_WIDTH, F32)],
        [((S5_WIDTH, D_MODEL), F32), ((S5_WIDTH, D_MODEL), F32), ((1, D_MODEL), F32)], 48, parts=2,
        into=(dproj, S5_WIDTH, COL_ZS // S5_WIDTH))
    return outs, accs


def _stage_inproj_bwd(dproj, x, dh1, norm_g, w_in_t):
    S = x.shape[0]

    def fn(rv, cr, out):
        g_ref, w_ref = cr
        dproj_b, x_b, dh1_b = rv
        d_u = jnp.dot(dproj_b, w_ref[...], preferred_element_type=F32)
        _, n, r = _rms_fwd(x_b, g_ref[...])
        dx, d_g = _rms_bwd(d_u, n, r, g_ref[...])
        return [dh1_b + dx], [d_g]

    rows = [(dproj, IN_COLS, 0), (x, D_MODEL, 0), (dh1, D_MODEL, 0)]
    (grad_x,), (d_g,) = _rowwise("inproj_bwd", fn, S, 512, rows, [norm_g, w_in_t], [(D_MODEL, F32)],
                                 [((1, D_MODEL), F32)], 56)
    return grad_x, d_g


def _chunk_row(shape):
    return lax.broadcasted_iota(jnp.int32, shape, 0) & (HG_CHUNK - 1)


def _chunk_cumsum(x):
    r_in = _chunk_row(x.shape)
    s = 1
    while s < HG_CHUNK:
        x = x + jnp.where(r_in >= s, pltpu.roll(x, s, 0), 0.0)
        s *= 2
    return x


def _chunk_suffix_sum(x):
    n = x.shape[0]
    r_in = _chunk_row(x.shape)
    s = 1
    while s < HG_CHUNK:
        x = x + jnp.where(r_in < HG_CHUNK - s, pltpu.roll(x, n - s, 0), 0.0)
        s *= 2
    return x


def _hgrn_prep(q, fl, lb):
    sup = q.shape[0]
    nc = sup // HG_CHUNK
    sig = _sigmoid(fl)
    f = lb + (1.0 - lb) * sig
    k = (1.0 - lb) * (1.0 - sig)
    b = _chunk_cumsum(jnp.log(f))
    b3 = b.reshape(nc, HG_CHUNK, HG_DIM)
    row3 = lax.broadcasted_iota(jnp.int32, b3.shape, 1)
    pick = lambda r: jnp.sum(jnp.where(row3 == r, b3, 0.0), axis=1, keepdims=True)
    b_mid = pick(HG_CHUNK // 2 - 1)
    b_last = pick(HG_CHUNK - 1)
    flat = lambda t: t.reshape(sup, HG_DIM)
    e_qa = flat(jnp.exp(b3 - b_mid))
    e_ka = flat(jnp.exp(b_mid - b3))
    e_qd = jnp.exp(b)
    e_kd = flat(jnp.exp(b_last - b3))
    dc = jnp.exp(b_last)
    return sig, f, k, e_qa, e_ka, e_qd, e_kd, dc


def _hgrn_mask(sup):
    r = lax.broadcasted_iota(jnp.int32, (sup, sup), 0)
    c = lax.broadcasted_iota(jnp.int32, (sup, sup), 1)
    shift = HG_CHUNK.bit_length() - 1
    return (jnp.right_shift(r, shift) == jnp.right_shift(c, shift)) & (r >= c)


def _hgrn_fwd(proj, lb):
    S = proj.shape[0]
    sup = HG_SUPER_FWD
    nb = S // sup
    nc = sup // HG_CHUNK
    hp = HG_HEADS_PER_STEP
    wide = hp * HG_DIM

    def body(q_ref, f_ref, iv_ref, lb_ref, o_ref, st_ref, state):
        @pl.when(pl.program_id(1) == 0)
        def _():
            state[...] = jnp.zeros(state.shape, F32)

        mask = _hgrn_mask(sup)
        for hh in range(hp):
            lanes = slice(hh * HG_DIM, (hh + 1) * HG_DIM)
            q, iv = q_ref[:, lanes], iv_ref[:, lanes]
            _, _, k, e_qa, e_ka, e_qd, e_kd, dc = _hgrn_prep(q, f_ref[:, lanes], lb_ref[:, lanes])
            scores = jnp.where(mask, _bdot_nt(q * e_qa, k * e_ka), 0.0)
            o_intra = _bdot(scores, iv)
            qd, kd = q * e_qd, k * e_kd
            for c in range(nc):
                sl = slice(c * HG_CHUNK, (c + 1) * HG_CHUNK)
                st = state[hh]
                st_ref[hh, c] = st
                o_ref[sl, lanes] = o_intra[sl] + _bdot_nt(qd[sl], st)
                state[hh] = dc[c] * st + _bdot_tn(iv[sl], kd[sl])

    blk = lambda base: pl.BlockSpec((sup, wide), functools.partial(lambda h, i, b: (i, b + h), b=base // wide))
    return _call(
        body, name="hgrn_fwd", grid=(HG_HEADS // hp, nb),
        in_specs=[blk(COL_Q), blk(COL_F), blk(COL_I), pl.BlockSpec((1, wide), lambda h, i: (0, h))],
        out_specs=[pl.BlockSpec((sup, wide), lambda h, i: (i, h)),
                   pl.BlockSpec((hp, nc, HG_DIM, HG_DIM), lambda h, i: (h, i, 0, 0))],
        out_shape=[jax.ShapeDtypeStruct((S, D_MODEL), F32),
                   jax.ShapeDtypeStruct((HG_HEADS, S // HG_CHUNK, HG_DIM, HG_DIM), F32)],
        scratch_shapes=[pltpu.VMEM((hp, HG_DIM, HG_DIM), F32)],
        compiler_params=_params(40, 2),
    )(proj, proj, proj, lb)


def _hgrn_bwd(proj, lb, d_o, states, dproj):
    S = proj.shape[0]
    sup = HG_SUPER_BWD
    nb = S // sup
    nc = sup // HG_CHUNK
    hp = HG_HEADS_PER_STEP
    wide = hp * HG_DIM

    assert hp == HG_HEADS

    def body(q_ref, f_ref, iv_ref, lb_ref, do_ref, st_ref, _, out_ref, dlb_ref, dstate):
        @pl.when(pl.program_id(1) == 0)
        def _():
            dstate[...] = jnp.zeros(dstate.shape, F32)
            dlb_ref[...] = jnp.zeros(dlb_ref.shape, F32)

        dq_ref, df_ref, div_ref = (out_ref.at[:, base:base + wide] for base in (COL_Q, COL_F, COL_I))

        mask = _hgrn_mask(sup)
        for hh in range(hp):
            lanes = slice(hh * HG_DIM, (hh + 1) * HG_DIM)
            q, iv, do, lb_v = q_ref[:, lanes], iv_ref[:, lanes], do_ref[:, lanes], lb_ref[:, lanes]
            sig, f, k, e_qa, e_ka, e_qd, e_kd, dc = _hgrn_prep(q, f_ref[:, lanes], lb_v)
            qa, ka, qd, kd = q * e_qa, k * e_ka, q * e_qd, k * e_kd
            scores = jnp.where(mask, _bdot_nt(qa, ka), 0.0)
            d_scores = jnp.where(mask, _bdot_nt(do, iv), 0.0)
            d_iv_intra = _bdot_tn(scores, do)
            d_qa = _bdot(d_scores, ka)
            d_ka = _bdot_tn(d_scores, qa)
            d_qd, d_kd, d_last = [None] * nc, [None] * nc, [None] * nc
            for c in reversed(range(nc)):
                sl = slice(c * HG_CHUNK, (c + 1) * HG_CHUNK)
                st = st_ref[hh, c]
                ds = dstate[hh]
                d_qd[c] = _bdot(do[sl], st)
                d_kd[c] = _bdot(iv[sl], ds)
                div_ref[sl, lanes] = (d_iv_intra[sl] + _bdot_nt(kd[sl], ds)).astype(div_ref.dtype)
                d_last[c] = (jnp.sum(ds * st, axis=0, keepdims=True) * dc[c]
                             + jnp.sum(d_kd[c] * kd[sl], axis=0, keepdims=True))
                dstate[hh] = dc[c] * ds + _bdot_tn(do[sl], qd[sl])
            d_qd = jnp.concatenate(d_qd, axis=0)
            d_kd = jnp.concatenate(d_kd, axis=0)
            d_b = d_qa * qa - d_ka * ka + d_qd * qd - d_kd * kd
            last_rows = jnp.concatenate([jnp.broadcast_to(t, (HG_CHUNK, HG_DIM)) for t in d_last], axis=0)
            d_b = d_b + jnp.where(_chunk_row(d_b.shape) == HG_CHUNK - 1, last_rows, 0.0)
            d_logf = _chunk_suffix_sum(d_b)
            d_k = d_ka * e_ka + d_kd * e_kd
            g_f = d_logf / f
            d_sig = (g_f - d_k) * (1.0 - lb_v)
            dq_ref[:, lanes] = (d_qa * e_qa + d_qd * e_qd).astype(dq_ref.dtype)
            df_ref[:, lanes] = (d_sig * sig * (1.0 - sig)).astype(df_ref.dtype)
            d_lb = jnp.sum((g_f - d_k) * (1.0 - sig), axis=0, keepdims=True)
            dlb_ref[:, lanes] += jnp.broadcast_to(d_lb, (8, HG_DIM))

    rev = lambda i: nb - 1 - i
    blk = lambda base: pl.BlockSpec((sup, wide), functools.partial(lambda h, i, b: (rev(i), b + h), b=base // wide))
    dproj, dlb = _call(
        body, name="hgrn_bwd", grid=(HG_HEADS // hp, nb),
        in_specs=[blk(COL_Q), blk(COL_F), blk(COL_I), pl.BlockSpec((1, wide), lambda h, i: (0, h)),
                  pl.BlockSpec((sup, wide), lambda h, i: (rev(i), h)),
                  pl.BlockSpec((hp, nc, HG_DIM, HG_DIM), lambda h, i: (h, rev(i), 0, 0)),
                  pl.BlockSpec(memory_space=pl.ANY)],
        out_specs=[pl.BlockSpec((sup, 3 * wide), lambda h, i: (rev(i), 0)), pl.BlockSpec((8, wide), lambda h, i: (0, h))],
        out_shape=[jax.ShapeDtypeStruct(dproj.shape, dproj.dtype), jax.ShapeDtypeStruct((8, D_MODEL), F32)],
        scratch_shapes=[pltpu.VMEM((hp, HG_DIM, HG_DIM), F32)],
        input_output_aliases={6: 0},
        compiler_params=_params(40, 2),
    )(proj, proj, proj, lb, d_o, states, dproj)
    return dproj, dlb[0:1]


def _s5_matrices(a_re, a_im, log_dt, b_re, b_im, c_re, c_im, d, seg_len):
    dt = jnp.exp(log_dt)[:, None]
    mag = jnp.exp(a_re * dt)
    lr, li = mag * jnp.cos(a_im * dt), mag * jnp.sin(a_im * dt)
    den = a_re * a_re + a_im * a_im
    nr = lr - 1.0
    sr = (nr * a_re + li * a_im) / den
    si = (li * a_re - nr * a_im) / den
    bbr = sr[..., None] * b_re - si[..., None] * b_im
    bbi = sr[..., None] * b_im + si[..., None] * b_re
    eye = jnp.eye(8, dtype=F32)

    def quad_cols(v):
        return v.reshape(S5_QUADS, 8 * S5_STATE)

    def lam_row(re_part, im_part):
        row = jnp.concatenate([quad_cols(re_part), quad_cols(im_part)], axis=1).reshape(1, S5_COLS)
        return jnp.broadcast_to(row, (S5_SEG, S5_COLS))

    def b_mat(bb):
        t = bb.reshape(S5_QUADS, 8, S5_STATE, S5_CH)
        return jnp.einsum("qgnc,gh->qgchn", t, eye).reshape(S5_QUADS, 8 * S5_CH, 8 * S5_STATE)

    def c_mat(cc):
        t = cc.reshape(S5_QUADS, 8, S5_CH, S5_STATE)
        return jnp.einsum("qgcn,gh->qgnhc", t, eye).reshape(S5_QUADS, 8 * S5_STATE, 8 * S5_CH)

    ang = a_im * dt * seg_len
    magp = jnp.exp(a_re * dt * seg_len)
    lpr, lpi = magp * jnp.cos(ang), magp * jnp.sin(ang)
    return dict(
        lam_r=lam_row(lr, lr), lam_i=lam_row(-li, li),
        b_q=jnp.concatenate([b_mat(bbr), b_mat(bbi)], axis=2),
        c_q=jnp.concatenate([c_mat(c_re), -c_mat(c_im)], axis=1),
        d_row=d.reshape(1, S5_WIDTH), pow_r=quad_cols(lpr), pow_i=quad_cols(lpi),
    )


def _s5_parts(v):
    half = S5_QCOLS // 2
    return tuple(v[:, k * half:(k + 1) * half] for k in range(2 * S5_QUADS))


def _s5_advance(parts, lr_ref, li_ref, x_ref, sl, conj):
    half = S5_QCOLS // 2
    out = []
    for q in range(S5_QUADS):
        re_c = slice(q * S5_QCOLS, q * S5_QCOLS + half)
        im_c = slice(q * S5_QCOLS + half, (q + 1) * S5_QCOLS)
        lr, li = lr_ref[:, re_c], li_ref[:, im_c]
        hr, hi = parts[2 * q], parts[2 * q + 1]
        if conj:
            out += [lr * hr + li * hi + x_ref[sl, re_c], lr * hi - li * hr + x_ref[sl, im_c]]
        else:
            out += [lr * hr - li * hi + x_ref[sl, re_c], lr * hi + li * hr + x_ref[sl, im_c]]
    return tuple(out)


def _scan_loop(step, init):
    def trip(o, carry):
        for j in range(S5_UNROLL):
            carry = step(o * S5_UNROLL + j, carry)
        return carry

    return lax.fori_loop(0, S5_TILE_STEPS // S5_UNROLL, trip, init)


def _s5_store(ref, sl, parts):
    half = S5_QCOLS // 2
    for k, v in enumerate(parts):
        ref[sl, k * half:(k + 1) * half] = v


def _s5_fwd_pass(u_perm, mats, h0, with_output):
    S = u_perm.shape[0]
    rows = S5_TILE_STEPS * S5_SEG
    nt = S // rows

    def body(*refs):
        if with_output:
            u_ref, b_ref, lr_ref, li_ref, h0_ref, c_ref, d_ref, y_ref, hinit_ref, hend_ref, xs, hcar = refs
        else:
            u_ref, b_ref, lr_ref, li_ref, h0_ref, hend_ref, xs, hcar = refs

        @pl.when(pl.program_id(0) == 0)
        def _():
            hcar[...] = h0_ref[...]

        if with_output:
            hinit_ref[...] = hcar[...]
        u = u_ref[...]
        ub = u.astype(BF16)
        for q in range(S5_QUADS):
            xs[:, q * S5_QCOLS:(q + 1) * S5_QCOLS] = jnp.dot(ub[:, q * 128:(q + 1) * 128], b_ref[q], preferred_element_type=F32)

        def step(t, h):
            sl = pl.ds(pl.multiple_of(t * S5_SEG, S5_SEG), S5_SEG)
            hn = _s5_advance(h, lr_ref, li_ref, xs, sl, False)
            _s5_store(xs, sl, hn)
            return hn

        h = _scan_loop(step, _s5_parts(hcar[...]))
        _s5_store(hcar, slice(None), h)
        _s5_store(hend_ref, slice(None), h)
        if with_output:
            ys = [jnp.dot(xs[:, q * S5_QCOLS:(q + 1) * S5_QCOLS].astype(BF16), c_ref[q], preferred_element_type=F32)
                  for q in range(S5_QUADS)]
            y_ref[...] = jnp.concatenate(ys, axis=1) + d_ref[...] * u

    full = lambda a: pl.BlockSpec(a.shape, functools.partial(lambda i, nd: (0,) * nd, nd=a.ndim))
    ins = [u_perm, mats["b_q"], mats["lam_r"], mats["lam_i"], h0]
    in_specs = [pl.BlockSpec((rows, S5_WIDTH), lambda i: (i, 0))] + [full(a) for a in ins[1:]]
    out_specs = [pl.BlockSpec((S5_SEG, S5_COLS), lambda i: (0, 0))]
    out_shape = [jax.ShapeDtypeStruct((S5_SEG, S5_COLS), F32)]
    if with_output:
        ins += [mats["c_q"], mats["d_row"]]
        in_specs += [full(mats["c_q"]), full(mats["d_row"])]
        out_specs = [pl.BlockSpec((rows, S5_WIDTH), lambda i: (i, 0)),
                     pl.BlockSpec((None, S5_SEG, S5_COLS), lambda i: (i, 0, 0))] + out_specs
        out_shape = [jax.ShapeDtypeStruct((S, S5_WIDTH), F32), jax.ShapeDtypeStruct((nt, S5_SEG, S5_COLS), F32)] + out_shape
    return _call(
        body, name="s5_fwd_y" if with_output else "s5_fwd_ends", grid=(nt,), in_specs=in_specs, out_specs=out_specs,
        out_shape=out_shape,
        scratch_shapes=[pltpu.VMEM((rows, S5_COLS), F32), pltpu.VMEM((S5_SEG, S5_COLS), F32)],
        compiler_params=_params(40),
    )(*ins)


def _s5_bwd_ends(dy_perm, mats):
    S = dy_perm.shape[0]
    rows = S5_TILE_STEPS * S5_SEG
    nt = S // rows

    def body(dy_ref, c_ref, lr_ref, li_ref, gend_ref, gs, gcar):
        @pl.when(pl.program_id(0) == 0)
        def _():
            gcar[...] = jnp.zeros(gcar.shape, F32)

        dyb = dy_ref[...].astype(BF16)
        for q in range(S5_QUADS):
            gs[:, q * S5_QCOLS:(q + 1) * S5_QCOLS] = lax.dot_general(
                dyb[:, q * 128:(q + 1) * 128], c_ref[q], (((1,), (1,)), ((), ())), preferred_element_type=F32)

        def step(k, g):
            t = S5_TILE_STEPS - 1 - k
            sl = pl.ds(pl.multiple_of(t * S5_SEG, S5_SEG), S5_SEG)
            return _s5_advance(g, lr_ref, li_ref, gs, sl, True)

        g = _scan_loop(step, _s5_parts(gcar[...]))
        _s5_store(gcar, slice(None), g)
        _s5_store(gend_ref, slice(None), g)

    full = lambda a: pl.BlockSpec(a.shape, functools.partial(lambda i, nd: (0,) * nd, nd=a.ndim))
    return _call(
        body, name="s5_bwd_ends", grid=(nt,),
        in_specs=[pl.BlockSpec((rows, S5_WIDTH), lambda i: (nt - 1 - i, 0)), full(mats["c_q"]), full(mats["lam_r"]),
                  full(mats["lam_i"])],
        out_specs=pl.BlockSpec((S5_SEG, S5_COLS), lambda i: (0, 0)),
        out_shape=jax.ShapeDtypeStruct((S5_SEG, S5_COLS), F32),
        scratch_shapes=[pltpu.VMEM((rows, S5_COLS), F32), pltpu.VMEM((S5_SEG, S5_COLS), F32)],
        compiler_params=_params(40),
    )(dy_perm, mats["c_q"], mats["lam_r"], mats["lam_i"])


def _s5_bwd_full(u_perm, dy_perm, hinit, g0, mats):
    S = u_perm.shape[0]
    rows = S5_TILE_STEPS * S5_SEG
    nt = S // rows

    def body(u_ref, dy_ref, hinit_ref, g0_ref, b_ref, c_ref, lr_ref, li_ref, d_ref,
             du_ref, dp_ref, dq_ref, db_ref, dc_ref, dd_ref, hs, gs, gcar):
        @pl.when(pl.program_id(0) == 0)
        def _():
            gcar[...] = g0_ref[...]
            for ref in (dp_ref, dq_ref, db_ref, dc_ref, dd_ref):
                ref[...] = jnp.zeros(ref.shape, F32)

        u, dy = u_ref[...], dy_ref[...]
        ub, dyb = u.astype(BF16), dy.astype(BF16)
        hs[0:S5_SEG, :] = hinit_ref[...]
        for q in range(S5_QUADS):
            cols = slice(q * S5_QCOLS, (q + 1) * S5_QCOLS)
            hs[S5_SEG:, cols] = jnp.dot(ub[:, q * 128:(q + 1) * 128], b_ref[q], preferred_element_type=F32)
            gs[:, cols] = lax.dot_general(dyb[:, q * 128:(q + 1) * 128], c_ref[q], (((1,), (1,)), ((), ())),
                                          preferred_element_type=F32)

        def fstep(t, h):
            sl = pl.ds(pl.multiple_of((t + 1) * S5_SEG, S5_SEG), S5_SEG)
            hn = _s5_advance(h, lr_ref, li_ref, hs, sl, False)
            _s5_store(hs, sl, hn)
            return hn

        _scan_loop(fstep, _s5_parts(hinit_ref[...]))

        def bstep(k, g):
            t = S5_TILE_STEPS - 1 - k
            sl = pl.ds(pl.multiple_of(t * S5_SEG, S5_SEG), S5_SEG)
            gn = _s5_advance(g, lr_ref, li_ref, gs, sl, True)
            _s5_store(gs, sl, gn)
            return gn

        _s5_store(gcar, slice(None), _scan_loop(bstep, _s5_parts(gcar[...])))

        half = S5_QCOLS // 2
        dus = []
        for q in range(S5_QUADS):
            cols = slice(q * S5_QCOLS, (q + 1) * S5_QCOLS)

            def astep(t, carry, q=q):
                sl = pl.ds(pl.multiple_of(t * S5_SEG, S5_SEG), S5_SEG)
                g = gs[sl, q * S5_QCOLS:(q + 1) * S5_QCOLS]
                hp = hs[sl, q * S5_QCOLS:(q + 1) * S5_QCOLS]
                hp_sw = jnp.concatenate([hp[:, half:], hp[:, :half]], axis=1)
                return carry[0] + g * hp, carry[1] + g * hp_sw

            zero = jnp.zeros((S5_SEG, S5_QCOLS), F32)
            acc_p, acc_q = _scan_loop(astep, (zero, zero))
            dp_ref[:, cols] += jnp.sum(acc_p, axis=0, keepdims=True)
            dq_ref[:, cols] += jnp.sum(acc_q, axis=0, keepdims=True)
            gq = gs[:, cols].astype(BF16)
            db_ref[q] += lax.dot_general(ub[:, q * 128:(q + 1) * 128], gq, (((0,), (0,)), ((), ())),
                                         preferred_element_type=F32)
            hq = hs[S5_SEG:, cols].astype(BF16)
            dc_ref[q] += lax.dot_general(dyb[:, q * 128:(q + 1) * 128], hq, (((0,), (0,)), ((), ())),
                                         preferred_element_type=F32)
            dus.append(lax.dot_general(gq, b_ref[q], (((1,), (1,)), ((), ())), preferred_element_type=F32))
        du_ref[...] = (jnp.concatenate(dus, axis=1) + d_ref[...] * dy).astype(du_ref.dtype)
        dd_ref[...] += jnp.sum(dy * u, axis=0, keepdims=True)

    full = lambda a: pl.BlockSpec(a.shape, functools.partial(lambda i, nd: (0,) * nd, nd=a.ndim))
    rev_rows = pl.BlockSpec((rows, S5_WIDTH), lambda i: (nt - 1 - i, 0))
    consts = [mats["b_q"], mats["c_q"], mats["lam_r"], mats["lam_i"], mats["d_row"]]
    acc = lambda s: pl.BlockSpec(s, functools.partial(lambda i, nd: (0,) * nd, nd=len(s)))
    acc_shapes = [(1, S5_COLS), (1, S5_COLS), (S5_QUADS, 128, S5_QCOLS), (S5_QUADS, 128, S5_QCOLS), (1, S5_WIDTH)]
    return _call(
        body, name="s5_bwd_full", grid=(nt,),
        in_specs=[rev_rows, rev_rows, pl.BlockSpec((None, S5_SEG, S5_COLS), lambda i: (nt - 1 - i, 0, 0)), full(g0)]
        + [full(a) for a in consts],
        out_specs=[rev_rows] + [acc(s) for s in acc_shapes],
        out_shape=[jax.ShapeDtypeStruct((S, S5_WIDTH), BF16)] + [jax.ShapeDtypeStruct(s, F32) for s in acc_shapes],
        scratch_shapes=[pltpu.VMEM((rows + S5_SEG, S5_COLS), F32), pltpu.VMEM((rows, S5_COLS), F32),
                        pltpu.VMEM((S5_SEG, S5_COLS), F32)],
        compiler_params=_params(56),
    )(u_perm, dy_perm, hinit, g0, *consts)


def _cmul(ar, ai, br, bi):
    return ar * br - ai * bi, ar * bi + ai * br


def _split_cols(v):
    t = v.reshape(v.shape[0], S5_QUADS, 2, S5_QCOLS // 2)
    return t[:, :, 0], t[:, :, 1]


def _join_cols(re, im):
    return jnp.stack([re, im], axis=2).reshape(re.shape[0], S5_COLS)


def _segment_starts(ends, pow_r, pow_i, reverse):
    er, ei = _split_cols(ends)
    pi = -pow_i if reverse else pow_i
    order = list(range(S5_SEG))
    if reverse:
        order = order[::-1]
    cr, ci = jnp.zeros_like(er[0]), jnp.zeros_like(ei[0])
    out_r, out_i = [None] * S5_SEG, [None] * S5_SEG
    for j in order:
        out_r[j], out_i[j] = cr, ci
        mr, mi = _cmul(pow_r, pi, cr, ci)
        cr, ci = mr + er[j], mi + ei[j]
    return _join_cols(jnp.stack(out_r), jnp.stack(out_i))


def _to_segments(a):
    S, w = a.shape
    return a.reshape(S5_SEG, S // S5_SEG, w).transpose(1, 0, 2).reshape(S, w)


def _from_segments(a):
    S, w = a.shape
    return a.reshape(S // S5_SEG, S5_SEG, w).transpose(1, 0, 2).reshape(S, w)


def _my_pos():
    return lax.axis_index("x"), lax.axis_index("y"), lax.axis_index("c")


def _flip(pos, k):
    x, y, c = pos
    return (1 - x if k & 4 else x, 1 - y if k & 2 else y, 1 - c if k & 1 else c)


def _index_of(pos):
    return 4 * pos[0] + 2 * pos[1] + pos[2]


_GATHER_FLIPS = (0, 1, 4, 5, 2, 3, 6, 7)


def _inproj_gather(x, norm_g, pack_a, pack_b, pack_c):
    S = x.shape[0]
    tm = min(S, 1024)
    n_i = S // tm
    order = jnp.stack([_index_of(_flip(_my_pos(), k)) for k in _GATHER_FLIPS]).astype(jnp.int32)

    def body(order_ref, x_ref, g_ref, pa_ref, pb_ref, pc_ref, ut_ref, proj_ref, oa_ref, ob_ref, oc_ref,
             wv, u_scr, send_sems, recv_sems, local_sems):
        s, i = pl.program_id(0), pl.program_id(1)
        me = _my_pos()
        mine = _index_of(me)
        sibling = _flip(me, 1)
        srcs = (pb_ref, pa_ref, pc_ref)
        dsts = (wv, oa_ref, oc_ref)

        def direct(a, k):
            return pltpu.make_async_remote_copy(
                src_ref=srcs[a], dst_ref=dsts[a].at[mine], send_sem=send_sems.at[a * 8 + k],
                recv_sem=recv_sems.at[a * 8 + k], device_id=_flip(me, k), device_id_type=MESH)

        def passed_on(a, k):
            slot = _index_of(_flip(me, k))
            return pltpu.make_async_remote_copy(
                src_ref=dsts[a].at[slot], dst_ref=dsts[a].at[slot], send_sem=send_sems.at[a * 8 + (k | 1)],
                recv_sem=recv_sems.at[a * 8 + (k | 1)], device_id=sibling, device_id_type=MESH)

        def arrival(a, k):
            slot = _index_of(_flip(me, k))
            pltpu.make_async_remote_copy(
                src_ref=dsts[a].at[slot], dst_ref=dsts[a].at[slot], send_sem=send_sems.at[a * 8 + k],
                recv_sem=recv_sems.at[a * 8 + k], device_id=me, device_id_type=MESH).wait_recv()

        def own_copy(a):
            return pltpu.make_async_copy(srcs[a], dsts[a].at[mine], local_sems.at[a])

        def keep(idx):
            slot = _index_of(_flip(me, _GATHER_FLIPS[idx]))
            return pltpu.make_async_copy(wv.at[slot], ob_ref.at[slot], local_sems.at[3 + idx])


        first = (s == 0) & (i == 0)

        @pl.when(first)
        def _():
            for a in range(3):
                own_copy(a).start()
            for k in (1, 4, 2):
                direct(0, k).start()
            own_copy(0).wait()
            keep(0).start()

        for idx, k in enumerate(_GATHER_FLIPS):
            if idx == 0:
                continue

            @pl.when((s == idx) & (i == 0))
            def _(idx=idx, k=k):
                arrival(0, k)
                if k in (4, 2, 6):
                    passed_on(0, k).start()
                keep(idx).start()
                if idx == 1:
                    direct(0, 6).start()
                if idx == 2:
                    for a in (1, 2):
                        for k in (1, 4, 2, 6):
                            direct(a, k).start()

        @pl.when(s == 0)
        def _():
            y, _, _ = _rms_fwd(x_ref[...], g_ref[...])
            u_scr[pl.ds(pl.multiple_of(i * tm, tm), tm), :] = y.astype(BF16)
            ut_ref[...] = y.T.astype(BF16)

        ub = u_scr[pl.ds(pl.multiple_of(i * tm, tm), tm), :]
        proj_ref[...] = jnp.dot(ub, wv[order_ref[s]], preferred_element_type=F32)

        @pl.when((s == N_DEV - 1) & (i == n_i - 1))
        def _():
            for a in (1, 2):
                for k in (4, 2, 6):
                    arrival(a, k)
                    passed_on(a, k).start()
            for a in (1, 2):
                for k in (1, 5, 3, 7):
                    arrival(a, k)
                own_copy(a).wait()
            for a in range(3):
                for k in (1, 4, 2, 6):
                    direct(a, k).wait_send()
                for k in (4, 2, 6):
                    passed_on(a, k).wait_send()
            for idx in range(N_DEV):
                keep(idx).wait()

    any_spec = pl.BlockSpec(memory_space=pl.ANY)
    vmem = pl.BlockSpec(memory_space=pltpu.VMEM)
    grid_spec = pltpu.PrefetchScalarGridSpec(
        num_scalar_prefetch=1, grid=(N_DEV, n_i),
        in_specs=[pl.BlockSpec((tm, D_MODEL), lambda s, i, o: (jnp.where(s == 0, i, 0), 0)),
                  pl.BlockSpec((1, D_MODEL), lambda s, i, o: (0, 0)), any_spec, vmem, any_spec],
        out_specs=[pl.BlockSpec((D_MODEL, tm), lambda s, i, o: (0, jnp.where(s == 0, i, n_i - 1))),
                   pl.BlockSpec((tm, SHARD_IN), lambda s, i, o: (i, o[s])), any_spec, any_spec, any_spec],
        scratch_shapes=[pltpu.VMEM((N_DEV,) + pack_b.shape, BF16), pltpu.VMEM((S, D_MODEL), BF16),
                        pltpu.SemaphoreType.DMA((24,)), pltpu.SemaphoreType.DMA((24,)), pltpu.SemaphoreType.DMA((3 + N_DEV,))],
    )
    return _call(
        body, name="inproj_gather", grid_spec=grid_spec,
        out_shape=[jax.ShapeDtypeStruct((D_MODEL, S), BF16), jax.ShapeDtypeStruct((S, IN_COLS), F32),
                   jax.ShapeDtypeStruct((N_DEV,) + pack_a.shape, BF16), jax.ShapeDtypeStruct((N_DEV,) + pack_b.shape, BF16),
                   jax.ShapeDtypeStruct((N_DEV,) + pack_c.shape, BF16)],
        compiler_params=_params(56, 2),
    )(order, x, norm_g, pack_a, pack_b, pack_c)


_SCATTER_FLIPS = (7, 6, 5, 4, 3, 2, 1, 0)
_N_CHIPS = 4


def _for_row_chunks(n_rows, chunk, fn):
    def step(c, carry):
        fn(pl.ds(pl.multiple_of(c * chunk, chunk), chunk))
        return carry

    lax.fori_loop(0, n_rows // chunk, step, 0)


def _grad_w_in_scatter(dproj, u_t, rs_a, rs_c, small_partial):
    S = u_t.shape[1]
    tm = min(S, 1024)
    n_i = S // tm
    order = jnp.stack([_index_of(_flip(_my_pos(), k)) for k in _SCATTER_FLIPS]).astype(jnp.int32)
    shapes = ((D_MODEL, SHARD_IN), rs_a.shape[1:], rs_c.shape[1:])
    row_chunk = 128

    def body(order_ref, dp_ref, ut_ref, ra_ref, rc_ref, p_ref, gb_ref, ga_ref, gc_ref, gs_ref, acc, sib_b, d2d_b,
             send_b, ici_b, mine_a, sib_a, ici_a, mine_c, sib_c, ici_c, gath, send_sems, recv_sems, local_sems):
        s, i = pl.program_id(0), pl.program_id(1)
        me = _my_pos()
        sibling = _flip(me, 1)

        my_chip = 2 * me[0] + me[1]

        small_d2d = pltpu.make_async_remote_copy(
            src_ref=p_ref, dst_ref=gath.at[_N_CHIPS], send_sem=send_sems.at[21], recv_sem=recv_sems.at[21],
            device_id=sibling, device_id_type=MESH)

        def small_ici(m):
            return pltpu.make_async_remote_copy(
                src_ref=gath.at[my_chip], dst_ref=gath.at[my_chip], send_sem=send_sems.at[22 + m],
                recv_sem=recv_sems.at[22 + m], device_id=_flip(me, 6 - 2 * m), device_id_type=MESH)
        sib = (sib_b, sib_a, sib_c)
        ici = (ici_b, ici_a, ici_c)
        outs = (gb_ref, ga_ref, gc_ref)

        def to_sibling(arr, m, src):
            return pltpu.make_async_remote_copy(
                src_ref=src, dst_ref=sib[arr].at[m], send_sem=send_sems.at[arr * 7 + m],
                recv_sem=recv_sems.at[arr * 7 + m], device_id=sibling, device_id_type=MESH)

        def over_ici(arr, m, src):
            return pltpu.make_async_remote_copy(
                src_ref=src, dst_ref=ici[arr].at[m], send_sem=send_sems.at[arr * 7 + 4 + m],
                recv_sem=recv_sems.at[arr * 7 + 4 + m], device_id=_flip(me, 6 - 2 * m), device_id_type=MESH)

        def from_sibling(arr, m):
            to_sibling(arr, m, sib[arr].at[m]).wait_recv()

        def from_ici(arr, m):
            over_ici(arr, m, ici[arr].at[m]).wait_recv()

        small = ((1, ra_ref, mine_a), (2, rc_ref, mine_c))

        def local_copy(arr, src, mine, m):
            return pltpu.make_async_copy(src.at[_index_of(_flip(me, 6 - 2 * m))], mine.at[m],
                                         local_sems.at[(arr - 1) * _N_CHIPS + m])

        @pl.when((s == 0) & (i == 0))
        def _():
            small_d2d.start()
            for arr, src, mine in small:
                for m in range(_N_CHIPS):
                    to_sibling(arr, m, src.at[_index_of(_flip(me, 7 - 2 * m))]).start()
                    local_copy(arr, src, mine, m).start()

        @pl.when((s == 1) & (i == 0))
        def _():
            small_d2d.wait_recv()
            gath[my_chip] = p_ref[...] + gath[_N_CHIPS]
            for m in range(_N_CHIPS - 1):
                small_ici(m).start()
            for arr, src, mine in small:
                rows, chunk = shapes[arr][0], 16
                for m in range(_N_CHIPS):
                    local_copy(arr, src, mine, m).wait()
                    from_sibling(arr, m)
                    if m < _N_CHIPS - 1:
                        def add(sl, arr=arr, mine=mine, m=m):
                            mine[m, sl, :] = (mine[m, sl, :].astype(F32) + sib[arr][m, sl, :].astype(F32)).astype(BF16)

                        _for_row_chunks(rows, chunk, add)
                        over_ici(arr, m, mine.at[m]).start()
                    else:
                        def keep(sl, arr=arr, mine=mine, m=m):
                            outs[arr][sl, :] = mine[m, sl, :].astype(F32) + sib[arr][m, sl, :].astype(F32)

                        _for_row_chunks(rows, chunk, keep)

        @pl.when(i == 0)
        def _():
            acc[...] = jnp.zeros(acc.shape, F32)

        acc[...] += jnp.dot(ut_ref[...], dp_ref[...], preferred_element_type=F32)

        def block_rows(c):
            return acc[c * row_chunk:(c + 1) * row_chunk, :]

        for m in range(_N_CHIPS):
            @pl.when((s == 2 * m) & (i == n_i - 1))
            def _(m=m):
                if m > 0:
                    to_sibling(0, m - 1, d2d_b).wait_send()
                for c in range(D_MODEL // row_chunk):
                    d2d_b[c * row_chunk:(c + 1) * row_chunk, :] = block_rows(c).astype(BF16)
                to_sibling(0, m, d2d_b).start()

            @pl.when((s == 2 * m + 1) & (i == n_i - 1))
            def _(m=m):
                from_sibling(0, m)
                slot = m % 2
                if m == 2:
                    over_ici(0, 0, send_b.at[0]).wait_send()
                for c in range(D_MODEL // row_chunk):
                    rows = slice(c * row_chunk, (c + 1) * row_chunk)
                    total = block_rows(c) + sib_b[m, rows, :].astype(F32)
                    if m < _N_CHIPS - 1:
                        send_b[slot, rows, :] = total.astype(BF16)
                    else:
                        gb_ref[rows, :] = total
                if m < _N_CHIPS - 1:
                    over_ici(0, m, send_b.at[slot]).start()

        @pl.when((s == N_DEV - 1) & (i == n_i - 1))
        def _():
            for arr in range(3):
                for m in range(_N_CHIPS - 1):
                    from_ici(arr, m)
                rows = shapes[arr][0]

                def add(sl, arr=arr):
                    outs[arr][sl, :] = (outs[arr][sl, :] + ici[arr][0, sl, :].astype(F32)
                                        + ici[arr][1, sl, :].astype(F32) + ici[arr][2, sl, :].astype(F32))

                _for_row_chunks(rows, 16, add)
            for m in range(_N_CHIPS - 1):
                small_ici(m).wait_recv()
            gs_ref[...] = (gath[0] + gath[1]) + (gath[2] + gath[3])
            small_d2d.wait_send()
            for m in range(_N_CHIPS - 1):
                small_ici(m).wait_send()
            to_sibling(0, _N_CHIPS - 1, d2d_b).wait_send()
            over_ici(0, 1, send_b.at[1]).wait_send()
            over_ici(0, 2, send_b.at[0]).wait_send()
            for arr, src, mine in small:
                for m in range(_N_CHIPS):
                    to_sibling(arr, m, src.at[0]).wait_send()
                for m in range(_N_CHIPS - 1):
                    over_ici(arr, m, mine.at[m]).wait_send()

    any_spec = pl.BlockSpec(memory_space=pl.ANY)
    vmem = pl.BlockSpec(memory_space=pltpu.VMEM)
    half = lambda shp, n: pltpu.VMEM((n,) + tuple(shp), BF16)
    grid_spec = pltpu.PrefetchScalarGridSpec(
        num_scalar_prefetch=1, grid=(N_DEV, n_i),
        in_specs=[pl.BlockSpec((tm, SHARD_IN), lambda s, i, o: (i, o[s])),
                  pl.BlockSpec((D_MODEL, tm), lambda s, i, o: (0, i)), any_spec, any_spec, vmem],
        out_specs=[vmem, vmem, vmem, vmem],
        scratch_shapes=[
            pltpu.VMEM((D_MODEL, SHARD_IN), F32), half(shapes[0], _N_CHIPS), pltpu.VMEM(shapes[0], BF16),
            half(shapes[0], 2), half(shapes[0], _N_CHIPS - 1),
            half(shapes[1], _N_CHIPS), half(shapes[1], _N_CHIPS), half(shapes[1], _N_CHIPS - 1),
            half(shapes[2], _N_CHIPS), half(shapes[2], _N_CHIPS), half(shapes[2], _N_CHIPS - 1),
            pltpu.VMEM((_N_CHIPS + 1,) + small_partial.shape, F32),
            pltpu.SemaphoreType.DMA((25,)), pltpu.SemaphoreType.DMA((25,)), pltpu.SemaphoreType.DMA((2 * _N_CHIPS,))],
    )
    return _call(
        body, name="grad_w_in_scatter", grid_spec=grid_spec,
        out_shape=[jax.ShapeDtypeStruct(shp, F32) for shp in shapes] + [jax.ShapeDtypeStruct(small_partial.shape, F32)],
        compiler_params=_params(60, 2),
    )(order, dproj, u_t, rs_a, rs_c, small_partial)


def _adam_update(g, w, m, v):
    m2 = ADAM_B1 * m + (1.0 - ADAM_B1) * g
    v2 = ADAM_B2 * v + (1.0 - ADAM_B2) * (g * g)
    m_hat = m2 / (1.0 - ADAM_B1 ** ADAM_STEP)
    v_hat = v2 / (1.0 - ADAM_B2 ** ADAM_STEP)
    delta = -ADAM_LR * (m_hat / (jnp.sqrt(v_hat) + ADAM_EPS) + ADAM_WD * w)
    return delta, m2, v2


def _adam_rows(g, w, m, v):
    rows, cols = w.shape
    tm = rows if rows % 256 else 256

    def fn(rv, cr, out):
        return list(_adam_update(*rv)), []

    outs, _ = _rowwise("adamw", fn, rows, tm, [(a, cols, 0) for a in (g, w, m, v)], [], [(cols, F32)] * 3, [], 32)
    return outs


_SMALL = ["norm_g", "hg_lb", "hg_norm_g", "s5_a_re", "s5_a_im", "s5_log_dt", "s5_b_re", "s5_b_im", "s5_c_re",
          "s5_c_im", "s5_d", "b_glu", "ple_norm_g", "final_norm_g"]
_BIG = ["w_in", "w_o_hg", "w_glu", "w_o_s5", "w_out", "w_ple", "w_ple_gate"]
_ORDER = ["norm_g", "w_in", "hg_lb", "hg_norm_g", "w_o_hg", "s5_a_re", "s5_a_im", "s5_log_dt", "s5_b_re", "s5_b_im",
          "s5_c_re", "s5_c_im", "s5_d", "w_glu", "b_glu", "w_o_s5", "w_out", "ple_norm_g", "w_ple", "w_ple_gate",
          "final_norm_g"]


def _pack_small(vals, tail=None):
    parts = []
    for name in _SMALL:
        flat = vals[name].reshape(-1).astype(F32)
        pad = (-flat.shape[0]) % 1024
        parts.append(jnp.pad(flat, (0, pad)))
    tail = jnp.zeros((0,), F32) if tail is None else tail.reshape(-1).astype(F32)
    parts.append(jnp.pad(tail, (0, 1024 - tail.shape[0])))
    return jnp.concatenate(parts).reshape(-1, 128)


def _unpack_small(packed, like):
    flat = packed.reshape(-1)
    out, off = {}, 0
    for name in _SMALL:
        size = like[name].size
        out[name] = flat[off:off + size].reshape(like[name].shape)
        off += size + (-size) % 1024
    return out


def _col_blocks(full):
    k = full.shape[0]
    return full.reshape(k, N_DEV, 128).transpose(1, 0, 2)


def _from_col_blocks(blocks):
    k = blocks.shape[1]
    return blocks.transpose(1, 0, 2).reshape(k, N_DEV * 128)


def kernel(x, p, norm_g, w_in, hg_lb, hg_norm_g, w_o_hg, s5_a_re, s5_a_im, s5_log_dt, s5_b_re, s5_b_im, s5_c_re, s5_c_im, s5_d, w_glu, b_glu, w_o_s5, w_out, ple_norm_g, w_ple, w_ple_gate, final_norm_g, loss_target, m_norm_g, m_w_in, m_hg_lb, m_hg_norm_g, m_w_o_hg, m_s5_a_re, m_s5_a_im, m_s5_log_dt, m_s5_b_re, m_s5_b_im, m_s5_c_re, m_s5_c_im, m_s5_d, m_w_glu, m_b_glu, m_w_o_s5, m_w_out, m_ple_norm_g, m_w_ple, m_w_ple_gate, m_final_norm_g, v_norm_g, v_w_in, v_hg_lb, v_hg_norm_g, v_w_o_hg, v_s5_a_re, v_s5_a_im, v_s5_log_dt, v_s5_b_re, v_s5_b_im, v_s5_c_re, v_s5_c_im, v_s5_d, v_w_glu, v_b_glu, v_w_o_s5, v_w_out, v_ple_norm_g, v_w_ple, v_w_ple_gate, v_final_norm_g):
    args = dict(locals())
    w = {n: args[n] for n in _ORDER}
    m = {n: args["m_" + n] for n in _ORDER}
    v = {n: args["v_" + n] for n in _ORDER}
    xs = x[0]
    ps = p[0, 0]
    tgt = loss_target[0]
    S = xs.shape[0]

    pack_a = jnp.concatenate([w_o_hg[0], w_out[0], w_ple_gate[0]], axis=0).astype(BF16)
    pack_b = w_in[0].astype(BF16)
    pack_c = jnp.concatenate([w_glu[0], w_o_s5[0], w_ple[0]], axis=0).astype(BF16)
    u_t, proj, all_a, all_b, all_c = _inproj_gather(xs, norm_g, pack_a, pack_b, pack_c)
    wf_o_hg = all_a[:, 0:128].reshape(D_MODEL, D_MODEL)
    wf_out = all_a[:, 128:256].reshape(D_MODEL, D_MODEL)
    wf_pg = all_a[:, 256:384].reshape(D_MODEL, D_MODEL)
    wf_glu = _from_col_blocks(all_c[:, 0:512])
    wf_o_s5 = _from_col_blocks(all_c[:, 512:1024])
    wf_ple = _from_col_blocks(all_c[:, 1024:1280])

    lb = jax.nn.sigmoid(hg_lb[0:1] - hg_lb[1:2])
    s5_names = ["s5_a_re", "s5_a_im", "s5_log_dt", "s5_b_re", "s5_b_im", "s5_c_re", "s5_c_im", "s5_d"]
    build = lambda *a: _s5_matrices(*a, seg_len=S // S5_SEG)
    mats_f32, mats_vjp = jax.vjp(build, *[w[n][0] for n in s5_names])
    mats = dict(mats_f32, b_q=mats_f32["b_q"].astype(BF16), c_q=mats_f32["c_q"].astype(BF16))
    bias_glu = b_glu

    o, states = _hgrn_fwd(proj, lb)
    u_perm = _to_segments(proj[:, COL_US:COL_US + S5_WIDTH])
    zeros_state = jnp.zeros((S5_SEG, S5_COLS), F32)
    (h_ends,) = _s5_fwd_pass(u_perm, mats, zeros_state, False)
    h0 = _segment_starts(h_ends, mats["pow_r"], mats["pow_i"], False)
    y_perm, h_init, _ = _s5_fwd_pass(u_perm, mats, h0, True)
    ys = _from_segments(y_perm)
    y_hg, y_s5, glu, h1 = _stage_branches(o, proj, ys, xs, hg_norm_g, wf_o_hg, wf_glu, bias_glu, wf_o_s5, wf_out)

    dh1, (loss_acc, d_final_g, d_ple_g, d_w_ple, d_w_pg) = _stage_ple_loss(
        h1, ps, tgt, ple_norm_g, wf_ple, wf_pg, final_norm_g.reshape(1, D_MODEL))
    dproj = lax.empty((S, IN_COLS), BF16)
    (d_gate_s5, d_yhg, d_ys5, dproj), (d_w_out,) = _stage_bwd_merge(dh1, y_hg, y_s5, proj, wf_out, dproj)
    (d_o, dproj), (d_w_o_hg, d_hg_norm) = _stage_bwd_hg_path(d_yhg, o, proj, hg_norm_g, wf_o_hg, dproj)
    (d_ys, dproj), (d_w_o_s5, d_w_glu, d_b_glu) = _stage_bwd_s5_path(d_ys5, ys, glu, proj, wf_o_s5, wf_glu, dproj)
    dproj, d_lb = _hgrn_bwd(proj, lb, d_o, states, dproj)
    dy_perm = _to_segments(d_ys)
    g_ends = _s5_bwd_ends(dy_perm, mats)
    g0 = _segment_starts(g_ends, mats["pow_r"], mats["pow_i"], True)
    du_perm, acc_p, acc_q, d_bq, d_cq_t, d_d = _s5_bwd_full(u_perm, dy_perm, h_init, g0, mats)
    dproj = lax.dynamic_update_slice(dproj, _from_segments(du_perm), (0, COL_US))
    dproj = lax.dynamic_update_slice(dproj, d_gate_s5, (0, COL_GS))
    w_in_t = all_b.transpose(0, 2, 1).reshape(IN_COLS, D_MODEL)
    grad_x, d_norm_g = _stage_inproj_bwd(dproj, xs, dh1, norm_g, w_in_t)

    p_re, p_im = _split_cols(acc_p)
    q_re, q_im = _split_cols(acc_q)
    d_lam_r = (p_re + p_im)[0]
    d_lam_i = (q_im - q_re)[0]
    zero_row = jnp.zeros((S5_SEG, S5_COLS), F32)
    row_of = lambda re_part, im_part: zero_row.at[0].set(_join_cols(re_part[None], im_part[None])[0])
    zeros_q = jnp.zeros_like(d_lam_r)
    cot = dict(
        lam_r=row_of(d_lam_r, zeros_q), lam_i=row_of(zeros_q, d_lam_i),
        b_q=d_bq, c_q=d_cq_t.transpose(0, 2, 1), d_row=d_d,
        pow_r=jnp.zeros_like(mats["pow_r"]), pow_i=jnp.zeros_like(mats["pow_i"]),
    )
    d_s5 = mats_vjp(cot)

    s_lb = lb * (1.0 - lb)
    d_hg_lb = jnp.concatenate([d_lb * s_lb, -d_lb * s_lb], axis=0)
    small_g = dict(norm_g=d_norm_g, hg_lb=d_hg_lb, hg_norm_g=d_hg_norm, b_glu=d_b_glu, ple_norm_g=d_ple_g,
                   final_norm_g=d_final_g)
    for name, g in zip(s5_names, d_s5):
        small_g[name] = g
    pk = lambda d: _pack_small({n: d[n] for n in _SMALL})
    rs_a = jnp.concatenate([d_w_o_hg.reshape(N_DEV, 128, D_MODEL), d_w_out.reshape(N_DEV, 128, D_MODEL),
                            d_w_pg.reshape(N_DEV, 128, D_MODEL)], axis=1).astype(BF16)
    rs_c = jnp.concatenate([_col_blocks(d_w_glu), _col_blocks(d_w_o_s5), _col_blocks(d_w_ple)], axis=1).astype(BF16)
    partial = _pack_small({n: small_g[n] for n in _SMALL}, tail=loss_acc[0, 0:1])
    g_b, g_a, g_c, sg = _grad_w_in_scatter(dproj, u_t, rs_a, rs_c, partial)
    sd, sm, sv = _adam_rows(sg, pk(w), pk(m), pk(v))
    like = {n: w[n] for n in _SMALL}
    out_g, out_d, out_m, out_v = (_unpack_small(t, like) for t in (sg, sd, sm, sv))
    big_g = dict(w_o_hg=g_a[0:128], w_out=g_a[128:256], w_ple_gate=g_a[256:384], w_in=g_b,
                 w_glu=g_c[0:512], w_o_s5=g_c[512:1024], w_ple=g_c[1024:1280])
    for name in _BIG:
        shape = w[name].shape
        g2 = big_g[name]
        d2, m2, v2 = _adam_rows(g2, w[name][0], m[name][0], v[name][0])
        out_g[name], out_d[name], out_m[name], out_v[name] = (t.reshape(shape) for t in (g2, d2, m2, v2))

    loss = sg[sg.shape[0] - 8, 0]
    return (loss, grad_x[None], *[out_g[n] for n in _ORDER], *[out_d[n] for n in _ORDER],
            *[out_m[n] for n in _ORDER], *[out_v[n] for n in _ORDER])
```

```python
import functools
import math

import jax
import jax.numpy as jnp
from jax import lax
from jax.experimental import pallas as pl
from jax.experimental.pallas import tpu as pltpu

F32 = jnp.float32
BF16 = jnp.bfloat16

D_MODEL = 1024
N_DEV = 8
IN_COLS = 7168
SHARD_IN = IN_COLS // N_DEV
HG_HEADS = 8
HG_DIM = 128
HG_CHUNK = 64
HG_SUPER_FWD = 256
HG_SUPER_BWD = 128
HG_HEADS_PER_STEP = 8
S5_WIDTH = 512
S5_GROUPS = 32
S5_STATE = 64
S5_CH = 16
S5_SEG = 8
S5_QUADS = 4
S5_QCOLS = 1024
S5_COLS = S5_QUADS * S5_QCOLS
S5_TILE_STEPS = 64
S5_UNROLL = 8
NORM_EPS = 1e-6
ADAM_LR = 0.001
ADAM_B1 = 0.9
ADAM_B2 = 0.999
ADAM_EPS = 1e-08
ADAM_WD = 0.01
ADAM_STEP = 10
MIB = 1024 * 1024
MESH = pl.DeviceIdType.MESH

COL_Q, COL_F, COL_I, COL_G, COL_US, COL_ZS, COL_GH, COL_GS = 0, 1024, 2048, 3072, 4096, 4608, 5120, 6144


def _call(body, **kw):
    return pl.pallas_call(body, **kw)


def _params(vmem_mb, n_grid=1):
    return pltpu.CompilerParams(
        dimension_semantics=("arbitrary",) * n_grid, vmem_limit_bytes=vmem_mb * MIB
    )


def _bdot(a, b):
    return jnp.dot(a.astype(BF16), b.astype(BF16), preferred_element_type=F32)


def _bdot_nt(a, b):
    return lax.dot_general(a.astype(BF16), b.astype(BF16), (((1,), (1,)), ((), ())), preferred_element_type=F32)


def _bdot_tn(a, b):
    return lax.dot_general(a.astype(BF16), b.astype(BF16), (((0,), (0,)), ((), ())), preferred_element_type=F32)


def _sigmoid(x):
    return jax.nn.sigmoid(x)


def _silu(x):
    return x * _sigmoid(x)


def _dsilu(x):
    s = _sigmoid(x)
    return s * (1.0 + x * (1.0 - s))


_GELU_C = math.sqrt(2.0 / math.pi)


def _gelu(x):
    return 0.5 * x * (1.0 + jnp.tanh(_GELU_C * (x + 0.044715 * x * x * x)))


def _dgelu(x):
    t = jnp.tanh(_GELU_C * (x + 0.044715 * x * x * x))
    return 0.5 * (1.0 + t) + 0.5 * x * (1.0 - t * t) * _GELU_C * (1.0 + 3.0 * 0.044715 * x * x)


def _rms_fwd(x, g):
    r = lax.rsqrt(jnp.mean(x * x, axis=-1, keepdims=True) + NORM_EPS)
    n = x * r
    return n * g, n, r


def _rms_bwd(dy, n, r, g):
    dn = dy * g
    dx = r * (dn - n * jnp.mean(dn * n, axis=-1, keepdims=True))
    return dx, jnp.sum(dy * n, axis=0, keepdims=True)


def _head_rms_fwd(o, g):
    ns, rs = [], []
    for h in range(HG_HEADS):
        oh = o[:, h * HG_DIM:(h + 1) * HG_DIM]
        r = lax.rsqrt(jnp.mean(oh * oh, axis=-1, keepdims=True) + NORM_EPS)
        ns.append(oh * r)
        rs.append(r)
    n = jnp.concatenate(ns, axis=1)
    return n * g, n, rs


def _head_rms_bwd(dy, n, rs, g):
    dn = dy * g
    dxs = []
    for h in range(HG_HEADS):
        sl = slice(h * HG_DIM, (h + 1) * HG_DIM)
        dxs.append(rs[h] * (dn[:, sl] - n[:, sl] * jnp.mean(dn[:, sl] * n[:, sl], axis=-1, keepdims=True)))
    return jnp.concatenate(dxs, axis=1), jnp.sum(dy * n, axis=0, keepdims=True)


def _rowwise(name, fn, n_rows, tm, rows, consts, out_rows, out_accs, vmem_mb, parts=1):
    n_r, n_c, n_or, n_oa = len(rows), len(consts), len(out_rows), len(out_accs)
    tp = tm // parts

    def body(*refs):
        r_refs = refs[:n_r]
        c_refs = refs[n_r:n_r + n_c]
        or_refs = refs[n_r + n_c:n_r + n_c + n_or]
        oa_refs = refs[n_r + n_c + n_or:]

        if n_oa:
            @pl.when(pl.program_id(0) == 0)
            def _():
                for ref in oa_refs:
                    ref[...] = jnp.zeros(ref.shape, ref.dtype)

        for part in range(parts):
            sl = slice(part * tp, (part + 1) * tp)
            outs, accs = fn([r[sl, :] for r in r_refs], c_refs, [o.at[sl, :] for o in or_refs])
            for ref, v in zip(or_refs, outs):
                if v is not None:
                    ref[sl, :] = v.astype(ref.dtype)
            for ref, v in zip(oa_refs, accs):
                ref[...] += v.astype(ref.dtype)

    in_specs = [pl.BlockSpec((tm, w), functools.partial(lambda i, c: (i, c), c=cb)) for (_, w, cb) in rows]
    in_specs += [pl.BlockSpec(c.shape, functools.partial(lambda i, nd: (0,) * nd, nd=c.ndim),
                              pipeline_mode=pl.Buffered(1)) for c in consts]
    out_specs = [pl.BlockSpec((tm, w), lambda i: (i, 0)) for (w, _) in out_rows]
    out_specs += [pl.BlockSpec(s, functools.partial(lambda i, nd: (0,) * nd, nd=len(s))) for (s, _) in out_accs]
    out_shape = [jax.ShapeDtypeStruct((n_rows, w), dt) for (w, dt) in out_rows]
    out_shape += [jax.ShapeDtypeStruct(s, dt) for (s, dt) in out_accs]
    res = _call(
        body, name=name, grid=(n_rows // tm,), in_specs=in_specs, out_specs=out_specs, out_shape=out_shape,
        compiler_params=_params(vmem_mb),
    )(*[a for (a, _, _) in rows], *consts)
    return res[:n_or], res[n_or:]


def _stage_branches(o, proj, ys, x, hg_norm_g, w_o_hg, w_glu, b_glu, w_o_s5, w_out):
    S = x.shape[0]

    def fn(rv, cr, out):
        o_b, g_hg, z_s, gate_hg, gate_s5, ys_b, x_b = rv
        gn_ref, wohg_ref, wglu_ref, bglu_ref, wos5_ref, wout_ref = cr
        on, _, _ = _head_rms_fwd(o_b, gn_ref[...])
        a = on * _silu(g_hg)
        y_hg = jnp.dot(a.astype(BF16), wohg_ref[...], preferred_element_type=F32)
        gl = _gelu(ys_b)
        glu = jnp.dot(gl.astype(BF16), wglu_ref[...], preferred_element_type=F32) + bglu_ref[...]
        ys2 = glu[:, :S5_WIDTH] * _sigmoid(glu[:, S5_WIDTH:]) * _silu(z_s)
        y_s5 = jnp.dot(ys2.astype(BF16), wos5_ref[...], preferred_element_type=F32)
        merged = _sigmoid(gate_hg) * y_hg + _sigmoid(gate_s5) * y_s5
        h1 = x_b + jnp.dot(merged.astype(BF16), wout_ref[...], preferred_element_type=F32)
        return [y_hg, y_s5, glu, h1], []

    rows = [(o, D_MODEL, 0), (proj, D_MODEL, COL_G // D_MODEL), (proj, S5_WIDTH, COL_ZS // S5_WIDTH),
            (proj, D_MODEL, COL_GH // D_MODEL), (proj, D_MODEL, COL_GS // D_MODEL), (ys, S5_WIDTH, 0), (x, D_MODEL, 0)]
    (y_hg, y_s5, glu, h1), _ = _rowwise(
        "branches", fn, S, 256, rows, [hg_norm_g, w_o_hg, w_glu, b_glu, w_o_s5, w_out],
        [(D_MODEL, F32)] * 4, [], 56)
    return y_hg, y_s5, glu, h1


def _stage_ple_loss(h1, p, target, ple_norm_g, w_ple, w_ple_gate, final_norm_g):
    S = h1.shape[0]

    def fn(rv, cr, out):
        h1_b, p_b, t_b = rv
        gp_ref, wple_ref, wpg_ref, gf_ref = cr
        n2g, n2, r2 = _rms_fwd(h1_b, gp_ref[...])
        z = jnp.dot(n2g.astype(BF16), wpg_ref[...], preferred_element_type=F32)
        gate = _sigmoid(z)
        pe = jnp.dot(p_b.astype(BF16), wple_ref[...], preferred_element_type=F32)
        h2 = h1_b + pe * gate
        y, nf, rf = _rms_fwd(h2, gf_ref[...])
        err = y - t_b
        loss_rows = 0.5 * jnp.mean(err * err, axis=-1, keepdims=True)
        loss_inc = jnp.broadcast_to(jnp.sum(loss_rows, axis=0, keepdims=True), (1, 128))
        dy = err * (1.0 / D_MODEL)
        dh2, d_gf = _rms_bwd(dy, nf, rf, gf_ref[...])
        d_pe = dh2 * gate
        dz = dh2 * pe * gate * (1.0 - gate)
        d_wple = _bdot_tn(p_b, d_pe)
        d_wpg = _bdot_tn(n2g, dz)
        dn2g = _bdot_nt(dz, wpg_ref[...])
        dh1n, d_gp = _rms_bwd(dn2g, n2, r2, gp_ref[...])
        return [dh2 + dh1n], [loss_inc, d_gf, d_gp, d_wple, d_wpg]

    (dh1,), accs = _rowwise(
        "ple_loss", fn, S, 512, [(h1, D_MODEL, 0), (p, 256, 0), (target, D_MODEL, 0)],
        [ple_norm_g, w_ple, w_ple_gate, final_norm_g], [(D_MODEL, F32)],
        [((1, 128), F32), ((1, D_MODEL), F32), ((1, D_MODEL), F32), ((256, D_MODEL), F32), ((D_MODEL, D_MODEL), F32)], 56,
        parts=2)
    return dh1, accs


def _stage_bwd_merge(dh1, y_hg, y_s5, proj, w_out):
    S = dh1.shape[0]

    def fn(rv, cr, out):
        dh1_b, yhg, ys5, gate_hg, gate_s5 = rv
        (wout_ref,) = cr
        sg_h, sg_s = _sigmoid(gate_hg), _sigmoid(gate_s5)
        merged = sg_h * yhg + sg_s * ys5
        d_wout = _bdot_tn(merged, dh1_b)
        d_merged = _bdot_nt(dh1_b, wout_ref[...])
        d_gate_hg = d_merged * yhg * sg_h * (1.0 - sg_h)
        d_gate_s5 = d_merged * ys5 * sg_s * (1.0 - sg_s)
        return [d_gate_hg, d_gate_s5, d_merged * sg_h, d_merged * sg_s], [d_wout]

    rows = [(dh1, D_MODEL, 0), (y_hg, D_MODEL, 0), (y_s5, D_MODEL, 0), (proj, D_MODEL, COL_GH // D_MODEL),
            (proj, D_MODEL, COL_GS // D_MODEL)]
    outs, accs = _rowwise("bwd_merge", fn, S, 512, rows, [w_out], [(D_MODEL, BF16)] * 4,
                          [((D_MODEL, D_MODEL), F32)], 56, parts=2)
    return outs, accs


def _stage_bwd_hg_path(d_yhg, o, proj, hg_norm_g, w_o_hg):
    S = o.shape[0]

    def fn(rv, cr, out):
        d_yhg_b, o_b, g_hg = rv
        gn_ref, wohg_ref = cr
        ong, on, rs = _head_rms_fwd(o_b, gn_ref[...])
        sil = _silu(g_hg)
        d_wohg = _bdot_tn(ong * sil, d_yhg_b)
        d_a = _bdot_nt(d_yhg_b, wohg_ref[...])
        d_g_hg = d_a * ong * _dsilu(g_hg)
        d_o, d_gn = _head_rms_bwd(d_a * sil, on, rs, gn_ref[...])
        return [d_o, d_g_hg], [d_wohg, d_gn]

    rows = [(d_yhg, D_MODEL, 0), (o, D_MODEL, 0), (proj, D_MODEL, COL_G // D_MODEL)]
    outs, accs = _rowwise("bwd_hg_path", fn, S, 512, rows, [hg_norm_g, w_o_hg], [(D_MODEL, BF16)] * 2,
                          [((D_MODEL, D_MODEL), F32), ((1, D_MODEL), F32)], 56, parts=2)
    return outs, accs


def _stage_bwd_s5_path(d_ys5, ys, glu, proj, w_o_s5, w_glu):
    S = ys.shape[0]

    def fn(rv, cr, out):
        d_ys5_b, ys_b, glu_b, z_s = rv
        wos5_ref, wglu_ref = cr
        ga, gb = glu_b[:, :S5_WIDTH], glu_b[:, S5_WIDTH:]
        sgb, silz = _sigmoid(gb), _silu(z_s)
        ys2 = ga * sgb * silz
        d_wos5 = _bdot_tn(ys2, d_ys5_b)
        d_ys2 = _bdot_nt(d_ys5_b, wos5_ref[...])
        d_ga = d_ys2 * sgb * silz
        d_gb = d_ys2 * ga * sgb * (1.0 - sgb) * silz
        d_z = d_ys2 * ga * sgb * _dsilu(z_s)
        d_glu = jnp.concatenate([d_ga, d_gb], axis=1)
        gl = _gelu(ys_b)
        d_wglu = _bdot_tn(gl, d_glu)
        d_bglu = jnp.sum(d_glu, axis=0, keepdims=True)
        d_gl = _bdot_nt(d_glu, wglu_ref[...])
        return [d_gl * _dgelu(ys_b), d_z], [d_wos5, d_wglu, d_bglu]

    rows = [(d_ys5, D_MODEL, 0), (ys, S5_WIDTH, 0), (glu, D_MODEL, 0), (proj, S5_WIDTH, COL_ZS // S5_WIDTH)]
    outs, accs = _rowwise(
        "bwd_s5_path", fn, S, 512, rows, [w_o_s5, w_glu], [(S5_WIDTH, F32), (S5_WIDTH, BF16)],
        [((S5_WIDTH, D_MODEL), F32), ((S5_WIDTH, D_MODEL), F32), ((1, D_MODEL), F32)], 48, parts=2)
    return outs, accs


def _stage_inproj_bwd(pieces, x, dh1, norm_g, w_in_all):
    S = x.shape[0]

    def fn(rv, cr, out):
        g_ref, w_ref = cr
        x_b, dh1_b = rv[8], rv[9]
        dproj_ref = out[1]
        col = 0
        for v in rv[:8]:
            dproj_ref[:, col:col + v.shape[1]] = v.astype(BF16)
            col += v.shape[1]
        d_u = jnp.zeros((x_b.shape[0], D_MODEL), F32)
        for j in range(N_DEV):
            d_u = d_u + lax.dot_general(dproj_ref[:, j * SHARD_IN:(j + 1) * SHARD_IN], w_ref[j],
                                        (((1,), (1,)), ((), ())), preferred_element_type=F32)
        _, n, r = _rms_fwd(x_b, g_ref[...])
        dx, d_g = _rms_bwd(d_u, n, r, g_ref[...])
        return [dh1_b + dx, None], [d_g]

    rows = [(a, a.shape[1], 0) for a in pieces] + [(x, D_MODEL, 0), (dh1, D_MODEL, 0)]
    (grad_x, dproj), (d_g,) = _rowwise(
        "inproj_bwd", fn, S, 256, rows, [norm_g, w_in_all], [(D_MODEL, F32), (IN_COLS, BF16)],
        [((1, D_MODEL), F32)], 56)
    return grad_x, dproj, d_g


def _chunk_row(shape):
    return lax.broadcasted_iota(jnp.int32, shape, 0) & (HG_CHUNK - 1)


def _chunk_cumsum(x):
    r_in = _chunk_row(x.shape)
    s = 1
    while s < HG_CHUNK:
        x = x + jnp.where(r_in >= s, pltpu.roll(x, s, 0), 0.0)
        s *= 2
    return x


def _chunk_suffix_sum(x):
    n = x.shape[0]
    r_in = _chunk_row(x.shape)
    s = 1
    while s < HG_CHUNK:
        x = x + jnp.where(r_in < HG_CHUNK - s, pltpu.roll(x, n - s, 0), 0.0)
        s *= 2
    return x


def _hgrn_prep(q, fl, lb):
    sup = q.shape[0]
    nc = sup // HG_CHUNK
    sig = _sigmoid(fl)
    f = lb + (1.0 - lb) * sig
    k = (1.0 - lb) * (1.0 - sig)
    b = _chunk_cumsum(jnp.log(f))
    b3 = b.reshape(nc, HG_CHUNK, HG_DIM)
    row3 = lax.broadcasted_iota(jnp.int32, b3.shape, 1)
    pick = lambda r: jnp.sum(jnp.where(row3 == r, b3, 0.0), axis=1, keepdims=True)
    b_mid = pick(HG_CHUNK // 2 - 1)
    b_last = pick(HG_CHUNK - 1)
    flat = lambda t: t.reshape(sup, HG_DIM)
    e_qa = flat(jnp.exp(b3 - b_mid))
    e_ka = flat(jnp.exp(b_mid - b3))
    e_qd = jnp.exp(b)
    e_kd = flat(jnp.exp(b_last - b3))
    dc = jnp.exp(b_last)
    return sig, f, k, e_qa, e_ka, e_qd, e_kd, dc


def _hgrn_mask(sup):
    r = lax.broadcasted_iota(jnp.int32, (sup, sup), 0)
    c = lax.broadcasted_iota(jnp.int32, (sup, sup), 1)
    shift = HG_CHUNK.bit_length() - 1
    return (jnp.right_shift(r, shift) == jnp.right_shift(c, shift)) & (r >= c)


def _hgrn_fwd(proj, lb):
    S = proj.shape[0]
    sup = HG_SUPER_FWD
    nb = S // sup
    nc = sup // HG_CHUNK
    hp = HG_HEADS_PER_STEP
    wide = hp * HG_DIM

    def body(q_ref, f_ref, iv_ref, lb_ref, o_ref, st_ref, state):
        @pl.when(pl.program_id(1) == 0)
        def _():
            state[...] = jnp.zeros(state.shape, F32)

        mask = _hgrn_mask(sup)
        for hh in range(hp):
            lanes = slice(hh * HG_DIM, (hh + 1) * HG_DIM)
            q, iv = q_ref[:, lanes], iv_ref[:, lanes]
            _, _, k, e_qa, e_ka, e_qd, e_kd, dc = _hgrn_prep(q, f_ref[:, lanes], lb_ref[:, lanes])
            scores = jnp.where(mask, _bdot_nt(q * e_qa, k * e_ka), 0.0)
            o_intra = _bdot(scores, iv)
            qd, kd = q * e_qd, k * e_kd
            for c in range(nc):
                sl = slice(c * HG_CHUNK, (c + 1) * HG_CHUNK)
                st = state[hh]
                st_ref[hh, c] = st
                o_ref[sl, lanes] = o_intra[sl] + _bdot_nt(qd[sl], st)
                state[hh] = dc[c] * st + _bdot_tn(iv[sl], kd[sl])

    blk = lambda base: pl.BlockSpec((sup, wide), functools.partial(lambda h, i, b: (i, b + h), b=base // wide))
    return _call(
        body, name="hgrn_fwd", grid=(HG_HEADS // hp, nb),
        in_specs=[blk(COL_Q), blk(COL_F), blk(COL_I), pl.BlockSpec((1, wide), lambda h, i: (0, h))],
        out_specs=[pl.BlockSpec((sup, wide), lambda h, i: (i, h)),
                   pl.BlockSpec((hp, nc, HG_DIM, HG_DIM), lambda h, i: (h, i, 0, 0))],
        out_shape=[jax.ShapeDtypeStruct((S, D_MODEL), F32),
                   jax.ShapeDtypeStruct((HG_HEADS, S // HG_CHUNK, HG_DIM, HG_DIM), F32)],
        scratch_shapes=[pltpu.VMEM((hp, HG_DIM, HG_DIM), F32)],
        compiler_params=_params(40, 2),
    )(proj, proj, proj, lb)


def _hgrn_bwd(proj, lb, d_o, states):
    S = proj.shape[0]
    sup = HG_SUPER_BWD
    nb = S // sup
    nc = sup // HG_CHUNK
    hp = HG_HEADS_PER_STEP
    wide = hp * HG_DIM

    def body(q_ref, f_ref, iv_ref, lb_ref, do_ref, st_ref, dq_ref, df_ref, div_ref, dlb_ref, dstate):
        @pl.when(pl.program_id(1) == 0)
        def _():
            dstate[...] = jnp.zeros(dstate.shape, F32)
            dlb_ref[...] = jnp.zeros(dlb_ref.shape, F32)

        mask = _hgrn_mask(sup)
        for hh in range(hp):
            lanes = slice(hh * HG_DIM, (hh + 1) * HG_DIM)
            q, iv, do, lb_v = q_ref[:, lanes], iv_ref[:, lanes], do_ref[:, lanes], lb_ref[:, lanes]
            sig, f, k, e_qa, e_ka, e_qd, e_kd, dc = _hgrn_prep(q, f_ref[:, lanes], lb_v)
            qa, ka, qd, kd = q * e_qa, k * e_ka, q * e_qd, k * e_kd
            scores = jnp.where(mask, _bdot_nt(qa, ka), 0.0)
            d_scores = jnp.where(mask, _bdot_nt(do, iv), 0.0)
            d_iv_intra = _bdot_tn(scores, do)
            d_qa = _bdot(d_scores, ka)
            d_ka = _bdot_tn(d_scores, qa)
            d_qd, d_kd, d_last = [None] * nc, [None] * nc, [None] * nc
            for c in reversed(range(nc)):
                sl = slice(c * HG_CHUNK, (c + 1) * HG_CHUNK)
                st = st_ref[hh, c]
                ds = dstate[hh]
                d_qd[c] = _bdot(do[sl], st)
                d_kd[c] = _bdot(iv[sl], ds)
                div_ref[sl, lanes] = (d_iv_intra[sl] + _bdot_nt(kd[sl], ds)).astype(div_ref.dtype)
                d_last[c] = (jnp.sum(ds * st, axis=0, keepdims=True) * dc[c]
                             + jnp.sum(d_kd[c] * kd[sl], axis=0, keepdims=True))
                dstate[hh] = dc[c] * ds + _bdot_tn(do[sl], qd[sl])
            d_qd = jnp.concatenate(d_qd, axis=0)
            d_kd = jnp.concatenate(d_kd, axis=0)
            d_b = d_qa * qa - d_ka * ka + d_qd * qd - d_kd * kd
            last_rows = jnp.concatenate([jnp.broadcast_to(t, (HG_CHUNK, HG_DIM)) for t in d_last], axis=0)
            d_b = d_b + jnp.where(_chunk_row(d_b.shape) == HG_CHUNK - 1, last_rows, 0.0)
            d_logf = _chunk_suffix_sum(d_b)
            d_k = d_ka * e_ka + d_kd * e_kd
            g_f = d_logf / f
            d_sig = (g_f - d_k) * (1.0 - lb_v)
            dq_ref[:, lanes] = (d_qa * e_qa + d_qd * e_qd).astype(dq_ref.dtype)
            df_ref[:, lanes] = (d_sig * sig * (1.0 - sig)).astype(df_ref.dtype)
            d_lb = jnp.sum((g_f - d_k) * (1.0 - sig), axis=0, keepdims=True)
            dlb_ref[:, lanes] += jnp.broadcast_to(d_lb, (8, HG_DIM))

    rev = lambda i: nb - 1 - i
    blk = lambda base: pl.BlockSpec((sup, wide), functools.partial(lambda h, i, b: (rev(i), b + h), b=base // wide))
    row_out = pl.BlockSpec((sup, wide), lambda h, i: (rev(i), h))
    dq, df, div, dlb = _call(
        body, name="hgrn_bwd", grid=(HG_HEADS // hp, nb),
        in_specs=[blk(COL_Q), blk(COL_F), blk(COL_I), pl.BlockSpec((1, wide), lambda h, i: (0, h)),
                  pl.BlockSpec((sup, wide), lambda h, i: (rev(i), h)),
                  pl.BlockSpec((hp, nc, HG_DIM, HG_DIM), lambda h, i: (h, rev(i), 0, 0))],
        out_specs=[row_out, row_out, row_out, pl.BlockSpec((8, wide), lambda h, i: (0, h))],
        out_shape=[jax.ShapeDtypeStruct((S, D_MODEL), BF16)] * 3 + [jax.ShapeDtypeStruct((8, D_MODEL), F32)],
        scratch_shapes=[pltpu.VMEM((hp, HG_DIM, HG_DIM), F32)],
        compiler_params=_params(40, 2),
    )(proj, proj, proj, lb, d_o, states)
    return dq, df, div, dlb[0:1]


def _s5_matrices(a_re, a_im, log_dt, b_re, b_im, c_re, c_im, d, seg_len):
    dt = jnp.exp(log_dt)[:, None]
    mag = jnp.exp(a_re * dt)
    lr, li = mag * jnp.cos(a_im * dt), mag * jnp.sin(a_im * dt)
    den = a_re * a_re + a_im * a_im
    nr = lr - 1.0
    sr = (nr * a_re + li * a_im) / den
    si = (li * a_re - nr * a_im) / den
    bbr = sr[..., None] * b_re - si[..., None] * b_im
    bbi = sr[..., None] * b_im + si[..., None] * b_re
    eye = jnp.eye(8, dtype=F32)

    def quad_cols(v):
        return v.reshape(S5_QUADS, 8 * S5_STATE)

    def lam_row(re_part, im_part):
        row = jnp.concatenate([quad_cols(re_part), quad_cols(im_part)], axis=1).reshape(1, S5_COLS)
        return jnp.broadcast_to(row, (S5_SEG, S5_COLS))

    def b_mat(bb):
        t = bb.reshape(S5_QUADS, 8, S5_STATE, S5_CH)
        return jnp.einsum("qgnc,gh->qgchn", t, eye).reshape(S5_QUADS, 8 * S5_CH, 8 * S5_STATE)

    def c_mat(cc):
        t = cc.reshape(S5_QUADS, 8, S5_CH, S5_STATE)
        return jnp.einsum("qgcn,gh->qgnhc", t, eye).reshape(S5_QUADS, 8 * S5_STATE, 8 * S5_CH)

    ang = a_im * dt * seg_len
    magp = jnp.exp(a_re * dt * seg_len)
    lpr, lpi = magp * jnp.cos(ang), magp * jnp.sin(ang)
    return dict(
        lam_r=lam_row(lr, lr), lam_i=lam_row(-li, li),
        b_q=jnp.concatenate([b_mat(bbr), b_mat(bbi)], axis=2),
        c_q=jnp.concatenate([c_mat(c_re), -c_mat(c_im)], axis=1),
        d_row=d.reshape(1, S5_WIDTH), pow_r=quad_cols(lpr), pow_i=quad_cols(lpi),
    )


def _s5_parts(v):
    half = S5_QCOLS // 2
    return tuple(v[:, k * half:(k + 1) * half] for k in range(2 * S5_QUADS))


def _s5_advance(parts, lr_ref, li_ref, x_ref, sl, conj):
    half = S5_QCOLS // 2
    out = []
    for q in range(S5_QUADS):
        re_c = slice(q * S5_QCOLS, q * S5_QCOLS + half)
        im_c = slice(q * S5_QCOLS + half, (q + 1) * S5_QCOLS)
        lr, li = lr_ref[:, re_c], li_ref[:, im_c]
        hr, hi = parts[2 * q], parts[2 * q + 1]
        if conj:
            out += [lr * hr + li * hi + x_ref[sl, re_c], lr * hi - li * hr + x_ref[sl, im_c]]
        else:
            out += [lr * hr - li * hi + x_ref[sl, re_c], lr * hi + li * hr + x_ref[sl, im_c]]
    return tuple(out)


def _scan_loop(step, init):
    def trip(o, carry):
        for j in range(S5_UNROLL):
            carry = step(o * S5_UNROLL + j, carry)
        return carry

    return lax.fori_loop(0, S5_TILE_STEPS // S5_UNROLL, trip, init)


def _s5_store(ref, sl, parts):
    half = S5_QCOLS // 2
    for k, v in enumerate(parts):
        ref[sl, k * half:(k + 1) * half] = v


def _s5_fwd_pass(u_perm, mats, h0, with_output):
    S = u_perm.shape[0]
    rows = S5_TILE_STEPS * S5_SEG
    nt = S // rows

    def body(*refs):
        if with_output:
            u_ref, b_ref, lr_ref, li_ref, h0_ref, c_ref, d_ref, y_ref, hinit_ref, hend_ref, xs, hcar = refs
        else:
            u_ref, b_ref, lr_ref, li_ref, h0_ref, hend_ref, xs, hcar = refs

        @pl.when(pl.program_id(0) == 0)
        def _():
            hcar[...] = h0_ref[...]

        if with_output:
            hinit_ref[...] = hcar[...]
        u = u_ref[...]
        ub = u.astype(BF16)
        for q in range(S5_QUADS):
            xs[:, q * S5_QCOLS:(q + 1) * S5_QCOLS] = jnp.dot(ub[:, q * 128:(q + 1) * 128], b_ref[q], preferred_element_type=F32)

        def step(t, h):
            sl = pl.ds(pl.multiple_of(t * S5_SEG, S5_SEG), S5_SEG)
            hn = _s5_advance(h, lr_ref, li_ref, xs, sl, False)
            _s5_store(xs, sl, hn)
            return hn

        h = _scan_loop(step, _s5_parts(hcar[...]))
        _s5_store(hcar, slice(None), h)
        _s5_store(hend_ref, slice(None), h)
        if with_output:
            ys = [jnp.dot(xs[:, q * S5_QCOLS:(q + 1) * S5_QCOLS].astype(BF16), c_ref[q], preferred_element_type=F32)
                  for q in range(S5_QUADS)]
            y_ref[...] = jnp.concatenate(ys, axis=1) + d_ref[...] * u

    full = lambda a: pl.BlockSpec(a.shape, functools.partial(lambda i, nd: (0,) * nd, nd=a.ndim))
    ins = [u_perm, mats["b_q"], mats["lam_r"], mats["lam_i"], h0]
    in_specs = [pl.BlockSpec((rows, S5_WIDTH), lambda i: (i, 0))] + [full(a) for a in ins[1:]]
    out_specs = [pl.BlockSpec((S5_SEG, S5_COLS), lambda i: (0, 0))]
    out_shape = [jax.ShapeDtypeStruct((S5_SEG, S5_COLS), F32)]
    if with_output:
        ins += [mats["c_q"], mats["d_row"]]
        in_specs += [full(mats["c_q"]), full(mats["d_row"])]
        out_specs = [pl.BlockSpec((rows, S5_WIDTH), lambda i: (i, 0)),
                     pl.BlockSpec((None, S5_SEG, S5_COLS), lambda i: (i, 0, 0))] + out_specs
        out_shape = [jax.ShapeDtypeStruct((S, S5_WIDTH), F32), jax.ShapeDtypeStruct((nt, S5_SEG, S5_COLS), F32)] + out_shape
    return _call(
        body, name="s5_fwd_y" if with_output else "s5_fwd_ends", grid=(nt,), in_specs=in_specs, out_specs=out_specs,
        out_shape=out_shape,
        scratch_shapes=[pltpu.VMEM((rows, S5_COLS), F32), pltpu.VMEM((S5_SEG, S5_COLS), F32)],
        compiler_params=_params(40),
    )(*ins)


def _s5_bwd_ends(dy_perm, mats):
    S = dy_perm.shape[0]
    rows = S5_TILE_STEPS * S5_SEG
    nt = S // rows

    def body(dy_ref, c_ref, lr_ref, li_ref, gend_ref, gs, gcar):
        @pl.when(pl.program_id(0) == 0)
        def _():
            gcar[...] = jnp.zeros(gcar.shape, F32)

        dyb = dy_ref[...].astype(BF16)
        for q in range(S5_QUADS):
            gs[:, q * S5_QCOLS:(q + 1) * S5_QCOLS] = lax.dot_general(
                dyb[:, q * 128:(q + 1) * 128], c_ref[q], (((1,), (1,)), ((), ())), preferred_element_type=F32)

        def step(k, g):
            t = S5_TILE_STEPS - 1 - k
            sl = pl.ds(pl.multiple_of(t * S5_SEG, S5_SEG), S5_SEG)
            return _s5_advance(g, lr_ref, li_ref, gs, sl, True)

        g = _scan_loop(step, _s5_parts(gcar[...]))
        _s5_store(gcar, slice(None), g)
        _s5_store(gend_ref, slice(None), g)

    full = lambda a: pl.BlockSpec(a.shape, functools.partial(lambda i, nd: (0,) * nd, nd=a.ndim))
    return _call(
        body, name="s5_bwd_ends", grid=(nt,),
        in_specs=[pl.BlockSpec((rows, S5_WIDTH), lambda i: (nt - 1 - i, 0)), full(mats["c_q"]), full(mats["lam_r"]),
                  full(mats["lam_i"])],
        out_specs=pl.BlockSpec((S5_SEG, S5_COLS), lambda i: (0, 0)),
        out_shape=jax.ShapeDtypeStruct((S5_SEG, S5_COLS), F32),
        scratch_shapes=[pltpu.VMEM((rows, S5_COLS), F32), pltpu.VMEM((S5_SEG, S5_COLS), F32)],
        compiler_params=_params(40),
    )(dy_perm, mats["c_q"], mats["lam_r"], mats["lam_i"])


def _s5_bwd_full(u_perm, dy_perm, hinit, g0, mats):
    S = u_perm.shape[0]
    rows = S5_TILE_STEPS * S5_SEG
    nt = S // rows

    def body(u_ref, dy_ref, hinit_ref, g0_ref, b_ref, c_ref, lr_ref, li_ref, d_ref,
             du_ref, dp_ref, dq_ref, db_ref, dc_ref, dd_ref, hs, gs, gcar):
        @pl.when(pl.program_id(0) == 0)
        def _():
            gcar[...] = g0_ref[...]
            for ref in (dp_ref, dq_ref, db_ref, dc_ref, dd_ref):
                ref[...] = jnp.zeros(ref.shape, F32)

        u, dy = u_ref[...], dy_ref[...]
        ub, dyb = u.astype(BF16), dy.astype(BF16)
        hs[0:S5_SEG, :] = hinit_ref[...]
        for q in range(S5_QUADS):
            cols = slice(q * S5_QCOLS, (q + 1) * S5_QCOLS)
            hs[S5_SEG:, cols] = jnp.dot(ub[:, q * 128:(q + 1) * 128], b_ref[q], preferred_element_type=F32)
            gs[:, cols] = lax.dot_general(dyb[:, q * 128:(q + 1) * 128], c_ref[q], (((1,), (1,)), ((), ())),
                                          preferred_element_type=F32)

        def fstep(t, h):
            sl = pl.ds(pl.multiple_of((t + 1) * S5_SEG, S5_SEG), S5_SEG)
            hn = _s5_advance(h, lr_ref, li_ref, hs, sl, False)
            _s5_store(hs, sl, hn)
            return hn

        _scan_loop(fstep, _s5_parts(hinit_ref[...]))

        def bstep(k, g):
            t = S5_TILE_STEPS - 1 - k
            sl = pl.ds(pl.multiple_of(t * S5_SEG, S5_SEG), S5_SEG)
            gn = _s5_advance(g, lr_ref, li_ref, gs, sl, True)
            _s5_store(gs, sl, gn)
            return gn

        _s5_store(gcar, slice(None), _scan_loop(bstep, _s5_parts(gcar[...])))

        half = S5_QCOLS // 2
        dus = []
        for q in range(S5_QUADS):
            cols = slice(q * S5_QCOLS, (q + 1) * S5_QCOLS)

            def astep(t, carry, q=q):
                sl = pl.ds(pl.multiple_of(t * S5_SEG, S5_SEG), S5_SEG)
                g = gs[sl, q * S5_QCOLS:(q + 1) * S5_QCOLS]
                hp = hs[sl, q * S5_QCOLS:(q + 1) * S5_QCOLS]
                hp_sw = jnp.concatenate([hp[:, half:], hp[:, :half]], axis=1)
                return carry[0] + g * hp, carry[1] + g * hp_sw

            zero = jnp.zeros((S5_SEG, S5_QCOLS), F32)
            acc_p, acc_q = _scan_loop(astep, (zero, zero))
            dp_ref[:, cols] += jnp.sum(acc_p, axis=0, keepdims=True)
            dq_ref[:, cols] += jnp.sum(acc_q, axis=0, keepdims=True)
            gq = gs[:, cols].astype(BF16)
            db_ref[q] += lax.dot_general(ub[:, q * 128:(q + 1) * 128], gq, (((0,), (0,)), ((), ())),
                                         preferred_element_type=F32)
            hq = hs[S5_SEG:, cols].astype(BF16)
            dc_ref[q] += lax.dot_general(dyb[:, q * 128:(q + 1) * 128], hq, (((0,), (0,)), ((), ())),
                                         preferred_element_type=F32)
            dus.append(lax.dot_general(gq, b_ref[q], (((1,), (1,)), ((), ())), preferred_element_type=F32))
        du_ref[...] = (jnp.concatenate(dus, axis=1) + d_ref[...] * dy).astype(du_ref.dtype)
        dd_ref[...] += jnp.sum(dy * u, axis=0, keepdims=True)

    full = lambda a: pl.BlockSpec(a.shape, functools.partial(lambda i, nd: (0,) * nd, nd=a.ndim))
    rev_rows = pl.BlockSpec((rows, S5_WIDTH), lambda i: (nt - 1 - i, 0))
    consts = [mats["b_q"], mats["c_q"], mats["lam_r"], mats["lam_i"], mats["d_row"]]
    acc = lambda s: pl.BlockSpec(s, functools.partial(lambda i, nd: (0,) * nd, nd=len(s)))
    acc_shapes = [(1, S5_COLS), (1, S5_COLS), (S5_QUADS, 128, S5_QCOLS), (S5_QUADS, 128, S5_QCOLS), (1, S5_WIDTH)]
    return _call(
        body, name="s5_bwd_full", grid=(nt,),
        in_specs=[rev_rows, rev_rows, pl.BlockSpec((None, S5_SEG, S5_COLS), lambda i: (nt - 1 - i, 0, 0)), full(g0)]
        + [full(a) for a in consts],
        out_specs=[rev_rows] + [acc(s) for s in acc_shapes],
        out_shape=[jax.ShapeDtypeStruct((S, S5_WIDTH), BF16)] + [jax.ShapeDtypeStruct(s, F32) for s in acc_shapes],
        scratch_shapes=[pltpu.VMEM((rows + S5_SEG, S5_COLS), F32), pltpu.VMEM((rows, S5_COLS), F32),
                        pltpu.VMEM((S5_SEG, S5_COLS), F32)],
        compiler_params=_params(56),
    )(u_perm, dy_perm, hinit, g0, *consts)


def _cmul(ar, ai, br, bi):
    return ar * br - ai * bi, ar * bi + ai * br


def _split_cols(v):
    t = v.reshape(v.shape[0], S5_QUADS, 2, S5_QCOLS // 2)
    return t[:, :, 0], t[:, :, 1]


def _join_cols(re, im):
    return jnp.stack([re, im], axis=2).reshape(re.shape[0], S5_COLS)


def _segment_starts(ends, pow_r, pow_i, reverse):
    er, ei = _split_cols(ends)
    pi = -pow_i if reverse else pow_i
    order = list(range(S5_SEG))
    if reverse:
        order = order[::-1]
    cr, ci = jnp.zeros_like(er[0]), jnp.zeros_like(ei[0])
    out_r, out_i = [None] * S5_SEG, [None] * S5_SEG
    for j in order:
        out_r[j], out_i[j] = cr, ci
        mr, mi = _cmul(pow_r, pi, cr, ci)
        cr, ci = mr + er[j], mi + ei[j]
    return _join_cols(jnp.stack(out_r), jnp.stack(out_i))


def _to_segments(a):
    S, w = a.shape
    return a.reshape(S5_SEG, S // S5_SEG, w).transpose(1, 0, 2).reshape(S, w)


def _from_segments(a):
    S, w = a.shape
    return a.reshape(S // S5_SEG, S5_SEG, w).transpose(1, 0, 2).reshape(S, w)


def _my_pos():
    return lax.axis_index("x"), lax.axis_index("y"), lax.axis_index("c")


def _flip(pos, k):
    x, y, c = pos
    return (1 - x if k & 4 else x, 1 - y if k & 2 else y, 1 - c if k & 1 else c)


def _index_of(pos):
    return 4 * pos[0] + 2 * pos[1] + pos[2]


_GATHER_FLIPS = (0, 1, 4, 5, 2, 3, 6, 7)


def _inproj_gather(x, norm_g, pack_a, pack_b, pack_c):
    S = x.shape[0]
    tm = min(S, 1024)
    n_i = S // tm
    order = jnp.stack([_index_of(_flip(_my_pos(), k)) for k in _GATHER_FLIPS]).astype(jnp.int32)

    def body(order_ref, x_ref, g_ref, pa_ref, pb_ref, pc_ref, ut_ref, proj_ref, oa_ref, ob_ref, oc_ref,
             wv, u_scr, send_sems, recv_sems, local_sems):
        s, i = pl.program_id(0), pl.program_id(1)
        me = _my_pos()
        mine = _index_of(me)
        sibling = _flip(me, 1)
        srcs = (pb_ref, pa_ref, pc_ref)
        dsts = (wv, oa_ref, oc_ref)

        def direct(a, k):
            return pltpu.make_async_remote_copy(
                src_ref=srcs[a], dst_ref=dsts[a].at[mine], send_sem=send_sems.at[a * 8 + k],
                recv_sem=recv_sems.at[a * 8 + k], device_id=_flip(me, k), device_id_type=MESH)

        def passed_on(a, k):
            slot = _index_of(_flip(me, k))
            return pltpu.make_async_remote_copy(
                src_ref=dsts[a].at[slot], dst_ref=dsts[a].at[slot], send_sem=send_sems.at[a * 8 + (k | 1)],
                recv_sem=recv_sems.at[a * 8 + (k | 1)], device_id=sibling, device_id_type=MESH)

        def arrival(a, k):
            slot = _index_of(_flip(me, k))
            pltpu.make_async_remote_copy(
                src_ref=dsts[a].at[slot], dst_ref=dsts[a].at[slot], send_sem=send_sems.at[a * 8 + k],
                recv_sem=recv_sems.at[a * 8 + k], device_id=me, device_id_type=MESH).wait_recv()

        def own_copy(a):
            return pltpu.make_async_copy(srcs[a], dsts[a].at[mine], local_sems.at[a])

        def keep(idx):
            slot = _index_of(_flip(me, _GATHER_FLIPS[idx]))
            return pltpu.make_async_copy(wv.at[slot], ob_ref.at[slot], local_sems.at[3 + idx])


        first = (s == 0) & (i == 0)

        @pl.when(first)
        def _():
            for a in range(3):
                own_copy(a).start()
            for k in (1, 4, 2):
                direct(0, k).start()
            own_copy(0).wait()
            keep(0).start()

        for idx, k in enumerate(_GATHER_FLIPS):
            if idx == 0:
                continue

            @pl.when((s == idx) & (i == 0))
            def _(idx=idx, k=k):
                arrival(0, k)
                if k in (4, 2, 6):
                    passed_on(0, k).start()
                keep(idx).start()
                if idx == 1:
                    direct(0, 6).start()
                if idx == 2:
                    for a in (1, 2):
                        for k in (1, 4, 2, 6):
                            direct(a, k).start()

        @pl.when(s == 0)
        def _():
            y, _, _ = _rms_fwd(x_ref[...], g_ref[...])
            u_scr[pl.ds(pl.multiple_of(i * tm, tm), tm), :] = y.astype(BF16)
            ut_ref[...] = y.T.astype(BF16)

        ub = u_scr[pl.ds(pl.multiple_of(i * tm, tm), tm), :]
        proj_ref[...] = jnp.dot(ub, wv[order_ref[s]], preferred_element_type=F32)

        @pl.when((s == N_DEV - 1) & (i == n_i - 1))
        def _():
            for a in (1, 2):
                for k in (4, 2, 6):
                    arrival(a, k)
                    passed_on(a, k).start()
            for a in (1, 2):
                for k in (1, 5, 3, 7):
                    arrival(a, k)
                own_copy(a).wait()
            for a in range(3):
                for k in (1, 4, 2, 6):
                    direct(a, k).wait_send()
                for k in (4, 2, 6):
                    passed_on(a, k).wait_send()
            for idx in range(N_DEV):
                keep(idx).wait()

    any_spec = pl.BlockSpec(memory_space=pl.ANY)
    vmem = pl.BlockSpec(memory_space=pltpu.VMEM)
    grid_spec = pltpu.PrefetchScalarGridSpec(
        num_scalar_prefetch=1, grid=(N_DEV, n_i),
        in_specs=[pl.BlockSpec((tm, D_MODEL), lambda s, i, o: (jnp.where(s == 0, i, 0), 0)),
                  pl.BlockSpec((1, D_MODEL), lambda s, i, o: (0, 0)), any_spec, vmem, any_spec],
        out_specs=[pl.BlockSpec((D_MODEL, tm), lambda s, i, o: (0, jnp.where(s == 0, i, n_i - 1))),
                   pl.BlockSpec((tm, SHARD_IN), lambda s, i, o: (i, o[s])), any_spec, any_spec, any_spec],
        scratch_shapes=[pltpu.VMEM((N_DEV,) + pack_b.shape, BF16), pltpu.VMEM((S, D_MODEL), BF16),
                        pltpu.SemaphoreType.DMA((24,)), pltpu.SemaphoreType.DMA((24,)), pltpu.SemaphoreType.DMA((3 + N_DEV,))],
    )
    return _call(
        body, name="inproj_gather", grid_spec=grid_spec,
        out_shape=[jax.ShapeDtypeStruct((D_MODEL, S), BF16), jax.ShapeDtypeStruct((S, IN_COLS), F32),
                   jax.ShapeDtypeStruct((N_DEV,) + pack_a.shape, BF16), jax.ShapeDtypeStruct((N_DEV,) + pack_b.shape, BF16),
                   jax.ShapeDtypeStruct((N_DEV,) + pack_c.shape, BF16)],
        compiler_params=_params(56, 2),
    )(order, x, norm_g, pack_a, pack_b, pack_c)


_SCATTER_FLIPS = (7, 6, 5, 4, 3, 2, 1, 0)
_N_CHIPS = 4


def _for_row_chunks(n_rows, chunk, fn):
    def step(c, carry):
        fn(pl.ds(pl.multiple_of(c * chunk, chunk), chunk))
        return carry

    lax.fori_loop(0, n_rows // chunk, step, 0)


def _grad_w_in_scatter(dproj, u_t, rs_a, rs_c, small_partial):
    S = u_t.shape[1]
    tm = min(S, 1024)
    n_i = S // tm
    order = jnp.stack([_index_of(_flip(_my_pos(), k)) for k in _SCATTER_FLIPS]).astype(jnp.int32)
    shapes = ((D_MODEL, SHARD_IN), rs_a.shape[1:], rs_c.shape[1:])
    row_chunk = 128

    def body(order_ref, dp_ref, ut_ref, ra_ref, rc_ref, p_ref, gb_ref, ga_ref, gc_ref, gs_ref, acc, sib_b, d2d_b,
             send_b, ici_b, mine_a, sib_a, ici_a, mine_c, sib_c, ici_c, gath, send_sems, recv_sems, local_sems):
        s, i = pl.program_id(0), pl.program_id(1)
        me = _my_pos()
        sibling = _flip(me, 1)

        my_chip = 2 * me[0] + me[1]

        small_d2d = pltpu.make_async_remote_copy(
            src_ref=p_ref, dst_ref=gath.at[_N_CHIPS], send_sem=send_sems.at[21], recv_sem=recv_sems.at[21],
            device_id=sibling, device_id_type=MESH)

        def small_ici(m):
            return pltpu.make_async_remote_copy(
                src_ref=gath.at[my_chip], dst_ref=gath.at[my_chip], send_sem=send_sems.at[22 + m],
                recv_sem=recv_sems.at[22 + m], device_id=_flip(me, 6 - 2 * m), device_id_type=MESH)
        sib = (sib_b, sib_a, sib_c)
        ici = (ici_b, ici_a, ici_c)
        outs = (gb_ref, ga_ref, gc_ref)

        def to_sibling(arr, m, src):
            return pltpu.make_async_remote_copy(
                src_ref=src, dst_ref=sib[arr].at[m], send_sem=send_sems.at[arr * 7 + m],
                recv_sem=recv_sems.at[arr * 7 + m], device_id=sibling, device_id_type=MESH)

        def over_ici(arr, m, src):
            return pltpu.make_async_remote_copy(
                src_ref=src, dst_ref=ici[arr].at[m], send_sem=send_sems.at[arr * 7 + 4 + m],
                recv_sem=recv_sems.at[arr * 7 + 4 + m], device_id=_flip(me, 6 - 2 * m), device_id_type=MESH)

        def from_sibling(arr, m):
            to_sibling(arr, m, sib[arr].at[m]).wait_recv()

        def from_ici(arr, m):
            over_ici(arr, m, ici[arr].at[m]).wait_recv()

        small = ((1, ra_ref, mine_a), (2, rc_ref, mine_c))

        def local_copy(arr, src, mine, m):
            return pltpu.make_async_copy(src.at[_index_of(_flip(me, 6 - 2 * m))], mine.at[m],
                                         local_sems.at[(arr - 1) * _N_CHIPS + m])

        @pl.when((s == 0) & (i == 0))
        def _():
            small_d2d.start()
            for arr, src, mine in small:
                for m in range(_N_CHIPS):
                    to_sibling(arr, m, src.at[_index_of(_flip(me, 7 - 2 * m))]).start()
                    local_copy(arr, src, mine, m).start()

        @pl.when((s == 1) & (i == 0))
        def _():
            small_d2d.wait_recv()
            gath[my_chip] = p_ref[...] + gath[_N_CHIPS]
            for m in range(_N_CHIPS - 1):
                small_ici(m).start()
            for arr, src, mine in small:
                rows, chunk = shapes[arr][0], 16
                for m in range(_N_CHIPS):
                    local_copy(arr, src, mine, m).wait()
                    from_sibling(arr, m)
                    if m < _N_CHIPS - 1:
                        def add(sl, arr=arr, mine=mine, m=m):
                            mine[m, sl, :] = (mine[m, sl, :].astype(F32) + sib[arr][m, sl, :].astype(F32)).astype(BF16)

                        _for_row_chunks(rows, chunk, add)
                        over_ici(arr, m, mine.at[m]).start()
                    else:
                        def keep(sl, arr=arr, mine=mine, m=m):
                            outs[arr][sl, :] = mine[m, sl, :].astype(F32) + sib[arr][m, sl, :].astype(F32)

                        _for_row_chunks(rows, chunk, keep)

        @pl.when(i == 0)
        def _():
            acc[...] = jnp.zeros(acc.shape, F32)

        acc[...] += jnp.dot(ut_ref[...], dp_ref[...], preferred_element_type=F32)

        def block_rows(c):
            return acc[c * row_chunk:(c + 1) * row_chunk, :]

        for m in range(_N_CHIPS):
            @pl.when((s == 2 * m) & (i == n_i - 1))
            def _(m=m):
                if m > 0:
                    to_sibling(0, m - 1, d2d_b).wait_send()
                for c in range(D_MODEL // row_chunk):
                    d2d_b[c * row_chunk:(c + 1) * row_chunk, :] = block_rows(c).astype(BF16)
                to_sibling(0, m, d2d_b).start()

            @pl.when((s == 2 * m + 1) & (i == n_i - 1))
            def _(m=m):
                from_sibling(0, m)
                slot = m % 2
                if m == 2:
                    over_ici(0, 0, send_b.at[0]).wait_send()
                for c in range(D_MODEL // row_chunk):
                    rows = slice(c * row_chunk, (c + 1) * row_chunk)
                    total = block_rows(c) + sib_b[m, rows, :].astype(F32)
                    if m < _N_CHIPS - 1:
                        send_b[slot, rows, :] = total.astype(BF16)
                    else:
                        gb_ref[rows, :] = total
                if m < _N_CHIPS - 1:
                    over_ici(0, m, send_b.at[slot]).start()

        @pl.when((s == N_DEV - 1) & (i == n_i - 1))
        def _():
            for arr in range(3):
                for m in range(_N_CHIPS - 1):
                    from_ici(arr, m)
                rows = shapes[arr][0]

                def add(sl, arr=arr):
                    outs[arr][sl, :] = (outs[arr][sl, :] + ici[arr][0, sl, :].astype(F32)
                                        + ici[arr][1, sl, :].astype(F32) + ici[arr][2, sl, :].astype(F32))

                _for_row_chunks(rows, 16, add)
            for m in range(_N_CHIPS - 1):
                small_ici(m).wait_recv()
            gs_ref[...] = (gath[0] + gath[1]) + (gath[2] + gath[3])
            small_d2d.wait_send()
            for m in range(_N_CHIPS - 1):
                small_ici(m).wait_send()
            to_sibling(0, _N_CHIPS - 1, d2d_b).wait_send()
            over_ici(0, 1, send_b.at[1]).wait_send()
            over_ici(0, 2, send_b.at[0]).wait_send()
            for arr, src, mine in small:
                for m in range(_N_CHIPS):
                    to_sibling(arr, m, src.at[0]).wait_send()
                for m in range(_N_CHIPS - 1):
                    over_ici(arr, m, mine.at[m]).wait_send()

    any_spec = pl.BlockSpec(memory_space=pl.ANY)
    vmem = pl.BlockSpec(memory_space=pltpu.VMEM)
    half = lambda shp, n: pltpu.VMEM((n,) + tuple(shp), BF16)
    grid_spec = pltpu.PrefetchScalarGridSpec(
        num_scalar_prefetch=1, grid=(N_DEV, n_i),
        in_specs=[pl.BlockSpec((tm, SHARD_IN), lambda s, i, o: (i, o[s])),
                  pl.BlockSpec((D_MODEL, tm), lambda s, i, o: (0, i)), any_spec, any_spec, vmem],
        out_specs=[vmem, vmem, vmem, vmem],
        scratch_shapes=[
            pltpu.VMEM((D_MODEL, SHARD_IN), F32), half(shapes[0], _N_CHIPS), pltpu.VMEM(shapes[0], BF16),
            half(shapes[0], 2), half(shapes[0], _N_CHIPS - 1),
            half(shapes[1], _N_CHIPS), half(shapes[1], _N_CHIPS), half(shapes[1], _N_CHIPS - 1),
            half(shapes[2], _N_CHIPS), half(shapes[2], _N_CHIPS), half(shapes[2], _N_CHIPS - 1),
            pltpu.VMEM((_N_CHIPS + 1,) + small_partial.shape, F32),
            pltpu.SemaphoreType.DMA((25,)), pltpu.SemaphoreType.DMA((25,)), pltpu.SemaphoreType.DMA((2 * _N_CHIPS,))],
    )
    return _call(
        body, name="grad_w_in_scatter", grid_spec=grid_spec,
        out_shape=[jax.ShapeDtypeStruct(shp, F32) for shp in shapes] + [jax.ShapeDtypeStruct(small_partial.shape, F32)],
        compiler_params=_params(60, 2),
    )(order, dproj, u_t, rs_a, rs_c, small_partial)


def _adam_update(g, w, m, v):
    m2 = ADAM_B1 * m + (1.0 - ADAM_B1) * g
    v2 = ADAM_B2 * v + (1.0 - ADAM_B2) * (g * g)
    m_hat = m2 / (1.0 - ADAM_B1 ** ADAM_STEP)
    v_hat = v2 / (1.0 - ADAM_B2 ** ADAM_STEP)
    delta = -ADAM_LR * (m_hat / (jnp.sqrt(v_hat) + ADAM_EPS) + ADAM_WD * w)
    return delta, m2, v2


def _adam_rows(g, w, m, v):
    rows, cols = w.shape
    tm = rows if rows % 256 else 256

    def fn(rv, cr, out):
        return list(_adam_update(*rv)), []

    outs, _ = _rowwise("adamw", fn, rows, tm, [(a, cols, 0) for a in (g, w, m, v)], [], [(cols, F32)] * 3, [], 32)
    return outs


_SMALL = ["norm_g", "hg_lb", "hg_norm_g", "s5_a_re", "s5_a_im", "s5_log_dt", "s5_b_re", "s5_b_im", "s5_c_re",
          "s5_c_im", "s5_d", "b_glu", "ple_norm_g", "final_norm_g"]
_BIG = ["w_in", "w_o_hg", "w_glu", "w_o_s5", "w_out", "w_ple", "w_ple_gate"]
_ORDER = ["norm_g", "w_in", "hg_lb", "hg_norm_g", "w_o_hg", "s5_a_re", "s5_a_im", "s5_log_dt", "s5_b_re", "s5_b_im",
          "s5_c_re", "s5_c_im", "s5_d", "w_glu", "b_glu", "w_o_s5", "w_out", "ple_norm_g", "w_ple", "w_ple_gate",
          "final_norm_g"]


def _pack_small(vals, tail=None):
    parts = []
    for name in _SMALL:
        flat = vals[name].reshape(-1).astype(F32)
        pad = (-flat.shape[0]) % 1024
        parts.append(jnp.pad(flat, (0, pad)))
    tail = jnp.zeros((0,), F32) if tail is None else tail.reshape(-1).astype(F32)
    parts.append(jnp.pad(tail, (0, 1024 - tail.shape[0])))
    return jnp.concatenate(parts).reshape(-1, 128)


def _unpack_small(packed, like):
    flat = packed.reshape(-1)
    out, off = {}, 0
    for name in _SMALL:
        size = like[name].size
        out[name] = flat[off:off + size].reshape(like[name].shape)
        off += size + (-size) % 1024
    return out


def _col_blocks(full):
    k = full.shape[0]
    return full.reshape(k, N_DEV, 128).transpose(1, 0, 2)


def _from_col_blocks(blocks):
    k = blocks.shape[1]
    return blocks.transpose(1, 0, 2).reshape(k, N_DEV * 128)


def kernel(x, p, norm_g, w_in, hg_lb, hg_norm_g, w_o_hg, s5_a_re, s5_a_im, s5_log_dt, s5_b_re, s5_b_im, s5_c_re, s5_c_im, s5_d, w_glu, b_glu, w_o_s5, w_out, ple_norm_g, w_ple, w_ple_gate, final_norm_g, loss_target, m_norm_g, m_w_in, m_hg_lb, m_hg_norm_g, m_w_o_hg, m_s5_a_re, m_s5_a_im, m_s5_log_dt, m_s5_b_re, m_s5_b_im, m_s5_c_re, m_s5_c_im, m_s5_d, m_w_glu, m_b_glu, m_w_o_s5, m_w_out, m_ple_norm_g, m_w_ple, m_w_ple_gate, m_final_norm_g, v_norm_g, v_w_in, v_hg_lb, v_hg_norm_g, v_w_o_hg, v_s5_a_re, v_s5_a_im, v_s5_log_dt, v_s5_b_re, v_s5_b_im, v_s5_c_re, v_s5_c_im, v_s5_d, v_w_glu, v_b_glu, v_w_o_s5, v_w_out, v_ple_norm_g, v_w_ple, v_w_ple_gate, v_final_norm_g):
    args = dict(locals())
    w = {n: args[n] for n in _ORDER}
    m = {n: args["m_" + n] for n in _ORDER}
    v = {n: args["v_" + n] for n in _ORDER}
    xs = x[0]
    ps = p[0, 0]
    tgt = loss_target[0]
    S = xs.shape[0]

    pack_a = jnp.concatenate([w_o_hg[0], w_out[0], w_ple_gate[0]], axis=0).astype(BF16)
    pack_b = w_in[0].astype(BF16)
    pack_c = jnp.concatenate([w_glu[0], w_o_s5[0], w_ple[0]], axis=0).astype(BF16)
    u_t, proj, all_a, all_b, all_c = _inproj_gather(xs, norm_g, pack_a, pack_b, pack_c)
    wf_o_hg = all_a[:, 0:128].reshape(D_MODEL, D_MODEL)
    wf_out = all_a[:, 128:256].reshape(D_MODEL, D_MODEL)
    wf_pg = all_a[:, 256:384].reshape(D_MODEL, D_MODEL)
    wf_glu = _from_col_blocks(all_c[:, 0:512])
    wf_o_s5 = _from_col_blocks(all_c[:, 512:1024])
    wf_ple = _from_col_blocks(all_c[:, 1024:1280])

    lb = jax.nn.sigmoid(hg_lb[0:1] - hg_lb[1:2])
    s5_names = ["s5_a_re", "s5_a_im", "s5_log_dt", "s5_b_re", "s5_b_im", "s5_c_re", "s5_c_im", "s5_d"]
    build = lambda *a: _s5_matrices(*a, seg_len=S // S5_SEG)
    mats_f32, mats_vjp = jax.vjp(build, *[w[n][0] for n in s5_names])
    mats = dict(mats_f32, b_q=mats_f32["b_q"].astype(BF16), c_q=mats_f32["c_q"].astype(BF16))
    bias_glu = b_glu

    o, states = _hgrn_fwd(proj, lb)
    u_perm = _to_segments(proj[:, COL_US:COL_US + S5_WIDTH])
    zeros_state = jnp.zeros((S5_SEG, S5_COLS), F32)
    (h_ends,) = _s5_fwd_pass(u_perm, mats, zeros_state, False)
    h0 = _segment_starts(h_ends, mats["pow_r"], mats["pow_i"], False)
    y_perm, h_init, _ = _s5_fwd_pass(u_perm, mats, h0, True)
    ys = _from_segments(y_perm)
    y_hg, y_s5, glu, h1 = _stage_branches(o, proj, ys, xs, hg_norm_g, wf_o_hg, wf_glu, bias_glu, wf_o_s5, wf_out)

    dh1, (loss_acc, d_final_g, d_ple_g, d_w_ple, d_w_pg) = _stage_ple_loss(
        h1, ps, tgt, ple_norm_g, wf_ple, wf_pg, final_norm_g.reshape(1, D_MODEL))
    (d_gate_hg, d_gate_s5, d_yhg, d_ys5), (d_w_out,) = _stage_bwd_merge(dh1, y_hg, y_s5, proj, wf_out)
    (d_o, d_g_hg), (d_w_o_hg, d_hg_norm) = _stage_bwd_hg_path(d_yhg, o, proj, hg_norm_g, wf_o_hg)
    (d_ys, d_z), (d_w_o_s5, d_w_glu, d_b_glu) = _stage_bwd_s5_path(d_ys5, ys, glu, proj, wf_o_s5, wf_glu)
    dq, df, div, d_lb = _hgrn_bwd(proj, lb, d_o, states)
    dy_perm = _to_segments(d_ys)
    g_ends = _s5_bwd_ends(dy_perm, mats)
    g0 = _segment_starts(g_ends, mats["pow_r"], mats["pow_i"], True)
    du_perm, acc_p, acc_q, d_bq, d_cq_t, d_d = _s5_bwd_full(u_perm, dy_perm, h_init, g0, mats)
    d_us = _from_segments(du_perm)
    grad_x, dproj, d_norm_g = _stage_inproj_bwd([dq, df, div, d_g_hg, d_us, d_z, d_gate_hg, d_gate_s5], xs, dh1,
                                                norm_g, all_b)

    p_re, p_im = _split_cols(acc_p)
    q_re, q_im = _split_cols(acc_q)
    d_lam_r = (p_re + p_im)[0]
    d_lam_i = (q_im - q_re)[0]
    zero_row = jnp.zeros((S5_SEG, S5_COLS), F32)
    row_of = lambda re_part, im_part: zero_row.at[0].set(_join_cols(re_part[None], im_part[None])[0])
    zeros_q = jnp.zeros_like(d_lam_r)
    cot = dict(
        lam_r=row_of(d_lam_r, zeros_q), lam_i=row_of(zeros_q, d_lam_i),
        b_q=d_bq, c_q=d_cq_t.transpose(0, 2, 1), d_row=d_d,
        pow_r=jnp.zeros_like(mats["pow_r"]), pow_i=jnp.zeros_like(mats["pow_i"]),
    )
    d_s5 = mats_vjp(cot)

    s_lb = lb * (1.0 - lb)
    d_hg_lb = jnp.concatenate([d_lb * s_lb, -d_lb * s_lb], axis=0)
    small_g = dict(norm_g=d_norm_g, hg_lb=d_hg_lb, hg_norm_g=d_hg_norm, b_glu=d_b_glu, ple_norm_g=d_ple_g,
                   final_norm_g=d_final_g)
    for name, g in zip(s5_names, d_s5):
        small_g[name] = g
    rs_a = jnp.concatenate([d_w_o_hg.reshape(N_DEV, 128, D_MODEL), d_w_out.reshape(N_DEV, 128, D_MODEL),
                            d_w_pg.reshape(N_DEV, 128, D_MODEL)], axis=1).astype(BF16)
    rs_c = jnp.concatenate([_col_blocks(d_w_glu), _col_blocks(d_w_o_s5), _col_blocks(d_w_ple)], axis=1).astype(BF16)
    partial = _pack_small({n: small_g[n] for n in _SMALL}, tail=loss_acc[0, 0:1])
    g_b, g_a, g_c, sg = _grad_w_in_scatter(dproj, u_t, rs_a, rs_c, partial)
    out_g = _unpack_small(sg, {n: w[n] for n in _SMALL})
    out_d, out_m, out_v = {}, {}, {}
    for name in _SMALL:
        shape = w[name].shape
        two_d = (-1, shape[-1]) if len(shape) > 1 else (1, -1)
        d2, m2, v2 = _adam_rows(*[t.reshape(two_d) for t in (out_g[name], w[name], m[name], v[name])])
        out_d[name], out_m[name], out_v[name] = (t.reshape(shape) for t in (d2, m2, v2))
    big_g = dict(w_o_hg=g_a[0:128], w_out=g_a[128:256], w_ple_gate=g_a[256:384], w_in=g_b,
                 w_glu=g_c[0:512], w_o_s5=g_c[512:1024], w_ple=g_c[1024:1280])
    for name in _BIG:
        shape = w[name].shape
        g2 = big_g[name]
        d2, m2, v2 = _adam_rows(g2, w[name][0], m[name][0], v[name][0])
        out_g[name], out_d[name], out_m[name], out_v[name] = (t.reshape(shape) for t in (g2, d2, m2, v2))

    loss = sg[sg.shape[0] - 8, 0]
    return (loss, grad_x[None], *[out_g[n] for n in _ORDER], *[out_d[n] for n in _ORDER],
            *[out_m[n] for n in _ORDER], *[out_v[n] for n in _ORDER])
```

```python
import functools
import math

import jax
import jax.numpy as jnp
from jax import lax
from jax.experimental import pallas as pl
from jax.experimental.pallas import tpu as pltpu

F32 = jnp.float32
BF16 = jnp.bfloat16

D_MODEL = 1024
N_DEV = 8
IN_COLS = 7168
SHARD_IN = IN_COLS // N_DEV
HG_HEADS = 8
HG_DIM = 128
HG_CHUNK = 64
HG_SUPER_FWD = 256
HG_SUPER_BWD = 128
HG_HEADS_PER_STEP = 8
S5_WIDTH = 512
S5_GROUPS = 32
S5_STATE = 64
S5_CH = 16
S5_SEG = 8
S5_QUADS = 4
S5_QCOLS = 1024
S5_COLS = S5_QUADS * S5_QCOLS
S5_TILE_STEPS = 64
S5_UNROLL = 8
NORM_EPS = 1e-6
ADAM_LR = 0.001
ADAM_B1 = 0.9
ADAM_B2 = 0.999
ADAM_EPS = 1e-08
ADAM_WD = 0.01
ADAM_STEP = 10
MIB = 1024 * 1024
MESH = pl.DeviceIdType.MESH

COL_Q, COL_F, COL_I, COL_G, COL_US, COL_ZS, COL_GH, COL_GS = 0, 1024, 2048, 3072, 4096, 4608, 5120, 6144


def _call(body, **kw):
    return pl.pallas_call(body, **kw)


def _params(vmem_mb, n_grid=1):
    return pltpu.CompilerParams(
        dimension_semantics=("arbitrary",) * n_grid, vmem_limit_bytes=vmem_mb * MIB
    )


def _bdot(a, b):
    return jnp.dot(a.astype(BF16), b.astype(BF16), preferred_element_type=F32)


def _bdot_nt(a, b):
    return lax.dot_general(a.astype(BF16), b.astype(BF16), (((1,), (1,)), ((), ())), preferred_element_type=F32)


def _bdot_tn(a, b):
    return lax.dot_general(a.astype(BF16), b.astype(BF16), (((0,), (0,)), ((), ())), preferred_element_type=F32)


def _sigmoid(x):
    return jax.nn.sigmoid(x)


def _silu(x):
    return x * _sigmoid(x)


def _dsilu(x):
    s = _sigmoid(x)
    return s * (1.0 + x * (1.0 - s))


_GELU_C = math.sqrt(2.0 / math.pi)


def _gelu(x):
    return 0.5 * x * (1.0 + jnp.tanh(_GELU_C * (x + 0.044715 * x * x * x)))


def _dgelu(x):
    t = jnp.tanh(_GELU_C * (x + 0.044715 * x * x * x))
    return 0.5 * (1.0 + t) + 0.5 * x * (1.0 - t * t) * _GELU_C * (1.0 + 3.0 * 0.044715 * x * x)


def _rms_fwd(x, g):
    r = lax.rsqrt(jnp.mean(x * x, axis=-1, keepdims=True) + NORM_EPS)
    n = x * r
    return n * g, n, r


def _rms_bwd(dy, n, r, g):
    dn = dy * g
    dx = r * (dn - n * jnp.mean(dn * n, axis=-1, keepdims=True))
    return dx, jnp.sum(dy * n, axis=0, keepdims=True)


def _head_rms_fwd(o, g):
    ns, rs = [], []
    for h in range(HG_HEADS):
        oh = o[:, h * HG_DIM:(h + 1) * HG_DIM]
        r = lax.rsqrt(jnp.mean(oh * oh, axis=-1, keepdims=True) + NORM_EPS)
        ns.append(oh * r)
        rs.append(r)
    n = jnp.concatenate(ns, axis=1)
    return n * g, n, rs


def _head_rms_bwd(dy, n, rs, g):
    dn = dy * g
    dxs = []
    for h in range(HG_HEADS):
        sl = slice(h * HG_DIM, (h + 1) * HG_DIM)
        dxs.append(rs[h] * (dn[:, sl] - n[:, sl] * jnp.mean(dn[:, sl] * n[:, sl], axis=-1, keepdims=True)))
    return jnp.concatenate(dxs, axis=1), jnp.sum(dy * n, axis=0, keepdims=True)


def _rowwise(name, fn, n_rows, tm, rows, consts, out_rows, out_accs, vmem_mb, parts=1):
    n_r, n_c, n_or, n_oa = len(rows), len(consts), len(out_rows), len(out_accs)
    tp = tm // parts

    def body(*refs):
        r_refs = refs[:n_r]
        c_refs = refs[n_r:n_r + n_c]
        or_refs = refs[n_r + n_c:n_r + n_c + n_or]
        oa_refs = refs[n_r + n_c + n_or:]

        if n_oa:
            @pl.when(pl.program_id(0) == 0)
            def _():
                for ref in oa_refs:
                    ref[...] = jnp.zeros(ref.shape, ref.dtype)

        for part in range(parts):
            sl = slice(part * tp, (part + 1) * tp)
            outs, accs = fn([r[sl, :] for r in r_refs], c_refs, [o.at[sl, :] for o in or_refs])
            for ref, v in zip(or_refs, outs):
                if v is not None:
                    ref[sl, :] = v.astype(ref.dtype)
            for ref, v in zip(oa_refs, accs):
                ref[...] += v.astype(ref.dtype)

    in_specs = [pl.BlockSpec((tm, w), functools.partial(lambda i, c: (i, c), c=cb)) for (_, w, cb) in rows]
    in_specs += [pl.BlockSpec(c.shape, functools.partial(lambda i, nd: (0,) * nd, nd=c.ndim),
                              pipeline_mode=pl.Buffered(1)) for c in consts]
    out_specs = [pl.BlockSpec((tm, w), lambda i: (i, 0)) for (w, _) in out_rows]
    out_specs += [pl.BlockSpec(s, functools.partial(lambda i, nd: (0,) * nd, nd=len(s))) for (s, _) in out_accs]
    out_shape = [jax.ShapeDtypeStruct((n_rows, w), dt) for (w, dt) in out_rows]
    out_shape += [jax.ShapeDtypeStruct(s, dt) for (s, dt) in out_accs]
    res = _call(
        body, name=name, grid=(n_rows // tm,), in_specs=in_specs, out_specs=out_specs, out_shape=out_shape,
        compiler_params=_params(vmem_mb),
    )(*[a for (a, _, _) in rows], *consts)
    return res[:n_or], res[n_or:]


def _stage_branches(o, proj, ys, x, hg_norm_g, w_o_hg, w_glu, b_glu, w_o_s5, w_out):
    S = x.shape[0]

    def fn(rv, cr, out):
        o_b, g_hg, z_s, gate_hg, gate_s5, ys_b, x_b = rv
        gn_ref, wohg_ref, wglu_ref, bglu_ref, wos5_ref, wout_ref = cr
        on, _, _ = _head_rms_fwd(o_b, gn_ref[...])
        a = on * _silu(g_hg)
        y_hg = jnp.dot(a.astype(BF16), wohg_ref[...], preferred_element_type=F32)
        gl = _gelu(ys_b)
        glu = jnp.dot(gl.astype(BF16), wglu_ref[...], preferred_element_type=F32) + bglu_ref[...]
        ys2 = glu[:, :S5_WIDTH] * _sigmoid(glu[:, S5_WIDTH:]) * _silu(z_s)
        y_s5 = jnp.dot(ys2.astype(BF16), wos5_ref[...], preferred_element_type=F32)
        merged = _sigmoid(gate_hg) * y_hg + _sigmoid(gate_s5) * y_s5
        h1 = x_b + jnp.dot(merged.astype(BF16), wout_ref[...], preferred_element_type=F32)
        return [y_hg, y_s5, glu, h1], []

    rows = [(o, D_MODEL, 0), (proj, D_MODEL, COL_G // D_MODEL), (proj, S5_WIDTH, COL_ZS // S5_WIDTH),
            (proj, D_MODEL, COL_GH // D_MODEL), (proj, D_MODEL, COL_GS // D_MODEL), (ys, S5_WIDTH, 0), (x, D_MODEL, 0)]
    (y_hg, y_s5, glu, h1), _ = _rowwise(
        "branches", fn, S, 256, rows, [hg_norm_g, w_o_hg, w_glu, b_glu, w_o_s5, w_out],
        [(D_MODEL, BF16), (D_MODEL, BF16), (D_MODEL, F32), (D_MODEL, F32)], [], 56)
    return y_hg, y_s5, glu, h1


def _stage_ple_loss(h1, p, target, ple_norm_g, w_ple, w_ple_gate, final_norm_g):
    S = h1.shape[0]

    def fn(rv, cr, out):
        h1_b, p_b, t_b = rv
        gp_ref, wple_ref, wpg_ref, gf_ref = cr
        n2g, n2, r2 = _rms_fwd(h1_b, gp_ref[...])
        z = jnp.dot(n2g.astype(BF16), wpg_ref[...], preferred_element_type=F32)
        gate = _sigmoid(z)
        pe = jnp.dot(p_b.astype(BF16), wple_ref[...], preferred_element_type=F32)
        h2 = h1_b + pe * gate
        y, nf, rf = _rms_fwd(h2, gf_ref[...])
        err = y - t_b
        loss_rows = 0.5 * jnp.mean(err * err, axis=-1, keepdims=True)
        loss_inc = jnp.broadcast_to(jnp.sum(loss_rows, axis=0, keepdims=True), (1, 128))
        dy = err * (1.0 / D_MODEL)
        dh2, d_gf = _rms_bwd(dy, nf, rf, gf_ref[...])
        d_pe = dh2 * gate
        dz = dh2 * pe * gate * (1.0 - gate)
        d_wple = _bdot_tn(p_b, d_pe)
        d_wpg = _bdot_tn(n2g, dz)
        dn2g = _bdot_nt(dz, wpg_ref[...])
        dh1n, d_gp = _rms_bwd(dn2g, n2, r2, gp_ref[...])
        return [dh2 + dh1n], [loss_inc, d_gf, d_gp, d_wple, d_wpg]

    (dh1,), accs = _rowwise(
        "ple_loss", fn, S, 512, [(h1, D_MODEL, 0), (p, 256, 0), (target, D_MODEL, 0)],
        [ple_norm_g, w_ple, w_ple_gate, final_norm_g], [(D_MODEL, F32)],
        [((1, 128), F32), ((1, D_MODEL), F32), ((1, D_MODEL), F32), ((256, D_MODEL), F32), ((D_MODEL, D_MODEL), F32)], 56,
        parts=2)
    return dh1, accs


def _stage_bwd_merge(dh1, y_hg, y_s5, proj, w_out):
    S = dh1.shape[0]

    def fn(rv, cr, out):
        dh1_b, yhg, ys5, gate_hg, gate_s5 = rv
        (wout_ref,) = cr
        sg_h, sg_s = _sigmoid(gate_hg), _sigmoid(gate_s5)
        merged = sg_h * yhg + sg_s * ys5
        d_wout = _bdot_tn(merged, dh1_b)
        d_merged = _bdot_nt(dh1_b, wout_ref[...])
        d_gate_hg = d_merged * yhg * sg_h * (1.0 - sg_h)
        d_gate_s5 = d_merged * ys5 * sg_s * (1.0 - sg_s)
        return [d_gate_hg, d_gate_s5, d_merged * sg_h, d_merged * sg_s], [d_wout]

    rows = [(dh1, D_MODEL, 0), (y_hg, D_MODEL, 0), (y_s5, D_MODEL, 0), (proj, D_MODEL, COL_GH // D_MODEL),
            (proj, D_MODEL, COL_GS // D_MODEL)]
    outs, accs = _rowwise("bwd_merge", fn, S, 512, rows, [w_out], [(D_MODEL, BF16)] * 4,
                          [((D_MODEL, D_MODEL), F32)], 56, parts=2)
    return outs, accs


def _stage_bwd_hg_path(d_yhg, o, proj, hg_norm_g, w_o_hg):
    S = o.shape[0]

    def fn(rv, cr, out):
        d_yhg_b, o_b, g_hg = rv
        gn_ref, wohg_ref = cr
        ong, on, rs = _head_rms_fwd(o_b, gn_ref[...])
        sil = _silu(g_hg)
        d_wohg = _bdot_tn(ong * sil, d_yhg_b)
        d_a = _bdot_nt(d_yhg_b, wohg_ref[...])
        d_g_hg = d_a * ong * _dsilu(g_hg)
        d_o, d_gn = _head_rms_bwd(d_a * sil, on, rs, gn_ref[...])
        return [d_o, d_g_hg], [d_wohg, d_gn]

    rows = [(d_yhg, D_MODEL, 0), (o, D_MODEL, 0), (proj, D_MODEL, COL_G // D_MODEL)]
    outs, accs = _rowwise("bwd_hg_path", fn, S, 512, rows, [hg_norm_g, w_o_hg], [(D_MODEL, BF16)] * 2,
                          [((D_MODEL, D_MODEL), F32), ((1, D_MODEL), F32)], 56, parts=2)
    return outs, accs


def _stage_bwd_s5_path(d_ys5, ys, glu, proj, w_o_s5, w_glu):
    S = ys.shape[0]

    def fn(rv, cr, out):
        d_ys5_b, ys_b, glu_b, z_s = rv
        wos5_ref, wglu_ref = cr
        ga, gb = glu_b[:, :S5_WIDTH], glu_b[:, S5_WIDTH:]
        sgb, silz = _sigmoid(gb), _silu(z_s)
        ys2 = ga * sgb * silz
        d_wos5 = _bdot_tn(ys2, d_ys5_b)
        d_ys2 = _bdot_nt(d_ys5_b, wos5_ref[...])
        d_ga = d_ys2 * sgb * silz
        d_gb = d_ys2 * ga * sgb * (1.0 - sgb) * silz
        d_z = d_ys2 * ga * sgb * _dsilu(z_s)
        d_glu = jnp.concatenate([d_ga, d_gb], axis=1)
        gl = _gelu(ys_b)
        d_wglu = _bdot_tn(gl, d_glu)
        d_bglu = jnp.sum(d_glu, axis=0, keepdims=True)
        d_gl = _bdot_nt(d_glu, wglu_ref[...])
        return [d_gl * _dgelu(ys_b), d_z], [d_wos5, d_wglu, d_bglu]

    rows = [(d_ys5, D_MODEL, 0), (ys, S5_WIDTH, 0), (glu, D_MODEL, 0), (proj, S5_WIDTH, COL_ZS // S5_WIDTH)]
    outs, accs = _rowwise(
        "bwd_s5_path", fn, S, 512, rows, [w_o_s5, w_glu], [(S5_WIDTH, F32), (S5_WIDTH, BF16)],
        [((S5_WIDTH, D_MODEL), F32), ((S5_WIDTH, D_MODEL), F32), ((1, D_MODEL), F32)], 48, parts=2)
    return outs, accs


def _stage_inproj_bwd(pieces, x, dh1, norm_g, w_in_all):
    S = x.shape[0]

    def fn(rv, cr, out):
        g_ref, w_ref = cr
        x_b, dh1_b = rv[8], rv[9]
        dproj_ref = out[1]
        col = 0
        for v in rv[:8]:
            dproj_ref[:, col:col + v.shape[1]] = v.astype(BF16)
            col += v.shape[1]
        d_u = jnp.zeros((x_b.shape[0], D_MODEL), F32)
        for j in range(N_DEV):
            d_u = d_u + lax.dot_general(dproj_ref[:, j * SHARD_IN:(j + 1) * SHARD_IN], w_ref[j],
                                        (((1,), (1,)), ((), ())), preferred_element_type=F32)
        _, n, r = _rms_fwd(x_b, g_ref[...])
        dx, d_g = _rms_bwd(d_u, n, r, g_ref[...])
        return [dh1_b + dx, None], [d_g]

    rows = [(a, a.shape[1], 0) for a in pieces] + [(x, D_MODEL, 0), (dh1, D_MODEL, 0)]
    (grad_x, dproj), (d_g,) = _rowwise(
        "inproj_bwd", fn, S, 256, rows, [norm_g, w_in_all], [(D_MODEL, F32), (IN_COLS, BF16)],
        [((1, D_MODEL), F32)], 56)
    return grad_x, dproj, d_g


def _chunk_row(shape):
    return lax.broadcasted_iota(jnp.int32, shape, 0) & (HG_CHUNK - 1)


def _chunk_cumsum(x):
    r_in = _chunk_row(x.shape)
    s = 1
    while s < HG_CHUNK:
        x = x + jnp.where(r_in >= s, pltpu.roll(x, s, 0), 0.0)
        s *= 2
    return x


def _chunk_suffix_sum(x):
    n = x.shape[0]
    r_in = _chunk_row(x.shape)
    s = 1
    while s < HG_CHUNK:
        x = x + jnp.where(r_in < HG_CHUNK - s, pltpu.roll(x, n - s, 0), 0.0)
        s *= 2
    return x


def _hgrn_prep(q, fl, lb):
    sup = q.shape[0]
    nc = sup // HG_CHUNK
    sig = _sigmoid(fl)
    f = lb + (1.0 - lb) * sig
    k = (1.0 - lb) * (1.0 - sig)
    b = _chunk_cumsum(jnp.log(f))
    b3 = b.reshape(nc, HG_CHUNK, HG_DIM)
    row3 = lax.broadcasted_iota(jnp.int32, b3.shape, 1)
    pick = lambda r: jnp.sum(jnp.where(row3 == r, b3, 0.0), axis=1, keepdims=True)
    b_mid = pick(HG_CHUNK // 2 - 1)
    b_last = pick(HG_CHUNK - 1)
    flat = lambda t: t.reshape(sup, HG_DIM)
    e_qa = flat(jnp.exp(b3 - b_mid))
    e_ka = flat(jnp.exp(b_mid - b3))
    e_qd = jnp.exp(b)
    e_kd = flat(jnp.exp(b_last - b3))
    dc = jnp.exp(b_last)
    return sig, f, k, e_qa, e_ka, e_qd, e_kd, dc


def _hgrn_mask(sup):
    r = lax.broadcasted_iota(jnp.int32, (sup, sup), 0)
    c = lax.broadcasted_iota(jnp.int32, (sup, sup), 1)
    shift = HG_CHUNK.bit_length() - 1
    return (jnp.right_shift(r, shift) == jnp.right_shift(c, shift)) & (r >= c)


def _hgrn_fwd(proj, lb):
    S = proj.shape[0]
    sup = HG_SUPER_FWD
    nb = S // sup
    nc = sup // HG_CHUNK
    hp = HG_HEADS_PER_STEP
    wide = hp * HG_DIM

    def body(q_ref, f_ref, iv_ref, lb_ref, o_ref, st_ref, state):
        @pl.when(pl.program_id(1) == 0)
        def _():
            state[...] = jnp.zeros(state.shape, F32)

        mask = _hgrn_mask(sup)
        for hh in range(hp):
            lanes = slice(hh * HG_DIM, (hh + 1) * HG_DIM)
            q, iv = q_ref[:, lanes], iv_ref[:, lanes]
            _, _, k, e_qa, e_ka, e_qd, e_kd, dc = _hgrn_prep(q, f_ref[:, lanes], lb_ref[:, lanes])
            scores = jnp.where(mask, _bdot_nt(q * e_qa, k * e_ka), 0.0)
            o_intra = _bdot(scores, iv)
            qd, kd = q * e_qd, k * e_kd
            for c in range(nc):
                sl = slice(c * HG_CHUNK, (c + 1) * HG_CHUNK)
                st = state[hh]
                st_ref[hh, c] = st
                o_ref[sl, lanes] = o_intra[sl] + _bdot_nt(qd[sl], st)
                state[hh] = dc[c] * st + _bdot_tn(iv[sl], kd[sl])

    blk = lambda base: pl.BlockSpec((sup, wide), functools.partial(lambda h, i, b: (i, b + h), b=base // wide))
    return _call(
        body, name="hgrn_fwd", grid=(HG_HEADS // hp, nb),
        in_specs=[blk(COL_Q), blk(COL_F), blk(COL_I), pl.BlockSpec((1, wide), lambda h, i: (0, h))],
        out_specs=[pl.BlockSpec((sup, wide), lambda h, i: (i, h)),
                   pl.BlockSpec((hp, nc, HG_DIM, HG_DIM), lambda h, i: (h, i, 0, 0))],
        out_shape=[jax.ShapeDtypeStruct((S, D_MODEL), F32),
                   jax.ShapeDtypeStruct((HG_HEADS, S // HG_CHUNK, HG_DIM, HG_DIM), F32)],
        scratch_shapes=[pltpu.VMEM((hp, HG_DIM, HG_DIM), F32)],
        compiler_params=_params(40, 2),
    )(proj, proj, proj, lb)


def _hgrn_bwd(proj, lb, d_o, states):
    S = proj.shape[0]
    sup = HG_SUPER_BWD
    nb = S // sup
    nc = sup // HG_CHUNK
    hp = HG_HEADS_PER_STEP
    wide = hp * HG_DIM

    def body(q_ref, f_ref, iv_ref, lb_ref, do_ref, st_ref, dq_ref, df_ref, div_ref, dlb_ref, dstate):
        @pl.when(pl.program_id(1) == 0)
        def _():
            dstate[...] = jnp.zeros(dstate.shape, F32)
            dlb_ref[...] = jnp.zeros(dlb_ref.shape, F32)

        mask = _hgrn_mask(sup)
        for hh in range(hp):
            lanes = slice(hh * HG_DIM, (hh + 1) * HG_DIM)
            q, iv, do, lb_v = q_ref[:, lanes], iv_ref[:, lanes], do_ref[:, lanes], lb_ref[:, lanes]
            sig, f, k, e_qa, e_ka, e_qd, e_kd, dc = _hgrn_prep(q, f_ref[:, lanes], lb_v)
            qa, ka, qd, kd = q * e_qa, k * e_ka, q * e_qd, k * e_kd
            scores = jnp.where(mask, _bdot_nt(qa, ka), 0.0)
            d_scores = jnp.where(mask, _bdot_nt(do, iv), 0.0)
            d_iv_intra = _bdot_tn(scores, do)
            d_qa = _bdot(d_scores, ka)
            d_ka = _bdot_tn(d_scores, qa)
            d_qd, d_kd, d_last = [None] * nc, [None] * nc, [None] * nc
            for c in reversed(range(nc)):
                sl = slice(c * HG_CHUNK, (c + 1) * HG_CHUNK)
                st = st_ref[hh, c]
                ds = dstate[hh]
                d_qd[c] = _bdot(do[sl], st)
                d_kd[c] = _bdot(iv[sl], ds)
                div_ref[sl, lanes] = (d_iv_intra[sl] + _bdot_nt(kd[sl], ds)).astype(div_ref.dtype)
                d_last[c] = (jnp.sum(ds * st, axis=0, keepdims=True) * dc[c]
                             + jnp.sum(d_kd[c] * kd[sl], axis=0, keepdims=True))
                dstate[hh] = dc[c] * ds + _bdot_tn(do[sl], qd[sl])
            d_qd = jnp.concatenate(d_qd, axis=0)
            d_kd = jnp.concatenate(d_kd, axis=0)
            d_b = d_qa * qa - d_ka * ka + d_qd * qd - d_kd * kd
            last_rows = jnp.concatenate([jnp.broadcast_to(t, (HG_CHUNK, HG_DIM)) for t in d_last], axis=0)
            d_b = d_b + jnp.where(_chunk_row(d_b.shape) == HG_CHUNK - 1, last_rows, 0.0)
            d_logf = _chunk_suffix_sum(d_b)
            d_k = d_ka * e_ka + d_kd * e_kd
            g_f = d_logf / f
            d_sig = (g_f - d_k) * (1.0 - lb_v)
            dq_ref[:, lanes] = (d_qa * e_qa + d_qd * e_qd).astype(dq_ref.dtype)
            df_ref[:, lanes] = (d_sig * sig * (1.0 - sig)).astype(df_ref.dtype)
            d_lb = jnp.sum((g_f - d_k) * (1.0 - sig), axis=0, keepdims=True)
            dlb_ref[:, lanes] += jnp.broadcast_to(d_lb, (8, HG_DIM))

    rev = lambda i: nb - 1 - i
    blk = lambda base: pl.BlockSpec((sup, wide), functools.partial(lambda h, i, b: (rev(i), b + h), b=base // wide))
    row_out = pl.BlockSpec((sup, wide), lambda h, i: (rev(i), h))
    dq, df, div, dlb = _call(
        body, name="hgrn_bwd", grid=(HG_HEADS // hp, nb),
        in_specs=[blk(COL_Q), blk(COL_F), blk(COL_I), pl.BlockSpec((1, wide), lambda h, i: (0, h)),
                  pl.BlockSpec((sup, wide), lambda h, i: (rev(i), h)),
                  pl.BlockSpec((hp, nc, HG_DIM, HG_DIM), lambda h, i: (h, rev(i), 0, 0))],
        out_specs=[row_out, row_out, row_out, pl.BlockSpec((8, wide), lambda h, i: (0, h))],
        out_shape=[jax.ShapeDtypeStruct((S, D_MODEL), BF16)] * 3 + [jax.ShapeDtypeStruct((8, D_MODEL), F32)],
        scratch_shapes=[pltpu.VMEM((hp, HG_DIM, HG_DIM), F32)],
        compiler_params=_params(40, 2),
    )(proj, proj, proj, lb, d_o, states)
    return dq, df, div, dlb[0:1]


def _s5_matrices(a_re, a_im, log_dt, b_re, b_im, c_re, c_im, d, seg_len):
    dt = jnp.exp(log_dt)[:, None]
    mag = jnp.exp(a_re * dt)
    lr, li = mag * jnp.cos(a_im * dt), mag * jnp.sin(a_im * dt)
    den = a_re * a_re + a_im * a_im
    nr = lr - 1.0
    sr = (nr * a_re + li * a_im) / den
    si = (li * a_re - nr * a_im) / den
    bbr = sr[..., None] * b_re - si[..., None] * b_im
    bbi = sr[..., None] * b_im + si[..., None] * b_re
    eye = jnp.eye(8, dtype=F32)

    def quad_cols(v):
        return v.reshape(S5_QUADS, 8 * S5_STATE)

    def lam_row(re_part, im_part):
        row = jnp.concatenate([quad_cols(re_part), quad_cols(im_part)], axis=1).reshape(1, S5_COLS)
        return jnp.broadcast_to(row, (S5_SEG, S5_COLS))

    def b_mat(bb):
        t = bb.reshape(S5_QUADS, 8, S5_STATE, S5_CH)
        return jnp.einsum("qgnc,gh->qgchn", t, eye).reshape(S5_QUADS, 8 * S5_CH, 8 * S5_STATE)

    def c_mat(cc):
        t = cc.reshape(S5_QUADS, 8, S5_CH, S5_STATE)
        return jnp.einsum("qgcn,gh->qgnhc", t, eye).reshape(S5_QUADS, 8 * S5_STATE, 8 * S5_CH)

    ang = a_im * dt * seg_len
    magp = jnp.exp(a_re * dt * seg_len)
    lpr, lpi = magp * jnp.cos(ang), magp * jnp.sin(ang)
    return dict(
        lam_r=lam_row(lr, lr), lam_i=lam_row(-li, li),
        b_q=jnp.concatenate([b_mat(bbr), b_mat(bbi)], axis=2),
        c_q=jnp.concatenate([c_mat(c_re), -c_mat(c_im)], axis=1),
        d_row=d.reshape(1, S5_WIDTH), pow_r=quad_cols(lpr), pow_i=quad_cols(lpi),
    )


def _s5_parts(v):
    half = S5_QCOLS // 2
    return tuple(v[:, k * half:(k + 1) * half] for k in range(2 * S5_QUADS))


def _s5_advance(parts, lr_ref, li_ref, x_ref, sl, conj):
    half = S5_QCOLS // 2
    out = []
    for q in range(S5_QUADS):
        re_c = slice(q * S5_QCOLS, q * S5_QCOLS + half)
        im_c = slice(q * S5_QCOLS + half, (q + 1) * S5_QCOLS)
        lr, li = lr_ref[:, re_c], li_ref[:, im_c]
        hr, hi = parts[2 * q], parts[2 * q + 1]
        if conj:
            out += [lr * hr + li * hi + x_ref[sl, re_c], lr * hi - li * hr + x_ref[sl, im_c]]
        else:
            out += [lr * hr - li * hi + x_ref[sl, re_c], lr * hi + li * hr + x_ref[sl, im_c]]
    return tuple(out)


def _scan_loop(step, init):
    def trip(o, carry):
        for j in range(S5_UNROLL):
            carry = step(o * S5_UNROLL + j, carry)
        return carry

    return lax.fori_loop(0, S5_TILE_STEPS // S5_UNROLL, trip, init)


def _s5_store(ref, sl, parts):
    half = S5_QCOLS // 2
    for k, v in enumerate(parts):
        ref[sl, k * half:(k + 1) * half] = v


def _s5_fwd_pass(u_perm, mats, h0, with_output):
    S = u_perm.shape[0]
    rows = S5_TILE_STEPS * S5_SEG
    nt = S // rows

    def body(*refs):
        if with_output:
            u_ref, b_ref, lr_ref, li_ref, h0_ref, c_ref, d_ref, y_ref, hinit_ref, hend_ref, xs, hcar = refs
        else:
            u_ref, b_ref, lr_ref, li_ref, h0_ref, hend_ref, xs, hcar = refs

        @pl.when(pl.program_id(0) == 0)
        def _():
            hcar[...] = h0_ref[...]

        if with_output:
            hinit_ref[...] = hcar[...]
        u = u_ref[...]
        ub = u.astype(BF16)
        for q in range(S5_QUADS):
            xs[:, q * S5_QCOLS:(q + 1) * S5_QCOLS] = jnp.dot(ub[:, q * 128:(q + 1) * 128], b_ref[q], preferred_element_type=F32)

        def step(t, h):
            sl = pl.ds(pl.multiple_of(t * S5_SEG, S5_SEG), S5_SEG)
            hn = _s5_advance(h, lr_ref, li_ref, xs, sl, False)
            _s5_store(xs, sl, hn)
            return hn

        h = _scan_loop(step, _s5_parts(hcar[...]))
        _s5_store(hcar, slice(None), h)
        _s5_store(hend_ref, slice(None), h)
        if with_output:
            ys = [jnp.dot(xs[:, q * S5_QCOLS:(q + 1) * S5_QCOLS].astype(BF16), c_ref[q], preferred_element_type=F32)
                  for q in range(S5_QUADS)]
            y_ref[...] = jnp.concatenate(ys, axis=1) + d_ref[...] * u

    full = lambda a: pl.BlockSpec(a.shape, functools.partial(lambda i, nd: (0,) * nd, nd=a.ndim))
    ins = [u_perm, mats["b_q"], mats["lam_r"], mats["lam_i"], h0]
    in_specs = [pl.BlockSpec((rows, S5_WIDTH), lambda i: (i, 0))] + [full(a) for a in ins[1:]]
    out_specs = [pl.BlockSpec((S5_SEG, S5_COLS), lambda i: (0, 0))]
    out_shape = [jax.ShapeDtypeStruct((S5_SEG, S5_COLS), F32)]
    if with_output:
        ins += [mats["c_q"], mats["d_row"]]
        in_specs += [full(mats["c_q"]), full(mats["d_row"])]
        out_specs = [pl.BlockSpec((rows, S5_WIDTH), lambda i: (i, 0)),
                     pl.BlockSpec((None, S5_SEG, S5_COLS), lambda i: (i, 0, 0))] + out_specs
        out_shape = [jax.ShapeDtypeStruct((S, S5_WIDTH), F32), jax.ShapeDtypeStruct((nt, S5_SEG, S5_COLS), F32)] + out_shape
    return _call(
        body, name="s5_fwd_y" if with_output else "s5_fwd_ends", grid=(nt,), in_specs=in_specs, out_specs=out_specs,
        out_shape=out_shape,
        scratch_shapes=[pltpu.VMEM((rows, S5_COLS), F32), pltpu.VMEM((S5_SEG, S5_COLS), F32)],
        compiler_params=_params(40),
    )(*ins)


def _s5_bwd_ends(dy_perm, mats):
    S = dy_perm.shape[0]
    rows = S5_TILE_STEPS * S5_SEG
    nt = S // rows

    def body(dy_ref, c_ref, lr_ref, li_ref, gend_ref, gs, gcar):
        @pl.when(pl.program_id(0) == 0)
        def _():
            gcar[...] = jnp.zeros(gcar.shape, F32)

        dyb = dy_ref[...].astype(BF16)
        for q in range(S5_QUADS):
            gs[:, q * S5_QCOLS:(q + 1) * S5_QCOLS] = lax.dot_general(
                dyb[:, q * 128:(q + 1) * 128], c_ref[q], (((1,), (1,)), ((), ())), preferred_element_type=F32)

        def step(k, g):
            t = S5_TILE_STEPS - 1 - k
            sl = pl.ds(pl.multiple_of(t * S5_SEG, S5_SEG), S5_SEG)
            return _s5_advance(g, lr_ref, li_ref, gs, sl, True)

        g = _scan_loop(step, _s5_parts(gcar[...]))
        _s5_store(gcar, slice(None), g)
        _s5_store(gend_ref, slice(None), g)

    full = lambda a: pl.BlockSpec(a.shape, functools.partial(lambda i, nd: (0,) * nd, nd=a.ndim))
    return _call(
        body, name="s5_bwd_ends", grid=(nt,),
        in_specs=[pl.BlockSpec((rows, S5_WIDTH), lambda i: (nt - 1 - i, 0)), full(mats["c_q"]), full(mats["lam_r"]),
                  full(mats["lam_i"])],
        out_specs=pl.BlockSpec((S5_SEG, S5_COLS), lambda i: (0, 0)),
        out_shape=jax.ShapeDtypeStruct((S5_SEG, S5_COLS), F32),
        scratch_shapes=[pltpu.VMEM((rows, S5_COLS), F32), pltpu.VMEM((S5_SEG, S5_COLS), F32)],
        compiler_params=_params(40),
    )(dy_perm, mats["c_q"], mats["lam_r"], mats["lam_i"])


def _s5_bwd_full(u_perm, dy_perm, hinit, g0, mats):
    S = u_perm.shape[0]
    rows = S5_TILE_STEPS * S5_SEG
    nt = S // rows

    def body(u_ref, dy_ref, hinit_ref, g0_ref, b_ref, c_ref, lr_ref, li_ref, d_ref,
             du_ref, dp_ref, dq_ref, db_ref, dc_ref, dd_ref, hs, gs, gcar):
        @pl.when(pl.program_id(0) == 0)
        def _():
            gcar[...] = g0_ref[...]
            for ref in (dp_ref, dq_ref, db_ref, dc_ref, dd_ref):
                ref[...] = jnp.zeros(ref.shape, F32)

        u, dy = u_ref[...], dy_ref[...]
        ub, dyb = u.astype(BF16), dy.astype(BF16)
        hs[0:S5_SEG, :] = hinit_ref[...]
        for q in range(S5_QUADS):
            cols = slice(q * S5_QCOLS, (q + 1) * S5_QCOLS)
            hs[S5_SEG:, cols] = jnp.dot(ub[:, q * 128:(q + 1) * 128], b_ref[q], preferred_element_type=F32)
            gs[:, cols] = lax.dot_general(dyb[:, q * 128:(q + 1) * 128], c_ref[q], (((1,), (1,)), ((), ())),
                                          preferred_element_type=F32)

        def fstep(t, h):
            sl = pl.ds(pl.multiple_of((t + 1) * S5_SEG, S5_SEG), S5_SEG)
            hn = _s5_advance(h, lr_ref, li_ref, hs, sl, False)
            _s5_store(hs, sl, hn)
            return hn

        _scan_loop(fstep, _s5_parts(hinit_ref[...]))

        def bstep(k, g):
            t = S5_TILE_STEPS - 1 - k
            sl = pl.ds(pl.multiple_of(t * S5_SEG, S5_SEG), S5_SEG)
            gn = _s5_advance(g, lr_ref, li_ref, gs, sl, True)
            _s5_store(gs, sl, gn)
            return gn

        _s5_store(gcar, slice(None), _scan_loop(bstep, _s5_parts(gcar[...])))

        half = S5_QCOLS // 2
        dus = []
        for q in range(S5_QUADS):
            cols = slice(q * S5_QCOLS, (q + 1) * S5_QCOLS)

            def astep(t, carry, q=q):
                sl = pl.ds(pl.multiple_of(t * S5_SEG, S5_SEG), S5_SEG)
                g = gs[sl, q * S5_QCOLS:(q + 1) * S5_QCOLS]
                hp = hs[sl, q * S5_QCOLS:(q + 1) * S5_QCOLS]
                hp_sw = jnp.concatenate([hp[:, half:], hp[:, :half]], axis=1)
                return carry[0] + g * hp, carry[1] + g * hp_sw

            zero = jnp.zeros((S5_SEG, S5_QCOLS), F32)
            acc_p, acc_q = _scan_loop(astep, (zero, zero))
            dp_ref[:, cols] += jnp.sum(acc_p, axis=0, keepdims=True)
            dq_ref[:, cols] += jnp.sum(acc_q, axis=0, keepdims=True)
            gq = gs[:, cols].astype(BF16)
            db_ref[q] += lax.dot_general(ub[:, q * 128:(q + 1) * 128], gq, (((0,), (0,)), ((), ())),
                                         preferred_element_type=F32)
            hq = hs[S5_SEG:, cols].astype(BF16)
            dc_ref[q] += lax.dot_general(dyb[:, q * 128:(q + 1) * 128], hq, (((0,), (0,)), ((), ())),
                                         preferred_element_type=F32)
            dus.append(lax.dot_general(gq, b_ref[q], (((1,), (1,)), ((), ())), preferred_element_type=F32))
        du_ref[...] = (jnp.concatenate(dus, axis=1) + d_ref[...] * dy).astype(du_ref.dtype)
        dd_ref[...] += jnp.sum(dy * u, axis=0, keepdims=True)

    full = lambda a: pl.BlockSpec(a.shape, functools.partial(lambda i, nd: (0,) * nd, nd=a.ndim))
    rev_rows = pl.BlockSpec((rows, S5_WIDTH), lambda i: (nt - 1 - i, 0))
    consts = [mats["b_q"], mats["c_q"], mats["lam_r"], mats["lam_i"], mats["d_row"]]
    acc = lambda s: pl.BlockSpec(s, functools.partial(lambda i, nd: (0,) * nd, nd=len(s)))
    acc_shapes = [(1, S5_COLS), (1, S5_COLS), (S5_QUADS, 128, S5_QCOLS), (S5_QUADS, 128, S5_QCOLS), (1, S5_WIDTH)]
    return _call(
        body, name="s5_bwd_full", grid=(nt,),
        in_specs=[rev_rows, rev_rows, pl.BlockSpec((None, S5_SEG, S5_COLS), lambda i: (nt - 1 - i, 0, 0)), full(g0)]
        + [full(a) for a in consts],
        out_specs=[rev_rows] + [acc(s) for s in acc_shapes],
        out_shape=[jax.ShapeDtypeStruct((S, S5_WIDTH), BF16)] + [jax.ShapeDtypeStruct(s, F32) for s in acc_shapes],
        scratch_shapes=[pltpu.VMEM((rows + S5_SEG, S5_COLS), F32), pltpu.VMEM((rows, S5_COLS), F32),
                        pltpu.VMEM((S5_SEG, S5_COLS), F32)],
        compiler_params=_params(56),
    )(u_perm, dy_perm, hinit, g0, *consts)


def _cmul(ar, ai, br, bi):
    return ar * br - ai * bi, ar * bi + ai * br


def _split_cols(v):
    t = v.reshape(v.shape[0], S5_QUADS, 2, S5_QCOLS // 2)
    return t[:, :, 0], t[:, :, 1]


def _join_cols(re, im):
    return jnp.stack([re, im], axis=2).reshape(re.shape[0], S5_COLS)


def _segment_starts(ends, pow_r, pow_i, reverse):
    er, ei = _split_cols(ends)
    pi = -pow_i if reverse else pow_i
    order = list(range(S5_SEG))
    if reverse:
        order = order[::-1]
    cr, ci = jnp.zeros_like(er[0]), jnp.zeros_like(ei[0])
    out_r, out_i = [None] * S5_SEG, [None] * S5_SEG
    for j in order:
        out_r[j], out_i[j] = cr, ci
        mr, mi = _cmul(pow_r, pi, cr, ci)
        cr, ci = mr + er[j], mi + ei[j]
    return _join_cols(jnp.stack(out_r), jnp.stack(out_i))


def _to_segments(a):
    S, w = a.shape
    return a.reshape(S5_SEG, S // S5_SEG, w).transpose(1, 0, 2).reshape(S, w)


def _from_segments(a):
    S, w = a.shape
    return a.reshape(S // S5_SEG, S5_SEG, w).transpose(1, 0, 2).reshape(S, w)


def _my_pos():
    return lax.axis_index("x"), lax.axis_index("y"), lax.axis_index("c")


def _flip(pos, k):
    x, y, c = pos
    return (1 - x if k & 4 else x, 1 - y if k & 2 else y, 1 - c if k & 1 else c)


def _index_of(pos):
    return 4 * pos[0] + 2 * pos[1] + pos[2]


_GATHER_FLIPS = (0, 1, 4, 5, 2, 3, 6, 7)


def _inproj_gather(x, norm_g, pack_a, pack_b, pack_c):
    S = x.shape[0]
    tm = min(S, 1024)
    n_i = S // tm
    order = jnp.stack([_index_of(_flip(_my_pos(), k)) for k in _GATHER_FLIPS]).astype(jnp.int32)

    def body(order_ref, x_ref, g_ref, pa_ref, pb_ref, pc_ref, ut_ref, proj_ref, oa_ref, ob_ref, oc_ref,
             wv, u_scr, send_sems, recv_sems, local_sems):
        s, i = pl.program_id(0), pl.program_id(1)
        me = _my_pos()
        mine = _index_of(me)
        sibling = _flip(me, 1)
        srcs = (pb_ref, pa_ref, pc_ref)
        dsts = (wv, oa_ref, oc_ref)

        def direct(a, k):
            return pltpu.make_async_remote_copy(
                src_ref=srcs[a], dst_ref=dsts[a].at[mine], send_sem=send_sems.at[a * 8 + k],
                recv_sem=recv_sems.at[a * 8 + k], device_id=_flip(me, k), device_id_type=MESH)

        def passed_on(a, k):
            slot = _index_of(_flip(me, k))
            return pltpu.make_async_remote_copy(
                src_ref=dsts[a].at[slot], dst_ref=dsts[a].at[slot], send_sem=send_sems.at[a * 8 + (k | 1)],
                recv_sem=recv_sems.at[a * 8 + (k | 1)], device_id=sibling, device_id_type=MESH)

        def arrival(a, k):
            slot = _index_of(_flip(me, k))
            pltpu.make_async_remote_copy(
                src_ref=dsts[a].at[slot], dst_ref=dsts[a].at[slot], send_sem=send_sems.at[a * 8 + k],
                recv_sem=recv_sems.at[a * 8 + k], device_id=me, device_id_type=MESH).wait_recv()

        def own_copy(a):
            return pltpu.make_async_copy(srcs[a], dsts[a].at[mine], local_sems.at[a])

        def keep(idx):
            slot = _index_of(_flip(me, _GATHER_FLIPS[idx]))
            return pltpu.make_async_copy(wv.at[slot], ob_ref.at[slot], local_sems.at[3 + idx])


        first = (s == 0) & (i == 0)

        @pl.when(first)
        def _():
            for a in range(3):
                own_copy(a).start()
            for k in (1, 4, 2):
                direct(0, k).start()
            own_copy(0).wait()
            keep(0).start()

        for idx, k in enumerate(_GATHER_FLIPS):
            if idx == 0:
                continue

            @pl.when((s == idx) & (i == 0))
            def _(idx=idx, k=k):
                arrival(0, k)
                if k in (4, 2, 6):
                    passed_on(0, k).start()
                keep(idx).start()
                if idx == 1:
                    direct(0, 6).start()
                if idx == 2:
                    for a in (1, 2):
                        for k in (1, 4, 2, 6):
                            direct(a, k).start()

        @pl.when(s == 0)
        def _():
            y, _, _ = _rms_fwd(x_ref[...], g_ref[...])
            u_scr[pl.ds(pl.multiple_of(i * tm, tm), tm), :] = y.astype(BF16)
            ut_ref[...] = y.T.astype(BF16)

        ub = u_scr[pl.ds(pl.multiple_of(i * tm, tm), tm), :]
        proj_ref[...] = jnp.dot(ub, wv[order_ref[s]], preferred_element_type=F32)

        @pl.when((s == N_DEV - 1) & (i == n_i - 1))
        def _():
            for a in (1, 2):
                for k in (4, 2, 6):
                    arrival(a, k)
                    passed_on(a, k).start()
            for a in (1, 2):
                for k in (1, 5, 3, 7):
                    arrival(a, k)
                own_copy(a).wait()
            for a in range(3):
                for k in (1, 4, 2, 6):
                    direct(a, k).wait_send()
                for k in (4, 2, 6):
                    passed_on(a, k).wait_send()
            for idx in range(N_DEV):
                keep(idx).wait()

    any_spec = pl.BlockSpec(memory_space=pl.ANY)
    vmem = pl.BlockSpec(memory_space=pltpu.VMEM)
    grid_spec = pltpu.PrefetchScalarGridSpec(
        num_scalar_prefetch=1, grid=(N_DEV, n_i),
        in_specs=[pl.BlockSpec((tm, D_MODEL), lambda s, i, o: (jnp.where(s == 0, i, 0), 0)),
                  pl.BlockSpec((1, D_MODEL), lambda s, i, o: (0, 0)), any_spec, vmem, any_spec],
        out_specs=[pl.BlockSpec((D_MODEL, tm), lambda s, i, o: (0, jnp.where(s == 0, i, n_i - 1))),
                   pl.BlockSpec((tm, SHARD_IN), lambda s, i, o: (i, o[s])), any_spec, any_spec, any_spec],
        scratch_shapes=[pltpu.VMEM((N_DEV,) + pack_b.shape, BF16), pltpu.VMEM((S, D_MODEL), BF16),
                        pltpu.SemaphoreType.DMA((24,)), pltpu.SemaphoreType.DMA((24,)), pltpu.SemaphoreType.DMA((3 + N_DEV,))],
    )
    return _call(
        body, name="inproj_gather", grid_spec=grid_spec,
        out_shape=[jax.ShapeDtypeStruct((D_MODEL, S), BF16), jax.ShapeDtypeStruct((S, IN_COLS), F32),
                   jax.ShapeDtypeStruct((N_DEV,) + pack_a.shape, BF16), jax.ShapeDtypeStruct((N_DEV,) + pack_b.shape, BF16),
                   jax.ShapeDtypeStruct((N_DEV,) + pack_c.shape, BF16)],
        compiler_params=_params(56, 2),
    )(order, x, norm_g, pack_a, pack_b, pack_c)


_SCATTER_FLIPS = (7, 6, 5, 4, 3, 2, 1, 0)
_N_CHIPS = 4


def _for_row_chunks(n_rows, chunk, fn):
    def step(c, carry):
        fn(pl.ds(pl.multiple_of(c * chunk, chunk), chunk))
        return carry

    lax.fori_loop(0, n_rows // chunk, step, 0)


def _grad_w_in_scatter(dproj, u_t, rs_a, rs_c, small_partial):
    S = u_t.shape[1]
    tm = min(S, 1024)
    n_i = S // tm
    order = jnp.stack([_index_of(_flip(_my_pos(), k)) for k in _SCATTER_FLIPS]).astype(jnp.int32)
    shapes = ((D_MODEL, SHARD_IN), rs_a.shape[1:], rs_c.shape[1:])
    row_chunk = 128

    def body(order_ref, dp_ref, ut_ref, ra_ref, rc_ref, p_ref, gb_ref, ga_ref, gc_ref, gs_ref, acc, sib_b, d2d_b,
             send_b, ici_b, mine_a, sib_a, ici_a, mine_c, sib_c, ici_c, gath, send_sems, recv_sems, local_sems):
        s, i = pl.program_id(0), pl.program_id(1)
        me = _my_pos()
        sibling = _flip(me, 1)

        my_chip = 2 * me[0] + me[1]

        small_d2d = pltpu.make_async_remote_copy(
            src_ref=p_ref, dst_ref=gath.at[_N_CHIPS], send_sem=send_sems.at[21], recv_sem=recv_sems.at[21],
            device_id=sibling, device_id_type=MESH)

        def small_ici(m):
            return pltpu.make_async_remote_copy(
                src_ref=gath.at[my_chip], dst_ref=gath.at[my_chip], send_sem=send_sems.at[22 + m],
                recv_sem=recv_sems.at[22 + m], device_id=_flip(me, 6 - 2 * m), device_id_type=MESH)
        sib = (sib_b, sib_a, sib_c)
        ici = (ici_b, ici_a, ici_c)
        outs = (gb_ref, ga_ref, gc_ref)

        def to_sibling(arr, m, src):
            return pltpu.make_async_remote_copy(
                src_ref=src, dst_ref=sib[arr].at[m], send_sem=send_sems.at[arr * 7 + m],
                recv_sem=recv_sems.at[arr * 7 + m], device_id=sibling, device_id_type=MESH)

        def over_ici(arr, m, src):
            return pltpu.make_async_remote_copy(
                src_ref=src, dst_ref=ici[arr].at[m], send_sem=send_sems.at[arr * 7 + 4 + m],
                recv_sem=recv_sems.at[arr * 7 + 4 + m], device_id=_flip(me, 6 - 2 * m), device_id_type=MESH)

        def from_sibling(arr, m):
            to_sibling(arr, m, sib[arr].at[m]).wait_recv()

        def from_ici(arr, m):
            over_ici(arr, m, ici[arr].at[m]).wait_recv()

        small = ((1, ra_ref, mine_a), (2, rc_ref, mine_c))

        def local_copy(arr, src, mine, m):
            return pltpu.make_async_copy(src.at[_index_of(_flip(me, 6 - 2 * m))], mine.at[m],
                                         local_sems.at[(arr - 1) * _N_CHIPS + m])

        @pl.when((s == 0) & (i == 0))
        def _():
            small_d2d.start()
            for arr, src, mine in small:
                for m in range(_N_CHIPS):
                    to_sibling(arr, m, src.at[_index_of(_flip(me, 7 - 2 * m))]).start()
                    local_copy(arr, src, mine, m).start()

        @pl.when((s == 1) & (i == 0))
        def _():
            small_d2d.wait_recv()
            gath[my_chip] = p_ref[...] + gath[_N_CHIPS]
            for m in range(_N_CHIPS - 1):
                small_ici(m).start()
            for arr, src, mine in small:
                rows, chunk = shapes[arr][0], 16
                for m in range(_N_CHIPS):
                    local_copy(arr, src, mine, m).wait()
                    from_sibling(arr, m)
                    if m < _N_CHIPS - 1:
                        def add(sl, arr=arr, mine=mine, m=m):
                            mine[m, sl, :] = (mine[m, sl, :].astype(F32) + sib[arr][m, sl, :].astype(F32)).astype(BF16)

                        _for_row_chunks(rows, chunk, add)
                        over_ici(arr, m, mine.at[m]).start()
                    else:
                        def keep(sl, arr=arr, mine=mine, m=m):
                            outs[arr][sl, :] = mine[m, sl, :].astype(F32) + sib[arr][m, sl, :].astype(F32)

                        _for_row_chunks(rows, chunk, keep)

        @pl.when(i == 0)
        def _():
            acc[...] = jnp.zeros(acc.shape, F32)

        acc[...] += jnp.dot(ut_ref[...], dp_ref[...], preferred_element_type=F32)

        def block_rows(c):
            return acc[c * row_chunk:(c + 1) * row_chunk, :]

        for m in range(_N_CHIPS):
            @pl.when((s == 2 * m) & (i == n_i - 1))
            def _(m=m):
                if m > 0:
                    to_sibling(0, m - 1, d2d_b).wait_send()
                for c in range(D_MODEL // row_chunk):
                    d2d_b[c * row_chunk:(c + 1) * row_chunk, :] = block_rows(c).astype(BF16)
                to_sibling(0, m, d2d_b).start()

            @pl.when((s == 2 * m + 1) & (i == n_i - 1))
            def _(m=m):
                from_sibling(0, m)
                slot = m % 2
                if m == 2:
                    over_ici(0, 0, send_b.at[0]).wait_send()
                for c in range(D_MODEL // row_chunk):
                    rows = slice(c * row_chunk, (c + 1) * row_chunk)
                    total = block_rows(c) + sib_b[m, rows, :].astype(F32)
                    if m < _N_CHIPS - 1:
                        send_b[slot, rows, :] = total.astype(BF16)
                    else:
                        gb_ref[rows, :] = total
                if m < _N_CHIPS - 1:
                    over_ici(0, m, send_b.at[slot]).start()

        @pl.when((s == N_DEV - 1) & (i == n_i - 1))
        def _():
            for arr in range(3):
                for m in range(_N_CHIPS - 1):
                    from_ici(arr, m)
                rows = shapes[arr][0]

                def add(sl, arr=arr):
                    outs[arr][sl, :] = (outs[arr][sl, :] + ici[arr][0, sl, :].astype(F32)
                                        + ici[arr][1, sl, :].astype(F32) + ici[arr][2, sl, :].astype(F32))

                _for_row_chunks(rows, 16, add)
            for m in range(_N_CHIPS - 1):
                small_ici(m).wait_recv()
            gs_ref[...] = (gath[0] + gath[1]) + (gath[2] + gath[3])
            small_d2d.wait_send()
            for m in range(_N_CHIPS - 1):
                small_ici(m).wait_send()
            to_sibling(0, _N_CHIPS - 1, d2d_b).wait_send()
            over_ici(0, 1, send_b.at[1]).wait_send()
            over_ici(0, 2, send_b.at[0]).wait_send()
            for arr, src, mine in small:
                for m in range(_N_CHIPS):
                    to_sibling(arr, m, src.at[0]).wait_send()
                for m in range(_N_CHIPS - 1):
                    over_ici(arr, m, mine.at[m]).wait_send()

    any_spec = pl.BlockSpec(memory_space=pl.ANY)
    vmem = pl.BlockSpec(memory_space=pltpu.VMEM)
    half = lambda shp, n: pltpu.VMEM((n,) + tuple(shp), BF16)
    grid_spec = pltpu.PrefetchScalarGridSpec(
        num_scalar_prefetch=1, grid=(N_DEV, n_i),
        in_specs=[pl.BlockSpec((tm, SHARD_IN), lambda s, i, o: (i, o[s])),
                  pl.BlockSpec((D_MODEL, tm), lambda s, i, o: (0, i)), any_spec, any_spec, vmem],
        out_specs=[vmem, vmem, vmem, vmem],
        scratch_shapes=[
            pltpu.VMEM((D_MODEL, SHARD_IN), F32), half(shapes[0], _N_CHIPS), pltpu.VMEM(shapes[0], BF16),
            half(shapes[0], 2), half(shapes[0], _N_CHIPS - 1),
            half(shapes[1], _N_CHIPS), half(shapes[1], _N_CHIPS), half(shapes[1], _N_CHIPS - 1),
            half(shapes[2], _N_CHIPS), half(shapes[2], _N_CHIPS), half(shapes[2], _N_CHIPS - 1),
            pltpu.VMEM((_N_CHIPS + 1,) + small_partial.shape, F32),
            pltpu.SemaphoreType.DMA((25,)), pltpu.SemaphoreType.DMA((25,)), pltpu.SemaphoreType.DMA((2 * _N_CHIPS,))],
    )
    return _call(
        body, name="grad_w_in_scatter", grid_spec=grid_spec,
        out_shape=[jax.ShapeDtypeStruct(shp, F32) for shp in shapes] + [jax.ShapeDtypeStruct(small_partial.shape, F32)],
        compiler_params=_params(60, 2),
    )(order, dproj, u_t, rs_a, rs_c, small_partial)


def _adam_update(g, w, m, v):
    m2 = ADAM_B1 * m + (1.0 - ADAM_B1) * g
    v2 = ADAM_B2 * v + (1.0 - ADAM_B2) * (g * g)
    m_hat = m2 / (1.0 - ADAM_B1 ** ADAM_STEP)
    v_hat = v2 / (1.0 - ADAM_B2 ** ADAM_STEP)
    delta = -ADAM_LR * (m_hat / (jnp.sqrt(v_hat) + ADAM_EPS) + ADAM_WD * w)
    return delta, m2, v2


def _adam_rows(g, w, m, v):
    rows, cols = w.shape
    tm = rows if rows % 256 else 256

    def fn(rv, cr, out):
        return list(_adam_update(*rv)), []

    outs, _ = _rowwise("adamw", fn, rows, tm, [(a, cols, 0) for a in (g, w, m, v)], [], [(cols, F32)] * 3, [], 32)
    return outs


_SMALL = ["norm_g", "hg_lb", "hg_norm_g", "s5_a_re", "s5_a_im", "s5_log_dt", "s5_b_re", "s5_b_im", "s5_c_re",
          "s5_c_im", "s5_d", "b_glu", "ple_norm_g", "final_norm_g"]
_BIG = ["w_in", "w_o_hg", "w_glu", "w_o_s5", "w_out", "w_ple", "w_ple_gate"]
_ORDER = ["norm_g", "w_in", "hg_lb", "hg_norm_g", "w_o_hg", "s5_a_re", "s5_a_im", "s5_log_dt", "s5_b_re", "s5_b_im",
          "s5_c_re", "s5_c_im", "s5_d", "w_glu", "b_glu", "w_o_s5", "w_out", "ple_norm_g", "w_ple", "w_ple_gate",
          "final_norm_g"]


def _pack_small(vals, tail=None):
    parts = []
    for name in _SMALL:
        flat = vals[name].reshape(-1).astype(F32)
        pad = (-flat.shape[0]) % 1024
        parts.append(jnp.pad(flat, (0, pad)))
    tail = jnp.zeros((0,), F32) if tail is None else tail.reshape(-1).astype(F32)
    parts.append(jnp.pad(tail, (0, 1024 - tail.shape[0])))
    return jnp.concatenate(parts).reshape(-1, 128)


def _unpack_small(packed, like):
    flat = packed.reshape(-1)
    out, off = {}, 0
    for name in _SMALL:
        size = like[name].size
        out[name] = flat[off:off + size].reshape(like[name].shape)
        off += size + (-size) % 1024
    return out


def _col_blocks(full):
    k = full.shape[0]
    return full.reshape(k, N_DEV, 128).transpose(1, 0, 2)


def _from_col_blocks(blocks):
    k = blocks.shape[1]
    return blocks.transpose(1, 0, 2).reshape(k, N_DEV * 128)


def kernel(x, p, norm_g, w_in, hg_lb, hg_norm_g, w_o_hg, s5_a_re, s5_a_im, s5_log_dt, s5_b_re, s5_b_im, s5_c_re, s5_c_im, s5_d, w_glu, b_glu, w_o_s5, w_out, ple_norm_g, w_ple, w_ple_gate, final_norm_g, loss_target, m_norm_g, m_w_in, m_hg_lb, m_hg_norm_g, m_w_o_hg, m_s5_a_re, m_s5_a_im, m_s5_log_dt, m_s5_b_re, m_s5_b_im, m_s5_c_re, m_s5_c_im, m_s5_d, m_w_glu, m_b_glu, m_w_o_s5, m_w_out, m_ple_norm_g, m_w_ple, m_w_ple_gate, m_final_norm_g, v_norm_g, v_w_in, v_hg_lb, v_hg_norm_g, v_w_o_hg, v_s5_a_re, v_s5_a_im, v_s5_log_dt, v_s5_b_re, v_s5_b_im, v_s5_c_re, v_s5_c_im, v_s5_d, v_w_glu, v_b_glu, v_w_o_s5, v_w_out, v_ple_norm_g, v_w_ple, v_w_ple_gate, v_final_norm_g):
    args = dict(locals())
    w = {n: args[n] for n in _ORDER}
    m = {n: args["m_" + n] for n in _ORDER}
    v = {n: args["v_" + n] for n in _ORDER}
    xs = x[0]
    ps = p[0, 0]
    tgt = loss_target[0]
    S = xs.shape[0]

    pack_a = jnp.concatenate([w_o_hg[0], w_out[0], w_ple_gate[0]], axis=0).astype(BF16)
    pack_b = w_in[0].astype(BF16)
    pack_c = jnp.concatenate([w_glu[0], w_o_s5[0], w_ple[0]], axis=0).astype(BF16)
    u_t, proj, all_a, all_b, all_c = _inproj_gather(xs, norm_g, pack_a, pack_b, pack_c)
    wf_o_hg = all_a[:, 0:128].reshape(D_MODEL, D_MODEL)
    wf_out = all_a[:, 128:256].reshape(D_MODEL, D_MODEL)
    wf_pg = all_a[:, 256:384].reshape(D_MODEL, D_MODEL)
    wf_glu = _from_col_blocks(all_c[:, 0:512])
    wf_o_s5 = _from_col_blocks(all_c[:, 512:1024])
    wf_ple = _from_col_blocks(all_c[:, 1024:1280])

    lb = jax.nn.sigmoid(hg_lb[0:1] - hg_lb[1:2])
    s5_names = ["s5_a_re", "s5_a_im", "s5_log_dt", "s5_b_re", "s5_b_im", "s5_c_re", "s5_c_im", "s5_d"]
    build = lambda *a: _s5_matrices(*a, seg_len=S // S5_SEG)
    mats_f32, mats_vjp = jax.vjp(build, *[w[n][0] for n in s5_names])
    mats = dict(mats_f32, b_q=mats_f32["b_q"].astype(BF16), c_q=mats_f32["c_q"].astype(BF16))
    bias_glu = b_glu

    o, states = _hgrn_fwd(proj, lb)
    u_perm = _to_segments(proj[:, COL_US:COL_US + S5_WIDTH])
    zeros_state = jnp.zeros((S5_SEG, S5_COLS), F32)
    (h_ends,) = _s5_fwd_pass(u_perm, mats, zeros_state, False)
    h0 = _segment_starts(h_ends, mats["pow_r"], mats["pow_i"], False)
    y_perm, h_init, _ = _s5_fwd_pass(u_perm, mats, h0, True)
    ys = _from_segments(y_perm)
    y_hg, y_s5, glu, h1 = _stage_branches(o, proj, ys, xs, hg_norm_g, wf_o_hg, wf_glu, bias_glu, wf_o_s5, wf_out)

    dh1, (loss_acc, d_final_g, d_ple_g, d_w_ple, d_w_pg) = _stage_ple_loss(
        h1, ps, tgt, ple_norm_g, wf_ple, wf_pg, final_norm_g.reshape(1, D_MODEL))
    (d_gate_hg, d_gate_s5, d_yhg, d_ys5), (d_w_out,) = _stage_bwd_merge(dh1, y_hg, y_s5, proj, wf_out)
    (d_o, d_g_hg), (d_w_o_hg, d_hg_norm) = _stage_bwd_hg_path(d_yhg, o, proj, hg_norm_g, wf_o_hg)
    (d_ys, d_z), (d_w_o_s5, d_w_glu, d_b_glu) = _stage_bwd_s5_path(d_ys5, ys, glu, proj, wf_o_s5, wf_glu)
    dq, df, div, d_lb = _hgrn_bwd(proj, lb, d_o, states)
    dy_perm = _to_segments(d_ys)
    g_ends = _s5_bwd_ends(dy_perm, mats)
    g0 = _segment_starts(g_ends, mats["pow_r"], mats["pow_i"], True)
    du_perm, acc_p, acc_q, d_bq, d_cq_t, d_d = _s5_bwd_full(u_perm, dy_perm, h_init, g0, mats)
    d_us = _from_segments(du_perm)
    grad_x, dproj, d_norm_g = _stage_inproj_bwd([dq, df, div, d_g_hg, d_us, d_z, d_gate_hg, d_gate_s5], xs, dh1,
                                                norm_g, all_b)

    p_re, p_im = _split_cols(acc_p)
    q_re, q_im = _split_cols(acc_q)
    d_lam_r = (p_re + p_im)[0]
    d_lam_i = (q_im - q_re)[0]
    zero_row = jnp.zeros((S5_SEG, S5_COLS), F32)
    row_of = lambda re_part, im_part: zero_row.at[0].set(_join_cols(re_part[None], im_part[None])[0])
    zeros_q = jnp.zeros_like(d_lam_r)
    cot = dict(
        lam_r=row_of(d_lam_r, zeros_q), lam_i=row_of(zeros_q, d_lam_i),
        b_q=d_bq, c_q=d_cq_t.transpose(0, 2, 1), d_row=d_d,
        pow_r=jnp.zeros_like(mats["pow_r"]), pow_i=jnp.zeros_like(mats["pow_i"]),
    )
    d_s5 = mats_vjp(cot)

    s_lb = lb * (1.0 - lb)
    d_hg_lb = jnp.concatenate([d_lb * s_lb, -d_lb * s_lb], axis=0)
    small_g = dict(norm_g=d_norm_g, hg_lb=d_hg_lb, hg_norm_g=d_hg_norm, b_glu=d_b_glu, ple_norm_g=d_ple_g,
                   final_norm_g=d_final_g)
    for name, g in zip(s5_names, d_s5):
        small_g[name] = g
    pk = lambda d: _pack_small({n: d[n] for n in _SMALL})
    rs_a = jnp.concatenate([d_w_o_hg.reshape(N_DEV, 128, D_MODEL), d_w_out.reshape(N_DEV, 128, D_MODEL),
                            d_w_pg.reshape(N_DEV, 128, D_MODEL)], axis=1).astype(BF16)
    rs_c = jnp.concatenate([_col_blocks(d_w_glu), _col_blocks(d_w_o_s5), _col_blocks(d_w_ple)], axis=1).astype(BF16)
    partial = _pack_small({n: small_g[n] for n in _SMALL}, tail=loss_acc[0, 0:1])
    g_b, g_a, g_c, sg = _grad_w_in_scatter(dproj, u_t, rs_a, rs_c, partial)
    sd, sm, sv = _adam_rows(sg, pk(w), pk(m), pk(v))
    like = {n: w[n] for n in _SMALL}
    out_g, out_d, out_m, out_v = (_unpack_small(t, like) for t in (sg, sd, sm, sv))
    big_g = dict(w_o_hg=g_a[0:128], w_out=g_a[128:256], w_ple_gate=g_a[256:384], w_in=g_b,
                 w_glu=g_c[0:512], w_o_s5=g_c[512:1024], w_ple=g_c[1024:1280])
    for name in _BIG:
        shape = w[name].shape
        g2 = big_g[name]
        d2, m2, v2 = _adam_rows(g2, w[name][0], m[name][0], v[name][0])
        out_g[name], out_d[name], out_m[name], out_v[name] = (t.reshape(shape) for t in (g2, d2, m2, v2))

    loss = sg[sg.shape[0] - 8, 0]
    return (loss, grad_x[None], *[out_g[n] for n in _ORDER], *[out_d[n] for n in _ORDER],
            *[out_m[n] for n in _ORDER], *[out_v[n] for n in _ORDER])
```

```python
import functools
import math

import jax
import jax.numpy as jnp
from jax import lax
from jax.experimental import pallas as pl
from jax.experimental.pallas import tpu as pltpu

F32 = jnp.float32
BF16 = jnp.bfloat16

D_MODEL = 1024
N_DEV = 8
IN_COLS = 7168
SHARD_IN = IN_COLS // N_DEV
HG_HEADS = 8
HG_DIM = 128
HG_CHUNK = 64
HG_SUPER_FWD = 256
HG_SUPER_BWD = 128
HG_HEADS_PER_STEP = 8
S5_WIDTH = 512
S5_GROUPS = 32
S5_STATE = 64
S5_CH = 16
S5_SEG = 8
S5_QUADS = 4
S5_QCOLS = 1024
S5_COLS = S5_QUADS * S5_QCOLS
S5_TILE_STEPS = 64
S5_UNROLL = 8
NORM_EPS = 1e-6
ADAM_LR = 0.001
ADAM_B1 = 0.9
ADAM_B2 = 0.999
ADAM_EPS = 1e-08
ADAM_WD = 0.01
ADAM_STEP = 10
MIB = 1024 * 1024
MESH = pl.DeviceIdType.MESH

COL_Q, COL_F, COL_I, COL_G, COL_US, COL_ZS, COL_GH, COL_GS = 0, 1024, 2048, 3072, 4096, 4608, 5120, 6144


def _call(body, **kw):
    return pl.pallas_call(body, **kw)


def _params(vmem_mb, n_grid=1):
    return pltpu.CompilerParams(
        dimension_semantics=("arbitrary",) * n_grid, vmem_limit_bytes=vmem_mb * MIB
    )


def _bdot(a, b):
    return jnp.dot(a.astype(BF16), b.astype(BF16), preferred_element_type=F32)


def _bdot_nt(a, b):
    return lax.dot_general(a.astype(BF16), b.astype(BF16), (((1,), (1,)), ((), ())), preferred_element_type=F32)


def _bdot_tn(a, b):
    return lax.dot_general(a.astype(BF16), b.astype(BF16), (((0,), (0,)), ((), ())), preferred_element_type=F32)


def _sigmoid(x):
    return jax.nn.sigmoid(x)


def _silu(x):
    return x * _sigmoid(x)


def _dsilu(x):
    s = _sigmoid(x)
    return s * (1.0 + x * (1.0 - s))


_GELU_C = math.sqrt(2.0 / math.pi)


def _gelu(x):
    return 0.5 * x * (1.0 + jnp.tanh(_GELU_C * (x + 0.044715 * x * x * x)))


def _dgelu(x):
    t = jnp.tanh(_GELU_C * (x + 0.044715 * x * x * x))
    return 0.5 * (1.0 + t) + 0.5 * x * (1.0 - t * t) * _GELU_C * (1.0 + 3.0 * 0.044715 * x * x)


def _rms_fwd(x, g):
    r = lax.rsqrt(jnp.mean(x * x, axis=-1, keepdims=True) + NORM_EPS)
    n = x * r
    return n * g, n, r


def _rms_bwd(dy, n, r, g):
    dn = dy * g
    dx = r * (dn - n * jnp.mean(dn * n, axis=-1, keepdims=True))
    return dx, jnp.sum(dy * n, axis=0, keepdims=True)


def _head_rms_fwd(o, g):
    ns, rs = [], []
    for h in range(HG_HEADS):
        oh = o[:, h * HG_DIM:(h + 1) * HG_DIM]
        r = lax.rsqrt(jnp.mean(oh * oh, axis=-1, keepdims=True) + NORM_EPS)
        ns.append(oh * r)
        rs.append(r)
    n = jnp.concatenate(ns, axis=1)
    return n * g, n, rs


def _head_rms_bwd(dy, n, rs, g):
    dn = dy * g
    dxs = []
    for h in range(HG_HEADS):
        sl = slice(h * HG_DIM, (h + 1) * HG_DIM)
        dxs.append(rs[h] * (dn[:, sl] - n[:, sl] * jnp.mean(dn[:, sl] * n[:, sl], axis=-1, keepdims=True)))
    return jnp.concatenate(dxs, axis=1), jnp.sum(dy * n, axis=0, keepdims=True)


def _rowwise(name, fn, n_rows, tm, rows, consts, out_rows, out_accs, vmem_mb, parts=1):
    n_r, n_c, n_or, n_oa = len(rows), len(consts), len(out_rows), len(out_accs)
    tp = tm // parts

    def body(*refs):
        r_refs = refs[:n_r]
        c_refs = refs[n_r:n_r + n_c]
        or_refs = refs[n_r + n_c:n_r + n_c + n_or]
        oa_refs = refs[n_r + n_c + n_or:]

        if n_oa:
            @pl.when(pl.program_id(0) == 0)
            def _():
                for ref in oa_refs:
                    ref[...] = jnp.zeros(ref.shape, ref.dtype)

        for part in range(parts):
            sl = slice(part * tp, (part + 1) * tp)
            outs, accs = fn([r[sl, :] for r in r_refs], c_refs, [o.at[sl, :] for o in or_refs])
            for ref, v in zip(or_refs, outs):
                if v is not None:
                    ref[sl, :] = v.astype(ref.dtype)
            for ref, v in zip(oa_refs, accs):
                ref[...] += v.astype(ref.dtype)

    in_specs = [pl.BlockSpec((tm, w), functools.partial(lambda i, c: (i, c), c=cb)) for (_, w, cb) in rows]
    in_specs += [pl.BlockSpec(c.shape, functools.partial(lambda i, nd: (0,) * nd, nd=c.ndim),
                              pipeline_mode=pl.Buffered(1)) for c in consts]
    out_specs = [pl.BlockSpec((tm, w), lambda i: (i, 0)) for (w, _) in out_rows]
    out_specs += [pl.BlockSpec(s, functools.partial(lambda i, nd: (0,) * nd, nd=len(s))) for (s, _) in out_accs]
    out_shape = [jax.ShapeDtypeStruct((n_rows, w), dt) for (w, dt) in out_rows]
    out_shape += [jax.ShapeDtypeStruct(s, dt) for (s, dt) in out_accs]
    res = _call(
        body, name=name, grid=(n_rows // tm,), in_specs=in_specs, out_specs=out_specs, out_shape=out_shape,
        compiler_params=_params(vmem_mb),
    )(*[a for (a, _, _) in rows], *consts)
    return res[:n_or], res[n_or:]


def _stage_branches(o, proj, ys, x, hg_norm_g, w_o_hg, w_glu, b_glu, w_o_s5, w_out):
    S = x.shape[0]

    def fn(rv, cr, out):
        o_b, ys_b, x_b = rv[0], rv[5], rv[6]
        g_hg, z_s, gate_hg, gate_s5 = (v.astype(F32) for v in rv[1:5])
        gn_ref, wohg_ref, wglu_ref, bglu_ref, wos5_ref, wout_ref = cr
        on, _, _ = _head_rms_fwd(o_b, gn_ref[...])
        a = on * _silu(g_hg)
        y_hg = jnp.dot(a.astype(BF16), wohg_ref[...], preferred_element_type=F32)
        gl = _gelu(ys_b)
        glu = jnp.dot(gl.astype(BF16), wglu_ref[...], preferred_element_type=F32) + bglu_ref[...]
        ys2 = glu[:, :S5_WIDTH] * _sigmoid(glu[:, S5_WIDTH:]) * _silu(z_s)
        y_s5 = jnp.dot(ys2.astype(BF16), wos5_ref[...], preferred_element_type=F32)
        merged = _sigmoid(gate_hg) * y_hg + _sigmoid(gate_s5) * y_s5
        h1 = x_b + jnp.dot(merged.astype(BF16), wout_ref[...], preferred_element_type=F32)
        return [y_hg, y_s5, glu, h1], []

    rows = [(o, D_MODEL, 0), (proj, D_MODEL, COL_G // D_MODEL), (proj, S5_WIDTH, COL_ZS // S5_WIDTH),
            (proj, D_MODEL, COL_GH // D_MODEL), (proj, D_MODEL, COL_GS // D_MODEL), (ys, S5_WIDTH, 0), (x, D_MODEL, 0)]
    (y_hg, y_s5, glu, h1), _ = _rowwise(
        "branches", fn, S, 256, rows, [hg_norm_g, w_o_hg, w_glu, b_glu, w_o_s5, w_out],
        [(D_MODEL, BF16), (D_MODEL, BF16), (D_MODEL, BF16), (D_MODEL, F32)], [], 56)
    return y_hg, y_s5, glu, h1


def _stage_ple_loss(h1, p, target, ple_norm_g, w_ple, w_ple_gate, final_norm_g):
    S = h1.shape[0]

    def fn(rv, cr, out):
        h1_b, p_b, t_b = rv
        gp_ref, wple_ref, wpg_ref, gf_ref = cr
        n2g, n2, r2 = _rms_fwd(h1_b, gp_ref[...])
        z = jnp.dot(n2g.astype(BF16), wpg_ref[...], preferred_element_type=F32)
        gate = _sigmoid(z)
        pe = jnp.dot(p_b.astype(BF16), wple_ref[...], preferred_element_type=F32)
        h2 = h1_b + pe * gate
        y, nf, rf = _rms_fwd(h2, gf_ref[...])
        err = y - t_b
        loss_rows = 0.5 * jnp.mean(err * err, axis=-1, keepdims=True)
        loss_inc = jnp.broadcast_to(jnp.sum(loss_rows, axis=0, keepdims=True), (1, 128))
        dy = err * (1.0 / D_MODEL)
        dh2, d_gf = _rms_bwd(dy, nf, rf, gf_ref[...])
        d_pe = dh2 * gate
        dz = dh2 * pe * gate * (1.0 - gate)
        d_wple = _bdot_tn(p_b, d_pe)
        d_wpg = _bdot_tn(n2g, dz)
        dn2g = _bdot_nt(dz, wpg_ref[...])
        dh1n, d_gp = _rms_bwd(dn2g, n2, r2, gp_ref[...])
        return [dh2 + dh1n], [loss_inc, d_gf, d_gp, d_wple, d_wpg]

    (dh1,), accs = _rowwise(
        "ple_loss", fn, S, 512, [(h1, D_MODEL, 0), (p, 256, 0), (target, D_MODEL, 0)],
        [ple_norm_g, w_ple, w_ple_gate, final_norm_g], [(D_MODEL, F32)],
        [((1, 128), F32), ((1, D_MODEL), F32), ((1, D_MODEL), F32), ((256, D_MODEL), F32), ((D_MODEL, D_MODEL), F32)], 56,
        parts=2)
    return dh1, accs


def _stage_bwd_merge(dh1, y_hg, y_s5, proj, w_out):
    S = dh1.shape[0]

    def fn(rv, cr, out):
        dh1_b, yhg, ys5 = rv[:3]
        gate_hg, gate_s5 = rv[3].astype(F32), rv[4].astype(F32)
        (wout_ref,) = cr
        sg_h, sg_s = _sigmoid(gate_hg), _sigmoid(gate_s5)
        merged = sg_h * yhg + sg_s * ys5
        d_wout = _bdot_tn(merged, dh1_b)
        d_merged = _bdot_nt(dh1_b, wout_ref[...])
        d_gate_hg = d_merged * yhg * sg_h * (1.0 - sg_h)
        d_gate_s5 = d_merged * ys5 * sg_s * (1.0 - sg_s)
        return [d_gate_hg, d_gate_s5, d_merged * sg_h, d_merged * sg_s], [d_wout]

    rows = [(dh1, D_MODEL, 0), (y_hg, D_MODEL, 0), (y_s5, D_MODEL, 0), (proj, D_MODEL, COL_GH // D_MODEL),
            (proj, D_MODEL, COL_GS // D_MODEL)]
    outs, accs = _rowwise("bwd_merge", fn, S, 512, rows, [w_out], [(D_MODEL, BF16)] * 4,
                          [((D_MODEL, D_MODEL), F32)], 56, parts=2)
    return outs, accs


def _stage_bwd_hg_path(d_yhg, o, proj, hg_norm_g, w_o_hg):
    S = o.shape[0]

    def fn(rv, cr, out):
        d_yhg_b, o_b, g_hg = rv[0], rv[1], rv[2].astype(F32)
        gn_ref, wohg_ref = cr
        ong, on, rs = _head_rms_fwd(o_b, gn_ref[...])
        sil = _silu(g_hg)
        d_wohg = _bdot_tn(ong * sil, d_yhg_b)
        d_a = _bdot_nt(d_yhg_b, wohg_ref[...])
        d_g_hg = d_a * ong * _dsilu(g_hg)
        d_o, d_gn = _head_rms_bwd(d_a * sil, on, rs, gn_ref[...])
        return [d_o, d_g_hg], [d_wohg, d_gn]

    rows = [(d_yhg, D_MODEL, 0), (o, D_MODEL, 0), (proj, D_MODEL, COL_G // D_MODEL)]
    outs, accs = _rowwise("bwd_hg_path", fn, S, 512, rows, [hg_norm_g, w_o_hg], [(D_MODEL, BF16)] * 2,
                          [((D_MODEL, D_MODEL), F32), ((1, D_MODEL), F32)], 56, parts=2)
    return outs, accs


def _stage_bwd_s5_path(d_ys5, ys, glu, proj, w_o_s5, w_glu):
    S = ys.shape[0]

    def fn(rv, cr, out):
        d_ys5_b, ys_b, glu_b, z_s = rv[0], rv[1], rv[2].astype(F32), rv[3].astype(F32)
        wos5_ref, wglu_ref = cr
        ga, gb = glu_b[:, :S5_WIDTH], glu_b[:, S5_WIDTH:]
        sgb, silz = _sigmoid(gb), _silu(z_s)
        ys2 = ga * sgb * silz
        d_wos5 = _bdot_tn(ys2, d_ys5_b)
        d_ys2 = _bdot_nt(d_ys5_b, wos5_ref[...])
        d_ga = d_ys2 * sgb * silz
        d_gb = d_ys2 * ga * sgb * (1.0 - sgb) * silz
        d_z = d_ys2 * ga * sgb * _dsilu(z_s)
        d_glu = jnp.concatenate([d_ga, d_gb], axis=1)
        gl = _gelu(ys_b)
        d_wglu = _bdot_tn(gl, d_glu)
        d_bglu = jnp.sum(d_glu, axis=0, keepdims=True)
        d_gl = _bdot_nt(d_glu, wglu_ref[...])
        return [d_gl * _dgelu(ys_b), d_z], [d_wos5, d_wglu, d_bglu]

    rows = [(d_ys5, D_MODEL, 0), (ys, S5_WIDTH, 0), (glu, D_MODEL, 0), (proj, S5_WIDTH, COL_ZS // S5_WIDTH)]
    outs, accs = _rowwise(
        "bwd_s5_path", fn, S, 512, rows, [w_o_s5, w_glu], [(S5_WIDTH, F32), (S5_WIDTH, BF16)],
        [((S5_WIDTH, D_MODEL), F32), ((S5_WIDTH, D_MODEL), F32), ((1, D_MODEL), F32)], 48, parts=2)
    return outs, accs


def _stage_inproj_bwd(pieces, x, dh1, norm_g, w_in_all):
    S = x.shape[0]

    def fn(rv, cr, out):
        g_ref, w_ref = cr
        x_b, dh1_b = rv[8], rv[9]
        dproj_ref = out[1]
        col = 0
        for v in rv[:8]:
            dproj_ref[:, col:col + v.shape[1]] = v.astype(BF16)
            col += v.shape[1]
        d_u = jnp.zeros((x_b.shape[0], D_MODEL), F32)
        for j in range(N_DEV):
            d_u = d_u + lax.dot_general(dproj_ref[:, j * SHARD_IN:(j + 1) * SHARD_IN], w_ref[j],
                                        (((1,), (1,)), ((), ())), preferred_element_type=F32)
        _, n, r = _rms_fwd(x_b, g_ref[...])
        dx, d_g = _rms_bwd(d_u, n, r, g_ref[...])
        return [dh1_b + dx, None], [d_g]

    rows = [(a, a.shape[1], 0) for a in pieces] + [(x, D_MODEL, 0), (dh1, D_MODEL, 0)]
    (grad_x, dproj), (d_g,) = _rowwise(
        "inproj_bwd", fn, S, 256, rows, [norm_g, w_in_all], [(D_MODEL, F32), (IN_COLS, BF16)],
        [((1, D_MODEL), F32)], 56)
    return grad_x, dproj, d_g


def _chunk_row(shape):
    return lax.broadcasted_iota(jnp.int32, shape, 0) & (HG_CHUNK - 1)


def _chunk_cumsum(x):
    r_in = _chunk_row(x.shape)
    s = 1
    while s < HG_CHUNK:
        x = x + jnp.where(r_in >= s, pltpu.roll(x, s, 0), 0.0)
        s *= 2
    return x


def _chunk_suffix_sum(x):
    n = x.shape[0]
    r_in = _chunk_row(x.shape)
    s = 1
    while s < HG_CHUNK:
        x = x + jnp.where(r_in < HG_CHUNK - s, pltpu.roll(x, n - s, 0), 0.0)
        s *= 2
    return x


def _hgrn_prep(q, fl, lb):
    sup = q.shape[0]
    nc = sup // HG_CHUNK
    sig = _sigmoid(fl)
    f = lb + (1.0 - lb) * sig
    k = (1.0 - lb) * (1.0 - sig)
    b = _chunk_cumsum(jnp.log(f))
    b3 = b.reshape(nc, HG_CHUNK, HG_DIM)
    row3 = lax.broadcasted_iota(jnp.int32, b3.shape, 1)
    pick = lambda r: jnp.sum(jnp.where(row3 == r, b3, 0.0), axis=1, keepdims=True)
    b_mid = pick(HG_CHUNK // 2 - 1)
    b_last = pick(HG_CHUNK - 1)
    flat = lambda t: t.reshape(sup, HG_DIM)
    e_qa = flat(jnp.exp(b3 - b_mid))
    e_ka = flat(jnp.exp(b_mid - b3))
    e_qd = jnp.exp(b)
    e_kd = flat(jnp.exp(b_last - b3))
    dc = jnp.exp(b_last)
    return sig, f, k, e_qa, e_ka, e_qd, e_kd, dc


def _hgrn_mask(sup):
    r = lax.broadcasted_iota(jnp.int32, (sup, sup), 0)
    c = lax.broadcasted_iota(jnp.int32, (sup, sup), 1)
    shift = HG_CHUNK.bit_length() - 1
    return (jnp.right_shift(r, shift) == jnp.right_shift(c, shift)) & (r >= c)


def _hgrn_fwd(proj, lb):
    S = proj.shape[0]
    sup = HG_SUPER_FWD
    nb = S // sup
    nc = sup // HG_CHUNK
    hp = HG_HEADS_PER_STEP
    wide = hp * HG_DIM

    def body(q_ref, f_ref, iv_ref, lb_ref, o_ref, st_ref, state):
        @pl.when(pl.program_id(1) == 0)
        def _():
            state[...] = jnp.zeros(state.shape, F32)

        mask = _hgrn_mask(sup)
        for hh in range(hp):
            lanes = slice(hh * HG_DIM, (hh + 1) * HG_DIM)
            q, iv = q_ref[:, lanes], iv_ref[:, lanes]
            _, _, k, e_qa, e_ka, e_qd, e_kd, dc = _hgrn_prep(q, f_ref[:, lanes], lb_ref[:, lanes])
            scores = jnp.where(mask, _bdot_nt(q * e_qa, k * e_ka), 0.0)
            o_intra = _bdot(scores, iv)
            qd, kd = q * e_qd, k * e_kd
            for c in range(nc):
                sl = slice(c * HG_CHUNK, (c + 1) * HG_CHUNK)
                st = state[hh]
                st_ref[hh, c] = st
                o_ref[sl, lanes] = o_intra[sl] + _bdot_nt(qd[sl], st)
                state[hh] = dc[c] * st + _bdot_tn(iv[sl], kd[sl])

    blk = lambda base: pl.BlockSpec((sup, wide), functools.partial(lambda h, i, b: (i, b + h), b=base // wide))
    return _call(
        body, name="hgrn_fwd", grid=(HG_HEADS // hp, nb),
        in_specs=[blk(COL_Q), blk(COL_F), blk(COL_I), pl.BlockSpec((1, wide), lambda h, i: (0, h))],
        out_specs=[pl.BlockSpec((sup, wide), lambda h, i: (i, h)),
                   pl.BlockSpec((hp, nc, HG_DIM, HG_DIM), lambda h, i: (h, i, 0, 0))],
        out_shape=[jax.ShapeDtypeStruct((S, D_MODEL), F32),
                   jax.ShapeDtypeStruct((HG_HEADS, S // HG_CHUNK, HG_DIM, HG_DIM), F32)],
        scratch_shapes=[pltpu.VMEM((hp, HG_DIM, HG_DIM), F32)],
        compiler_params=_params(40, 2),
    )(proj, proj, proj, lb)


def _hgrn_bwd(proj, lb, d_o, states):
    S = proj.shape[0]
    sup = HG_SUPER_BWD
    nb = S // sup
    nc = sup // HG_CHUNK
    hp = HG_HEADS_PER_STEP
    wide = hp * HG_DIM

    def body(q_ref, f_ref, iv_ref, lb_ref, do_ref, st_ref, dq_ref, df_ref, div_ref, dlb_ref, dstate):
        @pl.when(pl.program_id(1) == 0)
        def _():
            dstate[...] = jnp.zeros(dstate.shape, F32)
            dlb_ref[...] = jnp.zeros(dlb_ref.shape, F32)

        mask = _hgrn_mask(sup)
        for hh in range(hp):
            lanes = slice(hh * HG_DIM, (hh + 1) * HG_DIM)
            q, iv, do, lb_v = q_ref[:, lanes], iv_ref[:, lanes], do_ref[:, lanes], lb_ref[:, lanes]
            sig, f, k, e_qa, e_ka, e_qd, e_kd, dc = _hgrn_prep(q, f_ref[:, lanes], lb_v)
            qa, ka, qd, kd = q * e_qa, k * e_ka, q * e_qd, k * e_kd
            scores = jnp.where(mask, _bdot_nt(qa, ka), 0.0)
            d_scores = jnp.where(mask, _bdot_nt(do, iv), 0.0)
            d_iv_intra = _bdot_tn(scores, do)
            d_qa = _bdot(d_scores, ka)
            d_ka = _bdot_tn(d_scores, qa)
            d_qd, d_kd, d_last = [None] * nc, [None] * nc, [None] * nc
            for c in reversed(range(nc)):
                sl = slice(c * HG_CHUNK, (c + 1) * HG_CHUNK)
                st = st_ref[hh, c]
                ds = dstate[hh]
                d_qd[c] = _bdot(do[sl], st)
                d_kd[c] = _bdot(iv[sl], ds)
                div_ref[sl, lanes] = (d_iv_intra[sl] + _bdot_nt(kd[sl], ds)).astype(div_ref.dtype)
                d_last[c] = (jnp.sum(ds * st, axis=0, keepdims=True) * dc[c]
                             + jnp.sum(d_kd[c] * kd[sl], axis=0, keepdims=True))
                dstate[hh] = dc[c] * ds + _bdot_tn(do[sl], qd[sl])
            d_qd = jnp.concatenate(d_qd, axis=0)
            d_kd = jnp.concatenate(d_kd, axis=0)
            d_b = d_qa * qa - d_ka * ka + d_qd * qd - d_kd * kd
            last_rows = jnp.concatenate([jnp.broadcast_to(t, (HG_CHUNK, HG_DIM)) for t in d_last], axis=0)
            d_b = d_b + jnp.where(_chunk_row(d_b.shape) == HG_CHUNK - 1, last_rows, 0.0)
            d_logf = _chunk_suffix_sum(d_b)
            d_k = d_ka * e_ka + d_kd * e_kd
            g_f = d_logf / f
            d_sig = (g_f - d_k) * (1.0 - lb_v)
            dq_ref[:, lanes] = (d_qa * e_qa + d_qd * e_qd).astype(dq_ref.dtype)
            df_ref[:, lanes] = (d_sig * sig * (1.0 - sig)).astype(df_ref.dtype)
            d_lb = jnp.sum((g_f - d_k) * (1.0 - sig), axis=0, keepdims=True)
            dlb_ref[:, lanes] += jnp.broadcast_to(d_lb, (8, HG_DIM))

    rev = lambda i: nb - 1 - i
    blk = lambda base: pl.BlockSpec((sup, wide), functools.partial(lambda h, i, b: (rev(i), b + h), b=base // wide))
    row_out = pl.BlockSpec((sup, wide), lambda h, i: (rev(i), h))
    dq, df, div, dlb = _call(
        body, name="hgrn_bwd", grid=(HG_HEADS // hp, nb),
        in_specs=[blk(COL_Q), blk(COL_F), blk(COL_I), pl.BlockSpec((1, wide), lambda h, i: (0, h)),
                  pl.BlockSpec((sup, wide), lambda h, i: (rev(i), h)),
                  pl.BlockSpec((hp, nc, HG_DIM, HG_DIM), lambda h, i: (h, rev(i), 0, 0))],
        out_specs=[row_out, row_out, row_out, pl.BlockSpec((8, wide), lambda h, i: (0, h))],
        out_shape=[jax.ShapeDtypeStruct((S, D_MODEL), BF16)] * 3 + [jax.ShapeDtypeStruct((8, D_MODEL), F32)],
        scratch_shapes=[pltpu.VMEM((hp, HG_DIM, HG_DIM), F32)],
        compiler_params=_params(40, 2),
    )(proj, proj, proj, lb, d_o, states)
    return dq, df, div, dlb[0:1]


def _s5_matrices(a_re, a_im, log_dt, b_re, b_im, c_re, c_im, d, seg_len):
    dt = jnp.exp(log_dt)[:, None]
    mag = jnp.exp(a_re * dt)
    lr, li = mag * jnp.cos(a_im * dt), mag * jnp.sin(a_im * dt)
    den = a_re * a_re + a_im * a_im
    nr = lr - 1.0
    sr = (nr * a_re + li * a_im) / den
    si = (li * a_re - nr * a_im) / den
    bbr = sr[..., None] * b_re - si[..., None] * b_im
    bbi = sr[..., None] * b_im + si[..., None] * b_re
    eye = jnp.eye(8, dtype=F32)

    def quad_cols(v):
        return v.reshape(S5_QUADS, 8 * S5_STATE)

    def lam_row(re_part, im_part):
        row = jnp.concatenate([quad_cols(re_part), quad_cols(im_part)], axis=1).reshape(1, S5_COLS)
        return jnp.broadcast_to(row, (S5_SEG, S5_COLS))

    def b_mat(bb):
        t = bb.reshape(S5_QUADS, 8, S5_STATE, S5_CH)
        return jnp.einsum("qgnc,gh->qgchn", t, eye).reshape(S5_QUADS, 8 * S5_CH, 8 * S5_STATE)

    def c_mat(cc):
        t = cc.reshape(S5_QUADS, 8, S5_CH, S5_STATE)
        return jnp.einsum("qgcn,gh->qgnhc", t, eye).reshape(S5_QUADS, 8 * S5_STATE, 8 * S5_CH)

    ang = a_im * dt * seg_len
    magp = jnp.exp(a_re * dt * seg_len)
    lpr, lpi = magp * jnp.cos(ang), magp * jnp.sin(ang)
    return dict(
        lam_r=lam_row(lr, lr), lam_i=lam_row(-li, li),
        b_q=jnp.concatenate([b_mat(bbr), b_mat(bbi)], axis=2),
        c_q=jnp.concatenate([c_mat(c_re), -c_mat(c_im)], axis=1),
        d_row=d.reshape(1, S5_WIDTH), pow_r=quad_cols(lpr), pow_i=quad_cols(lpi),
    )


def _s5_parts(v):
    half = S5_QCOLS // 2
    return tuple(v[:, k * half:(k + 1) * half] for k in range(2 * S5_QUADS))


def _s5_advance(parts, lr_ref, li_ref, x_ref, sl, conj):
    half = S5_QCOLS // 2
    out = []
    for q in range(S5_QUADS):
        re_c = slice(q * S5_QCOLS, q * S5_QCOLS + half)
        im_c = slice(q * S5_QCOLS + half, (q + 1) * S5_QCOLS)
        lr, li = lr_ref[:, re_c], li_ref[:, im_c]
        hr, hi = parts[2 * q], parts[2 * q + 1]
        if conj:
            out += [lr * hr + li * hi + x_ref[sl, re_c], lr * hi - li * hr + x_ref[sl, im_c]]
        else:
            out += [lr * hr - li * hi + x_ref[sl, re_c], lr * hi + li * hr + x_ref[sl, im_c]]
    return tuple(out)


def _scan_loop(step, init):
    def trip(o, carry):
        for j in range(S5_UNROLL):
            carry = step(o * S5_UNROLL + j, carry)
        return carry

    return lax.fori_loop(0, S5_TILE_STEPS // S5_UNROLL, trip, init)


def _s5_store(ref, sl, parts):
    half = S5_QCOLS // 2
    for k, v in enumerate(parts):
        ref[sl, k * half:(k + 1) * half] = v


def _s5_fwd_pass(u_perm, mats, h0, with_output):
    S = u_perm.shape[0]
    rows = S5_TILE_STEPS * S5_SEG
    nt = S // rows

    def body(*refs):
        if with_output:
            u_ref, b_ref, lr_ref, li_ref, h0_ref, c_ref, d_ref, y_ref, hinit_ref, hend_ref, xs, hcar = refs
        else:
            u_ref, b_ref, lr_ref, li_ref, h0_ref, hend_ref, xs, hcar = refs

        @pl.when(pl.program_id(0) == 0)
        def _():
            hcar[...] = h0_ref[...]

        if with_output:
            hinit_ref[...] = hcar[...]
        u = u_ref[...]
        ub = u.astype(BF16)
        for q in range(S5_QUADS):
            xs[:, q * S5_QCOLS:(q + 1) * S5_QCOLS] = jnp.dot(ub[:, q * 128:(q + 1) * 128], b_ref[q], preferred_element_type=F32)

        def step(t, h):
            sl = pl.ds(pl.multiple_of(t * S5_SEG, S5_SEG), S5_SEG)
            hn = _s5_advance(h, lr_ref, li_ref, xs, sl, False)
            _s5_store(xs, sl, hn)
            return hn

        h = _scan_loop(step, _s5_parts(hcar[...]))
        _s5_store(hcar, slice(None), h)
        _s5_store(hend_ref, slice(None), h)
        if with_output:
            ys = [jnp.dot(xs[:, q * S5_QCOLS:(q + 1) * S5_QCOLS].astype(BF16), c_ref[q], preferred_element_type=F32)
                  for q in range(S5_QUADS)]
            y_ref[...] = jnp.concatenate(ys, axis=1) + d_ref[...] * u

    full = lambda a: pl.BlockSpec(a.shape, functools.partial(lambda i, nd: (0,) * nd, nd=a.ndim))
    ins = [u_perm, mats["b_q"], mats["lam_r"], mats["lam_i"], h0]
    in_specs = [pl.BlockSpec((rows, S5_WIDTH), lambda i: (i, 0))] + [full(a) for a in ins[1:]]
    out_specs = [pl.BlockSpec((S5_SEG, S5_COLS), lambda i: (0, 0))]
    out_shape = [jax.ShapeDtypeStruct((S5_SEG, S5_COLS), F32)]
    if with_output:
        ins += [mats["c_q"], mats["d_row"]]
        in_specs += [full(mats["c_q"]), full(mats["d_row"])]
        out_specs = [pl.BlockSpec((rows, S5_WIDTH), lambda i: (i, 0)),
                     pl.BlockSpec((None, S5_SEG, S5_COLS), lambda i: (i, 0, 0))] + out_specs
        out_shape = [jax.ShapeDtypeStruct((S, S5_WIDTH), F32), jax.ShapeDtypeStruct((nt, S5_SEG, S5_COLS), F32)] + out_shape
    return _call(
        body, name="s5_fwd_y" if with_output else "s5_fwd_ends", grid=(nt,), in_specs=in_specs, out_specs=out_specs,
        out_shape=out_shape,
        scratch_shapes=[pltpu.VMEM((rows, S5_COLS), F32), pltpu.VMEM((S5_SEG, S5_COLS), F32)],
        compiler_params=_params(40),
    )(*ins)


def _s5_bwd_ends(dy_perm, mats):
    S = dy_perm.shape[0]
    rows = S5_TILE_STEPS * S5_SEG
    nt = S // rows

    def body(dy_ref, c_ref, lr_ref, li_ref, gend_ref, gs, gcar):
        @pl.when(pl.program_id(0) == 0)
        def _():
            gcar[...] = jnp.zeros(gcar.shape, F32)

        dyb = dy_ref[...].astype(BF16)
        for q in range(S5_QUADS):
            gs[:, q * S5_QCOLS:(q + 1) * S5_QCOLS] = lax.dot_general(
                dyb[:, q * 128:(q + 1) * 128], c_ref[q], (((1,), (1,)), ((), ())), preferred_element_type=F32)

        def step(k, g):
            t = S5_TILE_STEPS - 1 - k
            sl = pl.ds(pl.multiple_of(t * S5_SEG, S5_SEG), S5_SEG)
            return _s5_advance(g, lr_ref, li_ref, gs, sl, True)

        g = _scan_loop(step, _s5_parts(gcar[...]))
        _s5_store(gcar, slice(None), g)
        _s5_store(gend_ref, slice(None), g)

    full = lambda a: pl.BlockSpec(a.shape, functools.partial(lambda i, nd: (0,) * nd, nd=a.ndim))
    return _call(
        body, name="s5_bwd_ends", grid=(nt,),
        in_specs=[pl.BlockSpec((rows, S5_WIDTH), lambda i: (nt - 1 - i, 0)), full(mats["c_q"]), full(mats["lam_r"]),
                  full(mats["lam_i"])],
        out_specs=pl.BlockSpec((S5_SEG, S5_COLS), lambda i: (0, 0)),
        out_shape=jax.ShapeDtypeStruct((S5_SEG, S5_COLS), F32),
        scratch_shapes=[pltpu.VMEM((rows, S5_COLS), F32), pltpu.VMEM((S5_SEG, S5_COLS), F32)],
        compiler_params=_params(40),
    )(dy_perm, mats["c_q"], mats["lam_r"], mats["lam_i"])


def _s5_bwd_full(u_perm, dy_perm, hinit, g0, mats):
    S = u_perm.shape[0]
    rows = S5_TILE_STEPS * S5_SEG
    nt = S // rows

    def body(u_ref, dy_ref, hinit_ref, g0_ref, b_ref, c_ref, lr_ref, li_ref, d_ref,
             du_ref, dp_ref, dq_ref, db_ref, dc_ref, dd_ref, hs, gs, gcar):
        @pl.when(pl.program_id(0) == 0)
        def _():
            gcar[...] = g0_ref[...]
            for ref in (dp_ref, dq_ref, db_ref, dc_ref, dd_ref):
                ref[...] = jnp.zeros(ref.shape, F32)

        u, dy = u_ref[...], dy_ref[...]
        ub, dyb = u.astype(BF16), dy.astype(BF16)
        hs[0:S5_SEG, :] = hinit_ref[...]
        for q in range(S5_QUADS):
            cols = slice(q * S5_QCOLS, (q + 1) * S5_QCOLS)
            hs[S5_SEG:, cols] = jnp.dot(ub[:, q * 128:(q + 1) * 128], b_ref[q], preferred_element_type=F32)
            gs[:, cols] = lax.dot_general(dyb[:, q * 128:(q + 1) * 128], c_ref[q], (((1,), (1,)), ((), ())),
                                          preferred_element_type=F32)

        def fstep(t, h):
            sl = pl.ds(pl.multiple_of((t + 1) * S5_SEG, S5_SEG), S5_SEG)
            hn = _s5_advance(h, lr_ref, li_ref, hs, sl, False)
            _s5_store(hs, sl, hn)
            return hn

        _scan_loop(fstep, _s5_parts(hinit_ref[...]))

        def bstep(k, g):
            t = S5_TILE_STEPS - 1 - k
            sl = pl.ds(pl.multiple_of(t * S5_SEG, S5_SEG), S5_SEG)
            gn = _s5_advance(g, lr_ref, li_ref, gs, sl, True)
            _s5_store(gs, sl, gn)
            return gn

        _s5_store(gcar, slice(None), _scan_loop(bstep, _s5_parts(gcar[...])))

        half = S5_QCOLS // 2
        dus = []
        for q in range(S5_QUADS):
            cols = slice(q * S5_QCOLS, (q + 1) * S5_QCOLS)

            def astep(t, carry, q=q):
                sl = pl.ds(pl.multiple_of(t * S5_SEG, S5_SEG), S5_SEG)
                g = gs[sl, q * S5_QCOLS:(q + 1) * S5_QCOLS]
                hp = hs[sl, q * S5_QCOLS:(q + 1) * S5_QCOLS]
                hp_sw = jnp.concatenate([hp[:, half:], hp[:, :half]], axis=1)
                return carry[0] + g * hp, carry[1] + g * hp_sw

            zero = jnp.zeros((S5_SEG, S5_QCOLS), F32)
            acc_p, acc_q = _scan_loop(astep, (zero, zero))
            dp_ref[:, cols] += jnp.sum(acc_p, axis=0, keepdims=True)
            dq_ref[:, cols] += jnp.sum(acc_q, axis=0, keepdims=True)
            gq = gs[:, cols].astype(BF16)
            db_ref[q] += lax.dot_general(ub[:, q * 128:(q + 1) * 128], gq, (((0,), (0,)), ((), ())),
                                         preferred_element_type=F32)
            hq = hs[S5_SEG:, cols].astype(BF16)
            dc_ref[q] += lax.dot_general(dyb[:, q * 128:(q + 1) * 128], hq, (((0,), (0,)), ((), ())),
                                         preferred_element_type=F32)
            dus.append(lax.dot_general(gq, b_ref[q], (((1,), (1,)), ((), ())), preferred_element_type=F32))
        du_ref[...] = (jnp.concatenate(dus, axis=1) + d_ref[...] * dy).astype(du_ref.dtype)
        dd_ref[...] += jnp.sum(dy * u, axis=0, keepdims=True)

    full = lambda a: pl.BlockSpec(a.shape, functools.partial(lambda i, nd: (0,) * nd, nd=a.ndim))
    rev_rows = pl.BlockSpec((rows, S5_WIDTH), lambda i: (nt - 1 - i, 0))
    consts = [mats["b_q"], mats["c_q"], mats["lam_r"], mats["lam_i"], mats["d_row"]]
    acc = lambda s: pl.BlockSpec(s, functools.partial(lambda i, nd: (0,) * nd, nd=len(s)))
    acc_shapes = [(1, S5_COLS), (1, S5_COLS), (S5_QUADS, 128, S5_QCOLS), (S5_QUADS, 128, S5_QCOLS), (1, S5_WIDTH)]
    return _call(
        body, name="s5_bwd_full", grid=(nt,),
        in_specs=[rev_rows, rev_rows, pl.BlockSpec((None, S5_SEG, S5_COLS), lambda i: (nt - 1 - i, 0, 0)), full(g0)]
        + [full(a) for a in consts],
        out_specs=[rev_rows] + [acc(s) for s in acc_shapes],
        out_shape=[jax.ShapeDtypeStruct((S, S5_WIDTH), BF16)] + [jax.ShapeDtypeStruct(s, F32) for s in acc_shapes],
        scratch_shapes=[pltpu.VMEM((rows + S5_SEG, S5_COLS), F32), pltpu.VMEM((rows, S5_COLS), F32),
                        pltpu.VMEM((S5_SEG, S5_COLS), F32)],
        compiler_params=_params(56),
    )(u_perm, dy_perm, hinit, g0, *consts)


def _cmul(ar, ai, br, bi):
    return ar * br - ai * bi, ar * bi + ai * br


def _split_cols(v):
    t = v.reshape(v.shape[0], S5_QUADS, 2, S5_QCOLS // 2)
    return t[:, :, 0], t[:, :, 1]


def _join_cols(re, im):
    return jnp.stack([re, im], axis=2).reshape(re.shape[0], S5_COLS)


def _segment_starts(ends, pow_r, pow_i, reverse):
    er, ei = _split_cols(ends)
    pi = -pow_i if reverse else pow_i
    order = list(range(S5_SEG))
    if reverse:
        order = order[::-1]
    cr, ci = jnp.zeros_like(er[0]), jnp.zeros_like(ei[0])
    out_r, out_i = [None] * S5_SEG, [None] * S5_SEG
    for j in order:
        out_r[j], out_i[j] = cr, ci
        mr, mi = _cmul(pow_r, pi, cr, ci)
        cr, ci = mr + er[j], mi + ei[j]
    return _join_cols(jnp.stack(out_r), jnp.stack(out_i))


def _to_segments(a):
    S, w = a.shape
    return a.reshape(S5_SEG, S // S5_SEG, w).transpose(1, 0, 2).reshape(S, w)


def _from_segments(a):
    S, w = a.shape
    return a.reshape(S // S5_SEG, S5_SEG, w).transpose(1, 0, 2).reshape(S, w)


def _my_pos():
    return lax.axis_index("x"), lax.axis_index("y"), lax.axis_index("c")


def _flip(pos, k):
    x, y, c = pos
    return (1 - x if k & 4 else x, 1 - y if k & 2 else y, 1 - c if k & 1 else c)


def _index_of(pos):
    return 4 * pos[0] + 2 * pos[1] + pos[2]


_GATHER_FLIPS = (0, 1, 4, 5, 2, 3, 6, 7)


def _inproj_gather(x, norm_g, pack_a, pack_b, pack_c):
    S = x.shape[0]
    tm = min(S, 1024)
    n_i = S // tm
    order = jnp.stack([_index_of(_flip(_my_pos(), k)) for k in _GATHER_FLIPS]).astype(jnp.int32)

    def body(order_ref, x_ref, g_ref, pa_ref, pb_ref, pc_ref, ut_ref, proj_ref, proj16_ref, oa_ref, ob_ref, oc_ref,
             wv, u_scr, send_sems, recv_sems, local_sems):
        s, i = pl.program_id(0), pl.program_id(1)
        me = _my_pos()
        mine = _index_of(me)
        sibling = _flip(me, 1)
        srcs = (pb_ref, pa_ref, pc_ref)
        dsts = (wv, oa_ref, oc_ref)

        def direct(a, k):
            return pltpu.make_async_remote_copy(
                src_ref=srcs[a], dst_ref=dsts[a].at[mine], send_sem=send_sems.at[a * 8 + k],
                recv_sem=recv_sems.at[a * 8 + k], device_id=_flip(me, k), device_id_type=MESH)

        def passed_on(a, k):
            slot = _index_of(_flip(me, k))
            return pltpu.make_async_remote_copy(
                src_ref=dsts[a].at[slot], dst_ref=dsts[a].at[slot], send_sem=send_sems.at[a * 8 + (k | 1)],
                recv_sem=recv_sems.at[a * 8 + (k | 1)], device_id=sibling, device_id_type=MESH)

        def arrival(a, k):
            slot = _index_of(_flip(me, k))
            pltpu.make_async_remote_copy(
                src_ref=dsts[a].at[slot], dst_ref=dsts[a].at[slot], send_sem=send_sems.at[a * 8 + k],
                recv_sem=recv_sems.at[a * 8 + k], device_id=me, device_id_type=MESH).wait_recv()

        def own_copy(a):
            return pltpu.make_async_copy(srcs[a], dsts[a].at[mine], local_sems.at[a])

        def keep(idx):
            slot = _index_of(_flip(me, _GATHER_FLIPS[idx]))
            return pltpu.make_async_copy(wv.at[slot], ob_ref.at[slot], local_sems.at[3 + idx])


        first = (s == 0) & (i == 0)

        @pl.when(first)
        def _():
            for a in range(3):
                own_copy(a).start()
            for k in (1, 4, 2):
                direct(0, k).start()
            own_copy(0).wait()
            keep(0).start()

        for idx, k in enumerate(_GATHER_FLIPS):
            if idx == 0:
                continue

            @pl.when((s == idx) & (i == 0))
            def _(idx=idx, k=k):
                arrival(0, k)
                if k in (4, 2, 6):
                    passed_on(0, k).start()
                keep(idx).start()
                if idx == 1:
                    direct(0, 6).start()
                if idx == 2:
                    for a in (1, 2):
                        for k in (1, 4, 2, 6):
                            direct(a, k).start()

        @pl.when(s == 0)
        def _():
            y, _, _ = _rms_fwd(x_ref[...], g_ref[...])
            u_scr[pl.ds(pl.multiple_of(i * tm, tm), tm), :] = y.astype(BF16)
            ut_ref[...] = y.T.astype(BF16)

        ub = u_scr[pl.ds(pl.multiple_of(i * tm, tm), tm), :]
        block = jnp.dot(ub, wv[order_ref[s]], preferred_element_type=F32)
        proj_ref[...] = block
        proj16_ref[...] = block.astype(BF16)

        @pl.when((s == N_DEV - 1) & (i == n_i - 1))
        def _():
            for a in (1, 2):
                for k in (4, 2, 6):
                    arrival(a, k)
                    passed_on(a, k).start()
            for a in (1, 2):
                for k in (1, 5, 3, 7):
                    arrival(a, k)
                own_copy(a).wait()
            for a in range(3):
                for k in (1, 4, 2, 6):
                    direct(a, k).wait_send()
                for k in (4, 2, 6):
                    passed_on(a, k).wait_send()
            for idx in range(N_DEV):
                keep(idx).wait()

    any_spec = pl.BlockSpec(memory_space=pl.ANY)
    vmem = pl.BlockSpec(memory_space=pltpu.VMEM)
    grid_spec = pltpu.PrefetchScalarGridSpec(
        num_scalar_prefetch=1, grid=(N_DEV, n_i),
        in_specs=[pl.BlockSpec((tm, D_MODEL), lambda s, i, o: (jnp.where(s == 0, i, 0), 0)),
                  pl.BlockSpec((1, D_MODEL), lambda s, i, o: (0, 0)), any_spec, vmem, any_spec],
        out_specs=[pl.BlockSpec((D_MODEL, tm), lambda s, i, o: (0, jnp.where(s == 0, i, n_i - 1))),
                   pl.BlockSpec((tm, SHARD_IN), lambda s, i, o: (i, o[s])),
                   pl.BlockSpec((tm, SHARD_IN), lambda s, i, o: (i, o[s])), any_spec, any_spec, any_spec],
        scratch_shapes=[pltpu.VMEM((N_DEV,) + pack_b.shape, BF16), pltpu.VMEM((S, D_MODEL), BF16),
                        pltpu.SemaphoreType.DMA((24,)), pltpu.SemaphoreType.DMA((24,)), pltpu.SemaphoreType.DMA((3 + N_DEV,))],
    )
    return _call(
        body, name="inproj_gather", grid_spec=grid_spec,
        out_shape=[jax.ShapeDtypeStruct((D_MODEL, S), BF16), jax.ShapeDtypeStruct((S, IN_COLS), F32),
                   jax.ShapeDtypeStruct((S, IN_COLS), BF16), jax.ShapeDtypeStruct((N_DEV,) + pack_a.shape, BF16), jax.ShapeDtypeStruct((N_DEV,) + pack_b.shape, BF16),
                   jax.ShapeDtypeStruct((N_DEV,) + pack_c.shape, BF16)],
        compiler_params=_params(60, 2),
    )(order, x, norm_g, pack_a, pack_b, pack_c)


_SCATTER_FLIPS = (7, 6, 5, 4, 3, 2, 1, 0)
_N_CHIPS = 4


def _for_row_chunks(n_rows, chunk, fn):
    def step(c, carry):
        fn(pl.ds(pl.multiple_of(c * chunk, chunk), chunk))
        return carry

    lax.fori_loop(0, n_rows // chunk, step, 0)


def _grad_w_in_scatter(dproj, u_t, rs_a, rs_c, small_partial):
    S = u_t.shape[1]
    tm = min(S, 1024)
    n_i = S // tm
    order = jnp.stack([_index_of(_flip(_my_pos(), k)) for k in _SCATTER_FLIPS]).astype(jnp.int32)
    shapes = ((D_MODEL, SHARD_IN), rs_a.shape[1:], rs_c.shape[1:])
    row_chunk = 128

    def body(order_ref, dp_ref, ut_ref, ra_ref, rc_ref, p_ref, gb_ref, ga_ref, gc_ref, gs_ref, acc, sib_b, d2d_b,
             send_b, ici_b, mine_a, sib_a, ici_a, mine_c, sib_c, ici_c, gath, send_sems, recv_sems, local_sems):
        s, i = pl.program_id(0), pl.program_id(1)
        me = _my_pos()
        sibling = _flip(me, 1)

        my_chip = 2 * me[0] + me[1]

        small_d2d = pltpu.make_async_remote_copy(
            src_ref=p_ref, dst_ref=gath.at[_N_CHIPS], send_sem=send_sems.at[21], recv_sem=recv_sems.at[21],
            device_id=sibling, device_id_type=MESH)

        def small_ici(m):
            return pltpu.make_async_remote_copy(
                src_ref=gath.at[my_chip], dst_ref=gath.at[my_chip], send_sem=send_sems.at[22 + m],
                recv_sem=recv_sems.at[22 + m], device_id=_flip(me, 6 - 2 * m), device_id_type=MESH)
        sib = (sib_b, sib_a, sib_c)
        ici = (ici_b, ici_a, ici_c)
        outs = (gb_ref, ga_ref, gc_ref)

        def to_sibling(arr, m, src):
            return pltpu.make_async_remote_copy(
                src_ref=src, dst_ref=sib[arr].at[m], send_sem=send_sems.at[arr * 7 + m],
                recv_sem=recv_sems.at[arr * 7 + m], device_id=sibling, device_id_type=MESH)

        def over_ici(arr, m, src):
            return pltpu.make_async_remote_copy(
                src_ref=src, dst_ref=ici[arr].at[m], send_sem=send_sems.at[arr * 7 + 4 + m],
                recv_sem=recv_sems.at[arr * 7 + 4 + m], device_id=_flip(me, 6 - 2 * m), device_id_type=MESH)

        def from_sibling(arr, m):
            to_sibling(arr, m, sib[arr].at[m]).wait_recv()

        def from_ici(arr, m):
            over_ici(arr, m, ici[arr].at[m]).wait_recv()

        small = ((1, ra_ref, mine_a), (2, rc_ref, mine_c))

        def local_copy(arr, src, mine, m):
            return pltpu.make_async_copy(src.at[_index_of(_flip(me, 6 - 2 * m))], mine.at[m],
                                         local_sems.at[(arr - 1) * _N_CHIPS + m])

        @pl.when((s == 0) & (i == 0))
        def _():
            small_d2d.start()
            for arr, src, mine in small:
                for m in range(_N_CHIPS):
                    to_sibling(arr, m, src.at[_index_of(_flip(me, 7 - 2 * m))]).start()
                    local_copy(arr, src, mine, m).start()

        @pl.when((s == 1) & (i == 0))
        def _():
            small_d2d.wait_recv()
            gath[my_chip] = p_ref[...] + gath[_N_CHIPS]
            for m in range(_N_CHIPS - 1):
                small_ici(m).start()
            for arr, src, mine in small:
                rows, chunk = shapes[arr][0], 16
                for m in range(_N_CHIPS):
                    local_copy(arr, src, mine, m).wait()
                    from_sibling(arr, m)
                    if m < _N_CHIPS - 1:
                        def add(sl, arr=arr, mine=mine, m=m):
                            mine[m, sl, :] = (mine[m, sl, :].astype(F32) + sib[arr][m, sl, :].astype(F32)).astype(BF16)

                        _for_row_chunks(rows, chunk, add)
                        over_ici(arr, m, mine.at[m]).start()
                    else:
                        def keep(sl, arr=arr, mine=mine, m=m):
                            outs[arr][sl, :] = mine[m, sl, :].astype(F32) + sib[arr][m, sl, :].astype(F32)

                        _for_row_chunks(rows, chunk, keep)

        @pl.when(i == 0)
        def _():
            acc[...] = jnp.zeros(acc.shape, F32)

        acc[...] += jnp.dot(ut_ref[...], dp_ref[...], preferred_element_type=F32)

        def block_rows(c):
            return acc[c * row_chunk:(c + 1) * row_chunk, :]

        for m in range(_N_CHIPS):
            @pl.when((s == 2 * m) & (i == n_i - 1))
            def _(m=m):
                if m > 0:
                    to_sibling(0, m - 1, d2d_b).wait_send()
                for c in range(D_MODEL // row_chunk):
                    d2d_b[c * row_chunk:(c + 1) * row_chunk, :] = block_rows(c).astype(BF16)
                to_sibling(0, m, d2d_b).start()

            @pl.when((s == 2 * m + 1) & (i == n_i - 1))
            def _(m=m):
                from_sibling(0, m)
                slot = m % 2
                if m == 2:
                    over_ici(0, 0, send_b.at[0]).wait_send()
                for c in range(D_MODEL // row_chunk):
                    rows = slice(c * row_chunk, (c + 1) * row_chunk)
                    total = block_rows(c) + sib_b[m, rows, :].astype(F32)
                    if m < _N_CHIPS - 1:
                        send_b[slot, rows, :] = total.astype(BF16)
                    else:
                        gb_ref[rows, :] = total
                if m < _N_CHIPS - 1:
                    over_ici(0, m, send_b.at[slot]).start()

        @pl.when((s == N_DEV - 1) & (i == n_i - 1))
        def _():
            for arr in range(3):
                for m in range(_N_CHIPS - 1):
                    from_ici(arr, m)
                rows = shapes[arr][0]

                def add(sl, arr=arr):
                    outs[arr][sl, :] = (outs[arr][sl, :] + ici[arr][0, sl, :].astype(F32)
                                        + ici[arr][1, sl, :].astype(F32) + ici[arr][2, sl, :].astype(F32))

                _for_row_chunks(rows, 16, add)
            for m in range(_N_CHIPS - 1):
                small_ici(m).wait_recv()
            gs_ref[...] = (gath[0] + gath[1]) + (gath[2] + gath[3])
            small_d2d.wait_send()
            for m in range(_N_CHIPS - 1):
                small_ici(m).wait_send()
            to_sibling(0, _N_CHIPS - 1, d2d_b).wait_send()
            over_ici(0, 1, send_b.at[1]).wait_send()
            over_ici(0, 2, send_b.at[0]).wait_send()
            for arr, src, mine in small:
                for m in range(_N_CHIPS):
                    to_sibling(arr, m, src.at[0]).wait_send()
                for m in range(_N_CHIPS - 1):
                    over_ici(arr, m, mine.at[m]).wait_send()

    any_spec = pl.BlockSpec(memory_space=pl.ANY)
    vmem = pl.BlockSpec(memory_space=pltpu.VMEM)
    half = lambda shp, n: pltpu.VMEM((n,) + tuple(shp), BF16)
    grid_spec = pltpu.PrefetchScalarGridSpec(
        num_scalar_prefetch=1, grid=(N_DEV, n_i),
        in_specs=[pl.BlockSpec((tm, SHARD_IN), lambda s, i, o: (i, o[s])),
                  pl.BlockSpec((D_MODEL, tm), lambda s, i, o: (0, i)), any_spec, any_spec, vmem],
        out_specs=[vmem, vmem, vmem, vmem],
        scratch_shapes=[
            pltpu.VMEM((D_MODEL, SHARD_IN), F32), half(shapes[0], _N_CHIPS), pltpu.VMEM(shapes[0], BF16),
            half(shapes[0], 2), half(shapes[0], _N_CHIPS - 1),
            half(shapes[1], _N_CHIPS), half(shapes[1], _N_CHIPS), half(shapes[1], _N_CHIPS - 1),
            half(shapes[2], _N_CHIPS), half(shapes[2], _N_CHIPS), half(shapes[2], _N_CHIPS - 1),
            pltpu.VMEM((_N_CHIPS + 1,) + small_partial.shape, F32),
            pltpu.SemaphoreType.DMA((25,)), pltpu.SemaphoreType.DMA((25,)), pltpu.SemaphoreType.DMA((2 * _N_CHIPS,))],
    )
    return _call(
        body, name="grad_w_in_scatter", grid_spec=grid_spec,
        out_shape=[jax.ShapeDtypeStruct(shp, F32) for shp in shapes] + [jax.ShapeDtypeStruct(small_partial.shape, F32)],
        compiler_params=_params(60, 2),
    )(order, dproj, u_t, rs_a, rs_c, small_partial)


def _adam_update(g, w, m, v):
    m2 = ADAM_B1 * m + (1.0 - ADAM_B1) * g
    v2 = ADAM_B2 * v + (1.0 - ADAM_B2) * (g * g)
    m_hat = m2 / (1.0 - ADAM_B1 ** ADAM_STEP)
    v_hat = v2 / (1.0 - ADAM_B2 ** ADAM_STEP)
    delta = -ADAM_LR * (m_hat / (jnp.sqrt(v_hat) + ADAM_EPS) + ADAM_WD * w)
    return delta, m2, v2


def _adam_rows(g, w, m, v):
    rows, cols = w.shape
    tm = rows if rows % 256 else 256

    def fn(rv, cr, out):
        return list(_adam_update(*rv)), []

    outs, _ = _rowwise("adamw", fn, rows, tm, [(a, cols, 0) for a in (g, w, m, v)], [], [(cols, F32)] * 3, [], 32)
    return outs


_SMALL = ["norm_g", "hg_lb", "hg_norm_g", "s5_a_re", "s5_a_im", "s5_log_dt", "s5_b_re", "s5_b_im", "s5_c_re",
          "s5_c_im", "s5_d", "b_glu", "ple_norm_g", "final_norm_g"]
_BIG = ["w_in", "w_o_hg", "w_glu", "w_o_s5", "w_out", "w_ple", "w_ple_gate"]
_ORDER = ["norm_g", "w_in", "hg_lb", "hg_norm_g", "w_o_hg", "s5_a_re", "s5_a_im", "s5_log_dt", "s5_b_re", "s5_b_im",
          "s5_c_re", "s5_c_im", "s5_d", "w_glu", "b_glu", "w_o_s5", "w_out", "ple_norm_g", "w_ple", "w_ple_gate",
          "final_norm_g"]


def _pack_small(vals, tail=None):
    parts = []
    for name in _SMALL:
        flat = vals[name].reshape(-1).astype(F32)
        pad = (-flat.shape[0]) % 1024
        parts.append(jnp.pad(flat, (0, pad)))
    tail = jnp.zeros((0,), F32) if tail is None else tail.reshape(-1).astype(F32)
    parts.append(jnp.pad(tail, (0, 1024 - tail.shape[0])))
    return jnp.concatenate(parts).reshape(-1, 128)


def _unpack_small(packed, like):
    flat = packed.reshape(-1)
    out, off = {}, 0
    for name in _SMALL:
        size = like[name].size
        out[name] = flat[off:off + size].reshape(like[name].shape)
        off += size + (-size) % 1024
    return out


def _col_blocks(full):
    k = full.shape[0]
    return full.reshape(k, N_DEV, 128).transpose(1, 0, 2)


def _from_col_blocks(blocks):
    k = blocks.shape[1]
    return blocks.transpose(1, 0, 2).reshape(k, N_DEV * 128)


def kernel(x, p, norm_g, w_in, hg_lb, hg_norm_g, w_o_hg, s5_a_re, s5_a_im, s5_log_dt, s5_b_re, s5_b_im, s5_c_re, s5_c_im, s5_d, w_glu, b_glu, w_o_s5, w_out, ple_norm_g, w_ple, w_ple_gate, final_norm_g, loss_target, m_norm_g, m_w_in, m_hg_lb, m_hg_norm_g, m_w_o_hg, m_s5_a_re, m_s5_a_im, m_s5_log_dt, m_s5_b_re, m_s5_b_im, m_s5_c_re, m_s5_c_im, m_s5_d, m_w_glu, m_b_glu, m_w_o_s5, m_w_out, m_ple_norm_g, m_w_ple, m_w_ple_gate, m_final_norm_g, v_norm_g, v_w_in, v_hg_lb, v_hg_norm_g, v_w_o_hg, v_s5_a_re, v_s5_a_im, v_s5_log_dt, v_s5_b_re, v_s5_b_im, v_s5_c_re, v_s5_c_im, v_s5_d, v_w_glu, v_b_glu, v_w_o_s5, v_w_out, v_ple_norm_g, v_w_ple, v_w_ple_gate, v_final_norm_g):
    args = dict(locals())
    w = {n: args[n] for n in _ORDER}
    m = {n: args["m_" + n] for n in _ORDER}
    v = {n: args["v_" + n] for n in _ORDER}
    xs = x[0]
    ps = p[0, 0]
    tgt = loss_target[0]
    S = xs.shape[0]

    pack_a = jnp.concatenate([w_o_hg[0], w_out[0], w_ple_gate[0]], axis=0).astype(BF16)
    pack_b = w_in[0].astype(BF16)
    pack_c = jnp.concatenate([w_glu[0], w_o_s5[0], w_ple[0]], axis=0).astype(BF16)
    u_t, proj, proj16, all_a, all_b, all_c = _inproj_gather(xs, norm_g, pack_a, pack_b, pack_c)
    wf_o_hg = all_a[:, 0:128].reshape(D_MODEL, D_MODEL)
    wf_out = all_a[:, 128:256].reshape(D_MODEL, D_MODEL)
    wf_pg = all_a[:, 256:384].reshape(D_MODEL, D_MODEL)
    wf_glu = _from_col_blocks(all_c[:, 0:512])
    wf_o_s5 = _from_col_blocks(all_c[:, 512:1024])
    wf_ple = _from_col_blocks(all_c[:, 1024:1280])

    lb = jax.nn.sigmoid(hg_lb[0:1] - hg_lb[1:2])
    s5_names = ["s5_a_re", "s5_a_im", "s5_log_dt", "s5_b_re", "s5_b_im", "s5_c_re", "s5_c_im", "s5_d"]
    build = lambda *a: _s5_matrices(*a, seg_len=S // S5_SEG)
    mats_f32, mats_vjp = jax.vjp(build, *[w[n][0] for n in s5_names])
    mats = dict(mats_f32, b_q=mats_f32["b_q"].astype(BF16), c_q=mats_f32["c_q"].astype(BF16))
    bias_glu = b_glu

    o, states = _hgrn_fwd(proj, lb)
    u_perm = _to_segments(proj[:, COL_US:COL_US + S5_WIDTH])
    zeros_state = jnp.zeros((S5_SEG, S5_COLS), F32)
    (h_ends,) = _s5_fwd_pass(u_perm, mats, zeros_state, False)
    h0 = _segment_starts(h_ends, mats["pow_r"], mats["pow_i"], False)
    y_perm, h_init, _ = _s5_fwd_pass(u_perm, mats, h0, True)
    ys = _from_segments(y_perm)
    y_hg, y_s5, glu, h1 = _stage_branches(o, proj16, ys, xs, hg_norm_g, wf_o_hg, wf_glu, bias_glu, wf_o_s5, wf_out)

    dh1, (loss_acc, d_final_g, d_ple_g, d_w_ple, d_w_pg) = _stage_ple_loss(
        h1, ps, tgt, ple_norm_g, wf_ple, wf_pg, final_norm_g.reshape(1, D_MODEL))
    (d_gate_hg, d_gate_s5, d_yhg, d_ys5), (d_w_out,) = _stage_bwd_merge(dh1, y_hg, y_s5, proj16, wf_out)
    (d_o, d_g_hg), (d_w_o_hg, d_hg_norm) = _stage_bwd_hg_path(d_yhg, o, proj16, hg_norm_g, wf_o_hg)
    (d_ys, d_z), (d_w_o_s5, d_w_glu, d_b_glu) = _stage_bwd_s5_path(d_ys5, ys, glu, proj16, wf_o_s5, wf_glu)
    dq, df, div, d_lb = _hgrn_bwd(proj, lb, d_o, states)
    dy_perm = _to_segments(d_ys)
    g_ends = _s5_bwd_ends(dy_perm, mats)
    g0 = _segment_starts(g_ends, mats["pow_r"], mats["pow_i"], True)
    du_perm, acc_p, acc_q, d_bq, d_cq_t, d_d = _s5_bwd_full(u_perm, dy_perm, h_init, g0, mats)
    d_us = _from_segments(du_perm)
    grad_x, dproj, d_norm_g = _stage_inproj_bwd([dq, df, div, d_g_hg, d_us, d_z, d_gate_hg, d_gate_s5], xs, dh1,
                                                norm_g, all_b)

    p_re, p_im = _split_cols(acc_p)
    q_re, q_im = _split_cols(acc_q)
    d_lam_r = (p_re + p_im)[0]
    d_lam_i = (q_im - q_re)[0]
    zero_row = jnp.zeros((S5_SEG, S5_COLS), F32)
    row_of = lambda re_part, im_part: zero_row.at[0].set(_join_cols(re_part[None], im_part[None])[0])
    zeros_q = jnp.zeros_like(d_lam_r)
    cot = dict(
        lam_r=row_of(d_lam_r, zeros_q), lam_i=row_of(zeros_q, d_lam_i),
        b_q=d_bq, c_q=d_cq_t.transpose(0, 2, 1), d_row=d_d,
        pow_r=jnp.zeros_like(mats["pow_r"]), pow_i=jnp.zeros_like(mats["pow_i"]),
    )
    d_s5 = mats_vjp(cot)

    s_lb = lb * (1.0 - lb)
    d_hg_lb = jnp.concatenate([d_lb * s_lb, -d_lb * s_lb], axis=0)
    small_g = dict(norm_g=d_norm_g, hg_lb=d_hg_lb, hg_norm_g=d_hg_norm, b_glu=d_b_glu, ple_norm_g=d_ple_g,
                   final_norm_g=d_final_g)
    for name, g in zip(s5_names, d_s5):
        small_g[name] = g
    pk = lambda d: _pack_small({n: d[n] for n in _SMALL})
    rs_a = jnp.concatenate([d_w_o_hg.reshape(N_DEV, 128, D_MODEL), d_w_out.reshape(N_DEV, 128, D_MODEL),
                            d_w_pg.reshape(N_DEV, 128, D_MODEL)], axis=1).astype(BF16)
    rs_c = jnp.concatenate([_col_blocks(d_w_glu), _col_blocks(d_w_o_s5), _col_blocks(d_w_ple)], axis=1).astype(BF16)
    partial = _pack_small({n: small_g[n] for n in _SMALL}, tail=loss_acc[0, 0:1])
    g_b, g_a, g_c, sg = _grad_w_in_scatter(dproj, u_t, rs_a, rs_c, partial)
    sd, sm, sv = _adam_rows(sg, pk(w), pk(m), pk(v))
    like = {n: w[n] for n in _SMALL}
    out_g, out_d, out_m, out_v = (_unpack_small(t, like) for t in (sg, sd, sm, sv))
    big_g = dict(w_o_hg=g_a[0:128], w_out=g_a[128:256], w_ple_gate=g_a[256:384], w_in=g_b,
                 w_glu=g_c[0:512], w_o_s5=g_c[512:1024], w_ple=g_c[1024:1280])
    for name in _BIG:
        shape = w[name].shape
        g2 = big_g[name]
        d2, m2, v2 = _adam_rows(g2, w[name][0], m[name][0], v[name][0])
        out_g[name], out_d[name], out_m[name], out_v[name] = (t.reshape(shape) for t in (g2, d2, m2, v2))

    loss = sg[sg.shape[0] - 8, 0]
    return (loss, grad_x[None], *[out_g[n] for n in _ORDER], *[out_d[n] for n in _ORDER],
            *[out_m[n] for n in _ORDER], *[out_v[n] for n in _ORDER])
```

```python
import functools
import math

import jax
import jax.numpy as jnp
from jax import lax
from jax.experimental import pallas as pl
from jax.experimental.pallas import tpu as pltpu

F32 = jnp.float32
BF16 = jnp.bfloat16

D_MODEL = 1024
N_DEV = 8
IN_COLS = 7168
SHARD_IN = IN_COLS // N_DEV
HG_HEADS = 8
HG_DIM = 128
HG_CHUNK = 64
HG_SUPER_FWD = 256
HG_SUPER_BWD = 128
HG_HEADS_PER_STEP = 8
S5_WIDTH = 512
S5_GROUPS = 32
S5_STATE = 64
S5_CH = 16
S5_SEG = 8
S5_QUADS = 4
S5_QCOLS = 1024
S5_COLS = S5_QUADS * S5_QCOLS
S5_TILE_STEPS = 64
S5_UNROLL = 8
NORM_EPS = 1e-6
ADAM_LR = 0.001
ADAM_B1 = 0.9
ADAM_B2 = 0.999
ADAM_EPS = 1e-08
ADAM_WD = 0.01
ADAM_STEP = 10
MIB = 1024 * 1024
MESH = pl.DeviceIdType.MESH

COL_Q, COL_F, COL_I, COL_G, COL_US, COL_ZS, COL_GH, COL_GS = 0, 1024, 2048, 3072, 4096, 4608, 5120, 6144


def _call(body, **kw):
    return pl.pallas_call(body, **kw)


def _params(vmem_mb, n_grid=1):
    return pltpu.CompilerParams(
        dimension_semantics=("arbitrary",) * n_grid, vmem_limit_bytes=vmem_mb * MIB
    )


def _bdot(a, b):
    return jnp.dot(a.astype(BF16), b.astype(BF16), preferred_element_type=F32)


def _bdot_nt(a, b):
    return lax.dot_general(a.astype(BF16), b.astype(BF16), (((1,), (1,)), ((), ())), preferred_element_type=F32)


def _bdot_tn(a, b):
    return lax.dot_general(a.astype(BF16), b.astype(BF16), (((0,), (0,)), ((), ())), preferred_element_type=F32)


def _sigmoid(x):
    return jax.nn.sigmoid(x)


def _silu(x):
    return x * _sigmoid(x)


def _dsilu(x):
    s = _sigmoid(x)
    return s * (1.0 + x * (1.0 - s))


_GELU_C = math.sqrt(2.0 / math.pi)


def _gelu(x):
    return 0.5 * x * (1.0 + jnp.tanh(_GELU_C * (x + 0.044715 * x * x * x)))


def _dgelu(x):
    t = jnp.tanh(_GELU_C * (x + 0.044715 * x * x * x))
    return 0.5 * (1.0 + t) + 0.5 * x * (1.0 - t * t) * _GELU_C * (1.0 + 3.0 * 0.044715 * x * x)


def _rms_fwd(x, g):
    r = lax.rsqrt(jnp.mean(x * x, axis=-1, keepdims=True) + NORM_EPS)
    n = x * r
    return n * g, n, r


def _rms_bwd(dy, n, r, g):
    dn = dy * g
    dx = r * (dn - n * jnp.mean(dn * n, axis=-1, keepdims=True))
    return dx, jnp.sum(dy * n, axis=0, keepdims=True)


def _head_rms_fwd(o, g):
    ns, rs = [], []
    for h in range(HG_HEADS):
        oh = o[:, h * HG_DIM:(h + 1) * HG_DIM]
        r = lax.rsqrt(jnp.mean(oh * oh, axis=-1, keepdims=True) + NORM_EPS)
        ns.append(oh * r)
        rs.append(r)
    n = jnp.concatenate(ns, axis=1)
    return n * g, n, rs


def _head_rms_bwd(dy, n, rs, g):
    dn = dy * g
    dxs = []
    for h in range(HG_HEADS):
        sl = slice(h * HG_DIM, (h + 1) * HG_DIM)
        dxs.append(rs[h] * (dn[:, sl] - n[:, sl] * jnp.mean(dn[:, sl] * n[:, sl], axis=-1, keepdims=True)))
    return jnp.concatenate(dxs, axis=1), jnp.sum(dy * n, axis=0, keepdims=True)


def _rowwise(name, fn, n_rows, tm, rows, consts, out_rows, out_accs, vmem_mb, parts=1):
    n_r, n_c, n_or, n_oa = len(rows), len(consts), len(out_rows), len(out_accs)
    tp = tm // parts

    def body(*refs):
        r_refs = refs[:n_r]
        c_refs = refs[n_r:n_r + n_c]
        or_refs = refs[n_r + n_c:n_r + n_c + n_or]
        oa_refs = refs[n_r + n_c + n_or:]

        if n_oa:
            @pl.when(pl.program_id(0) == 0)
            def _():
                for ref in oa_refs:
                    ref[...] = jnp.zeros(ref.shape, ref.dtype)

        for part in range(parts):
            sl = slice(part * tp, (part + 1) * tp)
            outs, accs = fn([r[sl, :] for r in r_refs], c_refs, [o.at[sl, :] for o in or_refs])
            for ref, v in zip(or_refs, outs):
                if v is not None:
                    ref[sl, :] = v.astype(ref.dtype)
            for ref, v in zip(oa_refs, accs):
                ref[...] += v.astype(ref.dtype)

    in_specs = [pl.BlockSpec((tm, w), functools.partial(lambda i, c: (i, c), c=cb)) for (_, w, cb) in rows]
    in_specs += [pl.BlockSpec(c.shape, functools.partial(lambda i, nd: (0,) * nd, nd=c.ndim),
                              pipeline_mode=pl.Buffered(1)) for c in consts]
    out_specs = [pl.BlockSpec((tm, w), lambda i: (i, 0)) for (w, _) in out_rows]
    out_specs += [pl.BlockSpec(s, functools.partial(lambda i, nd: (0,) * nd, nd=len(s))) for (s, _) in out_accs]
    out_shape = [jax.ShapeDtypeStruct((n_rows, w), dt) for (w, dt) in out_rows]
    out_shape += [jax.ShapeDtypeStruct(s, dt) for (s, dt) in out_accs]
    res = _call(
        body, name=name, grid=(n_rows // tm,), in_specs=in_specs, out_specs=out_specs, out_shape=out_shape,
        compiler_params=_params(vmem_mb),
    )(*[a for (a, _, _) in rows], *consts)
    return res[:n_or], res[n_or:]


def _stage_branches(o, proj, ys, x, hg_norm_g, w_o_hg, w_glu, b_glu, w_o_s5, w_out):
    S = x.shape[0]

    def fn(rv, cr, out):
        o_b, ys_b, x_b = rv[0], rv[5], rv[6]
        g_hg, z_s, gate_hg, gate_s5 = (v.astype(F32) for v in rv[1:5])
        gn_ref, wohg_ref, wglu_ref, bglu_ref, wos5_ref, wout_ref = cr
        on, _, _ = _head_rms_fwd(o_b, gn_ref[...])
        a = on * _silu(g_hg)
        y_hg = jnp.dot(a.astype(BF16), wohg_ref[...], preferred_element_type=F32)
        gl = _gelu(ys_b)
        glu = jnp.dot(gl.astype(BF16), wglu_ref[...], preferred_element_type=F32) + bglu_ref[...]
        ys2 = glu[:, :S5_WIDTH] * _sigmoid(glu[:, S5_WIDTH:]) * _silu(z_s)
        y_s5 = jnp.dot(ys2.astype(BF16), wos5_ref[...], preferred_element_type=F32)
        merged = _sigmoid(gate_hg) * y_hg + _sigmoid(gate_s5) * y_s5
        h1 = x_b + jnp.dot(merged.astype(BF16), wout_ref[...], preferred_element_type=F32)
        return [y_hg, y_s5, glu, h1], []

    rows = [(o, D_MODEL, 0), (proj, D_MODEL, COL_G // D_MODEL), (proj, S5_WIDTH, COL_ZS // S5_WIDTH),
            (proj, D_MODEL, COL_GH // D_MODEL), (proj, D_MODEL, COL_GS // D_MODEL), (ys, S5_WIDTH, 0), (x, D_MODEL, 0)]
    (y_hg, y_s5, glu, h1), _ = _rowwise(
        "branches", fn, S, 256, rows, [hg_norm_g, w_o_hg, w_glu, b_glu, w_o_s5, w_out],
        [(D_MODEL, BF16), (D_MODEL, BF16), (D_MODEL, BF16), (D_MODEL, F32)], [], 56)
    return y_hg, y_s5, glu, h1


def _stage_ple_loss(h1, p, target, ple_norm_g, w_ple, w_ple_gate, final_norm_g):
    S = h1.shape[0]

    def fn(rv, cr, out):
        h1_b, p_b, t_b = rv
        gp_ref, wple_ref, wpg_ref, gf_ref = cr
        n2g, n2, r2 = _rms_fwd(h1_b, gp_ref[...])
        z = jnp.dot(n2g.astype(BF16), wpg_ref[...], preferred_element_type=F32)
        gate = _sigmoid(z)
        pe = jnp.dot(p_b.astype(BF16), wple_ref[...], preferred_element_type=F32)
        h2 = h1_b + pe * gate
        y, nf, rf = _rms_fwd(h2, gf_ref[...])
        err = y - t_b
        loss_rows = 0.5 * jnp.mean(err * err, axis=-1, keepdims=True)
        loss_inc = jnp.broadcast_to(jnp.sum(loss_rows, axis=0, keepdims=True), (1, 128))
        dy = err * (1.0 / D_MODEL)
        dh2, d_gf = _rms_bwd(dy, nf, rf, gf_ref[...])
        d_pe = dh2 * gate
        dz = dh2 * pe * gate * (1.0 - gate)
        d_wple = _bdot_tn(p_b, d_pe)
        d_wpg = _bdot_tn(n2g, dz)
        dn2g = _bdot_nt(dz, wpg_ref[...])
        dh1n, d_gp = _rms_bwd(dn2g, n2, r2, gp_ref[...])
        dh1 = dh2 + dh1n
        return [dh1, dh1], [loss_inc, d_gf, d_gp, d_wple, d_wpg]

    (dh1, dh1_16), accs = _rowwise(
        "ple_loss", fn, S, 512, [(h1, D_MODEL, 0), (p, 256, 0), (target, D_MODEL, 0)],
        [ple_norm_g, w_ple, w_ple_gate, final_norm_g], [(D_MODEL, F32), (D_MODEL, BF16)],
        [((1, 128), F32), ((1, D_MODEL), F32), ((1, D_MODEL), F32), ((256, D_MODEL), F32), ((D_MODEL, D_MODEL), F32)], 56,
        parts=2)
    return dh1, dh1_16, accs


def _stage_bwd_merge(dh1, y_hg, y_s5, proj, w_out):
    S = dh1.shape[0]

    def fn(rv, cr, out):
        dh1_b, yhg, ys5 = rv[:3]
        gate_hg, gate_s5 = rv[3].astype(F32), rv[4].astype(F32)
        (wout_ref,) = cr
        sg_h, sg_s = _sigmoid(gate_hg), _sigmoid(gate_s5)
        merged = sg_h * yhg + sg_s * ys5
        d_wout = _bdot_tn(merged, dh1_b)
        d_merged = _bdot_nt(dh1_b, wout_ref[...])
        d_gate_hg = d_merged * yhg * sg_h * (1.0 - sg_h)
        d_gate_s5 = d_merged * ys5 * sg_s * (1.0 - sg_s)
        return [d_gate_hg, d_gate_s5, d_merged * sg_h, d_merged * sg_s], [d_wout]

    rows = [(dh1, D_MODEL, 0), (y_hg, D_MODEL, 0), (y_s5, D_MODEL, 0), (proj, D_MODEL, COL_GH // D_MODEL),
            (proj, D_MODEL, COL_GS // D_MODEL)]
    outs, accs = _rowwise("bwd_merge", fn, S, 512, rows, [w_out], [(D_MODEL, BF16)] * 4,
                          [((D_MODEL, D_MODEL), F32)], 56, parts=2)
    return outs, accs


def _stage_bwd_hg_path(d_yhg, o, proj, hg_norm_g, w_o_hg):
    S = o.shape[0]

    def fn(rv, cr, out):
        d_yhg_b, o_b, g_hg = rv[0], rv[1], rv[2].astype(F32)
        gn_ref, wohg_ref = cr
        ong, on, rs = _head_rms_fwd(o_b, gn_ref[...])
        sil = _silu(g_hg)
        d_wohg = _bdot_tn(ong * sil, d_yhg_b)
        d_a = _bdot_nt(d_yhg_b, wohg_ref[...])
        d_g_hg = d_a * ong * _dsilu(g_hg)
        d_o, d_gn = _head_rms_bwd(d_a * sil, on, rs, gn_ref[...])
        return [d_o, d_g_hg], [d_wohg, d_gn]

    rows = [(d_yhg, D_MODEL, 0), (o, D_MODEL, 0), (proj, D_MODEL, COL_G // D_MODEL)]
    outs, accs = _rowwise("bwd_hg_path", fn, S, 512, rows, [hg_norm_g, w_o_hg], [(D_MODEL, BF16)] * 2,
                          [((D_MODEL, D_MODEL), F32), ((1, D_MODEL), F32)], 56, parts=2)
    return outs, accs


def _stage_bwd_s5_path(d_ys5, ys, glu, proj, w_o_s5, w_glu):
    S = ys.shape[0]

    def fn(rv, cr, out):
        d_ys5_b, ys_b, glu_b, z_s = rv[0], rv[1], rv[2].astype(F32), rv[3].astype(F32)
        wos5_ref, wglu_ref = cr
        ga, gb = glu_b[:, :S5_WIDTH], glu_b[:, S5_WIDTH:]
        sgb, silz = _sigmoid(gb), _silu(z_s)
        ys2 = ga * sgb * silz
        d_wos5 = _bdot_tn(ys2, d_ys5_b)
        d_ys2 = _bdot_nt(d_ys5_b, wos5_ref[...])
        d_ga = d_ys2 * sgb * silz
        d_gb = d_ys2 * ga * sgb * (1.0 - sgb) * silz
        d_z = d_ys2 * ga * sgb * _dsilu(z_s)
        d_glu = jnp.concatenate([d_ga, d_gb], axis=1)
        gl = _gelu(ys_b)
        d_wglu = _bdot_tn(gl, d_glu)
        d_bglu = jnp.sum(d_glu, axis=0, keepdims=True)
        d_gl = _bdot_nt(d_glu, wglu_ref[...])
        return [d_gl * _dgelu(ys_b), d_z], [d_wos5, d_wglu, d_bglu]

    rows = [(d_ys5, D_MODEL, 0), (ys, S5_WIDTH, 0), (glu, D_MODEL, 0), (proj, S5_WIDTH, COL_ZS // S5_WIDTH)]
    outs, accs = _rowwise(
        "bwd_s5_path", fn, S, 512, rows, [w_o_s5, w_glu], [(S5_WIDTH, F32), (S5_WIDTH, BF16)],
        [((S5_WIDTH, D_MODEL), F32), ((S5_WIDTH, D_MODEL), F32), ((1, D_MODEL), F32)], 48, parts=2)
    return outs, accs


def _stage_inproj_bwd(pieces, x, dh1, norm_g, w_in_all):
    S = x.shape[0]

    def fn(rv, cr, out):
        g_ref, w_ref = cr
        x_b, dh1_b = rv[8], rv[9]
        dproj_ref = out[1]
        col = 0
        for v in rv[:8]:
            dproj_ref[:, col:col + v.shape[1]] = v.astype(BF16)
            col += v.shape[1]
        d_u = jnp.zeros((x_b.shape[0], D_MODEL), F32)
        for j in range(N_DEV):
            d_u = d_u + lax.dot_general(dproj_ref[:, j * SHARD_IN:(j + 1) * SHARD_IN], w_ref[j],
                                        (((1,), (1,)), ((), ())), preferred_element_type=F32)
        _, n, r = _rms_fwd(x_b, g_ref[...])
        dx, d_g = _rms_bwd(d_u, n, r, g_ref[...])
        return [dh1_b + dx, None], [d_g]

    rows = [(a, a.shape[1], 0) for a in pieces] + [(x, D_MODEL, 0), (dh1, D_MODEL, 0)]
    (grad_x, dproj), (d_g,) = _rowwise(
        "inproj_bwd", fn, S, 256, rows, [norm_g, w_in_all], [(D_MODEL, F32), (IN_COLS, BF16)],
        [((1, D_MODEL), F32)], 56)
    return grad_x, dproj, d_g


def _chunk_row(shape):
    return lax.broadcasted_iota(jnp.int32, shape, 0) & (HG_CHUNK - 1)


def _chunk_cumsum(x):
    r_in = _chunk_row(x.shape)
    s = 1
    while s < HG_CHUNK:
        x = x + jnp.where(r_in >= s, pltpu.roll(x, s, 0), 0.0)
        s *= 2
    return x


def _chunk_suffix_sum(x):
    n = x.shape[0]
    r_in = _chunk_row(x.shape)
    s = 1
    while s < HG_CHUNK:
        x = x + jnp.where(r_in < HG_CHUNK - s, pltpu.roll(x, n - s, 0), 0.0)
        s *= 2
    return x


def _hgrn_prep(q, fl, lb):
    sup = q.shape[0]
    nc = sup // HG_CHUNK
    sig = _sigmoid(fl)
    f = lb + (1.0 - lb) * sig
    k = (1.0 - lb) * (1.0 - sig)
    b = _chunk_cumsum(jnp.log(f))
    b3 = b.reshape(nc, HG_CHUNK, HG_DIM)
    row3 = lax.broadcasted_iota(jnp.int32, b3.shape, 1)
    pick = lambda r: jnp.sum(jnp.where(row3 == r, b3, 0.0), axis=1, keepdims=True)
    b_mid = pick(HG_CHUNK // 2 - 1)
    b_last = pick(HG_CHUNK - 1)
    flat = lambda t: t.reshape(sup, HG_DIM)
    e_qa = flat(jnp.exp(b3 - b_mid))
    e_ka = flat(jnp.exp(b_mid - b3))
    e_qd = jnp.exp(b)
    e_kd = flat(jnp.exp(b_last - b3))
    dc = jnp.exp(b_last)
    return sig, f, k, e_qa, e_ka, e_qd, e_kd, dc


def _hgrn_mask(sup):
    r = lax.broadcasted_iota(jnp.int32, (sup, sup), 0)
    c = lax.broadcasted_iota(jnp.int32, (sup, sup), 1)
    shift = HG_CHUNK.bit_length() - 1
    return (jnp.right_shift(r, shift) == jnp.right_shift(c, shift)) & (r >= c)


def _hgrn_fwd(proj, lb):
    S = proj.shape[0]
    sup = HG_SUPER_FWD
    nb = S // sup
    nc = sup // HG_CHUNK
    hp = HG_HEADS_PER_STEP
    wide = hp * HG_DIM

    def body(q_ref, f_ref, iv_ref, lb_ref, o_ref, st_ref, state):
        @pl.when(pl.program_id(1) == 0)
        def _():
            state[...] = jnp.zeros(state.shape, F32)

        mask = _hgrn_mask(sup)
        for hh in range(hp):
            lanes = slice(hh * HG_DIM, (hh + 1) * HG_DIM)
            q, iv = q_ref[:, lanes], iv_ref[:, lanes]
            _, _, k, e_qa, e_ka, e_qd, e_kd, dc = _hgrn_prep(q, f_ref[:, lanes], lb_ref[:, lanes])
            scores = jnp.where(mask, _bdot_nt(q * e_qa, k * e_ka), 0.0)
            o_intra = _bdot(scores, iv)
            qd, kd = q * e_qd, k * e_kd
            for c in range(nc):
                sl = slice(c * HG_CHUNK, (c + 1) * HG_CHUNK)
                st = state[hh]
                st_ref[hh, c] = st
                o_ref[sl, lanes] = o_intra[sl] + _bdot_nt(qd[sl], st)
                state[hh] = dc[c] * st + _bdot_tn(iv[sl], kd[sl])

    blk = lambda base: pl.BlockSpec((sup, wide), functools.partial(lambda h, i, b: (i, b + h), b=base // wide))
    return _call(
        body, name="hgrn_fwd", grid=(HG_HEADS // hp, nb),
        in_specs=[blk(COL_Q), blk(COL_F), blk(COL_I), pl.BlockSpec((1, wide), lambda h, i: (0, h))],
        out_specs=[pl.BlockSpec((sup, wide), lambda h, i: (i, h)),
                   pl.BlockSpec((hp, nc, HG_DIM, HG_DIM), lambda h, i: (h, i, 0, 0))],
        out_shape=[jax.ShapeDtypeStruct((S, D_MODEL), F32),
                   jax.ShapeDtypeStruct((HG_HEADS, S // HG_CHUNK, HG_DIM, HG_DIM), F32)],
        scratch_shapes=[pltpu.VMEM((hp, HG_DIM, HG_DIM), F32)],
        compiler_params=_params(40, 2),
    )(proj, proj, proj, lb)


def _hgrn_bwd(proj, lb, d_o, states):
    S = proj.shape[0]
    sup = HG_SUPER_BWD
    nb = S // sup
    nc = sup // HG_CHUNK
    hp = HG_HEADS_PER_STEP
    wide = hp * HG_DIM

    def body(q_ref, f_ref, iv_ref, lb_ref, do_ref, st_ref, dq_ref, df_ref, div_ref, dlb_ref, dstate):
        @pl.when(pl.program_id(1) == 0)
        def _():
            dstate[...] = jnp.zeros(dstate.shape, F32)
            dlb_ref[...] = jnp.zeros(dlb_ref.shape, F32)

        mask = _hgrn_mask(sup)
        for hh in range(hp):
            lanes = slice(hh * HG_DIM, (hh + 1) * HG_DIM)
            q, iv, do, lb_v = q_ref[:, lanes], iv_ref[:, lanes], do_ref[:, lanes], lb_ref[:, lanes]
            sig, f, k, e_qa, e_ka, e_qd, e_kd, dc = _hgrn_prep(q, f_ref[:, lanes], lb_v)
            qa, ka, qd, kd = q * e_qa, k * e_ka, q * e_qd, k * e_kd
            scores = jnp.where(mask, _bdot_nt(qa, ka), 0.0)
            d_scores = jnp.where(mask, _bdot_nt(do, iv), 0.0)
            d_iv_intra = _bdot_tn(scores, do)
            d_qa = _bdot(d_scores, ka)
            d_ka = _bdot_tn(d_scores, qa)
            d_qd, d_kd, d_last = [None] * nc, [None] * nc, [None] * nc
            for c in reversed(range(nc)):
                sl = slice(c * HG_CHUNK, (c + 1) * HG_CHUNK)
                st = st_ref[hh, c]
                ds = dstate[hh]
                d_qd[c] = _bdot(do[sl], st)
                d_kd[c] = _bdot(iv[sl], ds)
                div_ref[sl, lanes] = (d_iv_intra[sl] + _bdot_nt(kd[sl], ds)).astype(div_ref.dtype)
                d_last[c] = (jnp.sum(ds * st, axis=0, keepdims=True) * dc[c]
                             + jnp.sum(d_kd[c] * kd[sl], axis=0, keepdims=True))
                dstate[hh] = dc[c] * ds + _bdot_tn(do[sl], qd[sl])
            d_qd = jnp.concatenate(d_qd, axis=0)
            d_kd = jnp.concatenate(d_kd, axis=0)
            d_b = d_qa * qa - d_ka * ka + d_qd * qd - d_kd * kd
            last_rows = jnp.concatenate([jnp.broadcast_to(t, (HG_CHUNK, HG_DIM)) for t in d_last], axis=0)
            d_b = d_b + jnp.where(_chunk_row(d_b.shape) == HG_CHUNK - 1, last_rows, 0.0)
            d_logf = _chunk_suffix_sum(d_b)
            d_k = d_ka * e_ka + d_kd * e_kd
            g_f = d_logf / f
            d_sig = (g_f - d_k) * (1.0 - lb_v)
            dq_ref[:, lanes] = (d_qa * e_qa + d_qd * e_qd).astype(dq_ref.dtype)
            df_ref[:, lanes] = (d_sig * sig * (1.0 - sig)).astype(df_ref.dtype)
            d_lb = jnp.sum((g_f - d_k) * (1.0 - sig), axis=0, keepdims=True)
            dlb_ref[:, lanes] += jnp.broadcast_to(d_lb, (8, HG_DIM))

    rev = lambda i: nb - 1 - i
    blk = lambda base: pl.BlockSpec((sup, wide), functools.partial(lambda h, i, b: (rev(i), b + h), b=base // wide))
    row_out = pl.BlockSpec((sup, wide), lambda h, i: (rev(i), h))
    dq, df, div, dlb = _call(
        body, name="hgrn_bwd", grid=(HG_HEADS // hp, nb),
        in_specs=[blk(COL_Q), blk(COL_F), blk(COL_I), pl.BlockSpec((1, wide), lambda h, i: (0, h)),
                  pl.BlockSpec((sup, wide), lambda h, i: (rev(i), h)),
                  pl.BlockSpec((hp, nc, HG_DIM, HG_DIM), lambda h, i: (h, rev(i), 0, 0))],
        out_specs=[row_out, row_out, row_out, pl.BlockSpec((8, wide), lambda h, i: (0, h))],
        out_shape=[jax.ShapeDtypeStruct((S, D_MODEL), BF16)] * 3 + [jax.ShapeDtypeStruct((8, D_MODEL), F32)],
        scratch_shapes=[pltpu.VMEM((hp, HG_DIM, HG_DIM), F32)],
        compiler_params=_params(40, 2),
    )(proj, proj, proj, lb, d_o, states)
    return dq, df, div, dlb[0:1]


def _s5_matrices(a_re, a_im, log_dt, b_re, b_im, c_re, c_im, d, seg_len):
    dt = jnp.exp(log_dt)[:, None]
    mag = jnp.exp(a_re * dt)
    lr, li = mag * jnp.cos(a_im * dt), mag * jnp.sin(a_im * dt)
    den = a_re * a_re + a_im * a_im
    nr = lr - 1.0
    sr = (nr * a_re + li * a_im) / den
    si = (li * a_re - nr * a_im) / den
    bbr = sr[..., None] * b_re - si[..., None] * b_im
    bbi = sr[..., None] * b_im + si[..., None] * b_re
    eye = jnp.eye(8, dtype=F32)

    def quad_cols(v):
        return v.reshape(S5_QUADS, 8 * S5_STATE)

    def lam_row(re_part, im_part):
        row = jnp.concatenate([quad_cols(re_part), quad_cols(im_part)], axis=1).reshape(1, S5_COLS)
        return jnp.broadcast_to(row, (S5_SEG, S5_COLS))

    def b_mat(bb):
        t = bb.reshape(S5_QUADS, 8, S5_STATE, S5_CH)
        return jnp.einsum("qgnc,gh->qgchn", t, eye).reshape(S5_QUADS, 8 * S5_CH, 8 * S5_STATE)

    def c_mat(cc):
        t = cc.reshape(S5_QUADS, 8, S5_CH, S5_STATE)
        return jnp.einsum("qgcn,gh->qgnhc", t, eye).reshape(S5_QUADS, 8 * S5_STATE, 8 * S5_CH)

    ang = a_im * dt * seg_len
    magp = jnp.exp(a_re * dt * seg_len)
    lpr, lpi = magp * jnp.cos(ang), magp * jnp.sin(ang)
    return dict(
        lam_r=lam_row(lr, lr), lam_i=lam_row(-li, li),
        b_q=jnp.concatenate([b_mat(bbr), b_mat(bbi)], axis=2),
        c_q=jnp.concatenate([c_mat(c_re), -c_mat(c_im)], axis=1),
        d_row=d.reshape(1, S5_WIDTH), pow_r=quad_cols(lpr), pow_i=quad_cols(lpi),
    )


def _s5_parts(v):
    half = S5_QCOLS // 2
    return tuple(v[:, k * half:(k + 1) * half] for k in range(2 * S5_QUADS))


def _s5_advance(parts, lr_ref, li_ref, x_ref, sl, conj):
    half = S5_QCOLS // 2
    out = []
    for q in range(S5_QUADS):
        re_c = slice(q * S5_QCOLS, q * S5_QCOLS + half)
        im_c = slice(q * S5_QCOLS + half, (q + 1) * S5_QCOLS)
        lr, li = lr_ref[:, re_c], li_ref[:, im_c]
        hr, hi = parts[2 * q], parts[2 * q + 1]
        if conj:
            out += [lr * hr + li * hi + x_ref[sl, re_c], lr * hi - li * hr + x_ref[sl, im_c]]
        else:
            out += [lr * hr - li * hi + x_ref[sl, re_c], lr * hi + li * hr + x_ref[sl, im_c]]
    return tuple(out)


def _scan_loop(step, init):
    def trip(o, carry):
        for j in range(S5_UNROLL):
            carry = step(o * S5_UNROLL + j, carry)
        return carry

    return lax.fori_loop(0, S5_TILE_STEPS // S5_UNROLL, trip, init)


def _s5_store(ref, sl, parts):
    half = S5_QCOLS // 2
    for k, v in enumerate(parts):
        ref[sl, k * half:(k + 1) * half] = v


def _s5_fwd_pass(u_perm, mats, h0, with_output):
    S = u_perm.shape[0]
    rows = S5_TILE_STEPS * S5_SEG
    nt = S // rows

    def body(*refs):
        if with_output:
            u_ref, b_ref, lr_ref, li_ref, h0_ref, c_ref, d_ref, y_ref, hinit_ref, hend_ref, xs, hcar = refs
        else:
            u_ref, b_ref, lr_ref, li_ref, h0_ref, hend_ref, xs, hcar = refs

        @pl.when(pl.program_id(0) == 0)
        def _():
            hcar[...] = h0_ref[...]

        if with_output:
            hinit_ref[...] = hcar[...]
        u = u_ref[...]
        ub = u.astype(BF16)
        for q in range(S5_QUADS):
            xs[:, q * S5_QCOLS:(q + 1) * S5_QCOLS] = jnp.dot(ub[:, q * 128:(q + 1) * 128], b_ref[q], preferred_element_type=F32)

        def step(t, h):
            sl = pl.ds(pl.multiple_of(t * S5_SEG, S5_SEG), S5_SEG)
            hn = _s5_advance(h, lr_ref, li_ref, xs, sl, False)
            _s5_store(xs, sl, hn)
            return hn

        h = _scan_loop(step, _s5_parts(hcar[...]))
        _s5_store(hcar, slice(None), h)
        _s5_store(hend_ref, slice(None), h)
        if with_output:
            ys = [jnp.dot(xs[:, q * S5_QCOLS:(q + 1) * S5_QCOLS].astype(BF16), c_ref[q], preferred_element_type=F32)
                  for q in range(S5_QUADS)]
            y_ref[...] = jnp.concatenate(ys, axis=1) + d_ref[...] * u

    full = lambda a: pl.BlockSpec(a.shape, functools.partial(lambda i, nd: (0,) * nd, nd=a.ndim))
    ins = [u_perm, mats["b_q"], mats["lam_r"], mats["lam_i"], h0]
    in_specs = [pl.BlockSpec((rows, S5_WIDTH), lambda i: (i, 0))] + [full(a) for a in ins[1:]]
    out_specs = [pl.BlockSpec((S5_SEG, S5_COLS), lambda i: (0, 0))]
    out_shape = [jax.ShapeDtypeStruct((S5_SEG, S5_COLS), F32)]
    if with_output:
        ins += [mats["c_q"], mats["d_row"]]
        in_specs += [full(mats["c_q"]), full(mats["d_row"])]
        out_specs = [pl.BlockSpec((rows, S5_WIDTH), lambda i: (i, 0)),
                     pl.BlockSpec((None, S5_SEG, S5_COLS), lambda i: (i, 0, 0))] + out_specs
        out_shape = [jax.ShapeDtypeStruct((S, S5_WIDTH), F32), jax.ShapeDtypeStruct((nt, S5_SEG, S5_COLS), F32)] + out_shape
    return _call(
        body, name="s5_fwd_y" if with_output else "s5_fwd_ends", grid=(nt,), in_specs=in_specs, out_specs=out_specs,
        out_shape=out_shape,
        scratch_shapes=[pltpu.VMEM((rows, S5_COLS), F32), pltpu.VMEM((S5_SEG, S5_COLS), F32)],
        compiler_params=_params(40),
    )(*ins)


def _s5_bwd_ends(dy_perm, mats):
    S = dy_perm.shape[0]
    rows = S5_TILE_STEPS * S5_SEG
    nt = S // rows

    def body(dy_ref, c_ref, lr_ref, li_ref, gend_ref, gs, gcar):
        @pl.when(pl.program_id(0) == 0)
        def _():
            gcar[...] = jnp.zeros(gcar.shape, F32)

        dyb = dy_ref[...].astype(BF16)
        for q in range(S5_QUADS):
            gs[:, q * S5_QCOLS:(q + 1) * S5_QCOLS] = lax.dot_general(
                dyb[:, q * 128:(q + 1) * 128], c_ref[q], (((1,), (1,)), ((), ())), preferred_element_type=F32)

        def step(k, g):
            t = S5_TILE_STEPS - 1 - k
            sl = pl.ds(pl.multiple_of(t * S5_SEG, S5_SEG), S5_SEG)
            return _s5_advance(g, lr_ref, li_ref, gs, sl, True)

        g = _scan_loop(step, _s5_parts(gcar[...]))
        _s5_store(gcar, slice(None), g)
        _s5_store(gend_ref, slice(None), g)

    full = lambda a: pl.BlockSpec(a.shape, functools.partial(lambda i, nd: (0,) * nd, nd=a.ndim))
    return _call(
        body, name="s5_bwd_ends", grid=(nt,),
        in_specs=[pl.BlockSpec((rows, S5_WIDTH), lambda i: (nt - 1 - i, 0)), full(mats["c_q"]), full(mats["lam_r"]),
                  full(mats["lam_i"])],
        out_specs=pl.BlockSpec((S5_SEG, S5_COLS), lambda i: (0, 0)),
        out_shape=jax.ShapeDtypeStruct((S5_SEG, S5_COLS), F32),
        scratch_shapes=[pltpu.VMEM((rows, S5_COLS), F32), pltpu.VMEM((S5_SEG, S5_COLS), F32)],
        compiler_params=_params(40),
    )(dy_perm, mats["c_q"], mats["lam_r"], mats["lam_i"])


def _s5_bwd_full(u_perm, dy_perm, hinit, g0, mats):
    S = u_perm.shape[0]
    rows = S5_TILE_STEPS * S5_SEG
    nt = S // rows

    def body(u_ref, dy_ref, hinit_ref, g0_ref, b_ref, c_ref, lr_ref, li_ref, d_ref,
             du_ref, dp_ref, dq_ref, db_ref, dc_ref, dd_ref, hs, gs, gcar):
        @pl.when(pl.program_id(0) == 0)
        def _():
            gcar[...] = g0_ref[...]
            for ref in (dp_ref, dq_ref, db_ref, dc_ref, dd_ref):
                ref[...] = jnp.zeros(ref.shape, F32)

        u, dy = u_ref[...], dy_ref[...]
        ub, dyb = u.astype(BF16), dy.astype(BF16)
        hs[0:S5_SEG, :] = hinit_ref[...]
        for q in range(S5_QUADS):
            cols = slice(q * S5_QCOLS, (q + 1) * S5_QCOLS)
            hs[S5_SEG:, cols] = jnp.dot(ub[:, q * 128:(q + 1) * 128], b_ref[q], preferred_element_type=F32)
            gs[:, cols] = lax.dot_general(dyb[:, q * 128:(q + 1) * 128], c_ref[q], (((1,), (1,)), ((), ())),
                                          preferred_element_type=F32)

        def fstep(t, h):
            sl = pl.ds(pl.multiple_of((t + 1) * S5_SEG, S5_SEG), S5_SEG)
            hn = _s5_advance(h, lr_ref, li_ref, hs, sl, False)
            _s5_store(hs, sl, hn)
            return hn

        _scan_loop(fstep, _s5_parts(hinit_ref[...]))

        def bstep(k, g):
            t = S5_TILE_STEPS - 1 - k
            sl = pl.ds(pl.multiple_of(t * S5_SEG, S5_SEG), S5_SEG)
            gn = _s5_advance(g, lr_ref, li_ref, gs, sl, True)
            _s5_store(gs, sl, gn)
            return gn

        _s5_store(gcar, slice(None), _scan_loop(bstep, _s5_parts(gcar[...])))

        half = S5_QCOLS // 2
        dus = []
        for q in range(S5_QUADS):
            cols = slice(q * S5_QCOLS, (q + 1) * S5_QCOLS)

            def astep(t, carry, q=q):
                sl = pl.ds(pl.multiple_of(t * S5_SEG, S5_SEG), S5_SEG)
                g = gs[sl, q * S5_QCOLS:(q + 1) * S5_QCOLS]
                hp = hs[sl, q * S5_QCOLS:(q + 1) * S5_QCOLS]
                hp_sw = jnp.concatenate([hp[:, half:], hp[:, :half]], axis=1)
                return carry[0] + g * hp, carry[1] + g * hp_sw

            zero = jnp.zeros((S5_SEG, S5_QCOLS), F32)
            acc_p, acc_q = _scan_loop(astep, (zero, zero))
            dp_ref[:, cols] += jnp.sum(acc_p, axis=0, keepdims=True)
            dq_ref[:, cols] += jnp.sum(acc_q, axis=0, keepdims=True)
            gq = gs[:, cols].astype(BF16)
            db_ref[q] += lax.dot_general(ub[:, q * 128:(q + 1) * 128], gq, (((0,), (0,)), ((), ())),
                                         preferred_element_type=F32)
            hq = hs[S5_SEG:, cols].astype(BF16)
            dc_ref[q] += lax.dot_general(dyb[:, q * 128:(q + 1) * 128], hq, (((0,), (0,)), ((), ())),
                                         preferred_element_type=F32)
            dus.append(lax.dot_general(gq, b_ref[q], (((1,), (1,)), ((), ())), preferred_element_type=F32))
        du_ref[...] = (jnp.concatenate(dus, axis=1) + d_ref[...] * dy).astype(du_ref.dtype)
        dd_ref[...] += jnp.sum(dy * u, axis=0, keepdims=True)

    full = lambda a: pl.BlockSpec(a.shape, functools.partial(lambda i, nd: (0,) * nd, nd=a.ndim))
    rev_rows = pl.BlockSpec((rows, S5_WIDTH), lambda i: (nt - 1 - i, 0))
    consts = [mats["b_q"], mats["c_q"], mats["lam_r"], mats["lam_i"], mats["d_row"]]
    acc = lambda s: pl.BlockSpec(s, functools.partial(lambda i, nd: (0,) * nd, nd=len(s)))
    acc_shapes = [(1, S5_COLS), (1, S5_COLS), (S5_QUADS, 128, S5_QCOLS), (S5_QUADS, 128, S5_QCOLS), (1, S5_WIDTH)]
    return _call(
        body, name="s5_bwd_full", grid=(nt,),
        in_specs=[rev_rows, rev_rows, pl.BlockSpec((None, S5_SEG, S5_COLS), lambda i: (nt - 1 - i, 0, 0)), full(g0)]
        + [full(a) for a in consts],
        out_specs=[rev_rows] + [acc(s) for s in acc_shapes],
        out_shape=[jax.ShapeDtypeStruct((S, S5_WIDTH), BF16)] + [jax.ShapeDtypeStruct(s, F32) for s in acc_shapes],
        scratch_shapes=[pltpu.VMEM((rows + S5_SEG, S5_COLS), F32), pltpu.VMEM((rows, S5_COLS), F32),
                        pltpu.VMEM((S5_SEG, S5_COLS), F32)],
        compiler_params=_params(56),
    )(u_perm, dy_perm, hinit, g0, *consts)


def _cmul(ar, ai, br, bi):
    return ar * br - ai * bi, ar * bi + ai * br


def _split_cols(v):
    t = v.reshape(v.shape[0], S5_QUADS, 2, S5_QCOLS // 2)
    return t[:, :, 0], t[:, :, 1]


def _join_cols(re, im):
    return jnp.stack([re, im], axis=2).reshape(re.shape[0], S5_COLS)


def _segment_starts(ends, pow_r, pow_i, reverse):
    er, ei = _split_cols(ends)
    pi = -pow_i if reverse else pow_i
    order = list(range(S5_SEG))
    if reverse:
        order = order[::-1]
    cr, ci = jnp.zeros_like(er[0]), jnp.zeros_like(ei[0])
    out_r, out_i = [None] * S5_SEG, [None] * S5_SEG
    for j in order:
        out_r[j], out_i[j] = cr, ci
        mr, mi = _cmul(pow_r, pi, cr, ci)
        cr, ci = mr + er[j], mi + ei[j]
    return _join_cols(jnp.stack(out_r), jnp.stack(out_i))


def _to_segments(a):
    S, w = a.shape
    return a.reshape(S5_SEG, S // S5_SEG, w).transpose(1, 0, 2).reshape(S, w)


def _from_segments(a):
    S, w = a.shape
    return a.reshape(S // S5_SEG, S5_SEG, w).transpose(1, 0, 2).reshape(S, w)


def _my_pos():
    return lax.axis_index("x"), lax.axis_index("y"), lax.axis_index("c")


def _flip(pos, k):
    x, y, c = pos
    return (1 - x if k & 4 else x, 1 - y if k & 2 else y, 1 - c if k & 1 else c)


def _index_of(pos):
    return 4 * pos[0] + 2 * pos[1] + pos[2]


_GATHER_FLIPS = (0, 1, 4, 5, 2, 3, 6, 7)


def _inproj_gather(x, norm_g, pack_a, pack_b, pack_c):
    S = x.shape[0]
    tm = min(S, 1024)
    n_i = S // tm
    order = jnp.stack([_index_of(_flip(_my_pos(), k)) for k in _GATHER_FLIPS]).astype(jnp.int32)

    def body(order_ref, x_ref, g_ref, pa_ref, pb_ref, pc_ref, ut_ref, proj_ref, proj16_ref, oa_ref, ob_ref, oc_ref,
             wv, u_scr, send_sems, recv_sems, local_sems):
        s, i = pl.program_id(0), pl.program_id(1)
        me = _my_pos()
        mine = _index_of(me)
        sibling = _flip(me, 1)
        srcs = (pb_ref, pa_ref, pc_ref)
        dsts = (wv, oa_ref, oc_ref)

        def direct(a, k):
            return pltpu.make_async_remote_copy(
                src_ref=srcs[a], dst_ref=dsts[a].at[mine], send_sem=send_sems.at[a * 8 + k],
                recv_sem=recv_sems.at[a * 8 + k], device_id=_flip(me, k), device_id_type=MESH)

        def passed_on(a, k):
            slot = _index_of(_flip(me, k))
            return pltpu.make_async_remote_copy(
                src_ref=dsts[a].at[slot], dst_ref=dsts[a].at[slot], send_sem=send_sems.at[a * 8 + (k | 1)],
                recv_sem=recv_sems.at[a * 8 + (k | 1)], device_id=sibling, device_id_type=MESH)

        def arrival(a, k):
            slot = _index_of(_flip(me, k))
            pltpu.make_async_remote_copy(
                src_ref=dsts[a].at[slot], dst_ref=dsts[a].at[slot], send_sem=send_sems.at[a * 8 + k],
                recv_sem=recv_sems.at[a * 8 + k], device_id=me, device_id_type=MESH).wait_recv()

        def own_copy(a):
            return pltpu.make_async_copy(srcs[a], dsts[a].at[mine], local_sems.at[a])

        def keep(idx):
            slot = _index_of(_flip(me, _GATHER_FLIPS[idx]))
            return pltpu.make_async_copy(wv.at[slot], ob_ref.at[slot], local_sems.at[3 + idx])


        first = (s == 0) & (i == 0)

        @pl.when(first)
        def _():
            for a in range(3):
                own_copy(a).start()
            for k in (1, 4, 2):
                direct(0, k).start()
            own_copy(0).wait()
            keep(0).start()

        for idx, k in enumerate(_GATHER_FLIPS):
            if idx == 0:
                continue

            @pl.when((s == idx) & (i == 0))
            def _(idx=idx, k=k):
                arrival(0, k)
                if k in (4, 2, 6):
                    passed_on(0, k).start()
                keep(idx).start()
                if idx == 1:
                    direct(0, 6).start()
                if idx == 2:
                    for a in (1, 2):
                        for k in (1, 4, 2, 6):
                            direct(a, k).start()

        @pl.when(s == 0)
        def _():
            y, _, _ = _rms_fwd(x_ref[...], g_ref[...])
            u_scr[pl.ds(pl.multiple_of(i * tm, tm), tm), :] = y.astype(BF16)
            ut_ref[...] = y.T.astype(BF16)

        ub = u_scr[pl.ds(pl.multiple_of(i * tm, tm), tm), :]
        block = jnp.dot(ub, wv[order_ref[s]], preferred_element_type=F32)
        proj_ref[...] = block
        proj16_ref[...] = block.astype(BF16)

        @pl.when((s == N_DEV - 1) & (i == n_i - 1))
        def _():
            for a in (1, 2):
                for k in (4, 2, 6):
                    arrival(a, k)
                    passed_on(a, k).start()
            for a in (1, 2):
                for k in (1, 5, 3, 7):
                    arrival(a, k)
                own_copy(a).wait()
            for a in range(3):
                for k in (1, 4, 2, 6):
                    direct(a, k).wait_send()
                for k in (4, 2, 6):
                    passed_on(a, k).wait_send()
            for idx in range(N_DEV):
                keep(idx).wait()

    any_spec = pl.BlockSpec(memory_space=pl.ANY)
    vmem = pl.BlockSpec(memory_space=pltpu.VMEM)
    grid_spec = pltpu.PrefetchScalarGridSpec(
        num_scalar_prefetch=1, grid=(N_DEV, n_i),
        in_specs=[pl.BlockSpec((tm, D_MODEL), lambda s, i, o: (jnp.where(s == 0, i, 0), 0)),
                  pl.BlockSpec((1, D_MODEL), lambda s, i, o: (0, 0)), any_spec, vmem, any_spec],
        out_specs=[pl.BlockSpec((D_MODEL, tm), lambda s, i, o: (0, jnp.where(s == 0, i, n_i - 1))),
                   pl.BlockSpec((tm, SHARD_IN), lambda s, i, o: (i, o[s])),
                   pl.BlockSpec((tm, SHARD_IN), lambda s, i, o: (i, o[s])), any_spec, any_spec, any_spec],
        scratch_shapes=[pltpu.VMEM((N_DEV,) + pack_b.shape, BF16), pltpu.VMEM((S, D_MODEL), BF16),
                        pltpu.SemaphoreType.DMA((24,)), pltpu.SemaphoreType.DMA((24,)), pltpu.SemaphoreType.DMA((3 + N_DEV,))],
    )
    return _call(
        body, name="inproj_gather", grid_spec=grid_spec,
        out_shape=[jax.ShapeDtypeStruct((D_MODEL, S), BF16), jax.ShapeDtypeStruct((S, IN_COLS), F32),
                   jax.ShapeDtypeStruct((S, IN_COLS), BF16), jax.ShapeDtypeStruct((N_DEV,) + pack_a.shape, BF16), jax.ShapeDtypeStruct((N_DEV,) + pack_b.shape, BF16),
                   jax.ShapeDtypeStruct((N_DEV,) + pack_c.shape, BF16)],
        compiler_params=_params(60, 2),
    )(order, x, norm_g, pack_a, pack_b, pack_c)


_SCATTER_FLIPS = (7, 6, 5, 4, 3, 2, 1, 0)
_N_CHIPS = 4


def _for_row_chunks(n_rows, chunk, fn):
    def step(c, carry):
        fn(pl.ds(pl.multiple_of(c * chunk, chunk), chunk))
        return carry

    lax.fori_loop(0, n_rows // chunk, step, 0)


def _grad_w_in_scatter(dproj, u_t, rs_a, rs_c, small_partial):
    S = u_t.shape[1]
    tm = min(S, 1024)
    n_i = S // tm
    order = jnp.stack([_index_of(_flip(_my_pos(), k)) for k in _SCATTER_FLIPS]).astype(jnp.int32)
    shapes = ((D_MODEL, SHARD_IN), rs_a.shape[1:], rs_c.shape[1:])
    row_chunk = 128

    def body(order_ref, dp_ref, ut_ref, ra_ref, rc_ref, p_ref, gb_ref, ga_ref, gc_ref, gs_ref, acc, sib_b, d2d_b,
             send_b, ici_b, mine_a, sib_a, ici_a, mine_c, sib_c, ici_c, gath, send_sems, recv_sems, local_sems):
        s, i = pl.program_id(0), pl.program_id(1)
        me = _my_pos()
        sibling = _flip(me, 1)

        my_chip = 2 * me[0] + me[1]

        small_d2d = pltpu.make_async_remote_copy(
            src_ref=p_ref, dst_ref=gath.at[_N_CHIPS], send_sem=send_sems.at[21], recv_sem=recv_sems.at[21],
            device_id=sibling, device_id_type=MESH)

        def small_ici(m):
            return pltpu.make_async_remote_copy(
                src_ref=gath.at[my_chip], dst_ref=gath.at[my_chip], send_sem=send_sems.at[22 + m],
                recv_sem=recv_sems.at[22 + m], device_id=_flip(me, 6 - 2 * m), device_id_type=MESH)
        sib = (sib_b, sib_a, sib_c)
        ici = (ici_b, ici_a, ici_c)
        outs = (gb_ref, ga_ref, gc_ref)

        def to_sibling(arr, m, src):
            return pltpu.make_async_remote_copy(
                src_ref=src, dst_ref=sib[arr].at[m], send_sem=send_sems.at[arr * 7 + m],
                recv_sem=recv_sems.at[arr * 7 + m], device_id=sibling, device_id_type=MESH)

        def over_ici(arr, m, src):
            return pltpu.make_async_remote_copy(
                src_ref=src, dst_ref=ici[arr].at[m], send_sem=send_sems.at[arr * 7 + 4 + m],
                recv_sem=recv_sems.at[arr * 7 + 4 + m], device_id=_flip(me, 6 - 2 * m), device_id_type=MESH)

        def from_sibling(arr, m):
            to_sibling(arr, m, sib[arr].at[m]).wait_recv()

        def from_ici(arr, m):
            over_ici(arr, m, ici[arr].at[m]).wait_recv()

        small = ((1, ra_ref, mine_a), (2, rc_ref, mine_c))

        def local_copy(arr, src, mine, m):
            return pltpu.make_async_copy(src.at[_index_of(_flip(me, 6 - 2 * m))], mine.at[m],
                                         local_sems.at[(arr - 1) * _N_CHIPS + m])

        @pl.when((s == 0) & (i == 0))
        def _():
            small_d2d.start()
            for arr, src, mine in small:
                for m in range(_N_CHIPS):
                    to_sibling(arr, m, src.at[_index_of(_flip(me, 7 - 2 * m))]).start()
                    local_copy(arr, src, mine, m).start()

        @pl.when((s == 1) & (i == 0))
        def _():
            small_d2d.wait_recv()
            gath[my_chip] = p_ref[...] + gath[_N_CHIPS]
            for m in range(_N_CHIPS - 1):
                small_ici(m).start()
            for arr, src, mine in small:
                rows, chunk = shapes[arr][0], 16
                for m in range(_N_CHIPS):
                    local_copy(arr, src, mine, m).wait()
                    from_sibling(arr, m)
                    if m < _N_CHIPS - 1:
                        def add(sl, arr=arr, mine=mine, m=m):
                            mine[m, sl, :] = (mine[m, sl, :].astype(F32) + sib[arr][m, sl, :].astype(F32)).astype(BF16)

                        _for_row_chunks(rows, chunk, add)
                        over_ici(arr, m, mine.at[m]).start()
                    else:
                        def keep(sl, arr=arr, mine=mine, m=m):
                            outs[arr][sl, :] = mine[m, sl, :].astype(F32) + sib[arr][m, sl, :].astype(F32)

                        _for_row_chunks(rows, chunk, keep)

        @pl.when(i == 0)
        def _():
            acc[...] = jnp.zeros(acc.shape, F32)

        acc[...] += jnp.dot(ut_ref[...], dp_ref[...], preferred_element_type=F32)

        def block_rows(c):
            return acc[c * row_chunk:(c + 1) * row_chunk, :]

        for m in range(_N_CHIPS):
            @pl.when((s == 2 * m) & (i == n_i - 1))
            def _(m=m):
                if m > 0:
                    to_sibling(0, m - 1, d2d_b).wait_send()
                for c in range(D_MODEL // row_chunk):
                    d2d_b[c * row_chunk:(c + 1) * row_chunk, :] = block_rows(c).astype(BF16)
                to_sibling(0, m, d2d_b).start()

            @pl.when((s == 2 * m + 1) & (i == n_i - 1))
            def _(m=m):
                from_sibling(0, m)
                slot = m % 2
                if m == 2:
                    over_ici(0, 0, send_b.at[0]).wait_send()
                for c in range(D_MODEL // row_chunk):
                    rows = slice(c * row_chunk, (c + 1) * row_chunk)
                    total = block_rows(c) + sib_b[m, rows, :].astype(F32)
                    if m < _N_CHIPS - 1:
                        send_b[slot, rows, :] = total.astype(BF16)
                    else:
                        gb_ref[rows, :] = total
                if m < _N_CHIPS - 1:
                    over_ici(0, m, send_b.at[slot]).start()

        @pl.when((s == N_DEV - 1) & (i == n_i - 1))
        def _():
            for arr in range(3):
                for m in range(_N_CHIPS - 1):
                    from_ici(arr, m)
                rows = shapes[arr][0]

                def add(sl, arr=arr):
                    outs[arr][sl, :] = (outs[arr][sl, :] + ici[arr][0, sl, :].astype(F32)
                                        + ici[arr][1, sl, :].astype(F32) + ici[arr][2, sl, :].astype(F32))

                _for_row_chunks(rows, 16, add)
            for m in range(_N_CHIPS - 1):
                small_ici(m).wait_recv()
            gs_ref[...] = (gath[0] + gath[1]) + (gath[2] + gath[3])
            small_d2d.wait_send()
            for m in range(_N_CHIPS - 1):
                small_ici(m).wait_send()
            to_sibling(0, _N_CHIPS - 1, d2d_b).wait_send()
            over_ici(0, 1, send_b.at[1]).wait_send()
            over_ici(0, 2, send_b.at[0]).wait_send()
            for arr, src, mine in small:
                for m in range(_N_CHIPS):
                    to_sibling(arr, m, src.at[0]).wait_send()
                for m in range(_N_CHIPS - 1):
                    over_ici(arr, m, mine.at[m]).wait_send()

    any_spec = pl.BlockSpec(memory_space=pl.ANY)
    vmem = pl.BlockSpec(memory_space=pltpu.VMEM)
    half = lambda shp, n: pltpu.VMEM((n,) + tuple(shp), BF16)
    grid_spec = pltpu.PrefetchScalarGridSpec(
        num_scalar_prefetch=1, grid=(N_DEV, n_i),
        in_specs=[pl.BlockSpec((tm, SHARD_IN), lambda s, i, o: (i, o[s])),
                  pl.BlockSpec((D_MODEL, tm), lambda s, i, o: (0, i)), any_spec, any_spec, vmem],
        out_specs=[vmem, vmem, vmem, vmem],
        scratch_shapes=[
            pltpu.VMEM((D_MODEL, SHARD_IN), F32), half(shapes[0], _N_CHIPS), pltpu.VMEM(shapes[0], BF16),
            half(shapes[0], 2), half(shapes[0], _N_CHIPS - 1),
            half(shapes[1], _N_CHIPS), half(shapes[1], _N_CHIPS), half(shapes[1], _N_CHIPS - 1),
            half(shapes[2], _N_CHIPS), half(shapes[2], _N_CHIPS), half(shapes[2], _N_CHIPS - 1),
            pltpu.VMEM((_N_CHIPS + 1,) + small_partial.shape, F32),
            pltpu.SemaphoreType.DMA((25,)), pltpu.SemaphoreType.DMA((25,)), pltpu.SemaphoreType.DMA((2 * _N_CHIPS,))],
    )
    return _call(
        body, name="grad_w_in_scatter", grid_spec=grid_spec,
        out_shape=[jax.ShapeDtypeStruct(shp, F32) for shp in shapes] + [jax.ShapeDtypeStruct(small_partial.shape, F32)],
        compiler_params=_params(60, 2),
    )(order, dproj, u_t, rs_a, rs_c, small_partial)


def _adam_update(g, w, m, v):
    m2 = ADAM_B1 * m + (1.0 - ADAM_B1) * g
    v2 = ADAM_B2 * v + (1.0 - ADAM_B2) * (g * g)
    m_hat = m2 / (1.0 - ADAM_B1 ** ADAM_STEP)
    v_hat = v2 / (1.0 - ADAM_B2 ** ADAM_STEP)
    delta = -ADAM_LR * (m_hat / (jnp.sqrt(v_hat) + ADAM_EPS) + ADAM_WD * w)
    return delta, m2, v2


def _adam_rows(g, w, m, v):
    rows, cols = w.shape
    tm = rows if rows % 256 else 256

    def fn(rv, cr, out):
        return list(_adam_update(*rv)), []

    outs, _ = _rowwise("adamw", fn, rows, tm, [(a, cols, 0) for a in (g, w, m, v)], [], [(cols, F32)] * 3, [], 32)
    return outs


_SMALL = ["norm_g", "hg_lb", "hg_norm_g", "s5_a_re", "s5_a_im", "s5_log_dt", "s5_b_re", "s5_b_im", "s5_c_re",
          "s5_c_im", "s5_d", "b_glu", "ple_norm_g", "final_norm_g"]
_BIG = ["w_in", "w_o_hg", "w_glu", "w_o_s5", "w_out", "w_ple", "w_ple_gate"]
_ORDER = ["norm_g", "w_in", "hg_lb", "hg_norm_g", "w_o_hg", "s5_a_re", "s5_a_im", "s5_log_dt", "s5_b_re", "s5_b_im",
          "s5_c_re", "s5_c_im", "s5_d", "w_glu", "b_glu", "w_o_s5", "w_out", "ple_norm_g", "w_ple", "w_ple_gate",
          "final_norm_g"]


def _pack_small(vals, tail=None):
    parts = []
    for name in _SMALL:
        flat = vals[name].reshape(-1).astype(F32)
        pad = (-flat.shape[0]) % 1024
        parts.append(jnp.pad(flat, (0, pad)))
    tail = jnp.zeros((0,), F32) if tail is None else tail.reshape(-1).astype(F32)
    parts.append(jnp.pad(tail, (0, 1024 - tail.shape[0])))
    return jnp.concatenate(parts).reshape(-1, 128)


def _unpack_small(packed, like):
    flat = packed.reshape(-1)
    out, off = {}, 0
    for name in _SMALL:
        size = like[name].size
        out[name] = flat[off:off + size].reshape(like[name].shape)
        off += size + (-size) % 1024
    return out


def _col_blocks(full):
    k = full.shape[0]
    return full.reshape(k, N_DEV, 128).transpose(1, 0, 2)


def _from_col_blocks(blocks):
    k = blocks.shape[1]
    return blocks.transpose(1, 0, 2).reshape(k, N_DEV * 128)


def kernel(x, p, norm_g, w_in, hg_lb, hg_norm_g, w_o_hg, s5_a_re, s5_a_im, s5_log_dt, s5_b_re, s5_b_im, s5_c_re, s5_c_im, s5_d, w_glu, b_glu, w_o_s5, w_out, ple_norm_g, w_ple, w_ple_gate, final_norm_g, loss_target, m_norm_g, m_w_in, m_hg_lb, m_hg_norm_g, m_w_o_hg, m_s5_a_re, m_s5_a_im, m_s5_log_dt, m_s5_b_re, m_s5_b_im, m_s5_c_re, m_s5_c_im, m_s5_d, m_w_glu, m_b_glu, m_w_o_s5, m_w_out, m_ple_norm_g, m_w_ple, m_w_ple_gate, m_final_norm_g, v_norm_g, v_w_in, v_hg_lb, v_hg_norm_g, v_w_o_hg, v_s5_a_re, v_s5_a_im, v_s5_log_dt, v_s5_b_re, v_s5_b_im, v_s5_c_re, v_s5_c_im, v_s5_d, v_w_glu, v_b_glu, v_w_o_s5, v_w_out, v_ple_norm_g, v_w_ple, v_w_ple_gate, v_final_norm_g):
    args = dict(locals())
    w = {n: args[n] for n in _ORDER}
    m = {n: args["m_" + n] for n in _ORDER}
    v = {n: args["v_" + n] for n in _ORDER}
    xs = x[0]
    ps = p[0, 0]
    tgt = loss_target[0]
    S = xs.shape[0]

    pack_a = jnp.concatenate([w_o_hg[0], w_out[0], w_ple_gate[0]], axis=0).astype(BF16)
    pack_b = w_in[0].astype(BF16)
    pack_c = jnp.concatenate([w_glu[0], w_o_s5[0], w_ple[0]], axis=0).astype(BF16)
    u_t, proj, proj16, all_a, all_b, all_c = _inproj_gather(xs, norm_g, pack_a, pack_b, pack_c)
    wf_o_hg = all_a[:, 0:128].reshape(D_MODEL, D_MODEL)
    wf_out = all_a[:, 128:256].reshape(D_MODEL, D_MODEL)
    wf_pg = all_a[:, 256:384].reshape(D_MODEL, D_MODEL)
    wf_glu = _from_col_blocks(all_c[:, 0:512])
    wf_o_s5 = _from_col_blocks(all_c[:, 512:1024])
    wf_ple = _from_col_blocks(all_c[:, 1024:1280])

    lb = jax.nn.sigmoid(hg_lb[0:1] - hg_lb[1:2])
    s5_names = ["s5_a_re", "s5_a_im", "s5_log_dt", "s5_b_re", "s5_b_im", "s5_c_re", "s5_c_im", "s5_d"]
    build = lambda *a: _s5_matrices(*a, seg_len=S // S5_SEG)
    mats_f32, mats_vjp = jax.vjp(build, *[w[n][0] for n in s5_names])
    mats = dict(mats_f32, b_q=mats_f32["b_q"].astype(BF16), c_q=mats_f32["c_q"].astype(BF16))
    bias_glu = b_glu

    o, states = _hgrn_fwd(proj, lb)
    u_perm = _to_segments(proj[:, COL_US:COL_US + S5_WIDTH])
    zeros_state = jnp.zeros((S5_SEG, S5_COLS), F32)
    (h_ends,) = _s5_fwd_pass(u_perm, mats, zeros_state, False)
    h0 = _segment_starts(h_ends, mats["pow_r"], mats["pow_i"], False)
    y_perm, h_init, _ = _s5_fwd_pass(u_perm, mats, h0, True)
    ys = _from_segments(y_perm)
    y_hg, y_s5, glu, h1 = _stage_branches(o, proj16, ys, xs, hg_norm_g, wf_o_hg, wf_glu, bias_glu, wf_o_s5, wf_out)

    dh1, dh1_16, (loss_acc, d_final_g, d_ple_g, d_w_ple, d_w_pg) = _stage_ple_loss(
        h1, ps, tgt, ple_norm_g, wf_ple, wf_pg, final_norm_g.reshape(1, D_MODEL))
    (d_gate_hg, d_gate_s5, d_yhg, d_ys5), (d_w_out,) = _stage_bwd_merge(dh1_16, y_hg, y_s5, proj16, wf_out)
    (d_o, d_g_hg), (d_w_o_hg, d_hg_norm) = _stage_bwd_hg_path(d_yhg, o, proj16, hg_norm_g, wf_o_hg)
    (d_ys, d_z), (d_w_o_s5, d_w_glu, d_b_glu) = _stage_bwd_s5_path(d_ys5, ys, glu, proj16, wf_o_s5, wf_glu)
    dq, df, div, d_lb = _hgrn_bwd(proj, lb, d_o, states)
    dy_perm = _to_segments(d_ys)
    g_ends = _s5_bwd_ends(dy_perm, mats)
    g0 = _segment_starts(g_ends, mats["pow_r"], mats["pow_i"], True)
    du_perm, acc_p, acc_q, d_bq, d_cq_t, d_d = _s5_bwd_full(u_perm, dy_perm, h_init, g0, mats)
    d_us = _from_segments(du_perm)
    grad_x, dproj, d_norm_g = _stage_inproj_bwd([dq, df, div, d_g_hg, d_us, d_z, d_gate_hg, d_gate_s5], xs, dh1,
                                                norm_g, all_b)

    p_re, p_im = _split_cols(acc_p)
    q_re, q_im = _split_cols(acc_q)
    d_lam_r = (p_re + p_im)[0]
    d_lam_i = (q_im - q_re)[0]
    zero_row = jnp.zeros((S5_SEG, S5_COLS), F32)
    row_of = lambda re_part, im_part: zero_row.at[0].set(_join_cols(re_part[None], im_part[None])[0])
    zeros_q = jnp.zeros_like(d_lam_r)
    cot = dict(
        lam_r=row_of(d_lam_r, zeros_q), lam_i=row_of(zeros_q, d_lam_i),
        b_q=d_bq, c_q=d_cq_t.transpose(0, 2, 1), d_row=d_d,
        pow_r=jnp.zeros_like(mats["pow_r"]), pow_i=jnp.zeros_like(mats["pow_i"]),
    )
    d_s5 = mats_vjp(cot)

    s_lb = lb * (1.0 - lb)
    d_hg_lb = jnp.concatenate([d_lb * s_lb, -d_lb * s_lb], axis=0)
    small_g = dict(norm_g=d_norm_g, hg_lb=d_hg_lb, hg_norm_g=d_hg_norm, b_glu=d_b_glu, ple_norm_g=d_ple_g,
                   final_norm_g=d_final_g)
    for name, g in zip(s5_names, d_s5):
        small_g[name] = g
    pk = lambda d: _pack_small({n: d[n] for n in _SMALL})
    rs_a = jnp.concatenate([d_w_o_hg.reshape(N_DEV, 128, D_MODEL), d_w_out.reshape(N_DEV, 128, D_MODEL),
                            d_w_pg.reshape(N_DEV, 128, D_MODEL)], axis=1).astype(BF16)
    rs_c = jnp.concatenate([_col_blocks(d_w_glu), _col_blocks(d_w_o_s5), _col_blocks(d_w_ple)], axis=1).astype(BF16)
    partial = _pack_small({n: small_g[n] for n in _SMALL}, tail=loss_acc[0, 0:1])
    g_b, g_a, g_c, sg = _grad_w_in_scatter(dproj, u_t, rs_a, rs_c, partial)
    sd, sm, sv = _adam_rows(sg, pk(w), pk(m), pk(v))
    like = {n: w[n] for n in _SMALL}
    out_g, out_d, out_m, out_v = (_unpack_small(t, like) for t in (sg, sd, sm, sv))
    big_g = dict(w_o_hg=g_a[0:128], w_out=g_a[128:256], w_ple_gate=g_a[256:384], w_in=g_b,
                 w_glu=g_c[0:512], w_o_s5=g_c[512:1024], w_ple=g_c[1024:1280])
    for name in _BIG:
        shape = w[name].shape
        g2 = big_g[name]
        d2, m2, v2 = _adam_rows(g2, w[name][0], m[name][0], v[name][0])
        out_g[name], out_d[name], out_m[name], out_v[name] = (t.reshape(shape) for t in (g2, d2, m2, v2))

    loss = sg[sg.shape[0] - 8, 0]
    return (loss, grad_x[None], *[out_g[n] for n in _ORDER], *[out_d[n] for n in _ORDER],
            *[out_m[n] for n in _ORDER], *[out_v[n] for n in _ORDER])
```

```python
import functools
import math

import jax
import jax.numpy as jnp
from jax import lax
from jax.experimental import pallas as pl
from jax.experimental.pallas import tpu as pltpu

F32 = jnp.float32
BF16 = jnp.bfloat16

D_MODEL = 1024
N_DEV = 8
IN_COLS = 7168
SHARD_IN = IN_COLS // N_DEV
HG_HEADS = 8
HG_DIM = 128
HG_CHUNK = 64
HG_SUPER_FWD = 256
HG_SUPER_BWD = 128
HG_HEADS_PER_STEP = 8
S5_WIDTH = 512
S5_GROUPS = 32
S5_STATE = 64
S5_CH = 16
S5_SEG = 8
S5_QUADS = 4
S5_QCOLS = 1024
S5_COLS = S5_QUADS * S5_QCOLS
S5_TILE_STEPS = 64
S5_UNROLL = 8
NORM_EPS = 1e-6
ADAM_LR = 0.001
ADAM_B1 = 0.9
ADAM_B2 = 0.999
ADAM_EPS = 1e-08
ADAM_WD = 0.01
ADAM_STEP = 10
MIB = 1024 * 1024
MESH = pl.DeviceIdType.MESH

COL_Q, COL_F, COL_I, COL_G, COL_US, COL_ZS, COL_GH, COL_GS = 0, 1024, 2048, 3072, 4096, 4608, 5120, 6144


def _call(body, **kw):
    return pl.pallas_call(body, **kw)


def _params(vmem_mb, n_grid=1):
    return pltpu.CompilerParams(
        dimension_semantics=("arbitrary",) * n_grid, vmem_limit_bytes=vmem_mb * MIB
    )


def _bdot(a, b):
    return jnp.dot(a.astype(BF16), b.astype(BF16), preferred_element_type=F32)


def _bdot_nt(a, b):
    return lax.dot_general(a.astype(BF16), b.astype(BF16), (((1,), (1,)), ((), ())), preferred_element_type=F32)


def _bdot_tn(a, b):
    return lax.dot_general(a.astype(BF16), b.astype(BF16), (((0,), (0,)), ((), ())), preferred_element_type=F32)


def _sigmoid(x):
    return jax.nn.sigmoid(x)


def _silu(x):
    return x * _sigmoid(x)


def _dsilu(x):
    s = _sigmoid(x)
    return s * (1.0 + x * (1.0 - s))


_GELU_C = math.sqrt(2.0 / math.pi)


def _gelu(x):
    return 0.5 * x * (1.0 + jnp.tanh(_GELU_C * (x + 0.044715 * x * x * x)))


def _dgelu(x):
    t = jnp.tanh(_GELU_C * (x + 0.044715 * x * x * x))
    return 0.5 * (1.0 + t) + 0.5 * x * (1.0 - t * t) * _GELU_C * (1.0 + 3.0 * 0.044715 * x * x)


def _rms_fwd(x, g):
    r = lax.rsqrt(jnp.mean(x * x, axis=-1, keepdims=True) + NORM_EPS)
    n = x * r
    return n * g, n, r


def _rms_bwd(dy, n, r, g):
    dn = dy * g
    dx = r * (dn - n * jnp.mean(dn * n, axis=-1, keepdims=True))
    return dx, jnp.sum(dy * n, axis=0, keepdims=True)


def _row_weight(packed_ref, k):
    return packed_ref[:, k * 128:(k + 1) * 128, :].reshape(D_MODEL, D_MODEL)


def _head_rms_fwd(o, g):
    ns, rs = [], []
    for h in range(HG_HEADS):
        oh = o[:, h * HG_DIM:(h + 1) * HG_DIM]
        r = lax.rsqrt(jnp.mean(oh * oh, axis=-1, keepdims=True) + NORM_EPS)
        ns.append(oh * r)
        rs.append(r)
    n = jnp.concatenate(ns, axis=1)
    return n * g, n, rs


def _head_rms_bwd(dy, n, rs, g):
    dn = dy * g
    dxs = []
    for h in range(HG_HEADS):
        sl = slice(h * HG_DIM, (h + 1) * HG_DIM)
        dxs.append(rs[h] * (dn[:, sl] - n[:, sl] * jnp.mean(dn[:, sl] * n[:, sl], axis=-1, keepdims=True)))
    return jnp.concatenate(dxs, axis=1), jnp.sum(dy * n, axis=0, keepdims=True)


def _rowwise(name, fn, n_rows, tm, rows, consts, out_rows, out_accs, vmem_mb, parts=1):
    n_r, n_c, n_or, n_oa = len(rows), len(consts), len(out_rows), len(out_accs)
    tp = tm // parts

    def body(*refs):
        r_refs = refs[:n_r]
        c_refs = refs[n_r:n_r + n_c]
        or_refs = refs[n_r + n_c:n_r + n_c + n_or]
        oa_refs = refs[n_r + n_c + n_or:]

        if n_oa:
            @pl.when(pl.program_id(0) == 0)
            def _():
                for ref in oa_refs:
                    ref[...] = jnp.zeros(ref.shape, ref.dtype)

        for part in range(parts):
            sl = slice(part * tp, (part + 1) * tp)
            outs, accs = fn([r[sl, :] for r in r_refs], c_refs, [o.at[sl, :] for o in or_refs])
            for ref, v in zip(or_refs, outs):
                if v is not None:
                    ref[sl, :] = v.astype(ref.dtype)
            for ref, v in zip(oa_refs, accs):
                ref[...] += v.astype(ref.dtype)

    in_specs = [pl.BlockSpec((tm, w), functools.partial(lambda i, c: (i, c), c=cb)) for (_, w, cb) in rows]
    in_specs += [pl.BlockSpec(c.shape, functools.partial(lambda i, nd: (0,) * nd, nd=c.ndim),
                              pipeline_mode=pl.Buffered(1)) for c in consts]
    out_specs = [pl.BlockSpec((tm, w), lambda i: (i, 0)) for (w, _) in out_rows]
    out_specs += [pl.BlockSpec(s, functools.partial(lambda i, nd: (0,) * nd, nd=len(s))) for (s, _) in out_accs]
    out_shape = [jax.ShapeDtypeStruct((n_rows, w), dt) for (w, dt) in out_rows]
    out_shape += [jax.ShapeDtypeStruct(s, dt) for (s, dt) in out_accs]
    res = _call(
        body, name=name, grid=(n_rows // tm,), in_specs=in_specs, out_specs=out_specs, out_shape=out_shape,
        compiler_params=_params(vmem_mb),
    )(*[a for (a, _, _) in rows], *consts)
    return res[:n_or], res[n_or:]


def _stage_branches(o, proj, ys, x, hg_norm_g, w_rows, w_glu, b_glu, w_o_s5):
    S = x.shape[0]

    def fn(rv, cr, out):
        o_b, ys_b, x_b = rv[0], rv[5], rv[6]
        g_hg, z_s, gate_hg, gate_s5 = (v.astype(F32) for v in rv[1:5])
        gn_ref, wrows_ref, wglu_ref, bglu_ref, wos5_ref = cr
        on, _, _ = _head_rms_fwd(o_b, gn_ref[...])
        a = on * _silu(g_hg)
        y_hg = jnp.dot(a.astype(BF16), _row_weight(wrows_ref, 0), preferred_element_type=F32)
        gl = _gelu(ys_b)
        glu = jnp.dot(gl.astype(BF16), wglu_ref[...], preferred_element_type=F32) + bglu_ref[...]
        ys2 = glu[:, :S5_WIDTH] * _sigmoid(glu[:, S5_WIDTH:]) * _silu(z_s)
        y_s5 = jnp.dot(ys2.astype(BF16), wos5_ref[...], preferred_element_type=F32)
        merged = _sigmoid(gate_hg) * y_hg + _sigmoid(gate_s5) * y_s5
        h1 = x_b + jnp.dot(merged.astype(BF16), _row_weight(wrows_ref, 1), preferred_element_type=F32)
        return [y_hg, y_s5, glu, h1], []

    rows = [(o, D_MODEL, 0), (proj, D_MODEL, COL_G // D_MODEL), (proj, S5_WIDTH, COL_ZS // S5_WIDTH),
            (proj, D_MODEL, COL_GH // D_MODEL), (proj, D_MODEL, COL_GS // D_MODEL), (ys, S5_WIDTH, 0), (x, D_MODEL, 0)]
    (y_hg, y_s5, glu, h1), _ = _rowwise(
        "branches", fn, S, 256, rows, [hg_norm_g, w_rows, w_glu, b_glu, w_o_s5],
        [(D_MODEL, BF16), (D_MODEL, BF16), (D_MODEL, BF16), (D_MODEL, F32)], [], 56)
    return y_hg, y_s5, glu, h1


def _stage_ple_loss(h1, p, target, ple_norm_g, w_ple, w_rows, final_norm_g):
    S = h1.shape[0]

    def fn(rv, cr, out):
        h1_b, p_b, t_b = rv
        gp_ref, wple_ref, wrows_ref, gf_ref = cr
        w_pg = _row_weight(wrows_ref, 2)
        n2g, n2, r2 = _rms_fwd(h1_b, gp_ref[...])
        z = jnp.dot(n2g.astype(BF16), w_pg, preferred_element_type=F32)
        gate = _sigmoid(z)
        pe = jnp.dot(p_b.astype(BF16), wple_ref[...], preferred_element_type=F32)
        h2 = h1_b + pe * gate
        y, nf, rf = _rms_fwd(h2, gf_ref[...])
        err = y - t_b
        loss_rows = 0.5 * jnp.mean(err * err, axis=-1, keepdims=True)
        loss_inc = jnp.broadcast_to(jnp.sum(loss_rows, axis=0, keepdims=True), (1, 128))
        dy = err * (1.0 / D_MODEL)
        dh2, d_gf = _rms_bwd(dy, nf, rf, gf_ref[...])
        d_pe = dh2 * gate
        dz = dh2 * pe * gate * (1.0 - gate)
        d_wple = _bdot_tn(p_b, d_pe)
        d_wpg = _bdot_tn(n2g, dz)
        dn2g = _bdot_nt(dz, w_pg)
        dh1n, d_gp = _rms_bwd(dn2g, n2, r2, gp_ref[...])
        return [dh2 + dh1n], [loss_inc, d_gf, d_gp, d_wple, d_wpg]

    (dh1,), accs = _rowwise(
        "ple_loss", fn, S, 512, [(h1, D_MODEL, 0), (p, 256, 0), (target, D_MODEL, 0)],
        [ple_norm_g, w_ple, w_rows, final_norm_g], [(D_MODEL, F32)],
        [((1, 128), F32), ((1, D_MODEL), F32), ((1, D_MODEL), F32), ((256, D_MODEL), F32), ((D_MODEL, D_MODEL), F32)], 56,
        parts=2)
    return dh1, accs


def _stage_bwd_merge(dh1, y_hg, y_s5, proj, w_rows):
    S = dh1.shape[0]

    def fn(rv, cr, out):
        dh1_b, yhg, ys5 = rv[:3]
        gate_hg, gate_s5 = rv[3].astype(F32), rv[4].astype(F32)
        (wrows_ref,) = cr
        sg_h, sg_s = _sigmoid(gate_hg), _sigmoid(gate_s5)
        merged = sg_h * yhg + sg_s * ys5
        d_wout = _bdot_tn(merged, dh1_b)
        d_merged = _bdot_nt(dh1_b, _row_weight(wrows_ref, 1))
        d_gate_hg = d_merged * yhg * sg_h * (1.0 - sg_h)
        d_gate_s5 = d_merged * ys5 * sg_s * (1.0 - sg_s)
        return [d_gate_hg, d_gate_s5, d_merged * sg_h, d_merged * sg_s], [d_wout]

    rows = [(dh1, D_MODEL, 0), (y_hg, D_MODEL, 0), (y_s5, D_MODEL, 0), (proj, D_MODEL, COL_GH // D_MODEL),
            (proj, D_MODEL, COL_GS // D_MODEL)]
    outs, accs = _rowwise("bwd_merge", fn, S, 512, rows, [w_rows], [(D_MODEL, BF16)] * 4,
                          [((D_MODEL, D_MODEL), F32)], 56, parts=2)
    return outs, accs


def _stage_bwd_hg_path(d_yhg, o, proj, hg_norm_g, w_rows):
    S = o.shape[0]

    def fn(rv, cr, out):
        d_yhg_b, o_b, g_hg = rv[0], rv[1], rv[2].astype(F32)
        gn_ref, wrows_ref = cr
        ong, on, rs = _head_rms_fwd(o_b, gn_ref[...])
        sil = _silu(g_hg)
        d_wohg = _bdot_tn(ong * sil, d_yhg_b)
        d_a = _bdot_nt(d_yhg_b, _row_weight(wrows_ref, 0))
        d_g_hg = d_a * ong * _dsilu(g_hg)
        d_o, d_gn = _head_rms_bwd(d_a * sil, on, rs, gn_ref[...])
        return [d_o, d_g_hg], [d_wohg, d_gn]

    rows = [(d_yhg, D_MODEL, 0), (o, D_MODEL, 0), (proj, D_MODEL, COL_G // D_MODEL)]
    outs, accs = _rowwise("bwd_hg_path", fn, S, 512, rows, [hg_norm_g, w_rows], [(D_MODEL, BF16)] * 2,
                          [((D_MODEL, D_MODEL), F32), ((1, D_MODEL), F32)], 56, parts=2)
    return outs, accs


def _stage_bwd_s5_path(d_ys5, ys, glu, proj, w_o_s5, w_glu):
    S = ys.shape[0]

    def fn(rv, cr, out):
        d_ys5_b, ys_b, glu_b, z_s = rv[0], rv[1], rv[2].astype(F32), rv[3].astype(F32)
        wos5_ref, wglu_ref = cr
        ga, gb = glu_b[:, :S5_WIDTH], glu_b[:, S5_WIDTH:]
        sgb, silz = _sigmoid(gb), _silu(z_s)
        ys2 = ga * sgb * silz
        d_wos5 = _bdot_tn(ys2, d_ys5_b)
        d_ys2 = _bdot_nt(d_ys5_b, wos5_ref[...])
        d_ga = d_ys2 * sgb * silz
        d_gb = d_ys2 * ga * sgb * (1.0 - sgb) * silz
        d_z = d_ys2 * ga * sgb * _dsilu(z_s)
        d_glu = jnp.concatenate([d_ga, d_gb], axis=1)
        gl = _gelu(ys_b)
        d_wglu = _bdot_tn(gl, d_glu)
        d_bglu = jnp.sum(d_glu, axis=0, keepdims=True)
        d_gl = _bdot_nt(d_glu, wglu_ref[...])
        return [d_gl * _dgelu(ys_b), d_z], [d_wos5, d_wglu, d_bglu]

    rows = [(d_ys5, D_MODEL, 0), (ys, S5_WIDTH, 0), (glu, D_MODEL, 0), (proj, S5_WIDTH, COL_ZS // S5_WIDTH)]
    outs, accs = _rowwise(
        "bwd_s5_path", fn, S, 512, rows, [w_o_s5, w_glu], [(S5_WIDTH, F32), (S5_WIDTH, BF16)],
        [((S5_WIDTH, D_MODEL), F32), ((S5_WIDTH, D_MODEL), F32), ((1, D_MODEL), F32)], 48, parts=2)
    return outs, accs


def _stage_inproj_bwd(pieces, x, dh1, norm_g, w_in_all):
    S = x.shape[0]

    def fn(rv, cr, out):
        g_ref, w_ref = cr
        x_b, dh1_b = rv[8], rv[9]
        dproj_ref = out[1]
        col = 0
        for v in rv[:8]:
            dproj_ref[:, col:col + v.shape[1]] = v.astype(BF16)
            col += v.shape[1]
        d_u = jnp.zeros((x_b.shape[0], D_MODEL), F32)
        for j in range(N_DEV):
            d_u = d_u + lax.dot_general(dproj_ref[:, j * SHARD_IN:(j + 1) * SHARD_IN], w_ref[j],
                                        (((1,), (1,)), ((), ())), preferred_element_type=F32)
        _, n, r = _rms_fwd(x_b, g_ref[...])
        dx, d_g = _rms_bwd(d_u, n, r, g_ref[...])
        return [dh1_b + dx, None], [d_g]

    rows = [(a, a.shape[1], 0) for a in pieces] + [(x, D_MODEL, 0), (dh1, D_MODEL, 0)]
    (grad_x, dproj), (d_g,) = _rowwise(
        "inproj_bwd", fn, S, 256, rows, [norm_g, w_in_all], [(D_MODEL, F32), (IN_COLS, BF16)],
        [((1, D_MODEL), F32)], 56)
    return grad_x, dproj, d_g


def _chunk_row(shape):
    return lax.broadcasted_iota(jnp.int32, shape, 0) & (HG_CHUNK - 1)


def _chunk_cumsum(x):
    r_in = _chunk_row(x.shape)
    s = 1
    while s < HG_CHUNK:
        x = x + jnp.where(r_in >= s, pltpu.roll(x, s, 0), 0.0)
        s *= 2
    return x


def _chunk_suffix_sum(x):
    n = x.shape[0]
    r_in = _chunk_row(x.shape)
    s = 1
    while s < HG_CHUNK:
        x = x + jnp.where(r_in < HG_CHUNK - s, pltpu.roll(x, n - s, 0), 0.0)
        s *= 2
    return x


def _hgrn_prep(q, fl, lb):
    sup = q.shape[0]
    nc = sup // HG_CHUNK
    sig = _sigmoid(fl)
    f = lb + (1.0 - lb) * sig
    k = (1.0 - lb) * (1.0 - sig)
    b = _chunk_cumsum(jnp.log(f))
    b3 = b.reshape(nc, HG_CHUNK, HG_DIM)
    row3 = lax.broadcasted_iota(jnp.int32, b3.shape, 1)
    pick = lambda r: jnp.sum(jnp.where(row3 == r, b3, 0.0), axis=1, keepdims=True)
    b_mid = pick(HG_CHUNK // 2 - 1)
    b_last = pick(HG_CHUNK - 1)
    flat = lambda t: t.reshape(sup, HG_DIM)
    e_qa = flat(jnp.exp(b3 - b_mid))
    e_ka = flat(jnp.exp(b_mid - b3))
    e_qd = jnp.exp(b)
    e_kd = flat(jnp.exp(b_last - b3))
    dc = jnp.exp(b_last)
    return sig, f, k, e_qa, e_ka, e_qd, e_kd, dc


def _hgrn_mask(sup):
    r = lax.broadcasted_iota(jnp.int32, (sup, sup), 0)
    c = lax.broadcasted_iota(jnp.int32, (sup, sup), 1)
    shift = HG_CHUNK.bit_length() - 1
    return (jnp.right_shift(r, shift) == jnp.right_shift(c, shift)) & (r >= c)


def _hgrn_fwd(proj, lb):
    S = proj.shape[0]
    sup = HG_SUPER_FWD
    nb = S // sup
    nc = sup // HG_CHUNK
    hp = HG_HEADS_PER_STEP
    wide = hp * HG_DIM

    def body(q_ref, f_ref, iv_ref, lb_ref, o_ref, st_ref, state):
        @pl.when(pl.program_id(1) == 0)
        def _():
            state[...] = jnp.zeros(state.shape, F32)

        mask = _hgrn_mask(sup)
        for hh in range(hp):
            lanes = slice(hh * HG_DIM, (hh + 1) * HG_DIM)
            q, iv = q_ref[:, lanes], iv_ref[:, lanes]
            _, _, k, e_qa, e_ka, e_qd, e_kd, dc = _hgrn_prep(q, f_ref[:, lanes], lb_ref[:, lanes])
            scores = jnp.where(mask, _bdot_nt(q * e_qa, k * e_ka), 0.0)
            o_intra = _bdot(scores, iv)
            qd, kd = q * e_qd, k * e_kd
            for c in range(nc):
                sl = slice(c * HG_CHUNK, (c + 1) * HG_CHUNK)
                st = state[hh]
                st_ref[hh, c] = st
                o_ref[sl, lanes] = o_intra[sl] + _bdot_nt(qd[sl], st)
                state[hh] = dc[c] * st + _bdot_tn(iv[sl], kd[sl])

    blk = lambda base: pl.BlockSpec((sup, wide), functools.partial(lambda h, i, b: (i, b + h), b=base // wide))
    return _call(
        body, name="hgrn_fwd", grid=(HG_HEADS // hp, nb),
        in_specs=[blk(COL_Q), blk(COL_F), blk(COL_I), pl.BlockSpec((1, wide), lambda h, i: (0, h))],
        out_specs=[pl.BlockSpec((sup, wide), lambda h, i: (i, h)),
                   pl.BlockSpec((hp, nc, HG_DIM, HG_DIM), lambda h, i: (h, i, 0, 0))],
        out_shape=[jax.ShapeDtypeStruct((S, D_MODEL), F32),
                   jax.ShapeDtypeStruct((HG_HEADS, S // HG_CHUNK, HG_DIM, HG_DIM), F32)],
        scratch_shapes=[pltpu.VMEM((hp, HG_DIM, HG_DIM), F32)],
        compiler_params=_params(40, 2),
    )(proj, proj, proj, lb)


def _hgrn_bwd(proj, lb, d_o, states):
    S = proj.shape[0]
    sup = HG_SUPER_BWD
    nb = S // sup
    nc = sup // HG_CHUNK
    hp = HG_HEADS_PER_STEP
    wide = hp * HG_DIM

    def body(q_ref, f_ref, iv_ref, lb_ref, do_ref, st_ref, dq_ref, df_ref, div_ref, dlb_ref, dstate):
        @pl.when(pl.program_id(1) == 0)
        def _():
            dstate[...] = jnp.zeros(dstate.shape, F32)
            dlb_ref[...] = jnp.zeros(dlb_ref.shape, F32)

        mask = _hgrn_mask(sup)
        for hh in range(hp):
            lanes = slice(hh * HG_DIM, (hh + 1) * HG_DIM)
            q, iv, do, lb_v = q_ref[:, lanes], iv_ref[:, lanes], do_ref[:, lanes], lb_ref[:, lanes]
            sig, f, k, e_qa, e_ka, e_qd, e_kd, dc = _hgrn_prep(q, f_ref[:, lanes], lb_v)
            qa, ka, qd, kd = q * e_qa, k * e_ka, q * e_qd, k * e_kd
            scores = jnp.where(mask, _bdot_nt(qa, ka), 0.0)
            d_scores = jnp.where(mask, _bdot_nt(do, iv), 0.0)
            d_iv_intra = _bdot_tn(scores, do)
            d_qa = _bdot(d_scores, ka)
            d_ka = _bdot_tn(d_scores, qa)
            d_qd, d_kd, d_last = [None] * nc, [None] * nc, [None] * nc
            for c in reversed(range(nc)):
                sl = slice(c * HG_CHUNK, (c + 1) * HG_CHUNK)
                st = st_ref[hh, c]
                ds = dstate[hh]
                d_qd[c] = _bdot(do[sl], st)
                d_kd[c] = _bdot(iv[sl], ds)
                div_ref[sl, lanes] = (d_iv_intra[sl] + _bdot_nt(kd[sl], ds)).astype(div_ref.dtype)
                d_last[c] = (jnp.sum(ds * st, axis=0, keepdims=True) * dc[c]
                             + jnp.sum(d_kd[c] * kd[sl], axis=0, keepdims=True))
                dstate[hh] = dc[c] * ds + _bdot_tn(do[sl], qd[sl])
            d_qd = jnp.concatenate(d_qd, axis=0)
            d_kd = jnp.concatenate(d_kd, axis=0)
            d_b = d_qa * qa - d_ka * ka + d_qd * qd - d_kd * kd
            last_rows = jnp.concatenate([jnp.broadcast_to(t, (HG_CHUNK, HG_DIM)) for t in d_last], axis=0)
            d_b = d_b + jnp.where(_chunk_row(d_b.shape) == HG_CHUNK - 1, last_rows, 0.0)
            d_logf = _chunk_suffix_sum(d_b)
            d_k = d_ka * e_ka + d_kd * e_kd
            g_f = d_logf / f
            d_sig = (g_f - d_k) * (1.0 - lb_v)
            dq_ref[:, lanes] = (d_qa * e_qa + d_qd * e_qd).astype(dq_ref.dtype)
            df_ref[:, lanes] = (d_sig * sig * (1.0 - sig)).astype(df_ref.dtype)
            d_lb = jnp.sum((g_f - d_k) * (1.0 - sig), axis=0, keepdims=True)
            dlb_ref[:, lanes] += jnp.broadcast_to(d_lb, (8, HG_DIM))

    rev = lambda i: nb - 1 - i
    blk = lambda base: pl.BlockSpec((sup, wide), functools.partial(lambda h, i, b: (rev(i), b + h), b=base // wide))
    row_out = pl.BlockSpec((sup, wide), lambda h, i: (rev(i), h))
    dq, df, div, dlb = _call(
        body, name="hgrn_bwd", grid=(HG_HEADS // hp, nb),
        in_specs=[blk(COL_Q), blk(COL_F), blk(COL_I), pl.BlockSpec((1, wide), lambda h, i: (0, h)),
                  pl.BlockSpec((sup, wide), lambda h, i: (rev(i), h)),
                  pl.BlockSpec((hp, nc, HG_DIM, HG_DIM), lambda h, i: (h, rev(i), 0, 0))],
        out_specs=[row_out, row_out, row_out, pl.BlockSpec((8, wide), lambda h, i: (0, h))],
        out_shape=[jax.ShapeDtypeStruct((S, D_MODEL), BF16)] * 3 + [jax.ShapeDtypeStruct((8, D_MODEL), F32)],
        scratch_shapes=[pltpu.VMEM((hp, HG_DIM, HG_DIM), F32)],
        compiler_params=_params(40, 2),
    )(proj, proj, proj, lb, d_o, states)
    return dq, df, div, dlb[0:1]


def _s5_matrices(a_re, a_im, log_dt, b_re, b_im, c_re, c_im, d, seg_len):
    dt = jnp.exp(log_dt)[:, None]
    mag = jnp.exp(a_re * dt)
    lr, li = mag * jnp.cos(a_im * dt), mag * jnp.sin(a_im * dt)
    den = a_re * a_re + a_im * a_im
    nr = lr - 1.0
    sr = (nr * a_re + li * a_im) / den
    si = (li * a_re - nr * a_im) / den
    bbr = sr[..., None] * b_re - si[..., None] * b_im
    bbi = sr[..., None] * b_im + si[..., None] * b_re
    eye = jnp.eye(8, dtype=F32)

    def quad_cols(v):
        return v.reshape(S5_QUADS, 8 * S5_STATE)

    def lam_row(re_part, im_part):
        row = jnp.concatenate([quad_cols(re_part), quad_cols(im_part)], axis=1).reshape(1, S5_COLS)
        return jnp.broadcast_to(row, (S5_SEG, S5_COLS))

    def b_mat(bb):
        t = bb.reshape(S5_QUADS, 8, S5_STATE, S5_CH)
        return jnp.einsum("qgnc,gh->qgchn", t, eye).reshape(S5_QUADS, 8 * S5_CH, 8 * S5_STATE)

    def c_mat(cc):
        t = cc.reshape(S5_QUADS, 8, S5_CH, S5_STATE)
        return jnp.einsum("qgcn,gh->qgnhc", t, eye).reshape(S5_QUADS, 8 * S5_STATE, 8 * S5_CH)

    ang = a_im * dt * seg_len
    magp = jnp.exp(a_re * dt * seg_len)
    lpr, lpi = magp * jnp.cos(ang), magp * jnp.sin(ang)
    return dict(
        lam_r=lam_row(lr, lr), lam_i=lam_row(-li, li),
        b_q=jnp.concatenate([b_mat(bbr), b_mat(bbi)], axis=2),
        c_q=jnp.concatenate([c_mat(c_re), -c_mat(c_im)], axis=1),
        d_row=d.reshape(1, S5_WIDTH), pow_r=quad_cols(lpr), pow_i=quad_cols(lpi),
    )


def _s5_parts(v):
    half = S5_QCOLS // 2
    return tuple(v[:, k * half:(k + 1) * half] for k in range(2 * S5_QUADS))


def _s5_advance(parts, lr_ref, li_ref, x_ref, sl, conj):
    half = S5_QCOLS // 2
    out = []
    for q in range(S5_QUADS):
        re_c = slice(q * S5_QCOLS, q * S5_QCOLS + half)
        im_c = slice(q * S5_QCOLS + half, (q + 1) * S5_QCOLS)
        lr, li = lr_ref[:, re_c], li_ref[:, im_c]
        hr, hi = parts[2 * q], parts[2 * q + 1]
        if conj:
            out += [lr * hr + li * hi + x_ref[sl, re_c], lr * hi - li * hr + x_ref[sl, im_c]]
        else:
            out += [lr * hr - li * hi + x_ref[sl, re_c], lr * hi + li * hr + x_ref[sl, im_c]]
    return tuple(out)


def _scan_loop(step, init):
    def trip(o, carry):
        for j in range(S5_UNROLL):
            carry = step(o * S5_UNROLL + j, carry)
        return carry

    return lax.fori_loop(0, S5_TILE_STEPS // S5_UNROLL, trip, init)


def _s5_store(ref, sl, parts):
    half = S5_QCOLS // 2
    for k, v in enumerate(parts):
        ref[sl, k * half:(k + 1) * half] = v


def _s5_fwd_pass(u_perm, mats, h0, with_output):
    S = u_perm.shape[0]
    rows = S5_TILE_STEPS * S5_SEG
    nt = S // rows

    def body(*refs):
        if with_output:
            u_ref, b_ref, lr_ref, li_ref, h0_ref, c_ref, d_ref, y_ref, hinit_ref, hend_ref, xs, hcar = refs
        else:
            u_ref, b_ref, lr_ref, li_ref, h0_ref, hend_ref, xs, hcar = refs

        @pl.when(pl.program_id(0) == 0)
        def _():
            hcar[...] = h0_ref[...]

        if with_output:
            hinit_ref[...] = hcar[...]
        u = u_ref[...]
        ub = u.astype(BF16)
        for q in range(S5_QUADS):
            xs[:, q * S5_QCOLS:(q + 1) * S5_QCOLS] = jnp.dot(ub[:, q * 128:(q + 1) * 128], b_ref[q], preferred_element_type=F32)

        def step(t, h):
            sl = pl.ds(pl.multiple_of(t * S5_SEG, S5_SEG), S5_SEG)
            hn = _s5_advance(h, lr_ref, li_ref, xs, sl, False)
            _s5_store(xs, sl, hn)
            return hn

        h = _scan_loop(step, _s5_parts(hcar[...]))
        _s5_store(hcar, slice(None), h)
        _s5_store(hend_ref, slice(None), h)
        if with_output:
            ys = [jnp.dot(xs[:, q * S5_QCOLS:(q + 1) * S5_QCOLS].astype(BF16), c_ref[q], preferred_element_type=F32)
                  for q in range(S5_QUADS)]
            y_ref[...] = jnp.concatenate(ys, axis=1) + d_ref[...] * u

    full = lambda a: pl.BlockSpec(a.shape, functools.partial(lambda i, nd: (0,) * nd, nd=a.ndim))
    ins = [u_perm, mats["b_q"], mats["lam_r"], mats["lam_i"], h0]
    in_specs = [pl.BlockSpec((rows, S5_WIDTH), lambda i: (i, 0))] + [full(a) for a in ins[1:]]
    out_specs = [pl.BlockSpec((S5_SEG, S5_COLS), lambda i: (0, 0))]
    out_shape = [jax.ShapeDtypeStruct((S5_SEG, S5_COLS), F32)]
    if with_output:
        ins += [mats["c_q"], mats["d_row"]]
        in_specs += [full(mats["c_q"]), full(mats["d_row"])]
        out_specs = [pl.BlockSpec((rows, S5_WIDTH), lambda i: (i, 0)),
                     pl.BlockSpec((None, S5_SEG, S5_COLS), lambda i: (i, 0, 0))] + out_specs
        out_shape = [jax.ShapeDtypeStruct((S, S5_WIDTH), F32), jax.ShapeDtypeStruct((nt, S5_SEG, S5_COLS), F32)] + out_shape
    return _call(
        body, name="s5_fwd_y" if with_output else "s5_fwd_ends", grid=(nt,), in_specs=in_specs, out_specs=out_specs,
        out_shape=out_shape,
        scratch_shapes=[pltpu.VMEM((rows, S5_COLS), F32), pltpu.VMEM((S5_SEG, S5_COLS), F32)],
        compiler_params=_params(40),
    )(*ins)


def _s5_bwd_ends(dy_perm, mats):
    S = dy_perm.shape[0]
    rows = S5_TILE_STEPS * S5_SEG
    nt = S // rows

    def body(dy_ref, c_ref, lr_ref, li_ref, gend_ref, gs, gcar):
        @pl.when(pl.program_id(0) == 0)
        def _():
            gcar[...] = jnp.zeros(gcar.shape, F32)

        dyb = dy_ref[...].astype(BF16)
        for q in range(S5_QUADS):
            gs[:, q * S5_QCOLS:(q + 1) * S5_QCOLS] = lax.dot_general(
                dyb[:, q * 128:(q + 1) * 128], c_ref[q], (((1,), (1,)), ((), ())), preferred_element_type=F32)

        def step(k, g):
            t = S5_TILE_STEPS - 1 - k
            sl = pl.ds(pl.multiple_of(t * S5_SEG, S5_SEG), S5_SEG)
            return _s5_advance(g, lr_ref, li_ref, gs, sl, True)

        g = _scan_loop(step, _s5_parts(gcar[...]))
        _s5_store(gcar, slice(None), g)
        _s5_store(gend_ref, slice(None), g)

    full = lambda a: pl.BlockSpec(a.shape, functools.partial(lambda i, nd: (0,) * nd, nd=a.ndim))
    return _call(
        body, name="s5_bwd_ends", grid=(nt,),
        in_specs=[pl.BlockSpec((rows, S5_WIDTH), lambda i: (nt - 1 - i, 0)), full(mats["c_q"]), full(mats["lam_r"]),
                  full(mats["lam_i"])],
        out_specs=pl.BlockSpec((S5_SEG, S5_COLS), lambda i: (0, 0)),
        out_shape=jax.ShapeDtypeStruct((S5_SEG, S5_COLS), F32),
        scratch_shapes=[pltpu.VMEM((rows, S5_COLS), F32), pltpu.VMEM((S5_SEG, S5_COLS), F32)],
        compiler_params=_params(40),
    )(dy_perm, mats["c_q"], mats["lam_r"], mats["lam_i"])


def _s5_bwd_full(u_perm, dy_perm, hinit, g0, mats):
    S = u_perm.shape[0]
    rows = S5_TILE_STEPS * S5_SEG
    nt = S // rows

    def body(u_ref, dy_ref, hinit_ref, g0_ref, b_ref, c_ref, lr_ref, li_ref, d_ref,
             du_ref, dp_ref, dq_ref, db_ref, dc_ref, dd_ref, hs, gs, gcar):
        @pl.when(pl.program_id(0) == 0)
        def _():
            gcar[...] = g0_ref[...]
            for ref in (dp_ref, dq_ref, db_ref, dc_ref, dd_ref):
                ref[...] = jnp.zeros(ref.shape, F32)

        u, dy = u_ref[...], dy_ref[...]
        ub, dyb = u.astype(BF16), dy.astype(BF16)
        hs[0:S5_SEG, :] = hinit_ref[...]
        for q in range(S5_QUADS):
            cols = slice(q * S5_QCOLS, (q + 1) * S5_QCOLS)
            hs[S5_SEG:, cols] = jnp.dot(ub[:, q * 128:(q + 1) * 128], b_ref[q], preferred_element_type=F32)
            gs[:, cols] = lax.dot_general(dyb[:, q * 128:(q + 1) * 128], c_ref[q], (((1,), (1,)), ((), ())),
                                          preferred_element_type=F32)

        def fstep(t, h):
            sl = pl.ds(pl.multiple_of((t + 1) * S5_SEG, S5_SEG), S5_SEG)
            hn = _s5_advance(h, lr_ref, li_ref, hs, sl, False)
            _s5_store(hs, sl, hn)
            return hn

        _scan_loop(fstep, _s5_parts(hinit_ref[...]))

        def bstep(k, g):
            t = S5_TILE_STEPS - 1 - k
            sl = pl.ds(pl.multiple_of(t * S5_SEG, S5_SEG), S5_SEG)
            gn = _s5_advance(g, lr_ref, li_ref, gs, sl, True)
            _s5_store(gs, sl, gn)
            return gn

        _s5_store(gcar, slice(None), _scan_loop(bstep, _s5_parts(gcar[...])))

        half = S5_QCOLS // 2
        dus = []
        for q in range(S5_QUADS):
            cols = slice(q * S5_QCOLS, (q + 1) * S5_QCOLS)

            def astep(t, carry, q=q):
                sl = pl.ds(pl.multiple_of(t * S5_SEG, S5_SEG), S5_SEG)
                g = gs[sl, q * S5_QCOLS:(q + 1) * S5_QCOLS]
                hp = hs[sl, q * S5_QCOLS:(q + 1) * S5_QCOLS]
                hp_sw = jnp.concatenate([hp[:, half:], hp[:, :half]], axis=1)
                return carry[0] + g * hp, carry[1] + g * hp_sw

            zero = jnp.zeros((S5_SEG, S5_QCOLS), F32)
            acc_p, acc_q = _scan_loop(astep, (zero, zero))
            dp_ref[:, cols] += jnp.sum(acc_p, axis=0, keepdims=True)
            dq_ref[:, cols] += jnp.sum(acc_q, axis=0, keepdims=True)
            gq = gs[:, cols].astype(BF16)
            db_ref[q] += lax.dot_general(ub[:, q * 128:(q + 1) * 128], gq, (((0,), (0,)), ((), ())),
                                         preferred_element_type=F32)
            hq = hs[S5_SEG:, cols].astype(BF16)
            dc_ref[q] += lax.dot_general(dyb[:, q * 128:(q + 1) * 128], hq, (((0,), (0,)), ((), ())),
                                         preferred_element_type=F32)
            dus.append(lax.dot_general(gq, b_ref[q], (((1,), (1,)), ((), ())), preferred_element_type=F32))
        du_ref[...] = (jnp.concatenate(dus, axis=1) + d_ref[...] * dy).astype(du_ref.dtype)
        dd_ref[...] += jnp.sum(dy * u, axis=0, keepdims=True)

    full = lambda a: pl.BlockSpec(a.shape, functools.partial(lambda i, nd: (0,) * nd, nd=a.ndim))
    rev_rows = pl.BlockSpec((rows, S5_WIDTH), lambda i: (nt - 1 - i, 0))
    consts = [mats["b_q"], mats["c_q"], mats["lam_r"], mats["lam_i"], mats["d_row"]]
    acc = lambda s: pl.BlockSpec(s, functools.partial(lambda i, nd: (0,) * nd, nd=len(s)))
    acc_shapes = [(1, S5_COLS), (1, S5_COLS), (S5_QUADS, 128, S5_QCOLS), (S5_QUADS, 128, S5_QCOLS), (1, S5_WIDTH)]
    return _call(
        body, name="s5_bwd_full", grid=(nt,),
        in_specs=[rev_rows, rev_rows, pl.BlockSpec((None, S5_SEG, S5_COLS), lambda i: (nt - 1 - i, 0, 0)), full(g0)]
        + [full(a) for a in consts],
        out_specs=[rev_rows] + [acc(s) for s in acc_shapes],
        out_shape=[jax.ShapeDtypeStruct((S, S5_WIDTH), BF16)] + [jax.ShapeDtypeStruct(s, F32) for s in acc_shapes],
        scratch_shapes=[pltpu.VMEM((rows + S5_SEG, S5_COLS), F32), pltpu.VMEM((rows, S5_COLS), F32),
                        pltpu.VMEM((S5_SEG, S5_COLS), F32)],
        compiler_params=_params(56),
    )(u_perm, dy_perm, hinit, g0, *consts)


def _cmul(ar, ai, br, bi):
    return ar * br - ai * bi, ar * bi + ai * br


def _split_cols(v):
    t = v.reshape(v.shape[0], S5_QUADS, 2, S5_QCOLS // 2)
    return t[:, :, 0], t[:, :, 1]


def _join_cols(re, im):
    return jnp.stack([re, im], axis=2).reshape(re.shape[0], S5_COLS)


def _segment_starts(ends, pow_r, pow_i, reverse):
    er, ei = _split_cols(ends)
    pi = -pow_i if reverse else pow_i
    order = list(range(S5_SEG))
    if reverse:
        order = order[::-1]
    cr, ci = jnp.zeros_like(er[0]), jnp.zeros_like(ei[0])
    out_r, out_i = [None] * S5_SEG, [None] * S5_SEG
    for j in order:
        out_r[j], out_i[j] = cr, ci
        mr, mi = _cmul(pow_r, pi, cr, ci)
        cr, ci = mr + er[j], mi + ei[j]
    return _join_cols(jnp.stack(out_r), jnp.stack(out_i))


def _to_segments(a):
    S, w = a.shape
    return a.reshape(S5_SEG, S // S5_SEG, w).transpose(1, 0, 2).reshape(S, w)


def _from_segments(a):
    S, w = a.shape
    return a.reshape(S // S5_SEG, S5_SEG, w).transpose(1, 0, 2).reshape(S, w)


def _my_pos():
    return lax.axis_index("x"), lax.axis_index("y"), lax.axis_index("c")


def _flip(pos, k):
    x, y, c = pos
    return (1 - x if k & 4 else x, 1 - y if k & 2 else y, 1 - c if k & 1 else c)


def _index_of(pos):
    return 4 * pos[0] + 2 * pos[1] + pos[2]


_GATHER_FLIPS = (0, 1, 4, 5, 2, 3, 6, 7)


def _inproj_gather(x, norm_g, pack_a, pack_b, pack_c):
    S = x.shape[0]
    tm = min(S, 1024)
    n_i = S // tm
    order = jnp.stack([_index_of(_flip(_my_pos(), k)) for k in _GATHER_FLIPS]).astype(jnp.int32)

    def body(order_ref, x_ref, g_ref, pa_ref, pb_ref, pc_ref, ut_ref, proj_ref, proj16_ref, oa_ref, ob_ref, oc_ref,
             wv, u_scr, send_sems, recv_sems, local_sems):
        s, i = pl.program_id(0), pl.program_id(1)
        me = _my_pos()
        mine = _index_of(me)
        sibling = _flip(me, 1)
        srcs = (pb_ref, pa_ref, pc_ref)
        dsts = (wv, oa_ref, oc_ref)

        def direct(a, k):
            return pltpu.make_async_remote_copy(
                src_ref=srcs[a], dst_ref=dsts[a].at[mine], send_sem=send_sems.at[a * 8 + k],
                recv_sem=recv_sems.at[a * 8 + k], device_id=_flip(me, k), device_id_type=MESH)

        def passed_on(a, k):
            slot = _index_of(_flip(me, k))
            return pltpu.make_async_remote_copy(
                src_ref=dsts[a].at[slot], dst_ref=dsts[a].at[slot], send_sem=send_sems.at[a * 8 + (k | 1)],
                recv_sem=recv_sems.at[a * 8 + (k | 1)], device_id=sibling, device_id_type=MESH)

        def arrival(a, k):
            slot = _index_of(_flip(me, k))
            pltpu.make_async_remote_copy(
                src_ref=dsts[a].at[slot], dst_ref=dsts[a].at[slot], send_sem=send_sems.at[a * 8 + k],
                recv_sem=recv_sems.at[a * 8 + k], device_id=me, device_id_type=MESH).wait_recv()

        def own_copy(a):
            return pltpu.make_async_copy(srcs[a], dsts[a].at[mine], local_sems.at[a])

        def keep(idx):
            slot = _index_of(_flip(me, _GATHER_FLIPS[idx]))
            return pltpu.make_async_copy(wv.at[slot], ob_ref.at[slot], local_sems.at[3 + idx])


        first = (s == 0) & (i == 0)

        @pl.when(first)
        def _():
            for a in range(3):
                own_copy(a).start()
            for k in (1, 4, 2):
                direct(0, k).start()
            own_copy(0).wait()
            keep(0).start()

        for idx, k in enumerate(_GATHER_FLIPS):
            if idx == 0:
                continue

            @pl.when((s == idx) & (i == 0))
            def _(idx=idx, k=k):
                arrival(0, k)
                if k in (4, 2, 6):
                    passed_on(0, k).start()
                keep(idx).start()
                if idx == 1:
                    direct(0, 6).start()
                if idx == 2:
                    for a in (1, 2):
                        for k in (1, 4, 2, 6):
                            direct(a, k).start()

        @pl.when(s == 0)
        def _():
            y, _, _ = _rms_fwd(x_ref[...], g_ref[...])
            u_scr[pl.ds(pl.multiple_of(i * tm, tm), tm), :] = y.astype(BF16)
            ut_ref[...] = y.T.astype(BF16)

        ub = u_scr[pl.ds(pl.multiple_of(i * tm, tm), tm), :]
        block = jnp.dot(ub, wv[order_ref[s]], preferred_element_type=F32)
        proj_ref[...] = block
        proj16_ref[...] = block.astype(BF16)

        @pl.when((s == N_DEV - 1) & (i == n_i - 1))
        def _():
            for a in (1, 2):
                for k in (4, 2, 6):
                    arrival(a, k)
                    passed_on(a, k).start()
            for a in (1, 2):
                for k in (1, 5, 3, 7):
                    arrival(a, k)
                own_copy(a).wait()
            for a in range(3):
                for k in (1, 4, 2, 6):
                    direct(a, k).wait_send()
                for k in (4, 2, 6):
                    passed_on(a, k).wait_send()
            for idx in range(N_DEV):
                keep(idx).wait()

    any_spec = pl.BlockSpec(memory_space=pl.ANY)
    vmem = pl.BlockSpec(memory_space=pltpu.VMEM)
    grid_spec = pltpu.PrefetchScalarGridSpec(
        num_scalar_prefetch=1, grid=(N_DEV, n_i),
        in_specs=[pl.BlockSpec((tm, D_MODEL), lambda s, i, o: (jnp.where(s == 0, i, 0), 0)),
                  pl.BlockSpec((1, D_MODEL), lambda s, i, o: (0, 0)), any_spec, vmem, any_spec],
        out_specs=[pl.BlockSpec((D_MODEL, tm), lambda s, i, o: (0, jnp.where(s == 0, i, n_i - 1))),
                   pl.BlockSpec((tm, SHARD_IN), lambda s, i, o: (i, o[s])),
                   pl.BlockSpec((tm, SHARD_IN), lambda s, i, o: (i, o[s])), any_spec, any_spec, any_spec],
        scratch_shapes=[pltpu.VMEM((N_DEV,) + pack_b.shape, BF16), pltpu.VMEM((S, D_MODEL), BF16),
                        pltpu.SemaphoreType.DMA((24,)), pltpu.SemaphoreType.DMA((24,)), pltpu.SemaphoreType.DMA((3 + N_DEV,))],
    )
    return _call(
        body, name="inproj_gather", grid_spec=grid_spec,
        out_shape=[jax.ShapeDtypeStruct((D_MODEL, S), BF16), jax.ShapeDtypeStruct((S, IN_COLS), F32),
                   jax.ShapeDtypeStruct((S, IN_COLS), BF16), jax.ShapeDtypeStruct((N_DEV,) + pack_a.shape, BF16), jax.ShapeDtypeStruct((N_DEV,) + pack_b.shape, BF16),
                   jax.ShapeDtypeStruct((N_DEV,) + pack_c.shape, BF16)],
        compiler_params=_params(60, 2),
    )(order, x, norm_g, pack_a, pack_b, pack_c)


_SCATTER_FLIPS = (7, 6, 5, 4, 3, 2, 1, 0)
_N_CHIPS = 4


def _for_row_chunks(n_rows, chunk, fn):
    def step(c, carry):
        fn(pl.ds(pl.multiple_of(c * chunk, chunk), chunk))
        return carry

    lax.fori_loop(0, n_rows // chunk, step, 0)


def _grad_w_in_scatter(dproj, u_t, rs_a, rs_c, small_partial):
    S = u_t.shape[1]
    tm = min(S, 1024)
    n_i = S // tm
    order = jnp.stack([_index_of(_flip(_my_pos(), k)) for k in _SCATTER_FLIPS]).astype(jnp.int32)
    shapes = ((D_MODEL, SHARD_IN), rs_a.shape[1:], rs_c.shape[1:])
    row_chunk = 128

    def body(order_ref, dp_ref, ut_ref, ra_ref, rc_ref, p_ref, gb_ref, ga_ref, gc_ref, gs_ref, acc, sib_b, d2d_b,
             send_b, ici_b, mine_a, sib_a, ici_a, mine_c, sib_c, ici_c, gath, send_sems, recv_sems, local_sems):
        s, i = pl.program_id(0), pl.program_id(1)
        me = _my_pos()
        sibling = _flip(me, 1)

        my_chip = 2 * me[0] + me[1]

        small_d2d = pltpu.make_async_remote_copy(
            src_ref=p_ref, dst_ref=gath.at[_N_CHIPS], send_sem=send_sems.at[21], recv_sem=recv_sems.at[21],
            device_id=sibling, device_id_type=MESH)

        def small_ici(m):
            return pltpu.make_async_remote_copy(
                src_ref=gath.at[my_chip], dst_ref=gath.at[my_chip], send_sem=send_sems.at[22 + m],
                recv_sem=recv_sems.at[22 + m], device_id=_flip(me, 6 - 2 * m), device_id_type=MESH)
        sib = (sib_b, sib_a, sib_c)
        ici = (ici_b, ici_a, ici_c)
        outs = (gb_ref, ga_ref, gc_ref)

        def to_sibling(arr, m, src):
            return pltpu.make_async_remote_copy(
                src_ref=src, dst_ref=sib[arr].at[m], send_sem=send_sems.at[arr * 7 + m],
                recv_sem=recv_sems.at[arr * 7 + m], device_id=sibling, device_id_type=MESH)

        def over_ici(arr, m, src):
            return pltpu.make_async_remote_copy(
                src_ref=src, dst_ref=ici[arr].at[m], send_sem=send_sems.at[arr * 7 + 4 + m],
                recv_sem=recv_sems.at[arr * 7 + 4 + m], device_id=_flip(me, 6 - 2 * m), device_id_type=MESH)

        def from_sibling(arr, m):
            to_sibling(arr, m, sib[arr].at[m]).wait_recv()

        def from_ici(arr, m):
            over_ici(arr, m, ici[arr].at[m]).wait_recv()

        small = ((1, ra_ref, mine_a), (2, rc_ref, mine_c))

        def local_copy(arr, src, mine, m):
            return pltpu.make_async_copy(src.at[_index_of(_flip(me, 6 - 2 * m))], mine.at[m],
                                         local_sems.at[(arr - 1) * _N_CHIPS + m])

        @pl.when((s == 0) & (i == 0))
        def _():
            small_d2d.start()
            for arr, src, mine in small:
                for m in range(_N_CHIPS):
                    to_sibling(arr, m, src.at[_index_of(_flip(me, 7 - 2 * m))]).start()
                    local_copy(arr, src, mine, m).start()

        @pl.when((s == 1) & (i == 0))
        def _():
            small_d2d.wait_recv()
            gath[my_chip] = p_ref[...] + gath[_N_CHIPS]
            for m in range(_N_CHIPS - 1):
                small_ici(m).start()
            for arr, src, mine in small:
                rows, chunk = shapes[arr][0], 16
                for m in range(_N_CHIPS):
                    local_copy(arr, src, mine, m).wait()
                    from_sibling(arr, m)
                    if m < _N_CHIPS - 1:
                        def add(sl, arr=arr, mine=mine, m=m):
                            mine[m, sl, :] = (mine[m, sl, :].astype(F32) + sib[arr][m, sl, :].astype(F32)).astype(BF16)

                        _for_row_chunks(rows, chunk, add)
                        over_ici(arr, m, mine.at[m]).start()
                    else:
                        def keep(sl, arr=arr, mine=mine, m=m):
                            outs[arr][sl, :] = mine[m, sl, :].astype(F32) + sib[arr][m, sl, :].astype(F32)

                        _for_row_chunks(rows, chunk, keep)

        @pl.when(i == 0)
        def _():
            acc[...] = jnp.zeros(acc.shape, F32)

        acc[...] += jnp.dot(ut_ref[...], dp_ref[...], preferred_element_type=F32)

        def block_rows(c):
            return acc[c * row_chunk:(c + 1) * row_chunk, :]

        for m in range(_N_CHIPS):
            @pl.when((s == 2 * m) & (i == n_i - 1))
            def _(m=m):
                if m > 0:
                    to_sibling(0, m - 1, d2d_b).wait_send()
                for c in range(D_MODEL // row_chunk):
                    d2d_b[c * row_chunk:(c + 1) * row_chunk, :] = block_rows(c).astype(BF16)
                to_sibling(0, m, d2d_b).start()

            @pl.when((s == 2 * m + 1) & (i == n_i - 1))
            def _(m=m):
                from_sibling(0, m)
                slot = m % 2
                if m == 2:
                    over_ici(0, 0, send_b.at[0]).wait_send()
                for c in range(D_MODEL // row_chunk):
                    rows = slice(c * row_chunk, (c + 1) * row_chunk)
                    total = block_rows(c) + sib_b[m, rows, :].astype(F32)
                    if m < _N_CHIPS - 1:
                        send_b[slot, rows, :] = total.astype(BF16)
                    else:
                        gb_ref[rows, :] = total
                if m < _N_CHIPS - 1:
                    over_ici(0, m, send_b.at[slot]).start()

        @pl.when((s == N_DEV - 1) & (i == n_i - 1))
        def _():
            for arr in range(3):
                for m in range(_N_CHIPS - 1):
                    from_ici(arr, m)
                rows = shapes[arr][0]

                def add(sl, arr=arr):
                    outs[arr][sl, :] = (outs[arr][sl, :] + ici[arr][0, sl, :].astype(F32)
                                        + ici[arr][1, sl, :].astype(F32) + ici[arr][2, sl, :].astype(F32))

                _for_row_chunks(rows, 16, add)
            for m in range(_N_CHIPS - 1):
                small_ici(m).wait_recv()
            gs_ref[...] = (gath[0] + gath[1]) + (gath[2] + gath[3])
            small_d2d.wait_send()
            for m in range(_N_CHIPS - 1):
                small_ici(m).wait_send()
            to_sibling(0, _N_CHIPS - 1, d2d_b).wait_send()
            over_ici(0, 1, send_b.at[1]).wait_send()
            over_ici(0, 2, send_b.at[0]).wait_send()
            for arr, src, mine in small:
                for m in range(_N_CHIPS):
                    to_sibling(arr, m, src.at[0]).wait_send()
                for m in range(_N_CHIPS - 1):
                    over_ici(arr, m, mine.at[m]).wait_send()

    any_spec = pl.BlockSpec(memory_space=pl.ANY)
    vmem = pl.BlockSpec(memory_space=pltpu.VMEM)
    half = lambda shp, n: pltpu.VMEM((n,) + tuple(shp), BF16)
    grid_spec = pltpu.PrefetchScalarGridSpec(
        num_scalar_prefetch=1, grid=(N_DEV, n_i),
        in_specs=[pl.BlockSpec((tm, SHARD_IN), lambda s, i, o: (i, o[s])),
                  pl.BlockSpec((D_MODEL, tm), lambda s, i, o: (0, i)), any_spec, any_spec, vmem],
        out_specs=[vmem, vmem, vmem, vmem],
        scratch_shapes=[
            pltpu.VMEM((D_MODEL, SHARD_IN), F32), half(shapes[0], _N_CHIPS), pltpu.VMEM(shapes[0], BF16),
            half(shapes[0], 2), half(shapes[0], _N_CHIPS - 1),
            half(shapes[1], _N_CHIPS), half(shapes[1], _N_CHIPS), half(shapes[1], _N_CHIPS - 1),
            half(shapes[2], _N_CHIPS), half(shapes[2], _N_CHIPS), half(shapes[2], _N_CHIPS - 1),
            pltpu.VMEM((_N_CHIPS + 1,) + small_partial.shape, F32),
            pltpu.SemaphoreType.DMA((25,)), pltpu.SemaphoreType.DMA((25,)), pltpu.SemaphoreType.DMA((2 * _N_CHIPS,))],
    )
    return _call(
        body, name="grad_w_in_scatter", grid_spec=grid_spec,
        out_shape=[jax.ShapeDtypeStruct(shp, F32) for shp in shapes] + [jax.ShapeDtypeStruct(small_partial.shape, F32)],
        compiler_params=_params(60, 2),
    )(order, dproj, u_t, rs_a, rs_c, small_partial)


def _adam_update(g, w, m, v):
    m2 = ADAM_B1 * m + (1.0 - ADAM_B1) * g
    v2 = ADAM_B2 * v + (1.0 - ADAM_B2) * (g * g)
    m_hat = m2 / (1.0 - ADAM_B1 ** ADAM_STEP)
    v_hat = v2 / (1.0 - ADAM_B2 ** ADAM_STEP)
    delta = -ADAM_LR * (m_hat / (jnp.sqrt(v_hat) + ADAM_EPS) + ADAM_WD * w)
    return delta, m2, v2


def _adam_rows(g, w, m, v):
    rows, cols = w.shape
    tm = rows if rows % 256 else 256

    def fn(rv, cr, out):
        return list(_adam_update(*rv)), []

    outs, _ = _rowwise("adamw", fn, rows, tm, [(a, cols, 0) for a in (g, w, m, v)], [], [(cols, F32)] * 3, [], 32)
    return outs


_SMALL = ["norm_g", "hg_lb", "hg_norm_g", "s5_a_re", "s5_a_im", "s5_log_dt", "s5_b_re", "s5_b_im", "s5_c_re",
          "s5_c_im", "s5_d", "b_glu", "ple_norm_g", "final_norm_g"]
_BIG = ["w_in", "w_o_hg", "w_glu", "w_o_s5", "w_out", "w_ple", "w_ple_gate"]
_ORDER = ["norm_g", "w_in", "hg_lb", "hg_norm_g", "w_o_hg", "s5_a_re", "s5_a_im", "s5_log_dt", "s5_b_re", "s5_b_im",
          "s5_c_re", "s5_c_im", "s5_d", "w_glu", "b_glu", "w_o_s5", "w_out", "ple_norm_g", "w_ple", "w_ple_gate",
          "final_norm_g"]


def _pack_small(vals, tail=None):
    parts = []
    for name in _SMALL:
        flat = vals[name].reshape(-1).astype(F32)
        pad = (-flat.shape[0]) % 1024
        parts.append(jnp.pad(flat, (0, pad)))
    tail = jnp.zeros((0,), F32) if tail is None else tail.reshape(-1).astype(F32)
    parts.append(jnp.pad(tail, (0, 1024 - tail.shape[0])))
    return jnp.concatenate(parts).reshape(-1, 128)


def _unpack_small(packed, like):
    flat = packed.reshape(-1)
    out, off = {}, 0
    for name in _SMALL:
        size = like[name].size
        out[name] = flat[off:off + size].reshape(like[name].shape)
        off += size + (-size) % 1024
    return out


def _col_blocks(full):
    k = full.shape[0]
    return full.reshape(k, N_DEV, 128).transpose(1, 0, 2)


def _from_col_blocks(blocks):
    k = blocks.shape[1]
    return blocks.transpose(1, 0, 2).reshape(k, N_DEV * 128)


def kernel(x, p, norm_g, w_in, hg_lb, hg_norm_g, w_o_hg, s5_a_re, s5_a_im, s5_log_dt, s5_b_re, s5_b_im, s5_c_re, s5_c_im, s5_d, w_glu, b_glu, w_o_s5, w_out, ple_norm_g, w_ple, w_ple_gate, final_norm_g, loss_target, m_norm_g, m_w_in, m_hg_lb, m_hg_norm_g, m_w_o_hg, m_s5_a_re, m_s5_a_im, m_s5_log_dt, m_s5_b_re, m_s5_b_im, m_s5_c_re, m_s5_c_im, m_s5_d, m_w_glu, m_b_glu, m_w_o_s5, m_w_out, m_ple_norm_g, m_w_ple, m_w_ple_gate, m_final_norm_g, v_norm_g, v_w_in, v_hg_lb, v_hg_norm_g, v_w_o_hg, v_s5_a_re, v_s5_a_im, v_s5_log_dt, v_s5_b_re, v_s5_b_im, v_s5_c_re, v_s5_c_im, v_s5_d, v_w_glu, v_b_glu, v_w_o_s5, v_w_out, v_ple_norm_g, v_w_ple, v_w_ple_gate, v_final_norm_g):
    args = dict(locals())
    w = {n: args[n] for n in _ORDER}
    m = {n: args["m_" + n] for n in _ORDER}
    v = {n: args["v_" + n] for n in _ORDER}
    xs = x[0]
    ps = p[0, 0]
    tgt = loss_target[0]
    S = xs.shape[0]

    pack_a = jnp.concatenate([w_o_hg[0], w_out[0], w_ple_gate[0]], axis=0).astype(BF16)
    pack_b = w_in[0].astype(BF16)
    pack_c = jnp.concatenate([w_glu[0], w_o_s5[0], w_ple[0]], axis=0).astype(BF16)
    u_t, proj, proj16, all_a, all_b, all_c = _inproj_gather(xs, norm_g, pack_a, pack_b, pack_c)
    wf_glu = _from_col_blocks(all_c[:, 0:512])
    wf_o_s5 = _from_col_blocks(all_c[:, 512:1024])
    wf_ple = _from_col_blocks(all_c[:, 1024:1280])

    lb = jax.nn.sigmoid(hg_lb[0:1] - hg_lb[1:2])
    s5_names = ["s5_a_re", "s5_a_im", "s5_log_dt", "s5_b_re", "s5_b_im", "s5_c_re", "s5_c_im", "s5_d"]
    build = lambda *a: _s5_matrices(*a, seg_len=S // S5_SEG)
    mats_f32, mats_vjp = jax.vjp(build, *[w[n][0] for n in s5_names])
    mats = dict(mats_f32, b_q=mats_f32["b_q"].astype(BF16), c_q=mats_f32["c_q"].astype(BF16))
    bias_glu = b_glu

    o, states = _hgrn_fwd(proj, lb)
    u_perm = _to_segments(proj[:, COL_US:COL_US + S5_WIDTH])
    zeros_state = jnp.zeros((S5_SEG, S5_COLS), F32)
    (h_ends,) = _s5_fwd_pass(u_perm, mats, zeros_state, False)
    h0 = _segment_starts(h_ends, mats["pow_r"], mats["pow_i"], False)
    y_perm, h_init, _ = _s5_fwd_pass(u_perm, mats, h0, True)
    ys = _from_segments(y_perm)
    y_hg, y_s5, glu, h1 = _stage_branches(o, proj16, ys, xs, hg_norm_g, all_a, wf_glu, bias_glu, wf_o_s5)

    dh1, (loss_acc, d_final_g, d_ple_g, d_w_ple, d_w_pg) = _stage_ple_loss(
        h1, ps, tgt, ple_norm_g, wf_ple, all_a, final_norm_g.reshape(1, D_MODEL))
    (d_gate_hg, d_gate_s5, d_yhg, d_ys5), (d_w_out,) = _stage_bwd_merge(dh1, y_hg, y_s5, proj16, all_a)
    (d_o, d_g_hg), (d_w_o_hg, d_hg_norm) = _stage_bwd_hg_path(d_yhg, o, proj16, hg_norm_g, all_a)
    (d_ys, d_z), (d_w_o_s5, d_w_glu, d_b_glu) = _stage_bwd_s5_path(d_ys5, ys, glu, proj16, wf_o_s5, wf_glu)
    dq, df, div, d_lb = _hgrn_bwd(proj, lb, d_o, states)
    dy_perm = _to_segments(d_ys)
    g_ends = _s5_bwd_ends(dy_perm, mats)
    g0 = _segment_starts(g_ends, mats["pow_r"], mats["pow_i"], True)
    du_perm, acc_p, acc_q, d_bq, d_cq_t, d_d = _s5_bwd_full(u_perm, dy_perm, h_init, g0, mats)
    d_us = _from_segments(du_perm)
    grad_x, dproj, d_norm_g = _stage_inproj_bwd([dq, df, div, d_g_hg, d_us, d_z, d_gate_hg, d_gate_s5], xs, dh1,
                                                norm_g, all_b)

    p_re, p_im = _split_cols(acc_p)
    q_re, q_im = _split_cols(acc_q)
    d_lam_r = (p_re + p_im)[0]
    d_lam_i = (q_im - q_re)[0]
    zero_row = jnp.zeros((S5_SEG, S5_COLS), F32)
    row_of = lambda re_part, im_part: zero_row.at[0].set(_join_cols(re_part[None], im_part[None])[0])
    zeros_q = jnp.zeros_like(d_lam_r)
    cot = dict(
        lam_r=row_of(d_lam_r, zeros_q), lam_i=row_of(zeros_q, d_lam_i),
        b_q=d_bq, c_q=d_cq_t.transpose(0, 2, 1), d_row=d_d,
        pow_r=jnp.zeros_like(mats["pow_r"]), pow_i=jnp.zeros_like(mats["pow_i"]),
    )
    d_s5 = mats_vjp(cot)

    s_lb = lb * (1.0 - lb)
    d_hg_lb = jnp.concatenate([d_lb * s_lb, -d_lb * s_lb], axis=0)
    small_g = dict(norm_g=d_norm_g, hg_lb=d_hg_lb, hg_norm_g=d_hg_norm, b_glu=d_b_glu, ple_norm_g=d_ple_g,
                   final_norm_g=d_final_g)
    for name, g in zip(s5_names, d_s5):
        small_g[name] = g
    pk = lambda d: _pack_small({n: d[n] for n in _SMALL})
    rs_a = jnp.concatenate([d_w_o_hg.reshape(N_DEV, 128, D_MODEL), d_w_out.reshape(N_DEV, 128, D_MODEL),
                            d_w_pg.reshape(N_DEV, 128, D_MODEL)], axis=1).astype(BF16)
    rs_c = jnp.concatenate([_col_blocks(d_w_glu), _col_blocks(d_w_o_s5), _col_blocks(d_w_ple)], axis=1).astype(BF16)
    partial = _pack_small({n: small_g[n] for n in _SMALL}, tail=loss_acc[0, 0:1])
    g_b, g_a, g_c, sg = _grad_w_in_scatter(dproj, u_t, rs_a, rs_c, partial)
    sd, sm, sv = _adam_rows(sg, pk(w), pk(m), pk(v))
    like = {n: w[n] for n in _SMALL}
    out_g, out_d, out_m, out_v = (_unpack_small(t, like) for t in (sg, sd, sm, sv))
    big_g = dict(w_o_hg=g_a[0:128], w_out=g_a[128:256], w_ple_gate=g_a[256:384], w_in=g_b,
                 w_glu=g_c[0:512], w_o_s5=g_c[512:1024], w_ple=g_c[1024:1280])
    for name in _BIG:
        shape = w[name].shape
        g2 = big_g[name]
        d2, m2, v2 = _adam_rows(g2, w[name][0], m[name][0], v[name][0])
        out_g[name], out_d[name], out_m[name], out_v[name] = (t.reshape(shape) for t in (g2, d2, m2, v2))

    loss = sg[sg.shape[0] - 8, 0]
    return (loss, grad_x[None], *[out_g[n] for n in _ORDER], *[out_d[n] for n in _ORDER],
            *[out_m[n] for n in _ORDER], *[out_v[n] for n in _ORDER])
```

```python
import functools
import math

import jax
import jax.numpy as jnp
from jax import lax
from jax.experimental import pallas as pl
from jax.experimental.pallas import tpu as pltpu

F32 = jnp.float32
BF16 = jnp.bfloat16

D_MODEL = 1024
N_DEV = 8
IN_COLS = 7168
SHARD_IN = IN_COLS // N_DEV
HG_HEADS = 8
HG_DIM = 128
HG_CHUNK = 64
HG_SUPER_FWD = 256
HG_SUPER_BWD = 128
HG_HEADS_PER_STEP = 8
S5_WIDTH = 512
S5_GROUPS = 32
S5_STATE = 64
S5_CH = 16
S5_SEG = 8
S5_QUADS = 4
S5_QCOLS = 1024
S5_COLS = S5_QUADS * S5_QCOLS
S5_TILE_STEPS = 64
S5_UNROLL = 8
NORM_EPS = 1e-6
ADAM_LR = 0.001
ADAM_B1 = 0.9
ADAM_B2 = 0.999
ADAM_EPS = 1e-08
ADAM_WD = 0.01
ADAM_STEP = 10
MIB = 1024 * 1024
MESH = pl.DeviceIdType.MESH

COL_Q, COL_F, COL_I, COL_G, COL_US, COL_ZS, COL_GH, COL_GS = 0, 1024, 2048, 3072, 4096, 4608, 5120, 6144


def _call(body, **kw):
    return pl.pallas_call(body, **kw)


def _params(vmem_mb, n_grid=1):
    return pltpu.CompilerParams(
        dimension_semantics=("arbitrary",) * n_grid, vmem_limit_bytes=vmem_mb * MIB
    )


def _bdot(a, b):
    return jnp.dot(a.astype(BF16), b.astype(BF16), preferred_element_type=F32)


def _bdot_nt(a, b):
    return lax.dot_general(a.astype(BF16), b.astype(BF16), (((1,), (1,)), ((), ())), preferred_element_type=F32)


def _bdot_tn(a, b):
    return lax.dot_general(a.astype(BF16), b.astype(BF16), (((0,), (0,)), ((), ())), preferred_element_type=F32)


def _sigmoid(x):
    return jax.nn.sigmoid(x)


def _silu(x):
    return x * _sigmoid(x)


def _dsilu(x):
    s = _sigmoid(x)
    return s * (1.0 + x * (1.0 - s))


_GELU_C = math.sqrt(2.0 / math.pi)


def _gelu(x):
    return 0.5 * x * (1.0 + jnp.tanh(_GELU_C * (x + 0.044715 * x * x * x)))


def _dgelu(x):
    t = jnp.tanh(_GELU_C * (x + 0.044715 * x * x * x))
    return 0.5 * (1.0 + t) + 0.5 * x * (1.0 - t * t) * _GELU_C * (1.0 + 3.0 * 0.044715 * x * x)


def _rms_fwd(x, g):
    r = lax.rsqrt(jnp.mean(x * x, axis=-1, keepdims=True) + NORM_EPS)
    n = x * r
    return n * g, n, r


def _rms_bwd(dy, n, r, g):
    dn = dy * g
    dx = r * (dn - n * jnp.mean(dn * n, axis=-1, keepdims=True))
    return dx, jnp.sum(dy * n, axis=0, keepdims=True)


def _head_rms_fwd(o, g):
    ns, rs = [], []
    for h in range(HG_HEADS):
        oh = o[:, h * HG_DIM:(h + 1) * HG_DIM]
        r = lax.rsqrt(jnp.mean(oh * oh, axis=-1, keepdims=True) + NORM_EPS)
        ns.append(oh * r)
        rs.append(r)
    n = jnp.concatenate(ns, axis=1)
    return n * g, n, rs


def _head_rms_bwd(dy, n, rs, g):
    dn = dy * g
    dxs = []
    for h in range(HG_HEADS):
        sl = slice(h * HG_DIM, (h + 1) * HG_DIM)
        dxs.append(rs[h] * (dn[:, sl] - n[:, sl] * jnp.mean(dn[:, sl] * n[:, sl], axis=-1, keepdims=True)))
    return jnp.concatenate(dxs, axis=1), jnp.sum(dy * n, axis=0, keepdims=True)


def _rowwise(name, fn, n_rows, tm, rows, consts, out_rows, out_accs, vmem_mb, parts=1):
    n_r, n_c, n_or, n_oa = len(rows), len(consts), len(out_rows), len(out_accs)
    tp = tm // parts

    def body(*refs):
        r_refs = refs[:n_r]
        c_refs = refs[n_r:n_r + n_c]
        or_refs = refs[n_r + n_c:n_r + n_c + n_or]
        oa_refs = refs[n_r + n_c + n_or:]

        if n_oa:
            @pl.when(pl.program_id(0) == 0)
            def _():
                for ref in oa_refs:
                    ref[...] = jnp.zeros(ref.shape, ref.dtype)

        for part in range(parts):
            sl = slice(part * tp, (part + 1) * tp)
            outs, accs = fn([r[sl, :] for r in r_refs], c_refs, [o.at[sl, :] for o in or_refs])
            for ref, v in zip(or_refs, outs):
                if v is not None:
                    ref[sl, :] = v.astype(ref.dtype)
            for ref, v in zip(oa_refs, accs):
                ref[...] += v.astype(ref.dtype)

    in_specs = [pl.BlockSpec((tm, w), functools.partial(lambda i, c: (i, c), c=cb)) for (_, w, cb) in rows]
    in_specs += [pl.BlockSpec(c.shape, functools.partial(lambda i, nd: (0,) * nd, nd=c.ndim),
                              pipeline_mode=pl.Buffered(1)) for c in consts]
    out_specs = [pl.BlockSpec((tm, w), lambda i: (i, 0)) for (w, _) in out_rows]
    out_specs += [pl.BlockSpec(s, functools.partial(lambda i, nd: (0,) * nd, nd=len(s))) for (s, _) in out_accs]
    out_shape = [jax.ShapeDtypeStruct((n_rows, w), dt) for (w, dt) in out_rows]
    out_shape += [jax.ShapeDtypeStruct(s, dt) for (s, dt) in out_accs]
    res = _call(
        body, name=name, grid=(n_rows // tm,), in_specs=in_specs, out_specs=out_specs, out_shape=out_shape,
        compiler_params=_params(vmem_mb),
    )(*[a for (a, _, _) in rows], *consts)
    return res[:n_or], res[n_or:]


def _stage_branches(o, proj, ys, x, hg_norm_g, w_o_hg, w_glu, b_glu, w_o_s5, w_out):
    S = x.shape[0]

    def fn(rv, cr, out):
        o_b, ys_b, x_b = rv[0], rv[5], rv[6]
        g_hg, z_s, gate_hg, gate_s5 = (v.astype(F32) for v in rv[1:5])
        gn_ref, wohg_ref, wglu_ref, bglu_ref, wos5_ref, wout_ref = cr
        on, _, _ = _head_rms_fwd(o_b, gn_ref[...])
        a = on * _silu(g_hg)
        y_hg = jnp.dot(a.astype(BF16), wohg_ref[...], preferred_element_type=F32)
        gl = _gelu(ys_b)
        glu = jnp.dot(gl.astype(BF16), wglu_ref[...], preferred_element_type=F32) + bglu_ref[...]
        ys2 = glu[:, :S5_WIDTH] * _sigmoid(glu[:, S5_WIDTH:]) * _silu(z_s)
        y_s5 = jnp.dot(ys2.astype(BF16), wos5_ref[...], preferred_element_type=F32)
        merged = _sigmoid(gate_hg) * y_hg + _sigmoid(gate_s5) * y_s5
        h1 = x_b + jnp.dot(merged.astype(BF16), wout_ref[...], preferred_element_type=F32)
        return [y_hg, y_s5, glu, h1], []

    rows = [(o, D_MODEL, 0), (proj, D_MODEL, COL_G // D_MODEL), (proj, S5_WIDTH, COL_ZS // S5_WIDTH),
            (proj, D_MODEL, COL_GH // D_MODEL), (proj, D_MODEL, COL_GS // D_MODEL), (ys, S5_WIDTH, 0), (x, D_MODEL, 0)]
    (y_hg, y_s5, glu, h1), _ = _rowwise(
        "branches", fn, S, 256, rows, [hg_norm_g, w_o_hg, w_glu, b_glu, w_o_s5, w_out],
        [(D_MODEL, BF16), (D_MODEL, BF16), (D_MODEL, BF16), (D_MODEL, F32)], [], 56)
    return y_hg, y_s5, glu, h1


def _stage_ple_loss(h1, p, target, ple_norm_g, w_ple, w_ple_gate, final_norm_g):
    S = h1.shape[0]

    def fn(rv, cr, out):
        h1_b, p_b, t_b = rv
        gp_ref, wple_ref, wpg_ref, gf_ref = cr
        n2g, n2, r2 = _rms_fwd(h1_b, gp_ref[...])
        z = jnp.dot(n2g.astype(BF16), wpg_ref[...], preferred_element_type=F32)
        gate = _sigmoid(z)
        pe = jnp.dot(p_b.astype(BF16), wple_ref[...], preferred_element_type=F32)
        h2 = h1_b + pe * gate
        y, nf, rf = _rms_fwd(h2, gf_ref[...])
        err = y - t_b
        loss_rows = 0.5 * jnp.mean(err * err, axis=-1, keepdims=True)
        loss_inc = jnp.broadcast_to(jnp.sum(loss_rows, axis=0, keepdims=True), (1, 128))
        dy = err * (1.0 / D_MODEL)
        dh2, d_gf = _rms_bwd(dy, nf, rf, gf_ref[...])
        d_pe = dh2 * gate
        dz = dh2 * pe * gate * (1.0 - gate)
        d_wple = _bdot_tn(p_b, d_pe)
        d_wpg = _bdot_tn(n2g, dz)
        dn2g = _bdot_nt(dz, wpg_ref[...])
        dh1n, d_gp = _rms_bwd(dn2g, n2, r2, gp_ref[...])
        return [dh2 + dh1n], [loss_inc, d_gf, d_gp, d_wple, d_wpg]

    (dh1,), accs = _rowwise(
        "ple_loss", fn, S, 1024, [(h1, D_MODEL, 0), (p, 256, 0), (target, D_MODEL, 0)],
        [ple_norm_g, w_ple, w_ple_gate, final_norm_g], [(D_MODEL, F32)],
        [((1, 128), F32), ((1, D_MODEL), F32), ((1, D_MODEL), F32), ((256, D_MODEL), F32), ((D_MODEL, D_MODEL), F32)], 60,
        parts=4)
    return dh1, accs


def _stage_bwd_merge(dh1, y_hg, y_s5, proj, w_out):
    S = dh1.shape[0]

    def fn(rv, cr, out):
        dh1_b, yhg, ys5 = rv[:3]
        gate_hg, gate_s5 = rv[3].astype(F32), rv[4].astype(F32)
        (wout_ref,) = cr
        sg_h, sg_s = _sigmoid(gate_hg), _sigmoid(gate_s5)
        merged = sg_h * yhg + sg_s * ys5
        d_wout = _bdot_tn(merged, dh1_b)
        d_merged = _bdot_nt(dh1_b, wout_ref[...])
        d_gate_hg = d_merged * yhg * sg_h * (1.0 - sg_h)
        d_gate_s5 = d_merged * ys5 * sg_s * (1.0 - sg_s)
        return [d_gate_hg, d_gate_s5, d_merged * sg_h, d_merged * sg_s], [d_wout]

    rows = [(dh1, D_MODEL, 0), (y_hg, D_MODEL, 0), (y_s5, D_MODEL, 0), (proj, D_MODEL, COL_GH // D_MODEL),
            (proj, D_MODEL, COL_GS // D_MODEL)]
    outs, accs = _rowwise("bwd_merge", fn, S, 512, rows, [w_out], [(D_MODEL, BF16)] * 4,
                          [((D_MODEL, D_MODEL), F32)], 56, parts=2)
    return outs, accs


def _stage_bwd_hg_path(d_yhg, o, proj, hg_norm_g, w_o_hg):
    S = o.shape[0]

    def fn(rv, cr, out):
        d_yhg_b, o_b, g_hg = rv[0], rv[1], rv[2].astype(F32)
        gn_ref, wohg_ref = cr
        ong, on, rs = _head_rms_fwd(o_b, gn_ref[...])
        sil = _silu(g_hg)
        d_wohg = _bdot_tn(ong * sil, d_yhg_b)
        d_a = _bdot_nt(d_yhg_b, wohg_ref[...])
        d_g_hg = d_a * ong * _dsilu(g_hg)
        d_o, d_gn = _head_rms_bwd(d_a * sil, on, rs, gn_ref[...])
        return [d_o, d_g_hg], [d_wohg, d_gn]

    rows = [(d_yhg, D_MODEL, 0), (o, D_MODEL, 0), (proj, D_MODEL, COL_G // D_MODEL)]
    outs, accs = _rowwise("bwd_hg_path", fn, S, 512, rows, [hg_norm_g, w_o_hg], [(D_MODEL, BF16)] * 2,
                          [((D_MODEL, D_MODEL), F32), ((1, D_MODEL), F32)], 56, parts=2)
    return outs, accs


def _stage_bwd_s5_path(d_ys5, ys, glu, proj, w_o_s5, w_glu):
    S = ys.shape[0]

    def fn(rv, cr, out):
        d_ys5_b, ys_b, glu_b, z_s = rv[0], rv[1], rv[2].astype(F32), rv[3].astype(F32)
        wos5_ref, wglu_ref = cr
        ga, gb = glu_b[:, :S5_WIDTH], glu_b[:, S5_WIDTH:]
        sgb, silz = _sigmoid(gb), _silu(z_s)
        ys2 = ga * sgb * silz
        d_wos5 = _bdot_tn(ys2, d_ys5_b)
        d_ys2 = _bdot_nt(d_ys5_b, wos5_ref[...])
        d_ga = d_ys2 * sgb * silz
        d_gb = d_ys2 * ga * sgb * (1.0 - sgb) * silz
        d_z = d_ys2 * ga * sgb * _dsilu(z_s)
        d_glu = jnp.concatenate([d_ga, d_gb], axis=1)
        gl = _gelu(ys_b)
        d_wglu = _bdot_tn(gl, d_glu)
        d_bglu = jnp.sum(d_glu, axis=0, keepdims=True)
        d_gl = _bdot_nt(d_glu, wglu_ref[...])
        return [d_gl * _dgelu(ys_b), d_z], [d_wos5, d_wglu, d_bglu]

    rows = [(d_ys5, D_MODEL, 0), (ys, S5_WIDTH, 0), (glu, D_MODEL, 0), (proj, S5_WIDTH, COL_ZS // S5_WIDTH)]
    outs, accs = _rowwise(
        "bwd_s5_path", fn, S, 512, rows, [w_o_s5, w_glu], [(S5_WIDTH, F32), (S5_WIDTH, BF16)],
        [((S5_WIDTH, D_MODEL), F32), ((S5_WIDTH, D_MODEL), F32), ((1, D_MODEL), F32)], 48, parts=2)
    return outs, accs


def _stage_inproj_bwd(pieces, x, dh1, norm_g, w_in_all):
    S = x.shape[0]

    def fn(rv, cr, out):
        g_ref, w_ref = cr
        x_b, dh1_b = rv[8], rv[9]
        dproj_ref = out[1]
        col = 0
        for v in rv[:8]:
            dproj_ref[:, col:col + v.shape[1]] = v.astype(BF16)
            col += v.shape[1]
        d_u = jnp.zeros((x_b.shape[0], D_MODEL), F32)
        for j in range(N_DEV):
            d_u = d_u + lax.dot_general(dproj_ref[:, j * SHARD_IN:(j + 1) * SHARD_IN], w_ref[j],
                                        (((1,), (1,)), ((), ())), preferred_element_type=F32)
        _, n, r = _rms_fwd(x_b, g_ref[...])
        dx, d_g = _rms_bwd(d_u, n, r, g_ref[...])
        return [dh1_b + dx, None], [d_g]

    rows = [(a, a.shape[1], 0) for a in pieces] + [(x, D_MODEL, 0), (dh1, D_MODEL, 0)]
    (grad_x, dproj), (d_g,) = _rowwise(
        "inproj_bwd", fn, S, 256, rows, [norm_g, w_in_all], [(D_MODEL, F32), (IN_COLS, BF16)],
        [((1, D_MODEL), F32)], 56)
    return grad_x, dproj, d_g


def _chunk_row(shape):
    return lax.broadcasted_iota(jnp.int32, shape, 0) & (HG_CHUNK - 1)


def _chunk_cumsum(x):
    r_in = _chunk_row(x.shape)
    s = 1
    while s < HG_CHUNK:
        x = x + jnp.where(r_in >= s, pltpu.roll(x, s, 0), 0.0)
        s *= 2
    return x


def _chunk_suffix_sum(x):
    n = x.shape[0]
    r_in = _chunk_row(x.shape)
    s = 1
    while s < HG_CHUNK:
        x = x + jnp.where(r_in < HG_CHUNK - s, pltpu.roll(x, n - s, 0), 0.0)
        s *= 2
    return x


def _hgrn_prep(q, fl, lb):
    sup = q.shape[0]
    nc = sup // HG_CHUNK
    sig = _sigmoid(fl)
    f = lb + (1.0 - lb) * sig
    k = (1.0 - lb) * (1.0 - sig)
    b = _chunk_cumsum(jnp.log(f))
    b3 = b.reshape(nc, HG_CHUNK, HG_DIM)
    row3 = lax.broadcasted_iota(jnp.int32, b3.shape, 1)
    pick = lambda r: jnp.sum(jnp.where(row3 == r, b3, 0.0), axis=1, keepdims=True)
    b_mid = pick(HG_CHUNK // 2 - 1)
    b_last = pick(HG_CHUNK - 1)
    flat = lambda t: t.reshape(sup, HG_DIM)
    e_qa = flat(jnp.exp(b3 - b_mid))
    e_ka = flat(jnp.exp(b_mid - b3))
    e_qd = jnp.exp(b)
    e_kd = flat(jnp.exp(b_last - b3))
    dc = jnp.exp(b_last)
    return sig, f, k, e_qa, e_ka, e_qd, e_kd, dc


def _hgrn_mask(sup):
    r = lax.broadcasted_iota(jnp.int32, (sup, sup), 0)
    c = lax.broadcasted_iota(jnp.int32, (sup, sup), 1)
    shift = HG_CHUNK.bit_length() - 1
    return (jnp.right_shift(r, shift) == jnp.right_shift(c, shift)) & (r >= c)


def _hgrn_fwd(proj, lb):
    S = proj.shape[0]
    sup = HG_SUPER_FWD
    nb = S // sup
    nc = sup // HG_CHUNK
    hp = HG_HEADS_PER_STEP
    wide = hp * HG_DIM

    def body(q_ref, f_ref, iv_ref, lb_ref, o_ref, st_ref, state):
        @pl.when(pl.program_id(1) == 0)
        def _():
            state[...] = jnp.zeros(state.shape, F32)

        mask = _hgrn_mask(sup)
        for hh in range(hp):
            lanes = slice(hh * HG_DIM, (hh + 1) * HG_DIM)
            q, iv = q_ref[:, lanes], iv_ref[:, lanes]
            _, _, k, e_qa, e_ka, e_qd, e_kd, dc = _hgrn_prep(q, f_ref[:, lanes], lb_ref[:, lanes])
            scores = jnp.where(mask, _bdot_nt(q * e_qa, k * e_ka), 0.0)
            o_intra = _bdot(scores, iv)
            qd, kd = q * e_qd, k * e_kd
            for c in range(nc):
                sl = slice(c * HG_CHUNK, (c + 1) * HG_CHUNK)
                st = state[hh]
                st_ref[hh, c] = st
                o_ref[sl, lanes] = o_intra[sl] + _bdot_nt(qd[sl], st)
                state[hh] = dc[c] * st + _bdot_tn(iv[sl], kd[sl])

    blk = lambda base: pl.BlockSpec((sup, wide), functools.partial(lambda h, i, b: (i, b + h), b=base // wide))
    return _call(
        body, name="hgrn_fwd", grid=(HG_HEADS // hp, nb),
        in_specs=[blk(COL_Q), blk(COL_F), blk(COL_I), pl.BlockSpec((1, wide), lambda h, i: (0, h))],
        out_specs=[pl.BlockSpec((sup, wide), lambda h, i: (i, h)),
                   pl.BlockSpec((hp, nc, HG_DIM, HG_DIM), lambda h, i: (h, i, 0, 0))],
        out_shape=[jax.ShapeDtypeStruct((S, D_MODEL), F32),
                   jax.ShapeDtypeStruct((HG_HEADS, S // HG_CHUNK, HG_DIM, HG_DIM), F32)],
        scratch_shapes=[pltpu.VMEM((hp, HG_DIM, HG_DIM), F32)],
        compiler_params=_params(40, 2),
    )(proj, proj, proj, lb)


def _hgrn_bwd(proj, lb, d_o, states):
    S = proj.shape[0]
    sup = HG_SUPER_BWD
    nb = S // sup
    nc = sup // HG_CHUNK
    hp = HG_HEADS_PER_STEP
    wide = hp * HG_DIM

    def body(q_ref, f_ref, iv_ref, lb_ref, do_ref, st_ref, dq_ref, df_ref, div_ref, dlb_ref, dstate):
        @pl.when(pl.program_id(1) == 0)
        def _():
            dstate[...] = jnp.zeros(dstate.shape, F32)
            dlb_ref[...] = jnp.zeros(dlb_ref.shape, F32)

        mask = _hgrn_mask(sup)
        for hh in range(hp):
            lanes = slice(hh * HG_DIM, (hh + 1) * HG_DIM)
            q, iv, do, lb_v = q_ref[:, lanes], iv_ref[:, lanes], do_ref[:, lanes], lb_ref[:, lanes]
            sig, f, k, e_qa, e_ka, e_qd, e_kd, dc = _hgrn_prep(q, f_ref[:, lanes], lb_v)
            qa, ka, qd, kd = q * e_qa, k * e_ka, q * e_qd, k * e_kd
            scores = jnp.where(mask, _bdot_nt(qa, ka), 0.0)
            d_scores = jnp.where(mask, _bdot_nt(do, iv), 0.0)
            d_iv_intra = _bdot_tn(scores, do)
            d_qa = _bdot(d_scores, ka)
            d_ka = _bdot_tn(d_scores, qa)
            d_qd, d_kd, d_last = [None] * nc, [None] * nc, [None] * nc
            for c in reversed(range(nc)):
                sl = slice(c * HG_CHUNK, (c + 1) * HG_CHUNK)
                st = st_ref[hh, c]
                ds = dstate[hh]
                d_qd[c] = _bdot(do[sl], st)
                d_kd[c] = _bdot(iv[sl], ds)
                div_ref[sl, lanes] = (d_iv_intra[sl] + _bdot_nt(kd[sl], ds)).astype(div_ref.dtype)
                d_last[c] = (jnp.sum(ds * st, axis=0, keepdims=True) * dc[c]
                             + jnp.sum(d_kd[c] * kd[sl], axis=0, keepdims=True))
                dstate[hh] = dc[c] * ds + _bdot_tn(do[sl], qd[sl])
            d_qd = jnp.concatenate(d_qd, axis=0)
            d_kd = jnp.concatenate(d_kd, axis=0)
            d_b = d_qa * qa - d_ka * ka + d_qd * qd - d_kd * kd
            last_rows = jnp.concatenate([jnp.broadcast_to(t, (HG_CHUNK, HG_DIM)) for t in d_last], axis=0)
            d_b = d_b + jnp.where(_chunk_row(d_b.shape) == HG_CHUNK - 1, last_rows, 0.0)
            d_logf = _chunk_suffix_sum(d_b)
            d_k = d_ka * e_ka + d_kd * e_kd
            g_f = d_logf / f
            d_sig = (g_f - d_k) * (1.0 - lb_v)
            dq_ref[:, lanes] = (d_qa * e_qa + d_qd * e_qd).astype(dq_ref.dtype)
            df_ref[:, lanes] = (d_sig * sig * (1.0 - sig)).astype(df_ref.dtype)
            d_lb = jnp.sum((g_f - d_k) * (1.0 - sig), axis=0, keepdims=True)
            dlb_ref[:, lanes] += jnp.broadcast_to(d_lb, (8, HG_DIM))

    rev = lambda i: nb - 1 - i
    blk = lambda base: pl.BlockSpec((sup, wide), functools.partial(lambda h, i, b: (rev(i), b + h), b=base // wide))
    row_out = pl.BlockSpec((sup, wide), lambda h, i: (rev(i), h))
    dq, df, div, dlb = _call(
        body, name="hgrn_bwd", grid=(HG_HEADS // hp, nb),
        in_specs=[blk(COL_Q), blk(COL_F), blk(COL_I), pl.BlockSpec((1, wide), lambda h, i: (0, h)),
                  pl.BlockSpec((sup, wide), lambda h, i: (rev(i), h)),
                  pl.BlockSpec((hp, nc, HG_DIM, HG_DIM), lambda h, i: (h, rev(i), 0, 0))],
        out_specs=[row_out, row_out, row_out, pl.BlockSpec((8, wide), lambda h, i: (0, h))],
        out_shape=[jax.ShapeDtypeStruct((S, D_MODEL), BF16)] * 3 + [jax.ShapeDtypeStruct((8, D_MODEL), F32)],
        scratch_shapes=[pltpu.VMEM((hp, HG_DIM, HG_DIM), F32)],
        compiler_params=_params(40, 2),
    )(proj, proj, proj, lb, d_o, states)
    return dq, df, div, dlb[0:1]


def _s5_matrices(a_re, a_im, log_dt, b_re, b_im, c_re, c_im, d, seg_len):
    dt = jnp.exp(log_dt)[:, None]
    mag = jnp.exp(a_re * dt)
    lr, li = mag * jnp.cos(a_im * dt), mag * jnp.sin(a_im * dt)
    den = a_re * a_re + a_im * a_im
    nr = lr - 1.0
    sr = (nr * a_re + li * a_im) / den
    si = (li * a_re - nr * a_im) / den
    bbr = sr[..., None] * b_re - si[..., None] * b_im
    bbi = sr[..., None] * b_im + si[..., None] * b_re
    eye = jnp.eye(8, dtype=F32)

    def quad_cols(v):
        return v.reshape(S5_QUADS, 8 * S5_STATE)

    def lam_row(re_part, im_part):
        row = jnp.concatenate([quad_cols(re_part), quad_cols(im_part)], axis=1).reshape(1, S5_COLS)
        return jnp.broadcast_to(row, (S5_SEG, S5_COLS))

    def b_mat(bb):
        t = bb.reshape(S5_QUADS, 8, S5_STATE, S5_CH)
        return jnp.einsum("qgnc,gh->qgchn", t, eye).reshape(S5_QUADS, 8 * S5_CH, 8 * S5_STATE)

    def c_mat(cc):
        t = cc.reshape(S5_QUADS, 8, S5_CH, S5_STATE)
        return jnp.einsum("qgcn,gh->qgnhc", t, eye).reshape(S5_QUADS, 8 * S5_STATE, 8 * S5_CH)

    ang = a_im * dt * seg_len
    magp = jnp.exp(a_re * dt * seg_len)
    lpr, lpi = magp * jnp.cos(ang), magp * jnp.sin(ang)
    return dict(
        lam_r=lam_row(lr, lr), lam_i=lam_row(-li, li),
        b_q=jnp.concatenate([b_mat(bbr), b_mat(bbi)], axis=2),
        c_q=jnp.concatenate([c_mat(c_re), -c_mat(c_im)], axis=1),
        d_row=d.reshape(1, S5_WIDTH), pow_r=quad_cols(lpr), pow_i=quad_cols(lpi),
    )


def _s5_parts(v):
    half = S5_QCOLS // 2
    return tuple(v[:, k * half:(k + 1) * half] for k in range(2 * S5_QUADS))


def _s5_advance(parts, lr_ref, li_ref, x_ref, sl, conj):
    half = S5_QCOLS // 2
    out = []
    for q in range(S5_QUADS):
        re_c = slice(q * S5_QCOLS, q * S5_QCOLS + half)
        im_c = slice(q * S5_QCOLS + half, (q + 1) * S5_QCOLS)
        lr, li = lr_ref[:, re_c], li_ref[:, im_c]
        hr, hi = parts[2 * q], parts[2 * q + 1]
        if conj:
            out += [lr * hr + li * hi + x_ref[sl, re_c], lr * hi - li * hr + x_ref[sl, im_c]]
        else:
            out += [lr * hr - li * hi + x_ref[sl, re_c], lr * hi + li * hr + x_ref[sl, im_c]]
    return tuple(out)


def _scan_loop(step, init):
    def trip(o, carry):
        for j in range(S5_UNROLL):
            carry = step(o * S5_UNROLL + j, carry)
        return carry

    return lax.fori_loop(0, S5_TILE_STEPS // S5_UNROLL, trip, init)


def _s5_store(ref, sl, parts):
    half = S5_QCOLS // 2
    for k, v in enumerate(parts):
        ref[sl, k * half:(k + 1) * half] = v


def _s5_fwd_pass(u_perm, mats, h0, with_output):
    S = u_perm.shape[0]
    rows = S5_TILE_STEPS * S5_SEG
    nt = S // rows

    def body(*refs):
        if with_output:
            u_ref, b_ref, lr_ref, li_ref, h0_ref, c_ref, d_ref, y_ref, hinit_ref, hend_ref, xs, hcar = refs
        else:
            u_ref, b_ref, lr_ref, li_ref, h0_ref, hend_ref, xs, hcar = refs

        @pl.when(pl.program_id(0) == 0)
        def _():
            hcar[...] = h0_ref[...]

        if with_output:
            hinit_ref[...] = hcar[...]
        u = u_ref[...]
        ub = u.astype(BF16)
        for q in range(S5_QUADS):
            xs[:, q * S5_QCOLS:(q + 1) * S5_QCOLS] = jnp.dot(ub[:, q * 128:(q + 1) * 128], b_ref[q], preferred_element_type=F32)

        def step(t, h):
            sl = pl.ds(pl.multiple_of(t * S5_SEG, S5_SEG), S5_SEG)
            hn = _s5_advance(h, lr_ref, li_ref, xs, sl, False)
            _s5_store(xs, sl, hn)
            return hn

        h = _scan_loop(step, _s5_parts(hcar[...]))
        _s5_store(hcar, slice(None), h)
        _s5_store(hend_ref, slice(None), h)
        if with_output:
            ys = [jnp.dot(xs[:, q * S5_QCOLS:(q + 1) * S5_QCOLS].astype(BF16), c_ref[q], preferred_element_type=F32)
                  for q in range(S5_QUADS)]
            y_ref[...] = jnp.concatenate(ys, axis=1) + d_ref[...] * u

    full = lambda a: pl.BlockSpec(a.shape, functools.partial(lambda i, nd: (0,) * nd, nd=a.ndim))
    ins = [u_perm, mats["b_q"], mats["lam_r"], mats["lam_i"], h0]
    in_specs = [pl.BlockSpec((rows, S5_WIDTH), lambda i: (i, 0))] + [full(a) for a in ins[1:]]
    out_specs = [pl.BlockSpec((S5_SEG, S5_COLS), lambda i: (0, 0))]
    out_shape = [jax.ShapeDtypeStruct((S5_SEG, S5_COLS), F32)]
    if with_output:
        ins += [mats["c_q"], mats["d_row"]]
        in_specs += [full(mats["c_q"]), full(mats["d_row"])]
        out_specs = [pl.BlockSpec((rows, S5_WIDTH), lambda i: (i, 0)),
                     pl.BlockSpec((None, S5_SEG, S5_COLS), lambda i: (i, 0, 0))] + out_specs
        out_shape = [jax.ShapeDtypeStruct((S, S5_WIDTH), F32), jax.ShapeDtypeStruct((nt, S5_SEG, S5_COLS), F32)] + out_shape
    return _call(
        body, name="s5_fwd_y" if with_output else "s5_fwd_ends", grid=(nt,), in_specs=in_specs, out_specs=out_specs,
        out_shape=out_shape,
        scratch_shapes=[pltpu.VMEM((rows, S5_COLS), F32), pltpu.VMEM((S5_SEG, S5_COLS), F32)],
        compiler_params=_params(40),
    )(*ins)


def _s5_bwd_ends(dy_perm, mats):
    S = dy_perm.shape[0]
    rows = S5_TILE_STEPS * S5_SEG
    nt = S // rows

    def body(dy_ref, c_ref, lr_ref, li_ref, gend_ref, gs, gcar):
        @pl.when(pl.program_id(0) == 0)
        def _():
            gcar[...] = jnp.zeros(gcar.shape, F32)

        dyb = dy_ref[...].astype(BF16)
        for q in range(S5_QUADS):
            gs[:, q * S5_QCOLS:(q + 1) * S5_QCOLS] = lax.dot_general(
                dyb[:, q * 128:(q + 1) * 128], c_ref[q], (((1,), (1,)), ((), ())), preferred_element_type=F32)

        def step(k, g):
            t = S5_TILE_STEPS - 1 - k
            sl = pl.ds(pl.multiple_of(t * S5_SEG, S5_SEG), S5_SEG)
            return _s5_advance(g, lr_ref, li_ref, gs, sl, True)

        g = _scan_loop(step, _s5_parts(gcar[...]))
        _s5_store(gcar, slice(None), g)
        _s5_store(gend_ref, slice(None), g)

    full = lambda a: pl.BlockSpec(a.shape, functools.partial(lambda i, nd: (0,) * nd, nd=a.ndim))
    return _call(
        body, name="s5_bwd_ends", grid=(nt,),
        in_specs=[pl.BlockSpec((rows, S5_WIDTH), lambda i: (nt - 1 - i, 0)), full(mats["c_q"]), full(mats["lam_r"]),
                  full(mats["lam_i"])],
        out_specs=pl.BlockSpec((S5_SEG, S5_COLS), lambda i: (0, 0)),
        out_shape=jax.ShapeDtypeStruct((S5_SEG, S5_COLS), F32),
        scratch_shapes=[pltpu.VMEM((rows, S5_COLS), F32), pltpu.VMEM((S5_SEG, S5_COLS), F32)],
        compiler_params=_params(40),
    )(dy_perm, mats["c_q"], mats["lam_r"], mats["lam_i"])


def _s5_bwd_full(u_perm, dy_perm, hinit, g0, mats):
    S = u_perm.shape[0]
    rows = S5_TILE_STEPS * S5_SEG
    nt = S // rows

    def body(u_ref, dy_ref, hinit_ref, g0_ref, b_ref, c_ref, lr_ref, li_ref, d_ref,
             du_ref, dp_ref, dq_ref, db_ref, dc_ref, dd_ref, hs, gs, gcar):
        @pl.when(pl.program_id(0) == 0)
        def _():
            gcar[...] = g0_ref[...]
            for ref in (dp_ref, dq_ref, db_ref, dc_ref, dd_ref):
                ref[...] = jnp.zeros(ref.shape, F32)

        u, dy = u_ref[...], dy_ref[...]
        ub, dyb = u.astype(BF16), dy.astype(BF16)
        hs[0:S5_SEG, :] = hinit_ref[...]
        for q in range(S5_QUADS):
            cols = slice(q * S5_QCOLS, (q + 1) * S5_QCOLS)
            hs[S5_SEG:, cols] = jnp.dot(ub[:, q * 128:(q + 1) * 128], b_ref[q], preferred_element_type=F32)
            gs[:, cols] = lax.dot_general(dyb[:, q * 128:(q + 1) * 128], c_ref[q], (((1,), (1,)), ((), ())),
                                          preferred_element_type=F32)

        def fstep(t, h):
            sl = pl.ds(pl.multiple_of((t + 1) * S5_SEG, S5_SEG), S5_SEG)
            hn = _s5_advance(h, lr_ref, li_ref, hs, sl, False)
            _s5_store(hs, sl, hn)
            return hn

        _scan_loop(fstep, _s5_parts(hinit_ref[...]))

        def bstep(k, g):
            t = S5_TILE_STEPS - 1 - k
            sl = pl.ds(pl.multiple_of(t * S5_SEG, S5_SEG), S5_SEG)
            gn = _s5_advance(g, lr_ref, li_ref, gs, sl, True)
            _s5_store(gs, sl, gn)
            return gn

        _s5_store(gcar, slice(None), _scan_loop(bstep, _s5_parts(gcar[...])))

        half = S5_QCOLS // 2
        dus = []
        for q in range(S5_QUADS):
            cols = slice(q * S5_QCOLS, (q + 1) * S5_QCOLS)

            def astep(t, carry, q=q):
                sl = pl.ds(pl.multiple_of(t * S5_SEG, S5_SEG), S5_SEG)
                g = gs[sl, q * S5_QCOLS:(q + 1) * S5_QCOLS]
                hp = hs[sl, q * S5_QCOLS:(q + 1) * S5_QCOLS]
                hp_sw = jnp.concatenate([hp[:, half:], hp[:, :half]], axis=1)
                return carry[0] + g * hp, carry[1] + g * hp_sw

            zero = jnp.zeros((S5_SEG, S5_QCOLS), F32)
            acc_p, acc_q = _scan_loop(astep, (zero, zero))
            dp_ref[:, cols] += jnp.sum(acc_p, axis=0, keepdims=True)
            dq_ref[:, cols] += jnp.sum(acc_q, axis=0, keepdims=True)
            gq = gs[:, cols].astype(BF16)
            db_ref[q] += lax.dot_general(ub[:, q * 128:(q + 1) * 128], gq, (((0,), (0,)), ((), ())),
                                         preferred_element_type=F32)
            hq = hs[S5_SEG:, cols].astype(BF16)
            dc_ref[q] += lax.dot_general(dyb[:, q * 128:(q + 1) * 128], hq, (((0,), (0,)), ((), ())),
                                         preferred_element_type=F32)
            dus.append(lax.dot_general(gq, b_ref[q], (((1,), (1,)), ((), ())), preferred_element_type=F32))
        du_ref[...] = (jnp.concatenate(dus, axis=1) + d_ref[...] * dy).astype(du_ref.dtype)
        dd_ref[...] += jnp.sum(dy * u, axis=0, keepdims=True)

    full = lambda a: pl.BlockSpec(a.shape, functools.partial(lambda i, nd: (0,) * nd, nd=a.ndim))
    rev_rows = pl.BlockSpec((rows, S5_WIDTH), lambda i: (nt - 1 - i, 0))
    consts = [mats["b_q"], mats["c_q"], mats["lam_r"], mats["lam_i"], mats["d_row"]]
    acc = lambda s: pl.BlockSpec(s, functools.partial(lambda i, nd: (0,) * nd, nd=len(s)))
    acc_shapes = [(1, S5_COLS), (1, S5_COLS), (S5_QUADS, 128, S5_QCOLS), (S5_QUADS, 128, S5_QCOLS), (1, S5_WIDTH)]
    return _call(
        body, name="s5_bwd_full", grid=(nt,),
        in_specs=[rev_rows, rev_rows, pl.BlockSpec((None, S5_SEG, S5_COLS), lambda i: (nt - 1 - i, 0, 0)), full(g0)]
        + [full(a) for a in consts],
        out_specs=[rev_rows] + [acc(s) for s in acc_shapes],
        out_shape=[jax.ShapeDtypeStruct((S, S5_WIDTH), BF16)] + [jax.ShapeDtypeStruct(s, F32) for s in acc_shapes],
        scratch_shapes=[pltpu.VMEM((rows + S5_SEG, S5_COLS), F32), pltpu.VMEM((rows, S5_COLS), F32),
                        pltpu.VMEM((S5_SEG, S5_COLS), F32)],
        compiler_params=_params(56),
    )(u_perm, dy_perm, hinit, g0, *consts)


def _cmul(ar, ai, br, bi):
    return ar * br - ai * bi, ar * bi + ai * br


def _split_cols(v):
    t = v.reshape(v.shape[0], S5_QUADS, 2, S5_QCOLS // 2)
    return t[:, :, 0], t[:, :, 1]


def _join_cols(re, im):
    return jnp.stack([re, im], axis=2).reshape(re.shape[0], S5_COLS)


def _segment_starts(ends, pow_r, pow_i, reverse):
    er, ei = _split_cols(ends)
    pi = -pow_i if reverse else pow_i
    order = list(range(S5_SEG))
    if reverse:
        order = order[::-1]
    cr, ci = jnp.zeros_like(er[0]), jnp.zeros_like(ei[0])
    out_r, out_i = [None] * S5_SEG, [None] * S5_SEG
    for j in order:
        out_r[j], out_i[j] = cr, ci
        mr, mi = _cmul(pow_r, pi, cr, ci)
        cr, ci = mr + er[j], mi + ei[j]
    return _join_cols(jnp.stack(out_r), jnp.stack(out_i))


def _to_segments(a):
    S, w = a.shape
    return a.reshape(S5_SEG, S // S5_SEG, w).transpose(1, 0, 2).reshape(S, w)


def _from_segments(a):
    S, w = a.shape
    return a.reshape(S // S5_SEG, S5_SEG, w).transpose(1, 0, 2).reshape(S, w)


def _my_pos():
    return lax.axis_index("x"), lax.axis_index("y"), lax.axis_index("c")


def _flip(pos, k):
    x, y, c = pos
    return (1 - x if k & 4 else x, 1 - y if k & 2 else y, 1 - c if k & 1 else c)


def _index_of(pos):
    return 4 * pos[0] + 2 * pos[1] + pos[2]


_GATHER_FLIPS = (0, 1, 4, 5, 2, 3, 6, 7)


def _inproj_gather(x, norm_g, pack_a, pack_b, pack_c):
    S = x.shape[0]
    tm = min(S, 1024)
    n_i = S // tm
    order = jnp.stack([_index_of(_flip(_my_pos(), k)) for k in _GATHER_FLIPS]).astype(jnp.int32)

    def body(order_ref, x_ref, g_ref, pa_ref, pb_ref, pc_ref, ut_ref, proj_ref, proj16_ref, oa_ref, ob_ref, oc_ref,
             wv, u_scr, send_sems, recv_sems, local_sems):
        s, i = pl.program_id(0), pl.program_id(1)
        me = _my_pos()
        mine = _index_of(me)
        sibling = _flip(me, 1)
        srcs = (pb_ref, pa_ref, pc_ref)
        dsts = (wv, oa_ref, oc_ref)

        def direct(a, k):
            return pltpu.make_async_remote_copy(
                src_ref=srcs[a], dst_ref=dsts[a].at[mine], send_sem=send_sems.at[a * 8 + k],
                recv_sem=recv_sems.at[a * 8 + k], device_id=_flip(me, k), device_id_type=MESH)

        def passed_on(a, k):
            slot = _index_of(_flip(me, k))
            return pltpu.make_async_remote_copy(
                src_ref=dsts[a].at[slot], dst_ref=dsts[a].at[slot], send_sem=send_sems.at[a * 8 + (k | 1)],
                recv_sem=recv_sems.at[a * 8 + (k | 1)], device_id=sibling, device_id_type=MESH)

        def arrival(a, k):
            slot = _index_of(_flip(me, k))
            pltpu.make_async_remote_copy(
                src_ref=dsts[a].at[slot], dst_ref=dsts[a].at[slot], send_sem=send_sems.at[a * 8 + k],
                recv_sem=recv_sems.at[a * 8 + k], device_id=me, device_id_type=MESH).wait_recv()

        def own_copy(a):
            return pltpu.make_async_copy(srcs[a], dsts[a].at[mine], local_sems.at[a])

        def keep(idx):
            slot = _index_of(_flip(me, _GATHER_FLIPS[idx]))
            return pltpu.make_async_copy(wv.at[slot], ob_ref.at[slot], local_sems.at[3 + idx])


        first = (s == 0) & (i == 0)

        @pl.when(first)
        def _():
            for a in range(3):
                own_copy(a).start()
            for k in (1, 4, 2):
                direct(0, k).start()
            own_copy(0).wait()
            keep(0).start()

        for idx, k in enumerate(_GATHER_FLIPS):
            if idx == 0:
                continue

            @pl.when((s == idx) & (i == 0))
            def _(idx=idx, k=k):
                arrival(0, k)
                if k in (4, 2, 6):
                    passed_on(0, k).start()
                keep(idx).start()
                if idx == 1:
                    direct(0, 6).start()
                if idx == 2:
                    for a in (1, 2):
                        for k in (1, 4, 2, 6):
                            direct(a, k).start()

        @pl.when(s == 0)
        def _():
            y, _, _ = _rms_fwd(x_ref[...], g_ref[...])
            u_scr[pl.ds(pl.multiple_of(i * tm, tm), tm), :] = y.astype(BF16)
            ut_ref[...] = y.T.astype(BF16)

        ub = u_scr[pl.ds(pl.multiple_of(i * tm, tm), tm), :]
        block = jnp.dot(ub, wv[order_ref[s]], preferred_element_type=F32)
        proj_ref[...] = block
        proj16_ref[...] = block.astype(BF16)

        @pl.when((s == N_DEV - 1) & (i == n_i - 1))
        def _():
            for a in (1, 2):
                for k in (4, 2, 6):
                    arrival(a, k)
                    passed_on(a, k).start()
            for a in (1, 2):
                for k in (1, 5, 3, 7):
                    arrival(a, k)
                own_copy(a).wait()
            for a in range(3):
                for k in (1, 4, 2, 6):
                    direct(a, k).wait_send()
                for k in (4, 2, 6):
                    passed_on(a, k).wait_send()
            for idx in range(N_DEV):
                keep(idx).wait()

    any_spec = pl.BlockSpec(memory_space=pl.ANY)
    vmem = pl.BlockSpec(memory_space=pltpu.VMEM)
    grid_spec = pltpu.PrefetchScalarGridSpec(
        num_scalar_prefetch=1, grid=(N_DEV, n_i),
        in_specs=[pl.BlockSpec((tm, D_MODEL), lambda s, i, o: (jnp.where(s == 0, i, 0), 0)),
                  pl.BlockSpec((1, D_MODEL), lambda s, i, o: (0, 0)), any_spec, vmem, any_spec],
        out_specs=[pl.BlockSpec((D_MODEL, tm), lambda s, i, o: (0, jnp.where(s == 0, i, n_i - 1))),
                   pl.BlockSpec((tm, SHARD_IN), lambda s, i, o: (i, o[s])),
                   pl.BlockSpec((tm, SHARD_IN), lambda s, i, o: (i, o[s])), any_spec, any_spec, any_spec],
        scratch_shapes=[pltpu.VMEM((N_DEV,) + pack_b.shape, BF16), pltpu.VMEM((S, D_MODEL), BF16),
                        pltpu.SemaphoreType.DMA((24,)), pltpu.SemaphoreType.DMA((24,)), pltpu.SemaphoreType.DMA((3 + N_DEV,))],
    )
    return _call(
        body, name="inproj_gather", grid_spec=grid_spec,
        out_shape=[jax.ShapeDtypeStruct((D_MODEL, S), BF16), jax.ShapeDtypeStruct((S, IN_COLS), F32),
                   jax.ShapeDtypeStruct((S, IN_COLS), BF16), jax.ShapeDtypeStruct((N_DEV,) + pack_a.shape, BF16), jax.ShapeDtypeStruct((N_DEV,) + pack_b.shape, BF16),
                   jax.ShapeDtypeStruct((N_DEV,) + pack_c.shape, BF16)],
        compiler_params=_params(60, 2),
    )(order, x, norm_g, pack_a, pack_b, pack_c)


_SCATTER_FLIPS = (7, 6, 5, 4, 3, 2, 1, 0)
_N_CHIPS = 4


def _for_row_chunks(n_rows, chunk, fn):
    def step(c, carry):
        fn(pl.ds(pl.multiple_of(c * chunk, chunk), chunk))
        return carry

    lax.fori_loop(0, n_rows // chunk, step, 0)


def _grad_w_in_scatter(dproj, u_t, rs_a, rs_c, small_partial):
    S = u_t.shape[1]
    tm = min(S, 1024)
    n_i = S // tm
    order = jnp.stack([_index_of(_flip(_my_pos(), k)) for k in _SCATTER_FLIPS]).astype(jnp.int32)
    shapes = ((D_MODEL, SHARD_IN), rs_a.shape[1:], rs_c.shape[1:])
    row_chunk = 128

    def body(order_ref, dp_ref, ut_ref, ra_ref, rc_ref, p_ref, gb_ref, ga_ref, gc_ref, gs_ref, acc, sib_b, d2d_b,
             send_b, ici_b, mine_a, sib_a, ici_a, mine_c, sib_c, ici_c, gath, send_sems, recv_sems, local_sems):
        s, i = pl.program_id(0), pl.program_id(1)
        me = _my_pos()
        sibling = _flip(me, 1)

        my_chip = 2 * me[0] + me[1]

        small_d2d = pltpu.make_async_remote_copy(
            src_ref=p_ref, dst_ref=gath.at[_N_CHIPS], send_sem=send_sems.at[21], recv_sem=recv_sems.at[21],
            device_id=sibling, device_id_type=MESH)

        def small_ici(m):
            return pltpu.make_async_remote_copy(
                src_ref=gath.at[my_chip], dst_ref=gath.at[my_chip], send_sem=send_sems.at[22 + m],
                recv_sem=recv_sems.at[22 + m], device_id=_flip(me, 6 - 2 * m), device_id_type=MESH)
        sib = (sib_b, sib_a, sib_c)
        ici = (ici_b, ici_a, ici_c)
        outs = (gb_ref, ga_ref, gc_ref)

        def to_sibling(arr, m, src):
            return pltpu.make_async_remote_copy(
                src_ref=src, dst_ref=sib[arr].at[m], send_sem=send_sems.at[arr * 7 + m],
                recv_sem=recv_sems.at[arr * 7 + m], device_id=sibling, device_id_type=MESH)

        def over_ici(arr, m, src):
            return pltpu.make_async_remote_copy(
                src_ref=src, dst_ref=ici[arr].at[m], send_sem=send_sems.at[arr * 7 + 4 + m],
                recv_sem=recv_sems.at[arr * 7 + 4 + m], device_id=_flip(me, 6 - 2 * m), device_id_type=MESH)

        def from_sibling(arr, m):
            to_sibling(arr, m, sib[arr].at[m]).wait_recv()

        def from_ici(arr, m):
            over_ici(arr, m, ici[arr].at[m]).wait_recv()

        small = ((1, ra_ref, mine_a), (2, rc_ref, mine_c))

        def local_copy(arr, src, mine, m):
            return pltpu.make_async_copy(src.at[_index_of(_flip(me, 6 - 2 * m))], mine.at[m],
                                         local_sems.at[(arr - 1) * _N_CHIPS + m])

        @pl.when((s == 0) & (i == 0))
        def _():
            small_d2d.start()
            for arr, src, mine in small:
                for m in range(_N_CHIPS):
                    to_sibling(arr, m, src.at[_index_of(_flip(me, 7 - 2 * m))]).start()
                    local_copy(arr, src, mine, m).start()

        @pl.when((s == 1) & (i == 0))
        def _():
            small_d2d.wait_recv()
            gath[my_chip] = p_ref[...] + gath[_N_CHIPS]
            for m in range(_N_CHIPS - 1):
                small_ici(m).start()
            for arr, src, mine in small:
                rows, chunk = shapes[arr][0], 16
                for m in range(_N_CHIPS):
                    local_copy(arr, src, mine, m).wait()
                    from_sibling(arr, m)
                    if m < _N_CHIPS - 1:
                        def add(sl, arr=arr, mine=mine, m=m):
                            mine[m, sl, :] = (mine[m, sl, :].astype(F32) + sib[arr][m, sl, :].astype(F32)).astype(BF16)

                        _for_row_chunks(rows, chunk, add)
                        over_ici(arr, m, mine.at[m]).start()
                    else:
                        def keep(sl, arr=arr, mine=mine, m=m):
                            outs[arr][sl, :] = mine[m, sl, :].astype(F32) + sib[arr][m, sl, :].astype(F32)

                        _for_row_chunks(rows, chunk, keep)

        @pl.when(i == 0)
        def _():
            acc[...] = jnp.zeros(acc.shape, F32)

        acc[...] += jnp.dot(ut_ref[...], dp_ref[...], preferred_element_type=F32)

        def block_rows(c):
            return acc[c * row_chunk:(c + 1) * row_chunk, :]

        for m in range(_N_CHIPS):
            @pl.when((s == 2 * m) & (i == n_i - 1))
            def _(m=m):
                if m > 0:
                    to_sibling(0, m - 1, d2d_b).wait_send()
                for c in range(D_MODEL // row_chunk):
                    d2d_b[c * row_chunk:(c + 1) * row_chunk, :] = block_rows(c).astype(BF16)
                to_sibling(0, m, d2d_b).start()

            @pl.when((s == 2 * m + 1) & (i == n_i - 1))
            def _(m=m):
                from_sibling(0, m)
                slot = m % 2
                if m == 2:
                    over_ici(0, 0, send_b.at[0]).wait_send()
                for c in range(D_MODEL // row_chunk):
                    rows = slice(c * row_chunk, (c + 1) * row_chunk)
                    total = block_rows(c) + sib_b[m, rows, :].astype(F32)
                    if m < _N_CHIPS - 1:
                        send_b[slot, rows, :] = total.astype(BF16)
                    else:
                        gb_ref[rows, :] = total
                if m < _N_CHIPS - 1:
                    over_ici(0, m, send_b.at[slot]).start()

        @pl.when((s == N_DEV - 1) & (i == n_i - 1))
        def _():
            for arr in range(3):
                for m in range(_N_CHIPS - 1):
                    from_ici(arr, m)
                rows = shapes[arr][0]

                def add(sl, arr=arr):
                    outs[arr][sl, :] = (outs[arr][sl, :] + ici[arr][0, sl, :].astype(F32)
                                        + ici[arr][1, sl, :].astype(F32) + ici[arr][2, sl, :].astype(F32))

                _for_row_chunks(rows, 16, add)
            for m in range(_N_CHIPS - 1):
                small_ici(m).wait_recv()
            gs_ref[...] = (gath[0] + gath[1]) + (gath[2] + gath[3])
            small_d2d.wait_send()
            for m in range(_N_CHIPS - 1):
                small_ici(m).wait_send()
            to_sibling(0, _N_CHIPS - 1, d2d_b).wait_send()
            over_ici(0, 1, send_b.at[1]).wait_send()
            over_ici(0, 2, send_b.at[0]).wait_send()
            for arr, src, mine in small:
                for m in range(_N_CHIPS):
                    to_sibling(arr, m, src.at[0]).wait_send()
                for m in range(_N_CHIPS - 1):
                    over_ici(arr, m, mine.at[m]).wait_send()

    any_spec = pl.BlockSpec(memory_space=pl.ANY)
    vmem = pl.BlockSpec(memory_space=pltpu.VMEM)
    half = lambda shp, n: pltpu.VMEM((n,) + tuple(shp), BF16)
    grid_spec = pltpu.PrefetchScalarGridSpec(
        num_scalar_prefetch=1, grid=(N_DEV, n_i),
        in_specs=[pl.BlockSpec((tm, SHARD_IN), lambda s, i, o: (i, o[s])),
                  pl.BlockSpec((D_MODEL, tm), lambda s, i, o: (0, i)), any_spec, any_spec, vmem],
        out_specs=[vmem, vmem, vmem, vmem],
        scratch_shapes=[
            pltpu.VMEM((D_MODEL, SHARD_IN), F32), half(shapes[0], _N_CHIPS), pltpu.VMEM(shapes[0], BF16),
            half(shapes[0], 2), half(shapes[0], _N_CHIPS - 1),
            half(shapes[1], _N_CHIPS), half(shapes[1], _N_CHIPS), half(shapes[1], _N_CHIPS - 1),
            half(shapes[2], _N_CHIPS), half(shapes[2], _N_CHIPS), half(shapes[2], _N_CHIPS - 1),
            pltpu.VMEM((_N_CHIPS + 1,) + small_partial.shape, F32),
            pltpu.SemaphoreType.DMA((25,)), pltpu.SemaphoreType.DMA((25,)), pltpu.SemaphoreType.DMA((2 * _N_CHIPS,))],
    )
    return _call(
        body, name="grad_w_in_scatter", grid_spec=grid_spec,
        out_shape=[jax.ShapeDtypeStruct(shp, F32) for shp in shapes] + [jax.ShapeDtypeStruct(small_partial.shape, F32)],
        compiler_params=_params(60, 2),
    )(order, dproj, u_t, rs_a, rs_c, small_partial)


def _adam_update(g, w, m, v):
    m2 = ADAM_B1 * m + (1.0 - ADAM_B1) * g
    v2 = ADAM_B2 * v + (1.0 - ADAM_B2) * (g * g)
    m_hat = m2 / (1.0 - ADAM_B1 ** ADAM_STEP)
    v_hat = v2 / (1.0 - ADAM_B2 ** ADAM_STEP)
    delta = -ADAM_LR * (m_hat / (jnp.sqrt(v_hat) + ADAM_EPS) + ADAM_WD * w)
    return delta, m2, v2


def _adam_rows(g, w, m, v):
    rows, cols = w.shape
    tm = rows if rows % 256 else 256

    def fn(rv, cr, out):
        return list(_adam_update(*rv)), []

    outs, _ = _rowwise("adamw", fn, rows, tm, [(a, cols, 0) for a in (g, w, m, v)], [], [(cols, F32)] * 3, [], 32)
    return outs


_SMALL = ["norm_g", "hg_lb", "hg_norm_g", "s5_a_re", "s5_a_im", "s5_log_dt", "s5_b_re", "s5_b_im", "s5_c_re",
          "s5_c_im", "s5_d", "b_glu", "ple_norm_g", "final_norm_g"]
_BIG = ["w_in", "w_o_hg", "w_glu", "w_o_s5", "w_out", "w_ple", "w_ple_gate"]
_ORDER = ["norm_g", "w_in", "hg_lb", "hg_norm_g", "w_o_hg", "s5_a_re", "s5_a_im", "s5_log_dt", "s5_b_re", "s5_b_im",
          "s5_c_re", "s5_c_im", "s5_d", "w_glu", "b_glu", "w_o_s5", "w_out", "ple_norm_g", "w_ple", "w_ple_gate",
          "final_norm_g"]


def _pack_small(vals, tail=None):
    parts = []
    for name in _SMALL:
        flat = vals[name].reshape(-1).astype(F32)
        pad = (-flat.shape[0]) % 1024
        parts.append(jnp.pad(flat, (0, pad)))
    tail = jnp.zeros((0,), F32) if tail is None else tail.reshape(-1).astype(F32)
    parts.append(jnp.pad(tail, (0, 1024 - tail.shape[0])))
    return jnp.concatenate(parts).reshape(-1, 128)


def _unpack_small(packed, like):
    flat = packed.reshape(-1)
    out, off = {}, 0
    for name in _SMALL:
        size = like[name].size
        out[name] = flat[off:off + size].reshape(like[name].shape)
        off += size + (-size) % 1024
    return out


def _col_blocks(full):
    k = full.shape[0]
    return full.reshape(k, N_DEV, 128).transpose(1, 0, 2)


def _from_col_blocks(blocks):
    k = blocks.shape[1]
    return blocks.transpose(1, 0, 2).reshape(k, N_DEV * 128)


def kernel(x, p, norm_g, w_in, hg_lb, hg_norm_g, w_o_hg, s5_a_re, s5_a_im, s5_log_dt, s5_b_re, s5_b_im, s5_c_re, s5_c_im, s5_d, w_glu, b_glu, w_o_s5, w_out, ple_norm_g, w_ple, w_ple_gate, final_norm_g, loss_target, m_norm_g, m_w_in, m_hg_lb, m_hg_norm_g, m_w_o_hg, m_s5_a_re, m_s5_a_im, m_s5_log_dt, m_s5_b_re, m_s5_b_im, m_s5_c_re, m_s5_c_im, m_s5_d, m_w_glu, m_b_glu, m_w_o_s5, m_w_out, m_ple_norm_g, m_w_ple, m_w_ple_gate, m_final_norm_g, v_norm_g, v_w_in, v_hg_lb, v_hg_norm_g, v_w_o_hg, v_s5_a_re, v_s5_a_im, v_s5_log_dt, v_s5_b_re, v_s5_b_im, v_s5_c_re, v_s5_c_im, v_s5_d, v_w_glu, v_b_glu, v_w_o_s5, v_w_out, v_ple_norm_g, v_w_ple, v_w_ple_gate, v_final_norm_g):
    args = dict(locals())
    w = {n: args[n] for n in _ORDER}
    m = {n: args["m_" + n] for n in _ORDER}
    v = {n: args["v_" + n] for n in _ORDER}
    xs = x[0]
    ps = p[0, 0]
    tgt = loss_target[0]
    S = xs.shape[0]

    pack_a = jnp.concatenate([w_o_hg[0], w_out[0], w_ple_gate[0]], axis=0).astype(BF16)
    pack_b = w_in[0].astype(BF16)
    pack_c = jnp.concatenate([w_glu[0], w_o_s5[0], w_ple[0]], axis=0).astype(BF16)
    u_t, proj, proj16, all_a, all_b, all_c = _inproj_gather(xs, norm_g, pack_a, pack_b, pack_c)
    wf_o_hg = all_a[:, 0:128].reshape(D_MODEL, D_MODEL)
    wf_out = all_a[:, 128:256].reshape(D_MODEL, D_MODEL)
    wf_pg = all_a[:, 256:384].reshape(D_MODEL, D_MODEL)
    wf_glu = _from_col_blocks(all_c[:, 0:512])
    wf_o_s5 = _from_col_blocks(all_c[:, 512:1024])
    wf_ple = _from_col_blocks(all_c[:, 1024:1280])

    lb = jax.nn.sigmoid(hg_lb[0:1] - hg_lb[1:2])
    s5_names = ["s5_a_re", "s5_a_im", "s5_log_dt", "s5_b_re", "s5_b_im", "s5_c_re", "s5_c_im", "s5_d"]
    build = lambda *a: _s5_matrices(*a, seg_len=S // S5_SEG)
    mats_f32, mats_vjp = jax.vjp(build, *[w[n][0] for n in s5_names])
    mats = dict(mats_f32, b_q=mats_f32["b_q"].astype(BF16), c_q=mats_f32["c_q"].astype(BF16))
    bias_glu = b_glu

    o, states = _hgrn_fwd(proj, lb)
    u_perm = _to_segments(proj[:, COL_US:COL_US + S5_WIDTH])
    zeros_state = jnp.zeros((S5_SEG, S5_COLS), F32)
    (h_ends,) = _s5_fwd_pass(u_perm, mats, zeros_state, False)
    h0 = _segment_starts(h_ends, mats["pow_r"], mats["pow_i"], False)
    y_perm, h_init, _ = _s5_fwd_pass(u_perm, mats, h0, True)
    ys = _from_segments(y_perm)
    y_hg, y_s5, glu, h1 = _stage_branches(o, proj16, ys, xs, hg_norm_g, wf_o_hg, wf_glu, bias_glu, wf_o_s5, wf_out)

    dh1, (loss_acc, d_final_g, d_ple_g, d_w_ple, d_w_pg) = _stage_ple_loss(
        h1, ps, tgt, ple_norm_g, wf_ple, wf_pg, final_norm_g.reshape(1, D_MODEL))
    (d_gate_hg, d_gate_s5, d_yhg, d_ys5), (d_w_out,) = _stage_bwd_merge(dh1, y_hg, y_s5, proj16, wf_out)
    (d_o, d_g_hg), (d_w_o_hg, d_hg_norm) = _stage_bwd_hg_path(d_yhg, o, proj16, hg_norm_g, wf_o_hg)
    (d_ys, d_z), (d_w_o_s5, d_w_glu, d_b_glu) = _stage_bwd_s5_path(d_ys5, ys, glu, proj16, wf_o_s5, wf_glu)
    dq, df, div, d_lb = _hgrn_bwd(proj, lb, d_o, states)
    dy_perm = _to_segments(d_ys)
    g_ends = _s5_bwd_ends(dy_perm, mats)
    g0 = _segment_starts(g_ends, mats["pow_r"], mats["pow_i"], True)
    du_perm, acc_p, acc_q, d_bq, d_cq_t, d_d = _s5_bwd_full(u_perm, dy_perm, h_init, g0, mats)
    d_us = _from_segments(du_perm)
    grad_x, dproj, d_norm_g = _stage_inproj_bwd([dq, df, div, d_g_hg, d_us, d_z, d_gate_hg, d_gate_s5], xs, dh1,
                                                norm_g, all_b)

    p_re, p_im = _split_cols(acc_p)
    q_re, q_im = _split_cols(acc_q)
    d_lam_r = (p_re + p_im)[0]
    d_lam_i = (q_im - q_re)[0]
    zero_row = jnp.zeros((S5_SEG, S5_COLS), F32)
    row_of = lambda re_part, im_part: zero_row.at[0].set(_join_cols(re_part[None], im_part[None])[0])
    zeros_q = jnp.zeros_like(d_lam_r)
    cot = dict(
        lam_r=row_of(d_lam_r, zeros_q), lam_i=row_of(zeros_q, d_lam_i),
        b_q=d_bq, c_q=d_cq_t.transpose(0, 2, 1), d_row=d_d,
        pow_r=jnp.zeros_like(mats["pow_r"]), pow_i=jnp.zeros_like(mats["pow_i"]),
    )
    d_s5 = mats_vjp(cot)

    s_lb = lb * (1.0 - lb)
    d_hg_lb = jnp.concatenate([d_lb * s_lb, -d_lb * s_lb], axis=0)
    small_g = dict(norm_g=d_norm_g, hg_lb=d_hg_lb, hg_norm_g=d_hg_norm, b_glu=d_b_glu, ple_norm_g=d_ple_g,
                   final_norm_g=d_final_g)
    for name, g in zip(s5_names, d_s5):
        small_g[name] = g
    pk = lambda d: _pack_small({n: d[n] for n in _SMALL})
    rs_a = jnp.concatenate([d_w_o_hg.reshape(N_DEV, 128, D_MODEL), d_w_out.reshape(N_DEV, 128, D_MODEL),
                            d_w_pg.reshape(N_DEV, 128, D_MODEL)], axis=1).astype(BF16)
    rs_c = jnp.concatenate([_col_blocks(d_w_glu), _col_blocks(d_w_o_s5), _col_blocks(d_w_ple)], axis=1).astype(BF16)
    partial = _pack_small({n: small_g[n] for n in _SMALL}, tail=loss_acc[0, 0:1])
    g_b, g_a, g_c, sg = _grad_w_in_scatter(dproj, u_t, rs_a, rs_c, partial)
    sd, sm, sv = _adam_rows(sg, pk(w), pk(m), pk(v))
    like = {n: w[n] for n in _SMALL}
    out_g, out_d, out_m, out_v = (_unpack_small(t, like) for t in (sg, sd, sm, sv))
    big_g = dict(w_o_hg=g_a[0:128], w_out=g_a[128:256], w_ple_gate=g_a[256:384], w_in=g_b,
                 w_glu=g_c[0:512], w_o_s5=g_c[512:1024], w_ple=g_c[1024:1280])
    for name in _BIG:
        shape = w[name].shape
        g2 = big_g[name]
        d2, m2, v2 = _adam_rows(g2, w[name][0], m[name][0], v[name][0])
        out_g[name], out_d[name], out_m[name], out_v[name] = (t.reshape(shape) for t in (g2, d2, m2, v2))

    loss = sg[sg.shape[0] - 8, 0]
    return (loss, grad_x[None], *[out_g[n] for n in _ORDER], *[out_d[n] for n in _ORDER],
            *[out_m[n] for n in _ORDER], *[out_v[n] for n in _ORDER])
```

```python
import functools
import math

import jax
import jax.numpy as jnp
from jax import lax
from jax.experimental import pallas as pl
from jax.experimental.pallas import tpu as pltpu

F32 = jnp.float32
BF16 = jnp.bfloat16

D_MODEL = 1024
N_DEV = 8
IN_COLS = 7168
SHARD_IN = IN_COLS // N_DEV
HG_HEADS = 8
HG_DIM = 128
HG_CHUNK = 64
HG_SUPER_FWD = 256
HG_SUPER_BWD = 128
HG_HEADS_PER_STEP = 8
S5_WIDTH = 512
S5_GROUPS = 32
S5_STATE = 64
S5_CH = 16
S5_SEG = 8
S5_QUADS = 4
S5_QCOLS = 1024
S5_COLS = S5_QUADS * S5_QCOLS
S5_TILE_STEPS = 64
S5_UNROLL = 8
NORM_EPS = 1e-6
ADAM_LR = 0.001
ADAM_B1 = 0.9
ADAM_B2 = 0.999
ADAM_EPS = 1e-08
ADAM_WD = 0.01
ADAM_STEP = 10
MIB = 1024 * 1024
MESH = pl.DeviceIdType.MESH

COL_Q, COL_F, COL_I, COL_G, COL_US, COL_ZS, COL_GH, COL_GS = 0, 1024, 2048, 3072, 4096, 4608, 5120, 6144


def _call(body, **kw):
    return pl.pallas_call(body, **kw)


def _params(vmem_mb, n_grid=1):
    return pltpu.CompilerParams(
        dimension_semantics=("arbitrary",) * n_grid, vmem_limit_bytes=vmem_mb * MIB
    )


def _bdot(a, b):
    return jnp.dot(a.astype(BF16), b.astype(BF16), preferred_element_type=F32)


def _bdot_nt(a, b):
    return lax.dot_general(a.astype(BF16), b.astype(BF16), (((1,), (1,)), ((), ())), preferred_element_type=F32)


def _bdot_tn(a, b):
    return lax.dot_general(a.astype(BF16), b.astype(BF16), (((0,), (0,)), ((), ())), preferred_element_type=F32)


def _sigmoid(x):
    return jax.nn.sigmoid(x)


def _silu(x):
    return x * _sigmoid(x)


def _dsilu(x):
    s = _sigmoid(x)
    return s * (1.0 + x * (1.0 - s))


_GELU_C = math.sqrt(2.0 / math.pi)


def _gelu(x):
    return 0.5 * x * (1.0 + jnp.tanh(_GELU_C * (x + 0.044715 * x * x * x)))


def _dgelu(x):
    t = jnp.tanh(_GELU_C * (x + 0.044715 * x * x * x))
    return 0.5 * (1.0 + t) + 0.5 * x * (1.0 - t * t) * _GELU_C * (1.0 + 3.0 * 0.044715 * x * x)


def _rms_fwd(x, g):
    r = lax.rsqrt(jnp.mean(x * x, axis=-1, keepdims=True) + NORM_EPS)
    n = x * r
    return n * g, n, r


def _rms_bwd(dy, n, r, g):
    dn = dy * g
    dx = r * (dn - n * jnp.mean(dn * n, axis=-1, keepdims=True))
    return dx, jnp.sum(dy * n, axis=0, keepdims=True)


def _head_rms_fwd(o, g):
    ns, rs = [], []
    for h in range(HG_HEADS):
        oh = o[:, h * HG_DIM:(h + 1) * HG_DIM]
        r = lax.rsqrt(jnp.mean(oh * oh, axis=-1, keepdims=True) + NORM_EPS)
        ns.append(oh * r)
        rs.append(r)
    n = jnp.concatenate(ns, axis=1)
    return n * g, n, rs


def _head_rms_bwd(dy, n, rs, g):
    dn = dy * g
    dxs = []
    for h in range(HG_HEADS):
        sl = slice(h * HG_DIM, (h + 1) * HG_DIM)
        dxs.append(rs[h] * (dn[:, sl] - n[:, sl] * jnp.mean(dn[:, sl] * n[:, sl], axis=-1, keepdims=True)))
    return jnp.concatenate(dxs, axis=1), jnp.sum(dy * n, axis=0, keepdims=True)


def _rowwise(name, fn, n_rows, tm, rows, consts, out_rows, out_accs, vmem_mb, parts=1):
    n_r, n_c, n_or, n_oa = len(rows), len(consts), len(out_rows), len(out_accs)
    tp = tm // parts

    def body(*refs):
        r_refs = refs[:n_r]
        c_refs = refs[n_r:n_r + n_c]
        or_refs = refs[n_r + n_c:n_r + n_c + n_or]
        oa_refs = refs[n_r + n_c + n_or:]

        if n_oa:
            @pl.when(pl.program_id(0) == 0)
            def _():
                for ref in oa_refs:
                    ref[...] = jnp.zeros(ref.shape, ref.dtype)

        for part in range(parts):
            sl = slice(part * tp, (part + 1) * tp)
            outs, accs = fn([r[sl, :] for r in r_refs], c_refs, [o.at[sl, :] for o in or_refs])
            for ref, v in zip(or_refs, outs):
                if v is not None:
                    ref[sl, :] = v.astype(ref.dtype)
            for ref, v in zip(oa_refs, accs):
                ref[...] += v.astype(ref.dtype)

    in_specs = [pl.BlockSpec((tm, w), functools.partial(lambda i, c: (i, c), c=cb)) for (_, w, cb) in rows]
    in_specs += [pl.BlockSpec(c.shape, functools.partial(lambda i, nd: (0,) * nd, nd=c.ndim),
                              pipeline_mode=pl.Buffered(1)) for c in consts]
    out_specs = [pl.BlockSpec((tm, w), lambda i: (i, 0)) for (w, _) in out_rows]
    out_specs += [pl.BlockSpec(s, functools.partial(lambda i, nd: (0,) * nd, nd=len(s))) for (s, _) in out_accs]
    out_shape = [jax.ShapeDtypeStruct((n_rows, w), dt) for (w, dt) in out_rows]
    out_shape += [jax.ShapeDtypeStruct(s, dt) for (s, dt) in out_accs]
    res = _call(
        body, name=name, grid=(n_rows // tm,), in_specs=in_specs, out_specs=out_specs, out_shape=out_shape,
        compiler_params=_params(vmem_mb),
    )(*[a for (a, _, _) in rows], *consts)
    return res[:n_or], res[n_or:]


def _stage_branches(o, proj, ys, x, hg_norm_g, w_o_hg, w_glu, b_glu, w_o_s5, w_out):
    S = x.shape[0]

    def fn(rv, cr, out):
        o_b, ys_b, x_b = rv[0], rv[5], rv[6]
        g_hg, z_s, gate_hg, gate_s5 = (v.astype(F32) for v in rv[1:5])
        gn_ref, wohg_ref, wglu_ref, bglu_ref, wos5_ref, wout_ref = cr
        on, _, _ = _head_rms_fwd(o_b, gn_ref[...])
        a = on * _silu(g_hg)
        y_hg = jnp.dot(a.astype(BF16), wohg_ref[...], preferred_element_type=F32)
        gl = _gelu(ys_b)
        glu = jnp.dot(gl.astype(BF16), wglu_ref[...], preferred_element_type=F32) + bglu_ref[...]
        ys2 = glu[:, :S5_WIDTH] * _sigmoid(glu[:, S5_WIDTH:]) * _silu(z_s)
        y_s5 = jnp.dot(ys2.astype(BF16), wos5_ref[...], preferred_element_type=F32)
        merged = _sigmoid(gate_hg) * y_hg + _sigmoid(gate_s5) * y_s5
        h1 = x_b + jnp.dot(merged.astype(BF16), wout_ref[...], preferred_element_type=F32)
        return [y_hg, y_s5, glu, h1], []

    rows = [(o, D_MODEL, 0), (proj, D_MODEL, COL_G // D_MODEL), (proj, S5_WIDTH, COL_ZS // S5_WIDTH),
            (proj, D_MODEL, COL_GH // D_MODEL), (proj, D_MODEL, COL_GS // D_MODEL), (ys, S5_WIDTH, 0), (x, D_MODEL, 0)]
    (y_hg, y_s5, glu, h1), _ = _rowwise(
        "branches", fn, S, 512, rows, [hg_norm_g, w_o_hg, w_glu, b_glu, w_o_s5, w_out],
        [(D_MODEL, BF16), (D_MODEL, BF16), (D_MODEL, BF16), (D_MODEL, F32)], [], 56, parts=2)
    return y_hg, y_s5, glu, h1


def _stage_ple_loss(h1, p, target, ple_norm_g, w_ple, w_ple_gate, final_norm_g):
    S = h1.shape[0]

    def fn(rv, cr, out):
        h1_b, p_b, t_b = rv
        gp_ref, wple_ref, wpg_ref, gf_ref = cr
        n2g, n2, r2 = _rms_fwd(h1_b, gp_ref[...])
        z = jnp.dot(n2g.astype(BF16), wpg_ref[...], preferred_element_type=F32)
        gate = _sigmoid(z)
        pe = jnp.dot(p_b.astype(BF16), wple_ref[...], preferred_element_type=F32)
        h2 = h1_b + pe * gate
        y, nf, rf = _rms_fwd(h2, gf_ref[...])
        err = y - t_b
        loss_rows = 0.5 * jnp.mean(err * err, axis=-1, keepdims=True)
        loss_inc = jnp.broadcast_to(jnp.sum(loss_rows, axis=0, keepdims=True), (1, 128))
        dy = err * (1.0 / D_MODEL)
        dh2, d_gf = _rms_bwd(dy, nf, rf, gf_ref[...])
        d_pe = dh2 * gate
        dz = dh2 * pe * gate * (1.0 - gate)
        d_wple = _bdot_tn(p_b, d_pe)
        d_wpg = _bdot_tn(n2g, dz)
        dn2g = _bdot_nt(dz, wpg_ref[...])
        dh1n, d_gp = _rms_bwd(dn2g, n2, r2, gp_ref[...])
        return [dh2 + dh1n], [loss_inc, d_gf, d_gp, d_wple, d_wpg]

    (dh1,), accs = _rowwise(
        "ple_loss", fn, S, 512, [(h1, D_MODEL, 0), (p, 256, 0), (target, D_MODEL, 0)],
        [ple_norm_g, w_ple, w_ple_gate, final_norm_g], [(D_MODEL, F32)],
        [((1, 128), F32), ((1, D_MODEL), F32), ((1, D_MODEL), F32), ((256, D_MODEL), F32), ((D_MODEL, D_MODEL), F32)], 56,
        parts=2)
    return dh1, accs


def _stage_bwd_merge(dh1, y_hg, y_s5, proj, w_out):
    S = dh1.shape[0]

    def fn(rv, cr, out):
        dh1_b, yhg, ys5 = rv[:3]
        gate_hg, gate_s5 = rv[3].astype(F32), rv[4].astype(F32)
        (wout_ref,) = cr
        sg_h, sg_s = _sigmoid(gate_hg), _sigmoid(gate_s5)
        merged = sg_h * yhg + sg_s * ys5
        d_wout = _bdot_tn(merged, dh1_b)
        d_merged = _bdot_nt(dh1_b, wout_ref[...])
        d_gate_hg = d_merged * yhg * sg_h * (1.0 - sg_h)
        d_gate_s5 = d_merged * ys5 * sg_s * (1.0 - sg_s)
        return [d_gate_hg, d_gate_s5, d_merged * sg_h, d_merged * sg_s], [d_wout]

    rows = [(dh1, D_MODEL, 0), (y_hg, D_MODEL, 0), (y_s5, D_MODEL, 0), (proj, D_MODEL, COL_GH // D_MODEL),
            (proj, D_MODEL, COL_GS // D_MODEL)]
    outs, accs = _rowwise("bwd_merge", fn, S, 512, rows, [w_out], [(D_MODEL, BF16)] * 4,
                          [((D_MODEL, D_MODEL), F32)], 56, parts=2)
    return outs, accs


def _stage_bwd_hg_path(d_yhg, o, proj, hg_norm_g, w_o_hg):
    S = o.shape[0]

    def fn(rv, cr, out):
        d_yhg_b, o_b, g_hg = rv[0], rv[1], rv[2].astype(F32)
        gn_ref, wohg_ref = cr
        ong, on, rs = _head_rms_fwd(o_b, gn_ref[...])
        sil = _silu(g_hg)
        d_wohg = _bdot_tn(ong * sil, d_yhg_b)
        d_a = _bdot_nt(d_yhg_b, wohg_ref[...])
        d_g_hg = d_a * ong * _dsilu(g_hg)
        d_o, d_gn = _head_rms_bwd(d_a * sil, on, rs, gn_ref[...])
        return [d_o, d_g_hg], [d_wohg, d_gn]

    rows = [(d_yhg, D_MODEL, 0), (o, D_MODEL, 0), (proj, D_MODEL, COL_G // D_MODEL)]
    outs, accs = _rowwise("bwd_hg_path", fn, S, 512, rows, [hg_norm_g, w_o_hg], [(D_MODEL, BF16)] * 2,
                          [((D_MODEL, D_MODEL), F32), ((1, D_MODEL), F32)], 56, parts=2)
    return outs, accs


def _stage_bwd_s5_path(d_ys5, ys, glu, proj, w_o_s5, w_glu):
    S = ys.shape[0]

    def fn(rv, cr, out):
        d_ys5_b, ys_b, glu_b, z_s = rv[0], rv[1], rv[2].astype(F32), rv[3].astype(F32)
        wos5_ref, wglu_ref = cr
        ga, gb = glu_b[:, :S5_WIDTH], glu_b[:, S5_WIDTH:]
        sgb, silz = _sigmoid(gb), _silu(z_s)
        ys2 = ga * sgb * silz
        d_wos5 = _bdot_tn(ys2, d_ys5_b)
        d_ys2 = _bdot_nt(d_ys5_b, wos5_ref[...])
        d_ga = d_ys2 * sgb * silz
        d_gb = d_ys2 * ga * sgb * (1.0 - sgb) * silz
        d_z = d_ys2 * ga * sgb * _dsilu(z_s)
        d_glu = jnp.concatenate([d_ga, d_gb], axis=1)
        gl = _gelu(ys_b)
        d_wglu = _bdot_tn(gl, d_glu)
        d_bglu = jnp.sum(d_glu, axis=0, keepdims=True)
        d_gl = _bdot_nt(d_glu, wglu_ref[...])
        return [d_gl * _dgelu(ys_b), d_z], [d_wos5, d_wglu, d_bglu]

    rows = [(d_ys5, D_MODEL, 0), (ys, S5_WIDTH, 0), (glu, D_MODEL, 0), (proj, S5_WIDTH, COL_ZS // S5_WIDTH)]
    outs, accs = _rowwise(
        "bwd_s5_path", fn, S, 512, rows, [w_o_s5, w_glu], [(S5_WIDTH, F32), (S5_WIDTH, BF16)],
        [((S5_WIDTH, D_MODEL), F32), ((S5_WIDTH, D_MODEL), F32), ((1, D_MODEL), F32)], 48, parts=2)
    return outs, accs


def _stage_inproj_bwd(pieces, x, dh1, norm_g, w_in_all):
    S = x.shape[0]

    def fn(rv, cr, out):
        g_ref, w_ref = cr
        x_b, dh1_b = rv[8], rv[9]
        dproj_ref = out[1]
        col = 0
        for v in rv[:8]:
            dproj_ref[:, col:col + v.shape[1]] = v.astype(BF16)
            col += v.shape[1]
        d_u = jnp.zeros((x_b.shape[0], D_MODEL), F32)
        for j in range(N_DEV):
            d_u = d_u + lax.dot_general(dproj_ref[:, j * SHARD_IN:(j + 1) * SHARD_IN], w_ref[j],
                                        (((1,), (1,)), ((), ())), preferred_element_type=F32)
        _, n, r = _rms_fwd(x_b, g_ref[...])
        dx, d_g = _rms_bwd(d_u, n, r, g_ref[...])
        return [dh1_b + dx, None], [d_g]

    rows = [(a, a.shape[1], 0) for a in pieces] + [(x, D_MODEL, 0), (dh1, D_MODEL, 0)]
    (grad_x, dproj), (d_g,) = _rowwise(
        "inproj_bwd", fn, S, 256, rows, [norm_g, w_in_all], [(D_MODEL, F32), (IN_COLS, BF16)],
        [((1, D_MODEL), F32)], 56)
    return grad_x, dproj, d_g


def _chunk_row(shape):
    return lax.broadcasted_iota(jnp.int32, shape, 0) & (HG_CHUNK - 1)


def _chunk_cumsum(x):
    r_in = _chunk_row(x.shape)
    s = 1
    while s < HG_CHUNK:
        x = x + jnp.where(r_in >= s, pltpu.roll(x, s, 0), 0.0)
        s *= 2
    return x


def _chunk_suffix_sum(x):
    n = x.shape[0]
    r_in = _chunk_row(x.shape)
    s = 1
    while s < HG_CHUNK:
        x = x + jnp.where(r_in < HG_CHUNK - s, pltpu.roll(x, n - s, 0), 0.0)
        s *= 2
    return x


def _hgrn_prep(q, fl, lb):
    sup = q.shape[0]
    nc = sup // HG_CHUNK
    sig = _sigmoid(fl)
    f = lb + (1.0 - lb) * sig
    k = (1.0 - lb) * (1.0 - sig)
    b = _chunk_cumsum(jnp.log(f))
    b3 = b.reshape(nc, HG_CHUNK, HG_DIM)
    row3 = lax.broadcasted_iota(jnp.int32, b3.shape, 1)
    pick = lambda r: jnp.sum(jnp.where(row3 == r, b3, 0.0), axis=1, keepdims=True)
    b_mid = pick(HG_CHUNK // 2 - 1)
    b_last = pick(HG_CHUNK - 1)
    flat = lambda t: t.reshape(sup, HG_DIM)
    e_qa = flat(jnp.exp(b3 - b_mid))
    e_ka = flat(jnp.exp(b_mid - b3))
    e_qd = jnp.exp(b)
    e_kd = flat(jnp.exp(b_last - b3))
    dc = jnp.exp(b_last)
    return sig, f, k, e_qa, e_ka, e_qd, e_kd, dc


def _hgrn_mask(sup):
    r = lax.broadcasted_iota(jnp.int32, (sup, sup), 0)
    c = lax.broadcasted_iota(jnp.int32, (sup, sup), 1)
    shift = HG_CHUNK.bit_length() - 1
    return (jnp.right_shift(r, shift) == jnp.right_shift(c, shift)) & (r >= c)


def _hgrn_fwd(proj, lb):
    S = proj.shape[0]
    sup = HG_SUPER_FWD
    nb = S // sup
    nc = sup // HG_CHUNK
    hp = HG_HEADS_PER_STEP
    wide = hp * HG_DIM

    def body(q_ref, f_ref, iv_ref, lb_ref, o_ref, st_ref, state):
        @pl.when(pl.program_id(1) == 0)
        def _():
            state[...] = jnp.zeros(state.shape, F32)

        mask = _hgrn_mask(sup)
        for hh in range(hp):
            lanes = slice(hh * HG_DIM, (hh + 1) * HG_DIM)
            q, iv = q_ref[:, lanes], iv_ref[:, lanes]
            _, _, k, e_qa, e_ka, e_qd, e_kd, dc = _hgrn_prep(q, f_ref[:, lanes], lb_ref[:, lanes])
            scores = jnp.where(mask, _bdot_nt(q * e_qa, k * e_ka), 0.0)
            o_intra = _bdot(scores, iv)
            qd, kd = q * e_qd, k * e_kd
            for c in range(nc):
                sl = slice(c * HG_CHUNK, (c + 1) * HG_CHUNK)
                st = state[hh]
                st_ref[hh, c] = st
                o_ref[sl, lanes] = o_intra[sl] + _bdot_nt(qd[sl], st)
                state[hh] = dc[c] * st + _bdot_tn(iv[sl], kd[sl])

    blk = lambda base: pl.BlockSpec((sup, wide), functools.partial(lambda h, i, b: (i, b + h), b=base // wide))
    return _call(
        body, name="hgrn_fwd", grid=(HG_HEADS // hp, nb),
        in_specs=[blk(COL_Q), blk(COL_F), blk(COL_I), pl.BlockSpec((1, wide), lambda h, i: (0, h))],
        out_specs=[pl.BlockSpec((sup, wide), lambda h, i: (i, h)),
                   pl.BlockSpec((hp, nc, HG_DIM, HG_DIM), lambda h, i: (h, i, 0, 0))],
        out_shape=[jax.ShapeDtypeStruct((S, D_MODEL), F32),
                   jax.ShapeDtypeStruct((HG_HEADS, S // HG_CHUNK, HG_DIM, HG_DIM), F32)],
        scratch_shapes=[pltpu.VMEM((hp, HG_DIM, HG_DIM), F32)],
        compiler_params=_params(40, 2),
    )(proj, proj, proj, lb)


def _hgrn_bwd(proj, lb, d_o, states):
    S = proj.shape[0]
    sup = HG_SUPER_BWD
    nb = S // sup
    nc = sup // HG_CHUNK
    hp = HG_HEADS_PER_STEP
    wide = hp * HG_DIM

    def body(q_ref, f_ref, iv_ref, lb_ref, do_ref, st_ref, dq_ref, df_ref, div_ref, dlb_ref, dstate):
        @pl.when(pl.program_id(1) == 0)
        def _():
            dstate[...] = jnp.zeros(dstate.shape, F32)
            dlb_ref[...] = jnp.zeros(dlb_ref.shape, F32)

        mask = _hgrn_mask(sup)
        for hh in range(hp):
            lanes = slice(hh * HG_DIM, (hh + 1) * HG_DIM)
            q, iv, do, lb_v = q_ref[:, lanes], iv_ref[:, lanes], do_ref[:, lanes], lb_ref[:, lanes]
            sig, f, k, e_qa, e_ka, e_qd, e_kd, dc = _hgrn_prep(q, f_ref[:, lanes], lb_v)
            qa, ka, qd, kd = q * e_qa, k * e_ka, q * e_qd, k * e_kd
            scores = jnp.where(mask, _bdot_nt(qa, ka), 0.0)
            d_scores = jnp.where(mask, _bdot_nt(do, iv), 0.0)
            d_iv_intra = _bdot_tn(scores, do)
            d_qa = _bdot(d_scores, ka)
            d_ka = _bdot_tn(d_scores, qa)
            d_qd, d_kd, d_last = [None] * nc, [None] * nc, [None] * nc
            for c in reversed(range(nc)):
                sl = slice(c * HG_CHUNK, (c + 1) * HG_CHUNK)
                st = st_ref[hh, c]
                ds = dstate[hh]
                d_qd[c] = _bdot(do[sl], st)
                d_kd[c] = _bdot(iv[sl], ds)
                div_ref[sl, lanes] = (d_iv_intra[sl] + _bdot_nt(kd[sl], ds)).astype(div_ref.dtype)
                d_last[c] = (jnp.sum(ds * st, axis=0, keepdims=True) * dc[c]
                             + jnp.sum(d_kd[c] * kd[sl], axis=0, keepdims=True))
                dstate[hh] = dc[c] * ds + _bdot_tn(do[sl], qd[sl])
            d_qd = jnp.concatenate(d_qd, axis=0)
            d_kd = jnp.concatenate(d_kd, axis=0)
            d_b = d_qa * qa - d_ka * ka + d_qd * qd - d_kd * kd
            last_rows = jnp.concatenate([jnp.broadcast_to(t, (HG_CHUNK, HG_DIM)) for t in d_last], axis=0)
            d_b = d_b + jnp.where(_chunk_row(d_b.shape) == HG_CHUNK - 1, last_rows, 0.0)
            d_logf = _chunk_suffix_sum(d_b)
            d_k = d_ka * e_ka + d_kd * e_kd
            g_f = d_logf / f
            d_sig = (g_f - d_k) * (1.0 - lb_v)
            dq_ref[:, lanes] = (d_qa * e_qa + d_qd * e_qd).astype(dq_ref.dtype)
            df_ref[:, lanes] = (d_sig * sig * (1.0 - sig)).astype(df_ref.dtype)
            d_lb = jnp.sum((g_f - d_k) * (1.0 - sig), axis=0, keepdims=True)
            dlb_ref[:, lanes] += jnp.broadcast_to(d_lb, (8, HG_DIM))

    rev = lambda i: nb - 1 - i
    blk = lambda base: pl.BlockSpec((sup, wide), functools.partial(lambda h, i, b: (rev(i), b + h), b=base // wide))
    row_out = pl.BlockSpec((sup, wide), lambda h, i: (rev(i), h))
    dq, df, div, dlb = _call(
        body, name="hgrn_bwd", grid=(HG_HEADS // hp, nb),
        in_specs=[blk(COL_Q), blk(COL_F), blk(COL_I), pl.BlockSpec((1, wide), lambda h, i: (0, h)),
                  pl.BlockSpec((sup, wide), lambda h, i: (rev(i), h)),
                  pl.BlockSpec((hp, nc, HG_DIM, HG_DIM), lambda h, i: (h, rev(i), 0, 0))],
        out_specs=[row_out, row_out, row_out, pl.BlockSpec((8, wide), lambda h, i: (0, h))],
        out_shape=[jax.ShapeDtypeStruct((S, D_MODEL), BF16)] * 3 + [jax.ShapeDtypeStruct((8, D_MODEL), F32)],
        scratch_shapes=[pltpu.VMEM((hp, HG_DIM, HG_DIM), F32)],
        compiler_params=_params(40, 2),
    )(proj, proj, proj, lb, d_o, states)
    return dq, df, div, dlb[0:1]


def _s5_matrices(a_re, a_im, log_dt, b_re, b_im, c_re, c_im, d, seg_len):
    dt = jnp.exp(log_dt)[:, None]
    mag = jnp.exp(a_re * dt)
    lr, li = mag * jnp.cos(a_im * dt), mag * jnp.sin(a_im * dt)
    den = a_re * a_re + a_im * a_im
    nr = lr - 1.0
    sr = (nr * a_re + li * a_im) / den
    si = (li * a_re - nr * a_im) / den
    bbr = sr[..., None] * b_re - si[..., None] * b_im
    bbi = sr[..., None] * b_im + si[..., None] * b_re
    eye = jnp.eye(8, dtype=F32)

    def quad_cols(v):
        return v.reshape(S5_QUADS, 8 * S5_STATE)

    def lam_row(re_part, im_part):
        row = jnp.concatenate([quad_cols(re_part), quad_cols(im_part)], axis=1).reshape(1, S5_COLS)
        return jnp.broadcast_to(row, (S5_SEG, S5_COLS))

    def b_mat(bb):
        t = bb.reshape(S5_QUADS, 8, S5_STATE, S5_CH)
        return jnp.einsum("qgnc,gh->qgchn", t, eye).reshape(S5_QUADS, 8 * S5_CH, 8 * S5_STATE)

    def c_mat(cc):
        t = cc.reshape(S5_QUADS, 8, S5_CH, S5_STATE)
        return jnp.einsum("qgcn,gh->qgnhc", t, eye).reshape(S5_QUADS, 8 * S5_STATE, 8 * S5_CH)

    ang = a_im * dt * seg_len
    magp = jnp.exp(a_re * dt * seg_len)
    lpr, lpi = magp * jnp.cos(ang), magp * jnp.sin(ang)
    return dict(
        lam_r=lam_row(lr, lr), lam_i=lam_row(-li, li),
        b_q=jnp.concatenate([b_mat(bbr), b_mat(bbi)], axis=2),
        c_q=jnp.concatenate([c_mat(c_re), -c_mat(c_im)], axis=1),
        d_row=d.reshape(1, S5_WIDTH), pow_r=quad_cols(lpr), pow_i=quad_cols(lpi),
    )


def _s5_parts(v):
    half = S5_QCOLS // 2
    return tuple(v[:, k * half:(k + 1) * half] for k in range(2 * S5_QUADS))


def _s5_advance(parts, lr_ref, li_ref, x_ref, sl, conj):
    half = S5_QCOLS // 2
    out = []
    for q in range(S5_QUADS):
        re_c = slice(q * S5_QCOLS, q * S5_QCOLS + half)
        im_c = slice(q * S5_QCOLS + half, (q + 1) * S5_QCOLS)
        lr, li = lr_ref[:, re_c], li_ref[:, im_c]
        hr, hi = parts[2 * q], parts[2 * q + 1]
        if conj:
            out += [lr * hr + li * hi + x_ref[sl, re_c], lr * hi - li * hr + x_ref[sl, im_c]]
        else:
            out += [lr * hr - li * hi + x_ref[sl, re_c], lr * hi + li * hr + x_ref[sl, im_c]]
    return tuple(out)


def _scan_loop(step, init):
    def trip(o, carry):
        for j in range(S5_UNROLL):
            carry = step(o * S5_UNROLL + j, carry)
        return carry

    return lax.fori_loop(0, S5_TILE_STEPS // S5_UNROLL, trip, init)


def _s5_store(ref, sl, parts):
    half = S5_QCOLS // 2
    for k, v in enumerate(parts):
        ref[sl, k * half:(k + 1) * half] = v


def _s5_fwd_pass(u_perm, mats, h0, with_output):
    S = u_perm.shape[0]
    rows = S5_TILE_STEPS * S5_SEG
    nt = S // rows

    def body(*refs):
        if with_output:
            u_ref, b_ref, lr_ref, li_ref, h0_ref, c_ref, d_ref, y_ref, hinit_ref, hend_ref, xs, hcar = refs
        else:
            u_ref, b_ref, lr_ref, li_ref, h0_ref, hend_ref, xs, hcar = refs

        @pl.when(pl.program_id(0) == 0)
        def _():
            hcar[...] = h0_ref[...]

        if with_output:
            hinit_ref[...] = hcar[...]
        u = u_ref[...]
        ub = u.astype(BF16)
        for q in range(S5_QUADS):
            xs[:, q * S5_QCOLS:(q + 1) * S5_QCOLS] = jnp.dot(ub[:, q * 128:(q + 1) * 128], b_ref[q], preferred_element_type=F32)

        def step(t, h):
            sl = pl.ds(pl.multiple_of(t * S5_SEG, S5_SEG), S5_SEG)
            hn = _s5_advance(h, lr_ref, li_ref, xs, sl, False)
            _s5_store(xs, sl, hn)
            return hn

        h = _scan_loop(step, _s5_parts(hcar[...]))
        _s5_store(hcar, slice(None), h)
        _s5_store(hend_ref, slice(None), h)
        if with_output:
            ys = [jnp.dot(xs[:, q * S5_QCOLS:(q + 1) * S5_QCOLS].astype(BF16), c_ref[q], preferred_element_type=F32)
                  for q in range(S5_QUADS)]
            y_ref[...] = jnp.concatenate(ys, axis=1) + d_ref[...] * u

    full = lambda a: pl.BlockSpec(a.shape, functools.partial(lambda i, nd: (0,) * nd, nd=a.ndim))
    ins = [u_perm, mats["b_q"], mats["lam_r"], mats["lam_i"], h0]
    in_specs = [pl.BlockSpec((rows, S5_WIDTH), lambda i: (i, 0))] + [full(a) for a in ins[1:]]
    out_specs = [pl.BlockSpec((S5_SEG, S5_COLS), lambda i: (0, 0))]
    out_shape = [jax.ShapeDtypeStruct((S5_SEG, S5_COLS), F32)]
    if with_output:
        ins += [mats["c_q"], mats["d_row"]]
        in_specs += [full(mats["c_q"]), full(mats["d_row"])]
        out_specs = [pl.BlockSpec((rows, S5_WIDTH), lambda i: (i, 0)),
                     pl.BlockSpec((None, S5_SEG, S5_COLS), lambda i: (i, 0, 0))] + out_specs
        out_shape = [jax.ShapeDtypeStruct((S, S5_WIDTH), F32), jax.ShapeDtypeStruct((nt, S5_SEG, S5_COLS), F32)] + out_shape
    return _call(
        body, name="s5_fwd_y" if with_output else "s5_fwd_ends", grid=(nt,), in_specs=in_specs, out_specs=out_specs,
        out_shape=out_shape,
        scratch_shapes=[pltpu.VMEM((rows, S5_COLS), F32), pltpu.VMEM((S5_SEG, S5_COLS), F32)],
        compiler_params=_params(40),
    )(*ins)


def _s5_bwd_ends(dy_perm, mats):
    S = dy_perm.shape[0]
    rows = S5_TILE_STEPS * S5_SEG
    nt = S // rows

    def body(dy_ref, c_ref, lr_ref, li_ref, gend_ref, gs, gcar):
        @pl.when(pl.program_id(0) == 0)
        def _():
            gcar[...] = jnp.zeros(gcar.shape, F32)

        dyb = dy_ref[...].astype(BF16)
        for q in range(S5_QUADS):
            gs[:, q * S5_QCOLS:(q + 1) * S5_QCOLS] = lax.dot_general(
                dyb[:, q * 128:(q + 1) * 128], c_ref[q], (((1,), (1,)), ((), ())), preferred_element_type=F32)

        def step(k, g):
            t = S5_TILE_STEPS - 1 - k
            sl = pl.ds(pl.multiple_of(t * S5_SEG, S5_SEG), S5_SEG)
            return _s5_advance(g, lr_ref, li_ref, gs, sl, True)

        g = _scan_loop(step, _s5_parts(gcar[...]))
        _s5_store(gcar, slice(None), g)
        _s5_store(gend_ref, slice(None), g)

    full = lambda a: pl.BlockSpec(a.shape, functools.partial(lambda i, nd: (0,) * nd, nd=a.ndim))
    return _call(
        body, name="s5_bwd_ends", grid=(nt,),
        in_specs=[pl.BlockSpec((rows, S5_WIDTH), lambda i: (nt - 1 - i, 0)), full(mats["c_q"]), full(mats["lam_r"]),
                  full(mats["lam_i"])],
        out_specs=pl.BlockSpec((S5_SEG, S5_COLS), lambda i: (0, 0)),
        out_shape=jax.ShapeDtypeStruct((S5_SEG, S5_COLS), F32),
        scratch_shapes=[pltpu.VMEM((rows, S5_COLS), F32), pltpu.VMEM((S5_SEG, S5_COLS), F32)],
        compiler_params=_params(40),
    )(dy_perm, mats["c_q"], mats["lam_r"], mats["lam_i"])


def _s5_bwd_full(u_perm, dy_perm, hinit, g0, mats):
    S = u_perm.shape[0]
    rows = S5_TILE_STEPS * S5_SEG
    nt = S // rows

    def body(u_ref, dy_ref, hinit_ref, g0_ref, b_ref, c_ref, lr_ref, li_ref, d_ref,
             du_ref, dp_ref, dq_ref, db_ref, dc_ref, dd_ref, hs, gs, gcar):
        @pl.when(pl.program_id(0) == 0)
        def _():
            gcar[...] = g0_ref[...]
            for ref in (dp_ref, dq_ref, db_ref, dc_ref, dd_ref):
                ref[...] = jnp.zeros(ref.shape, F32)

        u, dy = u_ref[...], dy_ref[...]
        ub, dyb = u.astype(BF16), dy.astype(BF16)
        hs[0:S5_SEG, :] = hinit_ref[...]
        for q in range(S5_QUADS):
            cols = slice(q * S5_QCOLS, (q + 1) * S5_QCOLS)
            hs[S5_SEG:, cols] = jnp.dot(ub[:, q * 128:(q + 1) * 128], b_ref[q], preferred_element_type=F32)
            gs[:, cols] = lax.dot_general(dyb[:, q * 128:(q + 1) * 128], c_ref[q], (((1,), (1,)), ((), ())),
                                          preferred_element_type=F32)

        def fstep(t, h):
            sl = pl.ds(pl.multiple_of((t + 1) * S5_SEG, S5_SEG), S5_SEG)
            hn = _s5_advance(h, lr_ref, li_ref, hs, sl, False)
            _s5_store(hs, sl, hn)
            return hn

        _scan_loop(fstep, _s5_parts(hinit_ref[...]))

        def bstep(k, g):
            t = S5_TILE_STEPS - 1 - k
            sl = pl.ds(pl.multiple_of(t * S5_SEG, S5_SEG), S5_SEG)
            gn = _s5_advance(g, lr_ref, li_ref, gs, sl, True)
            _s5_store(gs, sl, gn)
            return gn

        _s5_store(gcar, slice(None), _scan_loop(bstep, _s5_parts(gcar[...])))

        half = S5_QCOLS // 2
        dus = []
        for q in range(S5_QUADS):
            cols = slice(q * S5_QCOLS, (q + 1) * S5_QCOLS)

            def astep(t, carry, q=q):
                sl = pl.ds(pl.multiple_of(t * S5_SEG, S5_SEG), S5_SEG)
                g = gs[sl, q * S5_QCOLS:(q + 1) * S5_QCOLS]
                hp = hs[sl, q * S5_QCOLS:(q + 1) * S5_QCOLS]
                hp_sw = jnp.concatenate([hp[:, half:], hp[:, :half]], axis=1)
                return carry[0] + g * hp, carry[1] + g * hp_sw

            zero = jnp.zeros((S5_SEG, S5_QCOLS), F32)
            acc_p, acc_q = _scan_loop(astep, (zero, zero))
            dp_ref[:, cols] += jnp.sum(acc_p, axis=0, keepdims=True)
            dq_ref[:, cols] += jnp.sum(acc_q, axis=0, keepdims=True)
            gq = gs[:, cols].astype(BF16)
            db_ref[q] += lax.dot_general(ub[:, q * 128:(q + 1) * 128], gq, (((0,), (0,)), ((), ())),
                                         preferred_element_type=F32)
            hq = hs[S5_SEG:, cols].astype(BF16)
            dc_ref[q] += lax.dot_general(dyb[:, q * 128:(q + 1) * 128], hq, (((0,), (0,)), ((), ())),
                                         preferred_element_type=F32)
            dus.append(lax.dot_general(gq, b_ref[q], (((1,), (1,)), ((), ())), preferred_element_type=F32))
        du_ref[...] = (jnp.concatenate(dus, axis=1) + d_ref[...] * dy).astype(du_ref.dtype)
        dd_ref[...] += jnp.sum(dy * u, axis=0, keepdims=True)

    full = lambda a: pl.BlockSpec(a.shape, functools.partial(lambda i, nd: (0,) * nd, nd=a.ndim))
    rev_rows = pl.BlockSpec((rows, S5_WIDTH), lambda i: (nt - 1 - i, 0))
    consts = [mats["b_q"], mats["c_q"], mats["lam_r"], mats["lam_i"], mats["d_row"]]
    acc = lambda s: pl.BlockSpec(s, functools.partial(lambda i, nd: (0,) * nd, nd=len(s)))
    acc_shapes = [(1, S5_COLS), (1, S5_COLS), (S5_QUADS, 128, S5_QCOLS), (S5_QUADS, 128, S5_QCOLS), (1, S5_WIDTH)]
    return _call(
        body, name="s5_bwd_full", grid=(nt,),
        in_specs=[rev_rows, rev_rows, pl.BlockSpec((None, S5_SEG, S5_COLS), lambda i: (nt - 1 - i, 0, 0)), full(g0)]
        + [full(a) for a in consts],
        out_specs=[rev_rows] + [acc(s) for s in acc_shapes],
        out_shape=[jax.ShapeDtypeStruct((S, S5_WIDTH), BF16)] + [jax.ShapeDtypeStruct(s, F32) for s in acc_shapes],
        scratch_shapes=[pltpu.VMEM((rows + S5_SEG, S5_COLS), F32), pltpu.VMEM((rows, S5_COLS), F32),
                        pltpu.VMEM((S5_SEG, S5_COLS), F32)],
        compiler_params=_params(56),
    )(u_perm, dy_perm, hinit, g0, *consts)


def _cmul(ar, ai, br, bi):
    return ar * br - ai * bi, ar * bi + ai * br


def _split_cols(v):
    t = v.reshape(v.shape[0], S5_QUADS, 2, S5_QCOLS // 2)
    return t[:, :, 0], t[:, :, 1]


def _join_cols(re, im):
    return jnp.stack([re, im], axis=2).reshape(re.shape[0], S5_COLS)


def _segment_starts(ends, pow_r, pow_i, reverse):
    er, ei = _split_cols(ends)
    pi = -pow_i if reverse else pow_i
    order = list(range(S5_SEG))
    if reverse:
        order = order[::-1]
    cr, ci = jnp.zeros_like(er[0]), jnp.zeros_like(ei[0])
    out_r, out_i = [None] * S5_SEG, [None] * S5_SEG
    for j in order:
        out_r[j], out_i[j] = cr, ci
        mr, mi = _cmul(pow_r, pi, cr, ci)
        cr, ci = mr + er[j], mi + ei[j]
    return _join_cols(jnp.stack(out_r), jnp.stack(out_i))


def _to_segments(a):
    S, w = a.shape
    return a.reshape(S5_SEG, S // S5_SEG, w).transpose(1, 0, 2).reshape(S, w)


def _from_segments(a):
    S, w = a.shape
    return a.reshape(S // S5_SEG, S5_SEG, w).transpose(1, 0, 2).reshape(S, w)


def _my_pos():
    return lax.axis_index("x"), lax.axis_index("y"), lax.axis_index("c")


def _flip(pos, k):
    x, y, c = pos
    return (1 - x if k & 4 else x, 1 - y if k & 2 else y, 1 - c if k & 1 else c)


def _index_of(pos):
    return 4 * pos[0] + 2 * pos[1] + pos[2]


_GATHER_FLIPS = (0, 1, 4, 5, 2, 3, 6, 7)


def _inproj_gather(x, norm_g, pack_a, pack_b, pack_c):
    S = x.shape[0]
    tm = min(S, 1024)
    n_i = S // tm
    order = jnp.stack([_index_of(_flip(_my_pos(), k)) for k in _GATHER_FLIPS]).astype(jnp.int32)

    def body(order_ref, x_ref, g_ref, pa_ref, pb_ref, pc_ref, ut_ref, proj_ref, proj16_ref, oa_ref, ob_ref, oc_ref,
             wv, u_scr, send_sems, recv_sems, local_sems):
        s, i = pl.program_id(0), pl.program_id(1)
        me = _my_pos()
        mine = _index_of(me)
        sibling = _flip(me, 1)
        srcs = (pb_ref, pa_ref, pc_ref)
        dsts = (wv, oa_ref, oc_ref)

        def direct(a, k):
            return pltpu.make_async_remote_copy(
                src_ref=srcs[a], dst_ref=dsts[a].at[mine], send_sem=send_sems.at[a * 8 + k],
                recv_sem=recv_sems.at[a * 8 + k], device_id=_flip(me, k), device_id_type=MESH)

        def passed_on(a, k):
            slot = _index_of(_flip(me, k))
            return pltpu.make_async_remote_copy(
                src_ref=dsts[a].at[slot], dst_ref=dsts[a].at[slot], send_sem=send_sems.at[a * 8 + (k | 1)],
                recv_sem=recv_sems.at[a * 8 + (k | 1)], device_id=sibling, device_id_type=MESH)

        def arrival(a, k):
            slot = _index_of(_flip(me, k))
            pltpu.make_async_remote_copy(
                src_ref=dsts[a].at[slot], dst_ref=dsts[a].at[slot], send_sem=send_sems.at[a * 8 + k],
                recv_sem=recv_sems.at[a * 8 + k], device_id=me, device_id_type=MESH).wait_recv()

        def own_copy(a):
            return pltpu.make_async_copy(srcs[a], dsts[a].at[mine], local_sems.at[a])

        def keep(idx):
            slot = _index_of(_flip(me, _GATHER_FLIPS[idx]))
            return pltpu.make_async_copy(wv.at[slot], ob_ref.at[slot], local_sems.at[3 + idx])


        first = (s == 0) & (i == 0)

        @pl.when(first)
        def _():
            for a in range(3):
                own_copy(a).start()
            for k in (1, 4, 2):
                direct(0, k).start()
            own_copy(0).wait()
            keep(0).start()

        for idx, k in enumerate(_GATHER_FLIPS):
            if idx == 0:
                continue

            @pl.when((s == idx) & (i == 0))
            def _(idx=idx, k=k):
                arrival(0, k)
                if k in (4, 2, 6):
                    passed_on(0, k).start()
                keep(idx).start()
                if idx == 1:
                    direct(0, 6).start()
                if idx == 2:
                    for a in (1, 2):
                        for k in (1, 4, 2, 6):
                            direct(a, k).start()

        @pl.when(s == 0)
        def _():
            y, _, _ = _rms_fwd(x_ref[...], g_ref[...])
            u_scr[pl.ds(pl.multiple_of(i * tm, tm), tm), :] = y.astype(BF16)
            ut_ref[...] = y.T.astype(BF16)

        ub = u_scr[pl.ds(pl.multiple_of(i * tm, tm), tm), :]
        block = jnp.dot(ub, wv[order_ref[s]], preferred_element_type=F32)
        proj_ref[...] = block
        proj16_ref[...] = block.astype(BF16)

        @pl.when((s == N_DEV - 1) & (i == n_i - 1))
        def _():
            for a in (1, 2):
                for k in (4, 2, 6):
                    arrival(a, k)
                    passed_on(a, k).start()
            for a in (1, 2):
                for k in (1, 5, 3, 7):
                    arrival(a, k)
                own_copy(a).wait()
            for a in range(3):
                for k in (1, 4, 2, 6):
                    direct(a, k).wait_send()
                for k in (4, 2, 6):
                    passed_on(a, k).wait_send()
            for idx in range(N_DEV):
                keep(idx).wait()

    any_spec = pl.BlockSpec(memory_space=pl.ANY)
    vmem = pl.BlockSpec(memory_space=pltpu.VMEM)
    grid_spec = pltpu.PrefetchScalarGridSpec(
        num_scalar_prefetch=1, grid=(N_DEV, n_i),
        in_specs=[pl.BlockSpec((tm, D_MODEL), lambda s, i, o: (jnp.where(s == 0, i, 0), 0)),
                  pl.BlockSpec((1, D_MODEL), lambda s, i, o: (0, 0)), any_spec, vmem, any_spec],
        out_specs=[pl.BlockSpec((D_MODEL, tm), lambda s, i, o: (0, jnp.where(s == 0, i, n_i - 1))),
                   pl.BlockSpec((tm, SHARD_IN), lambda s, i, o: (i, o[s])),
                   pl.BlockSpec((tm, SHARD_IN), lambda s, i, o: (i, o[s])), any_spec, any_spec, any_spec],
        scratch_shapes=[pltpu.VMEM((N_DEV,) + pack_b.shape, BF16), pltpu.VMEM((S, D_MODEL), BF16),
                        pltpu.SemaphoreType.DMA((24,)), pltpu.SemaphoreType.DMA((24,)), pltpu.SemaphoreType.DMA((3 + N_DEV,))],
    )
    return _call(
        body, name="inproj_gather", grid_spec=grid_spec,
        out_shape=[jax.ShapeDtypeStruct((D_MODEL, S), BF16), jax.ShapeDtypeStruct((S, IN_COLS), F32),
                   jax.ShapeDtypeStruct((S, IN_COLS), BF16), jax.ShapeDtypeStruct((N_DEV,) + pack_a.shape, BF16), jax.ShapeDtypeStruct((N_DEV,) + pack_b.shape, BF16),
                   jax.ShapeDtypeStruct((N_DEV,) + pack_c.shape, BF16)],
        compiler_params=_params(60, 2),
    )(order, x, norm_g, pack_a, pack_b, pack_c)


_SCATTER_FLIPS = (7, 6, 5, 4, 3, 2, 1, 0)
_N_CHIPS = 4


def _for_row_chunks(n_rows, chunk, fn):
    def step(c, carry):
        fn(pl.ds(pl.multiple_of(c * chunk, chunk), chunk))
        return carry

    lax.fori_loop(0, n_rows // chunk, step, 0)


def _grad_w_in_scatter(dproj, u_t, rs_a, rs_c, small_partial):
    S = u_t.shape[1]
    tm = min(S, 1024)
    n_i = S // tm
    order = jnp.stack([_index_of(_flip(_my_pos(), k)) for k in _SCATTER_FLIPS]).astype(jnp.int32)
    shapes = ((D_MODEL, SHARD_IN), rs_a.shape[1:], rs_c.shape[1:])
    row_chunk = 128

    def body(order_ref, dp_ref, ut_ref, ra_ref, rc_ref, p_ref, gb_ref, ga_ref, gc_ref, gs_ref, acc, sib_b, d2d_b,
             send_b, ici_b, mine_a, sib_a, ici_a, mine_c, sib_c, ici_c, gath, send_sems, recv_sems, local_sems):
        s, i = pl.program_id(0), pl.program_id(1)
        me = _my_pos()
        sibling = _flip(me, 1)

        my_chip = 2 * me[0] + me[1]

        small_d2d = pltpu.make_async_remote_copy(
            src_ref=p_ref, dst_ref=gath.at[_N_CHIPS], send_sem=send_sems.at[21], recv_sem=recv_sems.at[21],
            device_id=sibling, device_id_type=MESH)

        def small_ici(m):
            return pltpu.make_async_remote_copy(
                src_ref=gath.at[my_chip], dst_ref=gath.at[my_chip], send_sem=send_sems.at[22 + m],
                recv_sem=recv_sems.at[22 + m], device_id=_flip(me, 6 - 2 * m), device_id_type=MESH)
        sib = (sib_b, sib_a, sib_c)
        ici = (ici_b, ici_a, ici_c)
        outs = (gb_ref, ga_ref, gc_ref)

        def to_sibling(arr, m, src):
            return pltpu.make_async_remote_copy(
                src_ref=src, dst_ref=sib[arr].at[m], send_sem=send_sems.at[arr * 7 + m],
                recv_sem=recv_sems.at[arr * 7 + m], device_id=sibling, device_id_type=MESH)

        def over_ici(arr, m, src):
            return pltpu.make_async_remote_copy(
                src_ref=src, dst_ref=ici[arr].at[m], send_sem=send_sems.at[arr * 7 + 4 + m],
                recv_sem=recv_sems.at[arr * 7 + 4 + m], device_id=_flip(me, 6 - 2 * m), device_id_type=MESH)

        def from_sibling(arr, m):
            to_sibling(arr, m, sib[arr].at[m]).wait_recv()

        def from_ici(arr, m):
            over_ici(arr, m, ici[arr].at[m]).wait_recv()

        small = ((1, ra_ref, mine_a), (2, rc_ref, mine_c))

        def local_copy(arr, src, mine, m):
            return pltpu.make_async_copy(src.at[_index_of(_flip(me, 6 - 2 * m))], mine.at[m],
                                         local_sems.at[(arr - 1) * _N_CHIPS + m])

        @pl.when((s == 0) & (i == 0))
        def _():
            small_d2d.start()
            for arr, src, mine in small:
                for m in range(_N_CHIPS):
                    to_sibling(arr, m, src.at[_index_of(_flip(me, 7 - 2 * m))]).start()
                    local_copy(arr, src, mine, m).start()

        @pl.when((s == 1) & (i == 0))
        def _():
            small_d2d.wait_recv()
            gath[my_chip] = p_ref[...] + gath[_N_CHIPS]
            for m in range(_N_CHIPS - 1):
                small_ici(m).start()
            for arr, src, mine in small:
                rows, chunk = shapes[arr][0], 16
                for m in range(_N_CHIPS):
                    local_copy(arr, src, mine, m).wait()
                    from_sibling(arr, m)
                    if m < _N_CHIPS - 1:
                        def add(sl, arr=arr, mine=mine, m=m):
                            mine[m, sl, :] = (mine[m, sl, :].astype(F32) + sib[arr][m, sl, :].astype(F32)).astype(BF16)

                        _for_row_chunks(rows, chunk, add)
                        over_ici(arr, m, mine.at[m]).start()
                    else:
                        def keep(sl, arr=arr, mine=mine, m=m):
                            outs[arr][sl, :] = mine[m, sl, :].astype(F32) + sib[arr][m, sl, :].astype(F32)

                        _for_row_chunks(rows, chunk, keep)

        @pl.when(i == 0)
        def _():
            acc[...] = jnp.zeros(acc.shape, F32)

        acc[...] += jnp.dot(ut_ref[...], dp_ref[...], preferred_element_type=F32)

        def block_rows(c):
            return acc[c * row_chunk:(c + 1) * row_chunk, :]

        for m in range(_N_CHIPS):
            @pl.when((s == 2 * m) & (i == n_i - 1))
            def _(m=m):
                if m > 0:
                    to_sibling(0, m - 1, d2d_b).wait_send()
                for c in range(D_MODEL // row_chunk):
                    d2d_b[c * row_chunk:(c + 1) * row_chunk, :] = block_rows(c).astype(BF16)
                to_sibling(0, m, d2d_b).start()

            @pl.when((s == 2 * m + 1) & (i == n_i - 1))
            def _(m=m):
                from_sibling(0, m)
                slot = m % 2
                if m == 2:
                    over_ici(0, 0, send_b.at[0]).wait_send()
                for c in range(D_MODEL // row_chunk):
                    rows = slice(c * row_chunk, (c + 1) * row_chunk)
                    total = block_rows(c) + sib_b[m, rows, :].astype(F32)
                    if m < _N_CHIPS - 1:
                        send_b[slot, rows, :] = total.astype(BF16)
                    else:
                        gb_ref[rows, :] = total
                if m < _N_CHIPS - 1:
                    over_ici(0, m, send_b.at[slot]).start()

        @pl.when((s == N_DEV - 1) & (i == n_i - 1))
        def _():
            for arr in range(3):
                for m in range(_N_CHIPS - 1):
                    from_ici(arr, m)
                rows = shapes[arr][0]

                def add(sl, arr=arr):
                    outs[arr][sl, :] = (outs[arr][sl, :] + ici[arr][0, sl, :].astype(F32)
                                        + ici[arr][1, sl, :].astype(F32) + ici[arr][2, sl, :].astype(F32))

                _for_row_chunks(rows, 16, add)
            for m in range(_N_CHIPS - 1):
                small_ici(m).wait_recv()
            gs_ref[...] = (gath[0] + gath[1]) + (gath[2] + gath[3])
            small_d2d.wait_send()
            for m in range(_N_CHIPS - 1):
                small_ici(m).wait_send()
            to_sibling(0, _N_CHIPS - 1, d2d_b).wait_send()
            over_ici(0, 1, send_b.at[1]).wait_send()
            over_ici(0, 2, send_b.at[0]).wait_send()
            for arr, src, mine in small:
                for m in range(_N_CHIPS):
                    to_sibling(arr, m, src.at[0]).wait_send()
                for m in range(_N_CHIPS - 1):
                    over_ici(arr, m, mine.at[m]).wait_send()

    any_spec = pl.BlockSpec(memory_space=pl.ANY)
    vmem = pl.BlockSpec(memory_space=pltpu.VMEM)
    half = lambda shp, n: pltpu.VMEM((n,) + tuple(shp), BF16)
    grid_spec = pltpu.PrefetchScalarGridSpec(
        num_scalar_prefetch=1, grid=(N_DEV, n_i),
        in_specs=[pl.BlockSpec((tm, SHARD_IN), lambda s, i, o: (i, o[s])),
                  pl.BlockSpec((D_MODEL, tm), lambda s, i, o: (0, i)), any_spec, any_spec, vmem],
        out_specs=[vmem, vmem, vmem, vmem],
        scratch_shapes=[
            pltpu.VMEM((D_MODEL, SHARD_IN), F32), half(shapes[0], _N_CHIPS), pltpu.VMEM(shapes[0], BF16),
            half(shapes[0], 2), half(shapes[0], _N_CHIPS - 1),
            half(shapes[1], _N_CHIPS), half(shapes[1], _N_CHIPS), half(shapes[1], _N_CHIPS - 1),
            half(shapes[2], _N_CHIPS), half(shapes[2], _N_CHIPS), half(shapes[2], _N_CHIPS - 1),
            pltpu.VMEM((_N_CHIPS + 1,) + small_partial.shape, F32),
            pltpu.SemaphoreType.DMA((25,)), pltpu.SemaphoreType.DMA((25,)), pltpu.SemaphoreType.DMA((2 * _N_CHIPS,))],
    )
    return _call(
        body, name="grad_w_in_scatter", grid_spec=grid_spec,
        out_shape=[jax.ShapeDtypeStruct(shp, F32) for shp in shapes] + [jax.ShapeDtypeStruct(small_partial.shape, F32)],
        compiler_params=_params(60, 2),
    )(order, dproj, u_t, rs_a, rs_c, small_partial)


def _adam_update(g, w, m, v):
    m2 = ADAM_B1 * m + (1.0 - ADAM_B1) * g
    v2 = ADAM_B2 * v + (1.0 - ADAM_B2) * (g * g)
    m_hat = m2 / (1.0 - ADAM_B1 ** ADAM_STEP)
    v_hat = v2 / (1.0 - ADAM_B2 ** ADAM_STEP)
    delta = -ADAM_LR * (m_hat / (jnp.sqrt(v_hat) + ADAM_EPS) + ADAM_WD * w)
    return delta, m2, v2


def _adam_rows(g, w, m, v):
    rows, cols = w.shape
    tm = rows if rows % 256 else 256

    def fn(rv, cr, out):
        return list(_adam_update(*rv)), []

    outs, _ = _rowwise("adamw", fn, rows, tm, [(a, cols, 0) for a in (g, w, m, v)], [], [(cols, F32)] * 3, [], 32)
    return outs


_SMALL = ["norm_g", "hg_lb", "hg_norm_g", "s5_a_re", "s5_a_im", "s5_log_dt", "s5_b_re", "s5_b_im", "s5_c_re",
          "s5_c_im", "s5_d", "b_glu", "ple_norm_g", "final_norm_g"]
_BIG = ["w_in", "w_o_hg", "w_glu", "w_o_s5", "w_out", "w_ple", "w_ple_gate"]
_ORDER = ["norm_g", "w_in", "hg_lb", "hg_norm_g", "w_o_hg", "s5_a_re", "s5_a_im", "s5_log_dt", "s5_b_re", "s5_b_im",
          "s5_c_re", "s5_c_im", "s5_d", "w_glu", "b_glu", "w_o_s5", "w_out", "ple_norm_g", "w_ple", "w_ple_gate",
          "final_norm_g"]


def _pack_small(vals, tail=None):
    parts = []
    for name in _SMALL:
        flat = vals[name].reshape(-1).astype(F32)
        pad = (-flat.shape[0]) % 1024
        parts.append(jnp.pad(flat, (0, pad)))
    tail = jnp.zeros((0,), F32) if tail is None else tail.reshape(-1).astype(F32)
    parts.append(jnp.pad(tail, (0, 1024 - tail.shape[0])))
    return jnp.concatenate(parts).reshape(-1, 128)


def _unpack_small(packed, like):
    flat = packed.reshape(-1)
    out, off = {}, 0
    for name in _SMALL:
        size = like[name].size
        out[name] = flat[off:off + size].reshape(like[name].shape)
        off += size + (-size) % 1024
    return out


def _col_blocks(full):
    k = full.shape[0]
    return full.reshape(k, N_DEV, 128).transpose(1, 0, 2)


def _from_col_blocks(blocks):
    k = blocks.shape[1]
    return blocks.transpose(1, 0, 2).reshape(k, N_DEV * 128)


def kernel(x, p, norm_g, w_in, hg_lb, hg_norm_g, w_o_hg, s5_a_re, s5_a_im, s5_log_dt, s5_b_re, s5_b_im, s5_c_re, s5_c_im, s5_d, w_glu, b_glu, w_o_s5, w_out, ple_norm_g, w_ple, w_ple_gate, final_norm_g, loss_target, m_norm_g, m_w_in, m_hg_lb, m_hg_norm_g, m_w_o_hg, m_s5_a_re, m_s5_a_im, m_s5_log_dt, m_s5_b_re, m_s5_b_im, m_s5_c_re, m_s5_c_im, m_s5_d, m_w_glu, m_b_glu, m_w_o_s5, m_w_out, m_ple_norm_g, m_w_ple, m_w_ple_gate, m_final_norm_g, v_norm_g, v_w_in, v_hg_lb, v_hg_norm_g, v_w_o_hg, v_s5_a_re, v_s5_a_im, v_s5_log_dt, v_s5_b_re, v_s5_b_im, v_s5_c_re, v_s5_c_im, v_s5_d, v_w_glu, v_b_glu, v_w_o_s5, v_w_out, v_ple_norm_g, v_w_ple, v_w_ple_gate, v_final_norm_g):
    args = dict(locals())
    w = {n: args[n] for n in _ORDER}
    m = {n: args["m_" + n] for n in _ORDER}
    v = {n: args["v_" + n] for n in _ORDER}
    xs = x[0]
    ps = p[0, 0]
    tgt = loss_target[0]
    S = xs.shape[0]

    pack_a = jnp.concatenate([w_o_hg[0], w_out[0], w_ple_gate[0]], axis=0).astype(BF16)
    pack_b = w_in[0].astype(BF16)
    pack_c = jnp.concatenate([w_glu[0], w_o_s5[0], w_ple[0]], axis=0).astype(BF16)
    u_t, proj, proj16, all_a, all_b, all_c = _inproj_gather(xs, norm_g, pack_a, pack_b, pack_c)
    wf_o_hg = all_a[:, 0:128].reshape(D_MODEL, D_MODEL)
    wf_out = all_a[:, 128:256].reshape(D_MODEL, D_MODEL)
    wf_pg = all_a[:, 256:384].reshape(D_MODEL, D_MODEL)
    wf_glu = _from_col_blocks(all_c[:, 0:512])
    wf_o_s5 = _from_col_blocks(all_c[:, 512:1024])
    wf_ple = _from_col_blocks(all_c[:, 1024:1280])

    lb = jax.nn.sigmoid(hg_lb[0:1] - hg_lb[1:2])
    s5_names = ["s5_a_re", "s5_a_im", "s5_log_dt", "s5_b_re", "s5_b_im", "s5_c_re", "s5_c_im", "s5_d"]
    build = lambda *a: _s5_matrices(*a, seg_len=S // S5_SEG)
    mats_f32, mats_vjp = jax.vjp(build, *[w[n][0] for n in s5_names])
    mats = dict(mats_f32, b_q=mats_f32["b_q"].astype(BF16), c_q=mats_f32["c_q"].astype(BF16))
    bias_glu = b_glu

    o, states = _hgrn_fwd(proj, lb)
    u_perm = _to_segments(proj[:, COL_US:COL_US + S5_WIDTH])
    zeros_state = jnp.zeros((S5_SEG, S5_COLS), F32)
    (h_ends,) = _s5_fwd_pass(u_perm, mats, zeros_state, False)
    h0 = _segment_starts(h_ends, mats["pow_r"], mats["pow_i"], False)
    y_perm, h_init, _ = _s5_fwd_pass(u_perm, mats, h0, True)
    ys = _from_segments(y_perm)
    y_hg, y_s5, glu, h1 = _stage_branches(o, proj16, ys, xs, hg_norm_g, wf_o_hg, wf_glu, bias_glu, wf_o_s5, wf_out)

    dh1, (loss_acc, d_final_g, d_ple_g, d_w_ple, d_w_pg) = _stage_ple_loss(
        h1, ps, tgt, ple_norm_g, wf_ple, wf_pg, final_norm_g.reshape(1, D_MODEL))
    (d_gate_hg, d_gate_s5, d_yhg, d_ys5), (d_w_out,) = _stage_bwd_merge(dh1, y_hg, y_s5, proj16, wf_out)
    (d_o, d_g_hg), (d_w_o_hg, d_hg_norm) = _stage_bwd_hg_path(d_yhg, o, proj16, hg_norm_g, wf_o_hg)
    (d_ys, d_z), (d_w_o_s5, d_w_glu, d_b_glu) = _stage_bwd_s5_path(d_ys5, ys, glu, proj16, wf_o_s5, wf_glu)
    dq, df, div, d_lb = _hgrn_bwd(proj, lb, d_o, states)
    dy_perm = _to_segments(d_ys)
    g_ends = _s5_bwd_ends(dy_perm, mats)
    g0 = _segment_starts(g_ends, mats["pow_r"], mats["pow_i"], True)
    du_perm, acc_p, acc_q, d_bq, d_cq_t, d_d = _s5_bwd_full(u_perm, dy_perm, h_init, g0, mats)
    d_us = _from_segments(du_perm)
    grad_x, dproj, d_norm_g = _stage_inproj_bwd([dq, df, div, d_g_hg, d_us, d_z, d_gate_hg, d_gate_s5], xs, dh1,
                                                norm_g, all_b)

    p_re, p_im = _split_cols(acc_p)
    q_re, q_im = _split_cols(acc_q)
    d_lam_r = (p_re + p_im)[0]
    d_lam_i = (q_im - q_re)[0]
    zero_row = jnp.zeros((S5_SEG, S5_COLS), F32)
    row_of = lambda re_part, im_part: zero_row.at[0].set(_join_cols(re_part[None], im_part[None])[0])
    zeros_q = jnp.zeros_like(d_lam_r)
    cot = dict(
        lam_r=row_of(d_lam_r, zeros_q), lam_i=row_of(zeros_q, d_lam_i),
        b_q=d_bq, c_q=d_cq_t.transpose(0, 2, 1), d_row=d_d,
        pow_r=jnp.zeros_like(mats["pow_r"]), pow_i=jnp.zeros_like(mats["pow_i"]),
    )
    d_s5 = mats_vjp(cot)

    s_lb = lb * (1.0 - lb)
    d_hg_lb = jnp.concatenate([d_lb * s_lb, -d_lb * s_lb], axis=0)
    small_g = dict(norm_g=d_norm_g, hg_lb=d_hg_lb, hg_norm_g=d_hg_norm, b_glu=d_b_glu, ple_norm_g=d_ple_g,
                   final_norm_g=d_final_g)
    for name, g in zip(s5_names, d_s5):
        small_g[name] = g
    pk = lambda d: _pack_small({n: d[n] for n in _SMALL})
    rs_a = jnp.concatenate([d_w_o_hg.reshape(N_DEV, 128, D_MODEL), d_w_out.reshape(N_DEV, 128, D_MODEL),
                            d_w_pg.reshape(N_DEV, 128, D_MODEL)], axis=1).astype(BF16)
    rs_c = jnp.concatenate([_col_blocks(d_w_glu), _col_blocks(d_w_o_s5), _col_blocks(d_w_ple)], axis=1).astype(BF16)
    partial = _pack_small({n: small_g[n] for n in _SMALL}, tail=loss_acc[0, 0:1])
    g_b, g_a, g_c, sg = _grad_w_in_scatter(dproj, u_t, rs_a, rs_c, partial)
    sd, sm, sv = _adam_rows(sg, pk(w), pk(m), pk(v))
    like = {n: w[n] for n in _SMALL}
    out_g, out_d, out_m, out_v = (_unpack_small(t, like) for t in (sg, sd, sm, sv))
    big_g = dict(w_o_hg=g_a[0:128], w_out=g_a[128:256], w_ple_gate=g_a[256:384], w_in=g_b,
                 w_glu=g_c[0:512], w_o_s5=g_c[512:1024], w_ple=g_c[1024:1280])
    for name in _BIG:
        shape = w[name].shape
        g2 = big_g[name]
        d2, m2, v2 = _adam_rows(g2, w[name][0], m[name][0], v[name][0])
        out_g[name], out_d[name], out_m[name], out_v[name] = (t.reshape(shape) for t in (g2, d2, m2, v2))

    loss = sg[sg.shape[0] - 8, 0]
    return (loss, grad_x[None], *[out_g[n] for n in _ORDER], *[out_d[n] for n in _ORDER],
            *[out_m[n] for n in _ORDER], *[out_v[n] for n in _ORDER])
```
